```python
import math
import jax, jax.numpy as jnp
from jax import lax
import numpy as np

D_MODEL = 2048
BATCH = 4
SEQ = 2048
DEPTH = 1
DEC_BATCH = 128
DEC_SEQ = 1
PAST_LEN = 16384
PAGE_SIZE = 128

GLA_HEADS = 4
GLA_DK = 128
GLA_DV = 256
GLA_QK = GLA_HEADS * GLA_DK
GLA_V = GLA_HEADS * GLA_DV
GLA_GATE_RANK = 16
GLA_GATE_NORM = 16.0
RET_HEADS = 4
RET_DK = 256
RET_DV = 256
RET_QK = RET_HEADS * RET_DK
RET_V = RET_HEADS * RET_DV
ROPE_BASE = 10000.0
CHUNK = 64
EPS = 1e-6
D_FF = -(-8 * D_MODEL // (3 * 256)) * 256
SPLITS = (GLA_QK, GLA_QK, GLA_V, GLA_GATE_RANK, GLA_V, RET_QK, RET_QK, RET_V, RET_V, 2 * D_MODEL)
IN_WIDTH = sum(SPLITS)

kernel_name = "gla_retnet_gated_merge_decoder_step"


def rmsnorm(x, w):
    xf = x.astype(jnp.float32)
    y = xf * lax.rsqrt(jnp.mean(xf * xf, axis=-1, keepdims=True) + EPS)
    return (y * w.astype(jnp.float32)).astype(x.dtype)


def head_layernorm(o, w):
    mu = jnp.mean(o, axis=-1, keepdims=True)
    d = o - mu
    var = jnp.mean(d * d, axis=-1, keepdims=True)
    return d * lax.rsqrt(var + EPS) * w.astype(jnp.float32)


def rotary(x, pos):
    half = x.shape[-1] // 2
    inv = ROPE_BASE ** (-jnp.arange(half, dtype=jnp.float32) / half)
    ang = pos.astype(jnp.float32)[:, None] * inv[None, :]
    cos, sin = jnp.cos(ang), jnp.sin(ang)
    x1, x2 = x[..., :half], x[..., half:]
    return jnp.concatenate([x1 * cos - x2 * sin, x1 * sin + x2 * cos], axis=-1)


def chunked_linear_attention(q, k, v, g, s0, per_dim_decay):
    B, H, L, dk = q.shape
    dv = v.shape[-1]
    C = min(CHUNK, L)
    n = -(-L // C)
    pad = n * C - L
    if pad:
        padf = lambda a: jnp.pad(a, ((0, 0), (0, 0), (0, pad), (0, 0)))
        q, k, v, g = padf(q), padf(k), padf(v), padf(g)

    def to_chunks(a):
        return a.reshape(B, H, n, C, a.shape[-1]).transpose(2, 0, 1, 3, 4)

    causal = jnp.tril(jnp.ones((C, C), dtype=bool))[None, None, :, :, None]

    def step(S, inp):
        qi, ki, vi, gi = inp
        G = jnp.cumsum(gi, axis=2)
        Glast = G[:, :, -1:, :]
        o_inter = jnp.einsum('bhcd,bhde->bhce', qi * jnp.exp(G), S)
        rel = G[:, :, :, None, :] - G[:, :, None, :, :]
        decay = jnp.exp(jnp.where(causal, rel, -jnp.inf))
        if per_dim_decay:
            scores = jnp.einsum('bhid,bhjd,bhijd->bhij', qi, ki, decay)
        else:
            scores = jnp.einsum('bhid,bhjd->bhij', qi, ki) * decay[..., 0]
        o = o_inter + jnp.einsum('bhij,bhje->bhie', scores, vi)
        kd = ki * jnp.exp(Glast - G)
        S_new = jnp.exp(Glast)[:, :, 0, :, None] * S + jnp.einsum('bhcd,bhce->bhde', kd, vi)
        return S_new, o

    S_fin, oc = lax.scan(step, s0, (to_chunks(q), to_chunks(k), to_chunks(v), to_chunks(g)))
    o = oc.transpose(1, 2, 0, 3, 4).reshape(B, H, n * C, dv)[:, :, :L]
    return o, S_fin


def mixer(h, pos, s_gla, s_ret, w_in, w_gla_gate_up, b_gla_gate, gla_norm_w, w_gla_up,
          ret_norm_w, w_ret_up, w_out):
    B, L, _ = h.shape
    f32 = jnp.float32
    proj = h @ w_in
    offs = np.cumsum(SPLITS)[:-1].tolist()
    qa, ka, va, gdown, ga, qb, kb, vb, gb, mg = jnp.split(proj, offs, axis=-1)

    def heads(a, nh):
        return a.reshape(B, L, nh, -1).transpose(0, 2, 1, 3).astype(f32)

    q = heads(qa, GLA_HEADS) * (GLA_DK ** -0.5)
    k = heads(ka, GLA_HEADS)
    v = heads(va, GLA_HEADS)
    glog = jax.nn.log_sigmoid((gdown @ w_gla_gate_up + b_gla_gate).astype(f32)) / GLA_GATE_NORM
    glog = heads(glog, GLA_HEADS)
    o_a, sa = chunked_linear_attention(q, k, v, glog, s_gla.astype(f32), True)
    o_a = rmsnorm(o_a.transpose(0, 2, 1, 3), gla_norm_w).reshape(B, L, GLA_V)
    o_a = o_a * jax.nn.silu(ga.astype(f32))
    y_a = o_a.astype(h.dtype) @ w_gla_up

    qr = rotary(heads(qb, RET_HEADS), pos)
    kr = rotary(heads(kb, RET_HEADS), pos) * (RET_DK ** -0.5)
    vr = heads(vb, RET_HEADS)
    log_gamma = jnp.log1p(-jnp.exp(jnp.linspace(math.log(1.0 / 32), math.log(1.0 / 512), RET_HEADS))).astype(f32)
    gr = jnp.broadcast_to(log_gamma[None, :, None, None], (B, RET_HEADS, L, 1))
    o_b, sb = chunked_linear_attention(qr, kr, vr, gr, s_ret.astype(f32), False)
    o_b = head_layernorm(o_b.transpose(0, 2, 1, 3), ret_norm_w).reshape(B, L, RET_V)
    o_b = o_b * jax.nn.silu(gb.astype(f32))
    y_b = o_b.astype(h.dtype) @ w_ret_up

    gates = jax.nn.sigmoid(mg.astype(f32)).reshape(B, L, 2, D_MODEL)
    m = gates[:, :, 0] * y_a.astype(f32) + gates[:, :, 1] * y_b.astype(f32)
    return m.astype(h.dtype) @ w_out, sa, sb


def swiglu(h, w_gate, w_up, w_down):
    a = (h @ w_gate).astype(jnp.float32)
    b = (h @ w_up).astype(jnp.float32)
    return (jax.nn.silu(a) * b).astype(h.dtype) @ w_down


def trunk(x, pos, st_gla, st_ret, norm_mix, w_in, w_gla_gate_up, b_gla_gate, gla_norm_w, w_gla_up,
          ret_norm_w, w_ret_up, w_out, norm_ffn, w_ffn_gate, w_ffn_up, w_ffn_down, norm_final):
    new_gla, new_ret = [], []
    for l in range(DEPTH):
        h = rmsnorm(x, norm_mix[l])
        m, sa, sb = mixer(h, pos, st_gla[l], st_ret[l], w_in[l], w_gla_gate_up[l], b_gla_gate[l],
                          gla_norm_w[l], w_gla_up[l], ret_norm_w[l], w_ret_up[l], w_out[l])
        x = x + m
        x = x + swiglu(rmsnorm(x, norm_ffn[l]), w_ffn_gate[l], w_ffn_up[l], w_ffn_down[l])
        new_gla.append(sa)
        new_ret.append(sb)
    return rmsnorm(x, norm_final), jnp.stack(new_gla), jnp.stack(new_ret)


def setup_inputs(seed: int = 0) -> dict:
    key = jax.random.key(seed)
    ks = jax.random.split(key, 20)
    nrm = lambda k, shape, s: jax.random.normal(k, shape, jnp.float32) * s
    gain = lambda k, shape: 1.0 + 0.01 * jax.random.normal(k, shape, jnp.float32)
    return {
        "x_prompt": nrm(ks[0], (BATCH, SEQ, D_MODEL), 1.0),
        "x_sample": nrm(ks[1], (DEC_BATCH, DEC_SEQ, D_MODEL), 1.0),
        "state_gla": nrm(ks[2], (DEPTH, DEC_BATCH, GLA_HEADS, GLA_DK, GLA_DV), 0.5),
        "state_ret": nrm(ks[3], (DEPTH, DEC_BATCH, RET_HEADS, RET_DK, RET_DV), 0.5),
        "norm_mix": gain(ks[4], (DEPTH, D_MODEL)),
        "w_in": nrm(ks[5], (DEPTH, D_MODEL, IN_WIDTH), D_MODEL ** -0.5),
        "w_gla_gate_up": nrm(ks[6], (DEPTH, GLA_GATE_RANK, GLA_QK), GLA_GATE_RANK ** -0.5),
        "b_gla_gate": nrm(ks[7], (DEPTH, GLA_QK), 0.1),
        "gla_norm_w": gain(ks[8], (DEPTH, GLA_DV)),
        "w_gla_up": nrm(ks[9], (DEPTH, GLA_V, D_MODEL), GLA_V ** -0.5),
        "ret_norm_w": gain(ks[10], (DEPTH, RET_DV)),
        "w_ret_up": nrm(ks[11], (DEPTH, RET_V, D_MODEL), RET_V ** -0.5),
        "w_out": nrm(ks[12], (DEPTH, D_MODEL, D_MODEL), D_MODEL ** -0.5),
        "norm_ffn": gain(ks[13], (DEPTH, D_MODEL)),
        "w_ffn_gate": nrm(ks[14], (DEPTH, D_MODEL, D_FF), D_MODEL ** -0.5),
        "w_ffn_up": nrm(ks[15], (DEPTH, D_MODEL, D_FF), D_MODEL ** -0.5),
        "w_ffn_down": nrm(ks[16], (DEPTH, D_FF, D_MODEL), D_FF ** -0.5),
        "norm_final": gain(ks[17], (D_MODEL,)),
    }


def reference(x_prompt, x_sample, state_gla, state_ret, norm_mix, w_in, w_gla_gate_up, b_gla_gate,
              gla_norm_w, w_gla_up, ret_norm_w, w_ret_up, w_out, norm_ffn, w_ffn_gate, w_ffn_up,
              w_ffn_down, norm_final):
    weights = (norm_mix, w_in, w_gla_gate_up, b_gla_gate, gla_norm_w, w_gla_up, ret_norm_w, w_ret_up,
               w_out, norm_ffn, w_ffn_gate, w_ffn_up, w_ffn_down, norm_final)
    B, L, _ = x_prompt.shape
    pos_p = jnp.arange(L, dtype=jnp.int32)
    z_gla = jnp.zeros((DEPTH, B, GLA_HEADS, GLA_DK, GLA_DV), jnp.float32)
    z_ret = jnp.zeros((DEPTH, B, RET_HEADS, RET_DK, RET_DV), jnp.float32)
    y_prompt, gla_p, ret_p = trunk(x_prompt, pos_p, z_gla, z_ret, *weights)
    pos_s = PAST_LEN + jnp.arange(x_sample.shape[1], dtype=jnp.int32)
    y_sample, gla_s, ret_s = trunk(x_sample, pos_s, state_gla, state_ret, *weights)
    sd = state_gla.dtype
    return (y_prompt, y_sample, gla_p.astype(sd), ret_p.astype(state_ret.dtype), gla_s.astype(sd), ret_s.astype(state_ret.dtype))
```

```python
import functools
import math

import numpy as np
import jax
import jax.numpy as jnp
from jax import lax
from jax.experimental import pallas as pl
from jax.experimental.pallas import tpu as pltpu

EPS = 1e-6
ROPE_BASE = 10000.0
GLA_GATE_NORM = 16.0
PAST_LEN = 16384

V7X_LANES = 128
V7X_VMEM_BYTES = 64 * 1024 * 1024
V7X_VMEM_REQUEST_CAP = 58 * 1024 * 1024
COMPILER_SCRATCH_BYTES = 12 * 1024 * 1024

GLA_CHUNK = 64
GLA_SUB = 16
RET_CHUNK = 128
DEC_TOKENS = 16

BF16 = jnp.bfloat16
F32 = jnp.float32


def _params(n_axes, *block_bytes, scratch_bytes=0):
    need = 2 * sum(block_bytes) + scratch_bytes + COMPILER_SCRATCH_BYTES
    return pltpu.CompilerParams(
        dimension_semantics=("arbitrary",) * n_axes,
        vmem_limit_bytes=int(min(V7X_VMEM_REQUEST_CAP, need)),
    )


def _nbytes(shape, dtype):
    return int(np.prod(shape)) * jnp.dtype(dtype).itemsize


def _sigmoid(x):
    return 1.0 / (1.0 + jnp.exp(-x))


def _silu(x):
    return x * _sigmoid(x)


def _log_sigmoid(x):
    return jnp.minimum(x, 0.0) - jnp.log1p(jnp.exp(-jnp.abs(x)))


def _dot(a, b):
    return jnp.dot(a, b, preferred_element_type=F32)


def _dot_nt(a, b):
    return lax.dot_general(a, b, (((1,), (1,)), ((), ())), preferred_element_type=F32)


def _dot_tn(a, b):
    return lax.dot_general(a, b, (((0,), (0,)), ((), ())), preferred_element_type=F32)


def _row_tile(m, want):
    t = min(m, want)
    assert m % t == 0, (m, t)
    return t


def _rmsnorm_kernel(x_ref, w_ref, o_ref):
    x = x_ref[...]
    ms = jnp.mean(x * x, axis=-1, keepdims=True)
    o_ref[...] = (x * lax.rsqrt(ms + EPS) * w_ref[...]).astype(o_ref.dtype)


def _rmsnorm_bf16(x, w):
    m, d = x.shape
    tm = _row_tile(m, 512)
    return pl.pallas_call(
        _rmsnorm_kernel,
        grid=(m // tm,),
        in_specs=[pl.BlockSpec((tm, d), lambda i: (i, 0)), pl.BlockSpec((1, d), lambda i: (0, 0))],
        out_specs=pl.BlockSpec((tm, d), lambda i: (i, 0)),
        out_shape=jax.ShapeDtypeStruct((m, d), BF16),
        compiler_params=_params(1, _nbytes((tm, d), F32), _nbytes((tm, d), BF16)),
        name="rmsnorm",
    )(x, w.reshape(1, d))


def _in_proj_kernel(h_ref, wm_ref, wn_ref, o_ref, wbf_ref, *, n_plain, shift, row_chunk):
    j = pl.program_id(0)
    i = pl.program_id(1)
    kdim, tn = wbf_ref.shape

    @pl.when(jnp.logical_and(i == 0, j < n_plain))
    def _():
        wbf_ref[...] = wm_ref[...].astype(BF16)

    @pl.when(jnp.logical_and(i == 0, j >= n_plain))
    def _():
        def body(c, carry):
            r0 = pl.multiple_of(c * row_chunk, row_chunk)
            wm = wm_ref[pl.ds(r0, row_chunk), :]
            wn = wn_ref[pl.ds(r0, row_chunk), :]
            w = jnp.concatenate([wm[:, shift:], wn[:, :shift]], axis=1)
            wbf_ref[pl.ds(r0, row_chunk), :] = w.astype(BF16)
            return carry

        lax.fori_loop(0, kdim // row_chunk, body, 0)

    o_ref[...] = _dot(h_ref[...], wbf_ref[...]).astype(o_ref.dtype)


def _in_proj(h, w_in, plain_cols, shift, out_cols, tn):
    m, d = h.shape
    tm = _row_tile(m, 1024)
    assert plain_cols % tn == 0 and out_cols % tn == 0 and tn % V7X_LANES == 0
    n_plain = plain_cols // tn
    row_chunk = min(d, 256)
    kern = functools.partial(_in_proj_kernel, n_plain=n_plain, shift=shift, row_chunk=row_chunk)
    lanes_per_tile = tn // V7X_LANES
    return pl.pallas_call(
        kern,
        grid=(out_cols // tn, m // tm),
        in_specs=[
            pl.BlockSpec((tm, d), lambda j, i: (i, 0)),
            pl.BlockSpec((d, tn), lambda j, i: (0, j)),
            pl.BlockSpec((d, V7X_LANES), lambda j, i: (0, (j + 1) * lanes_per_tile)),
        ],
        out_specs=pl.BlockSpec((tm, tn), lambda j, i: (i, j)),
        out_shape=jax.ShapeDtypeStruct((m, out_cols), BF16),
        scratch_shapes=[pltpu.VMEM((d, tn), BF16)],
        compiler_params=_params(
            2, _nbytes((tm, d), BF16), _nbytes((d, tn), F32), _nbytes((d, V7X_LANES), F32),
            _nbytes((tm, tn), BF16), scratch_bytes=_nbytes((d, tn), BF16) + _nbytes((tm, tn), F32)),
        name="in_proj",
    )(h, w_in, w_in)


def _gate_logits(h_ref, wgd_ref, wup_ref, bup_ref):
    gd = _dot(h_ref[...], wgd_ref[...].astype(BF16))
    return _dot(gd.astype(BF16), wup_ref[...].astype(BF16)) + bup_ref[...]


def _prefix_sum_rows(tri_bf16, g):
    g0 = g.astype(BF16)
    r1 = g - g0.astype(F32)
    g1 = r1.astype(BF16)
    g2 = (r1 - g1.astype(F32)).astype(BF16)
    return _dot(tri_bf16, g0) + _dot(tri_bf16, g1) + _dot(tri_bf16, g2)


def _lane_bcast_cols(row, n):
    parts = []
    for c in range(n // V7X_LANES):
        tile = jnp.broadcast_to(row[:, c * V7X_LANES:(c + 1) * V7X_LANES], (V7X_LANES, V7X_LANES))
        parts.append(tile.T)
    return parts[0] if len(parts) == 1 else jnp.concatenate(parts, axis=0)


def _rms_gate_store(o, w, gate, out_ref, rows, cols):
    ms = jnp.mean(o * o, axis=-1, keepdims=True)
    y = o * lax.rsqrt(ms + EPS) * w
    out_ref[rows, cols] = (y * _silu(gate)).astype(out_ref.dtype)


def _ln_gate_store(o, w, gate, out_ref, rows, cols):
    mu = jnp.mean(o, axis=-1, keepdims=True)
    dlt = o - mu
    var = jnp.mean(dlt * dlt, axis=-1, keepdims=True)
    y = dlt * lax.rsqrt(var + EPS) * w
    out_ref[rows, cols] = (y * _silu(gate)).astype(out_ref.dtype)


def _gla_chunk_head(q, k, v, g_cum, s_ref, hh):
    c, dk = q.shape
    nsub = c // GLA_SUB
    row = lax.broadcasted_iota(jnp.int32, (c, dk), 0)
    g_last = g_cum[c - 1:c, :]
    state = s_ref[hh]

    o = _dot((q * jnp.exp(g_cum)).astype(BF16), state.astype(BF16))

    g_ref_rows = jnp.concatenate(
        [jnp.broadcast_to(g_cum[GLA_SUB * b:GLA_SUB * b + 1, :], (GLA_SUB, dk)) for b in range(nsub)], axis=0)
    q_rel = (q * jnp.exp(g_cum - g_ref_rows)).astype(BF16)
    blocks = [jnp.zeros((GLA_SUB, c), F32)]
    for b in range(1, nsub):
        g_ref = g_cum[GLA_SUB * b:GLA_SUB * b + 1, :]
        k_rel = k * jnp.exp(jnp.where(row < GLA_SUB * b, g_ref - g_cum, -jnp.inf))
        blocks.append(_dot_nt(q_rel[GLA_SUB * b:GLA_SUB * (b + 1), :], k_rel.astype(BF16)))
    scores = jnp.concatenate(blocks, axis=0)

    srow = lax.broadcasted_iota(jnp.int32, (c, c), 0)
    scol = lax.broadcasted_iota(jnp.int32, (c, c), 1)
    rel = scol - jnp.bitwise_and(srow, -GLA_SUB)
    row_in_sub = jnp.bitwise_and(row, GLA_SUB - 1)
    for r in range(GLA_SUB):
        k_r = jnp.concatenate(
            [jnp.broadcast_to(k[GLA_SUB * b + r:GLA_SUB * b + r + 1, :], (GLA_SUB, dk)) for b in range(nsub)], axis=0)
        g_r = jnp.concatenate(
            [jnp.broadcast_to(g_cum[GLA_SUB * b + r:GLA_SUB * b + r + 1, :], (GLA_SUB, dk)) for b in range(nsub)],
            axis=0)
        e = jnp.exp(jnp.where(row_in_sub >= r, g_cum - g_r, -jnp.inf))
        col = jnp.sum(q * e * k_r, axis=-1, keepdims=True)
        scores = scores + jnp.where(rel == r, col, 0.0)

    o = o + _dot(scores.astype(BF16), v)

    k_tail = (k * jnp.exp(g_last - g_cum)).astype(BF16)
    decay = _lane_bcast_cols(jnp.exp(g_last), dk)
    dv = v.shape[1]
    decay_full = jnp.concatenate([decay] * (dv // V7X_LANES), axis=1)
    s_ref[hh] = decay_full * state + _dot_tn(k_tail, v)
    return o


def _gla_prompt_kernel(q_ref, k_ref, v_ref, ga_ref, h_ref, wgd_ref, wup_ref, bup_ref, nw_ref,
                       o_ref, s_out_ref, s_ref, *, heads, dk, dv):
    t = pl.program_id(1)

    @pl.when(t == 0)
    def _():
        s_ref[...] = jnp.zeros_like(s_ref)

    ct = q_ref.shape[0]
    glog = _log_sigmoid(_gate_logits(h_ref, wgd_ref, wup_ref, bup_ref)) * (1.0 / GLA_GATE_NORM)
    ti = lax.broadcasted_iota(jnp.int32, (GLA_CHUNK, GLA_CHUNK), 0)
    tj = lax.broadcasted_iota(jnp.int32, (GLA_CHUNK, GLA_CHUNK), 1)
    tri = (ti >= tj).astype(BF16)
    nw = nw_ref[...]
    for c in range(ct // GLA_CHUNK):
        rows = slice(c * GLA_CHUNK, (c + 1) * GLA_CHUNK)
        g_cum_all = _prefix_sum_rows(tri, glog[rows, :])
        for hh in range(heads):
            kc = slice(hh * dk, (hh + 1) * dk)
            vc = slice(hh * dv, (hh + 1) * dv)
            q = q_ref[rows, kc].astype(F32) * (dk ** -0.5)
            k = k_ref[rows, kc].astype(F32)
            o = _gla_chunk_head(q, k, v_ref[rows, vc], g_cum_all[:, kc], s_ref, hh)
            _rms_gate_store(o, nw, ga_ref[rows, vc].astype(F32), o_ref, rows, vc)

    @pl.when(t == pl.num_programs(1) - 1)
    def _():
        s_out_ref[0] = s_ref[...]


def _gla_prompt(proj, h, w_in, wup_pad, bup, norm_w, lay, batch, seq):
    heads, dk, dv = lay["gla_heads"], lay["gla_dk"], lay["gla_dv"]
    qk, vw = heads * dk, heads * dv
    d = h.shape[1]
    ct = min(seq, 2 * GLA_CHUNK)
    assert seq % ct == 0 and ct % GLA_CHUNK == 0
    nt = seq // ct
    row = lambda b, t: b * nt + t
    kern = functools.partial(_gla_prompt_kernel, heads=heads, dk=dk, dv=dv)
    return pl.pallas_call(
        kern,
        grid=(batch, nt),
        in_specs=[
            pl.BlockSpec((ct, qk), lambda b, t: (row(b, t), lay["qa"] // qk)),
            pl.BlockSpec((ct, qk), lambda b, t: (row(b, t), lay["ka"] // qk)),
            pl.BlockSpec((ct, vw), lambda b, t: (row(b, t), lay["va"] // vw)),
            pl.BlockSpec((ct, vw), lambda b, t: (row(b, t), lay["ga"] // vw)),
            pl.BlockSpec((ct, d), lambda b, t: (row(b, t), 0)),
            pl.BlockSpec((d, V7X_LANES), lambda b, t: (0, lay["gd_src"] // V7X_LANES)),
            pl.BlockSpec((V7X_LANES, qk), lambda b, t: (0, 0)),
            pl.BlockSpec((1, qk), lambda b, t: (0, 0)),
            pl.BlockSpec((1, dv), lambda b, t: (0, 0)),
        ],
        out_specs=[
            pl.BlockSpec((ct, vw), lambda b, t: (row(b, t), 0)),
            pl.BlockSpec((1, heads, dk, dv), lambda b, t: (b, 0, 0, 0)),
        ],
        out_shape=[
            jax.ShapeDtypeStruct((batch * seq, vw), BF16),
            jax.ShapeDtypeStruct((batch, heads, dk, dv), F32),
        ],
        scratch_shapes=[pltpu.VMEM((heads, dk, dv), F32)],
        compiler_params=_params(
            2, 2 * _nbytes((ct, qk), BF16), 3 * _nbytes((ct, vw), BF16), _nbytes((ct, d), BF16),
            _nbytes((d, V7X_LANES), F32), _nbytes((V7X_LANES, qk), F32), _nbytes((heads, dk, dv), F32),
            scratch_bytes=_nbytes((heads, dk, dv), F32)),
        name="gla_prompt",
    )(proj, proj, proj, proj, h, w_in, wup_pad, bup, norm_w)


def _rotary(x, cos, sin):
    half = x.shape[1] // 2
    x1, x2 = x[:, :half], x[:, half:]
    return jnp.concatenate([x1 * cos - x2 * sin, x1 * sin + x2 * cos], axis=1)


def _ret_prompt_kernel(q_ref, k_ref, v_ref, gb_ref, cos_ref, sin_ref, lg_ref, nw_ref,
                       o_ref, s_out_ref, s_ref, *, heads, dk, dv):
    t = pl.program_id(1)

    @pl.when(t == 0)
    def _():
        s_ref[...] = jnp.zeros_like(s_ref)

    c = q_ref.shape[0]
    cos, sin = cos_ref[...], sin_ref[...]
    nw = nw_ref[...]
    ri = lax.broadcasted_iota(jnp.int32, (c, c), 0)
    rj = lax.broadcasted_iota(jnp.int32, (c, c), 1)
    dist = (ri - rj).astype(F32)
    rowk = lax.broadcasted_iota(jnp.int32, (c, dk), 0).astype(F32)
    rows = slice(0, c)
    for hh in range(heads):
        lg = lg_ref[hh]
        kc = slice(hh * dk, (hh + 1) * dk)
        vc = slice(hh * dv, (hh + 1) * dv)
        qr = _rotary(q_ref[:, kc].astype(F32), cos, sin)
        kr = _rotary(k_ref[:, kc].astype(F32), cos, sin) * (dk ** -0.5)
        v = v_ref[:, vc]
        lgk = jnp.concatenate([lg] * (dk // V7X_LANES), axis=1)
        lgc = lg[:, :1]
        decay = jnp.exp(jnp.where(ri >= rj, dist * lgc, -jnp.inf))
        scores = _dot_nt(qr.astype(BF16), kr.astype(BF16)) * decay
        state = s_ref[hh]
        o = _dot((qr * jnp.exp((rowk + 1.0) * lgk)).astype(BF16), state.astype(BF16))
        o = o + _dot(scores.astype(BF16), v)
        k_tail = (kr * jnp.exp((float(c - 1) - rowk) * lgk)).astype(BF16)
        lgv = jnp.concatenate([lg] * (dv // V7X_LANES), axis=1)
        s_ref[hh] = jnp.exp(float(c) * lgv) * state + _dot_tn(k_tail, v)
        _ln_gate_store(o, nw, gb_ref[:, vc].astype(F32), o_ref, rows, vc)

    @pl.when(t == pl.num_programs(1) - 1)
    def _():
        s_out_ref[0] = s_ref[...]


def _ret_prompt(proj, cos, sin, log_gamma, norm_w, lay, batch, seq):
    heads, dk, dv = lay["ret_heads"], lay["ret_dk"], lay["ret_dv"]
    qk, vw = heads * dk, heads * dv
    c = min(seq, RET_CHUNK)
    assert seq % c == 0
    nt = seq // c
    half = dk // 2
    row = lambda b, t: b * nt + t
    kern = functools.partial(_ret_prompt_kernel, heads=heads, dk=dk, dv=dv)
    return pl.pallas_call(
        kern,
        grid=(batch, nt),
        in_specs=[
            pl.BlockSpec((c, qk), lambda b, t: (row(b, t), lay["qb"] // qk)),
            pl.BlockSpec((c, qk), lambda b, t: (row(b, t), lay["kb"] // qk)),
            pl.BlockSpec((c, vw), lambda b, t: (row(b, t), lay["vb"] // vw)),
            pl.BlockSpec((c, vw), lambda b, t: (row(b, t), lay["gb"] // vw)),
            pl.BlockSpec((c, half), lambda b, t: (t, 0)),
            pl.BlockSpec((c, half), lambda b, t: (t, 0)),
            pl.BlockSpec((heads, 1, V7X_LANES), lambda b, t: (0, 0, 0)),
            pl.BlockSpec((1, dv), lambda b, t: (0, 0)),
        ],
        out_specs=[
            pl.BlockSpec((c, vw), lambda b, t: (row(b, t), 0)),
            pl.BlockSpec((1, heads, dk, dv), lambda b, t: (b, 0, 0, 0)),
        ],
        out_shape=[
            jax.ShapeDtypeStruct((batch * seq, vw), BF16),
            jax.ShapeDtypeStruct((batch, heads, dk, dv), F32),
        ],
        scratch_shapes=[pltpu.VMEM((heads, dk, dv), F32)],
        compiler_params=_params(
            2, 2 * _nbytes((c, qk), BF16), 3 * _nbytes((c, vw), BF16), 2 * _nbytes((c, half), F32),
            _nbytes((heads, dk, dv), F32), scratch_bytes=_nbytes((heads, dk, dv), F32)),
        name="ret_prompt",
    )(proj, proj, proj, proj, cos, sin, log_gamma, norm_w)


def _decode_tokens(a_cols_fn, k, q, v, s_in_ref, s_out_ref, o_scr):
    tokens, dk = k.shape
    dv = v.shape[1]
    reps = dv // V7X_LANES
    for tt in range(tokens):
        k_cols = jnp.concatenate([_lane_bcast_cols(k[tt:tt + 1, :], dk)] * reps, axis=1)
        q_cols = jnp.concatenate([_lane_bcast_cols(q[tt:tt + 1, :], dk)] * reps, axis=1)
        s_new = a_cols_fn(tt) * s_in_ref[tt, 0] + k_cols * v[tt:tt + 1, :]
        s_out_ref[tt, 0] = s_new
        o_scr[tt:tt + 1, :] = jnp.sum(q_cols * s_new, axis=0, keepdims=True)


def _gla_decode_kernel(q_ref, k_ref, v_ref, ga_ref, h_ref, wgd_ref, wup_ref, bup_ref, nw_ref, s_in_ref,
                       o_ref, s_out_ref, o_scr, *, dk, dv):
    glog = _log_sigmoid(_gate_logits(h_ref, wgd_ref, wup_ref, bup_ref)) * (1.0 / GLA_GATE_NORM)
    a = jnp.exp(glog)
    q = q_ref[...].astype(F32) * (dk ** -0.5)
    k = k_ref[...].astype(F32)
    v = v_ref[...].astype(F32)
    reps = dv // V7X_LANES
    a_cols = lambda tt: jnp.concatenate([_lane_bcast_cols(a[tt:tt + 1, :], dk)] * reps, axis=1)
    _decode_tokens(a_cols, k, q, v, s_in_ref, s_out_ref, o_scr)
    tokens = q.shape[0]
    _rms_gate_store(o_scr[...], nw_ref[...], ga_ref[...].astype(F32), o_ref, slice(0, tokens), slice(0, dv))


def _gla_decode(proj, h, w_in, wup_pad, bup, norm_w, state, lay):
    heads, dk, dv = lay["gla_heads"], lay["gla_dk"], lay["gla_dv"]
    n, d = h.shape
    tk = DEC_TOKENS
    assert n % tk == 0
    kern = functools.partial(_gla_decode_kernel, dk=dk, dv=dv)
    return pl.pallas_call(
        kern,
        grid=(n // tk, heads),
        in_specs=[
            pl.BlockSpec((tk, dk), lambda i, hh: (i, lay["qa"] // dk + hh)),
            pl.BlockSpec((tk, dk), lambda i, hh: (i, lay["ka"] // dk + hh)),
            pl.BlockSpec((tk, dv), lambda i, hh: (i, lay["va"] // dv + hh)),
            pl.BlockSpec((tk, dv), lambda i, hh: (i, lay["ga"] // dv + hh)),
            pl.BlockSpec((tk, d), lambda i, hh: (i, 0)),
            pl.BlockSpec((d, V7X_LANES), lambda i, hh: (0, lay["gd_src"] // V7X_LANES)),
            pl.BlockSpec((V7X_LANES, dk), lambda i, hh: (0, hh)),
            pl.BlockSpec((1, dk), lambda i, hh: (0, hh)),
            pl.BlockSpec((1, dv), lambda i, hh: (0, 0)),
            pl.BlockSpec((tk, 1, dk, dv), lambda i, hh: (i, hh, 0, 0)),
        ],
        out_specs=[
            pl.BlockSpec((tk, dv), lambda i, hh: (i, hh)),
            pl.BlockSpec((tk, 1, dk, dv), lambda i, hh: (i, hh, 0, 0)),
        ],
        out_shape=[
            jax.ShapeDtypeStruct((n, heads * dv), BF16),
            jax.ShapeDtypeStruct(state.shape, state.dtype),
        ],
        scratch_shapes=[pltpu.VMEM((tk, dv), F32)],
        compiler_params=_params(
            2, 2 * _nbytes((tk, dk, dv), F32), _nbytes((tk, d), BF16), _nbytes((d, V7X_LANES), F32),
            scratch_bytes=_nbytes((tk, dv), F32)),
        name="gla_decode",
    )(proj, proj, proj, proj, h, w_in, wup_pad, bup, norm_w, state)


def _ret_decode_kernel(q_ref, k_ref, v_ref, gb_ref, cos_ref, sin_ref, lg_ref, nw_ref, s_in_ref,
                       o_ref, s_out_ref, o_scr, *, dk, dv):
    cos, sin = cos_ref[0:1, :], sin_ref[0:1, :]
    q = _rotary(q_ref[...].astype(F32), cos, sin)
    k = _rotary(k_ref[...].astype(F32), cos, sin) * (dk ** -0.5)
    v = v_ref[...].astype(F32)
    gamma = jnp.exp(jnp.concatenate([lg_ref[0]] * (dv // V7X_LANES), axis=1))
    _decode_tokens(lambda tt: gamma, k, q, v, s_in_ref, s_out_ref, o_scr)
    tokens = q.shape[0]
    _ln_gate_store(o_scr[...], nw_ref[...], gb_ref[...].astype(F32), o_ref, slice(0, tokens), slice(0, dv))


def _ret_decode(proj, cos, sin, log_gamma, norm_w, state, lay):
    heads, dk, dv = lay["ret_heads"], lay["ret_dk"], lay["ret_dv"]
    n = proj.shape[0]
    tk = DEC_TOKENS
    half = dk // 2
    kern = functools.partial(_ret_decode_kernel, dk=dk, dv=dv)
    return pl.pallas_call(
        kern,
        grid=(n // tk, heads),
        in_specs=[
            pl.BlockSpec((tk, dk), lambda i, hh: (i, lay["qb"] // dk + hh)),
            pl.BlockSpec((tk, dk), lambda i, hh: (i, lay["kb"] // dk + hh)),
            pl.BlockSpec((tk, dv), lambda i, hh: (i, lay["vb"] // dv + hh)),
            pl.BlockSpec((tk, dv), lambda i, hh: (i, lay["gb"] // dv + hh)),
            pl.BlockSpec((8, half), lambda i, hh: (0, 0)),
            pl.BlockSpec((8, half), lambda i, hh: (0, 0)),
            pl.BlockSpec((1, 1, V7X_LANES), lambda i, hh: (hh, 0, 0)),
            pl.BlockSpec((1, dv), lambda i, hh: (0, 0)),
            pl.BlockSpec((tk, 1, dk, dv), lambda i, hh: (i, hh, 0, 0)),
        ],
        out_specs=[
            pl.BlockSpec((tk, dv), lambda i, hh: (i, hh)),
            pl.BlockSpec((tk, 1, dk, dv), lambda i, hh: (i, hh, 0, 0)),
        ],
        out_shape=[
            jax.ShapeDtypeStruct((n, heads * dv), BF16),
            jax.ShapeDtypeStruct(state.shape, state.dtype),
        ],
        scratch_shapes=[pltpu.VMEM((tk, dv), F32)],
        compiler_params=_params(2, 2 * _nbytes((tk, dk, dv), F32), scratch_bytes=_nbytes((tk, dv), F32)),
        name="ret_decode",
    )(proj, proj, proj, proj, cos, sin, log_gamma, norm_w, state)


def _rope_table_kernel(cos_ref, sin_ref, *, pos0, half):
    rows, lanes = cos_ref.shape
    pos = (lax.broadcasted_iota(jnp.int32, (rows, lanes), 0) + (pl.program_id(0) * rows + pos0)).astype(F32)
    idx = lax.broadcasted_iota(jnp.int32, (rows, lanes), 1).astype(F32)
    inv = jnp.exp(idx * (-math.log(ROPE_BASE) / half))
    ang = pos * inv
    cos_ref[...] = jnp.cos(ang)
    sin_ref[...] = jnp.sin(ang)


def _rope_tables(n_pos, pos0, half):
    rows = min(n_pos, 256)
    assert n_pos % rows == 0
    kern = functools.partial(_rope_table_kernel, pos0=pos0, half=half)
    spec = pl.BlockSpec((rows, half), lambda i: (i, 0))
    return pl.pallas_call(
        kern,
        grid=(n_pos // rows,),
        in_specs=[],
        out_specs=[spec, spec],
        out_shape=[jax.ShapeDtypeStruct((n_pos, half), F32)] * 2,
        compiler_params=_params(1, 2 * _nbytes((rows, half), F32)),
        name="rope_tables",
    )()


def _merge_kernel(oa_ref, ob_ref, wa_ref, wb_ref, g0_ref, g1_ref, o_ref, wa_bf, wb_bf):
    @pl.when(pl.program_id(1) == 0)
    def _():
        wa_bf[...] = wa_ref[...].astype(BF16)
        wb_bf[...] = wb_ref[...].astype(BF16)

    ya = _dot(oa_ref[...], wa_bf[...])
    yb = _dot(ob_ref[...], wb_bf[...])
    m = _sigmoid(g0_ref[...].astype(F32)) * ya + _sigmoid(g1_ref[...].astype(F32)) * yb
    o_ref[...] = m.astype(o_ref.dtype)


def _merge(oa, ob, wa, wb, proj, lay):
    m, ka = oa.shape
    kb = ob.shape[1]
    d = wa.shape[1]
    tm = _row_tile(m, 1024)
    tn = min(d, 512)
    assert d % tn == 0 and lay["mg"] % tn == 0 and d % tn == 0
    g0 = lay["mg"] // tn
    g1 = (lay["mg"] + d) // tn
    return pl.pallas_call(
        _merge_kernel,
        grid=(d // tn, m // tm),
        in_specs=[
            pl.BlockSpec((tm, ka), lambda j, i: (i, 0)),
            pl.BlockSpec((tm, kb), lambda j, i: (i, 0)),
            pl.BlockSpec((ka, tn), lambda j, i: (0, j)),
            pl.BlockSpec((kb, tn), lambda j, i: (0, j)),
            pl.BlockSpec((tm, tn), lambda j, i: (i, g0 + j)),
            pl.BlockSpec((tm, tn), lambda j, i: (i, g1 + j)),
        ],
        out_specs=pl.BlockSpec((tm, tn), lambda j, i: (i, j)),
        out_shape=jax.ShapeDtypeStruct((m, d), BF16),
        scratch_shapes=[pltpu.VMEM((ka, tn), BF16), pltpu.VMEM((kb, tn), BF16)],
        compiler_params=_params(
            2, _nbytes((tm, ka), BF16), _nbytes((tm, kb), BF16), _nbytes((ka, tn), F32), _nbytes((kb, tn), F32),
            3 * _nbytes((tm, tn), BF16),
            scratch_bytes=_nbytes((ka, tn), BF16) + _nbytes((kb, tn), BF16) + 3 * _nbytes((tm, tn), F32)),
        name="merge",
    )(oa, ob, wa, wb, proj, proj)


def _proj_res_norm_kernel(a_ref, w_ref, res_ref, nw_ref, *out_refs, emit_sum):
    k = pl.program_id(1)
    acc_ref = out_refs[0]
    tm, d = acc_ref.shape
    col_chunk = min(d, 512)
    row_chunk = min(tm, 128)

    @pl.when(k == 0)
    def _():
        acc_ref[...] = res_ref[...]

    a = a_ref[...]
    for c in range(d // col_chunk):
        cs = slice(c * col_chunk, (c + 1) * col_chunk)
        acc_ref[:, cs] += _dot(a, w_ref[:, cs].astype(BF16))

    @pl.when(k == pl.num_programs(1) - 1)
    def _():
        def body(r, carry):
            rows = pl.ds(pl.multiple_of(r * row_chunk, row_chunk), row_chunk)
            x = acc_ref[rows, :]
            ms = jnp.mean(x * x, axis=-1, keepdims=True)
            y = x * lax.rsqrt(ms + EPS) * nw_ref[...]
            if emit_sum:
                out_refs[1][rows, :] = y.astype(out_refs[1].dtype)
            else:
                acc_ref[rows, :] = y
            return carry

        lax.fori_loop(0, tm // row_chunk, body, 0)


def _proj_res_norm(a, w, res, norm_w, emit_sum, tk):
    m, kdim = a.shape
    d = w.shape[1]
    tm = _row_tile(m, 1024)
    tk = min(kdim, tk)
    assert kdim % tk == 0
    kern = functools.partial(_proj_res_norm_kernel, emit_sum=emit_sum)
    row_spec = pl.BlockSpec((tm, d), lambda i, k: (i, 0))
    out_specs = [row_spec, row_spec] if emit_sum else [row_spec]
    out_shape = [jax.ShapeDtypeStruct((m, d), F32)]
    blocks = [_nbytes((tm, tk), BF16), _nbytes((tk, d), F32), 2 * _nbytes((tm, d), F32)]
    if emit_sum:
        out_shape.append(jax.ShapeDtypeStruct((m, d), BF16))
        blocks.append(_nbytes((tm, d), BF16))
    return pl.pallas_call(
        kern,
        grid=(m // tm, kdim // tk),
        in_specs=[
            pl.BlockSpec((tm, tk), lambda i, k: (i, k)),
            pl.BlockSpec((tk, d), lambda i, k: (k, 0)),
            row_spec,
            pl.BlockSpec((1, d), lambda i, k: (0, 0)),
        ],
        out_specs=out_specs,
        out_shape=out_shape,
        compiler_params=_params(2, *blocks),
        name="proj_res_norm",
    )(a, w, res, norm_w.reshape(1, d))


def _swiglu_kernel(h_ref, wg_ref, wu_ref, o_ref, wg_bf, wu_bf):
    @pl.when(pl.program_id(1) == 0)
    def _():
        wg_bf[...] = wg_ref[...].astype(BF16)
        wu_bf[...] = wu_ref[...].astype(BF16)

    h = h_ref[...]
    a = _dot(h, wg_bf[...])
    b = _dot(h, wu_bf[...])
    o_ref[...] = (_silu(a) * b).astype(o_ref.dtype)


def _swiglu(h, wg, wu):
    m, d = h.shape
    f = wg.shape[1]
    tm = _row_tile(m, 1024)
    tn = 512 if f % 512 == 0 else 256
    assert f % tn == 0
    return pl.pallas_call(
        _swiglu_kernel,
        grid=(f // tn, m // tm),
        in_specs=[
            pl.BlockSpec((tm, d), lambda j, i: (i, 0)),
            pl.BlockSpec((d, tn), lambda j, i: (0, j)),
            pl.BlockSpec((d, tn), lambda j, i: (0, j)),
        ],
        out_specs=pl.BlockSpec((tm, tn), lambda j, i: (i, j)),
        out_shape=jax.ShapeDtypeStruct((m, f), BF16),
        scratch_shapes=[pltpu.VMEM((d, tn), BF16), pltpu.VMEM((d, tn), BF16)],
        compiler_params=_params(
            2, _nbytes((tm, d), BF16), 2 * _nbytes((d, tn), F32), _nbytes((tm, tn), BF16),
            scratch_bytes=2 * _nbytes((d, tn), BF16) + 3 * _nbytes((tm, tn), F32)),
        name="swiglu",
    )(h, wg, wu)


def _layout(d_model, in_width, state_gla, state_ret, gate_rank):
    _, _, gh, gdk, gdv = state_gla.shape
    _, _, rh, rdk, rdv = state_ret.shape
    gqk, gv, rqk, rv = gh * gdk, gh * gdv, rh * rdk, rh * rdv
    lay = dict(gla_heads=gh, gla_dk=gdk, gla_dv=gdv, ret_heads=rh, ret_dk=rdk, ret_dv=rdv, rank=gate_rank)
    off = 0
    for name, width in (("qa", gqk), ("ka", gqk), ("va", gv), ("ga", gv), ("qb", rqk), ("kb", rqk),
                        ("vb", rv), ("gb", rv), ("mg", 2 * d_model)):
        lay[name] = off
        off += width
    lay["out_cols"] = off
    lay["plain_cols"] = 2 * gqk + gv
    lay["gd_src"] = lay["plain_cols"]
    assert lay["gd_src"] % V7X_LANES == 0 and gate_rank <= V7X_LANES
    assert in_width == off + gate_rank
    return lay


def _layer(x_p, x_s, st_gla, st_ret, wts, lay, tables, final_norm):
    (norm_mix, w_in, w_gate_up, b_gate, gla_norm_w, w_gla_up, ret_norm_w, w_ret_up, w_out, norm_ffn,
     w_ffn_gate, w_ffn_up, w_ffn_down) = wts
    batch, seq, d = x_p.shape
    rank = lay["rank"]
    gqk = lay["gla_heads"] * lay["gla_dk"]
    wup_pad = jnp.zeros((V7X_LANES, gqk), F32).at[:rank].set(w_gate_up)
    bup = b_gate.reshape(1, gqk)
    gnw = gla_norm_w.reshape(1, -1)
    rnw = ret_norm_w.reshape(1, -1)
    tn = 1024 if (lay["out_cols"] % 1024 == 0 and lay["plain_cols"] % 1024 == 0) else 512
    cos_p, sin_p, cos_s, sin_s, log_gamma = tables

    outs = []
    for x, is_prompt in ((x_p.reshape(batch * seq, d), True), (x_s.reshape(-1, d), False)):
        h = _rmsnorm_bf16(x, norm_mix)
        proj = _in_proj(h, w_in, lay["plain_cols"], rank, lay["out_cols"], tn)
        if is_prompt:
            oa, sa = _gla_prompt(proj, h, w_in, wup_pad, bup, gnw, lay, batch, seq)
            ob, sb = _ret_prompt(proj, cos_p, sin_p, log_gamma, rnw, lay, batch, seq)
        else:
            oa, sa = _gla_decode(proj, h, w_in, wup_pad, bup, gnw, st_gla, lay)
            ob, sb = _ret_decode(proj, cos_s, sin_s, log_gamma, rnw, st_ret, lay)
        mrg = _merge(oa, ob, w_gla_up, w_ret_up, proj, lay)
        x1, h2 = _proj_res_norm(mrg, w_out, x, norm_ffn, True, 512)
        act = _swiglu(h2, w_ffn_gate, w_ffn_up)
        (y,) = _proj_res_norm(act, w_ffn_down, x1, final_norm, False, 512)
        outs.append((y, sa, sb))
    return outs


def kernel(x_prompt, x_sample, state_gla, state_ret, norm_mix, w_in, w_gla_gate_up, b_gla_gate, gla_norm_w,
           w_gla_up, ret_norm_w, w_ret_up, w_out, norm_ffn, w_ffn_gate, w_ffn_up, w_ffn_down, norm_final):
    depth = w_in.shape[0]
    assert depth == 1, "single-layer trunk"
    batch, seq, d = x_prompt.shape
    lay = _layout(d, w_in.shape[-1], state_gla, state_ret, w_gla_gate_up.shape[1])
    rh, rdk = lay["ret_heads"], lay["ret_dk"]
    half = rdk // 2
    assert half == V7X_LANES
    cos_p, sin_p = _rope_tables(seq, 0, half)
    cos_s, sin_s = _rope_tables(8, PAST_LEN, half)
    lg = jnp.log1p(-jnp.exp(jnp.linspace(math.log(1.0 / 32), math.log(1.0 / 512), rh))).astype(F32)
    log_gamma = jnp.broadcast_to(lg[:, None, None], (rh, 1, V7X_LANES))
    tables = (cos_p, sin_p, cos_s, sin_s, log_gamma)

    wts = (norm_mix[0], w_in[0], w_gla_gate_up[0], b_gla_gate[0], gla_norm_w[0], w_gla_up[0], ret_norm_w[0],
           w_ret_up[0], w_out[0], norm_ffn[0], w_ffn_gate[0], w_ffn_up[0], w_ffn_down[0])
    (y_p, ga_p, re_p), (y_s, ga_s, re_s) = _layer(
        x_prompt, x_sample, state_gla[0], state_ret[0], wts, lay, tables, norm_final)

    sd = state_gla.dtype
    return (y_p.reshape(batch, seq, d), y_s.reshape(x_sample.shape),
            ga_p[None].astype(sd), re_p[None].astype(state_ret.dtype),
            ga_s[None].astype(sd), re_s[None].astype(state_ret.dtype))
```

```python
import functools
import math

import numpy as np
import jax
import jax.numpy as jnp
from jax import lax
from jax.experimental import pallas as pl
from jax.experimental.pallas import tpu as pltpu

EPS = 1e-6
ROPE_BASE = 10000.0
GLA_GATE_NORM = 16.0
PAST_LEN = 16384

V7X_LANES = 128
V7X_VMEM_BYTES = 64 * 1024 * 1024
V7X_VMEM_REQUEST_CAP = 58 * 1024 * 1024
COMPILER_SCRATCH_BYTES = 12 * 1024 * 1024

GLA_CHUNK = 64
GLA_SUB = 16
RET_CHUNK = 128
DEC_TOKENS = 16

BF16 = jnp.bfloat16
F32 = jnp.float32


def _params(n_axes, *block_bytes, scratch_bytes=0):
    need = 2 * sum(block_bytes) + scratch_bytes + COMPILER_SCRATCH_BYTES
    return pltpu.CompilerParams(
        dimension_semantics=("arbitrary",) * n_axes,
        vmem_limit_bytes=int(min(V7X_VMEM_REQUEST_CAP, need)),
    )


def _nbytes(shape, dtype):
    return int(np.prod(shape)) * jnp.dtype(dtype).itemsize


def _sigmoid(x):
    return 1.0 / (1.0 + jnp.exp(-x))


def _silu(x):
    return x * _sigmoid(x)


def _log_sigmoid(x):
    return jnp.minimum(x, 0.0) - jnp.log1p(jnp.exp(-jnp.abs(x)))


def _dot(a, b):
    return jnp.dot(a, b, preferred_element_type=F32)


def _dot_nt(a, b):
    return lax.dot_general(a, b, (((1,), (1,)), ((), ())), preferred_element_type=F32)


def _dot_tn(a, b):
    return lax.dot_general(a, b, (((0,), (0,)), ((), ())), preferred_element_type=F32)


def _row_tile(m, want):
    t = min(m, want)
    assert m % t == 0, (m, t)
    return t


class _Rows:
    def __init__(self, m_p, tail, want):
        self.tm = min(m_p, want)
        assert m_p % self.tm == 0 and 0 < tail <= self.tm
        self.m_p, self.tail = m_p, tail
        self.n_full = m_p // self.tm
        self.steps = self.n_full + 1
        self.total = m_p + tail

    def prompt_block(self, i):
        return jnp.minimum(i, self.n_full - 1)

    def each(self, i, fn):
        @pl.when(i < self.n_full)
        def _():
            fn(slice(0, self.tm), False)

        @pl.when(i == self.n_full)
        def _():
            fn(slice(0, self.tail), True)


def _rmsnorm_rows(x, w):
    ms = jnp.mean(x * x, axis=-1, keepdims=True)
    return x * lax.rsqrt(ms + EPS) * w


def _rmsnorm_kernel(xp_ref, xs_ref, w_ref, o_ref, *, rows):
    def fn(r, is_tail):
        x = xs_ref[...] if is_tail else xp_ref[...]
        o_ref[r, :] = _rmsnorm_rows(x, w_ref[...]).astype(o_ref.dtype)

    rows.each(pl.program_id(0), fn)


def _rmsnorm_bf16(x_p, x_s, w):
    d = x_p.shape[1]
    rows = _Rows(x_p.shape[0], x_s.shape[0], 512)
    tm = rows.tm
    return pl.pallas_call(
        functools.partial(_rmsnorm_kernel, rows=rows),
        grid=(rows.steps,),
        in_specs=[
            pl.BlockSpec((tm, d), lambda i: (rows.prompt_block(i), 0)),
            pl.BlockSpec((rows.tail, d), lambda i: (0, 0)),
            pl.BlockSpec((1, d), lambda i: (0, 0)),
        ],
        out_specs=pl.BlockSpec((tm, d), lambda i: (i, 0)),
        out_shape=jax.ShapeDtypeStruct((rows.total, d), BF16),
        compiler_params=_params(1, _nbytes((tm, d), F32), _nbytes((rows.tail, d), F32), _nbytes((tm, d), BF16)),
        name="rmsnorm",
    )(x_p, x_s, w.reshape(1, d))


def _in_proj_kernel(h_ref, wm_ref, wn_ref, o_ref, wbf_ref, *, n_plain, shift, rows):
    j = pl.program_id(0)
    i = pl.program_id(1)
    tn = wbf_ref.shape[0]

    @pl.when(jnp.logical_and(i == 0, j < n_plain))
    def _():
        wbf_ref[...] = wm_ref[...].astype(BF16)

    @pl.when(jnp.logical_and(i == 0, j >= n_plain))
    def _():
        wbf_ref[0:tn - shift, :] = wm_ref[shift:tn, :].astype(BF16)
        wbf_ref[tn - shift:tn, :] = wn_ref[...].astype(BF16)

    def fn(r, is_tail):
        o_ref[r, :] = _dot_nt(h_ref[r, :], wbf_ref[...]).astype(o_ref.dtype)

    rows.each(i, fn)


def _in_proj(h, w_in_t, plain_cols, shift, out_cols, tn, rows):
    d = h.shape[1]
    tm = rows.tm
    assert plain_cols % tn == 0 and out_cols % tn == 0 and tn % shift == 0 and shift % 8 == 0
    n_plain = plain_cols // tn
    kern = functools.partial(_in_proj_kernel, n_plain=n_plain, shift=shift, rows=rows)
    return pl.pallas_call(
        kern,
        grid=(out_cols // tn, rows.steps),
        in_specs=[
            pl.BlockSpec((tm, d), lambda j, i: (i, 0)),
            pl.BlockSpec((tn, d), lambda j, i: (j, 0)),
            pl.BlockSpec((shift, d), lambda j, i: ((j + 1) * (tn // shift), 0)),
        ],
        out_specs=pl.BlockSpec((tm, tn), lambda j, i: (i, j)),
        out_shape=jax.ShapeDtypeStruct((rows.total, out_cols), BF16),
        scratch_shapes=[pltpu.VMEM((tn, d), BF16)],
        compiler_params=_params(
            2, _nbytes((tm, d), BF16), _nbytes((tn, d), F32), _nbytes((shift, d), F32),
            _nbytes((tm, tn), BF16), scratch_bytes=_nbytes((tn, d), BF16) + _nbytes((tm, tn), F32)),
        name="in_proj",
    )(h, w_in_t, w_in_t)


def _gate_logits(h_ref, wgd_ref, wup_ref, bup_ref):
    gd = _dot_nt(h_ref[...], wgd_ref[...].astype(BF16))
    return _dot(gd.astype(BF16), wup_ref[...].astype(BF16)) + bup_ref[...]


def _prefix_sum_rows(tri_bf16, g):
    g0 = g.astype(BF16)
    r1 = g - g0.astype(F32)
    g1 = r1.astype(BF16)
    g2 = (r1 - g1.astype(F32)).astype(BF16)
    return _dot(tri_bf16, g0) + _dot(tri_bf16, g1) + _dot(tri_bf16, g2)


def _lane_bcast_cols(row, n):
    parts = []
    for c in range(n // V7X_LANES):
        tile = jnp.broadcast_to(row[:, c * V7X_LANES:(c + 1) * V7X_LANES], (V7X_LANES, V7X_LANES))
        parts.append(tile.T)
    return parts[0] if len(parts) == 1 else jnp.concatenate(parts, axis=0)


def _rms_gate_store(o, w, gate, out_ref, rows, cols):
    ms = jnp.mean(o * o, axis=-1, keepdims=True)
    y = o * lax.rsqrt(ms + EPS) * w
    out_ref[rows, cols] = (y * _silu(gate)).astype(out_ref.dtype)


def _ln_gate_store(o, w, gate, out_ref, rows, cols):
    mu = jnp.mean(o, axis=-1, keepdims=True)
    dlt = o - mu
    var = jnp.mean(dlt * dlt, axis=-1, keepdims=True)
    y = dlt * lax.rsqrt(var + EPS) * w
    out_ref[rows, cols] = (y * _silu(gate)).astype(out_ref.dtype)


def _gla_chunk_head(q, k, v, g_cum, s_ref, hh):
    c, dk = q.shape
    nsub = c // GLA_SUB
    row = lax.broadcasted_iota(jnp.int32, (c, dk), 0)
    g_last = g_cum[c - 1:c, :]
    state = s_ref[hh]

    o = _dot((q * jnp.exp(g_cum)).astype(BF16), state.astype(BF16))

    g_ref_rows = jnp.concatenate(
        [jnp.broadcast_to(g_cum[GLA_SUB * b:GLA_SUB * b + 1, :], (GLA_SUB, dk)) for b in range(nsub)], axis=0)
    q_rel = (q * jnp.exp(g_cum - g_ref_rows)).astype(BF16)
    blocks = [jnp.zeros((GLA_SUB, c), F32)]
    for b in range(1, nsub):
        g_ref = g_cum[GLA_SUB * b:GLA_SUB * b + 1, :]
        k_rel = k * jnp.exp(jnp.where(row < GLA_SUB * b, g_ref - g_cum, -jnp.inf))
        blocks.append(_dot_nt(q_rel[GLA_SUB * b:GLA_SUB * (b + 1), :], k_rel.astype(BF16)))
    scores = jnp.concatenate(blocks, axis=0)

    srow = lax.broadcasted_iota(jnp.int32, (c, c), 0)
    scol = lax.broadcasted_iota(jnp.int32, (c, c), 1)
    rel = scol - jnp.bitwise_and(srow, -GLA_SUB)
    row_in_sub = jnp.bitwise_and(row, GLA_SUB - 1)
    for r in range(GLA_SUB):
        k_r = jnp.concatenate(
            [jnp.broadcast_to(k[GLA_SUB * b + r:GLA_SUB * b + r + 1, :], (GLA_SUB, dk)) for b in range(nsub)], axis=0)
        g_r = jnp.concatenate(
            [jnp.broadcast_to(g_cum[GLA_SUB * b + r:GLA_SUB * b + r + 1, :], (GLA_SUB, dk)) for b in range(nsub)],
            axis=0)
        e = jnp.exp(jnp.where(row_in_sub >= r, g_cum - g_r, -jnp.inf))
        col = jnp.sum(q * e * k_r, axis=-1, keepdims=True)
        scores = scores + jnp.where(rel == r, col, 0.0)

    o = o + _dot(scores.astype(BF16), v)

    k_tail = (k * jnp.exp(g_last - g_cum)).astype(BF16)
    decay = _lane_bcast_cols(jnp.exp(g_last), dk)
    dv = v.shape[1]
    decay_full = jnp.concatenate([decay] * (dv // V7X_LANES), axis=1)
    s_ref[hh] = decay_full * state + _dot_tn(k_tail, v)
    return o


def _gla_prompt_kernel(q_ref, k_ref, v_ref, ga_ref, h_ref, wgd_ref, wup_ref, bup_ref, nw_ref,
                       o_ref, s_out_ref, s_ref, *, heads, dk, dv):
    t = pl.program_id(1)

    @pl.when(t == 0)
    def _():
        s_ref[...] = jnp.zeros_like(s_ref)

    ct = q_ref.shape[0]
    glog = _log_sigmoid(_gate_logits(h_ref, wgd_ref, wup_ref, bup_ref)) * (1.0 / GLA_GATE_NORM)
    ti = lax.broadcasted_iota(jnp.int32, (GLA_CHUNK, GLA_CHUNK), 0)
    tj = lax.broadcasted_iota(jnp.int32, (GLA_CHUNK, GLA_CHUNK), 1)
    tri = (ti >= tj).astype(BF16)
    nw = nw_ref[...]
    for c in range(ct // GLA_CHUNK):
        rows = slice(c * GLA_CHUNK, (c + 1) * GLA_CHUNK)
        g_cum_all = _prefix_sum_rows(tri, glog[rows, :])
        for hh in range(heads):
            kc = slice(hh * dk, (hh + 1) * dk)
            vc = slice(hh * dv, (hh + 1) * dv)
            q = q_ref[rows, kc].astype(F32) * (dk ** -0.5)
            k = k_ref[rows, kc].astype(F32)
            o = _gla_chunk_head(q, k, v_ref[rows, vc], g_cum_all[:, kc], s_ref, hh)
            _rms_gate_store(o, nw, ga_ref[rows, vc].astype(F32), o_ref, rows, vc)

    @pl.when(t == pl.num_programs(1) - 1)
    def _():
        s_out_ref[0] = s_ref[...]


def _gla_prompt(proj, h, w_in, wup_pad, bup, norm_w, lay, batch, seq):
    heads, dk, dv = lay["gla_heads"], lay["gla_dk"], lay["gla_dv"]
    qk, vw = heads * dk, heads * dv
    d = h.shape[1]
    ct = min(seq, 2 * GLA_CHUNK)
    assert seq % ct == 0 and ct % GLA_CHUNK == 0
    nt = seq // ct
    row = lambda b, t: b * nt + t
    kern = functools.partial(_gla_prompt_kernel, heads=heads, dk=dk, dv=dv)
    return pl.pallas_call(
        kern,
        grid=(batch, nt),
        in_specs=[
            pl.BlockSpec((ct, qk), lambda b, t: (row(b, t), lay["qa"] // qk)),
            pl.BlockSpec((ct, qk), lambda b, t: (row(b, t), lay["ka"] // qk)),
            pl.BlockSpec((ct, vw), lambda b, t: (row(b, t), lay["va"] // vw)),
            pl.BlockSpec((ct, vw), lambda b, t: (row(b, t), lay["ga"] // vw)),
            pl.BlockSpec((ct, d), lambda b, t: (row(b, t), 0)),
            pl.BlockSpec((V7X_LANES, d), lambda b, t: (lay["gd_src"] // V7X_LANES, 0)),
            pl.BlockSpec((V7X_LANES, qk), lambda b, t: (0, 0)),
            pl.BlockSpec((1, qk), lambda b, t: (0, 0)),
            pl.BlockSpec((1, dv), lambda b, t: (0, 0)),
        ],
        out_specs=[
            pl.BlockSpec((ct, vw), lambda b, t: (row(b, t), 0)),
            pl.BlockSpec((1, heads, dk, dv), lambda b, t: (b, 0, 0, 0)),
        ],
        out_shape=[
            jax.ShapeDtypeStruct((batch * seq, vw), BF16),
            jax.ShapeDtypeStruct((batch, heads, dk, dv), F32),
        ],
        scratch_shapes=[pltpu.VMEM((heads, dk, dv), F32)],
        compiler_params=_params(
            2, 2 * _nbytes((ct, qk), BF16), 3 * _nbytes((ct, vw), BF16), _nbytes((ct, d), BF16),
            _nbytes((d, V7X_LANES), F32), _nbytes((V7X_LANES, qk), F32), _nbytes((heads, dk, dv), F32),
            scratch_bytes=_nbytes((heads, dk, dv), F32)),
        name="gla_prompt",
    )(proj, proj, proj, proj, h, w_in, wup_pad, bup, norm_w)


def _rotary(x, cos, sin):
    half = x.shape[1] // 2
    x1, x2 = x[:, :half], x[:, half:]
    return jnp.concatenate([x1 * cos - x2 * sin, x1 * sin + x2 * cos], axis=1)


def _ret_prompt_kernel(q_ref, k_ref, v_ref, gb_ref, cos_ref, sin_ref, lg_ref, nw_ref,
                       o_ref, s_out_ref, s_ref, *, heads, dk, dv):
    t = pl.program_id(1)

    @pl.when(t == 0)
    def _():
        s_ref[...] = jnp.zeros_like(s_ref)

    c = q_ref.shape[0]
    cos, sin = cos_ref[...], sin_ref[...]
    nw = nw_ref[...]
    ri = lax.broadcasted_iota(jnp.int32, (c, c), 0)
    rj = lax.broadcasted_iota(jnp.int32, (c, c), 1)
    dist = (ri - rj).astype(F32)
    rowk = lax.broadcasted_iota(jnp.int32, (c, dk), 0).astype(F32)
    rows = slice(0, c)
    for hh in range(heads):
        lg = lg_ref[hh]
        kc = slice(hh * dk, (hh + 1) * dk)
        vc = slice(hh * dv, (hh + 1) * dv)
        qr = _rotary(q_ref[:, kc].astype(F32), cos, sin)
        kr = _rotary(k_ref[:, kc].astype(F32), cos, sin) * (dk ** -0.5)
        v = v_ref[:, vc]
        lgk = jnp.concatenate([lg] * (dk // V7X_LANES), axis=1)
        lgc = lg[:, :1]
        decay = jnp.exp(jnp.where(ri >= rj, dist * lgc, -jnp.inf))
        scores = _dot_nt(qr.astype(BF16), kr.astype(BF16)) * decay
        state = s_ref[hh]
        o = _dot((qr * jnp.exp((rowk + 1.0) * lgk)).astype(BF16), state.astype(BF16))
        o = o + _dot(scores.astype(BF16), v)
        k_tail = (kr * jnp.exp((float(c - 1) - rowk) * lgk)).astype(BF16)
        lgv = jnp.concatenate([lg] * (dv // V7X_LANES), axis=1)
        s_ref[hh] = jnp.exp(float(c) * lgv) * state + _dot_tn(k_tail, v)
        _ln_gate_store(o, nw, gb_ref[:, vc].astype(F32), o_ref, rows, vc)

    @pl.when(t == pl.num_programs(1) - 1)
    def _():
        s_out_ref[0] = s_ref[...]


def _ret_prompt(proj, cos, sin, log_gamma, norm_w, lay, batch, seq):
    heads, dk, dv = lay["ret_heads"], lay["ret_dk"], lay["ret_dv"]
    qk, vw = heads * dk, heads * dv
    c = min(seq, RET_CHUNK)
    assert seq % c == 0
    nt = seq // c
    half = dk // 2
    row = lambda b, t: b * nt + t
    kern = functools.partial(_ret_prompt_kernel, heads=heads, dk=dk, dv=dv)
    return pl.pallas_call(
        kern,
        grid=(batch, nt),
        in_specs=[
            pl.BlockSpec((c, qk), lambda b, t: (row(b, t), lay["qb"] // qk)),
            pl.BlockSpec((c, qk), lambda b, t: (row(b, t), lay["kb"] // qk)),
            pl.BlockSpec((c, vw), lambda b, t: (row(b, t), lay["vb"] // vw)),
            pl.BlockSpec((c, vw), lambda b, t: (row(b, t), lay["gb"] // vw)),
            pl.BlockSpec((c, half), lambda b, t: (t, 0)),
            pl.BlockSpec((c, half), lambda b, t: (t, 0)),
            pl.BlockSpec((heads, 1, V7X_LANES), lambda b, t: (0, 0, 0)),
            pl.BlockSpec((1, dv), lambda b, t: (0, 0)),
        ],
        out_specs=[
            pl.BlockSpec((c, vw), lambda b, t: (row(b, t), 0)),
            pl.BlockSpec((1, heads, dk, dv), lambda b, t: (b, 0, 0, 0)),
        ],
        out_shape=[
            jax.ShapeDtypeStruct((batch * seq, vw), BF16),
            jax.ShapeDtypeStruct((batch, heads, dk, dv), F32),
        ],
        scratch_shapes=[pltpu.VMEM((heads, dk, dv), F32)],
        compiler_params=_params(
            2, 2 * _nbytes((c, qk), BF16), 3 * _nbytes((c, vw), BF16), 2 * _nbytes((c, half), F32),
            _nbytes((heads, dk, dv), F32), scratch_bytes=_nbytes((heads, dk, dv), F32)),
        name="ret_prompt",
    )(proj, proj, proj, proj, cos, sin, log_gamma, norm_w)


def _decode_tokens(a_cols_fn, k, q, v, s_in_ref, s_out_ref, o_scr):
    tokens, dk = k.shape
    dv = v.shape[1]
    reps = dv // V7X_LANES
    for tt in range(tokens):
        k_cols = jnp.concatenate([_lane_bcast_cols(k[tt:tt + 1, :], dk)] * reps, axis=1)
        q_cols = jnp.concatenate([_lane_bcast_cols(q[tt:tt + 1, :], dk)] * reps, axis=1)
        s_new = a_cols_fn(tt) * s_in_ref[tt, 0] + k_cols * v[tt:tt + 1, :]
        s_out_ref[tt, 0] = s_new
        o_scr[tt:tt + 1, :] = jnp.sum(q_cols * s_new, axis=0, keepdims=True)


def _gla_decode_kernel(q_ref, k_ref, v_ref, ga_ref, h_ref, wgd_ref, wup_ref, bup_ref, nw_ref, s_in_ref,
                       o_ref, s_out_ref, o_scr, *, dk, dv):
    glog = _log_sigmoid(_gate_logits(h_ref, wgd_ref, wup_ref, bup_ref)) * (1.0 / GLA_GATE_NORM)
    a = jnp.exp(glog)
    q = q_ref[...].astype(F32) * (dk ** -0.5)
    k = k_ref[...].astype(F32)
    v = v_ref[...].astype(F32)
    reps = dv // V7X_LANES
    a_cols = lambda tt: jnp.concatenate([_lane_bcast_cols(a[tt:tt + 1, :], dk)] * reps, axis=1)
    _decode_tokens(a_cols, k, q, v, s_in_ref, s_out_ref, o_scr)
    tokens = q.shape[0]
    _rms_gate_store(o_scr[...], nw_ref[...], ga_ref[...].astype(F32), o_ref, slice(0, tokens), slice(0, dv))


def _gla_decode(proj, h, w_in, wup_pad, bup, norm_w, state, lay, m_p):
    heads, dk, dv = lay["gla_heads"], lay["gla_dk"], lay["gla_dv"]
    n, d = state.shape[0], h.shape[1]
    tk = DEC_TOKENS
    assert n % tk == 0 and m_p % tk == 0
    r0 = m_p // tk
    kern = functools.partial(_gla_decode_kernel, dk=dk, dv=dv)
    return pl.pallas_call(
        kern,
        grid=(n // tk, heads),
        in_specs=[
            pl.BlockSpec((tk, dk), lambda i, hh: (r0 + i, lay["qa"] // dk + hh)),
            pl.BlockSpec((tk, dk), lambda i, hh: (r0 + i, lay["ka"] // dk + hh)),
            pl.BlockSpec((tk, dv), lambda i, hh: (r0 + i, lay["va"] // dv + hh)),
            pl.BlockSpec((tk, dv), lambda i, hh: (r0 + i, lay["ga"] // dv + hh)),
            pl.BlockSpec((tk, d), lambda i, hh: (r0 + i, 0)),
            pl.BlockSpec((V7X_LANES, d), lambda i, hh: (lay["gd_src"] // V7X_LANES, 0)),
            pl.BlockSpec((V7X_LANES, dk), lambda i, hh: (0, hh)),
            pl.BlockSpec((1, dk), lambda i, hh: (0, hh)),
            pl.BlockSpec((1, dv), lambda i, hh: (0, 0)),
            pl.BlockSpec((tk, 1, dk, dv), lambda i, hh: (i, hh, 0, 0)),
        ],
        out_specs=[
            pl.BlockSpec((tk, dv), lambda i, hh: (i, hh)),
            pl.BlockSpec((tk, 1, dk, dv), lambda i, hh: (i, hh, 0, 0)),
        ],
        out_shape=[
            jax.ShapeDtypeStruct((n, heads * dv), BF16),
            jax.ShapeDtypeStruct(state.shape, state.dtype),
        ],
        scratch_shapes=[pltpu.VMEM((tk, dv), F32)],
        compiler_params=_params(
            2, 2 * _nbytes((tk, dk, dv), F32), _nbytes((tk, d), BF16), _nbytes((d, V7X_LANES), F32),
            scratch_bytes=_nbytes((tk, dv), F32)),
        name="gla_decode",
    )(proj, proj, proj, proj, h, w_in, wup_pad, bup, norm_w, state)


def _ret_decode_kernel(q_ref, k_ref, v_ref, gb_ref, cos_ref, sin_ref, lg_ref, nw_ref, s_in_ref,
                       o_ref, s_out_ref, o_scr, *, dk, dv):
    cos, sin = cos_ref[0:1, :], sin_ref[0:1, :]
    q = _rotary(q_ref[...].astype(F32), cos, sin)
    k = _rotary(k_ref[...].astype(F32), cos, sin) * (dk ** -0.5)
    v = v_ref[...].astype(F32)
    gamma = jnp.exp(jnp.concatenate([lg_ref[0]] * (dv // V7X_LANES), axis=1))
    _decode_tokens(lambda tt: gamma, k, q, v, s_in_ref, s_out_ref, o_scr)
    tokens = q.shape[0]
    _ln_gate_store(o_scr[...], nw_ref[...], gb_ref[...].astype(F32), o_ref, slice(0, tokens), slice(0, dv))


def _ret_decode(proj, cos, sin, log_gamma, norm_w, state, lay, m_p):
    heads, dk, dv = lay["ret_heads"], lay["ret_dk"], lay["ret_dv"]
    n = state.shape[0]
    tk = DEC_TOKENS
    assert n % tk == 0 and m_p % tk == 0
    r0 = m_p // tk
    half = dk // 2
    kern = functools.partial(_ret_decode_kernel, dk=dk, dv=dv)
    return pl.pallas_call(
        kern,
        grid=(n // tk, heads),
        in_specs=[
            pl.BlockSpec((tk, dk), lambda i, hh: (r0 + i, lay["qb"] // dk + hh)),
            pl.BlockSpec((tk, dk), lambda i, hh: (r0 + i, lay["kb"] // dk + hh)),
            pl.BlockSpec((tk, dv), lambda i, hh: (r0 + i, lay["vb"] // dv + hh)),
            pl.BlockSpec((tk, dv), lambda i, hh: (r0 + i, lay["gb"] // dv + hh)),
            pl.BlockSpec((8, half), lambda i, hh: (0, 0)),
            pl.BlockSpec((8, half), lambda i, hh: (0, 0)),
            pl.BlockSpec((1, 1, V7X_LANES), lambda i, hh: (hh, 0, 0)),
            pl.BlockSpec((1, dv), lambda i, hh: (0, 0)),
            pl.BlockSpec((tk, 1, dk, dv), lambda i, hh: (i, hh, 0, 0)),
        ],
        out_specs=[
            pl.BlockSpec((tk, dv), lambda i, hh: (i, hh)),
            pl.BlockSpec((tk, 1, dk, dv), lambda i, hh: (i, hh, 0, 0)),
        ],
        out_shape=[
            jax.ShapeDtypeStruct((n, heads * dv), BF16),
            jax.ShapeDtypeStruct(state.shape, state.dtype),
        ],
        scratch_shapes=[pltpu.VMEM((tk, dv), F32)],
        compiler_params=_params(2, 2 * _nbytes((tk, dk, dv), F32), scratch_bytes=_nbytes((tk, dv), F32)),
        name="ret_decode",
    )(proj, proj, proj, proj, cos, sin, log_gamma, norm_w, state)


def _rope_table_kernel(cos_ref, sin_ref, *, pos0, half):
    rows, lanes = cos_ref.shape
    pos = (lax.broadcasted_iota(jnp.int32, (rows, lanes), 0) + (pl.program_id(0) * rows + pos0)).astype(F32)
    idx = lax.broadcasted_iota(jnp.int32, (rows, lanes), 1).astype(F32)
    inv = jnp.exp(idx * (-math.log(ROPE_BASE) / half))
    ang = pos * inv
    cos_ref[...] = jnp.cos(ang)
    sin_ref[...] = jnp.sin(ang)


def _rope_tables(n_pos, pos0, half):
    rows = min(n_pos, 256)
    assert n_pos % rows == 0
    kern = functools.partial(_rope_table_kernel, pos0=pos0, half=half)
    spec = pl.BlockSpec((rows, half), lambda i: (i, 0))
    return pl.pallas_call(
        kern,
        grid=(n_pos // rows,),
        in_specs=[],
        out_specs=[spec, spec],
        out_shape=[jax.ShapeDtypeStruct((n_pos, half), F32)] * 2,
        compiler_params=_params(1, 2 * _nbytes((rows, half), F32)),
        name="rope_tables",
    )()


def _merge_kernel(oap_ref, obp_ref, oas_ref, obs_ref, wa_ref, wb_ref, g0_ref, g1_ref, o_ref, wa_bf, wb_bf, *, rows):
    i = pl.program_id(1)

    @pl.when(i == 0)
    def _():
        wa_bf[...] = wa_ref[...].astype(BF16)
        wb_bf[...] = wb_ref[...].astype(BF16)

    def fn(r, is_tail):
        oa = oas_ref[...] if is_tail else oap_ref[...]
        ob = obs_ref[...] if is_tail else obp_ref[...]
        ya = _dot(oa, wa_bf[...])
        yb = _dot(ob, wb_bf[...])
        m = _sigmoid(g0_ref[r, :].astype(F32)) * ya + _sigmoid(g1_ref[r, :].astype(F32)) * yb
        o_ref[r, :] = m.astype(o_ref.dtype)

    rows.each(i, fn)


def _merge(oa_p, ob_p, oa_s, ob_s, wa, wb, proj, lay, rows):
    ka, kb = oa_p.shape[1], ob_p.shape[1]
    d = wa.shape[1]
    tm, tail = rows.tm, rows.tail
    tn = min(d, 512)
    assert d % tn == 0 and lay["mg"] % tn == 0
    g0 = lay["mg"] // tn
    g1 = (lay["mg"] + d) // tn
    return pl.pallas_call(
        functools.partial(_merge_kernel, rows=rows),
        grid=(d // tn, rows.steps),
        in_specs=[
            pl.BlockSpec((tm, ka), lambda j, i: (rows.prompt_block(i), 0)),
            pl.BlockSpec((tm, kb), lambda j, i: (rows.prompt_block(i), 0)),
            pl.BlockSpec((tail, ka), lambda j, i: (0, 0)),
            pl.BlockSpec((tail, kb), lambda j, i: (0, 0)),
            pl.BlockSpec((ka, tn), lambda j, i: (0, j)),
            pl.BlockSpec((kb, tn), lambda j, i: (0, j)),
            pl.BlockSpec((tm, tn), lambda j, i: (i, g0 + j)),
            pl.BlockSpec((tm, tn), lambda j, i: (i, g1 + j)),
        ],
        out_specs=pl.BlockSpec((tm, tn), lambda j, i: (i, j)),
        out_shape=jax.ShapeDtypeStruct((rows.total, d), BF16),
        scratch_shapes=[pltpu.VMEM((ka, tn), BF16), pltpu.VMEM((kb, tn), BF16)],
        compiler_params=_params(
            2, _nbytes((tm, ka), BF16), _nbytes((tm, kb), BF16), _nbytes((tail, ka), BF16), _nbytes((tail, kb), BF16),
            _nbytes((ka, tn), F32), _nbytes((kb, tn), F32), 3 * _nbytes((tm, tn), BF16),
            scratch_bytes=_nbytes((ka, tn), BF16) + _nbytes((kb, tn), BF16) + 3 * _nbytes((tm, tn), F32)),
        name="merge",
    )(oa_p, ob_p, oa_s, ob_s, wa, wb, proj, proj)


def _proj_res_norm_kernel(*refs, rows, first):
    a_ref, w_ref = refs[0], refs[1]
    if first:
        resp_ref, ress_ref, nw_ref, x_ref, hn_ref = refs[2:]
    else:
        res_ref, nw_ref, yp_ref, ys_ref = refs[2:]
    i = pl.program_id(0)
    k = pl.program_id(1)
    d = w_ref.shape[1]
    col_chunk = min(d, 512)

    def fn(r, is_tail):
        acc_ref = x_ref if first else (ys_ref if is_tail else yp_ref)
        nrow = r.stop
        row_chunk = min(nrow, 128)
        assert nrow % row_chunk == 0

        @pl.when(k == 0)
        def _():
            if first:
                acc_ref[r, :] = ress_ref[...] if is_tail else resp_ref[...]
            else:
                acc_ref[r, :] = res_ref[r, :]

        a = a_ref[r, :]
        for c in range(d // col_chunk):
            cs = slice(c * col_chunk, (c + 1) * col_chunk)
            acc_ref[r, cs] += _dot(a, w_ref[:, cs].astype(BF16))

        @pl.when(k == pl.num_programs(1) - 1)
        def _():
            def body(c, carry):
                rr = pl.ds(pl.multiple_of(c * row_chunk, row_chunk), row_chunk)
                y = _rmsnorm_rows(acc_ref[rr, :], nw_ref[...])
                if first:
                    hn_ref[rr, :] = y.astype(hn_ref.dtype)
                else:
                    acc_ref[rr, :] = y
                return carry

            lax.fori_loop(0, nrow // row_chunk, body, 0)

    rows.each(i, fn)


def _proj_res_norm(a, w, res, norm_w, rows, tk):
    kdim = a.shape[1]
    d = w.shape[1]
    tm, tail = rows.tm, rows.tail
    tk = min(kdim, tk)
    assert kdim % tk == 0
    first = isinstance(res, tuple)
    uni_spec = pl.BlockSpec((tm, d), lambda i, k: (i, 0))
    prm_spec = pl.BlockSpec((tm, d), lambda i, k: (rows.prompt_block(i), 0))
    dec_spec = pl.BlockSpec((tail, d), lambda i, k: (0, 0))
    blocks = [_nbytes((tm, tk), BF16), _nbytes((tk, d), F32), 2 * _nbytes((tm, d), F32), 2 * _nbytes((tail, d), F32)]
    if first:
        res_args, res_specs = list(res), [prm_spec, dec_spec]
        out_specs = [uni_spec, uni_spec]
        out_shape = [jax.ShapeDtypeStruct((rows.total, d), F32), jax.ShapeDtypeStruct((rows.total, d), BF16)]
        blocks.append(_nbytes((tm, d), BF16))
    else:
        res_args, res_specs = [res], [uni_spec]
        out_specs = [prm_spec, dec_spec]
        out_shape = [jax.ShapeDtypeStruct((rows.m_p, d), F32), jax.ShapeDtypeStruct((tail, d), F32)]
    return pl.pallas_call(
        functools.partial(_proj_res_norm_kernel, rows=rows, first=first),
        grid=(rows.steps, kdim // tk),
        in_specs=[pl.BlockSpec((tm, tk), lambda i, k: (i, k)), pl.BlockSpec((tk, d), lambda i, k: (k, 0))]
        + res_specs + [pl.BlockSpec((1, d), lambda i, k: (0, 0))],
        out_specs=out_specs,
        out_shape=out_shape,
        compiler_params=_params(2, *blocks),
        name="proj_res_norm",
    )(a, w, *res_args, norm_w.reshape(1, d))


def _swiglu_kernel(h_ref, wg_ref, wu_ref, o_ref, wg_bf, wu_bf, *, rows):
    i = pl.program_id(1)

    @pl.when(i == 0)
    def _():
        wg_bf[...] = wg_ref[...].astype(BF16)
        wu_bf[...] = wu_ref[...].astype(BF16)

    def fn(r, is_tail):
        h = h_ref[r, :]
        a = _dot(h, wg_bf[...])
        b = _dot(h, wu_bf[...])
        o_ref[r, :] = (_silu(a) * b).astype(o_ref.dtype)

    rows.each(i, fn)


def _swiglu(h, wg, wu, rows):
    d = h.shape[1]
    f = wg.shape[1]
    tm = rows.tm
    tn = 512 if f % 512 == 0 else 256
    assert f % tn == 0
    return pl.pallas_call(
        functools.partial(_swiglu_kernel, rows=rows),
        grid=(f // tn, rows.steps),
        in_specs=[
            pl.BlockSpec((tm, d), lambda j, i: (i, 0)),
            pl.BlockSpec((d, tn), lambda j, i: (0, j)),
            pl.BlockSpec((d, tn), lambda j, i: (0, j)),
        ],
        out_specs=pl.BlockSpec((tm, tn), lambda j, i: (i, j)),
        out_shape=jax.ShapeDtypeStruct((rows.total, f), BF16),
        scratch_shapes=[pltpu.VMEM((d, tn), BF16), pltpu.VMEM((d, tn), BF16)],
        compiler_params=_params(
            2, _nbytes((tm, d), BF16), 2 * _nbytes((d, tn), F32), _nbytes((tm, tn), BF16),
            scratch_bytes=2 * _nbytes((d, tn), BF16) + 3 * _nbytes((tm, tn), F32)),
        name="swiglu",
    )(h, wg, wu)


def _layout(d_model, in_width, state_gla, state_ret, gate_rank):
    _, _, gh, gdk, gdv = state_gla.shape
    _, _, rh, rdk, rdv = state_ret.shape
    gqk, gv, rqk, rv = gh * gdk, gh * gdv, rh * rdk, rh * rdv
    lay = dict(gla_heads=gh, gla_dk=gdk, gla_dv=gdv, ret_heads=rh, ret_dk=rdk, ret_dv=rdv, rank=gate_rank)
    off = 0
    for name, width in (("qa", gqk), ("ka", gqk), ("va", gv), ("ga", gv), ("qb", rqk), ("kb", rqk),
                        ("vb", rv), ("gb", rv), ("mg", 2 * d_model)):
        lay[name] = off
        off += width
    lay["out_cols"] = off
    lay["plain_cols"] = 2 * gqk + gv
    lay["gd_src"] = lay["plain_cols"]
    assert lay["gd_src"] % V7X_LANES == 0 and gate_rank <= V7X_LANES
    assert in_width == off + gate_rank
    return lay


def _layer(x_p, x_s, st_gla, st_ret, wts, lay, tables, final_norm):
    (norm_mix, w_in, w_gate_up, b_gate, gla_norm_w, w_gla_up, ret_norm_w, w_ret_up, w_out, norm_ffn,
     w_ffn_gate, w_ffn_up, w_ffn_down) = wts
    batch, seq, d = x_p.shape
    rank = lay["rank"]
    gqk = lay["gla_heads"] * lay["gla_dk"]
    wup_pad = jnp.zeros((V7X_LANES, gqk), F32).at[:rank].set(w_gate_up)
    bup = b_gate.reshape(1, gqk)
    gnw = gla_norm_w.reshape(1, -1)
    rnw = ret_norm_w.reshape(1, -1)
    tn = 1024 if (lay["out_cols"] % 1024 == 0 and lay["plain_cols"] % 1024 == 0) else 512
    cos_p, sin_p, cos_s, sin_s, log_gamma = tables
    xp = x_p.reshape(batch * seq, d)
    xs = x_s.reshape(-1, d)
    m_p = batch * seq
    rows = _Rows(m_p, xs.shape[0], 1024)
    w_in_t = w_in.T

    h = _rmsnorm_bf16(xp, xs, norm_mix)
    proj = _in_proj(h, w_in_t, lay["plain_cols"], rank, lay["out_cols"], tn, rows)
    oa_p, sa_p = _gla_prompt(proj, h, w_in_t, wup_pad, bup, gnw, lay, batch, seq)
    ob_p, sb_p = _ret_prompt(proj, cos_p, sin_p, log_gamma, rnw, lay, batch, seq)
    oa_s, sa_s = _gla_decode(proj, h, w_in_t, wup_pad, bup, gnw, st_gla, lay, m_p)
    ob_s, sb_s = _ret_decode(proj, cos_s, sin_s, log_gamma, rnw, st_ret, lay, m_p)
    mrg = _merge(oa_p, ob_p, oa_s, ob_s, w_gla_up, w_ret_up, proj, lay, rows)
    x1, h2 = _proj_res_norm(mrg, w_out, (xp, xs), norm_ffn, rows, 512)
    act = _swiglu(h2, w_ffn_gate, w_ffn_up, rows)
    y_p, y_s = _proj_res_norm(act, w_ffn_down, x1, final_norm, rows, 512)
    return (y_p, sa_p, sb_p), (y_s, sa_s, sb_s)


def kernel(x_prompt, x_sample, state_gla, state_ret, norm_mix, w_in, w_gla_gate_up, b_gla_gate, gla_norm_w,
           w_gla_up, ret_norm_w, w_ret_up, w_out, norm_ffn, w_ffn_gate, w_ffn_up, w_ffn_down, norm_final):
    depth = w_in.shape[0]
    assert depth == 1, "single-layer trunk"
    batch, seq, d = x_prompt.shape
    lay = _layout(d, w_in.shape[-1], state_gla, state_ret, w_gla_gate_up.shape[1])
    rh, rdk = lay["ret_heads"], lay["ret_dk"]
    half = rdk // 2
    assert half == V7X_LANES
    cos_p, sin_p = _rope_tables(seq, 0, half)
    cos_s, sin_s = _rope_tables(8, PAST_LEN, half)
    lg = jnp.log1p(-jnp.exp(jnp.linspace(math.log(1.0 / 32), math.log(1.0 / 512), rh))).astype(F32)
    log_gamma = jnp.broadcast_to(lg[:, None, None], (rh, 1, V7X_LANES))
    tables = (cos_p, sin_p, cos_s, sin_s, log_gamma)

    wts = (norm_mix[0], w_in[0], w_gla_gate_up[0], b_gla_gate[0], gla_norm_w[0], w_gla_up[0], ret_norm_w[0],
           w_ret_up[0], w_out[0], norm_ffn[0], w_ffn_gate[0], w_ffn_up[0], w_ffn_down[0])
    (y_p, ga_p, re_p), (y_s, ga_s, re_s) = _layer(
        x_prompt, x_sample, state_gla[0], state_ret[0], wts, lay, tables, norm_final)

    sd = state_gla.dtype
    return (y_p.reshape(batch, seq, d), y_s.reshape(x_sample.shape),
            ga_p[None].astype(sd), re_p[None].astype(state_ret.dtype),
            ga_s[None].astype(sd), re_s[None].astype(state_ret.dtype))
```

```python
import functools
import math

import numpy as np
import jax
import jax.numpy as jnp
from jax import lax
from jax.experimental import pallas as pl
from jax.experimental.pallas import tpu as pltpu

EPS = 1e-6
ROPE_BASE = 10000.0
GLA_GATE_NORM = 16.0
PAST_LEN = 16384

V7X_LANES = 128
V7X_VMEM_REQUEST_CAP = 58 * 1024 * 1024
COMPILER_SCRATCH_BYTES = 12 * 1024 * 1024

GLA_CHUNK = 64
GLA_SUB = 16
RET_CHUNK = 128
DEC_TOKENS = 16
ROW_TILE = 1024

BF16 = jnp.bfloat16
F32 = jnp.float32


def _params(n_axes, *block_bytes, scratch_bytes=0):
    need = 2 * sum(block_bytes) + scratch_bytes + COMPILER_SCRATCH_BYTES
    return pltpu.CompilerParams(
        dimension_semantics=("arbitrary",) * n_axes,
        vmem_limit_bytes=int(min(V7X_VMEM_REQUEST_CAP, need)),
    )


def _nbytes(shape, dtype):
    return int(np.prod(shape)) * jnp.dtype(dtype).itemsize


def _sigmoid(x):
    return 1.0 / (1.0 + jnp.exp(-x))


def _silu(x):
    return x * _sigmoid(x)


def _log_sigmoid(x):
    return jnp.minimum(x, 0.0) - jnp.log1p(jnp.exp(-jnp.abs(x)))


def _dot(a, b):
    return jnp.dot(a, b, preferred_element_type=F32)


def _dot_nt(a, b):
    return lax.dot_general(a, b, (((1,), (1,)), ((), ())), preferred_element_type=F32)


def _dot_tn(a, b):
    return lax.dot_general(a, b, (((0,), (0,)), ((), ())), preferred_element_type=F32)


def _row_tile(m, want):
    t = min(m, want)
    assert m % t == 0, (m, t)
    return t


def _rmsnorm_rows(x, w):
    ms = jnp.mean(x * x, axis=-1, keepdims=True)
    return x * lax.rsqrt(ms + EPS) * w


def _rmsnorm_kernel(xp_ref, xs_ref, w_ref, hp_ref, hs_ref):
    hp_ref[...] = _rmsnorm_rows(xp_ref[...], w_ref[...]).astype(hp_ref.dtype)

    @pl.when(pl.program_id(0) == 0)
    def _():
        hs_ref[...] = _rmsnorm_rows(xs_ref[...], w_ref[...]).astype(hs_ref.dtype)


def _rmsnorm_bf16(x_p, x_s, w):
    m_p, d = x_p.shape
    tail = x_s.shape[0]
    tm = _row_tile(m_p, 512)
    return pl.pallas_call(
        _rmsnorm_kernel,
        grid=(m_p // tm,),
        in_specs=[
            pl.BlockSpec((tm, d), lambda i: (i, 0)),
            pl.BlockSpec((tail, d), lambda i: (0, 0)),
            pl.BlockSpec((1, d), lambda i: (0, 0)),
        ],
        out_specs=[pl.BlockSpec((tm, d), lambda i: (i, 0)), pl.BlockSpec((tail, d), lambda i: (0, 0))],
        out_shape=[jax.ShapeDtypeStruct((m_p, d), BF16), jax.ShapeDtypeStruct((tail, d), BF16)],
        compiler_params=_params(1, _nbytes((tm, d), F32), _nbytes((tail, d), F32), _nbytes((tm, d), BF16)),
        name="rmsnorm",
    )(x_p, x_s, w.reshape(1, d))


def _in_proj_kernel(hp_ref, hs_ref, wm_ref, wn_ref, op_ref, os_ref, wbf_ref, *, n_plain, shift):
    j = pl.program_id(0)
    i = pl.program_id(1)
    tn = wbf_ref.shape[0]

    @pl.when(jnp.logical_and(i == 0, j < n_plain))
    def _():
        wbf_ref[...] = wm_ref[...].astype(BF16)

    @pl.when(jnp.logical_and(i == 0, j >= n_plain))
    def _():
        wbf_ref[0:tn - shift, :] = wm_ref[shift:tn, :].astype(BF16)
        wbf_ref[tn - shift:tn, :] = wn_ref[...].astype(BF16)

    @pl.when(i == 0)
    def _():
        os_ref[...] = _dot_nt(hs_ref[...], wbf_ref[...]).astype(os_ref.dtype)

    op_ref[...] = _dot_nt(hp_ref[...], wbf_ref[...]).astype(op_ref.dtype)


def _in_proj(h_p, h_s, w_in_t, plain_cols, shift, out_cols, tn):
    m_p, d = h_p.shape
    tail = h_s.shape[0]
    tm = _row_tile(m_p, ROW_TILE)
    assert plain_cols % tn == 0 and out_cols % tn == 0 and tn % shift == 0 and shift % 8 == 0
    n_plain = plain_cols // tn
    kern = functools.partial(_in_proj_kernel, n_plain=n_plain, shift=shift)
    return pl.pallas_call(
        kern,
        grid=(out_cols // tn, m_p // tm),
        in_specs=[
            pl.BlockSpec((tm, d), lambda j, i: (i, 0)),
            pl.BlockSpec((tail, d), lambda j, i: (0, 0)),
            pl.BlockSpec((tn, d), lambda j, i: (j, 0)),
            pl.BlockSpec((shift, d), lambda j, i: ((j + 1) * (tn // shift), 0)),
        ],
        out_specs=[pl.BlockSpec((tm, tn), lambda j, i: (i, j)), pl.BlockSpec((tail, tn), lambda j, i: (0, j))],
        out_shape=[jax.ShapeDtypeStruct((m_p, out_cols), BF16), jax.ShapeDtypeStruct((tail, out_cols), BF16)],
        scratch_shapes=[pltpu.VMEM((tn, d), BF16)],
        compiler_params=_params(
            2, _nbytes((tm, d), BF16), _nbytes((tail, d), BF16), _nbytes((tn, d), F32), _nbytes((shift, d), F32),
            _nbytes((tm, tn), BF16), _nbytes((tail, tn), BF16),
            scratch_bytes=_nbytes((tn, d), BF16) + _nbytes((tm, tn), F32)),
        name="in_proj",
    )(h_p, h_s, w_in_t, w_in_t)


def _gate_logits(h_ref, wgd_ref, wup_ref, bup_ref):
    gd = _dot_nt(h_ref[...], wgd_ref[...].astype(BF16))
    return _dot(gd.astype(BF16), wup_ref[...].astype(BF16)) + bup_ref[...]


def _prefix_sum_rows(tri_bf16, g):
    g0 = g.astype(BF16)
    r1 = g - g0.astype(F32)
    g1 = r1.astype(BF16)
    g2 = (r1 - g1.astype(F32)).astype(BF16)
    return _dot(tri_bf16, g0) + _dot(tri_bf16, g1) + _dot(tri_bf16, g2)


def _lane_bcast_cols(row, n):
    parts = []
    for c in range(n // V7X_LANES):
        tile = jnp.broadcast_to(row[:, c * V7X_LANES:(c + 1) * V7X_LANES], (V7X_LANES, V7X_LANES))
        parts.append(tile.T)
    return parts[0] if len(parts) == 1 else jnp.concatenate(parts, axis=0)


def _rms_gate_store(o, w, gate, out_ref, rows, cols):
    ms = jnp.mean(o * o, axis=-1, keepdims=True)
    y = o * lax.rsqrt(ms + EPS) * w
    out_ref[rows, cols] = (y * _silu(gate)).astype(out_ref.dtype)


def _ln_gate_store(o, w, gate, out_ref, rows, cols):
    mu = jnp.mean(o, axis=-1, keepdims=True)
    dlt = o - mu
    var = jnp.mean(dlt * dlt, axis=-1, keepdims=True)
    y = dlt * lax.rsqrt(var + EPS) * w
    out_ref[rows, cols] = (y * _silu(gate)).astype(out_ref.dtype)


def _gla_chunk_head(q, k, v, g_cum, s_ref, hh):
    c, dk = q.shape
    nsub = c // GLA_SUB
    row = lax.broadcasted_iota(jnp.int32, (c, dk), 0)
    g_last = g_cum[c - 1:c, :]
    state = s_ref[hh]

    o = _dot((q * jnp.exp(g_cum)).astype(BF16), state.astype(BF16))

    g_ref_rows = jnp.concatenate(
        [jnp.broadcast_to(g_cum[GLA_SUB * b:GLA_SUB * b + 1, :], (GLA_SUB, dk)) for b in range(nsub)], axis=0)
    q_rel = (q * jnp.exp(g_cum - g_ref_rows)).astype(BF16)
    blocks = [jnp.zeros((GLA_SUB, c), F32)]
    for b in range(1, nsub):
        g_ref = g_cum[GLA_SUB * b:GLA_SUB * b + 1, :]
        k_rel = k * jnp.exp(jnp.where(row < GLA_SUB * b, g_ref - g_cum, -jnp.inf))
        blocks.append(_dot_nt(q_rel[GLA_SUB * b:GLA_SUB * (b + 1), :], k_rel.astype(BF16)))
    scores = jnp.concatenate(blocks, axis=0)

    srow = lax.broadcasted_iota(jnp.int32, (c, c), 0)
    scol = lax.broadcasted_iota(jnp.int32, (c, c), 1)
    rel = scol - jnp.bitwise_and(srow, -GLA_SUB)
    row_in_sub = jnp.bitwise_and(row, GLA_SUB - 1)
    for r in range(GLA_SUB):
        k_r = jnp.concatenate(
            [jnp.broadcast_to(k[GLA_SUB * b + r:GLA_SUB * b + r + 1, :], (GLA_SUB, dk)) for b in range(nsub)], axis=0)
        g_r = jnp.concatenate(
            [jnp.broadcast_to(g_cum[GLA_SUB * b + r:GLA_SUB * b + r + 1, :], (GLA_SUB, dk)) for b in range(nsub)],
            axis=0)
        e = jnp.exp(jnp.where(row_in_sub >= r, g_cum - g_r, -jnp.inf))
        col = jnp.sum(q * e * k_r, axis=-1, keepdims=True)
        scores = scores + jnp.where(rel == r, col, 0.0)

    o = o + _dot(scores.astype(BF16), v)

    k_tail = (k * jnp.exp(g_last - g_cum)).astype(BF16)
    decay = _lane_bcast_cols(jnp.exp(g_last), dk)
    dv = v.shape[1]
    decay_full = jnp.concatenate([decay] * (dv // V7X_LANES), axis=1)
    s_ref[hh] = decay_full * state + _dot_tn(k_tail, v)
    return o


def _gla_prompt_kernel(q_ref, k_ref, v_ref, ga_ref, h_ref, wgd_ref, wup_ref, bup_ref, nw_ref,
                       o_ref, s_out_ref, s_ref, *, heads, dk, dv):
    t = pl.program_id(1)

    @pl.when(t == 0)
    def _():
        s_ref[...] = jnp.zeros_like(s_ref)

    ct = q_ref.shape[0]
    glog = _log_sigmoid(_gate_logits(h_ref, wgd_ref, wup_ref, bup_ref)) * (1.0 / GLA_GATE_NORM)
    ti = lax.broadcasted_iota(jnp.int32, (GLA_CHUNK, GLA_CHUNK), 0)
    tj = lax.broadcasted_iota(jnp.int32, (GLA_CHUNK, GLA_CHUNK), 1)
    tri = (ti >= tj).astype(BF16)
    nw = nw_ref[...]
    for c in range(ct // GLA_CHUNK):
        rows = slice(c * GLA_CHUNK, (c + 1) * GLA_CHUNK)
        g_cum_all = _prefix_sum_rows(tri, glog[rows, :])
        for hh in range(heads):
            kc = slice(hh * dk, (hh + 1) * dk)
            vc = slice(hh * dv, (hh + 1) * dv)
            q = q_ref[rows, kc].astype(F32) * (dk ** -0.5)
            k = k_ref[rows, kc].astype(F32)
            o = _gla_chunk_head(q, k, v_ref[rows, vc], g_cum_all[:, kc], s_ref, hh)
            _rms_gate_store(o, nw, ga_ref[rows, vc].astype(F32), o_ref, rows, vc)

    @pl.when(t == pl.num_programs(1) - 1)
    def _():
        s_out_ref[0] = s_ref[...]


def _gla_prompt(proj, h, w_in_t, wup_pad, bup, norm_w, lay, batch, seq):
    heads, dk, dv = lay["gla_heads"], lay["gla_dk"], lay["gla_dv"]
    qk, vw = heads * dk, heads * dv
    d = h.shape[1]
    ct = min(seq, 2 * GLA_CHUNK)
    assert seq % ct == 0 and ct % GLA_CHUNK == 0
    nt = seq // ct
    row = lambda b, t: b * nt + t
    kern = functools.partial(_gla_prompt_kernel, heads=heads, dk=dk, dv=dv)
    return pl.pallas_call(
        kern,
        grid=(batch, nt),
        in_specs=[
            pl.BlockSpec((ct, qk), lambda b, t: (row(b, t), lay["qa"] // qk)),
            pl.BlockSpec((ct, qk), lambda b, t: (row(b, t), lay["ka"] // qk)),
            pl.BlockSpec((ct, vw), lambda b, t: (row(b, t), lay["va"] // vw)),
            pl.BlockSpec((ct, vw), lambda b, t: (row(b, t), lay["ga"] // vw)),
            pl.BlockSpec((ct, d), lambda b, t: (row(b, t), 0)),
            pl.BlockSpec((V7X_LANES, d), lambda b, t: (lay["gd_src"] // V7X_LANES, 0)),
            pl.BlockSpec((V7X_LANES, qk), lambda b, t: (0, 0)),
            pl.BlockSpec((1, qk), lambda b, t: (0, 0)),
            pl.BlockSpec((1, dv), lambda b, t: (0, 0)),
        ],
        out_specs=[
            pl.BlockSpec((ct, vw), lambda b, t: (row(b, t), 0)),
            pl.BlockSpec((1, heads, dk, dv), lambda b, t: (b, 0, 0, 0)),
        ],
        out_shape=[
            jax.ShapeDtypeStruct((batch * seq, vw), BF16),
            jax.ShapeDtypeStruct((batch, heads, dk, dv), F32),
        ],
        scratch_shapes=[pltpu.VMEM((heads, dk, dv), F32)],
        compiler_params=_params(
            2, 2 * _nbytes((ct, qk), BF16), 3 * _nbytes((ct, vw), BF16), _nbytes((ct, d), BF16),
            _nbytes((V7X_LANES, d), F32), _nbytes((V7X_LANES, qk), F32), _nbytes((heads, dk, dv), F32),
            scratch_bytes=_nbytes((heads, dk, dv), F32)),
        name="gla_prompt",
    )(proj, proj, proj, proj, h, w_in_t, wup_pad, bup, norm_w)


def _rotary(x, cos, sin):
    half = x.shape[1] // 2
    x1, x2 = x[:, :half], x[:, half:]
    return jnp.concatenate([x1 * cos - x2 * sin, x1 * sin + x2 * cos], axis=1)


def _ret_prompt_kernel(q_ref, k_ref, v_ref, gb_ref, cos_ref, sin_ref, lg_ref, nw_ref,
                       o_ref, s_out_ref, s_ref, *, heads, dk, dv):
    t = pl.program_id(1)

    @pl.when(t == 0)
    def _():
        s_ref[...] = jnp.zeros_like(s_ref)

    c = q_ref.shape[0]
    cos, sin = cos_ref[...], sin_ref[...]
    nw = nw_ref[...]
    ri = lax.broadcasted_iota(jnp.int32, (c, c), 0)
    rj = lax.broadcasted_iota(jnp.int32, (c, c), 1)
    dist = (ri - rj).astype(F32)
    rowk = lax.broadcasted_iota(jnp.int32, (c, dk), 0).astype(F32)
    rows = slice(0, c)
    for hh in range(heads):
        lg = lg_ref[hh]
        kc = slice(hh * dk, (hh + 1) * dk)
        vc = slice(hh * dv, (hh + 1) * dv)
        qr = _rotary(q_ref[:, kc].astype(F32), cos, sin)
        kr = _rotary(k_ref[:, kc].astype(F32), cos, sin) * (dk ** -0.5)
        v = v_ref[:, vc]
        lgk = jnp.concatenate([lg] * (dk // V7X_LANES), axis=1)
        lgc = lg[:, :1]
        decay = jnp.exp(jnp.where(ri >= rj, dist * lgc, -jnp.inf))
        scores = _dot_nt(qr.astype(BF16), kr.astype(BF16)) * decay
        state = s_ref[hh]
        o = _dot((qr * jnp.exp((rowk + 1.0) * lgk)).astype(BF16), state.astype(BF16))
        o = o + _dot(scores.astype(BF16), v)
        k_tail = (kr * jnp.exp((float(c - 1) - rowk) * lgk)).astype(BF16)
        lgv = jnp.concatenate([lg] * (dv // V7X_LANES), axis=1)
        s_ref[hh] = jnp.exp(float(c) * lgv) * state + _dot_tn(k_tail, v)
        _ln_gate_store(o, nw, gb_ref[:, vc].astype(F32), o_ref, rows, vc)

    @pl.when(t == pl.num_programs(1) - 1)
    def _():
        s_out_ref[0] = s_ref[...]


def _ret_prompt(proj, cos, sin, log_gamma, norm_w, lay, batch, seq):
    heads, dk, dv = lay["ret_heads"], lay["ret_dk"], lay["ret_dv"]
    qk, vw = heads * dk, heads * dv
    c = min(seq, RET_CHUNK)
    assert seq % c == 0
    nt = seq // c
    half = dk // 2
    row = lambda b, t: b * nt + t
    kern = functools.partial(_ret_prompt_kernel, heads=heads, dk=dk, dv=dv)
    return pl.pallas_call(
        kern,
        grid=(batch, nt),
        in_specs=[
            pl.BlockSpec((c, qk), lambda b, t: (row(b, t), lay["qb"] // qk)),
            pl.BlockSpec((c, qk), lambda b, t: (row(b, t), lay["kb"] // qk)),
            pl.BlockSpec((c, vw), lambda b, t: (row(b, t), lay["vb"] // vw)),
            pl.BlockSpec((c, vw), lambda b, t: (row(b, t), lay["gb"] // vw)),
            pl.BlockSpec((c, half), lambda b, t: (t, 0)),
            pl.BlockSpec((c, half), lambda b, t: (t, 0)),
            pl.BlockSpec((heads, 1, V7X_LANES), lambda b, t: (0, 0, 0)),
            pl.BlockSpec((1, dv), lambda b, t: (0, 0)),
        ],
        out_specs=[
            pl.BlockSpec((c, vw), lambda b, t: (row(b, t), 0)),
            pl.BlockSpec((1, heads, dk, dv), lambda b, t: (b, 0, 0, 0)),
        ],
        out_shape=[
            jax.ShapeDtypeStruct((batch * seq, vw), BF16),
            jax.ShapeDtypeStruct((batch, heads, dk, dv), F32),
        ],
        scratch_shapes=[pltpu.VMEM((heads, dk, dv), F32)],
        compiler_params=_params(
            2, 2 * _nbytes((c, qk), BF16), 3 * _nbytes((c, vw), BF16), 2 * _nbytes((c, half), F32),
            _nbytes((heads, dk, dv), F32), scratch_bytes=_nbytes((heads, dk, dv), F32)),
        name="ret_prompt",
    )(proj, proj, proj, proj, cos, sin, log_gamma, norm_w)


def _decode_tokens(a_cols_fn, k, q, v, s_in_ref, s_out_ref, o_scr):
    tokens, dk = k.shape
    dv = v.shape[1]
    reps = dv // V7X_LANES
    for tt in range(tokens):
        k_cols = jnp.concatenate([_lane_bcast_cols(k[tt:tt + 1, :], dk)] * reps, axis=1)
        q_cols = jnp.concatenate([_lane_bcast_cols(q[tt:tt + 1, :], dk)] * reps, axis=1)
        s_new = a_cols_fn(tt) * s_in_ref[tt, 0] + k_cols * v[tt:tt + 1, :]
        s_out_ref[tt, 0] = s_new
        o_scr[tt:tt + 1, :] = jnp.sum(q_cols * s_new, axis=0, keepdims=True)


def _gla_decode_kernel(q_ref, k_ref, v_ref, ga_ref, h_ref, wgd_ref, wup_ref, bup_ref, nw_ref, s_in_ref,
                       o_ref, s_out_ref, o_scr, *, dk, dv):
    glog = _log_sigmoid(_gate_logits(h_ref, wgd_ref, wup_ref, bup_ref)) * (1.0 / GLA_GATE_NORM)
    a = jnp.exp(glog)
    q = q_ref[...].astype(F32) * (dk ** -0.5)
    k = k_ref[...].astype(F32)
    v = v_ref[...].astype(F32)
    reps = dv // V7X_LANES
    a_cols = lambda tt: jnp.concatenate([_lane_bcast_cols(a[tt:tt + 1, :], dk)] * reps, axis=1)
    _decode_tokens(a_cols, k, q, v, s_in_ref, s_out_ref, o_scr)
    tokens = q.shape[0]
    _rms_gate_store(o_scr[...], nw_ref[...], ga_ref[...].astype(F32), o_ref, slice(0, tokens), slice(0, dv))


def _gla_decode(proj, h, w_in_t, wup_pad, bup, norm_w, state, lay):
    heads, dk, dv = lay["gla_heads"], lay["gla_dk"], lay["gla_dv"]
    n, d = h.shape
    tk = DEC_TOKENS
    assert n % tk == 0
    kern = functools.partial(_gla_decode_kernel, dk=dk, dv=dv)
    return pl.pallas_call(
        kern,
        grid=(n // tk, heads),
        in_specs=[
            pl.BlockSpec((tk, dk), lambda i, hh: (i, lay["qa"] // dk + hh)),
            pl.BlockSpec((tk, dk), lambda i, hh: (i, lay["ka"] // dk + hh)),
            pl.BlockSpec((tk, dv), lambda i, hh: (i, lay["va"] // dv + hh)),
            pl.BlockSpec((tk, dv), lambda i, hh: (i, lay["ga"] // dv + hh)),
            pl.BlockSpec((tk, d), lambda i, hh: (i, 0)),
            pl.BlockSpec((V7X_LANES, d), lambda i, hh: (lay["gd_src"] // V7X_LANES, 0)),
            pl.BlockSpec((V7X_LANES, dk), lambda i, hh: (0, hh)),
            pl.BlockSpec((1, dk), lambda i, hh: (0, hh)),
            pl.BlockSpec((1, dv), lambda i, hh: (0, 0)),
            pl.BlockSpec((tk, 1, dk, dv), lambda i, hh: (i, hh, 0, 0)),
        ],
        out_specs=[
            pl.BlockSpec((tk, dv), lambda i, hh: (i, hh)),
            pl.BlockSpec((tk, 1, dk, dv), lambda i, hh: (i, hh, 0, 0)),
        ],
        out_shape=[
            jax.ShapeDtypeStruct((n, heads * dv), BF16),
            jax.ShapeDtypeStruct(state.shape, state.dtype),
        ],
        scratch_shapes=[pltpu.VMEM((tk, dv), F32)],
        compiler_params=_params(
            2, 2 * _nbytes((tk, dk, dv), F32), _nbytes((tk, d), BF16), _nbytes((V7X_LANES, d), F32),
            scratch_bytes=_nbytes((tk, dv), F32)),
        name="gla_decode",
    )(proj, proj, proj, proj, h, w_in_t, wup_pad, bup, norm_w, state)


def _ret_decode_kernel(q_ref, k_ref, v_ref, gb_ref, cos_ref, sin_ref, lg_ref, nw_ref, s_in_ref,
                       o_ref, s_out_ref, o_scr, *, dk, dv):
    cos, sin = cos_ref[0:1, :], sin_ref[0:1, :]
    q = _rotary(q_ref[...].astype(F32), cos, sin)
    k = _rotary(k_ref[...].astype(F32), cos, sin) * (dk ** -0.5)
    v = v_ref[...].astype(F32)
    gamma = jnp.exp(jnp.concatenate([lg_ref[0]] * (dv // V7X_LANES), axis=1))
    _decode_tokens(lambda tt: gamma, k, q, v, s_in_ref, s_out_ref, o_scr)
    tokens = q.shape[0]
    _ln_gate_store(o_scr[...], nw_ref[...], gb_ref[...].astype(F32), o_ref, slice(0, tokens), slice(0, dv))


def _ret_decode(proj, cos, sin, log_gamma, norm_w, state, lay):
    heads, dk, dv = lay["ret_heads"], lay["ret_dk"], lay["ret_dv"]
    n = proj.shape[0]
    tk = DEC_TOKENS
    assert n % tk == 0
    half = dk // 2
    kern = functools.partial(_ret_decode_kernel, dk=dk, dv=dv)
    return pl.pallas_call(
        kern,
        grid=(n // tk, heads),
        in_specs=[
            pl.BlockSpec((tk, dk), lambda i, hh: (i, lay["qb"] // dk + hh)),
            pl.BlockSpec((tk, dk), lambda i, hh: (i, lay["kb"] // dk + hh)),
            pl.BlockSpec((tk, dv), lambda i, hh: (i, lay["vb"] // dv + hh)),
            pl.BlockSpec((tk, dv), lambda i, hh: (i, lay["gb"] // dv + hh)),
            pl.BlockSpec((8, half), lambda i, hh: (0, 0)),
            pl.BlockSpec((8, half), lambda i, hh: (0, 0)),
            pl.BlockSpec((1, 1, V7X_LANES), lambda i, hh: (hh, 0, 0)),
            pl.BlockSpec((1, dv), lambda i, hh: (0, 0)),
            pl.BlockSpec((tk, 1, dk, dv), lambda i, hh: (i, hh, 0, 0)),
        ],
        out_specs=[
            pl.BlockSpec((tk, dv), lambda i, hh: (i, hh)),
            pl.BlockSpec((tk, 1, dk, dv), lambda i, hh: (i, hh, 0, 0)),
        ],
        out_shape=[
            jax.ShapeDtypeStruct((n, heads * dv), BF16),
            jax.ShapeDtypeStruct(state.shape, state.dtype),
        ],
        scratch_shapes=[pltpu.VMEM((tk, dv), F32)],
        compiler_params=_params(2, 2 * _nbytes((tk, dk, dv), F32), scratch_bytes=_nbytes((tk, dv), F32)),
        name="ret_decode",
    )(proj, proj, proj, proj, cos, sin, log_gamma, norm_w, state)


def _rope_table_kernel(cos_ref, sin_ref, *, pos0, half):
    rows, lanes = cos_ref.shape
    pos = (lax.broadcasted_iota(jnp.int32, (rows, lanes), 0) + (pl.program_id(0) * rows + pos0)).astype(F32)
    idx = lax.broadcasted_iota(jnp.int32, (rows, lanes), 1).astype(F32)
    inv = jnp.exp(idx * (-math.log(ROPE_BASE) / half))
    ang = pos * inv
    cos_ref[...] = jnp.cos(ang)
    sin_ref[...] = jnp.sin(ang)


def _rope_tables(n_pos, pos0, half):
    rows = min(n_pos, 256)
    assert n_pos % rows == 0
    kern = functools.partial(_rope_table_kernel, pos0=pos0, half=half)
    spec = pl.BlockSpec((rows, half), lambda i: (i, 0))
    return pl.pallas_call(
        kern,
        grid=(n_pos // rows,),
        in_specs=[],
        out_specs=[spec, spec],
        out_shape=[jax.ShapeDtypeStruct((n_pos, half), F32)] * 2,
        compiler_params=_params(1, 2 * _nbytes((rows, half), F32)),
        name="rope_tables",
    )()


def _merge_kernel(oap_ref, obp_ref, oas_ref, obs_ref, wa_ref, wb_ref, g0p_ref, g1p_ref, g0s_ref, g1s_ref,
                  mp_ref, ms_ref, wa_bf, wb_bf):
    def merged(oa, ob, g0, g1):
        ya = _dot(oa, wa_bf[...])
        yb = _dot(ob, wb_bf[...])
        return _sigmoid(g0.astype(F32)) * ya + _sigmoid(g1.astype(F32)) * yb

    @pl.when(pl.program_id(1) == 0)
    def _():
        wa_bf[...] = wa_ref[...].astype(BF16)
        wb_bf[...] = wb_ref[...].astype(BF16)
        ms_ref[...] = merged(oas_ref[...], obs_ref[...], g0s_ref[...], g1s_ref[...]).astype(ms_ref.dtype)

    mp_ref[...] = merged(oap_ref[...], obp_ref[...], g0p_ref[...], g1p_ref[...]).astype(mp_ref.dtype)


def _merge(oa_p, ob_p, oa_s, ob_s, wa, wb, proj_p, proj_s, lay):
    m_p, ka = oa_p.shape
    kb = ob_p.shape[1]
    tail = oa_s.shape[0]
    d = wa.shape[1]
    tm = _row_tile(m_p, ROW_TILE)
    tn = min(d, 512)
    assert d % tn == 0 and lay["mg"] % tn == 0
    g0 = lay["mg"] // tn
    g1 = (lay["mg"] + d) // tn
    return pl.pallas_call(
        _merge_kernel,
        grid=(d // tn, m_p // tm),
        in_specs=[
            pl.BlockSpec((tm, ka), lambda j, i: (i, 0)),
            pl.BlockSpec((tm, kb), lambda j, i: (i, 0)),
            pl.BlockSpec((tail, ka), lambda j, i: (0, 0)),
            pl.BlockSpec((tail, kb), lambda j, i: (0, 0)),
            pl.BlockSpec((ka, tn), lambda j, i: (0, j)),
            pl.BlockSpec((kb, tn), lambda j, i: (0, j)),
            pl.BlockSpec((tm, tn), lambda j, i: (i, g0 + j)),
            pl.BlockSpec((tm, tn), lambda j, i: (i, g1 + j)),
            pl.BlockSpec((tail, tn), lambda j, i: (0, g0 + j)),
            pl.BlockSpec((tail, tn), lambda j, i: (0, g1 + j)),
        ],
        out_specs=[pl.BlockSpec((tm, tn), lambda j, i: (i, j)), pl.BlockSpec((tail, tn), lambda j, i: (0, j))],
        out_shape=[jax.ShapeDtypeStruct((m_p, d), BF16), jax.ShapeDtypeStruct((tail, d), BF16)],
        scratch_shapes=[pltpu.VMEM((ka, tn), BF16), pltpu.VMEM((kb, tn), BF16)],
        compiler_params=_params(
            2, _nbytes((tm, ka), BF16), _nbytes((tm, kb), BF16), _nbytes((tail, ka), BF16), _nbytes((tail, kb), BF16),
            _nbytes((ka, tn), F32), _nbytes((kb, tn), F32), 3 * _nbytes((tm, tn), BF16), 3 * _nbytes((tail, tn), BF16),
            scratch_bytes=_nbytes((ka, tn), BF16) + _nbytes((kb, tn), BF16) + 3 * _nbytes((tm, tn), F32)),
        name="merge",
    )(oa_p, ob_p, oa_s, ob_s, wa, wb, proj_p, proj_p, proj_s, proj_s)


def _proj_res_norm_kernel(ap_ref, as_ref, w_ref, resp_ref, ress_ref, nw_ref, *out_refs, emit_sum):
    n_out = 2 if emit_sum else 1
    outs_p, outs_s = out_refs[:n_out], out_refs[n_out:]
    i = pl.program_id(0)
    k = pl.program_id(1)
    last_k = k == pl.num_programs(1) - 1
    d = w_ref.shape[1]
    col_chunk = min(d, 512)

    def step(a_ref, res_ref, outs):
        acc_ref = outs[0]
        nrow = acc_ref.shape[0]
        row_chunk = min(nrow, 128)
        assert nrow % row_chunk == 0

        @pl.when(k == 0)
        def _():
            acc_ref[...] = res_ref[...]

        a = a_ref[...]
        for c in range(d // col_chunk):
            cs = slice(c * col_chunk, (c + 1) * col_chunk)
            acc_ref[:, cs] += _dot(a, w_ref[:, cs].astype(BF16))

        @pl.when(last_k)
        def _():
            def body(c, carry):
                rr = pl.ds(pl.multiple_of(c * row_chunk, row_chunk), row_chunk)
                y = _rmsnorm_rows(acc_ref[rr, :], nw_ref[...])
                if emit_sum:
                    outs[1][rr, :] = y.astype(outs[1].dtype)
                else:
                    acc_ref[rr, :] = y
                return carry

            lax.fori_loop(0, nrow // row_chunk, body, 0)

    step(ap_ref, resp_ref, outs_p)

    @pl.when(i == pl.num_programs(0) - 1)
    def _():
        step(as_ref, ress_ref, outs_s)


def _proj_res_norm(a_p, a_s, w, res_p, res_s, norm_w, emit_sum, tk):
    m_p, kdim = a_p.shape
    tail = a_s.shape[0]
    d = w.shape[1]
    tm = _row_tile(m_p, ROW_TILE)
    tk = min(kdim, tk)
    assert kdim % tk == 0
    p_spec = pl.BlockSpec((tm, d), lambda i, k: (i, 0))
    s_spec = pl.BlockSpec((tail, d), lambda i, k: (0, 0))
    out_specs = [p_spec, s_spec]
    out_shape = [jax.ShapeDtypeStruct((m_p, d), F32), jax.ShapeDtypeStruct((tail, d), F32)]
    blocks = [_nbytes((tm, tk), BF16), _nbytes((tail, tk), BF16), _nbytes((tk, d), F32),
              2 * _nbytes((tm, d), F32), 2 * _nbytes((tail, d), F32)]
    if emit_sum:
        out_specs = [p_spec, p_spec, s_spec, s_spec]
        out_shape = [out_shape[0], jax.ShapeDtypeStruct((m_p, d), BF16),
                     out_shape[1], jax.ShapeDtypeStruct((tail, d), BF16)]
        blocks += [_nbytes((tm, d), BF16), _nbytes((tail, d), BF16)]
    return pl.pallas_call(
        functools.partial(_proj_res_norm_kernel, emit_sum=emit_sum),
        grid=(m_p // tm, kdim // tk),
        in_specs=[
            pl.BlockSpec((tm, tk), lambda i, k: (i, k)),
            pl.BlockSpec((tail, tk), lambda i, k: (0, k)),
            pl.BlockSpec((tk, d), lambda i, k: (k, 0)),
            p_spec,
            s_spec,
            pl.BlockSpec((1, d), lambda i, k: (0, 0)),
        ],
        out_specs=out_specs,
        out_shape=out_shape,
        compiler_params=_params(2, *blocks),
        name="proj_res_norm",
    )(a_p, a_s, w, res_p, res_s, norm_w.reshape(1, d))


def _swiglu_kernel(hp_ref, hs_ref, wg_ref, wu_ref, op_ref, os_ref, wg_bf, wu_bf):
    def act(h):
        a = _dot(h, wg_bf[...])
        b = _dot(h, wu_bf[...])
        return _silu(a) * b

    @pl.when(pl.program_id(1) == 0)
    def _():
        wg_bf[...] = wg_ref[...].astype(BF16)
        wu_bf[...] = wu_ref[...].astype(BF16)
        os_ref[...] = act(hs_ref[...]).astype(os_ref.dtype)

    op_ref[...] = act(hp_ref[...]).astype(op_ref.dtype)


def _swiglu(h_p, h_s, wg, wu):
    m_p, d = h_p.shape
    tail = h_s.shape[0]
    f = wg.shape[1]
    tm = _row_tile(m_p, ROW_TILE)
    tn = 512 if f % 512 == 0 else 256
    assert f % tn == 0
    return pl.pallas_call(
        _swiglu_kernel,
        grid=(f // tn, m_p // tm),
        in_specs=[
            pl.BlockSpec((tm, d), lambda j, i: (i, 0)),
            pl.BlockSpec((tail, d), lambda j, i: (0, 0)),
            pl.BlockSpec((d, tn), lambda j, i: (0, j)),
            pl.BlockSpec((d, tn), lambda j, i: (0, j)),
        ],
        out_specs=[pl.BlockSpec((tm, tn), lambda j, i: (i, j)), pl.BlockSpec((tail, tn), lambda j, i: (0, j))],
        out_shape=[jax.ShapeDtypeStruct((m_p, f), BF16), jax.ShapeDtypeStruct((tail, f), BF16)],
        scratch_shapes=[pltpu.VMEM((d, tn), BF16), pltpu.VMEM((d, tn), BF16)],
        compiler_params=_params(
            2, _nbytes((tm, d), BF16), _nbytes((tail, d), BF16), 2 * _nbytes((d, tn), F32),
            _nbytes((tm, tn), BF16), _nbytes((tail, tn), BF16),
            scratch_bytes=2 * _nbytes((d, tn), BF16) + 3 * _nbytes((tm, tn), F32)),
        name="swiglu",
    )(h_p, h_s, wg, wu)


def _layout(d_model, in_width, state_gla, state_ret, gate_rank):
    _, _, gh, gdk, gdv = state_gla.shape
    _, _, rh, rdk, rdv = state_ret.shape
    gqk, gv, rqk, rv = gh * gdk, gh * gdv, rh * rdk, rh * rdv
    lay = dict(gla_heads=gh, gla_dk=gdk, gla_dv=gdv, ret_heads=rh, ret_dk=rdk, ret_dv=rdv, rank=gate_rank)
    off = 0
    for name, width in (("qa", gqk), ("ka", gqk), ("va", gv), ("ga", gv), ("qb", rqk), ("kb", rqk),
                        ("vb", rv), ("gb", rv), ("mg", 2 * d_model)):
        lay[name] = off
        off += width
    lay["out_cols"] = off
    lay["plain_cols"] = 2 * gqk + gv
    lay["gd_src"] = lay["plain_cols"]
    assert lay["gd_src"] % V7X_LANES == 0 and gate_rank <= V7X_LANES
    assert in_width == off + gate_rank
    return lay


def _layer(x_p, x_s, st_gla, st_ret, wts, lay, tables, final_norm):
    (norm_mix, w_in, w_gate_up, b_gate, gla_norm_w, w_gla_up, ret_norm_w, w_ret_up, w_out, norm_ffn,
     w_ffn_gate, w_ffn_up, w_ffn_down) = wts
    batch, seq, d = x_p.shape
    rank = lay["rank"]
    gqk = lay["gla_heads"] * lay["gla_dk"]
    wup_pad = jnp.zeros((V7X_LANES, gqk), F32).at[:rank].set(w_gate_up)
    bup = b_gate.reshape(1, gqk)
    gnw = gla_norm_w.reshape(1, -1)
    rnw = ret_norm_w.reshape(1, -1)
    tn = 1024 if (lay["out_cols"] % 1024 == 0 and lay["plain_cols"] % 1024 == 0) else 512
    cos_p, sin_p, cos_s, sin_s, log_gamma = tables
    xp = x_p.reshape(batch * seq, d)
    xs = x_s.reshape(-1, d)
    w_in_t = w_in.T

    h_p, h_s = _rmsnorm_bf16(xp, xs, norm_mix)
    proj_p, proj_s = _in_proj(h_p, h_s, w_in_t, lay["plain_cols"], rank, lay["out_cols"], tn)
    oa_p, sa_p = _gla_prompt(proj_p, h_p, w_in_t, wup_pad, bup, gnw, lay, batch, seq)
    ob_p, sb_p = _ret_prompt(proj_p, cos_p, sin_p, log_gamma, rnw, lay, batch, seq)
    oa_s, sa_s = _gla_decode(proj_s, h_s, w_in_t, wup_pad, bup, gnw, st_gla, lay)
    ob_s, sb_s = _ret_decode(proj_s, cos_s, sin_s, log_gamma, rnw, st_ret, lay)
    m_p, m_s = _merge(oa_p, ob_p, oa_s, ob_s, w_gla_up, w_ret_up, proj_p, proj_s, lay)
    x1_p, h2_p, x1_s, h2_s = _proj_res_norm(m_p, m_s, w_out, xp, xs, norm_ffn, True, 512)
    act_p, act_s = _swiglu(h2_p, h2_s, w_ffn_gate, w_ffn_up)
    y_p, y_s = _proj_res_norm(act_p, act_s, w_ffn_down, x1_p, x1_s, final_norm, False, 512)
    return (y_p, sa_p, sb_p), (y_s, sa_s, sb_s)


def kernel(x_prompt, x_sample, state_gla, state_ret, norm_mix, w_in, w_gla_gate_up, b_gla_gate, gla_norm_w,
           w_gla_up, ret_norm_w, w_ret_up, w_out, norm_ffn, w_ffn_gate, w_ffn_up, w_ffn_down, norm_final):
    depth = w_in.shape[0]
    assert depth == 1, "single-layer trunk"
    batch, seq, d = x_prompt.shape
    lay = _layout(d, w_in.shape[-1], state_gla, state_ret, w_gla_gate_up.shape[1])
    rh, rdk = lay["ret_heads"], lay["ret_dk"]
    half = rdk // 2
    assert half == V7X_LANES
    cos_p, sin_p = _rope_tables(seq, 0, half)
    cos_s, sin_s = _rope_tables(8, PAST_LEN, half)
    lg = jnp.log1p(-jnp.exp(jnp.linspace(math.log(1.0 / 32), math.log(1.0 / 512), rh))).astype(F32)
    log_gamma = jnp.broadcast_to(lg[:, None, None], (rh, 1, V7X_LANES))
    tables = (cos_p, sin_p, cos_s, sin_s, log_gamma)

    wts = (norm_mix[0], w_in[0], w_gla_gate_up[0], b_gla_gate[0], gla_norm_w[0], w_gla_up[0], ret_norm_w[0],
           w_ret_up[0], w_out[0], norm_ffn[0], w_ffn_gate[0], w_ffn_up[0], w_ffn_down[0])
    (y_p, ga_p, re_p), (y_s, ga_s, re_s) = _layer(
        x_prompt, x_sample, state_gla[0], state_ret[0], wts, lay, tables, norm_final)

    sd = state_gla.dtype
    return (y_p.reshape(batch, seq, d), y_s.reshape(x_sample.shape),
            ga_p[None].astype(sd), re_p[None].astype(state_ret.dtype),
            ga_s[None].astype(sd), re_s[None].astype(state_ret.dtype))
```

```python
import functools
import math

import numpy as np
import jax
import jax.numpy as jnp
from jax import lax
from jax.experimental import pallas as pl
from jax.experimental.pallas import tpu as pltpu

EPS = 1e-6
ROPE_BASE = 10000.0
GLA_GATE_NORM = 16.0
PAST_LEN = 16384

V7X_LANES = 128
V7X_VMEM_REQUEST_CAP = 58 * 1024 * 1024
COMPILER_SCRATCH_BYTES = 12 * 1024 * 1024

GLA_CHUNK = 64
GLA_STEP_CHUNKS = 4
LOG2_E = 1.4426950408889634
RET_CHUNK = 128
DEC_TOKENS = 16
ROW_TILE = 1024

BF16 = jnp.bfloat16
F32 = jnp.float32


def _params(n_axes, *block_bytes, scratch_bytes=0):
    need = 2 * sum(block_bytes) + scratch_bytes + COMPILER_SCRATCH_BYTES
    return pltpu.CompilerParams(
        dimension_semantics=("arbitrary",) * n_axes,
        vmem_limit_bytes=int(min(V7X_VMEM_REQUEST_CAP, need)),
    )


def _nbytes(shape, dtype):
    return int(np.prod(shape)) * jnp.dtype(dtype).itemsize


def _sigmoid(x):
    return 1.0 / (1.0 + jnp.exp(-x))


def _silu(x):
    return x * _sigmoid(x)


def _log_sigmoid(x):
    return jnp.minimum(x, 0.0) - jnp.log1p(jnp.exp(-jnp.abs(x)))


def _dot(a, b):
    return jnp.dot(a, b, preferred_element_type=F32)


def _dot_nt(a, b):
    return lax.dot_general(a, b, (((1,), (1,)), ((), ())), preferred_element_type=F32)


def _dot_tn(a, b):
    return lax.dot_general(a, b, (((0,), (0,)), ((), ())), preferred_element_type=F32)


def _row_tile(m, want):
    t = min(m, want)
    assert m % t == 0, (m, t)
    return t


def _rmsnorm_rows(x, w):
    ms = jnp.mean(x * x, axis=-1, keepdims=True)
    return x * lax.rsqrt(ms + EPS) * w


def _rmsnorm_kernel(xp_ref, xs_ref, w_ref, hp_ref, hs_ref):
    hp_ref[...] = _rmsnorm_rows(xp_ref[...], w_ref[...]).astype(hp_ref.dtype)

    @pl.when(pl.program_id(0) == 0)
    def _():
        hs_ref[...] = _rmsnorm_rows(xs_ref[...], w_ref[...]).astype(hs_ref.dtype)


def _rmsnorm_bf16(x_p, x_s, w):
    m_p, d = x_p.shape
    tail = x_s.shape[0]
    tm = _row_tile(m_p, 512)
    return pl.pallas_call(
        _rmsnorm_kernel,
        grid=(m_p // tm,),
        in_specs=[
            pl.BlockSpec((tm, d), lambda i: (i, 0)),
            pl.BlockSpec((tail, d), lambda i: (0, 0)),
            pl.BlockSpec((1, d), lambda i: (0, 0)),
        ],
        out_specs=[pl.BlockSpec((tm, d), lambda i: (i, 0)), pl.BlockSpec((tail, d), lambda i: (0, 0))],
        out_shape=[jax.ShapeDtypeStruct((m_p, d), BF16), jax.ShapeDtypeStruct((tail, d), BF16)],
        compiler_params=_params(1, _nbytes((tm, d), F32), _nbytes((tail, d), F32), _nbytes((tm, d), BF16)),
        name="rmsnorm",
    )(x_p, x_s, w.reshape(1, d))


def _in_proj_kernel(hp_ref, hs_ref, wm_ref, wn_ref, wgd_ref, wup_ref, bup_ref, op_ref, os_ref, gp_ref, gs_ref,
                    wbf_ref, *, n_plain, shift):
    j = pl.program_id(0)
    i = pl.program_id(1)
    tn = wbf_ref.shape[0]

    def log2_decay(h):
        gd = _dot_nt(h, wgd_ref[...].astype(BF16))
        x = _dot(gd.astype(BF16), wup_ref[...].astype(BF16)) + bup_ref[...]
        return _log_sigmoid(x) * (LOG2_E / GLA_GATE_NORM)

    @pl.when(j == pl.num_programs(0) - 1)
    def _():
        gp_ref[...] = log2_decay(hp_ref[...])

        @pl.when(i == 0)
        def _():
            gs_ref[...] = log2_decay(hs_ref[...])

    @pl.when(jnp.logical_and(i == 0, j < n_plain))
    def _():
        wbf_ref[...] = wm_ref[...].astype(BF16)

    @pl.when(jnp.logical_and(i == 0, j >= n_plain))
    def _():
        wbf_ref[0:tn - shift, :] = wm_ref[shift:tn, :].astype(BF16)
        wbf_ref[tn - shift:tn, :] = wn_ref[...].astype(BF16)

    @pl.when(i == 0)
    def _():
        os_ref[...] = _dot_nt(hs_ref[...], wbf_ref[...]).astype(os_ref.dtype)

    op_ref[...] = _dot_nt(hp_ref[...], wbf_ref[...]).astype(op_ref.dtype)


def _in_proj(h_p, h_s, w_in_t, wup_pad, bup, plain_cols, shift, out_cols, tn):
    m_p, d = h_p.shape
    tail = h_s.shape[0]
    gw = wup_pad.shape[1]
    tm = _row_tile(m_p, ROW_TILE)
    assert plain_cols % tn == 0 and out_cols % tn == 0 and tn % shift == 0 and shift % 8 == 0
    assert plain_cols % V7X_LANES == 0
    n_plain = plain_cols // tn
    n_tiles = out_cols // tn
    kern = functools.partial(_in_proj_kernel, n_plain=n_plain, shift=shift)
    return pl.pallas_call(
        kern,
        grid=(n_tiles, m_p // tm),
        in_specs=[
            pl.BlockSpec((tm, d), lambda j, i: (i, 0)),
            pl.BlockSpec((tail, d), lambda j, i: (0, 0)),
            pl.BlockSpec((tn, d), lambda j, i: (j, 0)),
            pl.BlockSpec((shift, d), lambda j, i: ((j + 1) * (tn // shift), 0)),
            pl.BlockSpec((V7X_LANES, d), lambda j, i: (plain_cols // V7X_LANES, 0)),
            pl.BlockSpec((V7X_LANES, gw), lambda j, i: (0, 0)),
            pl.BlockSpec((1, gw), lambda j, i: (0, 0)),
        ],
        out_specs=[
            pl.BlockSpec((tm, tn), lambda j, i: (i, j)),
            pl.BlockSpec((tail, tn), lambda j, i: (0, j)),
            pl.BlockSpec((tm, gw), lambda j, i: (jnp.where(j == n_tiles - 1, i, 0), 0)),
            pl.BlockSpec((tail, gw), lambda j, i: (0, 0)),
        ],
        out_shape=[
            jax.ShapeDtypeStruct((m_p, out_cols), BF16), jax.ShapeDtypeStruct((tail, out_cols), BF16),
            jax.ShapeDtypeStruct((m_p, gw), F32), jax.ShapeDtypeStruct((tail, gw), F32),
        ],
        scratch_shapes=[pltpu.VMEM((tn, d), BF16)],
        compiler_params=_params(
            2, _nbytes((tm, d), BF16), _nbytes((tail, d), BF16), _nbytes((tn, d), F32), _nbytes((shift, d), F32),
            _nbytes((V7X_LANES, d), F32), _nbytes((V7X_LANES, gw), F32),
            _nbytes((tm, tn), BF16), _nbytes((tail, tn), BF16), _nbytes((tm, gw), F32), _nbytes((tail, gw), F32),
            scratch_bytes=_nbytes((tn, d), BF16) + _nbytes((tm, tn), F32)),
        name="in_proj",
    )(h_p, h_s, w_in_t, w_in_t, w_in_t, wup_pad, bup)


def _prefix_sum_rows(sel3_bf16, g):
    g0 = g.astype(BF16)
    r1 = g - g0.astype(F32)
    g1 = r1.astype(BF16)
    g2 = (r1 - g1.astype(F32)).astype(BF16)
    return _dot(sel3_bf16, jnp.concatenate([g0, g1, g2], axis=0))


def _lane_bcast_cols(row, n):
    parts = []
    for c in range(n // V7X_LANES):
        tile = jnp.broadcast_to(row[:, c * V7X_LANES:(c + 1) * V7X_LANES], (V7X_LANES, V7X_LANES))
        parts.append(tile.T)
    return parts[0] if len(parts) == 1 else jnp.concatenate(parts, axis=0)


def _rms_gate_store(o, w, gate, out_ref, rows, cols):
    ms = jnp.mean(o * o, axis=-1, keepdims=True)
    y = o * lax.rsqrt(ms + EPS) * w
    out_ref[rows, cols] = (y * _silu(gate)).astype(out_ref.dtype)


def _ln_gate_store(o, w, gate, out_ref, rows, cols):
    mu = jnp.mean(o, axis=-1, keepdims=True)
    dlt = o - mu
    var = jnp.mean(dlt * dlt, axis=-1, keepdims=True)
    y = dlt * lax.rsqrt(var + EPS) * w
    out_ref[rows, cols] = (y * _silu(gate)).astype(out_ref.dtype)


def _gla_sum_matrices(c):
    levels = c.bit_length() - 1
    assert 1 << levels == c
    i = lax.broadcasted_iota(jnp.int32, (c, c), 0)
    j = lax.broadcasted_iota(jnp.int32, (c, c), 1)
    mats = [j <= i]
    for l in range(levels):
        ref = jnp.bitwise_or(jnp.bitwise_and(i, -(2 << l)), 1 << l)
        mats.append(jnp.logical_and(j > jnp.minimum(i, ref), j <= jnp.maximum(i, ref)))
    mats.append(j > i)
    sel = jnp.concatenate([jnp.where(m, 1.0, 0.0).astype(BF16) for m in mats], axis=0)
    return jnp.concatenate([sel, sel, sel], axis=1)


def _pair_level(c):
    levels = c.bit_length() - 1
    i = lax.broadcasted_iota(jnp.int32, (c, c), 0)
    j = lax.broadcasted_iota(jnp.int32, (c, c), 1)
    x = jnp.bitwise_xor(i, j)
    lvl = jnp.zeros((c, c), jnp.int32)
    for l in range(1, levels):
        lvl = lvl + jnp.where(x >= (1 << l), 1, 0)
    return jnp.where(i > j, lvl, jnp.where(i == j, levels, -1))


def _queries_else_keys(q, k, l):
    c = q.shape[0]
    span = 1 << l
    if span >= 8:
        parts = [(q if (b & 1) else k)[b * span:(b + 1) * span, :] for b in range(c // span)]
        return jnp.concatenate(parts, axis=0)
    row = lax.broadcasted_iota(jnp.int32, q.shape, 0)
    return jnp.where(jnp.bitwise_and(row, span) != 0, q, k)


def _gla_level_scores(q, k, sums):
    c = q.shape[0]
    levels = c.bit_length() - 1
    out = []
    for l in range(levels):
        x = _queries_else_keys(q, k, l) * jnp.exp2(sums[(1 + l) * c:(2 + l) * c, :])
        xb = x.astype(BF16)
        out.append(_dot_nt(xb, xb))
    return out


def _gla_chunk_out(q, k, v, sums, level_scores, pair_level, state):
    c = q.shape[0]
    levels = c.bit_length() - 1
    scores = jnp.where(pair_level == levels, jnp.sum(q * k, axis=-1, keepdims=True), 0.0)
    for l in range(levels):
        scores = jnp.where(pair_level == l, level_scores[l], scores)
    o = _dot((q * jnp.exp2(sums[0:c, :])).astype(BF16), state.astype(BF16))
    return o + _dot(scores.astype(BF16), v)


def _gla_next_state(k, v, sums, state):
    c, dk = k.shape
    levels = c.bit_length() - 1
    k_tail = (k * jnp.exp2(sums[(levels + 1) * c:(levels + 2) * c, :])).astype(BF16)
    decay = _lane_bcast_cols(jnp.exp2(sums[c - 1:c, :]), dk)
    decay_full = jnp.concatenate([decay] * (v.shape[1] // V7X_LANES), axis=1)
    return decay_full * state + _dot_tn(k_tail, v)


def _gla_prompt_kernel(q_ref, k_ref, v_ref, ga_ref, g_ref, nw_ref,
                       o_ref, s_out_ref, s_ref, mats_ref, lvl_ref, *, heads, dk, dv):
    t = pl.program_id(1)

    @pl.when(t == 0)
    def _():
        s_ref[...] = jnp.zeros_like(s_ref)
        mats_ref[...] = _gla_sum_matrices(GLA_CHUNK)
        lvl_ref[...] = _pair_level(GLA_CHUNK)

    ct = q_ref.shape[0]

    def chunk(c, carry):
        rows = pl.ds(pl.multiple_of(c * GLA_CHUNK, GLA_CHUNK), GLA_CHUNK)
        sums_all = _prefix_sum_rows(mats_ref[...], g_ref[rows, :])
        pair_level = lvl_ref[...]
        kcs = [slice(hh * dk, (hh + 1) * dk) for hh in range(heads)]
        vcs = [slice(hh * dv, (hh + 1) * dv) for hh in range(heads)]
        qs = [q_ref[rows, kc].astype(F32) * (dk ** -0.5) for kc in kcs]
        ks = [k_ref[rows, kc].astype(F32) for kc in kcs]
        lvl_scores = [_gla_level_scores(qs[hh], ks[hh], sums_all[:, kcs[hh]]) for hh in range(heads)]
        outs = [_gla_chunk_out(qs[hh], ks[hh], v_ref[rows, vcs[hh]], sums_all[:, kcs[hh]], lvl_scores[hh],
                               pair_level, s_ref[hh]) for hh in range(heads)]
        for hh in range(heads):
            s_ref[hh] = _gla_next_state(ks[hh], v_ref[rows, vcs[hh]], sums_all[:, kcs[hh]], s_ref[hh])
        for hh in range(heads):
            _rms_gate_store(outs[hh], nw_ref[...], ga_ref[rows, vcs[hh]].astype(F32), o_ref, rows, vcs[hh])
        return carry

    lax.fori_loop(0, ct // GLA_CHUNK, chunk, 0)

    @pl.when(t == pl.num_programs(1) - 1)
    def _():
        s_out_ref[0] = s_ref[...]


def _gla_prompt(proj, log2_decay, norm_w, lay, batch, seq):
    heads, dk, dv = lay["gla_heads"], lay["gla_dk"], lay["gla_dv"]
    qk, vw = heads * dk, heads * dv
    ct = min(seq, GLA_STEP_CHUNKS * GLA_CHUNK)
    levels = GLA_CHUNK.bit_length() - 1
    assert seq % ct == 0 and ct % GLA_CHUNK == 0
    nt = seq // ct
    row = lambda b, t: b * nt + t
    kern = functools.partial(_gla_prompt_kernel, heads=heads, dk=dk, dv=dv)
    return pl.pallas_call(
        kern,
        grid=(batch, nt),
        in_specs=[
            pl.BlockSpec((ct, qk), lambda b, t: (row(b, t), lay["qa"] // qk)),
            pl.BlockSpec((ct, qk), lambda b, t: (row(b, t), lay["ka"] // qk)),
            pl.BlockSpec((ct, vw), lambda b, t: (row(b, t), lay["va"] // vw)),
            pl.BlockSpec((ct, vw), lambda b, t: (row(b, t), lay["ga"] // vw)),
            pl.BlockSpec((ct, qk), lambda b, t: (row(b, t), 0)),
            pl.BlockSpec((1, dv), lambda b, t: (0, 0)),
        ],
        out_specs=[
            pl.BlockSpec((ct, vw), lambda b, t: (row(b, t), 0)),
            pl.BlockSpec((1, heads, dk, dv), lambda b, t: (b, 0, 0, 0)),
        ],
        out_shape=[
            jax.ShapeDtypeStruct((batch * seq, vw), BF16),
            jax.ShapeDtypeStruct((batch, heads, dk, dv), F32),
        ],
        scratch_shapes=[
            pltpu.VMEM((heads, dk, dv), F32),
            pltpu.VMEM(((levels + 2) * GLA_CHUNK, 3 * GLA_CHUNK), BF16),
            pltpu.VMEM((GLA_CHUNK, GLA_CHUNK), jnp.int32),
        ],
        compiler_params=_params(
            2, 2 * _nbytes((ct, qk), BF16), 3 * _nbytes((ct, vw), BF16), _nbytes((ct, qk), F32),
            _nbytes((heads, dk, dv), F32), scratch_bytes=_nbytes((heads, dk, dv), F32)),
        name="gla_prompt",
    )(proj, proj, proj, proj, log2_decay, norm_w)


def _rotary(x, cos, sin):
    half = x.shape[1] // 2
    x1, x2 = x[:, :half], x[:, half:]
    return jnp.concatenate([x1 * cos - x2 * sin, x1 * sin + x2 * cos], axis=1)


def _ret_prompt_kernel(q_ref, k_ref, v_ref, gb_ref, cos_ref, sin_ref, lg_ref, nw_ref,
                       o_ref, s_out_ref, s_ref, *, heads, dk, dv):
    t = pl.program_id(1)

    @pl.when(t == 0)
    def _():
        s_ref[...] = jnp.zeros_like(s_ref)

    c = q_ref.shape[0]
    cos, sin = cos_ref[...], sin_ref[...]
    nw = nw_ref[...]
    ri = lax.broadcasted_iota(jnp.int32, (c, c), 0)
    rj = lax.broadcasted_iota(jnp.int32, (c, c), 1)
    dist = (ri - rj).astype(F32)
    rowk = lax.broadcasted_iota(jnp.int32, (c, dk), 0).astype(F32)
    rows = slice(0, c)
    for hh in range(heads):
        lg = lg_ref[hh]
        kc = slice(hh * dk, (hh + 1) * dk)
        vc = slice(hh * dv, (hh + 1) * dv)
        qr = _rotary(q_ref[:, kc].astype(F32), cos, sin)
        kr = _rotary(k_ref[:, kc].astype(F32), cos, sin) * (dk ** -0.5)
        v = v_ref[:, vc]
        lgk = jnp.concatenate([lg] * (dk // V7X_LANES), axis=1)
        lgc = lg[:, :1]
        decay = jnp.exp(jnp.where(ri >= rj, dist * lgc, -jnp.inf))
        scores = _dot_nt(qr.astype(BF16), kr.astype(BF16)) * decay
        state = s_ref[hh]
        o = _dot((qr * jnp.exp((rowk + 1.0) * lgk)).astype(BF16), state.astype(BF16))
        o = o + _dot(scores.astype(BF16), v)
        k_tail = (kr * jnp.exp((float(c - 1) - rowk) * lgk)).astype(BF16)
        lgv = jnp.concatenate([lg] * (dv // V7X_LANES), axis=1)
        s_ref[hh] = jnp.exp(float(c) * lgv) * state + _dot_tn(k_tail, v)
        _ln_gate_store(o, nw, gb_ref[:, vc].astype(F32), o_ref, rows, vc)

    @pl.when(t == pl.num_programs(1) - 1)
    def _():
        s_out_ref[0] = s_ref[...]


def _ret_prompt(proj, cos, sin, log_gamma, norm_w, lay, batch, seq):
    heads, dk, dv = lay["ret_heads"], lay["ret_dk"], lay["ret_dv"]
    qk, vw = heads * dk, heads * dv
    c = min(seq, RET_CHUNK)
    assert seq % c == 0
    nt = seq // c
    half = dk // 2
    row = lambda b, t: b * nt + t
    kern = functools.partial(_ret_prompt_kernel, heads=heads, dk=dk, dv=dv)
    return pl.pallas_call(
        kern,
        grid=(batch, nt),
        in_specs=[
            pl.BlockSpec((c, qk), lambda b, t: (row(b, t), lay["qb"] // qk)),
            pl.BlockSpec((c, qk), lambda b, t: (row(b, t), lay["kb"] // qk)),
            pl.BlockSpec((c, vw), lambda b, t: (row(b, t), lay["vb"] // vw)),
            pl.BlockSpec((c, vw), lambda b, t: (row(b, t), lay["gb"] // vw)),
            pl.BlockSpec((c, half), lambda b, t: (t, 0)),
            pl.BlockSpec((c, half), lambda b, t: (t, 0)),
            pl.BlockSpec((heads, 1, V7X_LANES), lambda b, t: (0, 0, 0)),
            pl.BlockSpec((1, dv), lambda b, t: (0, 0)),
        ],
        out_specs=[
            pl.BlockSpec((c, vw), lambda b, t: (row(b, t), 0)),
            pl.BlockSpec((1, heads, dk, dv), lambda b, t: (b, 0, 0, 0)),
        ],
        out_shape=[
            jax.ShapeDtypeStruct((batch * seq, vw), BF16),
            jax.ShapeDtypeStruct((batch, heads, dk, dv), F32),
        ],
        scratch_shapes=[pltpu.VMEM((heads, dk, dv), F32)],
        compiler_params=_params(
            2, 2 * _nbytes((c, qk), BF16), 3 * _nbytes((c, vw), BF16), 2 * _nbytes((c, half), F32),
            _nbytes((heads, dk, dv), F32), scratch_bytes=_nbytes((heads, dk, dv), F32)),
        name="ret_prompt",
    )(proj, proj, proj, proj, cos, sin, log_gamma, norm_w)


def _decode_tokens(a_cols_fn, k, q, v, s_in_ref, s_out_ref, o_scr):
    tokens, dk = k.shape
    dv = v.shape[1]
    reps = dv // V7X_LANES
    for tt in range(tokens):
        k_cols = jnp.concatenate([_lane_bcast_cols(k[tt:tt + 1, :], dk)] * reps, axis=1)
        q_cols = jnp.concatenate([_lane_bcast_cols(q[tt:tt + 1, :], dk)] * reps, axis=1)
        s_new = a_cols_fn(tt) * s_in_ref[tt, 0] + k_cols * v[tt:tt + 1, :]
        s_out_ref[tt, 0] = s_new
        o_scr[tt:tt + 1, :] = jnp.sum(q_cols * s_new, axis=0, keepdims=True)


def _gla_decode_kernel(q_ref, k_ref, v_ref, ga_ref, g_ref, nw_ref, s_in_ref,
                       o_ref, s_out_ref, o_scr, *, dk, dv):
    a = jnp.exp2(g_ref[...])
    q = q_ref[...].astype(F32) * (dk ** -0.5)
    k = k_ref[...].astype(F32)
    v = v_ref[...].astype(F32)
    reps = dv // V7X_LANES
    a_cols = lambda tt: jnp.concatenate([_lane_bcast_cols(a[tt:tt + 1, :], dk)] * reps, axis=1)
    _decode_tokens(a_cols, k, q, v, s_in_ref, s_out_ref, o_scr)
    tokens = q.shape[0]
    _rms_gate_store(o_scr[...], nw_ref[...], ga_ref[...].astype(F32), o_ref, slice(0, tokens), slice(0, dv))


def _gla_decode(proj, log2_decay, norm_w, state, lay):
    heads, dk, dv = lay["gla_heads"], lay["gla_dk"], lay["gla_dv"]
    n = proj.shape[0]
    tk = DEC_TOKENS
    assert n % tk == 0
    kern = functools.partial(_gla_decode_kernel, dk=dk, dv=dv)
    return pl.pallas_call(
        kern,
        grid=(n // tk, heads),
        in_specs=[
            pl.BlockSpec((tk, dk), lambda i, hh: (i, lay["qa"] // dk + hh)),
            pl.BlockSpec((tk, dk), lambda i, hh: (i, lay["ka"] // dk + hh)),
            pl.BlockSpec((tk, dv), lambda i, hh: (i, lay["va"] // dv + hh)),
            pl.BlockSpec((tk, dv), lambda i, hh: (i, lay["ga"] // dv + hh)),
            pl.BlockSpec((tk, dk), lambda i, hh: (i, hh)),
            pl.BlockSpec((1, dv), lambda i, hh: (0, 0)),
            pl.BlockSpec((tk, 1, dk, dv), lambda i, hh: (i, hh, 0, 0)),
        ],
        out_specs=[
            pl.BlockSpec((tk, dv), lambda i, hh: (i, hh)),
            pl.BlockSpec((tk, 1, dk, dv), lambda i, hh: (i, hh, 0, 0)),
        ],
        out_shape=[
            jax.ShapeDtypeStruct((n, heads * dv), BF16),
            jax.ShapeDtypeStruct(state.shape, state.dtype),
        ],
        scratch_shapes=[pltpu.VMEM((tk, dv), F32)],
        compiler_params=_params(2, 2 * _nbytes((tk, dk, dv), F32), scratch_bytes=_nbytes((tk, dv), F32)),
        name="gla_decode",
    )(proj, proj, proj, proj, log2_decay, norm_w, state)


def _ret_decode_kernel(q_ref, k_ref, v_ref, gb_ref, cos_ref, sin_ref, lg_ref, nw_ref, s_in_ref,
                       o_ref, s_out_ref, o_scr, *, dk, dv):
    cos, sin = cos_ref[0:1, :], sin_ref[0:1, :]
    q = _rotary(q_ref[...].astype(F32), cos, sin)
    k = _rotary(k_ref[...].astype(F32), cos, sin) * (dk ** -0.5)
    v = v_ref[...].astype(F32)
    gamma = jnp.exp(jnp.concatenate([lg_ref[0]] * (dv // V7X_LANES), axis=1))
    _decode_tokens(lambda tt: gamma, k, q, v, s_in_ref, s_out_ref, o_scr)
    tokens = q.shape[0]
    _ln_gate_store(o_scr[...], nw_ref[...], gb_ref[...].astype(F32), o_ref, slice(0, tokens), slice(0, dv))


def _ret_decode(proj, cos, sin, log_gamma, norm_w, state, lay):
    heads, dk, dv = lay["ret_heads"], lay["ret_dk"], lay["ret_dv"]
    n = proj.shape[0]
    tk = DEC_TOKENS
    assert n % tk == 0
    half = dk // 2
    kern = functools.partial(_ret_decode_kernel, dk=dk, dv=dv)
    return pl.pallas_call(
        kern,
        grid=(n // tk, heads),
        in_specs=[
            pl.BlockSpec((tk, dk), lambda i, hh: (i, lay["qb"] // dk + hh)),
            pl.BlockSpec((tk, dk), lambda i, hh: (i, lay["kb"] // dk + hh)),
            pl.BlockSpec((tk, dv), lambda i, hh: (i, lay["vb"] // dv + hh)),
            pl.BlockSpec((tk, dv), lambda i, hh: (i, lay["gb"] // dv + hh)),
            pl.BlockSpec((8, half), lambda i, hh: (0, 0)),
            pl.BlockSpec((8, half), lambda i, hh: (0, 0)),
            pl.BlockSpec((1, 1, V7X_LANES), lambda i, hh: (hh, 0, 0)),
            pl.BlockSpec((1, dv), lambda i, hh: (0, 0)),
            pl.BlockSpec((tk, 1, dk, dv), lambda i, hh: (i, hh, 0, 0)),
        ],
        out_specs=[
            pl.BlockSpec((tk, dv), lambda i, hh: (i, hh)),
            pl.BlockSpec((tk, 1, dk, dv), lambda i, hh: (i, hh, 0, 0)),
        ],
        out_shape=[
            jax.ShapeDtypeStruct((n, heads * dv), BF16),
            jax.ShapeDtypeStruct(state.shape, state.dtype),
        ],
        scratch_shapes=[pltpu.VMEM((tk, dv), F32)],
        compiler_params=_params(2, 2 * _nbytes((tk, dk, dv), F32), scratch_bytes=_nbytes((tk, dv), F32)),
        name="ret_decode",
    )(proj, proj, proj, proj, cos, sin, log_gamma, norm_w, state)


def _rope_table_kernel(cos_ref, sin_ref, *, pos0, half):
    rows, lanes = cos_ref.shape
    pos = (lax.broadcasted_iota(jnp.int32, (rows, lanes), 0) + (pl.program_id(0) * rows + pos0)).astype(F32)
    idx = lax.broadcasted_iota(jnp.int32, (rows, lanes), 1).astype(F32)
    inv = jnp.exp(idx * (-math.log(ROPE_BASE) / half))
    ang = pos * inv
    cos_ref[...] = jnp.cos(ang)
    sin_ref[...] = jnp.sin(ang)


def _rope_tables(n_pos, pos0, half):
    rows = min(n_pos, 256)
    assert n_pos % rows == 0
    kern = functools.partial(_rope_table_kernel, pos0=pos0, half=half)
    spec = pl.BlockSpec((rows, half), lambda i: (i, 0))
    return pl.pallas_call(
        kern,
        grid=(n_pos // rows,),
        in_specs=[],
        out_specs=[spec, spec],
        out_shape=[jax.ShapeDtypeStruct((n_pos, half), F32)] * 2,
        compiler_params=_params(1, 2 * _nbytes((rows, half), F32)),
        name="rope_tables",
    )()


def _merge_kernel(oap_ref, obp_ref, oas_ref, obs_ref, wa_ref, wb_ref, g0p_ref, g1p_ref, g0s_ref, g1s_ref,
                  mp_ref, ms_ref, wa_bf, wb_bf):
    def merged(oa, ob, g0, g1):
        ya = _dot(oa, wa_bf[...])
        yb = _dot(ob, wb_bf[...])
        return _sigmoid(g0.astype(F32)) * ya + _sigmoid(g1.astype(F32)) * yb

    @pl.when(pl.program_id(1) == 0)
    def _():
        wa_bf[...] = wa_ref[...].astype(BF16)
        wb_bf[...] = wb_ref[...].astype(BF16)
        ms_ref[...] = merged(oas_ref[...], obs_ref[...], g0s_ref[...], g1s_ref[...]).astype(ms_ref.dtype)

    mp_ref[...] = merged(oap_ref[...], obp_ref[...], g0p_ref[...], g1p_ref[...]).astype(mp_ref.dtype)


def _merge(oa_p, ob_p, oa_s, ob_s, wa, wb, proj_p, proj_s, lay):
    m_p, ka = oa_p.shape
    kb = ob_p.shape[1]
    tail = oa_s.shape[0]
    d = wa.shape[1]
    tm = _row_tile(m_p, ROW_TILE)
    tn = min(d, 512)
    assert d % tn == 0 and lay["mg"] % tn == 0
    g0 = lay["mg"] // tn
    g1 = (lay["mg"] + d) // tn
    return pl.pallas_call(
        _merge_kernel,
        grid=(d // tn, m_p // tm),
        in_specs=[
            pl.BlockSpec((tm, ka), lambda j, i: (i, 0)),
            pl.BlockSpec((tm, kb), lambda j, i: (i, 0)),
            pl.BlockSpec((tail, ka), lambda j, i: (0, 0)),
            pl.BlockSpec((tail, kb), lambda j, i: (0, 0)),
            pl.BlockSpec((ka, tn), lambda j, i: (0, j)),
            pl.BlockSpec((kb, tn), lambda j, i: (0, j)),
            pl.BlockSpec((tm, tn), lambda j, i: (i, g0 + j)),
            pl.BlockSpec((tm, tn), lambda j, i: (i, g1 + j)),
            pl.BlockSpec((tail, tn), lambda j, i: (0, g0 + j)),
            pl.BlockSpec((tail, tn), lambda j, i: (0, g1 + j)),
        ],
        out_specs=[pl.BlockSpec((tm, tn), lambda j, i: (i, j)), pl.BlockSpec((tail, tn), lambda j, i: (0, j))],
        out_shape=[jax.ShapeDtypeStruct((m_p, d), BF16), jax.ShapeDtypeStruct((tail, d), BF16)],
        scratch_shapes=[pltpu.VMEM((ka, tn), BF16), pltpu.VMEM((kb, tn), BF16)],
        compiler_params=_params(
            2, _nbytes((tm, ka), BF16), _nbytes((tm, kb), BF16), _nbytes((tail, ka), BF16), _nbytes((tail, kb), BF16),
            _nbytes((ka, tn), F32), _nbytes((kb, tn), F32), 3 * _nbytes((tm, tn), BF16), 3 * _nbytes((tail, tn), BF16),
            scratch_bytes=_nbytes((ka, tn), BF16) + _nbytes((kb, tn), BF16) + 3 * _nbytes((tm, tn), F32)),
        name="merge",
    )(oa_p, ob_p, oa_s, ob_s, wa, wb, proj_p, proj_p, proj_s, proj_s)


def _proj_res_norm_kernel(ap_ref, as_ref, w_ref, resp_ref, ress_ref, nw_ref, *out_refs, emit_sum):
    n_out = 2 if emit_sum else 1
    outs_p, outs_s = out_refs[:n_out], out_refs[n_out:]
    i = pl.program_id(0)
    k = pl.program_id(1)
    last_k = k == pl.num_programs(1) - 1
    d = w_ref.shape[1]
    col_chunk = min(d, 512)

    def step(a_ref, res_ref, outs):
        acc_ref = outs[0]
        nrow = acc_ref.shape[0]
        row_chunk = min(nrow, 128)
        assert nrow % row_chunk == 0

        @pl.when(k == 0)
        def _():
            acc_ref[...] = res_ref[...]

        a = a_ref[...]
        for c in range(d // col_chunk):
            cs = slice(c * col_chunk, (c + 1) * col_chunk)
            acc_ref[:, cs] += _dot(a, w_ref[:, cs].astype(BF16))

        @pl.when(last_k)
        def _():
            def body(c, carry):
                rr = pl.ds(pl.multiple_of(c * row_chunk, row_chunk), row_chunk)
                y = _rmsnorm_rows(acc_ref[rr, :], nw_ref[...])
                if emit_sum:
                    outs[1][rr, :] = y.astype(outs[1].dtype)
                else:
                    acc_ref[rr, :] = y
                return carry

            lax.fori_loop(0, nrow // row_chunk, body, 0)

    step(ap_ref, resp_ref, outs_p)

    @pl.when(i == pl.num_programs(0) - 1)
    def _():
        step(as_ref, ress_ref, outs_s)


def _proj_res_norm(a_p, a_s, w, res_p, res_s, norm_w, emit_sum, tk):
    m_p, kdim = a_p.shape
    tail = a_s.shape[0]
    d = w.shape[1]
    tm = _row_tile(m_p, ROW_TILE)
    tk = min(kdim, tk)
    assert kdim % tk == 0
    p_spec = pl.BlockSpec((tm, d), lambda i, k: (i, 0))
    s_spec = pl.BlockSpec((tail, d), lambda i, k: (0, 0))
    out_specs = [p_spec, s_spec]
    out_shape = [jax.ShapeDtypeStruct((m_p, d), F32), jax.ShapeDtypeStruct((tail, d), F32)]
    blocks = [_nbytes((tm, tk), BF16), _nbytes((tail, tk), BF16), _nbytes((tk, d), F32),
              2 * _nbytes((tm, d), F32), 2 * _nbytes((tail, d), F32)]
    if emit_sum:
        out_specs = [p_spec, p_spec, s_spec, s_spec]
        out_shape = [out_shape[0], jax.ShapeDtypeStruct((m_p, d), BF16),
                     out_shape[1], jax.ShapeDtypeStruct((tail, d), BF16)]
        blocks += [_nbytes((tm, d), BF16), _nbytes((tail, d), BF16)]
    return pl.pallas_call(
        functools.partial(_proj_res_norm_kernel, emit_sum=emit_sum),
        grid=(m_p // tm, kdim // tk),
        in_specs=[
            pl.BlockSpec((tm, tk), lambda i, k: (i, k)),
            pl.BlockSpec((tail, tk), lambda i, k: (0, k)),
            pl.BlockSpec((tk, d), lambda i, k: (k, 0)),
            p_spec,
            s_spec,
            pl.BlockSpec((1, d), lambda i, k: (0, 0)),
        ],
        out_specs=out_specs,
        out_shape=out_shape,
        compiler_params=_params(2, *blocks),
        name="proj_res_norm",
    )(a_p, a_s, w, res_p, res_s, norm_w.reshape(1, d))


def _swiglu_kernel(hp_ref, hs_ref, wg_ref, wu_ref, op_ref, os_ref, wg_bf, wu_bf):
    def act(h):
        a = _dot(h, wg_bf[...])
        b = _dot(h, wu_bf[...])
        return _silu(a) * b

    @pl.when(pl.program_id(1) == 0)
    def _():
        wg_bf[...] = wg_ref[...].astype(BF16)
        wu_bf[...] = wu_ref[...].astype(BF16)
        os_ref[...] = act(hs_ref[...]).astype(os_ref.dtype)

    op_ref[...] = act(hp_ref[...]).astype(op_ref.dtype)


def _swiglu(h_p, h_s, wg, wu):
    m_p, d = h_p.shape
    tail = h_s.shape[0]
    f = wg.shape[1]
    tm = _row_tile(m_p, ROW_TILE)
    tn = 512 if f % 512 == 0 else 256
    assert f % tn == 0
    return pl.pallas_call(
        _swiglu_kernel,
        grid=(f // tn, m_p // tm),
        in_specs=[
            pl.BlockSpec((tm, d), lambda j, i: (i, 0)),
            pl.BlockSpec((tail, d), lambda j, i: (0, 0)),
            pl.BlockSpec((d, tn), lambda j, i: (0, j)),
            pl.BlockSpec((d, tn), lambda j, i: (0, j)),
        ],
        out_specs=[pl.BlockSpec((tm, tn), lambda j, i: (i, j)), pl.BlockSpec((tail, tn), lambda j, i: (0, j))],
        out_shape=[jax.ShapeDtypeStruct((m_p, f), BF16), jax.ShapeDtypeStruct((tail, f), BF16)],
        scratch_shapes=[pltpu.VMEM((d, tn), BF16), pltpu.VMEM((d, tn), BF16)],
        compiler_params=_params(
            2, _nbytes((tm, d), BF16), _nbytes((tail, d), BF16), 2 * _nbytes((d, tn), F32),
            _nbytes((tm, tn), BF16), _nbytes((tail, tn), BF16),
            scratch_bytes=2 * _nbytes((d, tn), BF16) + 3 * _nbytes((tm, tn), F32)),
        name="swiglu",
    )(h_p, h_s, wg, wu)


def _layout(d_model, in_width, state_gla, state_ret, gate_rank):
    _, _, gh, gdk, gdv = state_gla.shape
    _, _, rh, rdk, rdv = state_ret.shape
    gqk, gv, rqk, rv = gh * gdk, gh * gdv, rh * rdk, rh * rdv
    lay = dict(gla_heads=gh, gla_dk=gdk, gla_dv=gdv, ret_heads=rh, ret_dk=rdk, ret_dv=rdv, rank=gate_rank)
    off = 0
    for name, width in (("qa", gqk), ("ka", gqk), ("va", gv), ("ga", gv), ("qb", rqk), ("kb", rqk),
                        ("vb", rv), ("gb", rv), ("mg", 2 * d_model)):
        lay[name] = off
        off += width
    lay["out_cols"] = off
    lay["plain_cols"] = 2 * gqk + gv
    lay["gd_src"] = lay["plain_cols"]
    assert lay["gd_src"] % V7X_LANES == 0 and gate_rank <= V7X_LANES
    assert in_width == off + gate_rank
    return lay


def _layer(x_p, x_s, st_gla, st_ret, wts, lay, tables, final_norm):
    (norm_mix, w_in, w_gate_up, b_gate, gla_norm_w, w_gla_up, ret_norm_w, w_ret_up, w_out, norm_ffn,
     w_ffn_gate, w_ffn_up, w_ffn_down) = wts
    batch, seq, d = x_p.shape
    rank = lay["rank"]
    gqk = lay["gla_heads"] * lay["gla_dk"]
    wup_pad = jnp.zeros((V7X_LANES, gqk), F32).at[:rank].set(w_gate_up)
    bup = b_gate.reshape(1, gqk)
    gnw = gla_norm_w.reshape(1, -1)
    rnw = ret_norm_w.reshape(1, -1)
    tn = 1024 if (lay["out_cols"] % 1024 == 0 and lay["plain_cols"] % 1024 == 0) else 512
    cos_p, sin_p, cos_s, sin_s, log_gamma = tables
    xp = x_p.reshape(batch * seq, d)
    xs = x_s.reshape(-1, d)
    w_in_t = w_in.T

    h_p, h_s = _rmsnorm_bf16(xp, xs, norm_mix)
    proj_p, proj_s, g_p, g_s = _in_proj(h_p, h_s, w_in_t, wup_pad, bup, lay["plain_cols"], rank, lay["out_cols"], tn)
    oa_p, sa_p = _gla_prompt(proj_p, g_p, gnw, lay, batch, seq)
    ob_p, sb_p = _ret_prompt(proj_p, cos_p, sin_p, log_gamma, rnw, lay, batch, seq)
    oa_s, sa_s = _gla_decode(proj_s, g_s, gnw, st_gla, lay)
    ob_s, sb_s = _ret_decode(proj_s, cos_s, sin_s, log_gamma, rnw, st_ret, lay)
    m_p, m_s = _merge(oa_p, ob_p, oa_s, ob_s, w_gla_up, w_ret_up, proj_p, proj_s, lay)
    x1_p, h2_p, x1_s, h2_s = _proj_res_norm(m_p, m_s, w_out, xp, xs, norm_ffn, True, 512)
    act_p, act_s = _swiglu(h2_p, h2_s, w_ffn_gate, w_ffn_up)
    y_p, y_s = _proj_res_norm(act_p, act_s, w_ffn_down, x1_p, x1_s, final_norm, False, 512)
    return (y_p, sa_p, sb_p), (y_s, sa_s, sb_s)


def kernel(x_prompt, x_sample, state_gla, state_ret, norm_mix, w_in, w_gla_gate_up, b_gla_gate, gla_norm_w,
           w_gla_up, ret_norm_w, w_ret_up, w_out, norm_ffn, w_ffn_gate, w_ffn_up, w_ffn_down, norm_final):
    depth = w_in.shape[0]
    assert depth == 1, "single-layer trunk"
    batch, seq, d = x_prompt.shape
    lay = _layout(d, w_in.shape[-1], state_gla, state_ret, w_gla_gate_up.shape[1])
    rh, rdk = lay["ret_heads"], lay["ret_dk"]
    half = rdk // 2
    assert half == V7X_LANES
    cos_p, sin_p = _rope_tables(seq, 0, half)
    cos_s, sin_s = _rope_tables(8, PAST_LEN, half)
    lg = jnp.log1p(-jnp.exp(jnp.linspace(math.log(1.0 / 32), math.log(1.0 / 512), rh))).astype(F32)
    log_gamma = jnp.broadcast_to(lg[:, None, None], (rh, 1, V7X_LANES))
    tables = (cos_p, sin_p, cos_s, sin_s, log_gamma)

    wts = (norm_mix[0], w_in[0], w_gla_gate_up[0], b_gla_gate[0], gla_norm_w[0], w_gla_up[0], ret_norm_w[0],
           w_ret_up[0], w_out[0], norm_ffn[0], w_ffn_gate[0], w_ffn_up[0], w_ffn_down[0])
    (y_p, ga_p, re_p), (y_s, ga_s, re_s) = _layer(
        x_prompt, x_sample, state_gla[0], state_ret[0], wts, lay, tables, norm_final)

    sd = state_gla.dtype
    return (y_p.reshape(batch, seq, d), y_s.reshape(x_sample.shape),
            ga_p[None].astype(sd), re_p[None].astype(state_ret.dtype),
            ga_s[None].astype(sd), re_s[None].astype(state_ret.dtype))
```

```python
import functools
import math

import numpy as np
import jax
import jax.numpy as jnp
from jax import lax
from jax.experimental import pallas as pl
from jax.experimental.pallas import tpu as pltpu

EPS = 1e-6
ROPE_BASE = 10000.0
GLA_GATE_NORM = 16.0
PAST_LEN = 16384

V7X_LANES = 128
V7X_VMEM_REQUEST_CAP = 58 * 1024 * 1024
COMPILER_SCRATCH_BYTES = 12 * 1024 * 1024

GLA_CHUNK = 64
GLA_STEP_CHUNKS = 8
LOG2_E = 1.4426950408889634
RET_CHUNK = 128
DEC_TOKENS = 16
ROW_TILE = 1024

BF16 = jnp.bfloat16
F32 = jnp.float32


def _params(n_axes, *block_bytes, scratch_bytes=0):
    need = 2 * sum(block_bytes) + scratch_bytes + COMPILER_SCRATCH_BYTES
    return pltpu.CompilerParams(
        dimension_semantics=("arbitrary",) * n_axes,
        vmem_limit_bytes=int(min(V7X_VMEM_REQUEST_CAP, need)),
    )


def _nbytes(shape, dtype):
    return int(np.prod(shape)) * jnp.dtype(dtype).itemsize


def _sigmoid(x):
    return 1.0 / (1.0 + jnp.exp(-x))


def _silu(x):
    return x * _sigmoid(x)


def _log_sigmoid(x):
    return jnp.minimum(x, 0.0) - jnp.log(1.0 + jnp.exp(-jnp.abs(x)))


def _dot(a, b):
    return jnp.dot(a, b, preferred_element_type=F32)


def _dot_nt(a, b):
    return lax.dot_general(a, b, (((1,), (1,)), ((), ())), preferred_element_type=F32)


def _dot_tn(a, b):
    return lax.dot_general(a, b, (((0,), (0,)), ((), ())), preferred_element_type=F32)


def _row_tile(m, want):
    t = min(m, want)
    assert m % t == 0, (m, t)
    return t


def _rmsnorm_rows(x, w):
    ms = jnp.mean(x * x, axis=-1, keepdims=True)
    return x * lax.rsqrt(ms + EPS) * w


def _rmsnorm_kernel(xp_ref, xs_ref, w_ref, hp_ref, hs_ref):
    hp_ref[...] = _rmsnorm_rows(xp_ref[...], w_ref[...]).astype(hp_ref.dtype)

    @pl.when(pl.program_id(0) == 0)
    def _():
        hs_ref[...] = _rmsnorm_rows(xs_ref[...], w_ref[...]).astype(hs_ref.dtype)


def _rmsnorm_bf16(x_p, x_s, w):
    m_p, d = x_p.shape
    tail = x_s.shape[0]
    tm = _row_tile(m_p, 512)
    return pl.pallas_call(
        _rmsnorm_kernel,
        grid=(m_p // tm,),
        in_specs=[
            pl.BlockSpec((tm, d), lambda i: (i, 0)),
            pl.BlockSpec((tail, d), lambda i: (0, 0)),
            pl.BlockSpec((1, d), lambda i: (0, 0)),
        ],
        out_specs=[pl.BlockSpec((tm, d), lambda i: (i, 0)), pl.BlockSpec((tail, d), lambda i: (0, 0))],
        out_shape=[jax.ShapeDtypeStruct((m_p, d), BF16), jax.ShapeDtypeStruct((tail, d), BF16)],
        compiler_params=_params(1, _nbytes((tm, d), F32), _nbytes((tail, d), F32), _nbytes((tm, d), BF16)),
        name="rmsnorm",
    )(x_p, x_s, w.reshape(1, d))


def _in_proj_kernel(hp_ref, hs_ref, wm_ref, wn_ref, wgd_ref, wup_ref, bup_ref, op_ref, os_ref, gp_ref, gs_ref,
                    wbf_ref, *, n_plain, shift):
    j = pl.program_id(0)
    i = pl.program_id(1)
    tn = wbf_ref.shape[0]

    def log2_decay(h):
        gd = _dot_nt(h, wgd_ref[...].astype(BF16))
        x = _dot(gd.astype(BF16), wup_ref[...].astype(BF16)) + bup_ref[...]
        return _log_sigmoid(x) * (LOG2_E / GLA_GATE_NORM)

    @pl.when(j == pl.num_programs(0) - 1)
    def _():
        gp_ref[...] = log2_decay(hp_ref[...])

        @pl.when(i == 0)
        def _():
            gs_ref[...] = log2_decay(hs_ref[...])

    @pl.when(jnp.logical_and(i == 0, j < n_plain))
    def _():
        wbf_ref[...] = wm_ref[...].astype(BF16)

    @pl.when(jnp.logical_and(i == 0, j >= n_plain))
    def _():
        wbf_ref[0:tn - shift, :] = wm_ref[shift:tn, :].astype(BF16)
        wbf_ref[tn - shift:tn, :] = wn_ref[...].astype(BF16)

    @pl.when(i == 0)
    def _():
        os_ref[...] = _dot_nt(hs_ref[...], wbf_ref[...]).astype(os_ref.dtype)

    op_ref[...] = _dot_nt(hp_ref[...], wbf_ref[...]).astype(op_ref.dtype)


def _in_proj(h_p, h_s, w_in_t, wup_pad, bup, plain_cols, shift, out_cols, tn):
    m_p, d = h_p.shape
    tail = h_s.shape[0]
    gw = wup_pad.shape[1]
    tm = _row_tile(m_p, ROW_TILE)
    assert plain_cols % tn == 0 and out_cols % tn == 0 and tn % shift == 0 and shift % 8 == 0
    assert plain_cols % V7X_LANES == 0
    n_plain = plain_cols // tn
    n_tiles = out_cols // tn
    kern = functools.partial(_in_proj_kernel, n_plain=n_plain, shift=shift)
    return pl.pallas_call(
        kern,
        grid=(n_tiles, m_p // tm),
        in_specs=[
            pl.BlockSpec((tm, d), lambda j, i: (i, 0)),
            pl.BlockSpec((tail, d), lambda j, i: (0, 0)),
            pl.BlockSpec((tn, d), lambda j, i: (j, 0)),
            pl.BlockSpec((shift, d), lambda j, i: ((j + 1) * (tn // shift), 0)),
            pl.BlockSpec((V7X_LANES, d), lambda j, i: (plain_cols // V7X_LANES, 0)),
            pl.BlockSpec((V7X_LANES, gw), lambda j, i: (0, 0)),
            pl.BlockSpec((1, gw), lambda j, i: (0, 0)),
        ],
        out_specs=[
            pl.BlockSpec((tm, tn), lambda j, i: (i, j)),
            pl.BlockSpec((tail, tn), lambda j, i: (0, j)),
            pl.BlockSpec((tm, gw), lambda j, i: (jnp.where(j == n_tiles - 1, i, 0), 0)),
            pl.BlockSpec((tail, gw), lambda j, i: (0, 0)),
        ],
        out_shape=[
            jax.ShapeDtypeStruct((m_p, out_cols), BF16), jax.ShapeDtypeStruct((tail, out_cols), BF16),
            jax.ShapeDtypeStruct((m_p, gw), F32), jax.ShapeDtypeStruct((tail, gw), F32),
        ],
        scratch_shapes=[pltpu.VMEM((tn, d), BF16)],
        compiler_params=_params(
            2, _nbytes((tm, d), BF16), _nbytes((tail, d), BF16), _nbytes((tn, d), F32), _nbytes((shift, d), F32),
            _nbytes((V7X_LANES, d), F32), _nbytes((V7X_LANES, gw), F32),
            _nbytes((tm, tn), BF16), _nbytes((tail, tn), BF16), _nbytes((tm, gw), F32), _nbytes((tail, gw), F32),
            scratch_bytes=_nbytes((tn, d), BF16) + _nbytes((tm, tn), F32)),
        name="in_proj",
    )(h_p, h_s, w_in_t, w_in_t, w_in_t, wup_pad, bup)


def _prefix_sum_rows(sel3_bf16, g):
    g0 = g.astype(BF16)
    r1 = g - g0.astype(F32)
    g1 = r1.astype(BF16)
    g2 = (r1 - g1.astype(F32)).astype(BF16)
    return _dot(sel3_bf16, jnp.concatenate([g0, g1, g2], axis=0))


def _lane_bcast_cols(row, n):
    parts = []
    for c in range(n // V7X_LANES):
        tile = jnp.broadcast_to(row[:, c * V7X_LANES:(c + 1) * V7X_LANES], (V7X_LANES, V7X_LANES))
        parts.append(tile.T)
    return parts[0] if len(parts) == 1 else jnp.concatenate(parts, axis=0)


def _rms_gate_store(o, w, gate, out_ref, rows, cols):
    ms = jnp.mean(o * o, axis=-1, keepdims=True)
    y = o * lax.rsqrt(ms + EPS) * w
    out_ref[rows, cols] = (y * _silu(gate)).astype(out_ref.dtype)


def _ln_gate_store(o, w, gate, out_ref, rows, cols):
    mu = jnp.mean(o, axis=-1, keepdims=True)
    dlt = o - mu
    var = jnp.mean(dlt * dlt, axis=-1, keepdims=True)
    y = dlt * lax.rsqrt(var + EPS) * w
    out_ref[rows, cols] = (y * _silu(gate)).astype(out_ref.dtype)


def _gla_sum_matrices(c):
    levels = c.bit_length() - 1
    assert 1 << levels == c
    i = lax.broadcasted_iota(jnp.int32, (c, c), 0)
    j = lax.broadcasted_iota(jnp.int32, (c, c), 1)
    mats = [j <= i]
    for l in range(levels):
        ref = jnp.bitwise_or(jnp.bitwise_and(i, -(2 << l)), 1 << l)
        mats.append(jnp.logical_and(j > jnp.minimum(i, ref), j <= jnp.maximum(i, ref)))
    mats.append(j > i)
    sel = jnp.concatenate([jnp.where(m, 1.0, 0.0).astype(BF16) for m in mats], axis=0)
    return jnp.concatenate([sel, sel, sel], axis=1)


def _pair_level(c):
    levels = c.bit_length() - 1
    i = lax.broadcasted_iota(jnp.int32, (c, c), 0)
    j = lax.broadcasted_iota(jnp.int32, (c, c), 1)
    x = jnp.bitwise_xor(i, j)
    lvl = jnp.zeros((c, c), jnp.int32)
    for l in range(1, levels):
        lvl = lvl + jnp.where(x >= (1 << l), 1, 0)
    return jnp.where(i > j, lvl, jnp.where(i == j, levels, -1))


def _queries_else_keys(q, k, l):
    c = q.shape[0]
    span = 1 << l
    if span >= 8:
        parts = [(q if (b & 1) else k)[b * span:(b + 1) * span, :] for b in range(c // span)]
        return jnp.concatenate(parts, axis=0)
    row = lax.broadcasted_iota(jnp.int32, q.shape, 0)
    return jnp.where(jnp.bitwise_and(row, span) != 0, q, k)


def _gla_level_scores(q, k, sums):
    c = q.shape[0]
    levels = c.bit_length() - 1
    out = []
    for l in range(levels):
        x = _queries_else_keys(q, k, l) * jnp.exp2(sums[(1 + l) * c:(2 + l) * c, :])
        xb = x.astype(BF16)
        out.append(_dot_nt(xb, xb))
    return out


def _gla_chunk_out(q, k, v, sums, level_scores, pair_level, state):
    c = q.shape[0]
    levels = c.bit_length() - 1
    scores = jnp.where(pair_level == levels, jnp.sum(q * k, axis=-1, keepdims=True), 0.0)
    for l in range(levels):
        scores = jnp.where(pair_level == l, level_scores[l], scores)
    o = _dot((q * jnp.exp2(sums[0:c, :])).astype(BF16), state.astype(BF16))
    return o + _dot(scores.astype(BF16), v)


def _gla_next_state(k, v, sums, state):
    c, dk = k.shape
    levels = c.bit_length() - 1
    k_tail = (k * jnp.exp2(sums[(levels + 1) * c:(levels + 2) * c, :])).astype(BF16)
    decay = _lane_bcast_cols(jnp.exp2(sums[c - 1:c, :]), dk)
    decay_full = jnp.concatenate([decay] * (v.shape[1] // V7X_LANES), axis=1)
    return decay_full * state + _dot_tn(k_tail, v)


def _gla_prompt_kernel(q_ref, k_ref, v_ref, ga_ref, g_ref, nw_ref,
                       o_ref, s_out_ref, s_ref, mats_ref, lvl_ref, *, heads, dk, dv):
    t = pl.program_id(1)

    @pl.when(t == 0)
    def _():
        s_ref[...] = jnp.zeros_like(s_ref)
        mats_ref[...] = _gla_sum_matrices(GLA_CHUNK)
        lvl_ref[...] = _pair_level(GLA_CHUNK)

    ct = q_ref.shape[0]

    def chunk(c, carry):
        rows = pl.ds(pl.multiple_of(c * GLA_CHUNK, GLA_CHUNK), GLA_CHUNK)
        sums_all = _prefix_sum_rows(mats_ref[...], g_ref[rows, :])
        pair_level = lvl_ref[...]
        kcs = [slice(hh * dk, (hh + 1) * dk) for hh in range(heads)]
        vcs = [slice(hh * dv, (hh + 1) * dv) for hh in range(heads)]
        qs = [q_ref[rows, kc].astype(F32) * (dk ** -0.5) for kc in kcs]
        ks = [k_ref[rows, kc].astype(F32) for kc in kcs]
        lvl_scores = [_gla_level_scores(qs[hh], ks[hh], sums_all[:, kcs[hh]]) for hh in range(heads)]
        outs = [_gla_chunk_out(qs[hh], ks[hh], v_ref[rows, vcs[hh]], sums_all[:, kcs[hh]], lvl_scores[hh],
                               pair_level, s_ref[hh]) for hh in range(heads)]
        for hh in range(heads):
            s_ref[hh] = _gla_next_state(ks[hh], v_ref[rows, vcs[hh]], sums_all[:, kcs[hh]], s_ref[hh])
        for hh in range(heads):
            _rms_gate_store(outs[hh], nw_ref[...], ga_ref[rows, vcs[hh]].astype(F32), o_ref, rows, vcs[hh])
        return carry

    lax.fori_loop(0, ct // GLA_CHUNK, chunk, 0)

    @pl.when(t == pl.num_programs(1) - 1)
    def _():
        s_out_ref[0] = s_ref[...]


def _gla_prompt(proj, log2_decay, norm_w, lay, batch, seq):
    heads, dk, dv = lay["gla_heads"], lay["gla_dk"], lay["gla_dv"]
    qk, vw = heads * dk, heads * dv
    ct = min(seq, GLA_STEP_CHUNKS * GLA_CHUNK)
    levels = GLA_CHUNK.bit_length() - 1
    assert seq % ct == 0 and ct % GLA_CHUNK == 0
    nt = seq // ct
    row = lambda b, t: b * nt + t
    kern = functools.partial(_gla_prompt_kernel, heads=heads, dk=dk, dv=dv)
    return pl.pallas_call(
        kern,
        grid=(batch, nt),
        in_specs=[
            pl.BlockSpec((ct, qk), lambda b, t: (row(b, t), lay["qa"] // qk)),
            pl.BlockSpec((ct, qk), lambda b, t: (row(b, t), lay["ka"] // qk)),
            pl.BlockSpec((ct, vw), lambda b, t: (row(b, t), lay["va"] // vw)),
            pl.BlockSpec((ct, vw), lambda b, t: (row(b, t), lay["ga"] // vw)),
            pl.BlockSpec((ct, qk), lambda b, t: (row(b, t), 0)),
            pl.BlockSpec((1, dv), lambda b, t: (0, 0)),
        ],
        out_specs=[
            pl.BlockSpec((ct, vw), lambda b, t: (row(b, t), 0)),
            pl.BlockSpec((1, heads, dk, dv), lambda b, t: (b, 0, 0, 0)),
        ],
        out_shape=[
            jax.ShapeDtypeStruct((batch * seq, vw), BF16),
            jax.ShapeDtypeStruct((batch, heads, dk, dv), F32),
        ],
        scratch_shapes=[
            pltpu.VMEM((heads, dk, dv), F32),
            pltpu.VMEM(((levels + 2) * GLA_CHUNK, 3 * GLA_CHUNK), BF16),
            pltpu.VMEM((GLA_CHUNK, GLA_CHUNK), jnp.int32),
        ],
        compiler_params=_params(
            2, 2 * _nbytes((ct, qk), BF16), 3 * _nbytes((ct, vw), BF16), _nbytes((ct, qk), F32),
            _nbytes((heads, dk, dv), F32), scratch_bytes=_nbytes((heads, dk, dv), F32)),
        name="gla_prompt",
    )(proj, proj, proj, proj, log2_decay, norm_w)


def _rotary(x, cos, sin):
    half = x.shape[1] // 2
    x1, x2 = x[:, :half], x[:, half:]
    return jnp.concatenate([x1 * cos - x2 * sin, x1 * sin + x2 * cos], axis=1)


def _ret_prompt_kernel(q_ref, k_ref, v_ref, gb_ref, cos_ref, sin_ref, lg_ref, nw_ref,
                       o_ref, s_out_ref, s_ref, *, heads, dk, dv):
    t = pl.program_id(1)

    @pl.when(t == 0)
    def _():
        s_ref[...] = jnp.zeros_like(s_ref)

    c = q_ref.shape[0]
    cos, sin = cos_ref[...], sin_ref[...]
    nw = nw_ref[...]
    ri = lax.broadcasted_iota(jnp.int32, (c, c), 0)
    rj = lax.broadcasted_iota(jnp.int32, (c, c), 1)
    dist = (ri - rj).astype(F32)
    rowk = lax.broadcasted_iota(jnp.int32, (c, dk), 0).astype(F32)
    rows = slice(0, c)
    for hh in range(heads):
        lg = lg_ref[hh]
        kc = slice(hh * dk, (hh + 1) * dk)
        vc = slice(hh * dv, (hh + 1) * dv)
        qr = _rotary(q_ref[:, kc].astype(F32), cos, sin)
        kr = _rotary(k_ref[:, kc].astype(F32), cos, sin) * (dk ** -0.5)
        v = v_ref[:, vc]
        lgk = jnp.concatenate([lg] * (dk // V7X_LANES), axis=1)
        lgc = lg[:, :1]
        decay = jnp.exp(jnp.where(ri >= rj, dist * lgc, -jnp.inf))
        scores = _dot_nt(qr.astype(BF16), kr.astype(BF16)) * decay
        state = s_ref[hh]
        o = _dot((qr * jnp.exp((rowk + 1.0) * lgk)).astype(BF16), state.astype(BF16))
        o = o + _dot(scores.astype(BF16), v)
        k_tail = (kr * jnp.exp((float(c - 1) - rowk) * lgk)).astype(BF16)
        lgv = jnp.concatenate([lg] * (dv // V7X_LANES), axis=1)
        s_ref[hh] = jnp.exp(float(c) * lgv) * state + _dot_tn(k_tail, v)
        _ln_gate_store(o, nw, gb_ref[:, vc].astype(F32), o_ref, rows, vc)

    @pl.when(t == pl.num_programs(1) - 1)
    def _():
        s_out_ref[0] = s_ref[...]


def _ret_prompt(proj, cos, sin, log_gamma, norm_w, lay, batch, seq):
    heads, dk, dv = lay["ret_heads"], lay["ret_dk"], lay["ret_dv"]
    qk, vw = heads * dk, heads * dv
    c = min(seq, RET_CHUNK)
    assert seq % c == 0
    nt = seq // c
    half = dk // 2
    row = lambda b, t: b * nt + t
    kern = functools.partial(_ret_prompt_kernel, heads=heads, dk=dk, dv=dv)
    return pl.pallas_call(
        kern,
        grid=(batch, nt),
        in_specs=[
            pl.BlockSpec((c, qk), lambda b, t: (row(b, t), lay["qb"] // qk)),
            pl.BlockSpec((c, qk), lambda b, t: (row(b, t), lay["kb"] // qk)),
            pl.BlockSpec((c, vw), lambda b, t: (row(b, t), lay["vb"] // vw)),
            pl.BlockSpec((c, vw), lambda b, t: (row(b, t), lay["gb"] // vw)),
            pl.BlockSpec((c, half), lambda b, t: (t, 0)),
            pl.BlockSpec((c, half), lambda b, t: (t, 0)),
            pl.BlockSpec((heads, 1, V7X_LANES), lambda b, t: (0, 0, 0)),
            pl.BlockSpec((1, dv), lambda b, t: (0, 0)),
        ],
        out_specs=[
            pl.BlockSpec((c, vw), lambda b, t: (row(b, t), 0)),
            pl.BlockSpec((1, heads, dk, dv), lambda b, t: (b, 0, 0, 0)),
        ],
        out_shape=[
            jax.ShapeDtypeStruct((batch * seq, vw), BF16),
            jax.ShapeDtypeStruct((batch, heads, dk, dv), F32),
        ],
        scratch_shapes=[pltpu.VMEM((heads, dk, dv), F32)],
        compiler_params=_params(
            2, 2 * _nbytes((c, qk), BF16), 3 * _nbytes((c, vw), BF16), 2 * _nbytes((c, half), F32),
            _nbytes((heads, dk, dv), F32), scratch_bytes=_nbytes((heads, dk, dv), F32)),
        name="ret_prompt",
    )(proj, proj, proj, proj, cos, sin, log_gamma, norm_w)


def _decode_tokens(a_cols_fn, k, q, v, s_in_ref, s_out_ref, o_scr):
    tokens, dk = k.shape
    dv = v.shape[1]
    reps = dv // V7X_LANES
    for tt in range(tokens):
        k_cols = jnp.concatenate([_lane_bcast_cols(k[tt:tt + 1, :], dk)] * reps, axis=1)
        q_cols = jnp.concatenate([_lane_bcast_cols(q[tt:tt + 1, :], dk)] * reps, axis=1)
        s_new = a_cols_fn(tt) * s_in_ref[tt, 0] + k_cols * v[tt:tt + 1, :]
        s_out_ref[tt, 0] = s_new
        o_scr[tt:tt + 1, :] = jnp.sum(q_cols * s_new, axis=0, keepdims=True)


def _gla_decode_kernel(q_ref, k_ref, v_ref, ga_ref, g_ref, nw_ref, s_in_ref,
                       o_ref, s_out_ref, o_scr, *, dk, dv):
    a = jnp.exp2(g_ref[...])
    q = q_ref[...].astype(F32) * (dk ** -0.5)
    k = k_ref[...].astype(F32)
    v = v_ref[...].astype(F32)
    reps = dv // V7X_LANES
    a_cols = lambda tt: jnp.concatenate([_lane_bcast_cols(a[tt:tt + 1, :], dk)] * reps, axis=1)
    _decode_tokens(a_cols, k, q, v, s_in_ref, s_out_ref, o_scr)
    tokens = q.shape[0]
    _rms_gate_store(o_scr[...], nw_ref[...], ga_ref[...].astype(F32), o_ref, slice(0, tokens), slice(0, dv))


def _gla_decode(proj, log2_decay, norm_w, state, lay):
    heads, dk, dv = lay["gla_heads"], lay["gla_dk"], lay["gla_dv"]
    n = proj.shape[0]
    tk = DEC_TOKENS
    assert n % tk == 0
    kern = functools.partial(_gla_decode_kernel, dk=dk, dv=dv)
    return pl.pallas_call(
        kern,
        grid=(n // tk, heads),
        in_specs=[
            pl.BlockSpec((tk, dk), lambda i, hh: (i, lay["qa"] // dk + hh)),
            pl.BlockSpec((tk, dk), lambda i, hh: (i, lay["ka"] // dk + hh)),
            pl.BlockSpec((tk, dv), lambda i, hh: (i, lay["va"] // dv + hh)),
            pl.BlockSpec((tk, dv), lambda i, hh: (i, lay["ga"] // dv + hh)),
            pl.BlockSpec((tk, dk), lambda i, hh: (i, hh)),
            pl.BlockSpec((1, dv), lambda i, hh: (0, 0)),
            pl.BlockSpec((tk, 1, dk, dv), lambda i, hh: (i, hh, 0, 0)),
        ],
        out_specs=[
            pl.BlockSpec((tk, dv), lambda i, hh: (i, hh)),
            pl.BlockSpec((tk, 1, dk, dv), lambda i, hh: (i, hh, 0, 0)),
        ],
        out_shape=[
            jax.ShapeDtypeStruct((n, heads * dv), BF16),
            jax.ShapeDtypeStruct(state.shape, state.dtype),
        ],
        scratch_shapes=[pltpu.VMEM((tk, dv), F32)],
        compiler_params=_params(2, 2 * _nbytes((tk, dk, dv), F32), scratch_bytes=_nbytes((tk, dv), F32)),
        name="gla_decode",
    )(proj, proj, proj, proj, log2_decay, norm_w, state)


def _ret_decode_kernel(q_ref, k_ref, v_ref, gb_ref, cos_ref, sin_ref, lg_ref, nw_ref, s_in_ref,
                       o_ref, s_out_ref, o_scr, *, dk, dv):
    cos, sin = cos_ref[0:1, :], sin_ref[0:1, :]
    q = _rotary(q_ref[...].astype(F32), cos, sin)
    k = _rotary(k_ref[...].astype(F32), cos, sin) * (dk ** -0.5)
    v = v_ref[...].astype(F32)
    gamma = jnp.exp(jnp.concatenate([lg_ref[0]] * (dv // V7X_LANES), axis=1))
    _decode_tokens(lambda tt: gamma, k, q, v, s_in_ref, s_out_ref, o_scr)
    tokens = q.shape[0]
    _ln_gate_store(o_scr[...], nw_ref[...], gb_ref[...].astype(F32), o_ref, slice(0, tokens), slice(0, dv))


def _ret_decode(proj, cos, sin, log_gamma, norm_w, state, lay):
    heads, dk, dv = lay["ret_heads"], lay["ret_dk"], lay["ret_dv"]
    n = proj.shape[0]
    tk = DEC_TOKENS
    assert n % tk == 0
    half = dk // 2
    kern = functools.partial(_ret_decode_kernel, dk=dk, dv=dv)
    return pl.pallas_call(
        kern,
        grid=(n // tk, heads),
        in_specs=[
            pl.BlockSpec((tk, dk), lambda i, hh: (i, lay["qb"] // dk + hh)),
            pl.BlockSpec((tk, dk), lambda i, hh: (i, lay["kb"] // dk + hh)),
            pl.BlockSpec((tk, dv), lambda i, hh: (i, lay["vb"] // dv + hh)),
            pl.BlockSpec((tk, dv), lambda i, hh: (i, lay["gb"] // dv + hh)),
            pl.BlockSpec((8, half), lambda i, hh: (0, 0)),
            pl.BlockSpec((8, half), lambda i, hh: (0, 0)),
            pl.BlockSpec((1, 1, V7X_LANES), lambda i, hh: (hh, 0, 0)),
            pl.BlockSpec((1, dv), lambda i, hh: (0, 0)),
            pl.BlockSpec((tk, 1, dk, dv), lambda i, hh: (i, hh, 0, 0)),
        ],
        out_specs=[
            pl.BlockSpec((tk, dv), lambda i, hh: (i, hh)),
            pl.BlockSpec((tk, 1, dk, dv), lambda i, hh: (i, hh, 0, 0)),
        ],
        out_shape=[
            jax.ShapeDtypeStruct((n, heads * dv), BF16),
            jax.ShapeDtypeStruct(state.shape, state.dtype),
        ],
        scratch_shapes=[pltpu.VMEM((tk, dv), F32)],
        compiler_params=_params(2, 2 * _nbytes((tk, dk, dv), F32), scratch_bytes=_nbytes((tk, dv), F32)),
        name="ret_decode",
    )(proj, proj, proj, proj, cos, sin, log_gamma, norm_w, state)


def _rope_table_kernel(cos_ref, sin_ref, *, pos0, half):
    rows, lanes = cos_ref.shape
    pos = (lax.broadcasted_iota(jnp.int32, (rows, lanes), 0) + (pl.program_id(0) * rows + pos0)).astype(F32)
    idx = lax.broadcasted_iota(jnp.int32, (rows, lanes), 1).astype(F32)
    inv = jnp.exp(idx * (-math.log(ROPE_BASE) / half))
    ang = pos * inv
    cos_ref[...] = jnp.cos(ang)
    sin_ref[...] = jnp.sin(ang)


def _rope_tables(n_pos, pos0, half):
    rows = min(n_pos, 256)
    assert n_pos % rows == 0
    kern = functools.partial(_rope_table_kernel, pos0=pos0, half=half)
    spec = pl.BlockSpec((rows, half), lambda i: (i, 0))
    return pl.pallas_call(
        kern,
        grid=(n_pos // rows,),
        in_specs=[],
        out_specs=[spec, spec],
        out_shape=[jax.ShapeDtypeStruct((n_pos, half), F32)] * 2,
        compiler_params=_params(1, 2 * _nbytes((rows, half), F32)),
        name="rope_tables",
    )()


def _merge_kernel(oap_ref, obp_ref, oas_ref, obs_ref, wa_ref, wb_ref, g0p_ref, g1p_ref, g0s_ref, g1s_ref, wnext_ref,
                  mp_ref, ms_ref, wnext_bf_ref, wa_bf, wb_bf):
    wnext_bf_ref[...] = wnext_ref[...].astype(BF16)

    def merged(oa, ob, g0, g1):
        ya = _dot(oa, wa_bf[...])
        yb = _dot(ob, wb_bf[...])
        return _sigmoid(g0.astype(F32)) * ya + _sigmoid(g1.astype(F32)) * yb

    @pl.when(pl.program_id(1) == 0)
    def _():
        wa_bf[...] = wa_ref[...].astype(BF16)
        wb_bf[...] = wb_ref[...].astype(BF16)
        ms_ref[...] = merged(oas_ref[...], obs_ref[...], g0s_ref[...], g1s_ref[...]).astype(ms_ref.dtype)

    mp_ref[...] = merged(oap_ref[...], obp_ref[...], g0p_ref[...], g1p_ref[...]).astype(mp_ref.dtype)


def _slab_specs(w_next, n_steps, step_of):
    kn, dn = w_next.shape
    assert kn % n_steps == 0 and (kn // n_steps) % 16 == 0, (kn, n_steps)
    slab = kn // n_steps
    spec = pl.BlockSpec((slab, dn), lambda j, i: (step_of(j, i), 0))
    return spec, spec, jax.ShapeDtypeStruct((kn, dn), BF16), _nbytes((slab, dn), F32) + _nbytes((slab, dn), BF16)


def _merge(oa_p, ob_p, oa_s, ob_s, wa, wb, proj_p, proj_s, lay, w_next):
    m_p, ka = oa_p.shape
    kb = ob_p.shape[1]
    tail = oa_s.shape[0]
    d = wa.shape[1]
    tm = _row_tile(m_p, ROW_TILE)
    tn = min(d, 1024)
    assert d % tn == 0 and lay["mg"] % tn == 0
    g0 = lay["mg"] // tn
    g1 = (lay["mg"] + d) // tn
    n_m = m_p // tm
    slab_in, slab_out, slab_shape, slab_bytes = _slab_specs(w_next, (d // tn) * n_m, lambda j, i: j * n_m + i)
    return pl.pallas_call(
        _merge_kernel,
        grid=(d // tn, n_m),
        in_specs=[
            pl.BlockSpec((tm, ka), lambda j, i: (i, 0)),
            pl.BlockSpec((tm, kb), lambda j, i: (i, 0)),
            pl.BlockSpec((tail, ka), lambda j, i: (0, 0)),
            pl.BlockSpec((tail, kb), lambda j, i: (0, 0)),
            pl.BlockSpec((ka, tn), lambda j, i: (0, j)),
            pl.BlockSpec((kb, tn), lambda j, i: (0, j)),
            pl.BlockSpec((tm, tn), lambda j, i: (i, g0 + j)),
            pl.BlockSpec((tm, tn), lambda j, i: (i, g1 + j)),
            pl.BlockSpec((tail, tn), lambda j, i: (0, g0 + j)),
            pl.BlockSpec((tail, tn), lambda j, i: (0, g1 + j)),
            slab_in,
        ],
        out_specs=[pl.BlockSpec((tm, tn), lambda j, i: (i, j)), pl.BlockSpec((tail, tn), lambda j, i: (0, j)),
                   slab_out],
        out_shape=[jax.ShapeDtypeStruct((m_p, d), BF16), jax.ShapeDtypeStruct((tail, d), BF16), slab_shape],
        scratch_shapes=[pltpu.VMEM((ka, tn), BF16), pltpu.VMEM((kb, tn), BF16)],
        compiler_params=_params(
            2, _nbytes((tm, ka), BF16), _nbytes((tm, kb), BF16), _nbytes((tail, ka), BF16), _nbytes((tail, kb), BF16),
            _nbytes((ka, tn), F32), _nbytes((kb, tn), F32), 3 * _nbytes((tm, tn), BF16), 3 * _nbytes((tail, tn), BF16),
            slab_bytes,
            scratch_bytes=_nbytes((ka, tn), BF16) + _nbytes((kb, tn), BF16) + 3 * _nbytes((tm, tn), F32)),
        name="merge",
    )(oa_p, ob_p, oa_s, ob_s, wa, wb, proj_p, proj_p, proj_s, proj_s, w_next)


def _proj_res_norm_kernel(ap_ref, as_ref, w_ref, resp_ref, ress_ref, nw_ref, *out_refs, emit_sum):
    n_out = 2 if emit_sum else 1
    outs_p, outs_s = out_refs[:n_out], out_refs[n_out:]
    i = pl.program_id(0)
    k = pl.program_id(1)
    last_k = k == pl.num_programs(1) - 1
    d = w_ref.shape[1]
    col_chunk = min(d, 512)

    def step(a_ref, res_ref, outs):
        acc_ref = outs[0]
        nrow = acc_ref.shape[0]
        row_chunk = min(nrow, 128)
        assert nrow % row_chunk == 0

        @pl.when(k == 0)
        def _():
            acc_ref[...] = res_ref[...]

        a = a_ref[...]
        for c in range(d // col_chunk):
            cs = slice(c * col_chunk, (c + 1) * col_chunk)
            acc_ref[:, cs] += _dot(a, w_ref[:, cs])

        @pl.when(last_k)
        def _():
            def body(c, carry):
                rr = pl.ds(pl.multiple_of(c * row_chunk, row_chunk), row_chunk)
                y = _rmsnorm_rows(acc_ref[rr, :], nw_ref[...])
                if emit_sum:
                    outs[1][rr, :] = y.astype(outs[1].dtype)
                else:
                    acc_ref[rr, :] = y
                return carry

            lax.fori_loop(0, nrow // row_chunk, body, 0)

    step(ap_ref, resp_ref, outs_p)

    @pl.when(i == pl.num_programs(0) - 1)
    def _():
        step(as_ref, ress_ref, outs_s)


def _proj_res_norm(a_p, a_s, w, res_p, res_s, norm_w, emit_sum, tk):
    m_p, kdim = a_p.shape
    tail = a_s.shape[0]
    d = w.shape[1]
    tm = _row_tile(m_p, ROW_TILE)
    tk = min(kdim, tk)
    assert kdim % tk == 0
    p_spec = pl.BlockSpec((tm, d), lambda i, k: (i, 0))
    s_spec = pl.BlockSpec((tail, d), lambda i, k: (0, 0))
    out_specs = [p_spec, s_spec]
    out_shape = [jax.ShapeDtypeStruct((m_p, d), F32), jax.ShapeDtypeStruct((tail, d), F32)]
    assert w.dtype == BF16
    blocks = [_nbytes((tm, tk), BF16), _nbytes((tail, tk), BF16), _nbytes((tk, d), BF16),
              2 * _nbytes((tm, d), F32), 2 * _nbytes((tail, d), F32)]
    if emit_sum:
        out_specs = [p_spec, p_spec, s_spec, s_spec]
        out_shape = [out_shape[0], jax.ShapeDtypeStruct((m_p, d), BF16),
                     out_shape[1], jax.ShapeDtypeStruct((tail, d), BF16)]
        blocks += [_nbytes((tm, d), BF16), _nbytes((tail, d), BF16)]
    return pl.pallas_call(
        functools.partial(_proj_res_norm_kernel, emit_sum=emit_sum),
        grid=(m_p // tm, kdim // tk),
        in_specs=[
            pl.BlockSpec((tm, tk), lambda i, k: (i, k)),
            pl.BlockSpec((tail, tk), lambda i, k: (0, k)),
            pl.BlockSpec((tk, d), lambda i, k: (k, 0)),
            p_spec,
            s_spec,
            pl.BlockSpec((1, d), lambda i, k: (0, 0)),
        ],
        out_specs=out_specs,
        out_shape=out_shape,
        compiler_params=_params(2, *blocks),
        name="proj_res_norm",
    )(a_p, a_s, w, res_p, res_s, norm_w.reshape(1, d))


def _swiglu_kernel(hp_ref, hs_ref, wg_ref, wu_ref, wnext_ref, op_ref, os_ref, wnext_bf_ref, wg_bf, wu_bf):
    wnext_bf_ref[...] = wnext_ref[...].astype(BF16)

    def act(h):
        a = _dot(h, wg_bf[...])
        b = _dot(h, wu_bf[...])
        return _silu(a) * b

    @pl.when(pl.program_id(1) == 0)
    def _():
        wg_bf[...] = wg_ref[...].astype(BF16)
        wu_bf[...] = wu_ref[...].astype(BF16)
        os_ref[...] = act(hs_ref[...]).astype(os_ref.dtype)

    op_ref[...] = act(hp_ref[...]).astype(op_ref.dtype)


def _swiglu(h_p, h_s, wg, wu, w_next):
    m_p, d = h_p.shape
    tail = h_s.shape[0]
    f = wg.shape[1]
    tm = _row_tile(m_p, ROW_TILE)
    tn = 512 if f % 512 == 0 else 256
    assert f % tn == 0
    n_m = m_p // tm
    slab_in, slab_out, slab_shape, slab_bytes = _slab_specs(w_next, (f // tn) * n_m, lambda j, i: j * n_m + i)
    return pl.pallas_call(
        _swiglu_kernel,
        grid=(f // tn, n_m),
        in_specs=[
            pl.BlockSpec((tm, d), lambda j, i: (i, 0)),
            pl.BlockSpec((tail, d), lambda j, i: (0, 0)),
            pl.BlockSpec((d, tn), lambda j, i: (0, j)),
            pl.BlockSpec((d, tn), lambda j, i: (0, j)),
            slab_in,
        ],
        out_specs=[pl.BlockSpec((tm, tn), lambda j, i: (i, j)), pl.BlockSpec((tail, tn), lambda j, i: (0, j)),
                   slab_out],
        out_shape=[jax.ShapeDtypeStruct((m_p, f), BF16), jax.ShapeDtypeStruct((tail, f), BF16), slab_shape],
        scratch_shapes=[pltpu.VMEM((d, tn), BF16), pltpu.VMEM((d, tn), BF16)],
        compiler_params=_params(
            2, _nbytes((tm, d), BF16), _nbytes((tail, d), BF16), 2 * _nbytes((d, tn), F32),
            _nbytes((tm, tn), BF16), _nbytes((tail, tn), BF16), slab_bytes,
            scratch_bytes=2 * _nbytes((d, tn), BF16) + 3 * _nbytes((tm, tn), F32)),
        name="swiglu",
    )(h_p, h_s, wg, wu, w_next)


def _layout(d_model, in_width, state_gla, state_ret, gate_rank):
    _, _, gh, gdk, gdv = state_gla.shape
    _, _, rh, rdk, rdv = state_ret.shape
    gqk, gv, rqk, rv = gh * gdk, gh * gdv, rh * rdk, rh * rdv
    lay = dict(gla_heads=gh, gla_dk=gdk, gla_dv=gdv, ret_heads=rh, ret_dk=rdk, ret_dv=rdv, rank=gate_rank)
    off = 0
    for name, width in (("qa", gqk), ("ka", gqk), ("va", gv), ("ga", gv), ("qb", rqk), ("kb", rqk),
                        ("vb", rv), ("gb", rv), ("mg", 2 * d_model)):
        lay[name] = off
        off += width
    lay["out_cols"] = off
    lay["plain_cols"] = 2 * gqk + gv
    lay["gd_src"] = lay["plain_cols"]
    assert lay["gd_src"] % V7X_LANES == 0 and gate_rank <= V7X_LANES
    assert in_width == off + gate_rank
    return lay


def _layer(x_p, x_s, st_gla, st_ret, wts, lay, tables, final_norm):
    (norm_mix, w_in, w_gate_up, b_gate, gla_norm_w, w_gla_up, ret_norm_w, w_ret_up, w_out, norm_ffn,
     w_ffn_gate, w_ffn_up, w_ffn_down) = wts
    batch, seq, d = x_p.shape
    rank = lay["rank"]
    gqk = lay["gla_heads"] * lay["gla_dk"]
    wup_pad = jnp.zeros((V7X_LANES, gqk), F32).at[:rank].set(w_gate_up)
    bup = b_gate.reshape(1, gqk)
    gnw = gla_norm_w.reshape(1, -1)
    rnw = ret_norm_w.reshape(1, -1)
    tn = 1024 if (lay["out_cols"] % 1024 == 0 and lay["plain_cols"] % 1024 == 0) else 512
    cos_p, sin_p, cos_s, sin_s, log_gamma = tables
    xp = x_p.reshape(batch * seq, d)
    xs = x_s.reshape(-1, d)
    w_in_t = w_in.T

    h_p, h_s = _rmsnorm_bf16(xp, xs, norm_mix)
    proj_p, proj_s, g_p, g_s = _in_proj(h_p, h_s, w_in_t, wup_pad, bup, lay["plain_cols"], rank, lay["out_cols"], tn)
    oa_p, sa_p = _gla_prompt(proj_p, g_p, gnw, lay, batch, seq)
    ob_p, sb_p = _ret_prompt(proj_p, cos_p, sin_p, log_gamma, rnw, lay, batch, seq)
    oa_s, sa_s = _gla_decode(proj_s, g_s, gnw, st_gla, lay)
    ob_s, sb_s = _ret_decode(proj_s, cos_s, sin_s, log_gamma, rnw, st_ret, lay)
    m_p, m_s, w_out_bf = _merge(oa_p, ob_p, oa_s, ob_s, w_gla_up, w_ret_up, proj_p, proj_s, lay, w_out)
    x1_p, h2_p, x1_s, h2_s = _proj_res_norm(m_p, m_s, w_out_bf, xp, xs, norm_ffn, True, 512)
    act_p, act_s, w_down_bf = _swiglu(h2_p, h2_s, w_ffn_gate, w_ffn_up, w_ffn_down)
    y_p, y_s = _proj_res_norm(act_p, act_s, w_down_bf, x1_p, x1_s, final_norm, False, 512)
    return (y_p, sa_p, sb_p), (y_s, sa_s, sb_s)


def kernel(x_prompt, x_sample, state_gla, state_ret, norm_mix, w_in, w_gla_gate_up, b_gla_gate, gla_norm_w,
           w_gla_up, ret_norm_w, w_ret_up, w_out, norm_ffn, w_ffn_gate, w_ffn_up, w_ffn_down, norm_final):
    depth = w_in.shape[0]
    assert depth == 1, "single-layer trunk"
    batch, seq, d = x_prompt.shape
    lay = _layout(d, w_in.shape[-1], state_gla, state_ret, w_gla_gate_up.shape[1])
    rh, rdk = lay["ret_heads"], lay["ret_dk"]
    half = rdk // 2
    assert half == V7X_LANES
    cos_p, sin_p = _rope_tables(seq, 0, half)
    cos_s, sin_s = _rope_tables(8, PAST_LEN, half)
    lg = jnp.log1p(-jnp.exp(jnp.linspace(math.log(1.0 / 32), math.log(1.0 / 512), rh))).astype(F32)
    log_gamma = jnp.broadcast_to(lg[:, None, None], (rh, 1, V7X_LANES))
    tables = (cos_p, sin_p, cos_s, sin_s, log_gamma)

    wts = (norm_mix[0], w_in[0], w_gla_gate_up[0], b_gla_gate[0], gla_norm_w[0], w_gla_up[0], ret_norm_w[0],
           w_ret_up[0], w_out[0], norm_ffn[0], w_ffn_gate[0], w_ffn_up[0], w_ffn_down[0])
    (y_p, ga_p, re_p), (y_s, ga_s, re_s) = _layer(
        x_prompt, x_sample, state_gla[0], state_ret[0], wts, lay, tables, norm_final)

    sd = state_gla.dtype
    return (y_p.reshape(batch, seq, d), y_s.reshape(x_sample.shape),
            ga_p[None].astype(sd), re_p[None].astype(state_ret.dtype),
            ga_s[None].astype(sd), re_s[None].astype(state_ret.dtype))
```

```python
import functools
import math

import numpy as np
import jax
import jax.numpy as jnp
from jax import lax
from jax.experimental import pallas as pl
from jax.experimental.pallas import tpu as pltpu

EPS = 1e-6
ROPE_BASE = 10000.0
GLA_GATE_NORM = 16.0
PAST_LEN = 16384

V7X_LANES = 128
V7X_VMEM_REQUEST_CAP = 60000 * 1024
COMPILER_SCRATCH_BYTES = 12 * 1024 * 1024

GLA_CHUNK = 64
GLA_STEP_CHUNKS = 8
LOG2_E = 1.4426950408889634
RET_CHUNK = 128
DEC_TOKENS = 16
ROW_TILE = 1024
ROW_TILE_WIDE = 2048

BF16 = jnp.bfloat16
F32 = jnp.float32


def _params(n_axes, *block_bytes, scratch_bytes=0):
    need = 2 * sum(block_bytes) + scratch_bytes + COMPILER_SCRATCH_BYTES
    return pltpu.CompilerParams(
        dimension_semantics=("arbitrary",) * n_axes,
        vmem_limit_bytes=int(min(V7X_VMEM_REQUEST_CAP, need)),
    )


def _nbytes(shape, dtype):
    return int(np.prod(shape)) * jnp.dtype(dtype).itemsize


def _sigmoid(x):
    return 1.0 / (1.0 + jnp.exp(-x))


def _silu(x):
    return x * _sigmoid(x)


def _log_sigmoid(x):
    return jnp.minimum(x, 0.0) - jnp.log(1.0 + jnp.exp(-jnp.abs(x)))


def _dot(a, b):
    return jnp.dot(a, b, preferred_element_type=F32)


def _dot_nt(a, b):
    return lax.dot_general(a, b, (((1,), (1,)), ((), ())), preferred_element_type=F32)


def _dot_tn(a, b):
    return lax.dot_general(a, b, (((0,), (0,)), ((), ())), preferred_element_type=F32)


def _row_tile(m, want):
    t = min(m, want)
    assert m % t == 0, (m, t)
    return t


def _rmsnorm_rows(x, w):
    ms = jnp.mean(x * x, axis=-1, keepdims=True)
    return x * lax.rsqrt(ms + EPS) * w


def _rmsnorm_kernel(xp_ref, xs_ref, w_ref, wgd_ref, wup_ref, bup_ref, hp_ref, hs_ref, gp_ref, gs_ref, wgd_bf, wup_bf):
    @pl.when(pl.program_id(0) == 0)
    def _():
        wgd_bf[...] = wgd_ref[...].astype(BF16)
        wup_bf[...] = wup_ref[...].astype(BF16)

    def rows(x_ref, h_ref, g_ref):
        h = _rmsnorm_rows(x_ref[...], w_ref[...]).astype(h_ref.dtype)
        h_ref[...] = h
        gd = _dot_nt(h, wgd_bf[...])
        x = _dot(gd.astype(BF16), wup_bf[...]) + bup_ref[...]
        g_ref[...] = _log_sigmoid(x) * (LOG2_E / GLA_GATE_NORM)

    rows(xp_ref, hp_ref, gp_ref)

    @pl.when(pl.program_id(0) == 0)
    def _():
        rows(xs_ref, hs_ref, gs_ref)


def _rmsnorm_gate(x_p, x_s, w, w_in_t, gate_row0, wup_pad, bup):
    m_p, d = x_p.shape
    tail = x_s.shape[0]
    gw = wup_pad.shape[1]
    tm = _row_tile(m_p, 512)
    assert gate_row0 % V7X_LANES == 0
    return pl.pallas_call(
        _rmsnorm_kernel,
        grid=(m_p // tm,),
        in_specs=[
            pl.BlockSpec((tm, d), lambda i: (i, 0)),
            pl.BlockSpec((tail, d), lambda i: (0, 0)),
            pl.BlockSpec((1, d), lambda i: (0, 0)),
            pl.BlockSpec((V7X_LANES, d), lambda i: (gate_row0 // V7X_LANES, 0)),
            pl.BlockSpec((V7X_LANES, gw), lambda i: (0, 0)),
            pl.BlockSpec((1, gw), lambda i: (0, 0)),
        ],
        out_specs=[
            pl.BlockSpec((tm, d), lambda i: (i, 0)), pl.BlockSpec((tail, d), lambda i: (0, 0)),
            pl.BlockSpec((tm, gw), lambda i: (i, 0)), pl.BlockSpec((tail, gw), lambda i: (0, 0)),
        ],
        out_shape=[
            jax.ShapeDtypeStruct((m_p, d), BF16), jax.ShapeDtypeStruct((tail, d), BF16),
            jax.ShapeDtypeStruct((m_p, gw), F32), jax.ShapeDtypeStruct((tail, gw), F32),
        ],
        scratch_shapes=[pltpu.VMEM((V7X_LANES, d), BF16), pltpu.VMEM((V7X_LANES, gw), BF16)],
        compiler_params=_params(
            1, _nbytes((tm, d), F32), _nbytes((tail, d), F32), _nbytes((tm, d), BF16), _nbytes((V7X_LANES, d), F32),
            _nbytes((tm, gw), F32), _nbytes((tail, gw), F32)),
        name="rmsnorm_gate",
    )(x_p, x_s, w.reshape(1, d), w_in_t, wup_pad, bup)


def _in_proj_kernel(hp_ref, hs_ref, wm_ref, wn_ref, op_ref, os_ref, wbf_ref, *, n_plain, shift):
    j = pl.program_id(0)
    i = pl.program_id(1)
    tn = wbf_ref.shape[0]

    @pl.when(jnp.logical_and(i == 0, j < n_plain))
    def _():
        wbf_ref[...] = wm_ref[...].astype(BF16)

    @pl.when(jnp.logical_and(i == 0, j >= n_plain))
    def _():
        wbf_ref[0:tn - shift, :] = wm_ref[shift:tn, :].astype(BF16)
        wbf_ref[tn - shift:tn, :] = wn_ref[...].astype(BF16)

    @pl.when(i == 0)
    def _():
        os_ref[...] = _dot_nt(hs_ref[...], wbf_ref[...]).astype(os_ref.dtype)

    op_ref[...] = _dot_nt(hp_ref[...], wbf_ref[...]).astype(op_ref.dtype)


def _in_proj(h_p, h_s, w_in_t, plain_cols, shift, out_cols, tn):
    m_p, d = h_p.shape
    tail = h_s.shape[0]
    tm = _row_tile(m_p, ROW_TILE_WIDE)
    assert plain_cols % tn == 0 and out_cols % tn == 0 and tn % shift == 0 and shift % 8 == 0
    n_plain = plain_cols // tn
    kern = functools.partial(_in_proj_kernel, n_plain=n_plain, shift=shift)
    return pl.pallas_call(
        kern,
        grid=(out_cols // tn, m_p // tm),
        in_specs=[
            pl.BlockSpec((tm, d), lambda j, i: (i, 0)),
            pl.BlockSpec((tail, d), lambda j, i: (0, 0)),
            pl.BlockSpec((tn, d), lambda j, i: (j, 0)),
            pl.BlockSpec((shift, d), lambda j, i: ((j + 1) * (tn // shift), 0)),
        ],
        out_specs=[pl.BlockSpec((tm, tn), lambda j, i: (i, j)), pl.BlockSpec((tail, tn), lambda j, i: (0, j))],
        out_shape=[jax.ShapeDtypeStruct((m_p, out_cols), BF16), jax.ShapeDtypeStruct((tail, out_cols), BF16)],
        scratch_shapes=[pltpu.VMEM((tn, d), BF16)],
        compiler_params=_params(
            2, _nbytes((tm, d), BF16), _nbytes((tail, d), BF16), _nbytes((tn, d), F32), _nbytes((shift, d), F32),
            _nbytes((tm, tn), BF16), _nbytes((tail, tn), BF16),
            scratch_bytes=_nbytes((tn, d), BF16) + _nbytes((tm, tn), F32)),
        name="in_proj",
    )(h_p, h_s, w_in_t, w_in_t)


def _prefix_sum_rows(sel3_bf16, g):
    g0 = g.astype(BF16)
    r1 = g - g0.astype(F32)
    g1 = r1.astype(BF16)
    g2 = (r1 - g1.astype(F32)).astype(BF16)
    return _dot(sel3_bf16, jnp.concatenate([g0, g1, g2], axis=0))


def _lane_bcast_cols(row, n):
    parts = []
    for c in range(n // V7X_LANES):
        tile = jnp.broadcast_to(row[:, c * V7X_LANES:(c + 1) * V7X_LANES], (V7X_LANES, V7X_LANES))
        parts.append(tile.T)
    return parts[0] if len(parts) == 1 else jnp.concatenate(parts, axis=0)


def _rms_gate_store(o, w, gate, out_ref, rows, cols):
    ms = jnp.mean(o * o, axis=-1, keepdims=True)
    y = o * lax.rsqrt(ms + EPS) * w
    out_ref[rows, cols] = (y * _silu(gate)).astype(out_ref.dtype)


def _ln_gate_store(o, w, gate, out_ref, rows, cols):
    mu = jnp.mean(o, axis=-1, keepdims=True)
    dlt = o - mu
    var = jnp.mean(dlt * dlt, axis=-1, keepdims=True)
    y = dlt * lax.rsqrt(var + EPS) * w
    out_ref[rows, cols] = (y * _silu(gate)).astype(out_ref.dtype)


def _gla_sum_matrices(c):
    levels = c.bit_length() - 1
    assert 1 << levels == c
    i = lax.broadcasted_iota(jnp.int32, (c, c), 0)
    j = lax.broadcasted_iota(jnp.int32, (c, c), 1)
    mats = [j <= i]
    for l in range(levels):
        ref = jnp.bitwise_or(jnp.bitwise_and(i, -(2 << l)), 1 << l)
        mats.append(jnp.logical_and(j > jnp.minimum(i, ref), j <= jnp.maximum(i, ref)))
    mats.append(j > i)
    sel = jnp.concatenate([jnp.where(m, 1.0, 0.0).astype(BF16) for m in mats], axis=0)
    return jnp.concatenate([sel, sel, sel], axis=1)


def _pair_level(c):
    levels = c.bit_length() - 1
    i = lax.broadcasted_iota(jnp.int32, (c, c), 0)
    j = lax.broadcasted_iota(jnp.int32, (c, c), 1)
    x = jnp.bitwise_xor(i, j)
    lvl = jnp.zeros((c, c), jnp.int32)
    for l in range(1, levels):
        lvl = lvl + jnp.where(x >= (1 << l), 1, 0)
    return jnp.where(i > j, lvl, jnp.where(i == j, levels, -1))


def _queries_else_keys(q, k, l):
    c = q.shape[0]
    span = 1 << l
    if span >= 8:
        parts = [(q if (b & 1) else k)[b * span:(b + 1) * span, :] for b in range(c // span)]
        return jnp.concatenate(parts, axis=0)
    row = lax.broadcasted_iota(jnp.int32, q.shape, 0)
    return jnp.where(jnp.bitwise_and(row, span) != 0, q, k)


def _gla_level_scores(q, k, sums):
    c = q.shape[0]
    levels = c.bit_length() - 1
    out = []
    for l in range(levels):
        x = _queries_else_keys(q, k, l) * jnp.exp2(sums[(1 + l) * c:(2 + l) * c, :])
        xb = x.astype(BF16)
        out.append(_dot_nt(xb, xb))
    return out


def _gla_chunk_out(q, k, v, sums, level_scores, pair_level, state):
    c = q.shape[0]
    levels = c.bit_length() - 1
    scores = jnp.where(pair_level == levels, jnp.sum(q * k, axis=-1, keepdims=True), 0.0)
    for l in range(levels):
        scores = jnp.where(pair_level == l, level_scores[l], scores)
    o = _dot((q * jnp.exp2(sums[0:c, :])).astype(BF16), state.astype(BF16))
    return o + _dot(scores.astype(BF16), v)


def _gla_next_state(k, v, sums, state):
    c, dk = k.shape
    levels = c.bit_length() - 1
    k_tail = (k * jnp.exp2(sums[(levels + 1) * c:(levels + 2) * c, :])).astype(BF16)
    decay = _lane_bcast_cols(jnp.exp2(sums[c - 1:c, :]), dk)
    decay_full = jnp.concatenate([decay] * (v.shape[1] // V7X_LANES), axis=1)
    return decay_full * state + _dot_tn(k_tail, v)


def _gla_prompt_kernel(q_ref, k_ref, v_ref, ga_ref, g_ref, nw_ref,
                       o_ref, s_out_ref, s_ref, mats_ref, lvl_ref, *, heads, dk, dv):
    t = pl.program_id(1)

    @pl.when(t == 0)
    def _():
        s_ref[...] = jnp.zeros_like(s_ref)
        mats_ref[...] = _gla_sum_matrices(GLA_CHUNK)
        lvl_ref[...] = _pair_level(GLA_CHUNK)

    ct = q_ref.shape[0]

    def chunk(c, carry):
        rows = pl.ds(pl.multiple_of(c * GLA_CHUNK, GLA_CHUNK), GLA_CHUNK)
        sums_all = _prefix_sum_rows(mats_ref[...], g_ref[rows, :])
        pair_level = lvl_ref[...]
        kcs = [slice(hh * dk, (hh + 1) * dk) for hh in range(heads)]
        vcs = [slice(hh * dv, (hh + 1) * dv) for hh in range(heads)]
        qs = [q_ref[rows, kc].astype(F32) * (dk ** -0.5) for kc in kcs]
        ks = [k_ref[rows, kc].astype(F32) for kc in kcs]
        lvl_scores = [_gla_level_scores(qs[hh], ks[hh], sums_all[:, kcs[hh]]) for hh in range(heads)]
        outs = [_gla_chunk_out(qs[hh], ks[hh], v_ref[rows, vcs[hh]], sums_all[:, kcs[hh]], lvl_scores[hh],
                               pair_level, s_ref[hh]) for hh in range(heads)]
        for hh in range(heads):
            s_ref[hh] = _gla_next_state(ks[hh], v_ref[rows, vcs[hh]], sums_all[:, kcs[hh]], s_ref[hh])
        for hh in range(heads):
            _rms_gate_store(outs[hh], nw_ref[...], ga_ref[rows, vcs[hh]].astype(F32), o_ref, rows, vcs[hh])
        return carry

    lax.fori_loop(0, ct // GLA_CHUNK, chunk, 0, unroll=2)

    @pl.when(t == pl.num_programs(1) - 1)
    def _():
        s_out_ref[0] = s_ref[...]


def _gla_prompt(proj, log2_decay, norm_w, lay, batch, seq):
    heads, dk, dv = lay["gla_heads"], lay["gla_dk"], lay["gla_dv"]
    qk, vw = heads * dk, heads * dv
    ct = min(seq, GLA_STEP_CHUNKS * GLA_CHUNK)
    levels = GLA_CHUNK.bit_length() - 1
    assert seq % ct == 0 and ct % GLA_CHUNK == 0
    nt = seq // ct
    row = lambda b, t: b * nt + t
    kern = functools.partial(_gla_prompt_kernel, heads=heads, dk=dk, dv=dv)
    return pl.pallas_call(
        kern,
        grid=(batch, nt),
        in_specs=[
            pl.BlockSpec((ct, qk), lambda b, t: (row(b, t), lay["qa"] // qk)),
            pl.BlockSpec((ct, qk), lambda b, t: (row(b, t), lay["ka"] // qk)),
            pl.BlockSpec((ct, vw), lambda b, t: (row(b, t), lay["va"] // vw)),
            pl.BlockSpec((ct, vw), lambda b, t: (row(b, t), lay["ga"] // vw)),
            pl.BlockSpec((ct, qk), lambda b, t: (row(b, t), 0)),
            pl.BlockSpec((1, dv), lambda b, t: (0, 0)),
        ],
        out_specs=[
            pl.BlockSpec((ct, vw), lambda b, t: (row(b, t), 0)),
            pl.BlockSpec((1, heads, dk, dv), lambda b, t: (b, 0, 0, 0)),
        ],
        out_shape=[
            jax.ShapeDtypeStruct((batch * seq, vw), BF16),
            jax.ShapeDtypeStruct((batch, heads, dk, dv), F32),
        ],
        scratch_shapes=[
            pltpu.VMEM((heads, dk, dv), F32),
            pltpu.VMEM(((levels + 2) * GLA_CHUNK, 3 * GLA_CHUNK), BF16),
            pltpu.VMEM((GLA_CHUNK, GLA_CHUNK), jnp.int32),
        ],
        compiler_params=_params(
            2, 2 * _nbytes((ct, qk), BF16), 3 * _nbytes((ct, vw), BF16), _nbytes((ct, qk), F32),
            _nbytes((heads, dk, dv), F32), scratch_bytes=_nbytes((heads, dk, dv), F32)),
        name="gla_prompt",
    )(proj, proj, proj, proj, log2_decay, norm_w)


def _rotary(x, cos, sin):
    half = x.shape[1] // 2
    x1, x2 = x[:, :half], x[:, half:]
    return jnp.concatenate([x1 * cos - x2 * sin, x1 * sin + x2 * cos], axis=1)


def _ret_prompt_kernel(q_ref, k_ref, v_ref, gb_ref, cos_ref, sin_ref, lg_ref, nw_ref,
                       o_ref, s_out_ref, s_ref, *, heads, dk, dv):
    t = pl.program_id(1)

    @pl.when(t == 0)
    def _():
        s_ref[...] = jnp.zeros_like(s_ref)

    c = q_ref.shape[0]
    cos, sin = cos_ref[...], sin_ref[...]
    nw = nw_ref[...]
    ri = lax.broadcasted_iota(jnp.int32, (c, c), 0)
    rj = lax.broadcasted_iota(jnp.int32, (c, c), 1)
    dist = (ri - rj).astype(F32)
    rowk = lax.broadcasted_iota(jnp.int32, (c, dk), 0).astype(F32)
    rows = slice(0, c)
    for hh in range(heads):
        lg = lg_ref[hh]
        kc = slice(hh * dk, (hh + 1) * dk)
        vc = slice(hh * dv, (hh + 1) * dv)
        qr = _rotary(q_ref[:, kc].astype(F32), cos, sin)
        kr = _rotary(k_ref[:, kc].astype(F32), cos, sin) * (dk ** -0.5)
        v = v_ref[:, vc]
        lgk = jnp.concatenate([lg] * (dk // V7X_LANES), axis=1)
        lgc = lg[:, :1]
        decay = jnp.exp(jnp.where(ri >= rj, dist * lgc, -jnp.inf))
        scores = _dot_nt(qr.astype(BF16), kr.astype(BF16)) * decay
        state = s_ref[hh]
        o = _dot((qr * jnp.exp((rowk + 1.0) * lgk)).astype(BF16), state.astype(BF16))
        o = o + _dot(scores.astype(BF16), v)
        k_tail = (kr * jnp.exp((float(c - 1) - rowk) * lgk)).astype(BF16)
        lgv = jnp.concatenate([lg] * (dv // V7X_LANES), axis=1)
        s_ref[hh] = jnp.exp(float(c) * lgv) * state + _dot_tn(k_tail, v)
        _ln_gate_store(o, nw, gb_ref[:, vc].astype(F32), o_ref, rows, vc)

    @pl.when(t == pl.num_programs(1) - 1)
    def _():
        s_out_ref[0] = s_ref[...]


def _ret_prompt(proj, cos, sin, log_gamma, norm_w, lay, batch, seq):
    heads, dk, dv = lay["ret_heads"], lay["ret_dk"], lay["ret_dv"]
    qk, vw = heads * dk, heads * dv
    c = min(seq, RET_CHUNK)
    assert seq % c == 0
    nt = seq // c
    half = dk // 2
    row = lambda b, t: b * nt + t
    kern = functools.partial(_ret_prompt_kernel, heads=heads, dk=dk, dv=dv)
    return pl.pallas_call(
        kern,
        grid=(batch, nt),
        in_specs=[
            pl.BlockSpec((c, qk), lambda b, t: (row(b, t), lay["qb"] // qk)),
            pl.BlockSpec((c, qk), lambda b, t: (row(b, t), lay["kb"] // qk)),
            pl.BlockSpec((c, vw), lambda b, t: (row(b, t), lay["vb"] // vw)),
            pl.BlockSpec((c, vw), lambda b, t: (row(b, t), lay["gb"] // vw)),
            pl.BlockSpec((c, half), lambda b, t: (t, 0)),
            pl.BlockSpec((c, half), lambda b, t: (t, 0)),
            pl.BlockSpec((heads, 1, V7X_LANES), lambda b, t: (0, 0, 0)),
            pl.BlockSpec((1, dv), lambda b, t: (0, 0)),
        ],
        out_specs=[
            pl.BlockSpec((c, vw), lambda b, t: (row(b, t), 0)),
            pl.BlockSpec((1, heads, dk, dv), lambda b, t: (b, 0, 0, 0)),
        ],
        out_shape=[
            jax.ShapeDtypeStruct((batch * seq, vw), BF16),
            jax.ShapeDtypeStruct((batch, heads, dk, dv), F32),
        ],
        scratch_shapes=[pltpu.VMEM((heads, dk, dv), F32)],
        compiler_params=_params(
            2, 2 * _nbytes((c, qk), BF16), 3 * _nbytes((c, vw), BF16), 2 * _nbytes((c, half), F32),
            _nbytes((heads, dk, dv), F32), scratch_bytes=_nbytes((heads, dk, dv), F32)),
        name="ret_prompt",
    )(proj, proj, proj, proj, cos, sin, log_gamma, norm_w)


def _decode_tokens(a_cols_fn, k, q, v, s_in_ref, s_out_ref, o_scr):
    tokens, dk = k.shape
    dv = v.shape[1]
    reps = dv // V7X_LANES
    for tt in range(tokens):
        k_cols = jnp.concatenate([_lane_bcast_cols(k[tt:tt + 1, :], dk)] * reps, axis=1)
        q_cols = jnp.concatenate([_lane_bcast_cols(q[tt:tt + 1, :], dk)] * reps, axis=1)
        s_new = a_cols_fn(tt) * s_in_ref[tt, 0] + k_cols * v[tt:tt + 1, :]
        s_out_ref[tt, 0] = s_new
        o_scr[tt:tt + 1, :] = jnp.sum(q_cols * s_new, axis=0, keepdims=True)


def _gla_decode_kernel(q_ref, k_ref, v_ref, ga_ref, g_ref, nw_ref, s_in_ref,
                       o_ref, s_out_ref, o_scr, *, dk, dv):
    a = jnp.exp2(g_ref[...])
    q = q_ref[...].astype(F32) * (dk ** -0.5)
    k = k_ref[...].astype(F32)
    v = v_ref[...].astype(F32)
    reps = dv // V7X_LANES
    a_cols = lambda tt: jnp.concatenate([_lane_bcast_cols(a[tt:tt + 1, :], dk)] * reps, axis=1)
    _decode_tokens(a_cols, k, q, v, s_in_ref, s_out_ref, o_scr)
    tokens = q.shape[0]
    _rms_gate_store(o_scr[...], nw_ref[...], ga_ref[...].astype(F32), o_ref, slice(0, tokens), slice(0, dv))


def _gla_decode(proj, log2_decay, norm_w, state, lay):
    heads, dk, dv = lay["gla_heads"], lay["gla_dk"], lay["gla_dv"]
    n = proj.shape[0]
    tk = DEC_TOKENS
    assert n % tk == 0
    kern = functools.partial(_gla_decode_kernel, dk=dk, dv=dv)
    return pl.pallas_call(
        kern,
        grid=(n // tk, heads),
        in_specs=[
            pl.BlockSpec((tk, dk), lambda i, hh: (i, lay["qa"] // dk + hh)),
            pl.BlockSpec((tk, dk), lambda i, hh: (i, lay["ka"] // dk + hh)),
            pl.BlockSpec((tk, dv), lambda i, hh: (i, lay["va"] // dv + hh)),
            pl.BlockSpec((tk, dv), lambda i, hh: (i, lay["ga"] // dv + hh)),
            pl.BlockSpec((tk, dk), lambda i, hh: (i, hh)),
            pl.BlockSpec((1, dv), lambda i, hh: (0, 0)),
            pl.BlockSpec((tk, 1, dk, dv), lambda i, hh: (i, hh, 0, 0)),
        ],
        out_specs=[
            pl.BlockSpec((tk, dv), lambda i, hh: (i, hh)),
            pl.BlockSpec((tk, 1, dk, dv), lambda i, hh: (i, hh, 0, 0)),
        ],
        out_shape=[
            jax.ShapeDtypeStruct((n, heads * dv), BF16),
            jax.ShapeDtypeStruct(state.shape, state.dtype),
        ],
        scratch_shapes=[pltpu.VMEM((tk, dv), F32)],
        compiler_params=_params(2, 2 * _nbytes((tk, dk, dv), F32), scratch_bytes=_nbytes((tk, dv), F32)),
        name="gla_decode",
    )(proj, proj, proj, proj, log2_decay, norm_w, state)


def _ret_decode_kernel(q_ref, k_ref, v_ref, gb_ref, cos_ref, sin_ref, lg_ref, nw_ref, s_in_ref,
                       o_ref, s_out_ref, o_scr, *, dk, dv):
    cos, sin = cos_ref[0:1, :], sin_ref[0:1, :]
    q = _rotary(q_ref[...].astype(F32), cos, sin)
    k = _rotary(k_ref[...].astype(F32), cos, sin) * (dk ** -0.5)
    v = v_ref[...].astype(F32)
    gamma = jnp.exp(jnp.concatenate([lg_ref[0]] * (dv // V7X_LANES), axis=1))
    _decode_tokens(lambda tt: gamma, k, q, v, s_in_ref, s_out_ref, o_scr)
    tokens = q.shape[0]
    _ln_gate_store(o_scr[...], nw_ref[...], gb_ref[...].astype(F32), o_ref, slice(0, tokens), slice(0, dv))


def _ret_decode(proj, cos, sin, log_gamma, norm_w, state, lay):
    heads, dk, dv = lay["ret_heads"], lay["ret_dk"], lay["ret_dv"]
    n = proj.shape[0]
    tk = DEC_TOKENS
    assert n % tk == 0
    half = dk // 2
    kern = functools.partial(_ret_decode_kernel, dk=dk, dv=dv)
    return pl.pallas_call(
        kern,
        grid=(n // tk, heads),
        in_specs=[
            pl.BlockSpec((tk, dk), lambda i, hh: (i, lay["qb"] // dk + hh)),
            pl.BlockSpec((tk, dk), lambda i, hh: (i, lay["kb"] // dk + hh)),
            pl.BlockSpec((tk, dv), lambda i, hh: (i, lay["vb"] // dv + hh)),
            pl.BlockSpec((tk, dv), lambda i, hh: (i, lay["gb"] // dv + hh)),
            pl.BlockSpec((8, half), lambda i, hh: (0, 0)),
            pl.BlockSpec((8, half), lambda i, hh: (0, 0)),
            pl.BlockSpec((1, 1, V7X_LANES), lambda i, hh: (hh, 0, 0)),
            pl.BlockSpec((1, dv), lambda i, hh: (0, 0)),
            pl.BlockSpec((tk, 1, dk, dv), lambda i, hh: (i, hh, 0, 0)),
        ],
        out_specs=[
            pl.BlockSpec((tk, dv), lambda i, hh: (i, hh)),
            pl.BlockSpec((tk, 1, dk, dv), lambda i, hh: (i, hh, 0, 0)),
        ],
        out_shape=[
            jax.ShapeDtypeStruct((n, heads * dv), BF16),
            jax.ShapeDtypeStruct(state.shape, state.dtype),
        ],
        scratch_shapes=[pltpu.VMEM((tk, dv), F32)],
        compiler_params=_params(2, 2 * _nbytes((tk, dk, dv), F32), scratch_bytes=_nbytes((tk, dv), F32)),
        name="ret_decode",
    )(proj, proj, proj, proj, cos, sin, log_gamma, norm_w, state)


def _rope_table_kernel(cos_ref, sin_ref, *, pos0, half):
    rows, lanes = cos_ref.shape
    pos = (lax.broadcasted_iota(jnp.int32, (rows, lanes), 0) + (pl.program_id(0) * rows + pos0)).astype(F32)
    idx = lax.broadcasted_iota(jnp.int32, (rows, lanes), 1).astype(F32)
    inv = jnp.exp(idx * (-math.log(ROPE_BASE) / half))
    ang = pos * inv
    cos_ref[...] = jnp.cos(ang)
    sin_ref[...] = jnp.sin(ang)


def _rope_tables(n_pos, pos0, half):
    rows = min(n_pos, 256)
    assert n_pos % rows == 0
    kern = functools.partial(_rope_table_kernel, pos0=pos0, half=half)
    spec = pl.BlockSpec((rows, half), lambda i: (i, 0))
    return pl.pallas_call(
        kern,
        grid=(n_pos // rows,),
        in_specs=[],
        out_specs=[spec, spec],
        out_shape=[jax.ShapeDtypeStruct((n_pos, half), F32)] * 2,
        compiler_params=_params(1, 2 * _nbytes((rows, half), F32)),
        name="rope_tables",
    )()


def _merge_kernel(oap_ref, obp_ref, oas_ref, obs_ref, wa_ref, wb_ref, g0p_ref, g1p_ref, g0s_ref, g1s_ref, wnext_ref,
                  mp_ref, ms_ref, wnext_bf_ref, wa_bf, wb_bf):
    wnext_bf_ref[...] = wnext_ref[...].astype(BF16)

    def merged(oa, ob, g0, g1):
        ya = _dot(oa, wa_bf[...])
        yb = _dot(ob, wb_bf[...])
        return _sigmoid(g0.astype(F32)) * ya + _sigmoid(g1.astype(F32)) * yb

    @pl.when(pl.program_id(1) == 0)
    def _():
        wa_bf[...] = wa_ref[...].astype(BF16)
        wb_bf[...] = wb_ref[...].astype(BF16)
        ms_ref[...] = merged(oas_ref[...], obs_ref[...], g0s_ref[...], g1s_ref[...]).astype(ms_ref.dtype)

    mp_ref[...] = merged(oap_ref[...], obp_ref[...], g0p_ref[...], g1p_ref[...]).astype(mp_ref.dtype)


def _slab_specs(w_next, n_steps, step_of):
    kn, dn = w_next.shape
    assert kn % n_steps == 0 and (kn // n_steps) % 16 == 0, (kn, n_steps)
    slab = kn // n_steps
    spec = pl.BlockSpec((slab, dn), lambda j, i: (step_of(j, i), 0))
    return spec, spec, jax.ShapeDtypeStruct((kn, dn), BF16), _nbytes((slab, dn), F32) + _nbytes((slab, dn), BF16)


def _merge(oa_p, ob_p, oa_s, ob_s, wa, wb, proj_p, proj_s, lay, w_next):
    m_p, ka = oa_p.shape
    kb = ob_p.shape[1]
    tail = oa_s.shape[0]
    d = wa.shape[1]
    tm = _row_tile(m_p, ROW_TILE)
    tn = min(d, 1024)
    assert d % tn == 0 and lay["mg"] % tn == 0
    g0 = lay["mg"] // tn
    g1 = (lay["mg"] + d) // tn
    n_m = m_p // tm
    slab_in, slab_out, slab_shape, slab_bytes = _slab_specs(w_next, (d // tn) * n_m, lambda j, i: j * n_m + i)
    return pl.pallas_call(
        _merge_kernel,
        grid=(d // tn, n_m),
        in_specs=[
            pl.BlockSpec((tm, ka), lambda j, i: (i, 0)),
            pl.BlockSpec((tm, kb), lambda j, i: (i, 0)),
            pl.BlockSpec((tail, ka), lambda j, i: (0, 0)),
            pl.BlockSpec((tail, kb), lambda j, i: (0, 0)),
            pl.BlockSpec((ka, tn), lambda j, i: (0, j)),
            pl.BlockSpec((kb, tn), lambda j, i: (0, j)),
            pl.BlockSpec((tm, tn), lambda j, i: (i, g0 + j)),
            pl.BlockSpec((tm, tn), lambda j, i: (i, g1 + j)),
            pl.BlockSpec((tail, tn), lambda j, i: (0, g0 + j)),
            pl.BlockSpec((tail, tn), lambda j, i: (0, g1 + j)),
            slab_in,
        ],
        out_specs=[pl.BlockSpec((tm, tn), lambda j, i: (i, j)), pl.BlockSpec((tail, tn), lambda j, i: (0, j)),
                   slab_out],
        out_shape=[jax.ShapeDtypeStruct((m_p, d), BF16), jax.ShapeDtypeStruct((tail, d), BF16), slab_shape],
        scratch_shapes=[pltpu.VMEM((ka, tn), BF16), pltpu.VMEM((kb, tn), BF16)],
        compiler_params=_params(
            2, _nbytes((tm, ka), BF16), _nbytes((tm, kb), BF16), _nbytes((tail, ka), BF16), _nbytes((tail, kb), BF16),
            _nbytes((ka, tn), F32), _nbytes((kb, tn), F32), 3 * _nbytes((tm, tn), BF16), 3 * _nbytes((tail, tn), BF16),
            slab_bytes,
            scratch_bytes=_nbytes((ka, tn), BF16) + _nbytes((kb, tn), BF16) + 3 * _nbytes((tm, tn), F32)),
        name="merge",
    )(oa_p, ob_p, oa_s, ob_s, wa, wb, proj_p, proj_p, proj_s, proj_s, w_next)


def _proj_res_norm_kernel(ap_ref, as_ref, w_ref, resp_ref, ress_ref, nw_ref, *out_refs, emit_sum):
    n_out = 2 if emit_sum else 1
    outs_p, outs_s = out_refs[:n_out], out_refs[n_out:]
    i = pl.program_id(0)
    k = pl.program_id(1)
    last_k = k == pl.num_programs(1) - 1
    d = w_ref.shape[1]
    col_chunk = min(d, 512)

    def step(a_ref, res_ref, outs):
        acc_ref = outs[0]
        nrow = acc_ref.shape[0]
        row_chunk = min(nrow, 128)
        assert nrow % row_chunk == 0

        @pl.when(k == 0)
        def _():
            acc_ref[...] = res_ref[...]

        a = a_ref[...]
        for c in range(d // col_chunk):
            cs = slice(c * col_chunk, (c + 1) * col_chunk)
            acc_ref[:, cs] += _dot(a, w_ref[:, cs])

        @pl.when(last_k)
        def _():
            def body(c, carry):
                rr = pl.ds(pl.multiple_of(c * row_chunk, row_chunk), row_chunk)
                y = _rmsnorm_rows(acc_ref[rr, :], nw_ref[...])
                if emit_sum:
                    outs[1][rr, :] = y.astype(outs[1].dtype)
                else:
                    acc_ref[rr, :] = y
                return carry

            lax.fori_loop(0, nrow // row_chunk, body, 0)

    step(ap_ref, resp_ref, outs_p)

    @pl.when(i == pl.num_programs(0) - 1)
    def _():
        step(as_ref, ress_ref, outs_s)


def _proj_res_norm(a_p, a_s, w, res_p, res_s, norm_w, emit_sum, tk):
    m_p, kdim = a_p.shape
    tail = a_s.shape[0]
    d = w.shape[1]
    tm = _row_tile(m_p, ROW_TILE)
    tk = min(kdim, tk)
    assert kdim % tk == 0
    p_spec = pl.BlockSpec((tm, d), lambda i, k: (i, 0))
    s_spec = pl.BlockSpec((tail, d), lambda i, k: (0, 0))
    out_specs = [p_spec, s_spec]
    out_shape = [jax.ShapeDtypeStruct((m_p, d), F32), jax.ShapeDtypeStruct((tail, d), F32)]
    assert w.dtype == BF16
    blocks = [_nbytes((tm, tk), BF16), _nbytes((tail, tk), BF16), _nbytes((tk, d), BF16),
              2 * _nbytes((tm, d), F32), 2 * _nbytes((tail, d), F32)]
    if emit_sum:
        out_specs = [p_spec, p_spec, s_spec, s_spec]
        out_shape = [out_shape[0], jax.ShapeDtypeStruct((m_p, d), BF16),
                     out_shape[1], jax.ShapeDtypeStruct((tail, d), BF16)]
        blocks += [_nbytes((tm, d), BF16), _nbytes((tail, d), BF16)]
    return pl.pallas_call(
        functools.partial(_proj_res_norm_kernel, emit_sum=emit_sum),
        grid=(m_p // tm, kdim // tk),
        in_specs=[
            pl.BlockSpec((tm, tk), lambda i, k: (i, k)),
            pl.BlockSpec((tail, tk), lambda i, k: (0, k)),
            pl.BlockSpec((tk, d), lambda i, k: (k, 0)),
            p_spec,
            s_spec,
            pl.BlockSpec((1, d), lambda i, k: (0, 0)),
        ],
        out_specs=out_specs,
        out_shape=out_shape,
        compiler_params=_params(2, *blocks),
        name="proj_res_norm",
    )(a_p, a_s, w, res_p, res_s, norm_w.reshape(1, d))


def _swiglu_kernel(hp_ref, hs_ref, wg_ref, wu_ref, wnext_ref, op_ref, os_ref, wnext_bf_ref, wg_bf, wu_bf):
    wnext_bf_ref[...] = wnext_ref[...].astype(BF16)

    tn = wg_bf.shape[1]
    col_chunk = min(tn, 256)

    def act(h_ref, o_ref):
        h = h_ref[...]
        for c in range(tn // col_chunk):
            cs = slice(c * col_chunk, (c + 1) * col_chunk)
            a = _dot(h, wg_bf[:, cs])
            b = _dot(h, wu_bf[:, cs])
            o_ref[:, cs] = (_silu(a) * b).astype(o_ref.dtype)

    @pl.when(pl.program_id(1) == 0)
    def _():
        wg_bf[...] = wg_ref[...].astype(BF16)
        wu_bf[...] = wu_ref[...].astype(BF16)
        act(hs_ref, os_ref)

    act(hp_ref, op_ref)


def _swiglu(h_p, h_s, wg, wu, w_next):
    m_p, d = h_p.shape
    tail = h_s.shape[0]
    f = wg.shape[1]
    tm = _row_tile(m_p, ROW_TILE_WIDE)
    tn = 512 if f % 512 == 0 else 256
    assert f % tn == 0
    n_m = m_p // tm
    slab_in, slab_out, slab_shape, slab_bytes = _slab_specs(w_next, (f // tn) * n_m, lambda j, i: j * n_m + i)
    return pl.pallas_call(
        _swiglu_kernel,
        grid=(f // tn, n_m),
        in_specs=[
            pl.BlockSpec((tm, d), lambda j, i: (i, 0)),
            pl.BlockSpec((tail, d), lambda j, i: (0, 0)),
            pl.BlockSpec((d, tn), lambda j, i: (0, j)),
            pl.BlockSpec((d, tn), lambda j, i: (0, j)),
            slab_in,
        ],
        out_specs=[pl.BlockSpec((tm, tn), lambda j, i: (i, j)), pl.BlockSpec((tail, tn), lambda j, i: (0, j)),
                   slab_out],
        out_shape=[jax.ShapeDtypeStruct((m_p, f), BF16), jax.ShapeDtypeStruct((tail, f), BF16), slab_shape],
        scratch_shapes=[pltpu.VMEM((d, tn), BF16), pltpu.VMEM((d, tn), BF16)],
        compiler_params=_params(
            2, _nbytes((tm, d), BF16), _nbytes((tail, d), BF16), 2 * _nbytes((d, tn), F32),
            _nbytes((tm, tn), BF16), _nbytes((tail, tn), BF16), slab_bytes,
            scratch_bytes=2 * _nbytes((d, tn), BF16) + 3 * _nbytes((tm, tn), F32)),
        name="swiglu",
    )(h_p, h_s, wg, wu, w_next)


def _layout(d_model, in_width, state_gla, state_ret, gate_rank):
    _, _, gh, gdk, gdv = state_gla.shape
    _, _, rh, rdk, rdv = state_ret.shape
    gqk, gv, rqk, rv = gh * gdk, gh * gdv, rh * rdk, rh * rdv
    lay = dict(gla_heads=gh, gla_dk=gdk, gla_dv=gdv, ret_heads=rh, ret_dk=rdk, ret_dv=rdv, rank=gate_rank)
    off = 0
    for name, width in (("qa", gqk), ("ka", gqk), ("va", gv), ("ga", gv), ("qb", rqk), ("kb", rqk),
                        ("vb", rv), ("gb", rv), ("mg", 2 * d_model)):
        lay[name] = off
        off += width
    lay["out_cols"] = off
    lay["plain_cols"] = 2 * gqk + gv
    lay["gd_src"] = lay["plain_cols"]
    assert lay["gd_src"] % V7X_LANES == 0 and gate_rank <= V7X_LANES
    assert in_width == off + gate_rank
    return lay


def _layer(x_p, x_s, st_gla, st_ret, wts, lay, tables, final_norm):
    (norm_mix, w_in, w_gate_up, b_gate, gla_norm_w, w_gla_up, ret_norm_w, w_ret_up, w_out, norm_ffn,
     w_ffn_gate, w_ffn_up, w_ffn_down) = wts
    batch, seq, d = x_p.shape
    rank = lay["rank"]
    gqk = lay["gla_heads"] * lay["gla_dk"]
    wup_pad = jnp.zeros((V7X_LANES, gqk), F32).at[:rank].set(w_gate_up)
    bup = b_gate.reshape(1, gqk)
    gnw = gla_norm_w.reshape(1, -1)
    rnw = ret_norm_w.reshape(1, -1)
    tn = 1024 if (lay["out_cols"] % 1024 == 0 and lay["plain_cols"] % 1024 == 0) else 512
    cos_p, sin_p, cos_s, sin_s, log_gamma = tables
    xp = x_p.reshape(batch * seq, d)
    xs = x_s.reshape(-1, d)
    w_in_t = w_in.T

    h_p, h_s, g_p, g_s = _rmsnorm_gate(xp, xs, norm_mix, w_in_t, lay["gd_src"], wup_pad, bup)
    proj_p, proj_s = _in_proj(h_p, h_s, w_in_t, lay["plain_cols"], rank, lay["out_cols"], tn)
    oa_p, sa_p = _gla_prompt(proj_p, g_p, gnw, lay, batch, seq)
    ob_p, sb_p = _ret_prompt(proj_p, cos_p, sin_p, log_gamma, rnw, lay, batch, seq)
    oa_s, sa_s = _gla_decode(proj_s, g_s, gnw, st_gla, lay)
    ob_s, sb_s = _ret_decode(proj_s, cos_s, sin_s, log_gamma, rnw, st_ret, lay)
    m_p, m_s, w_out_bf = _merge(oa_p, ob_p, oa_s, ob_s, w_gla_up, w_ret_up, proj_p, proj_s, lay, w_out)
    x1_p, h2_p, x1_s, h2_s = _proj_res_norm(m_p, m_s, w_out_bf, xp, xs, norm_ffn, True, 1024)
    act_p, act_s, w_down_bf = _swiglu(h2_p, h2_s, w_ffn_gate, w_ffn_up, w_ffn_down)
    y_p, y_s = _proj_res_norm(act_p, act_s, w_down_bf, x1_p, x1_s, final_norm, False, 1408)
    return (y_p, sa_p, sb_p), (y_s, sa_s, sb_s)


def kernel(x_prompt, x_sample, state_gla, state_ret, norm_mix, w_in, w_gla_gate_up, b_gla_gate, gla_norm_w,
           w_gla_up, ret_norm_w, w_ret_up, w_out, norm_ffn, w_ffn_gate, w_ffn_up, w_ffn_down, norm_final):
    depth = w_in.shape[0]
    assert depth == 1, "single-layer trunk"
    batch, seq, d = x_prompt.shape
    lay = _layout(d, w_in.shape[-1], state_gla, state_ret, w_gla_gate_up.shape[1])
    rh, rdk = lay["ret_heads"], lay["ret_dk"]
    half = rdk // 2
    assert half == V7X_LANES
    cos_p, sin_p = _rope_tables(seq, 0, half)
    cos_s, sin_s = _rope_tables(8, PAST_LEN, half)
    lg = jnp.log1p(-jnp.exp(jnp.linspace(math.log(1.0 / 32), math.log(1.0 / 512), rh))).astype(F32)
    log_gamma = jnp.broadcast_to(lg[:, None, None], (rh, 1, V7X_LANES))
    tables = (cos_p, sin_p, cos_s, sin_s, log_gamma)

    wts = (norm_mix[0], w_in[0], w_gla_gate_up[0], b_gla_gate[0], gla_norm_w[0], w_gla_up[0], ret_norm_w[0],
           w_ret_up[0], w_out[0], norm_ffn[0], w_ffn_gate[0], w_ffn_up[0], w_ffn_down[0])
    (y_p, ga_p, re_p), (y_s, ga_s, re_s) = _layer(
        x_prompt, x_sample, state_gla[0], state_ret[0], wts, lay, tables, norm_final)

    sd = state_gla.dtype
    return (y_p.reshape(batch, seq, d), y_s.reshape(x_sample.shape),
            ga_p[None].astype(sd), re_p[None].astype(state_ret.dtype),
            ga_s[None].astype(sd), re_s[None].astype(state_ret.dtype))
```

```python
import functools
import math

import numpy as np
import jax
import jax.numpy as jnp
from jax import lax
from jax.experimental import pallas as pl
from jax.experimental.pallas import tpu as pltpu

EPS = 1e-6
ROPE_BASE = 10000.0
GLA_GATE_NORM = 16.0
PAST_LEN = 16384

V7X_LANES = 128
V7X_VMEM_REQUEST_CAP = 60000 * 1024
COMPILER_SCRATCH_BYTES = 12 * 1024 * 1024

GLA_CHUNK = 64
GLA_STEP_CHUNKS = 8
LOG2_E = 1.4426950408889634
RET_CHUNK = 128
RET_STEP_CHUNKS = 4
DEC_TOKENS = 32
ROW_TILE = 1024
ROW_TILE_WIDE = 2048

BF16 = jnp.bfloat16
F32 = jnp.float32


def _params(n_axes, *block_bytes, scratch_bytes=0, claim_all=False):
    need = 2 * sum(block_bytes) + scratch_bytes + COMPILER_SCRATCH_BYTES
    if claim_all:
        need = V7X_VMEM_REQUEST_CAP
    return pltpu.CompilerParams(
        dimension_semantics=("arbitrary",) * n_axes,
        vmem_limit_bytes=int(min(V7X_VMEM_REQUEST_CAP, need)),
    )


def _nbytes(shape, dtype):
    return int(np.prod(shape)) * jnp.dtype(dtype).itemsize


def _sigmoid(x):
    return 1.0 / (1.0 + jnp.exp(-x))


def _silu(x):
    return x * _sigmoid(x)


def _log_sigmoid(x):
    return jnp.minimum(x, 0.0) - jnp.log(1.0 + jnp.exp(-jnp.abs(x)))


def _dot(a, b):
    return jnp.dot(a, b, preferred_element_type=F32)


def _dot_nt(a, b):
    return lax.dot_general(a, b, (((1,), (1,)), ((), ())), preferred_element_type=F32)


def _dot_tn(a, b):
    return lax.dot_general(a, b, (((0,), (0,)), ((), ())), preferred_element_type=F32)


def _row_tile(m, want):
    t = min(m, want)
    assert m % t == 0, (m, t)
    return t


def _rmsnorm_rows(x, w):
    ms = jnp.mean(x * x, axis=-1, keepdims=True)
    return x * lax.rsqrt(ms + EPS) * w


def _rmsnorm_kernel(xp_ref, xs_ref, w_ref, wgd_ref, wup_ref, bup_ref, hp_ref, hs_ref, gp_ref, gs_ref, wgd_bf, wup_bf):
    @pl.when(pl.program_id(0) == 0)
    def _():
        wgd_bf[...] = wgd_ref[...].astype(BF16)
        wup_bf[...] = wup_ref[...].astype(BF16)

    def rows(x_ref, h_ref, g_ref):
        h = _rmsnorm_rows(x_ref[...], w_ref[...]).astype(h_ref.dtype)
        h_ref[...] = h
        gd = _dot_nt(h, wgd_bf[...])
        x = _dot(gd.astype(BF16), wup_bf[...]) + bup_ref[...]
        g_ref[...] = _log_sigmoid(x) * (LOG2_E / GLA_GATE_NORM)

    rows(xp_ref, hp_ref, gp_ref)

    @pl.when(pl.program_id(0) == 0)
    def _():
        rows(xs_ref, hs_ref, gs_ref)


def _rmsnorm_gate(x_p, x_s, w, w_in_t, gate_row0, wup_pad, bup):
    m_p, d = x_p.shape
    tail = x_s.shape[0]
    gw = wup_pad.shape[1]
    tm = _row_tile(m_p, 512)
    assert gate_row0 % V7X_LANES == 0
    return pl.pallas_call(
        _rmsnorm_kernel,
        grid=(m_p // tm,),
        in_specs=[
            pl.BlockSpec((tm, d), lambda i: (i, 0)),
            pl.BlockSpec((tail, d), lambda i: (0, 0)),
            pl.BlockSpec((1, d), lambda i: (0, 0)),
            pl.BlockSpec((V7X_LANES, d), lambda i: (gate_row0 // V7X_LANES, 0)),
            pl.BlockSpec((V7X_LANES, gw), lambda i: (0, 0)),
            pl.BlockSpec((1, gw), lambda i: (0, 0)),
        ],
        out_specs=[
            pl.BlockSpec((tm, d), lambda i: (i, 0)), pl.BlockSpec((tail, d), lambda i: (0, 0)),
            pl.BlockSpec((tm, gw), lambda i: (i, 0)), pl.BlockSpec((tail, gw), lambda i: (0, 0)),
        ],
        out_shape=[
            jax.ShapeDtypeStruct((m_p, d), BF16), jax.ShapeDtypeStruct((tail, d), BF16),
            jax.ShapeDtypeStruct((m_p, gw), F32), jax.ShapeDtypeStruct((tail, gw), F32),
        ],
        scratch_shapes=[pltpu.VMEM((V7X_LANES, d), BF16), pltpu.VMEM((V7X_LANES, gw), BF16)],
        compiler_params=_params(
            1, _nbytes((tm, d), F32), _nbytes((tail, d), F32), _nbytes((tm, d), BF16), _nbytes((V7X_LANES, d), F32),
            _nbytes((tm, gw), F32), _nbytes((tail, gw), F32), claim_all=True),
        name="rmsnorm_gate",
    )(x_p, x_s, w.reshape(1, d), w_in_t, wup_pad, bup)


def _in_proj_kernel(hp_ref, hs_ref, wm_ref, wn_ref, op_ref, os_ref, wbf_ref, *, n_plain, shift):
    j = pl.program_id(0)
    i = pl.program_id(1)
    tn = wbf_ref.shape[0]

    @pl.when(jnp.logical_and(i == 0, j < n_plain))
    def _():
        wbf_ref[...] = wm_ref[...].astype(BF16)

    @pl.when(jnp.logical_and(i == 0, j >= n_plain))
    def _():
        wbf_ref[0:tn - shift, :] = wm_ref[shift:tn, :].astype(BF16)
        wbf_ref[tn - shift:tn, :] = wn_ref[...].astype(BF16)

    @pl.when(i == 0)
    def _():
        os_ref[...] = _dot_nt(hs_ref[...], wbf_ref[...]).astype(os_ref.dtype)

    op_ref[...] = _dot_nt(hp_ref[...], wbf_ref[...]).astype(op_ref.dtype)


def _in_proj(h_p, h_s, w_in_t, plain_cols, shift, out_cols, tn):
    m_p, d = h_p.shape
    tail = h_s.shape[0]
    tm = _row_tile(m_p, ROW_TILE_WIDE)
    assert plain_cols % tn == 0 and out_cols % tn == 0 and tn % shift == 0 and shift % 8 == 0
    n_plain = plain_cols // tn
    kern = functools.partial(_in_proj_kernel, n_plain=n_plain, shift=shift)
    return pl.pallas_call(
        kern,
        grid=(out_cols // tn, m_p // tm),
        in_specs=[
            pl.BlockSpec((tm, d), lambda j, i: (i, 0)),
            pl.BlockSpec((tail, d), lambda j, i: (0, 0)),
            pl.BlockSpec((tn, d), lambda j, i: (j, 0)),
            pl.BlockSpec((shift, d), lambda j, i: ((j + 1) * (tn // shift), 0)),
        ],
        out_specs=[pl.BlockSpec((tm, tn), lambda j, i: (i, j)), pl.BlockSpec((tail, tn), lambda j, i: (0, j))],
        out_shape=[jax.ShapeDtypeStruct((m_p, out_cols), BF16), jax.ShapeDtypeStruct((tail, out_cols), BF16)],
        scratch_shapes=[pltpu.VMEM((tn, d), BF16)],
        compiler_params=_params(
            2, _nbytes((tm, d), BF16), _nbytes((tail, d), BF16), _nbytes((tn, d), F32), _nbytes((shift, d), F32),
            _nbytes((tm, tn), BF16), _nbytes((tail, tn), BF16),
            scratch_bytes=_nbytes((tn, d), BF16) + _nbytes((tm, tn), F32)),
        name="in_proj",
    )(h_p, h_s, w_in_t, w_in_t)


def _prefix_sum_rows(sel3_bf16, g):
    g0 = g.astype(BF16)
    r1 = g - g0.astype(F32)
    g1 = r1.astype(BF16)
    g2 = (r1 - g1.astype(F32)).astype(BF16)
    return _dot(sel3_bf16, jnp.concatenate([g0, g1, g2], axis=0))


def _lane_bcast_cols(row, n):
    parts = []
    for c in range(n // V7X_LANES):
        tile = jnp.broadcast_to(row[:, c * V7X_LANES:(c + 1) * V7X_LANES], (V7X_LANES, V7X_LANES))
        parts.append(tile.T)
    return parts[0] if len(parts) == 1 else jnp.concatenate(parts, axis=0)


def _rms_gate_store(o, w, gate, out_ref, rows, cols):
    ms = jnp.mean(o * o, axis=-1, keepdims=True)
    y = o * lax.rsqrt(ms + EPS) * w
    out_ref[rows, cols] = (y * _silu(gate)).astype(out_ref.dtype)


def _ln_gate_store(o, w, gate, out_ref, rows, cols):
    mu = jnp.mean(o, axis=-1, keepdims=True)
    dlt = o - mu
    var = jnp.mean(dlt * dlt, axis=-1, keepdims=True)
    y = dlt * lax.rsqrt(var + EPS) * w
    out_ref[rows, cols] = (y * _silu(gate)).astype(out_ref.dtype)


def _gla_sum_matrices(c):
    levels = c.bit_length() - 1
    assert 1 << levels == c
    i = lax.broadcasted_iota(jnp.int32, (c, c), 0)
    j = lax.broadcasted_iota(jnp.int32, (c, c), 1)
    mats = [j <= i]
    for l in range(levels):
        ref = jnp.bitwise_or(jnp.bitwise_and(i, -(2 << l)), 1 << l)
        mats.append(jnp.logical_and(j > jnp.minimum(i, ref), j <= jnp.maximum(i, ref)))
    mats.append(j > i)
    sel = jnp.concatenate([jnp.where(m, 1.0, 0.0).astype(BF16) for m in mats], axis=0)
    return jnp.concatenate([sel, sel, sel], axis=1)


def _pair_level(c):
    levels = c.bit_length() - 1
    i = lax.broadcasted_iota(jnp.int32, (c, c), 0)
    j = lax.broadcasted_iota(jnp.int32, (c, c), 1)
    x = jnp.bitwise_xor(i, j)
    lvl = jnp.zeros((c, c), jnp.int32)
    for l in range(1, levels):
        lvl = lvl + jnp.where(x >= (1 << l), 1, 0)
    return jnp.where(i > j, lvl, jnp.where(i == j, levels, -1))


def _queries_else_keys(q, k, l):
    c = q.shape[0]
    span = 1 << l
    if span >= 8:
        parts = [(q if (b & 1) else k)[b * span:(b + 1) * span, :] for b in range(c // span)]
        return jnp.concatenate(parts, axis=0)
    row = lax.broadcasted_iota(jnp.int32, q.shape, 0)
    return jnp.where(jnp.bitwise_and(row, span) != 0, q, k)


def _gla_level_scores(q, k, sums):
    c = q.shape[0]
    levels = c.bit_length() - 1
    out = []
    for l in range(levels):
        x = _queries_else_keys(q, k, l) * jnp.exp2(sums[(1 + l) * c:(2 + l) * c, :])
        xb = x.astype(BF16)
        out.append(_dot_nt(xb, xb))
    return out


def _gla_chunk_out(q, k, v, sums, level_scores, pair_level, state):
    c = q.shape[0]
    levels = c.bit_length() - 1
    scores = jnp.where(pair_level == levels, jnp.sum(q * k, axis=-1, keepdims=True), 0.0)
    for l in range(levels):
        scores = jnp.where(pair_level == l, level_scores[l], scores)
    o = _dot((q * jnp.exp2(sums[0:c, :])).astype(BF16), state.astype(BF16))
    return o + _dot(scores.astype(BF16), v)


def _gla_next_state(k, v, sums, state):
    c, dk = k.shape
    levels = c.bit_length() - 1
    k_tail = (k * jnp.exp2(sums[(levels + 1) * c:(levels + 2) * c, :])).astype(BF16)
    decay = _lane_bcast_cols(jnp.exp2(sums[c - 1:c, :]), dk)
    decay_full = jnp.concatenate([decay] * (v.shape[1] // V7X_LANES), axis=1)
    return decay_full * state + _dot_tn(k_tail, v)


def _gla_prompt_kernel(q_ref, k_ref, v_ref, ga_ref, g_ref, nw_ref,
                       o_ref, s_out_ref, s_ref, mats_ref, lvl_ref, *, heads, dk, dv):
    t = pl.program_id(1)

    @pl.when(t == 0)
    def _():
        s_ref[...] = jnp.zeros_like(s_ref)
        mats_ref[...] = _gla_sum_matrices(GLA_CHUNK)
        lvl_ref[...] = _pair_level(GLA_CHUNK)

    ct = q_ref.shape[0]

    def chunk(c, carry):
        rows = pl.ds(pl.multiple_of(c * GLA_CHUNK, GLA_CHUNK), GLA_CHUNK)
        sums_all = _prefix_sum_rows(mats_ref[...], g_ref[rows, :])
        pair_level = lvl_ref[...]
        kcs = [slice(hh * dk, (hh + 1) * dk) for hh in range(heads)]
        vcs = [slice(hh * dv, (hh + 1) * dv) for hh in range(heads)]
        qs = [q_ref[rows, kc].astype(F32) * (dk ** -0.5) for kc in kcs]
        ks = [k_ref[rows, kc].astype(F32) for kc in kcs]
        lvl_scores = [_gla_level_scores(qs[hh], ks[hh], sums_all[:, kcs[hh]]) for hh in range(heads)]
        outs = [_gla_chunk_out(qs[hh], ks[hh], v_ref[rows, vcs[hh]], sums_all[:, kcs[hh]], lvl_scores[hh],
                               pair_level, s_ref[hh]) for hh in range(heads)]
        for hh in range(heads):
            s_ref[hh] = _gla_next_state(ks[hh], v_ref[rows, vcs[hh]], sums_all[:, kcs[hh]], s_ref[hh])
        for hh in range(heads):
            _rms_gate_store(outs[hh], nw_ref[...], ga_ref[rows, vcs[hh]].astype(F32), o_ref, rows, vcs[hh])
        return carry

    lax.fori_loop(0, ct // GLA_CHUNK, chunk, 0, unroll=2)

    @pl.when(t == pl.num_programs(1) - 1)
    def _():
        s_out_ref[0] = s_ref[...]


def _gla_prompt(proj, log2_decay, norm_w, lay, batch, seq):
    heads, dk, dv = lay["gla_heads"], lay["gla_dk"], lay["gla_dv"]
    qk, vw = heads * dk, heads * dv
    ct = min(seq, GLA_STEP_CHUNKS * GLA_CHUNK)
    levels = GLA_CHUNK.bit_length() - 1
    assert seq % ct == 0 and ct % GLA_CHUNK == 0
    nt = seq // ct
    row = lambda b, t: b * nt + t
    kern = functools.partial(_gla_prompt_kernel, heads=heads, dk=dk, dv=dv)
    return pl.pallas_call(
        kern,
        grid=(batch, nt),
        in_specs=[
            pl.BlockSpec((ct, qk), lambda b, t: (row(b, t), lay["qa"] // qk)),
            pl.BlockSpec((ct, qk), lambda b, t: (row(b, t), lay["ka"] // qk)),
            pl.BlockSpec((ct, vw), lambda b, t: (row(b, t), lay["va"] // vw)),
            pl.BlockSpec((ct, vw), lambda b, t: (row(b, t), lay["ga"] // vw)),
            pl.BlockSpec((ct, qk), lambda b, t: (row(b, t), 0)),
            pl.BlockSpec((1, dv), lambda b, t: (0, 0)),
        ],
        out_specs=[
            pl.BlockSpec((ct, vw), lambda b, t: (row(b, t), 0)),
            pl.BlockSpec((1, heads, dk, dv), lambda b, t: (b, 0, 0, 0)),
        ],
        out_shape=[
            jax.ShapeDtypeStruct((batch * seq, vw), BF16),
            jax.ShapeDtypeStruct((batch, heads, dk, dv), F32),
        ],
        scratch_shapes=[
            pltpu.VMEM((heads, dk, dv), F32),
            pltpu.VMEM(((levels + 2) * GLA_CHUNK, 3 * GLA_CHUNK), BF16),
            pltpu.VMEM((GLA_CHUNK, GLA_CHUNK), jnp.int32),
        ],
        compiler_params=_params(
            2, 2 * _nbytes((ct, qk), BF16), 3 * _nbytes((ct, vw), BF16), _nbytes((ct, qk), F32),
            _nbytes((heads, dk, dv), F32), scratch_bytes=_nbytes((heads, dk, dv), F32), claim_all=True),
        name="gla_prompt",
    )(proj, proj, proj, proj, log2_decay, norm_w)


def _rotary(x, cos, sin):
    half = x.shape[1] // 2
    x1, x2 = x[:, :half], x[:, half:]
    return jnp.concatenate([x1 * cos - x2 * sin, x1 * sin + x2 * cos], axis=1)


def _ret_prompt_kernel(q_ref, k_ref, v_ref, gb_ref, cos_ref, sin_ref, lg_ref, nw_ref,
                       o_ref, s_out_ref, s_ref, dmat_ref, qdec_ref, kdec_ref, *, heads, dk, dv, c):
    t = pl.program_id(1)

    @pl.when(t == 0)
    def _():
        s_ref[...] = jnp.zeros_like(s_ref)
        ri = lax.broadcasted_iota(jnp.int32, (c, c), 0)
        rj = lax.broadcasted_iota(jnp.int32, (c, c), 1)
        dist = (ri - rj).astype(F32)
        rowl = lax.broadcasted_iota(jnp.int32, (c, V7X_LANES), 0).astype(F32)
        for hh in range(heads):
            lg = lg_ref[hh]
            dmat_ref[hh] = jnp.exp(jnp.where(ri >= rj, dist * lg[:, :1], -jnp.inf))
            qdec_ref[hh] = jnp.exp((rowl + 1.0) * lg)
            kdec_ref[hh] = jnp.exp((float(c - 1) - rowl) * lg)

    ct = q_ref.shape[0]
    kcs = [slice(hh * dk, (hh + 1) * dk) for hh in range(heads)]
    vcs = [slice(hh * dv, (hh + 1) * dv) for hh in range(heads)]

    def chunk(ci, carry):
        rows = pl.ds(pl.multiple_of(ci * c, c), c)
        cos, sin = cos_ref[rows, :], sin_ref[rows, :]
        qrs = [_rotary(q_ref[rows, kc].astype(F32), cos, sin).astype(BF16) for kc in kcs]
        krs = [_rotary(k_ref[rows, kc].astype(F32), cos, sin) * (dk ** -0.5) for kc in kcs]
        scores = [_dot_nt(qrs[hh], krs[hh].astype(BF16)) * dmat_ref[hh] for hh in range(heads)]
        outs = []
        for hh in range(heads):
            qdec = jnp.concatenate([qdec_ref[hh]] * (dv // V7X_LANES), axis=1)
            o = qdec * _dot(qrs[hh], s_ref[hh].astype(BF16))
            outs.append(o + _dot(scores[hh].astype(BF16), v_ref[rows, vcs[hh]]))
        for hh in range(heads):
            kdec = jnp.concatenate([kdec_ref[hh]] * (dk // V7X_LANES), axis=1)
            k_tail = (krs[hh] * kdec).astype(BF16)
            lgv = jnp.concatenate([lg_ref[hh]] * (dv // V7X_LANES), axis=1)
            s_ref[hh] = jnp.exp(float(c) * lgv) * s_ref[hh] + _dot_tn(k_tail, v_ref[rows, vcs[hh]])
        for hh in range(heads):
            _ln_gate_store(outs[hh], nw_ref[...], gb_ref[rows, vcs[hh]].astype(F32), o_ref, rows, vcs[hh])
        return carry

    lax.fori_loop(0, ct // c, chunk, 0)

    @pl.when(t == pl.num_programs(1) - 1)
    def _():
        s_out_ref[0] = s_ref[...]


def _ret_prompt(proj, cos, sin, log_gamma, norm_w, lay, batch, seq):
    heads, dk, dv = lay["ret_heads"], lay["ret_dk"], lay["ret_dv"]
    qk, vw = heads * dk, heads * dv
    c = min(seq, RET_CHUNK)
    ct = min(seq, RET_STEP_CHUNKS * c)
    assert seq % ct == 0 and ct % c == 0
    nt = seq // ct
    half = dk // 2
    row = lambda b, t: b * nt + t
    kern = functools.partial(_ret_prompt_kernel, heads=heads, dk=dk, dv=dv, c=c)
    return pl.pallas_call(
        kern,
        grid=(batch, nt),
        in_specs=[
            pl.BlockSpec((ct, qk), lambda b, t: (row(b, t), lay["qb"] // qk)),
            pl.BlockSpec((ct, qk), lambda b, t: (row(b, t), lay["kb"] // qk)),
            pl.BlockSpec((ct, vw), lambda b, t: (row(b, t), lay["vb"] // vw)),
            pl.BlockSpec((ct, vw), lambda b, t: (row(b, t), lay["gb"] // vw)),
            pl.BlockSpec((ct, half), lambda b, t: (t, 0)),
            pl.BlockSpec((ct, half), lambda b, t: (t, 0)),
            pl.BlockSpec((heads, 1, V7X_LANES), lambda b, t: (0, 0, 0)),
            pl.BlockSpec((1, dv), lambda b, t: (0, 0)),
        ],
        out_specs=[
            pl.BlockSpec((ct, vw), lambda b, t: (row(b, t), 0)),
            pl.BlockSpec((1, heads, dk, dv), lambda b, t: (b, 0, 0, 0)),
        ],
        out_shape=[
            jax.ShapeDtypeStruct((batch * seq, vw), BF16),
            jax.ShapeDtypeStruct((batch, heads, dk, dv), F32),
        ],
        scratch_shapes=[
            pltpu.VMEM((heads, dk, dv), F32),
            pltpu.VMEM((heads, c, c), F32),
            pltpu.VMEM((heads, c, V7X_LANES), F32),
            pltpu.VMEM((heads, c, V7X_LANES), F32),
        ],
        compiler_params=_params(
            2, 2 * _nbytes((ct, qk), BF16), 3 * _nbytes((ct, vw), BF16), 2 * _nbytes((ct, half), F32),
            _nbytes((heads, dk, dv), F32), scratch_bytes=_nbytes((heads, dk, dv), F32), claim_all=True),
        name="ret_prompt",
    )(proj, proj, proj, proj, cos, sin, log_gamma, norm_w)


def _decode_tokens(a_cols_fn, k, q, v, s_in_ref, s_out_ref, o_scr):
    tokens, dk = k.shape
    dv = v.shape[1]
    reps = dv // V7X_LANES
    for tt in range(tokens):
        k_cols = jnp.concatenate([_lane_bcast_cols(k[tt:tt + 1, :], dk)] * reps, axis=1)
        q_cols = jnp.concatenate([_lane_bcast_cols(q[tt:tt + 1, :], dk)] * reps, axis=1)
        s_new = a_cols_fn(tt) * s_in_ref[tt, 0] + k_cols * v[tt:tt + 1, :]
        s_out_ref[tt, 0] = s_new
        o_scr[tt:tt + 1, :] = jnp.sum(q_cols * s_new, axis=0, keepdims=True)


def _gla_decode_kernel(q_ref, k_ref, v_ref, ga_ref, g_ref, nw_ref, s_in_ref,
                       o_ref, s_out_ref, o_scr, *, dk, dv):
    a = jnp.exp2(g_ref[...])
    q = q_ref[...].astype(F32) * (dk ** -0.5)
    k = k_ref[...].astype(F32)
    v = v_ref[...].astype(F32)
    reps = dv // V7X_LANES
    a_cols = lambda tt: jnp.concatenate([_lane_bcast_cols(a[tt:tt + 1, :], dk)] * reps, axis=1)
    _decode_tokens(a_cols, k, q, v, s_in_ref, s_out_ref, o_scr)
    tokens = q.shape[0]
    _rms_gate_store(o_scr[...], nw_ref[...], ga_ref[...].astype(F32), o_ref, slice(0, tokens), slice(0, dv))


def _gla_decode(proj, log2_decay, norm_w, state, lay):
    heads, dk, dv = lay["gla_heads"], lay["gla_dk"], lay["gla_dv"]
    n = proj.shape[0]
    tk = min(DEC_TOKENS, n)
    assert n % tk == 0
    kern = functools.partial(_gla_decode_kernel, dk=dk, dv=dv)
    return pl.pallas_call(
        kern,
        grid=(n // tk, heads),
        in_specs=[
            pl.BlockSpec((tk, dk), lambda i, hh: (i, lay["qa"] // dk + hh)),
            pl.BlockSpec((tk, dk), lambda i, hh: (i, lay["ka"] // dk + hh)),
            pl.BlockSpec((tk, dv), lambda i, hh: (i, lay["va"] // dv + hh)),
            pl.BlockSpec((tk, dv), lambda i, hh: (i, lay["ga"] // dv + hh)),
            pl.BlockSpec((tk, dk), lambda i, hh: (i, hh)),
            pl.BlockSpec((1, dv), lambda i, hh: (0, 0)),
            pl.BlockSpec((tk, 1, dk, dv), lambda i, hh: (i, hh, 0, 0)),
        ],
        out_specs=[
            pl.BlockSpec((tk, dv), lambda i, hh: (i, hh)),
            pl.BlockSpec((tk, 1, dk, dv), lambda i, hh: (i, hh, 0, 0)),
        ],
        out_shape=[
            jax.ShapeDtypeStruct((n, heads * dv), BF16),
            jax.ShapeDtypeStruct(state.shape, state.dtype),
        ],
        scratch_shapes=[pltpu.VMEM((tk, dv), F32)],
        compiler_params=_params(2, 2 * _nbytes((tk, dk, dv), F32), scratch_bytes=_nbytes((tk, dv), F32)),
        name="gla_decode",
    )(proj, proj, proj, proj, log2_decay, norm_w, state)


def _ret_decode_kernel(q_ref, k_ref, v_ref, gb_ref, cos_ref, sin_ref, lg_ref, nw_ref, s_in_ref,
                       o_ref, s_out_ref, o_scr, *, dk, dv):
    cos, sin = cos_ref[0:1, :], sin_ref[0:1, :]
    q = _rotary(q_ref[...].astype(F32), cos, sin)
    k = _rotary(k_ref[...].astype(F32), cos, sin) * (dk ** -0.5)
    v = v_ref[...].astype(F32)
    gamma = jnp.exp(jnp.concatenate([lg_ref[0]] * (dv // V7X_LANES), axis=1))
    _decode_tokens(lambda tt: gamma, k, q, v, s_in_ref, s_out_ref, o_scr)
    tokens = q.shape[0]
    _ln_gate_store(o_scr[...], nw_ref[...], gb_ref[...].astype(F32), o_ref, slice(0, tokens), slice(0, dv))


def _ret_decode(proj, cos, sin, log_gamma, norm_w, state, lay):
    heads, dk, dv = lay["ret_heads"], lay["ret_dk"], lay["ret_dv"]
    n = proj.shape[0]
    tk = min(DEC_TOKENS, n)
    assert n % tk == 0
    half = dk // 2
    kern = functools.partial(_ret_decode_kernel, dk=dk, dv=dv)
    return pl.pallas_call(
        kern,
        grid=(n // tk, heads),
        in_specs=[
            pl.BlockSpec((tk, dk), lambda i, hh: (i, lay["qb"] // dk + hh)),
            pl.BlockSpec((tk, dk), lambda i, hh: (i, lay["kb"] // dk + hh)),
            pl.BlockSpec((tk, dv), lambda i, hh: (i, lay["vb"] // dv + hh)),
            pl.BlockSpec((tk, dv), lambda i, hh: (i, lay["gb"] // dv + hh)),
            pl.BlockSpec((8, half), lambda i, hh: (0, 0)),
            pl.BlockSpec((8, half), lambda i, hh: (0, 0)),
            pl.BlockSpec((1, 1, V7X_LANES), lambda i, hh: (hh, 0, 0)),
            pl.BlockSpec((1, dv), lambda i, hh: (0, 0)),
            pl.BlockSpec((tk, 1, dk, dv), lambda i, hh: (i, hh, 0, 0)),
        ],
        out_specs=[
            pl.BlockSpec((tk, dv), lambda i, hh: (i, hh)),
            pl.BlockSpec((tk, 1, dk, dv), lambda i, hh: (i, hh, 0, 0)),
        ],
        out_shape=[
            jax.ShapeDtypeStruct((n, heads * dv), BF16),
            jax.ShapeDtypeStruct(state.shape, state.dtype),
        ],
        scratch_shapes=[pltpu.VMEM((tk, dv), F32)],
        compiler_params=_params(2, 2 * _nbytes((tk, dk, dv), F32), scratch_bytes=_nbytes((tk, dv), F32)),
        name="ret_decode",
    )(proj, proj, proj, proj, cos, sin, log_gamma, norm_w, state)


def _rope_table_kernel(cos_ref, sin_ref, *, pos0, half):
    rows, lanes = cos_ref.shape
    pos = (lax.broadcasted_iota(jnp.int32, (rows, lanes), 0) + (pl.program_id(0) * rows + pos0)).astype(F32)
    idx = lax.broadcasted_iota(jnp.int32, (rows, lanes), 1).astype(F32)
    inv = jnp.exp(idx * (-math.log(ROPE_BASE) / half))
    ang = pos * inv
    cos_ref[...] = jnp.cos(ang)
    sin_ref[...] = jnp.sin(ang)


def _rope_tables(n_pos, pos0, half):
    rows = min(n_pos, 256)
    assert n_pos % rows == 0
    kern = functools.partial(_rope_table_kernel, pos0=pos0, half=half)
    spec = pl.BlockSpec((rows, half), lambda i: (i, 0))
    return pl.pallas_call(
        kern,
        grid=(n_pos // rows,),
        in_specs=[],
        out_specs=[spec, spec],
        out_shape=[jax.ShapeDtypeStruct((n_pos, half), F32)] * 2,
        compiler_params=_params(1, 2 * _nbytes((rows, half), F32)),
        name="rope_tables",
    )()


def _merge_kernel(oap_ref, obp_ref, oas_ref, obs_ref, wa_ref, wb_ref, g0p_ref, g1p_ref, g0s_ref, g1s_ref, wnext_ref,
                  mp_ref, ms_ref, wnext_bf_ref, wa_bf, wb_bf):
    wnext_bf_ref[...] = wnext_ref[...].astype(BF16)

    def merged(oa, ob, g0, g1):
        ya = _dot(oa, wa_bf[...])
        yb = _dot(ob, wb_bf[...])
        return _sigmoid(g0.astype(F32)) * ya + _sigmoid(g1.astype(F32)) * yb

    @pl.when(pl.program_id(1) == 0)
    def _():
        wa_bf[...] = wa_ref[...].astype(BF16)
        wb_bf[...] = wb_ref[...].astype(BF16)
        ms_ref[...] = merged(oas_ref[...], obs_ref[...], g0s_ref[...], g1s_ref[...]).astype(ms_ref.dtype)

    mp_ref[...] = merged(oap_ref[...], obp_ref[...], g0p_ref[...], g1p_ref[...]).astype(mp_ref.dtype)


def _slab_specs(w_next, n_steps, step_of):
    kn, dn = w_next.shape
    assert kn % n_steps == 0 and (kn // n_steps) % 16 == 0, (kn, n_steps)
    slab = kn // n_steps
    spec = pl.BlockSpec((slab, dn), lambda j, i: (step_of(j, i), 0))
    return spec, spec, jax.ShapeDtypeStruct((kn, dn), BF16), _nbytes((slab, dn), F32) + _nbytes((slab, dn), BF16)


def _merge(oa_p, ob_p, oa_s, ob_s, wa, wb, proj_p, proj_s, lay, w_next):
    m_p, ka = oa_p.shape
    kb = ob_p.shape[1]
    tail = oa_s.shape[0]
    d = wa.shape[1]
    tm = _row_tile(m_p, ROW_TILE)
    tn = min(d, 1024)
    assert d % tn == 0 and lay["mg"] % tn == 0
    g0 = lay["mg"] // tn
    g1 = (lay["mg"] + d) // tn
    n_m = m_p // tm
    slab_in, slab_out, slab_shape, slab_bytes = _slab_specs(w_next, (d // tn) * n_m, lambda j, i: j * n_m + i)
    return pl.pallas_call(
        _merge_kernel,
        grid=(d // tn, n_m),
        in_specs=[
            pl.BlockSpec((tm, ka), lambda j, i: (i, 0)),
            pl.BlockSpec((tm, kb), lambda j, i: (i, 0)),
            pl.BlockSpec((tail, ka), lambda j, i: (0, 0)),
            pl.BlockSpec((tail, kb), lambda j, i: (0, 0)),
            pl.BlockSpec((ka, tn), lambda j, i: (0, j)),
            pl.BlockSpec((kb, tn), lambda j, i: (0, j)),
            pl.BlockSpec((tm, tn), lambda j, i: (i, g0 + j)),
            pl.BlockSpec((tm, tn), lambda j, i: (i, g1 + j)),
            pl.BlockSpec((tail, tn), lambda j, i: (0, g0 + j)),
            pl.BlockSpec((tail, tn), lambda j, i: (0, g1 + j)),
            slab_in,
        ],
        out_specs=[pl.BlockSpec((tm, tn), lambda j, i: (i, j)), pl.BlockSpec((tail, tn), lambda j, i: (0, j)),
                   slab_out],
        out_shape=[jax.ShapeDtypeStruct((m_p, d), BF16), jax.ShapeDtypeStruct((tail, d), BF16), slab_shape],
        scratch_shapes=[pltpu.VMEM((ka, tn), BF16), pltpu.VMEM((kb, tn), BF16)],
        compiler_params=_params(
            2, _nbytes((tm, ka), BF16), _nbytes((tm, kb), BF16), _nbytes((tail, ka), BF16), _nbytes((tail, kb), BF16),
            _nbytes((ka, tn), F32), _nbytes((kb, tn), F32), 3 * _nbytes((tm, tn), BF16), 3 * _nbytes((tail, tn), BF16),
            slab_bytes,
            scratch_bytes=_nbytes((ka, tn), BF16) + _nbytes((kb, tn), BF16) + 3 * _nbytes((tm, tn), F32)),
        name="merge",
    )(oa_p, ob_p, oa_s, ob_s, wa, wb, proj_p, proj_p, proj_s, proj_s, w_next)


def _proj_res_norm_kernel(ap_ref, as_ref, w_ref, resp_ref, ress_ref, nw_ref, *out_refs, emit_sum):
    n_out = 2 if emit_sum else 1
    outs_p, outs_s = out_refs[:n_out], out_refs[n_out:]
    i = pl.program_id(0)
    k = pl.program_id(1)
    last_k = k == pl.num_programs(1) - 1
    d = w_ref.shape[1]
    col_chunk = min(d, 512)

    def step(a_ref, res_ref, outs):
        acc_ref = outs[0]
        nrow = acc_ref.shape[0]
        row_chunk = min(nrow, 128)
        assert nrow % row_chunk == 0

        @pl.when(k == 0)
        def _():
            acc_ref[...] = res_ref[...]

        a = a_ref[...]
        for c in range(d // col_chunk):
            cs = slice(c * col_chunk, (c + 1) * col_chunk)
            acc_ref[:, cs] += _dot(a, w_ref[:, cs])

        @pl.when(last_k)
        def _():
            def body(c, carry):
                rr = pl.ds(pl.multiple_of(c * row_chunk, row_chunk), row_chunk)
                y = _rmsnorm_rows(acc_ref[rr, :], nw_ref[...])
                if emit_sum:
                    outs[1][rr, :] = y.astype(outs[1].dtype)
                else:
                    acc_ref[rr, :] = y
                return carry

            lax.fori_loop(0, nrow // row_chunk, body, 0)

    step(ap_ref, resp_ref, outs_p)

    @pl.when(i == pl.num_programs(0) - 1)
    def _():
        step(as_ref, ress_ref, outs_s)


def _proj_res_norm(a_p, a_s, w, res_p, res_s, norm_w, emit_sum, tk):
    m_p, kdim = a_p.shape
    tail = a_s.shape[0]
    d = w.shape[1]
    tm = _row_tile(m_p, ROW_TILE)
    tk = min(kdim, tk)
    assert kdim % tk == 0
    p_spec = pl.BlockSpec((tm, d), lambda i, k: (i, 0))
    s_spec = pl.BlockSpec((tail, d), lambda i, k: (0, 0))
    out_specs = [p_spec, s_spec]
    out_shape = [jax.ShapeDtypeStruct((m_p, d), F32), jax.ShapeDtypeStruct((tail, d), F32)]
    assert w.dtype == BF16
    blocks = [_nbytes((tm, tk), BF16), _nbytes((tail, tk), BF16), _nbytes((tk, d), BF16),
              2 * _nbytes((tm, d), F32), 2 * _nbytes((tail, d), F32)]
    if emit_sum:
        out_specs = [p_spec, p_spec, s_spec, s_spec]
        out_shape = [out_shape[0], jax.ShapeDtypeStruct((m_p, d), BF16),
                     out_shape[1], jax.ShapeDtypeStruct((tail, d), BF16)]
        blocks += [_nbytes((tm, d), BF16), _nbytes((tail, d), BF16)]
    return pl.pallas_call(
        functools.partial(_proj_res_norm_kernel, emit_sum=emit_sum),
        grid=(m_p // tm, kdim // tk),
        in_specs=[
            pl.BlockSpec((tm, tk), lambda i, k: (i, k)),
            pl.BlockSpec((tail, tk), lambda i, k: (0, k)),
            pl.BlockSpec((tk, d), lambda i, k: (k, 0)),
            p_spec,
            s_spec,
            pl.BlockSpec((1, d), lambda i, k: (0, 0)),
        ],
        out_specs=out_specs,
        out_shape=out_shape,
        compiler_params=_params(2, *blocks),
        name="proj_res_norm",
    )(a_p, a_s, w, res_p, res_s, norm_w.reshape(1, d))


def _swiglu_kernel(hp_ref, hs_ref, wg_ref, wu_ref, wnext_ref, op_ref, os_ref, wnext_bf_ref, wg_bf, wu_bf):
    wnext_bf_ref[...] = wnext_ref[...].astype(BF16)

    tn = wg_bf.shape[1]
    col_chunk = min(tn, 256)

    def act(h_ref, o_ref):
        h = h_ref[...]
        for c in range(tn // col_chunk):
            cs = slice(c * col_chunk, (c + 1) * col_chunk)
            a = _dot(h, wg_bf[:, cs])
            b = _dot(h, wu_bf[:, cs])
            o_ref[:, cs] = (_silu(a) * b).astype(o_ref.dtype)

    @pl.when(pl.program_id(1) == 0)
    def _():
        wg_bf[...] = wg_ref[...].astype(BF16)
        wu_bf[...] = wu_ref[...].astype(BF16)
        act(hs_ref, os_ref)

    act(hp_ref, op_ref)


def _swiglu(h_p, h_s, wg, wu, w_next):
    m_p, d = h_p.shape
    tail = h_s.shape[0]
    f = wg.shape[1]
    tm = _row_tile(m_p, ROW_TILE_WIDE)
    tn = 512 if f % 512 == 0 else 256
    assert f % tn == 0
    n_m = m_p // tm
    slab_in, slab_out, slab_shape, slab_bytes = _slab_specs(w_next, (f // tn) * n_m, lambda j, i: j * n_m + i)
    return pl.pallas_call(
        _swiglu_kernel,
        grid=(f // tn, n_m),
        in_specs=[
            pl.BlockSpec((tm, d), lambda j, i: (i, 0)),
            pl.BlockSpec((tail, d), lambda j, i: (0, 0)),
            pl.BlockSpec((d, tn), lambda j, i: (0, j)),
            pl.BlockSpec((d, tn), lambda j, i: (0, j)),
            slab_in,
        ],
        out_specs=[pl.BlockSpec((tm, tn), lambda j, i: (i, j)), pl.BlockSpec((tail, tn), lambda j, i: (0, j)),
                   slab_out],
        out_shape=[jax.ShapeDtypeStruct((m_p, f), BF16), jax.ShapeDtypeStruct((tail, f), BF16), slab_shape],
        scratch_shapes=[pltpu.VMEM((d, tn), BF16), pltpu.VMEM((d, tn), BF16)],
        compiler_params=_params(
            2, _nbytes((tm, d), BF16), _nbytes((tail, d), BF16), 2 * _nbytes((d, tn), F32),
            _nbytes((tm, tn), BF16), _nbytes((tail, tn), BF16), slab_bytes,
            scratch_bytes=2 * _nbytes((d, tn), BF16) + 3 * _nbytes((tm, tn), F32)),
        name="swiglu",
    )(h_p, h_s, wg, wu, w_next)


def _layout(d_model, in_width, state_gla, state_ret, gate_rank):
    _, _, gh, gdk, gdv = state_gla.shape
    _, _, rh, rdk, rdv = state_ret.shape
    gqk, gv, rqk, rv = gh * gdk, gh * gdv, rh * rdk, rh * rdv
    lay = dict(gla_heads=gh, gla_dk=gdk, gla_dv=gdv, ret_heads=rh, ret_dk=rdk, ret_dv=rdv, rank=gate_rank)
    off = 0
    for name, width in (("qa", gqk), ("ka", gqk), ("va", gv), ("ga", gv), ("qb", rqk), ("kb", rqk),
                        ("vb", rv), ("gb", rv), ("mg", 2 * d_model)):
        lay[name] = off
        off += width
    lay["out_cols"] = off
    lay["plain_cols"] = 2 * gqk + gv
    lay["gd_src"] = lay["plain_cols"]
    assert lay["gd_src"] % V7X_LANES == 0 and gate_rank <= V7X_LANES
    assert in_width == off + gate_rank
    return lay


def _layer(x_p, x_s, st_gla, st_ret, wts, lay, tables, final_norm):
    (norm_mix, w_in, w_gate_up, b_gate, gla_norm_w, w_gla_up, ret_norm_w, w_ret_up, w_out, norm_ffn,
     w_ffn_gate, w_ffn_up, w_ffn_down) = wts
    batch, seq, d = x_p.shape
    rank = lay["rank"]
    gqk = lay["gla_heads"] * lay["gla_dk"]
    wup_pad = jnp.zeros((V7X_LANES, gqk), F32).at[:rank].set(w_gate_up)
    bup = b_gate.reshape(1, gqk)
    gnw = gla_norm_w.reshape(1, -1)
    rnw = ret_norm_w.reshape(1, -1)
    tn = 1024 if (lay["out_cols"] % 1024 == 0 and lay["plain_cols"] % 1024 == 0) else 512
    cos_p, sin_p, cos_s, sin_s, log_gamma = tables
    xp = x_p.reshape(batch * seq, d)
    xs = x_s.reshape(-1, d)
    w_in_t = w_in.T

    h_p, h_s, g_p, g_s = _rmsnorm_gate(xp, xs, norm_mix, w_in_t, lay["gd_src"], wup_pad, bup)
    proj_p, proj_s = _in_proj(h_p, h_s, w_in_t, lay["plain_cols"], rank, lay["out_cols"], tn)
    oa_p, sa_p = _gla_prompt(proj_p, g_p, gnw, lay, batch, seq)
    ob_p, sb_p = _ret_prompt(proj_p, cos_p, sin_p, log_gamma, rnw, lay, batch, seq)
    oa_s, sa_s = _gla_decode(proj_s, g_s, gnw, st_gla, lay)
    ob_s, sb_s = _ret_decode(proj_s, cos_s, sin_s, log_gamma, rnw, st_ret, lay)
    m_p, m_s, w_out_bf = _merge(oa_p, ob_p, oa_s, ob_s, w_gla_up, w_ret_up, proj_p, proj_s, lay, w_out)
    x1_p, h2_p, x1_s, h2_s = _proj_res_norm(m_p, m_s, w_out_bf, xp, xs, norm_ffn, True, 1024)
    act_p, act_s, w_down_bf = _swiglu(h2_p, h2_s, w_ffn_gate, w_ffn_up, w_ffn_down)
    y_p, y_s = _proj_res_norm(act_p, act_s, w_down_bf, x1_p, x1_s, final_norm, False, 1408)
    return (y_p, sa_p, sb_p), (y_s, sa_s, sb_s)


def kernel(x_prompt, x_sample, state_gla, state_ret, norm_mix, w_in, w_gla_gate_up, b_gla_gate, gla_norm_w,
           w_gla_up, ret_norm_w, w_ret_up, w_out, norm_ffn, w_ffn_gate, w_ffn_up, w_ffn_down, norm_final):
    depth = w_in.shape[0]
    assert depth == 1, "single-layer trunk"
    batch, seq, d = x_prompt.shape
    lay = _layout(d, w_in.shape[-1], state_gla, state_ret, w_gla_gate_up.shape[1])
    rh, rdk = lay["ret_heads"], lay["ret_dk"]
    half = rdk // 2
    assert half == V7X_LANES
    cos_p, sin_p = _rope_tables(seq, 0, half)
    cos_s, sin_s = _rope_tables(8, PAST_LEN, half)
    lg = jnp.log1p(-jnp.exp(jnp.linspace(math.log(1.0 / 32), math.log(1.0 / 512), rh))).astype(F32)
    log_gamma = jnp.broadcast_to(lg[:, None, None], (rh, 1, V7X_LANES))
    tables = (cos_p, sin_p, cos_s, sin_s, log_gamma)

    wts = (norm_mix[0], w_in[0], w_gla_gate_up[0], b_gla_gate[0], gla_norm_w[0], w_gla_up[0], ret_norm_w[0],
           w_ret_up[0], w_out[0], norm_ffn[0], w_ffn_gate[0], w_ffn_up[0], w_ffn_down[0])
    (y_p, ga_p, re_p), (y_s, ga_s, re_s) = _layer(
        x_prompt, x_sample, state_gla[0], state_ret[0], wts, lay, tables, norm_final)

    sd = state_gla.dtype
    return (y_p.reshape(batch, seq, d), y_s.reshape(x_sample.shape),
            ga_p[None].astype(sd), re_p[None].astype(state_ret.dtype),
            ga_s[None].astype(sd), re_s[None].astype(state_ret.dtype))
```

```python
import functools
import math

import numpy as np
import jax
import jax.numpy as jnp
from jax import lax
from jax.experimental import pallas as pl
from jax.experimental.pallas import tpu as pltpu

EPS = 1e-6
ROPE_BASE = 10000.0
GLA_GATE_NORM = 16.0
PAST_LEN = 16384

V7X_LANES = 128
V7X_VMEM_REQUEST_CAP = 60000 * 1024
COMPILER_SCRATCH_BYTES = 12 * 1024 * 1024

GLA_CHUNK = 64
GLA_STEP_CHUNKS = 8
LOG2_E = 1.4426950408889634
RET_CHUNK = 128
RET_STEP_CHUNKS = 4
ROW_TILE = 1024
ROW_TILE_WIDE = 2048

BF16 = jnp.bfloat16
F32 = jnp.float32


def _params(n_axes, *block_bytes, scratch_bytes=0, claim_all=False):
    need = 2 * sum(block_bytes) + scratch_bytes + COMPILER_SCRATCH_BYTES
    if claim_all:
        need = V7X_VMEM_REQUEST_CAP
    return pltpu.CompilerParams(
        dimension_semantics=("arbitrary",) * n_axes,
        vmem_limit_bytes=int(min(V7X_VMEM_REQUEST_CAP, need)),
    )


def _nbytes(shape, dtype):
    return int(np.prod(shape)) * jnp.dtype(dtype).itemsize


def _sigmoid(x):
    return 1.0 / (1.0 + jnp.exp(-x))


def _silu(x):
    return x * _sigmoid(x)


def _log_sigmoid(x):
    return jnp.minimum(x, 0.0) - jnp.log(1.0 + jnp.exp(-jnp.abs(x)))


def _dot(a, b):
    return jnp.dot(a, b, preferred_element_type=F32)


def _dot_nt(a, b):
    return lax.dot_general(a, b, (((1,), (1,)), ((), ())), preferred_element_type=F32)


def _dot_tn(a, b):
    return lax.dot_general(a, b, (((0,), (0,)), ((), ())), preferred_element_type=F32)


def _row_tile(m, want):
    t = min(m, want)
    assert m % t == 0, (m, t)
    return t


def _rmsnorm_rows(x, w):
    ms = jnp.mean(x * x, axis=-1, keepdims=True)
    return x * lax.rsqrt(ms + EPS) * w


def _rmsnorm_kernel(xp_ref, xs_ref, w_ref, wgd_ref, wup_ref, bup_ref, hp_ref, hs_ref, gp_ref, gs_ref, wgd_bf, wup_bf):
    @pl.when(pl.program_id(0) == 0)
    def _():
        wgd_bf[...] = wgd_ref[...].astype(BF16)
        wup_bf[...] = wup_ref[...].astype(BF16)

    def rows(x_ref, h_ref, g_ref):
        h = _rmsnorm_rows(x_ref[...], w_ref[...]).astype(h_ref.dtype)
        h_ref[...] = h
        gd = _dot_nt(h, wgd_bf[...])
        x = _dot(gd.astype(BF16), wup_bf[...]) + bup_ref[...]
        g_ref[...] = _log_sigmoid(x) * (LOG2_E / GLA_GATE_NORM)

    rows(xp_ref, hp_ref, gp_ref)

    @pl.when(pl.program_id(0) == 0)
    def _():
        rows(xs_ref, hs_ref, gs_ref)


def _rmsnorm_gate(x_p, x_s, w, w_in_t, gate_row0, wup_pad, bup):
    m_p, d = x_p.shape
    tail = x_s.shape[0]
    gw = wup_pad.shape[1]
    tm = _row_tile(m_p, 512)
    assert gate_row0 % V7X_LANES == 0
    return pl.pallas_call(
        _rmsnorm_kernel,
        grid=(m_p // tm,),
        in_specs=[
            pl.BlockSpec((tm, d), lambda i: (i, 0)),
            pl.BlockSpec((tail, d), lambda i: (0, 0)),
            pl.BlockSpec((1, d), lambda i: (0, 0)),
            pl.BlockSpec((V7X_LANES, d), lambda i: (gate_row0 // V7X_LANES, 0)),
            pl.BlockSpec((V7X_LANES, gw), lambda i: (0, 0)),
            pl.BlockSpec((1, gw), lambda i: (0, 0)),
        ],
        out_specs=[
            pl.BlockSpec((tm, d), lambda i: (i, 0)), pl.BlockSpec((tail, d), lambda i: (0, 0)),
            pl.BlockSpec((tm, gw), lambda i: (i, 0)), pl.BlockSpec((tail, gw), lambda i: (0, 0)),
        ],
        out_shape=[
            jax.ShapeDtypeStruct((m_p, d), BF16), jax.ShapeDtypeStruct((tail, d), BF16),
            jax.ShapeDtypeStruct((m_p, gw), F32), jax.ShapeDtypeStruct((tail, gw), F32),
        ],
        scratch_shapes=[pltpu.VMEM((V7X_LANES, d), BF16), pltpu.VMEM((V7X_LANES, gw), BF16)],
        compiler_params=_params(
            1, _nbytes((tm, d), F32), _nbytes((tail, d), F32), _nbytes((tm, d), BF16), _nbytes((V7X_LANES, d), F32),
            _nbytes((tm, gw), F32), _nbytes((tail, gw), F32), claim_all=True),
        name="rmsnorm_gate",
    )(x_p, x_s, w.reshape(1, d), w_in_t, wup_pad, bup)


def _in_proj_kernel(hp_ref, hs_ref, wm_ref, wn_ref, op_ref, os_ref, wbf_ref, *, n_plain, shift):
    j = pl.program_id(0)
    i = pl.program_id(1)
    tn = wbf_ref.shape[0]

    @pl.when(jnp.logical_and(i == 0, j < n_plain))
    def _():
        wbf_ref[...] = wm_ref[...].astype(BF16)

    @pl.when(jnp.logical_and(i == 0, j >= n_plain))
    def _():
        wbf_ref[0:tn - shift, :] = wm_ref[shift:tn, :].astype(BF16)
        wbf_ref[tn - shift:tn, :] = wn_ref[...].astype(BF16)

    @pl.when(i == 0)
    def _():
        os_ref[...] = _dot_nt(hs_ref[...], wbf_ref[...]).astype(os_ref.dtype)

    op_ref[...] = _dot_nt(hp_ref[...], wbf_ref[...]).astype(op_ref.dtype)


def _in_proj(h_p, h_s, w_in_t, plain_cols, shift, out_cols, tn):
    m_p, d = h_p.shape
    tail = h_s.shape[0]
    tm = _row_tile(m_p, ROW_TILE_WIDE)
    assert plain_cols % tn == 0 and out_cols % tn == 0 and tn % shift == 0 and shift % 8 == 0
    n_plain = plain_cols // tn
    kern = functools.partial(_in_proj_kernel, n_plain=n_plain, shift=shift)
    return pl.pallas_call(
        kern,
        grid=(out_cols // tn, m_p // tm),
        in_specs=[
            pl.BlockSpec((tm, d), lambda j, i: (i, 0)),
            pl.BlockSpec((tail, d), lambda j, i: (0, 0)),
            pl.BlockSpec((tn, d), lambda j, i: (j, 0)),
            pl.BlockSpec((shift, d), lambda j, i: ((j + 1) * (tn // shift), 0)),
        ],
        out_specs=[pl.BlockSpec((tm, tn), lambda j, i: (i, j)), pl.BlockSpec((tail, tn), lambda j, i: (0, j))],
        out_shape=[jax.ShapeDtypeStruct((m_p, out_cols), BF16), jax.ShapeDtypeStruct((tail, out_cols), BF16)],
        scratch_shapes=[pltpu.VMEM((tn, d), BF16)],
        compiler_params=_params(
            2, _nbytes((tm, d), BF16), _nbytes((tail, d), BF16), _nbytes((tn, d), F32), _nbytes((shift, d), F32),
            _nbytes((tm, tn), BF16), _nbytes((tail, tn), BF16),
            scratch_bytes=_nbytes((tn, d), BF16) + _nbytes((tm, tn), F32)),
        name="in_proj",
    )(h_p, h_s, w_in_t, w_in_t)


def _prefix_sum_rows(sel3_bf16, g):
    g0 = g.astype(BF16)
    r1 = g - g0.astype(F32)
    g1 = r1.astype(BF16)
    g2 = (r1 - g1.astype(F32)).astype(BF16)
    return _dot(sel3_bf16, jnp.concatenate([g0, g1, g2], axis=0))


def _lane_bcast_cols(row, n):
    parts = []
    for c in range(n // V7X_LANES):
        tile = jnp.broadcast_to(row[:, c * V7X_LANES:(c + 1) * V7X_LANES], (V7X_LANES, V7X_LANES))
        parts.append(tile.T)
    return parts[0] if len(parts) == 1 else jnp.concatenate(parts, axis=0)


def _rms_gate_store(o, w, gate, out_ref, rows, cols):
    ms = jnp.mean(o * o, axis=-1, keepdims=True)
    y = o * lax.rsqrt(ms + EPS) * w
    out_ref[rows, cols] = (y * _silu(gate)).astype(out_ref.dtype)


def _ln_gate_store(o, w, gate, out_ref, rows, cols):
    mu = jnp.mean(o, axis=-1, keepdims=True)
    dlt = o - mu
    var = jnp.mean(dlt * dlt, axis=-1, keepdims=True)
    y = dlt * lax.rsqrt(var + EPS) * w
    out_ref[rows, cols] = (y * _silu(gate)).astype(out_ref.dtype)


def _decode_advance(tok0, decay_fn, k_ref, q_ref, v_ref, s_in_ref, s_out_ref, o_ref, *, heads, dk, dv):
    n_tok = s_in_ref.shape[0]
    reps = dv // V7X_LANES
    rows = pl.ds(pl.multiple_of(tok0, n_tok), n_tok)
    for hh in range(heads):
        kc = slice(hh * dk, (hh + 1) * dk)
        vc = slice(hh * dv, (hh + 1) * dv)
        k, q, v = k_ref[rows, kc], q_ref[rows, kc], v_ref[rows, vc]
        o_rows = []
        for tt in range(n_tok):
            k_cols = jnp.concatenate([_lane_bcast_cols(k[tt:tt + 1, :], dk)] * reps, axis=1)
            q_cols = jnp.concatenate([_lane_bcast_cols(q[tt:tt + 1, :], dk)] * reps, axis=1)
            s_new = decay_fn(rows, tt, hh) * s_in_ref[tt, hh] + k_cols * v[tt:tt + 1, :]
            s_out_ref[tt, hh] = s_new
            o_rows.append(jnp.sum(q_cols * s_new, axis=0, keepdims=True))
        o_ref[rows, vc] = jnp.concatenate(o_rows, axis=0)


def _decode_plan(n_dec, n_steps):
    assert n_dec % n_steps == 0 and (n_dec // n_steps) % 8 == 0, (n_dec, n_steps)
    return n_dec // n_steps


def _gla_sum_matrices(c):
    levels = c.bit_length() - 1
    assert 1 << levels == c
    i = lax.broadcasted_iota(jnp.int32, (c, c), 0)
    j = lax.broadcasted_iota(jnp.int32, (c, c), 1)
    mats = [j <= i]
    for l in range(levels):
        ref = jnp.bitwise_or(jnp.bitwise_and(i, -(2 << l)), 1 << l)
        mats.append(jnp.logical_and(j > jnp.minimum(i, ref), j <= jnp.maximum(i, ref)))
    mats.append(j > i)
    sel = jnp.concatenate([jnp.where(m, 1.0, 0.0).astype(BF16) for m in mats], axis=0)
    return jnp.concatenate([sel, sel, sel], axis=1)


def _pair_level(c):
    levels = c.bit_length() - 1
    i = lax.broadcasted_iota(jnp.int32, (c, c), 0)
    j = lax.broadcasted_iota(jnp.int32, (c, c), 1)
    x = jnp.bitwise_xor(i, j)
    lvl = jnp.zeros((c, c), jnp.int32)
    for l in range(1, levels):
        lvl = lvl + jnp.where(x >= (1 << l), 1, 0)
    return jnp.where(i > j, lvl, jnp.where(i == j, levels, -1))


def _queries_else_keys(q, k, l):
    c = q.shape[0]
    span = 1 << l
    if span >= 8:
        parts = [(q if (b & 1) else k)[b * span:(b + 1) * span, :] for b in range(c // span)]
        return jnp.concatenate(parts, axis=0)
    row = lax.broadcasted_iota(jnp.int32, q.shape, 0)
    return jnp.where(jnp.bitwise_and(row, span) != 0, q, k)


def _gla_level_scores(q, k, sums):
    c = q.shape[0]
    levels = c.bit_length() - 1
    out = []
    for l in range(levels):
        x = _queries_else_keys(q, k, l) * jnp.exp2(sums[(1 + l) * c:(2 + l) * c, :])
        xb = x.astype(BF16)
        out.append(_dot_nt(xb, xb))
    return out


def _gla_chunk_out(q, k, v, sums, level_scores, pair_level, state):
    c = q.shape[0]
    levels = c.bit_length() - 1
    scores = jnp.where(pair_level == levels, jnp.sum(q * k, axis=-1, keepdims=True), 0.0)
    for l in range(levels):
        scores = jnp.where(pair_level == l, level_scores[l], scores)
    o = _dot((q * jnp.exp2(sums[0:c, :])).astype(BF16), state.astype(BF16))
    return o + _dot(scores.astype(BF16), v)


def _gla_next_state(k, v, sums, state):
    c, dk = k.shape
    levels = c.bit_length() - 1
    k_tail = (k * jnp.exp2(sums[(levels + 1) * c:(levels + 2) * c, :])).astype(BF16)
    decay = _lane_bcast_cols(jnp.exp2(sums[c - 1:c, :]), dk)
    decay_full = jnp.concatenate([decay] * (v.shape[1] // V7X_LANES), axis=1)
    return decay_full * state + _dot_tn(k_tail, v)


def _gla_kernel(q_ref, k_ref, v_ref, ga_ref, g_ref, qd_ref, kd_ref, vd_ref, gad_ref, gd_ref, nw_ref, sd_in_ref,
                o_ref, s_out_ref, od_ref, sd_out_ref,
                s_ref, mats_ref, lvl_ref, a_dec, q_dec, k_dec, v_dec, o_dec, *, heads, dk, dv):
    t = pl.program_id(1)
    step = pl.program_id(0) * pl.num_programs(1) + t
    last_step = pl.num_programs(0) * pl.num_programs(1) - 1

    @pl.when(t == 0)
    def _():
        s_ref[...] = jnp.zeros_like(s_ref)
        mats_ref[...] = _gla_sum_matrices(GLA_CHUNK)
        lvl_ref[...] = _pair_level(GLA_CHUNK)

    @pl.when(step == 0)
    def _():
        a_dec[...] = jnp.exp2(gd_ref[...])
        q_dec[...] = qd_ref[...].astype(F32) * (dk ** -0.5)
        k_dec[...] = kd_ref[...].astype(F32)
        v_dec[...] = vd_ref[...].astype(F32)

    reps = dv // V7X_LANES

    def decay(rows, tt, hh):
        a = a_dec[rows, hh * dk:(hh + 1) * dk]
        return jnp.concatenate([_lane_bcast_cols(a[tt:tt + 1, :], dk)] * reps, axis=1)

    _decode_advance(step * sd_in_ref.shape[0], decay, k_dec, q_dec, v_dec, sd_in_ref, sd_out_ref, o_dec,
                    heads=heads, dk=dk, dv=dv)

    @pl.when(step == last_step)
    def _():
        n_dec = o_dec.shape[0]
        for hh in range(heads):
            vc = slice(hh * dv, (hh + 1) * dv)
            _rms_gate_store(o_dec[:, vc], nw_ref[...], gad_ref[:, vc].astype(F32), od_ref, slice(0, n_dec), vc)

    ct = q_ref.shape[0]

    def chunk(c, carry):
        rows = pl.ds(pl.multiple_of(c * GLA_CHUNK, GLA_CHUNK), GLA_CHUNK)
        sums_all = _prefix_sum_rows(mats_ref[...], g_ref[rows, :])
        pair_level = lvl_ref[...]
        kcs = [slice(hh * dk, (hh + 1) * dk) for hh in range(heads)]
        vcs = [slice(hh * dv, (hh + 1) * dv) for hh in range(heads)]
        qs = [q_ref[rows, kc].astype(F32) * (dk ** -0.5) for kc in kcs]
        ks = [k_ref[rows, kc].astype(F32) for kc in kcs]
        lvl_scores = [_gla_level_scores(qs[hh], ks[hh], sums_all[:, kcs[hh]]) for hh in range(heads)]
        outs = [_gla_chunk_out(qs[hh], ks[hh], v_ref[rows, vcs[hh]], sums_all[:, kcs[hh]], lvl_scores[hh],
                               pair_level, s_ref[hh]) for hh in range(heads)]
        for hh in range(heads):
            s_ref[hh] = _gla_next_state(ks[hh], v_ref[rows, vcs[hh]], sums_all[:, kcs[hh]], s_ref[hh])
        for hh in range(heads):
            _rms_gate_store(outs[hh], nw_ref[...], ga_ref[rows, vcs[hh]].astype(F32), o_ref, rows, vcs[hh])
        return carry

    lax.fori_loop(0, ct // GLA_CHUNK, chunk, 0, unroll=2)

    @pl.when(t == pl.num_programs(1) - 1)
    def _():
        s_out_ref[0] = s_ref[...]


def _gla(proj, log2_decay, proj_dec, log2_decay_dec, norm_w, state_dec, lay, batch, seq):
    heads, dk, dv = lay["gla_heads"], lay["gla_dk"], lay["gla_dv"]
    qk, vw = heads * dk, heads * dv
    ct = min(seq, GLA_STEP_CHUNKS * GLA_CHUNK)
    levels = GLA_CHUNK.bit_length() - 1
    assert seq % ct == 0 and ct % GLA_CHUNK == 0
    nt = seq // ct
    n_dec = proj_dec.shape[0]
    tps = _decode_plan(n_dec, batch * nt)
    row = lambda b, t: b * nt + t
    dec_in = lambda width, col: pl.BlockSpec((n_dec, width), lambda b, t: (0, col // width))
    dec_state = pl.BlockSpec((tps, heads, dk, dv), lambda b, t: (row(b, t), 0, 0, 0))
    kern = functools.partial(_gla_kernel, heads=heads, dk=dk, dv=dv)
    return pl.pallas_call(
        kern,
        grid=(batch, nt),
        in_specs=[
            pl.BlockSpec((ct, qk), lambda b, t: (row(b, t), lay["qa"] // qk)),
            pl.BlockSpec((ct, qk), lambda b, t: (row(b, t), lay["ka"] // qk)),
            pl.BlockSpec((ct, vw), lambda b, t: (row(b, t), lay["va"] // vw)),
            pl.BlockSpec((ct, vw), lambda b, t: (row(b, t), lay["ga"] // vw)),
            pl.BlockSpec((ct, qk), lambda b, t: (row(b, t), 0)),
            dec_in(qk, lay["qa"]), dec_in(qk, lay["ka"]), dec_in(vw, lay["va"]), dec_in(vw, lay["ga"]),
            dec_in(qk, 0),
            pl.BlockSpec((1, dv), lambda b, t: (0, 0)),
            dec_state,
        ],
        out_specs=[
            pl.BlockSpec((ct, vw), lambda b, t: (row(b, t), 0)),
            pl.BlockSpec((1, heads, dk, dv), lambda b, t: (b, 0, 0, 0)),
            pl.BlockSpec((n_dec, vw), lambda b, t: (0, 0)),
            dec_state,
        ],
        out_shape=[
            jax.ShapeDtypeStruct((batch * seq, vw), BF16),
            jax.ShapeDtypeStruct((batch, heads, dk, dv), F32),
            jax.ShapeDtypeStruct((n_dec, vw), BF16),
            jax.ShapeDtypeStruct(state_dec.shape, state_dec.dtype),
        ],
        scratch_shapes=[
            pltpu.VMEM((heads, dk, dv), F32),
            pltpu.VMEM(((levels + 2) * GLA_CHUNK, 3 * GLA_CHUNK), BF16),
            pltpu.VMEM((GLA_CHUNK, GLA_CHUNK), jnp.int32),
            pltpu.VMEM((n_dec, qk), F32), pltpu.VMEM((n_dec, qk), F32), pltpu.VMEM((n_dec, qk), F32),
            pltpu.VMEM((n_dec, vw), F32), pltpu.VMEM((n_dec, vw), F32),
        ],
        compiler_params=_params(2, claim_all=True),
        name="gla",
    )(proj, proj, proj, proj, log2_decay, proj_dec, proj_dec, proj_dec, proj_dec, log2_decay_dec, norm_w, state_dec)


def _rotary(x, cos, sin):
    half = x.shape[1] // 2
    x1, x2 = x[:, :half], x[:, half:]
    return jnp.concatenate([x1 * cos - x2 * sin, x1 * sin + x2 * cos], axis=1)


def _ret_kernel(q_ref, k_ref, v_ref, gb_ref, cos_ref, sin_ref, qd_ref, kd_ref, vd_ref, gbd_ref, cosd_ref, sind_ref,
                lg_ref, nw_ref, sd_in_ref,
                o_ref, s_out_ref, od_ref, sd_out_ref,
                s_ref, dmat_ref, qdec_ref, kdec_ref, q_dec, k_dec, v_dec, o_dec, *, heads, dk, dv, c):
    t = pl.program_id(1)
    step = pl.program_id(0) * pl.num_programs(1) + t
    last_step = pl.num_programs(0) * pl.num_programs(1) - 1

    @pl.when(step == 0)
    def _():
        cosd, sind = cosd_ref[0:1, :], sind_ref[0:1, :]
        for hh in range(heads):
            kc = slice(hh * dk, (hh + 1) * dk)
            q_dec[:, kc] = _rotary(qd_ref[:, kc].astype(F32), cosd, sind)
            k_dec[:, kc] = _rotary(kd_ref[:, kc].astype(F32), cosd, sind) * (dk ** -0.5)
        v_dec[...] = vd_ref[...].astype(F32)

    def decay(rows, tt, hh):
        return jnp.exp(jnp.concatenate([lg_ref[hh]] * (dv // V7X_LANES), axis=1))

    _decode_advance(step * sd_in_ref.shape[0], decay, k_dec, q_dec, v_dec, sd_in_ref, sd_out_ref, o_dec,
                    heads=heads, dk=dk, dv=dv)

    @pl.when(step == last_step)
    def _():
        n_dec = o_dec.shape[0]
        for hh in range(heads):
            vc = slice(hh * dv, (hh + 1) * dv)
            _ln_gate_store(o_dec[:, vc], nw_ref[...], gbd_ref[:, vc].astype(F32), od_ref, slice(0, n_dec), vc)

    @pl.when(t == 0)
    def _():
        s_ref[...] = jnp.zeros_like(s_ref)
        ri = lax.broadcasted_iota(jnp.int32, (c, c), 0)
        rj = lax.broadcasted_iota(jnp.int32, (c, c), 1)
        dist = (ri - rj).astype(F32)
        rowl = lax.broadcasted_iota(jnp.int32, (c, V7X_LANES), 0).astype(F32)
        for hh in range(heads):
            lg = lg_ref[hh]
            dmat_ref[hh] = jnp.exp(jnp.where(ri >= rj, dist * lg[:, :1], -jnp.inf))
            qdec_ref[hh] = jnp.exp((rowl + 1.0) * lg)
            kdec_ref[hh] = jnp.exp((float(c - 1) - rowl) * lg)

    ct = q_ref.shape[0]
    kcs = [slice(hh * dk, (hh + 1) * dk) for hh in range(heads)]
    vcs = [slice(hh * dv, (hh + 1) * dv) for hh in range(heads)]

    def chunk(ci, carry):
        rows = pl.ds(pl.multiple_of(ci * c, c), c)
        cos, sin = cos_ref[rows, :], sin_ref[rows, :]
        qrs = [_rotary(q_ref[rows, kc].astype(F32), cos, sin).astype(BF16) for kc in kcs]
        krs = [_rotary(k_ref[rows, kc].astype(F32), cos, sin) * (dk ** -0.5) for kc in kcs]
        scores = [_dot_nt(qrs[hh], krs[hh].astype(BF16)) * dmat_ref[hh] for hh in range(heads)]
        outs = []
        for hh in range(heads):
            qdec = jnp.concatenate([qdec_ref[hh]] * (dv // V7X_LANES), axis=1)
            o = qdec * _dot(qrs[hh], s_ref[hh].astype(BF16))
            outs.append(o + _dot(scores[hh].astype(BF16), v_ref[rows, vcs[hh]]))
        for hh in range(heads):
            kdec = jnp.concatenate([kdec_ref[hh]] * (dk // V7X_LANES), axis=1)
            k_tail = (krs[hh] * kdec).astype(BF16)
            lgv = jnp.concatenate([lg_ref[hh]] * (dv // V7X_LANES), axis=1)
            s_ref[hh] = jnp.exp(float(c) * lgv) * s_ref[hh] + _dot_tn(k_tail, v_ref[rows, vcs[hh]])
        for hh in range(heads):
            _ln_gate_store(outs[hh], nw_ref[...], gb_ref[rows, vcs[hh]].astype(F32), o_ref, rows, vcs[hh])
        return carry

    lax.fori_loop(0, ct // c, chunk, 0)

    @pl.when(t == pl.num_programs(1) - 1)
    def _():
        s_out_ref[0] = s_ref[...]


def _ret(proj, cos, sin, proj_dec, cos_dec, sin_dec, log_gamma, norm_w, state_dec, lay, batch, seq):
    heads, dk, dv = lay["ret_heads"], lay["ret_dk"], lay["ret_dv"]
    qk, vw = heads * dk, heads * dv
    c = min(seq, RET_CHUNK)
    ct = min(seq, RET_STEP_CHUNKS * c)
    assert seq % ct == 0 and ct % c == 0
    nt = seq // ct
    half = dk // 2
    n_dec = proj_dec.shape[0]
    tps = _decode_plan(n_dec, batch * nt)
    row = lambda b, t: b * nt + t
    dec_in = lambda width, col: pl.BlockSpec((n_dec, width), lambda b, t: (0, col // width))
    dec_state = pl.BlockSpec((tps, heads, dk, dv), lambda b, t: (row(b, t), 0, 0, 0))
    table_dec = pl.BlockSpec((cos_dec.shape[0], half), lambda b, t: (0, 0))
    kern = functools.partial(_ret_kernel, heads=heads, dk=dk, dv=dv, c=c)
    return pl.pallas_call(
        kern,
        grid=(batch, nt),
        in_specs=[
            pl.BlockSpec((ct, qk), lambda b, t: (row(b, t), lay["qb"] // qk)),
            pl.BlockSpec((ct, qk), lambda b, t: (row(b, t), lay["kb"] // qk)),
            pl.BlockSpec((ct, vw), lambda b, t: (row(b, t), lay["vb"] // vw)),
            pl.BlockSpec((ct, vw), lambda b, t: (row(b, t), lay["gb"] // vw)),
            pl.BlockSpec((ct, half), lambda b, t: (t, 0)),
            pl.BlockSpec((ct, half), lambda b, t: (t, 0)),
            dec_in(qk, lay["qb"]), dec_in(qk, lay["kb"]), dec_in(vw, lay["vb"]), dec_in(vw, lay["gb"]),
            table_dec, table_dec,
            pl.BlockSpec((heads, 1, V7X_LANES), lambda b, t: (0, 0, 0)),
            pl.BlockSpec((1, dv), lambda b, t: (0, 0)),
            dec_state,
        ],
        out_specs=[
            pl.BlockSpec((ct, vw), lambda b, t: (row(b, t), 0)),
            pl.BlockSpec((1, heads, dk, dv), lambda b, t: (b, 0, 0, 0)),
            pl.BlockSpec((n_dec, vw), lambda b, t: (0, 0)),
            dec_state,
        ],
        out_shape=[
            jax.ShapeDtypeStruct((batch * seq, vw), BF16),
            jax.ShapeDtypeStruct((batch, heads, dk, dv), F32),
            jax.ShapeDtypeStruct((n_dec, vw), BF16),
            jax.ShapeDtypeStruct(state_dec.shape, state_dec.dtype),
        ],
        scratch_shapes=[
            pltpu.VMEM((heads, dk, dv), F32),
            pltpu.VMEM((heads, c, c), F32),
            pltpu.VMEM((heads, c, V7X_LANES), F32),
            pltpu.VMEM((heads, c, V7X_LANES), F32),
            pltpu.VMEM((n_dec, qk), F32), pltpu.VMEM((n_dec, qk), F32),
            pltpu.VMEM((n_dec, vw), F32), pltpu.VMEM((n_dec, vw), F32),
        ],
        compiler_params=_params(2, claim_all=True),
        name="ret",
    )(proj, proj, proj, proj, cos, sin, proj_dec, proj_dec, proj_dec, proj_dec, cos_dec, sin_dec,
      log_gamma, norm_w, state_dec)


def _rope_table_kernel(cos_ref, sin_ref, *, pos0, half):
    rows, lanes = cos_ref.shape
    pos = (lax.broadcasted_iota(jnp.int32, (rows, lanes), 0) + (pl.program_id(0) * rows + pos0)).astype(F32)
    idx = lax.broadcasted_iota(jnp.int32, (rows, lanes), 1).astype(F32)
    inv = jnp.exp(idx * (-math.log(ROPE_BASE) / half))
    ang = pos * inv
    cos_ref[...] = jnp.cos(ang)
    sin_ref[...] = jnp.sin(ang)


def _rope_tables(n_pos, pos0, half):
    rows = min(n_pos, 256)
    assert n_pos % rows == 0
    kern = functools.partial(_rope_table_kernel, pos0=pos0, half=half)
    spec = pl.BlockSpec((rows, half), lambda i: (i, 0))
    return pl.pallas_call(
        kern,
        grid=(n_pos // rows,),
        in_specs=[],
        out_specs=[spec, spec],
        out_shape=[jax.ShapeDtypeStruct((n_pos, half), F32)] * 2,
        compiler_params=_params(1, 2 * _nbytes((rows, half), F32)),
        name="rope_tables",
    )()


def _merge_kernel(oap_ref, obp_ref, oas_ref, obs_ref, wa_ref, wb_ref, g0p_ref, g1p_ref, g0s_ref, g1s_ref, wnext_ref,
                  mp_ref, ms_ref, wnext_bf_ref, wa_bf, wb_bf):
    wnext_bf_ref[...] = wnext_ref[...].astype(BF16)

    def merged(oa, ob, g0, g1):
        ya = _dot(oa, wa_bf[...])
        yb = _dot(ob, wb_bf[...])
        return _sigmoid(g0.astype(F32)) * ya + _sigmoid(g1.astype(F32)) * yb

    @pl.when(pl.program_id(1) == 0)
    def _():
        wa_bf[...] = wa_ref[...].astype(BF16)
        wb_bf[...] = wb_ref[...].astype(BF16)
        ms_ref[...] = merged(oas_ref[...], obs_ref[...], g0s_ref[...], g1s_ref[...]).astype(ms_ref.dtype)

    mp_ref[...] = merged(oap_ref[...], obp_ref[...], g0p_ref[...], g1p_ref[...]).astype(mp_ref.dtype)


def _slab_specs(w_next, n_steps, step_of):
    kn, dn = w_next.shape
    assert kn % n_steps == 0 and (kn // n_steps) % 16 == 0, (kn, n_steps)
    slab = kn // n_steps
    spec = pl.BlockSpec((slab, dn), lambda j, i: (step_of(j, i), 0))
    return spec, spec, jax.ShapeDtypeStruct((kn, dn), BF16), _nbytes((slab, dn), F32) + _nbytes((slab, dn), BF16)


def _merge(oa_p, ob_p, oa_s, ob_s, wa, wb, proj_p, proj_s, lay, w_next):
    m_p, ka = oa_p.shape
    kb = ob_p.shape[1]
    tail = oa_s.shape[0]
    d = wa.shape[1]
    tm = _row_tile(m_p, ROW_TILE)
    tn = min(d, 1024)
    assert d % tn == 0 and lay["mg"] % tn == 0
    g0 = lay["mg"] // tn
    g1 = (lay["mg"] + d) // tn
    n_m = m_p // tm
    slab_in, slab_out, slab_shape, slab_bytes = _slab_specs(w_next, (d // tn) * n_m, lambda j, i: j * n_m + i)
    return pl.pallas_call(
        _merge_kernel,
        grid=(d // tn, n_m),
        in_specs=[
            pl.BlockSpec((tm, ka), lambda j, i: (i, 0)),
            pl.BlockSpec((tm, kb), lambda j, i: (i, 0)),
            pl.BlockSpec((tail, ka), lambda j, i: (0, 0)),
            pl.BlockSpec((tail, kb), lambda j, i: (0, 0)),
            pl.BlockSpec((ka, tn), lambda j, i: (0, j)),
            pl.BlockSpec((kb, tn), lambda j, i: (0, j)),
            pl.BlockSpec((tm, tn), lambda j, i: (i, g0 + j)),
            pl.BlockSpec((tm, tn), lambda j, i: (i, g1 + j)),
            pl.BlockSpec((tail, tn), lambda j, i: (0, g0 + j)),
            pl.BlockSpec((tail, tn), lambda j, i: (0, g1 + j)),
            slab_in,
        ],
        out_specs=[pl.BlockSpec((tm, tn), lambda j, i: (i, j)), pl.BlockSpec((tail, tn), lambda j, i: (0, j)),
                   slab_out],
        out_shape=[jax.ShapeDtypeStruct((m_p, d), BF16), jax.ShapeDtypeStruct((tail, d), BF16), slab_shape],
        scratch_shapes=[pltpu.VMEM((ka, tn), BF16), pltpu.VMEM((kb, tn), BF16)],
        compiler_params=_params(
            2, _nbytes((tm, ka), BF16), _nbytes((tm, kb), BF16), _nbytes((tail, ka), BF16), _nbytes((tail, kb), BF16),
            _nbytes((ka, tn), F32), _nbytes((kb, tn), F32), 3 * _nbytes((tm, tn), BF16), 3 * _nbytes((tail, tn), BF16),
            slab_bytes,
            scratch_bytes=_nbytes((ka, tn), BF16) + _nbytes((kb, tn), BF16) + 3 * _nbytes((tm, tn), F32)),
        name="merge",
    )(oa_p, ob_p, oa_s, ob_s, wa, wb, proj_p, proj_p, proj_s, proj_s, w_next)


def _proj_res_norm_kernel(ap_ref, as_ref, w_ref, resp_ref, ress_ref, nw_ref, *out_refs, emit_sum):
    n_out = 2 if emit_sum else 1
    outs_p, outs_s = out_refs[:n_out], out_refs[n_out:]
    i = pl.program_id(0)
    k = pl.program_id(1)
    last_k = k == pl.num_programs(1) - 1
    d = w_ref.shape[1]
    col_chunk = min(d, 512)

    def step(a_ref, res_ref, outs):
        acc_ref = outs[0]
        nrow = acc_ref.shape[0]
        row_chunk = min(nrow, 128)
        assert nrow % row_chunk == 0

        @pl.when(k == 0)
        def _():
            acc_ref[...] = res_ref[...]

        a = a_ref[...]
        for c in range(d // col_chunk):
            cs = slice(c * col_chunk, (c + 1) * col_chunk)
            acc_ref[:, cs] += _dot(a, w_ref[:, cs])

        @pl.when(last_k)
        def _():
            def body(c, carry):
                rr = pl.ds(pl.multiple_of(c * row_chunk, row_chunk), row_chunk)
                y = _rmsnorm_rows(acc_ref[rr, :], nw_ref[...])
                if emit_sum:
                    outs[1][rr, :] = y.astype(outs[1].dtype)
                else:
                    acc_ref[rr, :] = y
                return carry

            lax.fori_loop(0, nrow // row_chunk, body, 0)

    step(ap_ref, resp_ref, outs_p)

    @pl.when(i == pl.num_programs(0) - 1)
    def _():
        step(as_ref, ress_ref, outs_s)


def _proj_res_norm(a_p, a_s, w, res_p, res_s, norm_w, emit_sum, tk):
    m_p, kdim = a_p.shape
    tail = a_s.shape[0]
    d = w.shape[1]
    tm = _row_tile(m_p, ROW_TILE)
    tk = min(kdim, tk)
    assert kdim % tk == 0
    p_spec = pl.BlockSpec((tm, d), lambda i, k: (i, 0))
    s_spec = pl.BlockSpec((tail, d), lambda i, k: (0, 0))
    out_specs = [p_spec, s_spec]
    out_shape = [jax.ShapeDtypeStruct((m_p, d), F32), jax.ShapeDtypeStruct((tail, d), F32)]
    assert w.dtype == BF16
    blocks = [_nbytes((tm, tk), BF16), _nbytes((tail, tk), BF16), _nbytes((tk, d), BF16),
              2 * _nbytes((tm, d), F32), 2 * _nbytes((tail, d), F32)]
    if emit_sum:
        out_specs = [p_spec, p_spec, s_spec, s_spec]
        out_shape = [out_shape[0], jax.ShapeDtypeStruct((m_p, d), BF16),
                     out_shape[1], jax.ShapeDtypeStruct((tail, d), BF16)]
        blocks += [_nbytes((tm, d), BF16), _nbytes((tail, d), BF16)]
    return pl.pallas_call(
        functools.partial(_proj_res_norm_kernel, emit_sum=emit_sum),
        grid=(m_p // tm, kdim // tk),
        in_specs=[
            pl.BlockSpec((tm, tk), lambda i, k: (i, k)),
            pl.BlockSpec((tail, tk), lambda i, k: (0, k)),
            pl.BlockSpec((tk, d), lambda i, k: (k, 0)),
            p_spec,
            s_spec,
            pl.BlockSpec((1, d), lambda i, k: (0, 0)),
        ],
        out_specs=out_specs,
        out_shape=out_shape,
        compiler_params=_params(2, *blocks),
        name="proj_res_norm",
    )(a_p, a_s, w, res_p, res_s, norm_w.reshape(1, d))


def _swiglu_kernel(hp_ref, hs_ref, wg_ref, wu_ref, wnext_ref, op_ref, os_ref, wnext_bf_ref, wg_bf, wu_bf):
    wnext_bf_ref[...] = wnext_ref[...].astype(BF16)

    tn = wg_bf.shape[1]
    col_chunk = min(tn, 256)

    def act(h_ref, o_ref):
        h = h_ref[...]
        for c in range(tn // col_chunk):
            cs = slice(c * col_chunk, (c + 1) * col_chunk)
            a = _dot(h, wg_bf[:, cs])
            b = _dot(h, wu_bf[:, cs])
            o_ref[:, cs] = (_silu(a) * b).astype(o_ref.dtype)

    @pl.when(pl.program_id(1) == 0)
    def _():
        wg_bf[...] = wg_ref[...].astype(BF16)
        wu_bf[...] = wu_ref[...].astype(BF16)
        act(hs_ref, os_ref)

    act(hp_ref, op_ref)


def _swiglu(h_p, h_s, wg, wu, w_next):
    m_p, d = h_p.shape
    tail = h_s.shape[0]
    f = wg.shape[1]
    tm = _row_tile(m_p, ROW_TILE_WIDE)
    tn = 512 if f % 512 == 0 else 256
    assert f % tn == 0
    n_m = m_p // tm
    slab_in, slab_out, slab_shape, slab_bytes = _slab_specs(w_next, (f // tn) * n_m, lambda j, i: j * n_m + i)
    return pl.pallas_call(
        _swiglu_kernel,
        grid=(f // tn, n_m),
        in_specs=[
            pl.BlockSpec((tm, d), lambda j, i: (i, 0)),
            pl.BlockSpec((tail, d), lambda j, i: (0, 0)),
            pl.BlockSpec((d, tn), lambda j, i: (0, j)),
            pl.BlockSpec((d, tn), lambda j, i: (0, j)),
            slab_in,
        ],
        out_specs=[pl.BlockSpec((tm, tn), lambda j, i: (i, j)), pl.BlockSpec((tail, tn), lambda j, i: (0, j)),
                   slab_out],
        out_shape=[jax.ShapeDtypeStruct((m_p, f), BF16), jax.ShapeDtypeStruct((tail, f), BF16), slab_shape],
        scratch_shapes=[pltpu.VMEM((d, tn), BF16), pltpu.VMEM((d, tn), BF16)],
        compiler_params=_params(
            2, _nbytes((tm, d), BF16), _nbytes((tail, d), BF16), 2 * _nbytes((d, tn), F32),
            _nbytes((tm, tn), BF16), _nbytes((tail, tn), BF16), slab_bytes,
            scratch_bytes=2 * _nbytes((d, tn), BF16) + 3 * _nbytes((tm, tn), F32)),
        name="swiglu",
    )(h_p, h_s, wg, wu, w_next)


def _layout(d_model, in_width, state_gla, state_ret, gate_rank):
    _, _, gh, gdk, gdv = state_gla.shape
    _, _, rh, rdk, rdv = state_ret.shape
    gqk, gv, rqk, rv = gh * gdk, gh * gdv, rh * rdk, rh * rdv
    lay = dict(gla_heads=gh, gla_dk=gdk, gla_dv=gdv, ret_heads=rh, ret_dk=rdk, ret_dv=rdv, rank=gate_rank)
    off = 0
    for name, width in (("qa", gqk), ("ka", gqk), ("va", gv), ("ga", gv), ("qb", rqk), ("kb", rqk),
                        ("vb", rv), ("gb", rv), ("mg", 2 * d_model)):
        lay[name] = off
        off += width
    lay["out_cols"] = off
    lay["plain_cols"] = 2 * gqk + gv
    lay["gd_src"] = lay["plain_cols"]
    assert lay["gd_src"] % V7X_LANES == 0 and gate_rank <= V7X_LANES
    assert in_width == off + gate_rank
    return lay


def _layer(x_p, x_s, st_gla, st_ret, wts, lay, tables, final_norm):
    (norm_mix, w_in, w_gate_up, b_gate, gla_norm_w, w_gla_up, ret_norm_w, w_ret_up, w_out, norm_ffn,
     w_ffn_gate, w_ffn_up, w_ffn_down) = wts
    batch, seq, d = x_p.shape
    rank = lay["rank"]
    gqk = lay["gla_heads"] * lay["gla_dk"]
    wup_pad = jnp.zeros((V7X_LANES, gqk), F32).at[:rank].set(w_gate_up)
    bup = b_gate.reshape(1, gqk)
    gnw = gla_norm_w.reshape(1, -1)
    rnw = ret_norm_w.reshape(1, -1)
    tn = 1024 if (lay["out_cols"] % 1024 == 0 and lay["plain_cols"] % 1024 == 0) else 512
    cos_p, sin_p, cos_s, sin_s, log_gamma = tables
    xp = x_p.reshape(batch * seq, d)
    xs = x_s.reshape(-1, d)
    w_in_t = w_in.T

    h_p, h_s, g_p, g_s = _rmsnorm_gate(xp, xs, norm_mix, w_in_t, lay["gd_src"], wup_pad, bup)
    proj_p, proj_s = _in_proj(h_p, h_s, w_in_t, lay["plain_cols"], rank, lay["out_cols"], tn)
    oa_p, sa_p, oa_s, sa_s = _gla(proj_p, g_p, proj_s, g_s, gnw, st_gla, lay, batch, seq)
    ob_p, sb_p, ob_s, sb_s = _ret(proj_p, cos_p, sin_p, proj_s, cos_s, sin_s, log_gamma, rnw, st_ret, lay, batch, seq)
    m_p, m_s, w_out_bf = _merge(oa_p, ob_p, oa_s, ob_s, w_gla_up, w_ret_up, proj_p, proj_s, lay, w_out)
    x1_p, h2_p, x1_s, h2_s = _proj_res_norm(m_p, m_s, w_out_bf, xp, xs, norm_ffn, True, 1024)
    act_p, act_s, w_down_bf = _swiglu(h2_p, h2_s, w_ffn_gate, w_ffn_up, w_ffn_down)
    y_p, y_s = _proj_res_norm(act_p, act_s, w_down_bf, x1_p, x1_s, final_norm, False, 1408)
    return (y_p, sa_p, sb_p), (y_s, sa_s, sb_s)


def kernel(x_prompt, x_sample, state_gla, state_ret, norm_mix, w_in, w_gla_gate_up, b_gla_gate, gla_norm_w,
           w_gla_up, ret_norm_w, w_ret_up, w_out, norm_ffn, w_ffn_gate, w_ffn_up, w_ffn_down, norm_final):
    depth = w_in.shape[0]
    assert depth == 1, "single-layer trunk"
    batch, seq, d = x_prompt.shape
    lay = _layout(d, w_in.shape[-1], state_gla, state_ret, w_gla_gate_up.shape[1])
    rh, rdk = lay["ret_heads"], lay["ret_dk"]
    half = rdk // 2
    assert half == V7X_LANES
    cos_p, sin_p = _rope_tables(seq, 0, half)
    cos_s, sin_s = _rope_tables(8, PAST_LEN, half)
    lg = jnp.log1p(-jnp.exp(jnp.linspace(math.log(1.0 / 32), math.log(1.0 / 512), rh))).astype(F32)
    log_gamma = jnp.broadcast_to(lg[:, None, None], (rh, 1, V7X_LANES))
    tables = (cos_p, sin_p, cos_s, sin_s, log_gamma)

    wts = (norm_mix[0], w_in[0], w_gla_gate_up[0], b_gla_gate[0], gla_norm_w[0], w_gla_up[0], ret_norm_w[0],
           w_ret_up[0], w_out[0], norm_ffn[0], w_ffn_gate[0], w_ffn_up[0], w_ffn_down[0])
    (y_p, ga_p, re_p), (y_s, ga_s, re_s) = _layer(
        x_prompt, x_sample, state_gla[0], state_ret[0], wts, lay, tables, norm_final)

    sd = state_gla.dtype
    return (y_p.reshape(batch, seq, d), y_s.reshape(x_sample.shape),
            ga_p[None].astype(sd), re_p[None].astype(state_ret.dtype),
            ga_s[None].astype(sd), re_s[None].astype(state_ret.dtype))
```

```python
import functools
import math

import numpy as np
import jax
import jax.numpy as jnp
from jax import lax
from jax.experimental import pallas as pl
from jax.experimental.pallas import tpu as pltpu

EPS = 1e-6
ROPE_BASE = 10000.0
GLA_GATE_NORM = 16.0
PAST_LEN = 16384

V7X_LANES = 128
V7X_VMEM_REQUEST_CAP = 60000 * 1024
COMPILER_SCRATCH_BYTES = 12 * 1024 * 1024

GLA_CHUNK = 64
GLA_STEP_CHUNKS = 8
LOG2_E = 1.4426950408889634
RET_CHUNK = 128
RET_STEP_CHUNKS = 4
ROW_TILE = 1024
ROW_TILE_WIDE = 2048

BF16 = jnp.bfloat16
F32 = jnp.float32


def _params(n_axes, *block_bytes, scratch_bytes=0, claim_all=False):
    need = 2 * sum(block_bytes) + scratch_bytes + COMPILER_SCRATCH_BYTES
    if claim_all:
        need = V7X_VMEM_REQUEST_CAP
    return pltpu.CompilerParams(
        dimension_semantics=("arbitrary",) * n_axes,
        vmem_limit_bytes=int(min(V7X_VMEM_REQUEST_CAP, need)),
    )


def _nbytes(shape, dtype):
    return int(np.prod(shape)) * jnp.dtype(dtype).itemsize


def _sigmoid(x):
    return 1.0 / (1.0 + jnp.exp(-x))


def _silu(x):
    return x * _sigmoid(x)


def _log_sigmoid(x):
    return jnp.minimum(x, 0.0) - jnp.log(1.0 + jnp.exp(-jnp.abs(x)))


def _dot(a, b):
    return jnp.dot(a, b, preferred_element_type=F32)


def _dot_nt(a, b):
    return lax.dot_general(a, b, (((1,), (1,)), ((), ())), preferred_element_type=F32)


def _dot_tn(a, b):
    return lax.dot_general(a, b, (((0,), (0,)), ((), ())), preferred_element_type=F32)


def _row_tile(m, want):
    t = min(m, want)
    assert m % t == 0, (m, t)
    return t


def _rmsnorm_rows(x, w):
    ms = jnp.mean(x * x, axis=-1, keepdims=True)
    return x * lax.rsqrt(ms + EPS) * w


def _rope_rows(cos_ref, sin_ref, pos0):
    rows, half = cos_ref.shape
    pos = (lax.broadcasted_iota(jnp.int32, (rows, half), 0) + pos0).astype(F32)
    idx = lax.broadcasted_iota(jnp.int32, (rows, half), 1).astype(F32)
    ang = pos * jnp.exp(idx * (-math.log(ROPE_BASE) / half))
    cos_ref[...] = jnp.cos(ang)
    sin_ref[...] = jnp.sin(ang)


def _rmsnorm_kernel(xp_ref, xs_ref, w_ref, wgd_ref, wup_ref, bup_ref,
                    hp_ref, hs_ref, gp_ref, gs_ref, cosp_ref, sinp_ref, cosd_ref, sind_ref, wgd_bf, wup_bf):
    i = pl.program_id(0)

    @pl.when(i == 0)
    def _():
        wgd_bf[...] = wgd_ref[...].astype(BF16)
        wup_bf[...] = jnp.zeros_like(wup_bf)
        wup_bf[0:wup_ref.shape[0], :] = wup_ref[...].astype(BF16)

    def rows(x, h_ref, g_ref):
        h = _rmsnorm_rows(x, w_ref[...]).astype(h_ref.dtype)
        h_ref[...] = h
        gd = _dot_nt(h, wgd_bf[...])
        x = _dot(gd.astype(BF16), wup_bf[...]) + bup_ref[...]
        g_ref[...] = _log_sigmoid(x) * (LOG2_E / GLA_GATE_NORM)

    rows(xp_ref[...], hp_ref, gp_ref)
    _rope_rows(cosp_ref, sinp_ref, i * cosp_ref.shape[0])

    @pl.when(i == 0)
    def _():
        rows(xs_ref[...], hs_ref, gs_ref)
        _rope_rows(cosd_ref, sind_ref, PAST_LEN)


def _rmsnorm_gate(x_p, x_s, w, w_in_t, gate_row0, w_gate_up, bup, seq, half):
    m_p, d = x_p.shape
    tail = x_s.shape[0]
    rank, gw = w_gate_up.shape
    tm = _row_tile(m_p, ROW_TILE)
    n_steps = m_p // tm
    assert gate_row0 % V7X_LANES == 0 and seq % n_steps == 0 and rank % 16 == 0
    pos_rows = seq // n_steps
    table = pl.BlockSpec((pos_rows, half), lambda i: (i, 0))
    table_dec = pl.BlockSpec((8, half), lambda i: (0, 0))
    return pl.pallas_call(
        _rmsnorm_kernel,
        grid=(n_steps,),
        in_specs=[
            pl.BlockSpec((tm, d), lambda i: (i, 0)),
            pl.BlockSpec((tail, None, d), lambda i: (0, 0, 0)),
            pl.BlockSpec((1, d), lambda i: (0, 0)),
            pl.BlockSpec((V7X_LANES, d), lambda i: (gate_row0 // V7X_LANES, 0)),
            pl.BlockSpec((rank, gw), lambda i: (0, 0)),
            pl.BlockSpec((1, gw), lambda i: (0, 0)),
        ],
        out_specs=[
            pl.BlockSpec((tm, d), lambda i: (i, 0)), pl.BlockSpec((tail, d), lambda i: (0, 0)),
            pl.BlockSpec((tm, gw), lambda i: (i, 0)), pl.BlockSpec((tail, gw), lambda i: (0, 0)),
            table, table, table_dec, table_dec,
        ],
        out_shape=[
            jax.ShapeDtypeStruct((m_p, d), BF16), jax.ShapeDtypeStruct((tail, d), BF16),
            jax.ShapeDtypeStruct((m_p, gw), F32), jax.ShapeDtypeStruct((tail, gw), F32),
            jax.ShapeDtypeStruct((seq, half), F32), jax.ShapeDtypeStruct((seq, half), F32),
            jax.ShapeDtypeStruct((8, half), F32), jax.ShapeDtypeStruct((8, half), F32),
        ],
        scratch_shapes=[pltpu.VMEM((V7X_LANES, d), BF16), pltpu.VMEM((V7X_LANES, gw), BF16)],
        compiler_params=_params(1, claim_all=True),
        name="rmsnorm_gate",
    )(x_p, x_s, w.reshape(1, d), w_in_t, w_gate_up, bup)


def _in_proj_kernel(hp_ref, hs_ref, wm_ref, wn_ref, op_ref, os_ref, wbf_ref, *, n_plain, shift):
    j = pl.program_id(0)
    i = pl.program_id(1)
    tn = wbf_ref.shape[0]

    @pl.when(jnp.logical_and(i == 0, j < n_plain))
    def _():
        wbf_ref[...] = wm_ref[...].astype(BF16)

    @pl.when(jnp.logical_and(i == 0, j >= n_plain))
    def _():
        wbf_ref[0:tn - shift, :] = wm_ref[shift:tn, :].astype(BF16)
        wbf_ref[tn - shift:tn, :] = wn_ref[...].astype(BF16)

    @pl.when(i == 0)
    def _():
        os_ref[...] = _dot_nt(hs_ref[...], wbf_ref[...]).astype(os_ref.dtype)

    op_ref[...] = _dot_nt(hp_ref[...], wbf_ref[...]).astype(op_ref.dtype)


def _in_proj(h_p, h_s, w_in_t, plain_cols, shift, out_cols, tn):
    m_p, d = h_p.shape
    tail = h_s.shape[0]
    tm = _row_tile(m_p, ROW_TILE_WIDE)
    assert plain_cols % tn == 0 and out_cols % tn == 0 and tn % shift == 0 and shift % 8 == 0
    n_plain = plain_cols // tn
    kern = functools.partial(_in_proj_kernel, n_plain=n_plain, shift=shift)
    return pl.pallas_call(
        kern,
        grid=(out_cols // tn, m_p // tm),
        in_specs=[
            pl.BlockSpec((tm, d), lambda j, i: (i, 0)),
            pl.BlockSpec((tail, d), lambda j, i: (0, 0)),
            pl.BlockSpec((tn, d), lambda j, i: (j, 0)),
            pl.BlockSpec((shift, d), lambda j, i: ((j + 1) * (tn // shift), 0)),
        ],
        out_specs=[pl.BlockSpec((tm, tn), lambda j, i: (i, j)), pl.BlockSpec((tail, tn), lambda j, i: (0, j))],
        out_shape=[jax.ShapeDtypeStruct((m_p, out_cols), BF16), jax.ShapeDtypeStruct((tail, out_cols), BF16)],
        scratch_shapes=[pltpu.VMEM((tn, d), BF16)],
        compiler_params=_params(
            2, _nbytes((tm, d), BF16), _nbytes((tail, d), BF16), _nbytes((tn, d), F32), _nbytes((shift, d), F32),
            _nbytes((tm, tn), BF16), _nbytes((tail, tn), BF16),
            scratch_bytes=_nbytes((tn, d), BF16) + _nbytes((tm, tn), F32)),
        name="in_proj",
    )(h_p, h_s, w_in_t, w_in_t)


def _prefix_sum_rows(sel3_bf16, g):
    g0 = g.astype(BF16)
    r1 = g - g0.astype(F32)
    g1 = r1.astype(BF16)
    g2 = (r1 - g1.astype(F32)).astype(BF16)
    return _dot(sel3_bf16, jnp.concatenate([g0, g1, g2], axis=0))


def _lane_bcast_cols(row, n):
    parts = []
    for c in range(n // V7X_LANES):
        tile = jnp.broadcast_to(row[:, c * V7X_LANES:(c + 1) * V7X_LANES], (V7X_LANES, V7X_LANES))
        parts.append(tile.T)
    return parts[0] if len(parts) == 1 else jnp.concatenate(parts, axis=0)


def _rms_gate_store(o, w, gate, out_ref, rows, cols):
    ms = jnp.mean(o * o, axis=-1, keepdims=True)
    y = o * lax.rsqrt(ms + EPS) * w
    out_ref[rows, cols] = (y * _silu(gate)).astype(out_ref.dtype)


def _ln_gate_store(o, w, gate, out_ref, rows, cols):
    mu = jnp.mean(o, axis=-1, keepdims=True)
    dlt = o - mu
    var = jnp.mean(dlt * dlt, axis=-1, keepdims=True)
    y = dlt * lax.rsqrt(var + EPS) * w
    out_ref[rows, cols] = (y * _silu(gate)).astype(out_ref.dtype)


def _decode_advance(tok0, decay_fn, k_ref, q_ref, v_ref, s_in_ref, s_out_ref, o_ref, *, heads, dk, dv):
    n_tok = s_in_ref.shape[0]
    reps = dv // V7X_LANES
    rows = pl.ds(pl.multiple_of(tok0, n_tok), n_tok)
    for hh in range(heads):
        kc = slice(hh * dk, (hh + 1) * dk)
        vc = slice(hh * dv, (hh + 1) * dv)
        k, q, v = k_ref[rows, kc], q_ref[rows, kc], v_ref[rows, vc]
        o_rows = []
        for tt in range(n_tok):
            k_cols = jnp.concatenate([_lane_bcast_cols(k[tt:tt + 1, :], dk)] * reps, axis=1)
            q_cols = jnp.concatenate([_lane_bcast_cols(q[tt:tt + 1, :], dk)] * reps, axis=1)
            s_new = decay_fn(rows, tt, hh) * s_in_ref[tt, hh] + k_cols * v[tt:tt + 1, :]
            s_out_ref[tt, hh] = s_new
            o_rows.append(jnp.sum(q_cols * s_new, axis=0, keepdims=True))
        o_ref[rows, vc] = jnp.concatenate(o_rows, axis=0)


def _decode_plan(n_dec, n_steps):
    assert n_dec % n_steps == 0 and (n_dec // n_steps) % 8 == 0, (n_dec, n_steps)
    return n_dec // n_steps


def _gla_sum_matrices(c):
    levels = c.bit_length() - 1
    assert 1 << levels == c
    i = lax.broadcasted_iota(jnp.int32, (c, c), 0)
    j = lax.broadcasted_iota(jnp.int32, (c, c), 1)
    mats = [j <= i]
    for l in range(levels):
        ref = jnp.bitwise_or(jnp.bitwise_and(i, -(2 << l)), 1 << l)
        mats.append(jnp.logical_and(j > jnp.minimum(i, ref), j <= jnp.maximum(i, ref)))
    mats.append(j > i)
    sel = jnp.concatenate([jnp.where(m, 1.0, 0.0).astype(BF16) for m in mats], axis=0)
    return jnp.concatenate([sel, sel, sel], axis=1)


def _pair_level(c):
    levels = c.bit_length() - 1
    i = lax.broadcasted_iota(jnp.int32, (c, c), 0)
    j = lax.broadcasted_iota(jnp.int32, (c, c), 1)
    x = jnp.bitwise_xor(i, j)
    lvl = jnp.zeros((c, c), jnp.int32)
    for l in range(1, levels):
        lvl = lvl + jnp.where(x >= (1 << l), 1, 0)
    return jnp.where(i > j, lvl, jnp.where(i == j, levels, -1))


def _queries_else_keys(q, k, l):
    c = q.shape[0]
    span = 1 << l
    if span >= 8:
        parts = [(q if (b & 1) else k)[b * span:(b + 1) * span, :] for b in range(c // span)]
        return jnp.concatenate(parts, axis=0)
    row = lax.broadcasted_iota(jnp.int32, q.shape, 0)
    return jnp.where(jnp.bitwise_and(row, span) != 0, q, k)


def _gla_level_scores(q, k, sums):
    c = q.shape[0]
    levels = c.bit_length() - 1
    out = []
    for l in range(levels):
        x = _queries_else_keys(q, k, l) * jnp.exp2(sums[(1 + l) * c:(2 + l) * c, :])
        xb = x.astype(BF16)
        out.append(_dot_nt(xb, xb))
    return out


def _gla_chunk_out(q, k, v, sums, level_scores, pair_level, state):
    c = q.shape[0]
    levels = c.bit_length() - 1
    scores = jnp.where(pair_level == levels, jnp.sum(q * k, axis=-1, keepdims=True), 0.0)
    for l in range(levels):
        scores = jnp.where(pair_level == l, level_scores[l], scores)
    o = _dot((q * jnp.exp2(sums[0:c, :])).astype(BF16), state.astype(BF16))
    return o + _dot(scores.astype(BF16), v)


def _gla_next_state(k, v, sums, state):
    c, dk = k.shape
    levels = c.bit_length() - 1
    k_tail = (k * jnp.exp2(sums[(levels + 1) * c:(levels + 2) * c, :])).astype(BF16)
    decay = _lane_bcast_cols(jnp.exp2(sums[c - 1:c, :]), dk)
    decay_full = jnp.concatenate([decay] * (v.shape[1] // V7X_LANES), axis=1)
    return decay_full * state + _dot_tn(k_tail, v)


def _gla_kernel(q_ref, k_ref, v_ref, ga_ref, g_ref, qd_ref, kd_ref, vd_ref, gad_ref, gd_ref, nw_ref, sd_in_ref,
                o_ref, s_out_ref, od_ref, sd_out_ref,
                s_ref, mats_ref, lvl_ref, a_dec, q_dec, k_dec, v_dec, o_dec, *, heads, dk, dv):
    t = pl.program_id(1)
    step = pl.program_id(0) * pl.num_programs(1) + t
    last_step = pl.num_programs(0) * pl.num_programs(1) - 1

    @pl.when(t == 0)
    def _():
        s_ref[...] = jnp.zeros_like(s_ref)
        mats_ref[...] = _gla_sum_matrices(GLA_CHUNK)
        lvl_ref[...] = _pair_level(GLA_CHUNK)

    @pl.when(step == 0)
    def _():
        a_dec[...] = jnp.exp2(gd_ref[...])
        q_dec[...] = qd_ref[...].astype(F32) * (dk ** -0.5)
        k_dec[...] = kd_ref[...].astype(F32)
        v_dec[...] = vd_ref[...].astype(F32)

    reps = dv // V7X_LANES

    def decay(rows, tt, hh):
        a = a_dec[rows, hh * dk:(hh + 1) * dk]
        return jnp.concatenate([_lane_bcast_cols(a[tt:tt + 1, :], dk)] * reps, axis=1)

    _decode_advance(step * sd_in_ref.shape[0], decay, k_dec, q_dec, v_dec, sd_in_ref, sd_out_ref, o_dec,
                    heads=heads, dk=dk, dv=dv)

    @pl.when(step == last_step)
    def _():
        n_dec = o_dec.shape[0]
        for hh in range(heads):
            vc = slice(hh * dv, (hh + 1) * dv)
            _rms_gate_store(o_dec[:, vc], nw_ref[...], gad_ref[:, vc].astype(F32), od_ref, slice(0, n_dec), vc)

    ct = q_ref.shape[0]

    kcs = [slice(hh * dk, (hh + 1) * dk) for hh in range(heads)]
    vcs = [slice(hh * dv, (hh + 1) * dv) for hh in range(heads)]
    group = 2 if (ct // GLA_CHUNK) % 2 == 0 else 1

    def chunk_group(cg, carry):
        pair_level = lvl_ref[...]
        rows, sums, qs, ks, lvl_scores = [], [], [], [], []
        for u in range(group):
            r = pl.ds(pl.multiple_of((cg * group + u) * GLA_CHUNK, GLA_CHUNK), GLA_CHUNK)
            rows.append(r)
            sums.append(_prefix_sum_rows(mats_ref[...], g_ref[r, :]))
            qs.append([q_ref[r, kc].astype(F32) * (dk ** -0.5) for kc in kcs])
            ks.append([k_ref[r, kc].astype(F32) for kc in kcs])
            lvl_scores.append([_gla_level_scores(qs[u][hh], ks[u][hh], sums[u][:, kcs[hh]]) for hh in range(heads)])
        outs = []
        for u in range(group):
            r = rows[u]
            outs.append([_gla_chunk_out(qs[u][hh], ks[u][hh], v_ref[r, vcs[hh]], sums[u][:, kcs[hh]],
                                        lvl_scores[u][hh], pair_level, s_ref[hh]) for hh in range(heads)])
            for hh in range(heads):
                s_ref[hh] = _gla_next_state(ks[u][hh], v_ref[r, vcs[hh]], sums[u][:, kcs[hh]], s_ref[hh])
        for u in range(group):
            for hh in range(heads):
                _rms_gate_store(outs[u][hh], nw_ref[...], ga_ref[rows[u], vcs[hh]].astype(F32), o_ref, rows[u],
                                vcs[hh])
        return carry

    lax.fori_loop(0, ct // GLA_CHUNK // group, chunk_group, 0)

    @pl.when(t == pl.num_programs(1) - 1)
    def _():
        s_out_ref[0] = s_ref[...]


def _gla(proj, log2_decay, proj_dec, log2_decay_dec, norm_w, state_dec, lay, batch, seq):
    heads, dk, dv = lay["gla_heads"], lay["gla_dk"], lay["gla_dv"]
    qk, vw = heads * dk, heads * dv
    ct = min(seq, GLA_STEP_CHUNKS * GLA_CHUNK)
    levels = GLA_CHUNK.bit_length() - 1
    assert seq % ct == 0 and ct % GLA_CHUNK == 0
    nt = seq // ct
    n_dec = proj_dec.shape[0]
    tps = _decode_plan(n_dec, batch * nt)
    row = lambda b, t: b * nt + t
    dec_in = lambda width, col: pl.BlockSpec((n_dec, width), lambda b, t: (0, col // width))
    dec_state = pl.BlockSpec((tps, heads, dk, dv), lambda b, t: (row(b, t), 0, 0, 0))
    kern = functools.partial(_gla_kernel, heads=heads, dk=dk, dv=dv)
    return pl.pallas_call(
        kern,
        grid=(batch, nt),
        in_specs=[
            pl.BlockSpec((ct, qk), lambda b, t: (row(b, t), lay["qa"] // qk)),
            pl.BlockSpec((ct, qk), lambda b, t: (row(b, t), lay["ka"] // qk)),
            pl.BlockSpec((ct, vw), lambda b, t: (row(b, t), lay["va"] // vw)),
            pl.BlockSpec((ct, vw), lambda b, t: (row(b, t), lay["ga"] // vw)),
            pl.BlockSpec((ct, qk), lambda b, t: (row(b, t), 0)),
            dec_in(qk, lay["qa"]), dec_in(qk, lay["ka"]), dec_in(vw, lay["va"]), dec_in(vw, lay["ga"]),
            dec_in(qk, 0),
            pl.BlockSpec((1, dv), lambda b, t: (0, 0)),
            dec_state,
        ],
        out_specs=[
            pl.BlockSpec((ct, vw), lambda b, t: (row(b, t), 0)),
            pl.BlockSpec((1, heads, dk, dv), lambda b, t: (b, 0, 0, 0)),
            pl.BlockSpec((n_dec, vw), lambda b, t: (0, 0)),
            dec_state,
        ],
        out_shape=[
            jax.ShapeDtypeStruct((batch * seq, vw), BF16),
            jax.ShapeDtypeStruct((batch, heads, dk, dv), F32),
            jax.ShapeDtypeStruct((n_dec, vw), BF16),
            jax.ShapeDtypeStruct(state_dec.shape, state_dec.dtype),
        ],
        scratch_shapes=[
            pltpu.VMEM((heads, dk, dv), F32),
            pltpu.VMEM(((levels + 2) * GLA_CHUNK, 3 * GLA_CHUNK), BF16),
            pltpu.VMEM((GLA_CHUNK, GLA_CHUNK), jnp.int32),
            pltpu.VMEM((n_dec, qk), F32), pltpu.VMEM((n_dec, qk), F32), pltpu.VMEM((n_dec, qk), F32),
            pltpu.VMEM((n_dec, vw), F32), pltpu.VMEM((n_dec, vw), F32),
        ],
        compiler_params=_params(2, claim_all=True),
        name="gla",
    )(proj, proj, proj, proj, log2_decay, proj_dec, proj_dec, proj_dec, proj_dec, log2_decay_dec, norm_w, state_dec)


def _rotary(x, cos, sin):
    half = x.shape[1] // 2
    x1, x2 = x[:, :half], x[:, half:]
    return jnp.concatenate([x1 * cos - x2 * sin, x1 * sin + x2 * cos], axis=1)


def _ret_kernel(q_ref, k_ref, v_ref, gb_ref, cos_ref, sin_ref, qd_ref, kd_ref, vd_ref, gbd_ref, cosd_ref, sind_ref,
                lg_ref, nw_ref, sd_in_ref,
                o_ref, s_out_ref, od_ref, sd_out_ref,
                s_ref, dmat_ref, qdec_ref, kdec_ref, q_dec, k_dec, v_dec, o_dec, *, heads, dk, dv, c):
    t = pl.program_id(1)
    step = pl.program_id(0) * pl.num_programs(1) + t
    last_step = pl.num_programs(0) * pl.num_programs(1) - 1

    @pl.when(step == 0)
    def _():
        cosd, sind = cosd_ref[0:1, :], sind_ref[0:1, :]
        for hh in range(heads):
            kc = slice(hh * dk, (hh + 1) * dk)
            q_dec[:, kc] = _rotary(qd_ref[:, kc].astype(F32), cosd, sind)
            k_dec[:, kc] = _rotary(kd_ref[:, kc].astype(F32), cosd, sind) * (dk ** -0.5)
        v_dec[...] = vd_ref[...].astype(F32)

    def decay(rows, tt, hh):
        return jnp.exp(jnp.concatenate([lg_ref[hh]] * (dv // V7X_LANES), axis=1))

    _decode_advance(step * sd_in_ref.shape[0], decay, k_dec, q_dec, v_dec, sd_in_ref, sd_out_ref, o_dec,
                    heads=heads, dk=dk, dv=dv)

    @pl.when(step == last_step)
    def _():
        n_dec = o_dec.shape[0]
        for hh in range(heads):
            vc = slice(hh * dv, (hh + 1) * dv)
            _ln_gate_store(o_dec[:, vc], nw_ref[...], gbd_ref[:, vc].astype(F32), od_ref, slice(0, n_dec), vc)

    @pl.when(t == 0)
    def _():
        s_ref[...] = jnp.zeros_like(s_ref)
        ri = lax.broadcasted_iota(jnp.int32, (c, c), 0)
        rj = lax.broadcasted_iota(jnp.int32, (c, c), 1)
        dist = (ri - rj).astype(F32)
        rowl = lax.broadcasted_iota(jnp.int32, (c, V7X_LANES), 0).astype(F32)
        for hh in range(heads):
            lg = lg_ref[hh]
            dmat_ref[hh] = jnp.exp(jnp.where(ri >= rj, dist * lg[:, :1], -jnp.inf))
            qdec_ref[hh] = jnp.exp((rowl + 1.0) * lg)
            kdec_ref[hh] = jnp.exp((float(c - 1) - rowl) * lg)

    ct = q_ref.shape[0]
    kcs = [slice(hh * dk, (hh + 1) * dk) for hh in range(heads)]
    vcs = [slice(hh * dv, (hh + 1) * dv) for hh in range(heads)]

    def chunk(ci, carry):
        rows = pl.ds(pl.multiple_of(ci * c, c), c)
        cos, sin = cos_ref[rows, :], sin_ref[rows, :]
        qrs = [_rotary(q_ref[rows, kc].astype(F32), cos, sin).astype(BF16) for kc in kcs]
        krs = [_rotary(k_ref[rows, kc].astype(F32), cos, sin) * (dk ** -0.5) for kc in kcs]
        scores = [_dot_nt(qrs[hh], krs[hh].astype(BF16)) * dmat_ref[hh] for hh in range(heads)]
        outs = []
        for hh in range(heads):
            qdec = jnp.concatenate([qdec_ref[hh]] * (dv // V7X_LANES), axis=1)
            o = qdec * _dot(qrs[hh], s_ref[hh].astype(BF16))
            outs.append(o + _dot(scores[hh].astype(BF16), v_ref[rows, vcs[hh]]))
        for hh in range(heads):
            kdec = jnp.concatenate([kdec_ref[hh]] * (dk // V7X_LANES), axis=1)
            k_tail = (krs[hh] * kdec).astype(BF16)
            lgv = jnp.concatenate([lg_ref[hh]] * (dv // V7X_LANES), axis=1)
            s_ref[hh] = jnp.exp(float(c) * lgv) * s_ref[hh] + _dot_tn(k_tail, v_ref[rows, vcs[hh]])
        for hh in range(heads):
            _ln_gate_store(outs[hh], nw_ref[...], gb_ref[rows, vcs[hh]].astype(F32), o_ref, rows, vcs[hh])
        return carry

    lax.fori_loop(0, ct // c, chunk, 0)

    @pl.when(t == pl.num_programs(1) - 1)
    def _():
        s_out_ref[0] = s_ref[...]


def _ret(proj, cos, sin, proj_dec, cos_dec, sin_dec, log_gamma, norm_w, state_dec, lay, batch, seq):
    heads, dk, dv = lay["ret_heads"], lay["ret_dk"], lay["ret_dv"]
    qk, vw = heads * dk, heads * dv
    c = min(seq, RET_CHUNK)
    ct = min(seq, RET_STEP_CHUNKS * c)
    assert seq % ct == 0 and ct % c == 0
    nt = seq // ct
    half = dk // 2
    n_dec = proj_dec.shape[0]
    tps = _decode_plan(n_dec, batch * nt)
    row = lambda b, t: b * nt + t
    dec_in = lambda width, col: pl.BlockSpec((n_dec, width), lambda b, t: (0, col // width))
    dec_state = pl.BlockSpec((tps, heads, dk, dv), lambda b, t: (row(b, t), 0, 0, 0))
    table_dec = pl.BlockSpec((cos_dec.shape[0], half), lambda b, t: (0, 0))
    kern = functools.partial(_ret_kernel, heads=heads, dk=dk, dv=dv, c=c)
    return pl.pallas_call(
        kern,
        grid=(batch, nt),
        in_specs=[
            pl.BlockSpec((ct, qk), lambda b, t: (row(b, t), lay["qb"] // qk)),
            pl.BlockSpec((ct, qk), lambda b, t: (row(b, t), lay["kb"] // qk)),
            pl.BlockSpec((ct, vw), lambda b, t: (row(b, t), lay["vb"] // vw)),
            pl.BlockSpec((ct, vw), lambda b, t: (row(b, t), lay["gb"] // vw)),
            pl.BlockSpec((ct, half), lambda b, t: (t, 0)),
            pl.BlockSpec((ct, half), lambda b, t: (t, 0)),
            dec_in(qk, lay["qb"]), dec_in(qk, lay["kb"]), dec_in(vw, lay["vb"]), dec_in(vw, lay["gb"]),
            table_dec, table_dec,
            pl.BlockSpec((heads, 1, V7X_LANES), lambda b, t: (0, 0, 0)),
            pl.BlockSpec((1, dv), lambda b, t: (0, 0)),
            dec_state,
        ],
        out_specs=[
            pl.BlockSpec((ct, vw), lambda b, t: (row(b, t), 0)),
            pl.BlockSpec((1, heads, dk, dv), lambda b, t: (b, 0, 0, 0)),
            pl.BlockSpec((n_dec, vw), lambda b, t: (0, 0)),
            dec_state,
        ],
        out_shape=[
            jax.ShapeDtypeStruct((batch * seq, vw), BF16),
            jax.ShapeDtypeStruct((batch, heads, dk, dv), F32),
            jax.ShapeDtypeStruct((n_dec, vw), BF16),
            jax.ShapeDtypeStruct(state_dec.shape, state_dec.dtype),
        ],
        scratch_shapes=[
            pltpu.VMEM((heads, dk, dv), F32),
            pltpu.VMEM((heads, c, c), F32),
            pltpu.VMEM((heads, c, V7X_LANES), F32),
            pltpu.VMEM((heads, c, V7X_LANES), F32),
            pltpu.VMEM((n_dec, qk), F32), pltpu.VMEM((n_dec, qk), F32),
            pltpu.VMEM((n_dec, vw), F32), pltpu.VMEM((n_dec, vw), F32),
        ],
        compiler_params=_params(2, claim_all=True),
        name="ret",
    )(proj, proj, proj, proj, cos, sin, proj_dec, proj_dec, proj_dec, proj_dec, cos_dec, sin_dec,
      log_gamma, norm_w, state_dec)


def _merge_kernel(oap_ref, obp_ref, oas_ref, obs_ref, wa_ref, wb_ref, g0p_ref, g1p_ref, g0s_ref, g1s_ref, wnext_ref,
                  mp_ref, ms_ref, wnext_bf_ref, wa_bf, wb_bf):
    wnext_bf_ref[...] = wnext_ref[...].astype(BF16)

    def merged(oa, ob, g0, g1):
        ya = _dot(oa, wa_bf[...])
        yb = _dot(ob, wb_bf[...])
        return _sigmoid(g0.astype(F32)) * ya + _sigmoid(g1.astype(F32)) * yb

    @pl.when(pl.program_id(1) == 0)
    def _():
        wa_bf[...] = wa_ref[...].astype(BF16)
        wb_bf[...] = wb_ref[...].astype(BF16)
        ms_ref[...] = merged(oas_ref[...], obs_ref[...], g0s_ref[...], g1s_ref[...]).astype(ms_ref.dtype)

    mp_ref[...] = merged(oap_ref[...], obp_ref[...], g0p_ref[...], g1p_ref[...]).astype(mp_ref.dtype)


def _slab_specs(w_next, n_steps, step_of):
    kn, dn = w_next.shape
    assert kn % n_steps == 0 and (kn // n_steps) % 16 == 0, (kn, n_steps)
    slab = kn // n_steps
    spec = pl.BlockSpec((slab, dn), lambda j, i: (step_of(j, i), 0))
    return spec, spec, jax.ShapeDtypeStruct((kn, dn), BF16), _nbytes((slab, dn), F32) + _nbytes((slab, dn), BF16)


def _merge(oa_p, ob_p, oa_s, ob_s, wa, wb, proj_p, proj_s, lay, w_next):
    m_p, ka = oa_p.shape
    kb = ob_p.shape[1]
    tail = oa_s.shape[0]
    d = wa.shape[1]
    tm = _row_tile(m_p, ROW_TILE_WIDE)
    tn = min(d, 512)
    assert d % tn == 0 and lay["mg"] % tn == 0
    g0 = lay["mg"] // tn
    g1 = (lay["mg"] + d) // tn
    n_m = m_p // tm
    slab_in, slab_out, slab_shape, slab_bytes = _slab_specs(w_next, (d // tn) * n_m, lambda j, i: j * n_m + i)
    return pl.pallas_call(
        _merge_kernel,
        grid=(d // tn, n_m),
        in_specs=[
            pl.BlockSpec((tm, ka), lambda j, i: (i, 0)),
            pl.BlockSpec((tm, kb), lambda j, i: (i, 0)),
            pl.BlockSpec((tail, ka), lambda j, i: (0, 0)),
            pl.BlockSpec((tail, kb), lambda j, i: (0, 0)),
            pl.BlockSpec((ka, tn), lambda j, i: (0, j)),
            pl.BlockSpec((kb, tn), lambda j, i: (0, j)),
            pl.BlockSpec((tm, tn), lambda j, i: (i, g0 + j)),
            pl.BlockSpec((tm, tn), lambda j, i: (i, g1 + j)),
            pl.BlockSpec((tail, tn), lambda j, i: (0, g0 + j)),
            pl.BlockSpec((tail, tn), lambda j, i: (0, g1 + j)),
            slab_in,
        ],
        out_specs=[pl.BlockSpec((tm, tn), lambda j, i: (i, j)), pl.BlockSpec((tail, tn), lambda j, i: (0, j)),
                   slab_out],
        out_shape=[jax.ShapeDtypeStruct((m_p, d), BF16), jax.ShapeDtypeStruct((tail, d), BF16), slab_shape],
        scratch_shapes=[pltpu.VMEM((ka, tn), BF16), pltpu.VMEM((kb, tn), BF16)],
        compiler_params=_params(
            2, _nbytes((tm, ka), BF16), _nbytes((tm, kb), BF16), _nbytes((tail, ka), BF16), _nbytes((tail, kb), BF16),
            _nbytes((ka, tn), F32), _nbytes((kb, tn), F32), 3 * _nbytes((tm, tn), BF16), 3 * _nbytes((tail, tn), BF16),
            slab_bytes,
            scratch_bytes=_nbytes((ka, tn), BF16) + _nbytes((kb, tn), BF16) + 3 * _nbytes((tm, tn), F32)),
        name="merge",
    )(oa_p, ob_p, oa_s, ob_s, wa, wb, proj_p, proj_p, proj_s, proj_s, w_next)


def _proj_res_norm_kernel(ap_ref, as_ref, w_ref, resp_ref, ress_ref, nw_ref, *out_refs, emit_sum):
    n_out = 2 if emit_sum else 1
    outs_p, outs_s = out_refs[:n_out], out_refs[n_out:]
    i = pl.program_id(0)
    k = pl.program_id(1)
    last_k = k == pl.num_programs(1) - 1
    d = w_ref.shape[1]
    col_chunk = min(d, 512)

    def step(a_ref, res_ref, outs):
        acc_ref = outs[0]
        nrow = acc_ref.shape[0]
        row_chunk = min(nrow, 128)
        assert nrow % row_chunk == 0

        @pl.when(k == 0)
        def _():
            acc_ref[...] = res_ref[...]

        a = a_ref[...]
        for c in range(d // col_chunk):
            cs = slice(c * col_chunk, (c + 1) * col_chunk)
            acc_ref[:, cs] += _dot(a, w_ref[:, cs])

        @pl.when(last_k)
        def _():
            def body(c, carry):
                rr = pl.ds(pl.multiple_of(c * row_chunk, row_chunk), row_chunk)
                y = _rmsnorm_rows(acc_ref[rr, :], nw_ref[...])
                if emit_sum:
                    outs[1][rr, :] = y.astype(outs[1].dtype)
                else:
                    acc_ref[rr, :] = y
                return carry

            lax.fori_loop(0, nrow // row_chunk, body, 0)

    step(ap_ref, resp_ref, outs_p)

    @pl.when(i == pl.num_programs(0) - 1)
    def _():
        step(as_ref, ress_ref, outs_s)


def _proj_res_norm(a_p, a_s, w, res_p, res_s, norm_w, emit_sum, tk):
    m_p, kdim = a_p.shape
    tail = a_s.shape[0]
    d = w.shape[1]
    tm = _row_tile(m_p, ROW_TILE)
    tk = min(kdim, tk)
    assert kdim % tk == 0
    p_spec = pl.BlockSpec((tm, d), lambda i, k: (i, 0))
    s_spec = pl.BlockSpec((tail, d), lambda i, k: (0, 0))
    s3_spec = pl.BlockSpec((tail, None, d), lambda i, k: (0, 0, 0))
    res_s_spec = s3_spec if res_s.ndim == 3 else s_spec
    out_specs = [p_spec, s3_spec]
    out_shape = [jax.ShapeDtypeStruct((m_p, d), F32), jax.ShapeDtypeStruct((tail, 1, d), F32)]
    assert w.dtype == BF16
    blocks = [_nbytes((tm, tk), BF16), _nbytes((tail, tk), BF16), _nbytes((tk, d), BF16),
              2 * _nbytes((tm, d), F32), 2 * _nbytes((tail, d), F32)]
    if emit_sum:
        out_specs = [p_spec, p_spec, s_spec, s_spec]
        out_shape = [out_shape[0], jax.ShapeDtypeStruct((m_p, d), BF16),
                     jax.ShapeDtypeStruct((tail, d), F32), jax.ShapeDtypeStruct((tail, d), BF16)]
        blocks += [_nbytes((tm, d), BF16), _nbytes((tail, d), BF16)]
    return pl.pallas_call(
        functools.partial(_proj_res_norm_kernel, emit_sum=emit_sum),
        grid=(m_p // tm, kdim // tk),
        in_specs=[
            pl.BlockSpec((tm, tk), lambda i, k: (i, k)),
            pl.BlockSpec((tail, tk), lambda i, k: (0, k)),
            pl.BlockSpec((tk, d), lambda i, k: (k, 0)),
            p_spec,
            res_s_spec,
            pl.BlockSpec((1, d), lambda i, k: (0, 0)),
        ],
        out_specs=out_specs,
        out_shape=out_shape,
        compiler_params=_params(2, *blocks),
        name="proj_res_norm",
    )(a_p, a_s, w, res_p, res_s, norm_w.reshape(1, d))


def _swiglu_kernel(hp_ref, hs_ref, wg_ref, wu_ref, wnext_ref, op_ref, os_ref, wnext_bf_ref, wg_bf, wu_bf):
    wnext_bf_ref[...] = wnext_ref[...].astype(BF16)

    tn = wg_bf.shape[1]
    col_chunk = min(tn, 256)

    def act(h_ref, o_ref):
        h = h_ref[...]
        for c in range(tn // col_chunk):
            cs = slice(c * col_chunk, (c + 1) * col_chunk)
            a = _dot(h, wg_bf[:, cs])
            b = _dot(h, wu_bf[:, cs])
            o_ref[:, cs] = (_silu(a) * b).astype(o_ref.dtype)

    @pl.when(pl.program_id(1) == 0)
    def _():
        wg_bf[...] = wg_ref[...].astype(BF16)
        wu_bf[...] = wu_ref[...].astype(BF16)
        act(hs_ref, os_ref)

    act(hp_ref, op_ref)


def _swiglu(h_p, h_s, wg, wu, w_next):
    m_p, d = h_p.shape
    tail = h_s.shape[0]
    f = wg.shape[1]
    tm = _row_tile(m_p, ROW_TILE_WIDE)
    tn = 512 if f % 512 == 0 else 256
    assert f % tn == 0
    n_m = m_p // tm
    slab_in, slab_out, slab_shape, slab_bytes = _slab_specs(w_next, (f // tn) * n_m, lambda j, i: j * n_m + i)
    return pl.pallas_call(
        _swiglu_kernel,
        grid=(f // tn, n_m),
        in_specs=[
            pl.BlockSpec((tm, d), lambda j, i: (i, 0)),
            pl.BlockSpec((tail, d), lambda j, i: (0, 0)),
            pl.BlockSpec((d, tn), lambda j, i: (0, j)),
            pl.BlockSpec((d, tn), lambda j, i: (0, j)),
            slab_in,
        ],
        out_specs=[pl.BlockSpec((tm, tn), lambda j, i: (i, j)), pl.BlockSpec((tail, tn), lambda j, i: (0, j)),
                   slab_out],
        out_shape=[jax.ShapeDtypeStruct((m_p, f), BF16), jax.ShapeDtypeStruct((tail, f), BF16), slab_shape],
        scratch_shapes=[pltpu.VMEM((d, tn), BF16), pltpu.VMEM((d, tn), BF16)],
        compiler_params=_params(
            2, _nbytes((tm, d), BF16), _nbytes((tail, d), BF16), 2 * _nbytes((d, tn), F32),
            _nbytes((tm, tn), BF16), _nbytes((tail, tn), BF16), slab_bytes,
            scratch_bytes=2 * _nbytes((d, tn), BF16) + 3 * _nbytes((tm, tn), F32)),
        name="swiglu",
    )(h_p, h_s, wg, wu, w_next)


def _layout(d_model, in_width, state_gla, state_ret, gate_rank):
    _, _, gh, gdk, gdv = state_gla.shape
    _, _, rh, rdk, rdv = state_ret.shape
    gqk, gv, rqk, rv = gh * gdk, gh * gdv, rh * rdk, rh * rdv
    lay = dict(gla_heads=gh, gla_dk=gdk, gla_dv=gdv, ret_heads=rh, ret_dk=rdk, ret_dv=rdv, rank=gate_rank)
    off = 0
    for name, width in (("qa", gqk), ("ka", gqk), ("va", gv), ("ga", gv), ("qb", rqk), ("kb", rqk),
                        ("vb", rv), ("gb", rv), ("mg", 2 * d_model)):
        lay[name] = off
        off += width
    lay["out_cols"] = off
    lay["plain_cols"] = 2 * gqk + gv
    lay["gd_src"] = lay["plain_cols"]
    assert lay["gd_src"] % V7X_LANES == 0 and gate_rank <= V7X_LANES
    assert in_width == off + gate_rank
    return lay


def _layer(x_p, x_s, st_gla, st_ret, wts, lay, log_gamma, final_norm):
    (norm_mix, w_in, w_gate_up, b_gate, gla_norm_w, w_gla_up, ret_norm_w, w_ret_up, w_out, norm_ffn,
     w_ffn_gate, w_ffn_up, w_ffn_down) = wts
    batch, seq, d = x_p.shape
    rank = lay["rank"]
    gqk = lay["gla_heads"] * lay["gla_dk"]
    bup = b_gate.reshape(1, gqk)
    gnw = gla_norm_w.reshape(1, -1)
    rnw = ret_norm_w.reshape(1, -1)
    tn = 1024 if (lay["out_cols"] % 1024 == 0 and lay["plain_cols"] % 1024 == 0) else 512
    xp = x_p.reshape(batch * seq, d)
    assert x_s.ndim == 3 and x_s.shape[1] == 1, "one new token per decode sequence"
    xs = x_s
    w_in_t = w_in.T

    h_p, h_s, g_p, g_s, cos_p, sin_p, cos_s, sin_s = _rmsnorm_gate(
        xp, xs, norm_mix, w_in_t, lay["gd_src"], w_gate_up, bup, seq, lay["ret_dk"] // 2)
    proj_p, proj_s = _in_proj(h_p, h_s, w_in_t, lay["plain_cols"], rank, lay["out_cols"], tn)
    oa_p, sa_p, oa_s, sa_s = _gla(proj_p, g_p, proj_s, g_s, gnw, st_gla, lay, batch, seq)
    ob_p, sb_p, ob_s, sb_s = _ret(proj_p, cos_p, sin_p, proj_s, cos_s, sin_s, log_gamma, rnw, st_ret, lay, batch, seq)
    m_p, m_s, w_out_bf = _merge(oa_p, ob_p, oa_s, ob_s, w_gla_up, w_ret_up, proj_p, proj_s, lay, w_out)
    x1_p, h2_p, x1_s, h2_s = _proj_res_norm(m_p, m_s, w_out_bf, xp, xs, norm_ffn, True, 1024)
    act_p, act_s, w_down_bf = _swiglu(h2_p, h2_s, w_ffn_gate, w_ffn_up, w_ffn_down)
    y_p, y_s = _proj_res_norm(act_p, act_s, w_down_bf, x1_p, x1_s, final_norm, False, 1408)
    return (y_p, sa_p, sb_p), (y_s, sa_s, sb_s)


def kernel(x_prompt, x_sample, state_gla, state_ret, norm_mix, w_in, w_gla_gate_up, b_gla_gate, gla_norm_w,
           w_gla_up, ret_norm_w, w_ret_up, w_out, norm_ffn, w_ffn_gate, w_ffn_up, w_ffn_down, norm_final):
    depth = w_in.shape[0]
    assert depth == 1, "single-layer trunk"
    batch, seq, d = x_prompt.shape
    lay = _layout(d, w_in.shape[-1], state_gla, state_ret, w_gla_gate_up.shape[1])
    rh, rdk = lay["ret_heads"], lay["ret_dk"]
    assert rdk // 2 == V7X_LANES
    lg = jnp.log1p(-jnp.exp(jnp.linspace(math.log(1.0 / 32), math.log(1.0 / 512), rh))).astype(F32)
    log_gamma = jnp.broadcast_to(lg[:, None, None], (rh, 1, V7X_LANES))

    wts = (norm_mix[0], w_in[0], w_gla_gate_up[0], b_gla_gate[0], gla_norm_w[0], w_gla_up[0], ret_norm_w[0],
           w_ret_up[0], w_out[0], norm_ffn[0], w_ffn_gate[0], w_ffn_up[0], w_ffn_down[0])
    (y_p, ga_p, re_p), (y_s, ga_s, re_s) = _layer(
        x_prompt, x_sample, state_gla[0], state_ret[0], wts, lay, log_gamma, norm_final)

    sd = state_gla.dtype
    return (y_p.reshape(batch, seq, d), y_s.reshape(x_sample.shape),
            ga_p[None].astype(sd), re_p[None].astype(state_ret.dtype),
            ga_s[None].astype(sd), re_s[None].astype(state_ret.dtype))
```

```python
import functools
import math

import numpy as np
import jax
import jax.numpy as jnp
from jax import lax
from jax.experimental import pallas as pl
from jax.experimental.pallas import tpu as pltpu

EPS = 1e-6
ROPE_BASE = 10000.0
GLA_GATE_NORM = 16.0
PAST_LEN = 16384

V7X_LANES = 128
V7X_VMEM_REQUEST_CAP = 60000 * 1024
COMPILER_SCRATCH_BYTES = 12 * 1024 * 1024

GLA_CHUNK = 64
GLA_STEP_CHUNKS = 8
LOG2_E = 1.4426950408889634
RET_CHUNK = 128
RET_STEP_CHUNKS = 4
ROW_TILE = 1024
ROW_TILE_WIDE = 2048

BF16 = jnp.bfloat16
F32 = jnp.float32


def _params(n_axes, *block_bytes, scratch_bytes=0, claim_all=False):
    need = 2 * sum(block_bytes) + scratch_bytes + COMPILER_SCRATCH_BYTES
    if claim_all:
        need = V7X_VMEM_REQUEST_CAP
    return pltpu.CompilerParams(
        dimension_semantics=("arbitrary",) * n_axes,
        vmem_limit_bytes=int(min(V7X_VMEM_REQUEST_CAP, need)),
    )


def _nbytes(shape, dtype):
    return int(np.prod(shape)) * jnp.dtype(dtype).itemsize


def _sigmoid(x):
    return 1.0 / (1.0 + jnp.exp(-x))


def _silu(x):
    return x * _sigmoid(x)


def _log_sigmoid(x):
    return jnp.minimum(x, 0.0) - jnp.log(1.0 + jnp.exp(-jnp.abs(x)))


def _dot(a, b):
    return jnp.dot(a, b, preferred_element_type=F32)


def _dot_nt(a, b):
    return lax.dot_general(a, b, (((1,), (1,)), ((), ())), preferred_element_type=F32)


def _dot_tn(a, b):
    return lax.dot_general(a, b, (((0,), (0,)), ((), ())), preferred_element_type=F32)


def _row_tile(m, want):
    t = min(m, want)
    assert m % t == 0, (m, t)
    return t


def _rmsnorm_rows(x, w):
    ms = jnp.mean(x * x, axis=-1, keepdims=True)
    return x * lax.rsqrt(ms + EPS) * w


def _rope_rows(cos_ref, sin_ref, pos0):
    rows, half = cos_ref.shape
    pos = (lax.broadcasted_iota(jnp.int32, (rows, half), 0) + pos0).astype(F32)
    idx = lax.broadcasted_iota(jnp.int32, (rows, half), 1).astype(F32)
    ang = pos * jnp.exp(idx * (-math.log(ROPE_BASE) / half))
    cos_ref[...] = jnp.cos(ang)
    sin_ref[...] = jnp.sin(ang)


def _rmsnorm_kernel(xp_ref, xs_ref, w_ref, wgd_ref, wup_ref, bup_ref,
                    hp_ref, hs_ref, gp_ref, gs_ref, cosp_ref, sinp_ref, cosd_ref, sind_ref, wgd_bf, wup_bf):
    i = pl.program_id(0)

    @pl.when(i == 0)
    def _():
        wgd_bf[...] = wgd_ref[...].astype(BF16)
        wup_bf[...] = jnp.zeros_like(wup_bf)
        wup_bf[0:wup_ref.shape[0], :] = wup_ref[...].astype(BF16)

    def rows(x, h_ref, g_ref):
        h = _rmsnorm_rows(x, w_ref[...]).astype(h_ref.dtype)
        h_ref[...] = h
        gd = _dot_nt(h, wgd_bf[...])
        x = _dot(gd.astype(BF16), wup_bf[...]) + bup_ref[...]
        g_ref[...] = _log_sigmoid(x) * (LOG2_E / GLA_GATE_NORM)

    rows(xp_ref[...], hp_ref, gp_ref)
    _rope_rows(cosp_ref, sinp_ref, i * cosp_ref.shape[0])

    @pl.when(i == 0)
    def _():
        rows(xs_ref[...], hs_ref, gs_ref)
        _rope_rows(cosd_ref, sind_ref, PAST_LEN)


def _rmsnorm_gate(x_p, x_s, w, w_in_t, gate_row0, w_gate_up, bup, seq, half):
    m_p, d = x_p.shape
    tail = x_s.shape[0]
    rank, gw = w_gate_up.shape
    tm = _row_tile(m_p, ROW_TILE)
    n_steps = m_p // tm
    assert gate_row0 % V7X_LANES == 0 and seq % n_steps == 0 and rank % 16 == 0
    pos_rows = seq // n_steps
    table = pl.BlockSpec((pos_rows, half), lambda i: (i, 0))
    table_dec = pl.BlockSpec((8, half), lambda i: (0, 0))
    return pl.pallas_call(
        _rmsnorm_kernel,
        grid=(n_steps,),
        in_specs=[
            pl.BlockSpec((tm, d), lambda i: (i, 0)),
            pl.BlockSpec((tail, None, d), lambda i: (0, 0, 0)),
            pl.BlockSpec((1, d), lambda i: (0, 0)),
            pl.BlockSpec((V7X_LANES, d), lambda i: (gate_row0 // V7X_LANES, 0)),
            pl.BlockSpec((rank, gw), lambda i: (0, 0)),
            pl.BlockSpec((1, gw), lambda i: (0, 0)),
        ],
        out_specs=[
            pl.BlockSpec((tm, d), lambda i: (i, 0)), pl.BlockSpec((tail, d), lambda i: (0, 0)),
            pl.BlockSpec((tm, gw), lambda i: (i, 0)), pl.BlockSpec((tail, gw), lambda i: (0, 0)),
            table, table, table_dec, table_dec,
        ],
        out_shape=[
            jax.ShapeDtypeStruct((m_p, d), BF16), jax.ShapeDtypeStruct((tail, d), BF16),
            jax.ShapeDtypeStruct((m_p, gw), F32), jax.ShapeDtypeStruct((tail, gw), F32),
            jax.ShapeDtypeStruct((seq, half), F32), jax.ShapeDtypeStruct((seq, half), F32),
            jax.ShapeDtypeStruct((8, half), F32), jax.ShapeDtypeStruct((8, half), F32),
        ],
        scratch_shapes=[pltpu.VMEM((V7X_LANES, d), BF16), pltpu.VMEM((V7X_LANES, gw), BF16)],
        compiler_params=_params(1, claim_all=True),
        name="rmsnorm_gate",
    )(x_p, x_s, w.reshape(1, d), w_in_t, w_gate_up, bup)


def _in_proj_kernel(hp_ref, hs_ref, wm_ref, wn_ref, op_ref, os_ref, wbf_ref, *, n_plain, shift):
    j = pl.program_id(0)
    i = pl.program_id(1)
    tn = wbf_ref.shape[0]

    @pl.when(jnp.logical_and(i == 0, j < n_plain))
    def _():
        wbf_ref[...] = wm_ref[...].astype(BF16)

    @pl.when(jnp.logical_and(i == 0, j >= n_plain))
    def _():
        wbf_ref[0:tn - shift, :] = wm_ref[shift:tn, :].astype(BF16)
        wbf_ref[tn - shift:tn, :] = wn_ref[...].astype(BF16)

    @pl.when(i == 0)
    def _():
        os_ref[...] = _dot_nt(hs_ref[...], wbf_ref[...]).astype(os_ref.dtype)

    op_ref[...] = _dot_nt(hp_ref[...], wbf_ref[...]).astype(op_ref.dtype)


def _in_proj(h_p, h_s, w_in_t, plain_cols, shift, out_cols, tn):
    m_p, d = h_p.shape
    tail = h_s.shape[0]
    tm = _row_tile(m_p, ROW_TILE_WIDE)
    assert plain_cols % tn == 0 and out_cols % tn == 0 and tn % shift == 0 and shift % 8 == 0
    n_plain = plain_cols // tn
    kern = functools.partial(_in_proj_kernel, n_plain=n_plain, shift=shift)
    return pl.pallas_call(
        kern,
        grid=(out_cols // tn, m_p // tm),
        in_specs=[
            pl.BlockSpec((tm, d), lambda j, i: (i, 0)),
            pl.BlockSpec((tail, d), lambda j, i: (0, 0)),
            pl.BlockSpec((tn, d), lambda j, i: (j, 0)),
            pl.BlockSpec((shift, d), lambda j, i: ((j + 1) * (tn // shift), 0)),
        ],
        out_specs=[pl.BlockSpec((tm, tn), lambda j, i: (i, j)), pl.BlockSpec((tail, tn), lambda j, i: (0, j))],
        out_shape=[jax.ShapeDtypeStruct((m_p, out_cols), BF16), jax.ShapeDtypeStruct((tail, out_cols), BF16)],
        scratch_shapes=[pltpu.VMEM((tn, d), BF16)],
        compiler_params=_params(
            2, _nbytes((tm, d), BF16), _nbytes((tail, d), BF16), _nbytes((tn, d), F32), _nbytes((shift, d), F32),
            _nbytes((tm, tn), BF16), _nbytes((tail, tn), BF16),
            scratch_bytes=_nbytes((tn, d), BF16) + _nbytes((tm, tn), F32)),
        name="in_proj",
    )(h_p, h_s, w_in_t, w_in_t)


def _prefix_sum_rows(sel3_bf16, g):
    g0 = g.astype(BF16)
    r1 = g - g0.astype(F32)
    g1 = r1.astype(BF16)
    g2 = (r1 - g1.astype(F32)).astype(BF16)
    return _dot(sel3_bf16, jnp.concatenate([g0, g1, g2], axis=0))


def _lane_bcast_cols(row, n):
    parts = []
    for c in range(n // V7X_LANES):
        tile = jnp.broadcast_to(row[:, c * V7X_LANES:(c + 1) * V7X_LANES], (V7X_LANES, V7X_LANES))
        parts.append(tile.T)
    return parts[0] if len(parts) == 1 else jnp.concatenate(parts, axis=0)


def _rms_gate_store(o, w, gate, out_ref, rows, cols):
    ms = jnp.mean(o * o, axis=-1, keepdims=True)
    y = o * lax.rsqrt(ms + EPS) * w
    out_ref[rows, cols] = (y * _silu(gate)).astype(out_ref.dtype)


def _ln_gate_store(o, w, gate, out_ref, rows, cols):
    mu = jnp.mean(o, axis=-1, keepdims=True)
    dlt = o - mu
    var = jnp.mean(dlt * dlt, axis=-1, keepdims=True)
    y = dlt * lax.rsqrt(var + EPS) * w
    out_ref[rows, cols] = (y * _silu(gate)).astype(out_ref.dtype)


def _token_selectors(n_tok):
    assert 3 * n_tok <= V7X_LANES
    j = lax.broadcasted_iota(jnp.int32, (V7X_LANES, V7X_LANES), 0)
    sel = []
    for t in range(n_tok):
        hit = jnp.logical_or(j == t, jnp.logical_or(j == n_tok + t, j == 2 * n_tok + t))
        sel.append(jnp.where(hit, 1.0, 0.0).astype(BF16))
    return jnp.stack(sel, axis=0)


def _column_source(x):
    n_tok, w = x.shape
    hi = x.astype(BF16).astype(F32)
    r1 = x - hi
    mid = r1.astype(BF16).astype(F32)
    lo = (r1 - mid).astype(BF16).astype(F32)
    x3 = jnp.concatenate([hi, mid, lo, jnp.zeros((V7X_LANES - 3 * n_tok, w), F32)], axis=0)
    parts = [x3[:, c * V7X_LANES:(c + 1) * V7X_LANES].T for c in range(w // V7X_LANES)]
    return (parts[0] if len(parts) == 1 else jnp.concatenate(parts, axis=0)).astype(BF16)


def _decode_advance(tok0, decay_rows_fn, decay_const_fn, k_ref, q_ref, v_ref, s_in_ref, s_out_ref, o_ref, sel_ref,
                    *, heads, dk, dv):
    n_tok = s_in_ref.shape[0]
    reps = dv // V7X_LANES
    rows = pl.ds(pl.multiple_of(tok0, n_tok), n_tok)

    def cols(src, tt):
        return jnp.concatenate([_dot(src, sel_ref[tt])] * reps, axis=1)

    for hh in range(heads):
        kc = slice(hh * dk, (hh + 1) * dk)
        vc = slice(hh * dv, (hh + 1) * dv)
        k_src, q_src = _column_source(k_ref[rows, kc]), _column_source(q_ref[rows, kc])
        a_src = None if decay_rows_fn is None else _column_source(decay_rows_fn(rows, hh))
        v = v_ref[rows, vc]
        o_rows = []
        for tt in range(n_tok):
            decay = decay_const_fn(hh) if a_src is None else cols(a_src, tt)
            s_new = decay * s_in_ref[tt, hh] + cols(k_src, tt) * v[tt:tt + 1, :]
            s_out_ref[tt, hh] = s_new
            o_rows.append(jnp.sum(cols(q_src, tt) * s_new, axis=0, keepdims=True))
        o_ref[rows, vc] = jnp.concatenate(o_rows, axis=0)


def _decode_plan(n_dec, n_steps):
    assert n_dec % n_steps == 0 and (n_dec // n_steps) % 8 == 0, (n_dec, n_steps)
    return n_dec // n_steps


def _gla_sum_matrices(c):
    levels = c.bit_length() - 1
    assert 1 << levels == c
    i = lax.broadcasted_iota(jnp.int32, (c, c), 0)
    j = lax.broadcasted_iota(jnp.int32, (c, c), 1)
    mats = [j <= i]
    for l in range(levels):
        ref = jnp.bitwise_or(jnp.bitwise_and(i, -(2 << l)), 1 << l)
        mats.append(jnp.logical_and(j > jnp.minimum(i, ref), j <= jnp.maximum(i, ref)))
    mats.append(j > i)
    sel = jnp.concatenate([jnp.where(m, 1.0, 0.0).astype(BF16) for m in mats], axis=0)
    return jnp.concatenate([sel, sel, sel], axis=1)


def _pair_level(c):
    levels = c.bit_length() - 1
    i = lax.broadcasted_iota(jnp.int32, (c, c), 0)
    j = lax.broadcasted_iota(jnp.int32, (c, c), 1)
    x = jnp.bitwise_xor(i, j)
    lvl = jnp.zeros((c, c), jnp.int32)
    for l in range(1, levels):
        lvl = lvl + jnp.where(x >= (1 << l), 1, 0)
    return jnp.where(i > j, lvl, jnp.where(i == j, levels, -1))


def _queries_else_keys(q, k, l):
    c = q.shape[0]
    span = 1 << l
    if span >= 8:
        parts = [(q if (b & 1) else k)[b * span:(b + 1) * span, :] for b in range(c // span)]
        return jnp.concatenate(parts, axis=0)
    row = lax.broadcasted_iota(jnp.int32, q.shape, 0)
    return jnp.where(jnp.bitwise_and(row, span) != 0, q, k)


def _gla_level_scores(q, k, sums):
    c = q.shape[0]
    levels = c.bit_length() - 1
    out = []
    for l in range(levels):
        x = _queries_else_keys(q, k, l) * jnp.exp2(sums[(1 + l) * c:(2 + l) * c, :])
        xb = x.astype(BF16)
        out.append(_dot_nt(xb, xb))
    return out


def _gla_chunk_out(q, k, v, sums, level_scores, pair_level, state):
    c = q.shape[0]
    levels = c.bit_length() - 1
    scores = jnp.where(pair_level == levels, jnp.sum(q * k, axis=-1, keepdims=True), 0.0)
    for l in range(levels):
        scores = jnp.where(pair_level == l, level_scores[l], scores)
    o = _dot((q * jnp.exp2(sums[0:c, :])).astype(BF16), state.astype(BF16))
    return o + _dot(scores.astype(BF16), v)


def _gla_next_state(k, v, sums, state):
    c, dk = k.shape
    levels = c.bit_length() - 1
    k_tail = (k * jnp.exp2(sums[(levels + 1) * c:(levels + 2) * c, :])).astype(BF16)
    decay = _lane_bcast_cols(jnp.exp2(sums[c - 1:c, :]), dk)
    decay_full = jnp.concatenate([decay] * (v.shape[1] // V7X_LANES), axis=1)
    return decay_full * state + _dot_tn(k_tail, v)


def _gla_kernel(q_ref, k_ref, v_ref, ga_ref, g_ref, qd_ref, kd_ref, vd_ref, gad_ref, gd_ref, nw_ref, sd_in_ref,
                o_ref, s_out_ref, od_ref, sd_out_ref,
                s_ref, mats_ref, lvl_ref, a_dec, q_dec, k_dec, v_dec, o_dec, sel_ref, *, heads, dk, dv):
    t = pl.program_id(1)
    step = pl.program_id(0) * pl.num_programs(1) + t
    last_step = pl.num_programs(0) * pl.num_programs(1) - 1

    @pl.when(t == 0)
    def _():
        s_ref[...] = jnp.zeros_like(s_ref)
        mats_ref[...] = _gla_sum_matrices(GLA_CHUNK)
        lvl_ref[...] = _pair_level(GLA_CHUNK)

    @pl.when(step == 0)
    def _():
        a_dec[...] = jnp.exp2(gd_ref[...])
        q_dec[...] = qd_ref[...].astype(F32) * (dk ** -0.5)
        k_dec[...] = kd_ref[...].astype(F32)
        v_dec[...] = vd_ref[...].astype(F32)
        sel_ref[...] = _token_selectors(sd_in_ref.shape[0])

    _decode_advance(step * sd_in_ref.shape[0], lambda rows, hh: a_dec[rows, hh * dk:(hh + 1) * dk], None,
                    k_dec, q_dec, v_dec, sd_in_ref, sd_out_ref, o_dec, sel_ref, heads=heads, dk=dk, dv=dv)

    @pl.when(step == last_step)
    def _():
        n_dec = o_dec.shape[0]
        for hh in range(heads):
            vc = slice(hh * dv, (hh + 1) * dv)
            _rms_gate_store(o_dec[:, vc], nw_ref[...], gad_ref[:, vc].astype(F32), od_ref, slice(0, n_dec), vc)

    ct = q_ref.shape[0]

    kcs = [slice(hh * dk, (hh + 1) * dk) for hh in range(heads)]
    vcs = [slice(hh * dv, (hh + 1) * dv) for hh in range(heads)]
    group = 2 if (ct // GLA_CHUNK) % 2 == 0 else 1

    def chunk_group(cg, carry):
        pair_level = lvl_ref[...]
        rows, sums, qs, ks, lvl_scores = [], [], [], [], []
        for u in range(group):
            r = pl.ds(pl.multiple_of((cg * group + u) * GLA_CHUNK, GLA_CHUNK), GLA_CHUNK)
            rows.append(r)
            sums.append(_prefix_sum_rows(mats_ref[...], g_ref[r, :]))
            qs.append([q_ref[r, kc].astype(F32) * (dk ** -0.5) for kc in kcs])
            ks.append([k_ref[r, kc].astype(F32) for kc in kcs])
            lvl_scores.append([_gla_level_scores(qs[u][hh], ks[u][hh], sums[u][:, kcs[hh]]) for hh in range(heads)])
        outs = []
        for u in range(group):
            r = rows[u]
            outs.append([_gla_chunk_out(qs[u][hh], ks[u][hh], v_ref[r, vcs[hh]], sums[u][:, kcs[hh]],
                                        lvl_scores[u][hh], pair_level, s_ref[hh]) for hh in range(heads)])
            for hh in range(heads):
                s_ref[hh] = _gla_next_state(ks[u][hh], v_ref[r, vcs[hh]], sums[u][:, kcs[hh]], s_ref[hh])
        for u in range(group):
            for hh in range(heads):
                _rms_gate_store(outs[u][hh], nw_ref[...], ga_ref[rows[u], vcs[hh]].astype(F32), o_ref, rows[u],
                                vcs[hh])
        return carry

    lax.fori_loop(0, ct // GLA_CHUNK // group, chunk_group, 0)

    @pl.when(t == pl.num_programs(1) - 1)
    def _():
        s_out_ref[0] = s_ref[...]


def _gla(proj, log2_decay, proj_dec, log2_decay_dec, norm_w, state_dec, lay, batch, seq):
    heads, dk, dv = lay["gla_heads"], lay["gla_dk"], lay["gla_dv"]
    qk, vw = heads * dk, heads * dv
    ct = min(seq, GLA_STEP_CHUNKS * GLA_CHUNK)
    levels = GLA_CHUNK.bit_length() - 1
    assert seq % ct == 0 and ct % GLA_CHUNK == 0
    nt = seq // ct
    n_dec = proj_dec.shape[0]
    tps = _decode_plan(n_dec, batch * nt)
    row = lambda b, t: b * nt + t
    dec_in = lambda width, col: pl.BlockSpec((n_dec, width), lambda b, t: (0, col // width))
    dec_state = pl.BlockSpec((tps, heads, dk, dv), lambda b, t: (row(b, t), 0, 0, 0))
    kern = functools.partial(_gla_kernel, heads=heads, dk=dk, dv=dv)
    return pl.pallas_call(
        kern,
        grid=(batch, nt),
        in_specs=[
            pl.BlockSpec((ct, qk), lambda b, t: (row(b, t), lay["qa"] // qk)),
            pl.BlockSpec((ct, qk), lambda b, t: (row(b, t), lay["ka"] // qk)),
            pl.BlockSpec((ct, vw), lambda b, t: (row(b, t), lay["va"] // vw)),
            pl.BlockSpec((ct, vw), lambda b, t: (row(b, t), lay["ga"] // vw)),
            pl.BlockSpec((ct, qk), lambda b, t: (row(b, t), 0)),
            dec_in(qk, lay["qa"]), dec_in(qk, lay["ka"]), dec_in(vw, lay["va"]), dec_in(vw, lay["ga"]),
            dec_in(qk, 0),
            pl.BlockSpec((1, dv), lambda b, t: (0, 0)),
            dec_state,
        ],
        out_specs=[
            pl.BlockSpec((ct, vw), lambda b, t: (row(b, t), 0)),
            pl.BlockSpec((1, heads, dk, dv), lambda b, t: (b, 0, 0, 0)),
            pl.BlockSpec((n_dec, vw), lambda b, t: (0, 0)),
            dec_state,
        ],
        out_shape=[
            jax.ShapeDtypeStruct((batch * seq, vw), BF16),
            jax.ShapeDtypeStruct((batch, heads, dk, dv), F32),
            jax.ShapeDtypeStruct((n_dec, vw), BF16),
            jax.ShapeDtypeStruct(state_dec.shape, state_dec.dtype),
        ],
        scratch_shapes=[
            pltpu.VMEM((heads, dk, dv), F32),
            pltpu.VMEM(((levels + 2) * GLA_CHUNK, 3 * GLA_CHUNK), BF16),
            pltpu.VMEM((GLA_CHUNK, GLA_CHUNK), jnp.int32),
            pltpu.VMEM((n_dec, qk), F32), pltpu.VMEM((n_dec, qk), F32), pltpu.VMEM((n_dec, qk), F32),
            pltpu.VMEM((n_dec, vw), F32), pltpu.VMEM((n_dec, vw), F32),
            pltpu.VMEM((tps, V7X_LANES, V7X_LANES), BF16),
        ],
        compiler_params=_params(2, claim_all=True),
        name="gla",
    )(proj, proj, proj, proj, log2_decay, proj_dec, proj_dec, proj_dec, proj_dec, log2_decay_dec, norm_w, state_dec)


def _rotary(x, cos, sin):
    half = x.shape[1] // 2
    x1, x2 = x[:, :half], x[:, half:]
    return jnp.concatenate([x1 * cos - x2 * sin, x1 * sin + x2 * cos], axis=1)


def _ret_kernel(q_ref, k_ref, v_ref, gb_ref, cos_ref, sin_ref, qd_ref, kd_ref, vd_ref, gbd_ref, cosd_ref, sind_ref,
                lg_ref, nw_ref, sd_in_ref,
                o_ref, s_out_ref, od_ref, sd_out_ref,
                s_ref, dmat_ref, qdec_ref, kdec_ref, q_dec, k_dec, v_dec, o_dec, sel_ref, *, heads, dk, dv, c):
    t = pl.program_id(1)
    step = pl.program_id(0) * pl.num_programs(1) + t
    last_step = pl.num_programs(0) * pl.num_programs(1) - 1

    @pl.when(step == 0)
    def _():
        cosd, sind = cosd_ref[0:1, :], sind_ref[0:1, :]
        for hh in range(heads):
            kc = slice(hh * dk, (hh + 1) * dk)
            q_dec[:, kc] = _rotary(qd_ref[:, kc].astype(F32), cosd, sind)
            k_dec[:, kc] = _rotary(kd_ref[:, kc].astype(F32), cosd, sind) * (dk ** -0.5)
        v_dec[...] = vd_ref[...].astype(F32)
        sel_ref[...] = _token_selectors(sd_in_ref.shape[0])

    def gamma(hh):
        return jnp.exp(jnp.concatenate([lg_ref[hh]] * (dv // V7X_LANES), axis=1))

    _decode_advance(step * sd_in_ref.shape[0], None, gamma, k_dec, q_dec, v_dec, sd_in_ref, sd_out_ref, o_dec,
                    sel_ref, heads=heads, dk=dk, dv=dv)

    @pl.when(step == last_step)
    def _():
        n_dec = o_dec.shape[0]
        for hh in range(heads):
            vc = slice(hh * dv, (hh + 1) * dv)
            _ln_gate_store(o_dec[:, vc], nw_ref[...], gbd_ref[:, vc].astype(F32), od_ref, slice(0, n_dec), vc)

    @pl.when(t == 0)
    def _():
        s_ref[...] = jnp.zeros_like(s_ref)
        ri = lax.broadcasted_iota(jnp.int32, (c, c), 0)
        rj = lax.broadcasted_iota(jnp.int32, (c, c), 1)
        dist = (ri - rj).astype(F32)
        rowl = lax.broadcasted_iota(jnp.int32, (c, V7X_LANES), 0).astype(F32)
        for hh in range(heads):
            lg = lg_ref[hh]
            dmat_ref[hh] = jnp.exp(jnp.where(ri >= rj, dist * lg[:, :1], -jnp.inf))
            qdec_ref[hh] = jnp.exp((rowl + 1.0) * lg)
            kdec_ref[hh] = jnp.exp((float(c - 1) - rowl) * lg)

    ct = q_ref.shape[0]
    kcs = [slice(hh * dk, (hh + 1) * dk) for hh in range(heads)]
    vcs = [slice(hh * dv, (hh + 1) * dv) for hh in range(heads)]

    def chunk(ci, carry):
        rows = pl.ds(pl.multiple_of(ci * c, c), c)
        cos, sin = cos_ref[rows, :], sin_ref[rows, :]
        qrs = [_rotary(q_ref[rows, kc].astype(F32), cos, sin).astype(BF16) for kc in kcs]
        krs = [_rotary(k_ref[rows, kc].astype(F32), cos, sin) * (dk ** -0.5) for kc in kcs]
        scores = [_dot_nt(qrs[hh], krs[hh].astype(BF16)) * dmat_ref[hh] for hh in range(heads)]
        outs = []
        for hh in range(heads):
            qdec = jnp.concatenate([qdec_ref[hh]] * (dv // V7X_LANES), axis=1)
            o = qdec * _dot(qrs[hh], s_ref[hh].astype(BF16))
            outs.append(o + _dot(scores[hh].astype(BF16), v_ref[rows, vcs[hh]]))
        for hh in range(heads):
            kdec = jnp.concatenate([kdec_ref[hh]] * (dk // V7X_LANES), axis=1)
            k_tail = (krs[hh] * kdec).astype(BF16)
            lgv = jnp.concatenate([lg_ref[hh]] * (dv // V7X_LANES), axis=1)
            s_ref[hh] = jnp.exp(float(c) * lgv) * s_ref[hh] + _dot_tn(k_tail, v_ref[rows, vcs[hh]])
        for hh in range(heads):
            _ln_gate_store(outs[hh], nw_ref[...], gb_ref[rows, vcs[hh]].astype(F32), o_ref, rows, vcs[hh])
        return carry

    lax.fori_loop(0, ct // c, chunk, 0)

    @pl.when(t == pl.num_programs(1) - 1)
    def _():
        s_out_ref[0] = s_ref[...]


def _ret(proj, cos, sin, proj_dec, cos_dec, sin_dec, log_gamma, norm_w, state_dec, lay, batch, seq):
    heads, dk, dv = lay["ret_heads"], lay["ret_dk"], lay["ret_dv"]
    qk, vw = heads * dk, heads * dv
    c = min(seq, RET_CHUNK)
    ct = min(seq, RET_STEP_CHUNKS * c)
    assert seq % ct == 0 and ct % c == 0
    nt = seq // ct
    half = dk // 2
    n_dec = proj_dec.shape[0]
    tps = _decode_plan(n_dec, batch * nt)
    row = lambda b, t: b * nt + t
    dec_in = lambda width, col: pl.BlockSpec((n_dec, width), lambda b, t: (0, col // width))
    dec_state = pl.BlockSpec((tps, heads, dk, dv), lambda b, t: (row(b, t), 0, 0, 0))
    table_dec = pl.BlockSpec((cos_dec.shape[0], half), lambda b, t: (0, 0))
    kern = functools.partial(_ret_kernel, heads=heads, dk=dk, dv=dv, c=c)
    return pl.pallas_call(
        kern,
        grid=(batch, nt),
        in_specs=[
            pl.BlockSpec((ct, qk), lambda b, t: (row(b, t), lay["qb"] // qk)),
            pl.BlockSpec((ct, qk), lambda b, t: (row(b, t), lay["kb"] // qk)),
            pl.BlockSpec((ct, vw), lambda b, t: (row(b, t), lay["vb"] // vw)),
            pl.BlockSpec((ct, vw), lambda b, t: (row(b, t), lay["gb"] // vw)),
            pl.BlockSpec((ct, half), lambda b, t: (t, 0)),
            pl.BlockSpec((ct, half), lambda b, t: (t, 0)),
            dec_in(qk, lay["qb"]), dec_in(qk, lay["kb"]), dec_in(vw, lay["vb"]), dec_in(vw, lay["gb"]),
            table_dec, table_dec,
            pl.BlockSpec((heads, 1, V7X_LANES), lambda b, t: (0, 0, 0)),
            pl.BlockSpec((1, dv), lambda b, t: (0, 0)),
            dec_state,
        ],
        out_specs=[
            pl.BlockSpec((ct, vw), lambda b, t: (row(b, t), 0)),
            pl.BlockSpec((1, heads, dk, dv), lambda b, t: (b, 0, 0, 0)),
            pl.BlockSpec((n_dec, vw), lambda b, t: (0, 0)),
            dec_state,
        ],
        out_shape=[
            jax.ShapeDtypeStruct((batch * seq, vw), BF16),
            jax.ShapeDtypeStruct((batch, heads, dk, dv), F32),
            jax.ShapeDtypeStruct((n_dec, vw), BF16),
            jax.ShapeDtypeStruct(state_dec.shape, state_dec.dtype),
        ],
        scratch_shapes=[
            pltpu.VMEM((heads, dk, dv), F32),
            pltpu.VMEM((heads, c, c), F32),
            pltpu.VMEM((heads, c, V7X_LANES), F32),
            pltpu.VMEM((heads, c, V7X_LANES), F32),
            pltpu.VMEM((n_dec, qk), F32), pltpu.VMEM((n_dec, qk), F32),
            pltpu.VMEM((n_dec, vw), F32), pltpu.VMEM((n_dec, vw), F32),
            pltpu.VMEM((tps, V7X_LANES, V7X_LANES), BF16),
        ],
        compiler_params=_params(2, claim_all=True),
        name="ret",
    )(proj, proj, proj, proj, cos, sin, proj_dec, proj_dec, proj_dec, proj_dec, cos_dec, sin_dec,
      log_gamma, norm_w, state_dec)


def _merge_kernel(oap_ref, obp_ref, oas_ref, obs_ref, wa_ref, wb_ref, g0p_ref, g1p_ref, g0s_ref, g1s_ref, wnext_ref,
                  mp_ref, ms_ref, wnext_bf_ref, wa_bf, wb_bf):
    wnext_bf_ref[...] = wnext_ref[...].astype(BF16)

    def merged(oa, ob, g0, g1):
        ya = _dot(oa, wa_bf[...])
        yb = _dot(ob, wb_bf[...])
        return _sigmoid(g0.astype(F32)) * ya + _sigmoid(g1.astype(F32)) * yb

    @pl.when(pl.program_id(1) == 0)
    def _():
        wa_bf[...] = wa_ref[...].astype(BF16)
        wb_bf[...] = wb_ref[...].astype(BF16)
        ms_ref[...] = merged(oas_ref[...], obs_ref[...], g0s_ref[...], g1s_ref[...]).astype(ms_ref.dtype)

    mp_ref[...] = merged(oap_ref[...], obp_ref[...], g0p_ref[...], g1p_ref[...]).astype(mp_ref.dtype)


def _slab_specs(w_next, n_steps, step_of):
    kn, dn = w_next.shape
    assert kn % n_steps == 0 and (kn // n_steps) % 16 == 0, (kn, n_steps)
    slab = kn // n_steps
    spec = pl.BlockSpec((slab, dn), lambda j, i: (step_of(j, i), 0))
    return spec, spec, jax.ShapeDtypeStruct((kn, dn), BF16), _nbytes((slab, dn), F32) + _nbytes((slab, dn), BF16)


def _merge(oa_p, ob_p, oa_s, ob_s, wa, wb, proj_p, proj_s, lay, w_next):
    m_p, ka = oa_p.shape
    kb = ob_p.shape[1]
    tail = oa_s.shape[0]
    d = wa.shape[1]
    tm = _row_tile(m_p, ROW_TILE)
    tn = min(d, 1024)
    assert d % tn == 0 and lay["mg"] % tn == 0
    g0 = lay["mg"] // tn
    g1 = (lay["mg"] + d) // tn
    n_m = m_p // tm
    slab_in, slab_out, slab_shape, slab_bytes = _slab_specs(w_next, (d // tn) * n_m, lambda j, i: j * n_m + i)
    return pl.pallas_call(
        _merge_kernel,
        grid=(d // tn, n_m),
        in_specs=[
            pl.BlockSpec((tm, ka), lambda j, i: (i, 0)),
            pl.BlockSpec((tm, kb), lambda j, i: (i, 0)),
            pl.BlockSpec((tail, ka), lambda j, i: (0, 0)),
            pl.BlockSpec((tail, kb), lambda j, i: (0, 0)),
            pl.BlockSpec((ka, tn), lambda j, i: (0, j)),
            pl.BlockSpec((kb, tn), lambda j, i: (0, j)),
            pl.BlockSpec((tm, tn), lambda j, i: (i, g0 + j)),
            pl.BlockSpec((tm, tn), lambda j, i: (i, g1 + j)),
            pl.BlockSpec((tail, tn), lambda j, i: (0, g0 + j)),
            pl.BlockSpec((tail, tn), lambda j, i: (0, g1 + j)),
            slab_in,
        ],
        out_specs=[pl.BlockSpec((tm, tn), lambda j, i: (i, j)), pl.BlockSpec((tail, tn), lambda j, i: (0, j)),
                   slab_out],
        out_shape=[jax.ShapeDtypeStruct((m_p, d), BF16), jax.ShapeDtypeStruct((tail, d), BF16), slab_shape],
        scratch_shapes=[pltpu.VMEM((ka, tn), BF16), pltpu.VMEM((kb, tn), BF16)],
        compiler_params=_params(
            2, _nbytes((tm, ka), BF16), _nbytes((tm, kb), BF16), _nbytes((tail, ka), BF16), _nbytes((tail, kb), BF16),
            _nbytes((ka, tn), F32), _nbytes((kb, tn), F32), 3 * _nbytes((tm, tn), BF16), 3 * _nbytes((tail, tn), BF16),
            slab_bytes,
            scratch_bytes=_nbytes((ka, tn), BF16) + _nbytes((kb, tn), BF16) + 3 * _nbytes((tm, tn), F32)),
        name="merge",
    )(oa_p, ob_p, oa_s, ob_s, wa, wb, proj_p, proj_p, proj_s, proj_s, w_next)


def _proj_res_norm_kernel(ap_ref, as_ref, w_ref, resp_ref, ress_ref, nw_ref, *out_refs, emit_sum):
    n_out = 2 if emit_sum else 1
    outs_p, outs_s = out_refs[:n_out], out_refs[n_out:]
    i = pl.program_id(0)
    k = pl.program_id(1)
    last_k = k == pl.num_programs(1) - 1
    d = w_ref.shape[1]
    col_chunk = min(d, 512)

    def step(a_ref, res_ref, outs):
        acc_ref = outs[0]
        nrow = acc_ref.shape[0]
        row_chunk = min(nrow, 128)
        assert nrow % row_chunk == 0

        @pl.when(k == 0)
        def _():
            acc_ref[...] = res_ref[...]

        a = a_ref[...]
        for c in range(d // col_chunk):
            cs = slice(c * col_chunk, (c + 1) * col_chunk)
            acc_ref[:, cs] += _dot(a, w_ref[:, cs])

        @pl.when(last_k)
        def _():
            def body(c, carry):
                rr = pl.ds(pl.multiple_of(c * row_chunk, row_chunk), row_chunk)
                y = _rmsnorm_rows(acc_ref[rr, :], nw_ref[...])
                if emit_sum:
                    outs[1][rr, :] = y.astype(outs[1].dtype)
                else:
                    acc_ref[rr, :] = y
                return carry

            lax.fori_loop(0, nrow // row_chunk, body, 0)

    step(ap_ref, resp_ref, outs_p)

    @pl.when(i == pl.num_programs(0) - 1)
    def _():
        step(as_ref, ress_ref, outs_s)


def _proj_res_norm(a_p, a_s, w, res_p, res_s, norm_w, emit_sum, tk):
    m_p, kdim = a_p.shape
    tail = a_s.shape[0]
    d = w.shape[1]
    tm = _row_tile(m_p, ROW_TILE)
    tk = min(kdim, tk)
    assert kdim % tk == 0
    p_spec = pl.BlockSpec((tm, d), lambda i, k: (i, 0))
    s_spec = pl.BlockSpec((tail, d), lambda i, k: (0, 0))
    s3_spec = pl.BlockSpec((tail, None, d), lambda i, k: (0, 0, 0))
    res_s_spec = s3_spec if res_s.ndim == 3 else s_spec
    out_specs = [p_spec, s3_spec]
    out_shape = [jax.ShapeDtypeStruct((m_p, d), F32), jax.ShapeDtypeStruct((tail, 1, d), F32)]
    assert w.dtype == BF16
    blocks = [_nbytes((tm, tk), BF16), _nbytes((tail, tk), BF16), _nbytes((tk, d), BF16),
              2 * _nbytes((tm, d), F32), 2 * _nbytes((tail, d), F32)]
    if emit_sum:
        out_specs = [p_spec, p_spec, s_spec, s_spec]
        out_shape = [out_shape[0], jax.ShapeDtypeStruct((m_p, d), BF16),
                     jax.ShapeDtypeStruct((tail, d), F32), jax.ShapeDtypeStruct((tail, d), BF16)]
        blocks += [_nbytes((tm, d), BF16), _nbytes((tail, d), BF16)]
    return pl.pallas_call(
        functools.partial(_proj_res_norm_kernel, emit_sum=emit_sum),
        grid=(m_p // tm, kdim // tk),
        in_specs=[
            pl.BlockSpec((tm, tk), lambda i, k: (i, k)),
            pl.BlockSpec((tail, tk), lambda i, k: (0, k)),
            pl.BlockSpec((tk, d), lambda i, k: (k, 0)),
            p_spec,
            res_s_spec,
            pl.BlockSpec((1, d), lambda i, k: (0, 0)),
        ],
        out_specs=out_specs,
        out_shape=out_shape,
        compiler_params=_params(2, *blocks),
        name="proj_res_norm",
    )(a_p, a_s, w, res_p, res_s, norm_w.reshape(1, d))


def _swiglu_kernel(hp_ref, hs_ref, wg_ref, wu_ref, wnext_ref, op_ref, os_ref, wnext_bf_ref, wg_bf, wu_bf):
    wnext_bf_ref[...] = wnext_ref[...].astype(BF16)

    tn = wg_bf.shape[1]
    col_chunk = min(tn, 256)

    def act(h_ref, o_ref):
        h = h_ref[...]
        for c in range(tn // col_chunk):
            cs = slice(c * col_chunk, (c + 1) * col_chunk)
            a = _dot(h, wg_bf[:, cs])
            b = _dot(h, wu_bf[:, cs])
            o_ref[:, cs] = (_silu(a) * b).astype(o_ref.dtype)

    @pl.when(pl.program_id(1) == 0)
    def _():
        wg_bf[...] = wg_ref[...].astype(BF16)
        wu_bf[...] = wu_ref[...].astype(BF16)
        act(hs_ref, os_ref)

    act(hp_ref, op_ref)


def _swiglu(h_p, h_s, wg, wu, w_next):
    m_p, d = h_p.shape
    tail = h_s.shape[0]
    f = wg.shape[1]
    tm = _row_tile(m_p, ROW_TILE_WIDE)
    tn = 512 if f % 512 == 0 else 256
    assert f % tn == 0
    n_m = m_p // tm
    slab_in, slab_out, slab_shape, slab_bytes = _slab_specs(w_next, (f // tn) * n_m, lambda j, i: j * n_m + i)
    return pl.pallas_call(
        _swiglu_kernel,
        grid=(f // tn, n_m),
        in_specs=[
            pl.BlockSpec((tm, d), lambda j, i: (i, 0)),
            pl.BlockSpec((tail, d), lambda j, i: (0, 0)),
            pl.BlockSpec((d, tn), lambda j, i: (0, j)),
            pl.BlockSpec((d, tn), lambda j, i: (0, j)),
            slab_in,
        ],
        out_specs=[pl.BlockSpec((tm, tn), lambda j, i: (i, j)), pl.BlockSpec((tail, tn), lambda j, i: (0, j)),
                   slab_out],
        out_shape=[jax.ShapeDtypeStruct((m_p, f), BF16), jax.ShapeDtypeStruct((tail, f), BF16), slab_shape],
        scratch_shapes=[pltpu.VMEM((d, tn), BF16), pltpu.VMEM((d, tn), BF16)],
        compiler_params=_params(
            2, _nbytes((tm, d), BF16), _nbytes((tail, d), BF16), 2 * _nbytes((d, tn), F32),
            _nbytes((tm, tn), BF16), _nbytes((tail, tn), BF16), slab_bytes,
            scratch_bytes=2 * _nbytes((d, tn), BF16) + 3 * _nbytes((tm, tn), F32)),
        name="swiglu",
    )(h_p, h_s, wg, wu, w_next)


def _layout(d_model, in_width, state_gla, state_ret, gate_rank):
    _, _, gh, gdk, gdv = state_gla.shape
    _, _, rh, rdk, rdv = state_ret.shape
    gqk, gv, rqk, rv = gh * gdk, gh * gdv, rh * rdk, rh * rdv
    lay = dict(gla_heads=gh, gla_dk=gdk, gla_dv=gdv, ret_heads=rh, ret_dk=rdk, ret_dv=rdv, rank=gate_rank)
    off = 0
    for name, width in (("qa", gqk), ("ka", gqk), ("va", gv), ("ga", gv), ("qb", rqk), ("kb", rqk),
                        ("vb", rv), ("gb", rv), ("mg", 2 * d_model)):
        lay[name] = off
        off += width
    lay["out_cols"] = off
    lay["plain_cols"] = 2 * gqk + gv
    lay["gd_src"] = lay["plain_cols"]
    assert lay["gd_src"] % V7X_LANES == 0 and gate_rank <= V7X_LANES
    assert in_width == off + gate_rank
    return lay


def _layer(x_p, x_s, st_gla, st_ret, wts, lay, log_gamma, final_norm):
    (norm_mix, w_in, w_gate_up, b_gate, gla_norm_w, w_gla_up, ret_norm_w, w_ret_up, w_out, norm_ffn,
     w_ffn_gate, w_ffn_up, w_ffn_down) = wts
    batch, seq, d = x_p.shape
    rank = lay["rank"]
    gqk = lay["gla_heads"] * lay["gla_dk"]
    bup = b_gate.reshape(1, gqk)
    gnw = gla_norm_w.reshape(1, -1)
    rnw = ret_norm_w.reshape(1, -1)
    tn = 1024 if (lay["out_cols"] % 1024 == 0 and lay["plain_cols"] % 1024 == 0) else 512
    xp = x_p.reshape(batch * seq, d)
    assert x_s.ndim == 3 and x_s.shape[1] == 1, "one new token per decode sequence"
    xs = x_s
    w_in_t = w_in.T

    h_p, h_s, g_p, g_s, cos_p, sin_p, cos_s, sin_s = _rmsnorm_gate(
        xp, xs, norm_mix, w_in_t, lay["gd_src"], w_gate_up, bup, seq, lay["ret_dk"] // 2)
    proj_p, proj_s = _in_proj(h_p, h_s, w_in_t, lay["plain_cols"], rank, lay["out_cols"], tn)
    oa_p, sa_p, oa_s, sa_s = _gla(proj_p, g_p, proj_s, g_s, gnw, st_gla, lay, batch, seq)
    ob_p, sb_p, ob_s, sb_s = _ret(proj_p, cos_p, sin_p, proj_s, cos_s, sin_s, log_gamma, rnw, st_ret, lay, batch, seq)
    m_p, m_s, w_out_bf = _merge(oa_p, ob_p, oa_s, ob_s, w_gla_up, w_ret_up, proj_p, proj_s, lay, w_out)
    x1_p, h2_p, x1_s, h2_s = _proj_res_norm(m_p, m_s, w_out_bf, xp, xs, norm_ffn, True, 1024)
    act_p, act_s, w_down_bf = _swiglu(h2_p, h2_s, w_ffn_gate, w_ffn_up, w_ffn_down)
    y_p, y_s = _proj_res_norm(act_p, act_s, w_down_bf, x1_p, x1_s, final_norm, False, 1408)
    return (y_p, sa_p, sb_p), (y_s, sa_s, sb_s)


def kernel(x_prompt, x_sample, state_gla, state_ret, norm_mix, w_in, w_gla_gate_up, b_gla_gate, gla_norm_w,
           w_gla_up, ret_norm_w, w_ret_up, w_out, norm_ffn, w_ffn_gate, w_ffn_up, w_ffn_down, norm_final):
    depth = w_in.shape[0]
    assert depth == 1, "single-layer trunk"
    batch, seq, d = x_prompt.shape
    lay = _layout(d, w_in.shape[-1], state_gla, state_ret, w_gla_gate_up.shape[1])
    rh, rdk = lay["ret_heads"], lay["ret_dk"]
    assert rdk // 2 == V7X_LANES
    lg = jnp.log1p(-jnp.exp(jnp.linspace(math.log(1.0 / 32), math.log(1.0 / 512), rh))).astype(F32)
    log_gamma = jnp.broadcast_to(lg[:, None, None], (rh, 1, V7X_LANES))

    wts = (norm_mix[0], w_in[0], w_gla_gate_up[0], b_gla_gate[0], gla_norm_w[0], w_gla_up[0], ret_norm_w[0],
           w_ret_up[0], w_out[0], norm_ffn[0], w_ffn_gate[0], w_ffn_up[0], w_ffn_down[0])
    (y_p, ga_p, re_p), (y_s, ga_s, re_s) = _layer(
        x_prompt, x_sample, state_gla[0], state_ret[0], wts, lay, log_gamma, norm_final)

    sd = state_gla.dtype
    return (y_p.reshape(batch, seq, d), y_s.reshape(x_sample.shape),
            ga_p[None].astype(sd), re_p[None].astype(state_ret.dtype),
            ga_s[None].astype(sd), re_s[None].astype(state_ret.dtype))
```

```python
import functools
import math

import numpy as np
import jax
import jax.numpy as jnp
from jax import lax
from jax.experimental import pallas as pl
from jax.experimental.pallas import tpu as pltpu

EPS = 1e-6
ROPE_BASE = 10000.0
GLA_GATE_NORM = 16.0
PAST_LEN = 16384

V7X_LANES = 128
V7X_VMEM_REQUEST_CAP = 60000 * 1024
COMPILER_SCRATCH_BYTES = 12 * 1024 * 1024

GLA_CHUNK = 64
GLA_STEP_CHUNKS = 8
LOG2_E = 1.4426950408889634
RET_CHUNK = 128
RET_STEP_CHUNKS = 4
ROW_TILE = 1024
ROW_TILE_WIDE = 2048
K_BLOCK = 512

BF16 = jnp.bfloat16
F32 = jnp.float32


def _params(n_axes, *block_bytes, scratch_bytes=0, claim_all=False):
    need = 2 * sum(block_bytes) + scratch_bytes + COMPILER_SCRATCH_BYTES
    if claim_all:
        need = V7X_VMEM_REQUEST_CAP
    return pltpu.CompilerParams(
        dimension_semantics=("arbitrary",) * n_axes,
        vmem_limit_bytes=int(min(V7X_VMEM_REQUEST_CAP, need)),
    )


def _nbytes(shape, dtype):
    return int(np.prod(shape)) * jnp.dtype(dtype).itemsize


def _sigmoid(x):
    return 1.0 / (1.0 + jnp.exp(-x))


def _silu(x):
    return x * _sigmoid(x)


def _log_sigmoid(x):
    return jnp.minimum(x, 0.0) - jnp.log(1.0 + jnp.exp(-jnp.abs(x)))


def _dot(a, b):
    return jnp.dot(a, b, preferred_element_type=F32)


def _dot_nt(a, b):
    return lax.dot_general(a, b, (((1,), (1,)), ((), ())), preferred_element_type=F32)


def _dot_tn(a, b):
    return lax.dot_general(a, b, (((0,), (0,)), ((), ())), preferred_element_type=F32)


def _row_tile(m, want):
    t = min(m, want)
    assert m % t == 0, (m, t)
    return t


def _rmsnorm_rows(x, w):
    ms = jnp.mean(x * x, axis=-1, keepdims=True)
    return x * lax.rsqrt(ms + EPS) * w


def _rope_rows(cos_ref, sin_ref, pos0):
    rows, half = cos_ref.shape
    pos = (lax.broadcasted_iota(jnp.int32, (rows, half), 0) + pos0).astype(F32)
    idx = lax.broadcasted_iota(jnp.int32, (rows, half), 1).astype(F32)
    ang = pos * jnp.exp(idx * (-math.log(ROPE_BASE) / half))
    cos_ref[...] = jnp.cos(ang)
    sin_ref[...] = jnp.sin(ang)


def _rmsnorm_kernel(xp_ref, xs_ref, w_ref, wgd_ref, wup_ref, bup_ref,
                    hp_ref, hs_ref, gp_ref, gs_ref, cosp_ref, sinp_ref, cosd_ref, sind_ref, wgd_bf, wup_bf):
    i = pl.program_id(0)

    @pl.when(i == 0)
    def _():
        wgd_bf[...] = wgd_ref[...].astype(BF16)
        wup_bf[...] = jnp.zeros_like(wup_bf)
        wup_bf[0:wup_ref.shape[0], :] = wup_ref[...].astype(BF16)

    def rows(x, h_ref, g_ref):
        h = _rmsnorm_rows(x, w_ref[...]).astype(h_ref.dtype)
        h_ref[...] = h
        gd = _dot_nt(h, wgd_bf[...])
        x = _dot(gd.astype(BF16), wup_bf[...]) + bup_ref[...]
        g_ref[...] = _log_sigmoid(x) * (LOG2_E / GLA_GATE_NORM)

    rows(xp_ref[...], hp_ref, gp_ref)
    _rope_rows(cosp_ref, sinp_ref, i * cosp_ref.shape[0])

    @pl.when(i == 0)
    def _():
        rows(xs_ref[...], hs_ref, gs_ref)
        _rope_rows(cosd_ref, sind_ref, PAST_LEN)


def _rmsnorm_gate(x_p, x_s, w, w_in_t, gate_row0, w_gate_up, bup, seq, half):
    m_p, d = x_p.shape
    tail = x_s.shape[0]
    rank, gw = w_gate_up.shape
    tm = _row_tile(m_p, ROW_TILE)
    n_steps = m_p // tm
    assert gate_row0 % V7X_LANES == 0 and seq % n_steps == 0 and rank % 16 == 0
    pos_rows = seq // n_steps
    table = pl.BlockSpec((pos_rows, half), lambda i: (i, 0))
    table_dec = pl.BlockSpec((8, half), lambda i: (0, 0))
    return pl.pallas_call(
        _rmsnorm_kernel,
        grid=(n_steps,),
        in_specs=[
            pl.BlockSpec((tm, d), lambda i: (i, 0)),
            pl.BlockSpec((tail, None, d), lambda i: (0, 0, 0)),
            pl.BlockSpec((1, d), lambda i: (0, 0)),
            pl.BlockSpec((V7X_LANES, d), lambda i: (gate_row0 // V7X_LANES, 0)),
            pl.BlockSpec((rank, gw), lambda i: (0, 0)),
            pl.BlockSpec((1, gw), lambda i: (0, 0)),
        ],
        out_specs=[
            pl.BlockSpec((tm, d), lambda i: (i, 0)), pl.BlockSpec((tail, d), lambda i: (0, 0)),
            pl.BlockSpec((tm, gw), lambda i: (i, 0)), pl.BlockSpec((tail, gw), lambda i: (0, 0)),
            table, table, table_dec, table_dec,
        ],
        out_shape=[
            jax.ShapeDtypeStruct((m_p, d), BF16), jax.ShapeDtypeStruct((tail, d), BF16),
            jax.ShapeDtypeStruct((m_p, gw), F32), jax.ShapeDtypeStruct((tail, gw), F32),
            jax.ShapeDtypeStruct((seq, half), F32), jax.ShapeDtypeStruct((seq, half), F32),
            jax.ShapeDtypeStruct((8, half), F32), jax.ShapeDtypeStruct((8, half), F32),
        ],
        scratch_shapes=[pltpu.VMEM((V7X_LANES, d), BF16), pltpu.VMEM((V7X_LANES, gw), BF16)],
        compiler_params=_params(1, claim_all=True),
        name="rmsnorm_gate",
    )(x_p, x_s, w.reshape(1, d), w_in_t, w_gate_up, bup)


def _in_proj_kernel(hp_ref, hs_ref, wm_ref, wn_ref, op_ref, os_ref, wbf_ref, *, n_plain, shift):
    j = pl.program_id(0)
    i = pl.program_id(1)
    tn = wbf_ref.shape[0]

    @pl.when(jnp.logical_and(i == 0, j < n_plain))
    def _():
        wbf_ref[...] = wm_ref[...].astype(BF16)

    @pl.when(jnp.logical_and(i == 0, j >= n_plain))
    def _():
        wbf_ref[0:tn - shift, :] = wm_ref[shift:tn, :].astype(BF16)
        wbf_ref[tn - shift:tn, :] = wn_ref[...].astype(BF16)

    @pl.when(i == 0)
    def _():
        os_ref[...] = _dot_nt(hs_ref[...], wbf_ref[...]).astype(os_ref.dtype)

    op_ref[...] = _dot_nt(hp_ref[...], wbf_ref[...]).astype(op_ref.dtype)


def _in_proj(h_p, h_s, w_in_t, plain_cols, shift, out_cols, tn):
    m_p, d = h_p.shape
    tail = h_s.shape[0]
    tm = _row_tile(m_p, ROW_TILE_WIDE)
    assert plain_cols % tn == 0 and out_cols % tn == 0 and tn % shift == 0 and shift % 8 == 0
    n_plain = plain_cols // tn
    kern = functools.partial(_in_proj_kernel, n_plain=n_plain, shift=shift)
    return pl.pallas_call(
        kern,
        grid=(out_cols // tn, m_p // tm),
        in_specs=[
            pl.BlockSpec((tm, d), lambda j, i: (i, 0)),
            pl.BlockSpec((tail, d), lambda j, i: (0, 0)),
            pl.BlockSpec((tn, d), lambda j, i: (j, 0)),
            pl.BlockSpec((shift, d), lambda j, i: ((j + 1) * (tn // shift), 0)),
        ],
        out_specs=[pl.BlockSpec((tm, tn), lambda j, i: (i, j)), pl.BlockSpec((tail, tn), lambda j, i: (0, j))],
        out_shape=[jax.ShapeDtypeStruct((m_p, out_cols), BF16), jax.ShapeDtypeStruct((tail, out_cols), BF16)],
        scratch_shapes=[pltpu.VMEM((tn, d), BF16)],
        compiler_params=_params(
            2, _nbytes((tm, d), BF16), _nbytes((tail, d), BF16), _nbytes((tn, d), F32), _nbytes((shift, d), F32),
            _nbytes((tm, tn), BF16), _nbytes((tail, tn), BF16),
            scratch_bytes=_nbytes((tn, d), BF16) + _nbytes((tm, tn), F32)),
        name="in_proj",
    )(h_p, h_s, w_in_t, w_in_t)


def _prefix_sum_rows(sel3_bf16, g):
    g0 = g.astype(BF16)
    r1 = g - g0.astype(F32)
    g1 = r1.astype(BF16)
    g2 = (r1 - g1.astype(F32)).astype(BF16)
    return _dot(sel3_bf16, jnp.concatenate([g0, g1, g2], axis=0))


def _lane_bcast_cols(row, n):
    parts = []
    for c in range(n // V7X_LANES):
        tile = jnp.broadcast_to(row[:, c * V7X_LANES:(c + 1) * V7X_LANES], (V7X_LANES, V7X_LANES))
        parts.append(tile.T)
    return parts[0] if len(parts) == 1 else jnp.concatenate(parts, axis=0)


def _rms_gate_store(o, w, gate, out_ref, rows, cols):
    ms = jnp.mean(o * o, axis=-1, keepdims=True)
    y = o * lax.rsqrt(ms + EPS) * w
    out_ref[rows, cols] = (y * _silu(gate)).astype(out_ref.dtype)


def _ln_gate_store(o, w, gate, out_ref, rows, cols):
    mu = jnp.mean(o, axis=-1, keepdims=True)
    dlt = o - mu
    var = jnp.mean(dlt * dlt, axis=-1, keepdims=True)
    y = dlt * lax.rsqrt(var + EPS) * w
    out_ref[rows, cols] = (y * _silu(gate)).astype(out_ref.dtype)


def _token_selectors(n_tok):
    assert 3 * n_tok <= V7X_LANES
    j = lax.broadcasted_iota(jnp.int32, (V7X_LANES, V7X_LANES), 0)
    sel = []
    for t in range(n_tok):
        hit = jnp.logical_or(j == t, jnp.logical_or(j == n_tok + t, j == 2 * n_tok + t))
        sel.append(jnp.where(hit, 1.0, 0.0).astype(BF16))
    return jnp.stack(sel, axis=0)


def _column_source(x):
    n_tok, w = x.shape
    hi = x.astype(BF16).astype(F32)
    r1 = x - hi
    mid = r1.astype(BF16).astype(F32)
    lo = (r1 - mid).astype(BF16).astype(F32)
    x3 = jnp.concatenate([hi, mid, lo, jnp.zeros((V7X_LANES - 3 * n_tok, w), F32)], axis=0)
    parts = [x3[:, c * V7X_LANES:(c + 1) * V7X_LANES].T for c in range(w // V7X_LANES)]
    return (parts[0] if len(parts) == 1 else jnp.concatenate(parts, axis=0)).astype(BF16)


def _decode_advance(tok0, decay_rows_fn, decay_const_fn, k_ref, q_ref, v_ref, s_in_ref, s_out_ref, o_ref, sel_ref,
                    *, heads, dk, dv):
    n_tok = s_in_ref.shape[0]
    reps = dv // V7X_LANES
    rows = pl.ds(pl.multiple_of(tok0, n_tok), n_tok)

    def cols(src, tt):
        return jnp.concatenate([_dot(src, sel_ref[tt])] * reps, axis=1)

    for hh in range(heads):
        kc = slice(hh * dk, (hh + 1) * dk)
        vc = slice(hh * dv, (hh + 1) * dv)
        k_src, q_src = _column_source(k_ref[rows, kc]), _column_source(q_ref[rows, kc])
        a_src = None if decay_rows_fn is None else _column_source(decay_rows_fn(rows, hh))
        v = v_ref[rows, vc]
        o_rows = []
        for tt in range(n_tok):
            decay = decay_const_fn(hh) if a_src is None else cols(a_src, tt)
            s_new = decay * s_in_ref[tt, hh] + cols(k_src, tt) * v[tt:tt + 1, :]
            s_out_ref[tt, hh] = s_new
            o_rows.append(jnp.sum(cols(q_src, tt) * s_new, axis=0, keepdims=True))
        o_ref[rows, vc] = jnp.concatenate(o_rows, axis=0)


def _decode_plan(n_dec, n_steps):
    assert n_dec % n_steps == 0 and (n_dec // n_steps) % 8 == 0, (n_dec, n_steps)
    return n_dec // n_steps


def _gla_sum_matrices(c):
    levels = c.bit_length() - 1
    assert 1 << levels == c
    i = lax.broadcasted_iota(jnp.int32, (c, c), 0)
    j = lax.broadcasted_iota(jnp.int32, (c, c), 1)
    mats = [j <= i]
    for l in range(levels):
        ref = jnp.bitwise_or(jnp.bitwise_and(i, -(2 << l)), 1 << l)
        mats.append(jnp.logical_and(j > jnp.minimum(i, ref), j <= jnp.maximum(i, ref)))
    mats.append(j > i)
    sel = jnp.concatenate([jnp.where(m, 1.0, 0.0).astype(BF16) for m in mats], axis=0)
    return jnp.concatenate([sel, sel, sel], axis=1)


def _pair_level(c):
    levels = c.bit_length() - 1
    i = lax.broadcasted_iota(jnp.int32, (c, c), 0)
    j = lax.broadcasted_iota(jnp.int32, (c, c), 1)
    x = jnp.bitwise_xor(i, j)
    lvl = jnp.zeros((c, c), jnp.int32)
    for l in range(1, levels):
        lvl = lvl + jnp.where(x >= (1 << l), 1, 0)
    return jnp.where(i > j, lvl, jnp.where(i == j, levels, -1))


def _queries_else_keys(q, k, l):
    c = q.shape[0]
    span = 1 << l
    if span >= 8:
        parts = [(q if (b & 1) else k)[b * span:(b + 1) * span, :] for b in range(c // span)]
        return jnp.concatenate(parts, axis=0)
    row = lax.broadcasted_iota(jnp.int32, q.shape, 0)
    return jnp.where(jnp.bitwise_and(row, span) != 0, q, k)


def _gla_level_scores(q, k, sums):
    c = q.shape[0]
    levels = c.bit_length() - 1
    out = []
    for l in range(levels):
        x = _queries_else_keys(q, k, l) * jnp.exp2(sums[(1 + l) * c:(2 + l) * c, :])
        xb = x.astype(BF16)
        out.append(_dot_nt(xb, xb))
    return out


def _gla_chunk_out(q, k, v, sums, level_scores, pair_level, state):
    c = q.shape[0]
    levels = c.bit_length() - 1
    scores = jnp.where(pair_level == levels, jnp.sum(q * k, axis=-1, keepdims=True), 0.0)
    for l in range(levels):
        scores = jnp.where(pair_level == l, level_scores[l], scores)
    o = _dot((q * jnp.exp2(sums[0:c, :])).astype(BF16), state.astype(BF16))
    return o + _dot(scores.astype(BF16), v)


def _gla_next_state(k, v, sums, state):
    c, dk = k.shape
    levels = c.bit_length() - 1
    k_tail = (k * jnp.exp2(sums[(levels + 1) * c:(levels + 2) * c, :])).astype(BF16)
    decay = _lane_bcast_cols(jnp.exp2(sums[c - 1:c, :]), dk)
    decay_full = jnp.concatenate([decay] * (v.shape[1] // V7X_LANES), axis=1)
    return decay_full * state + _dot_tn(k_tail, v)


def _gla_kernel(q_ref, k_ref, v_ref, ga_ref, g_ref, qd_ref, kd_ref, vd_ref, gad_ref, gd_ref, nw_ref, sd_in_ref,
                o_ref, s_out_ref, od_ref, sd_out_ref,
                s_ref, mats_ref, lvl_ref, a_dec, q_dec, k_dec, v_dec, o_dec, sel_ref, *, heads, dk, dv):
    t = pl.program_id(1)
    step = pl.program_id(0) * pl.num_programs(1) + t
    last_step = pl.num_programs(0) * pl.num_programs(1) - 1

    @pl.when(t == 0)
    def _():
        s_ref[...] = jnp.zeros_like(s_ref)
        mats_ref[...] = _gla_sum_matrices(GLA_CHUNK)
        lvl_ref[...] = _pair_level(GLA_CHUNK)

    @pl.when(step == 0)
    def _():
        a_dec[...] = jnp.exp2(gd_ref[...])
        q_dec[...] = qd_ref[...].astype(F32) * (dk ** -0.5)
        k_dec[...] = kd_ref[...].astype(F32)
        v_dec[...] = vd_ref[...].astype(F32)
        sel_ref[...] = _token_selectors(sd_in_ref.shape[0])

    _decode_advance(step * sd_in_ref.shape[0], lambda rows, hh: a_dec[rows, hh * dk:(hh + 1) * dk], None,
                    k_dec, q_dec, v_dec, sd_in_ref, sd_out_ref, o_dec, sel_ref, heads=heads, dk=dk, dv=dv)

    @pl.when(step == last_step)
    def _():
        n_dec = o_dec.shape[0]
        for hh in range(heads):
            vc = slice(hh * dv, (hh + 1) * dv)
            _rms_gate_store(o_dec[:, vc], nw_ref[...], gad_ref[:, vc].astype(F32), od_ref, slice(0, n_dec), vc)

    ct = q_ref.shape[0]

    kcs = [slice(hh * dk, (hh + 1) * dk) for hh in range(heads)]
    vcs = [slice(hh * dv, (hh + 1) * dv) for hh in range(heads)]
    group = 2 if (ct // GLA_CHUNK) % 2 == 0 else 1

    def chunk_group(cg, carry):
        pair_level = lvl_ref[...]
        rows, sums, qs, ks, lvl_scores = [], [], [], [], []
        for u in range(group):
            r = pl.ds(pl.multiple_of((cg * group + u) * GLA_CHUNK, GLA_CHUNK), GLA_CHUNK)
            rows.append(r)
            sums.append(_prefix_sum_rows(mats_ref[...], g_ref[r, :]))
            qs.append([q_ref[r, kc].astype(F32) * (dk ** -0.5) for kc in kcs])
            ks.append([k_ref[r, kc].astype(F32) for kc in kcs])
            lvl_scores.append([_gla_level_scores(qs[u][hh], ks[u][hh], sums[u][:, kcs[hh]]) for hh in range(heads)])
        outs = []
        for u in range(group):
            r = rows[u]
            outs.append([_gla_chunk_out(qs[u][hh], ks[u][hh], v_ref[r, vcs[hh]], sums[u][:, kcs[hh]],
                                        lvl_scores[u][hh], pair_level, s_ref[hh]) for hh in range(heads)])
            for hh in range(heads):
                s_ref[hh] = _gla_next_state(ks[u][hh], v_ref[r, vcs[hh]], sums[u][:, kcs[hh]], s_ref[hh])
        for u in range(group):
            for hh in range(heads):
                _rms_gate_store(outs[u][hh], nw_ref[...], ga_ref[rows[u], vcs[hh]].astype(F32), o_ref, rows[u],
                                vcs[hh])
        return carry

    lax.fori_loop(0, ct // GLA_CHUNK // group, chunk_group, 0)

    @pl.when(t == pl.num_programs(1) - 1)
    def _():
        s_out_ref[0] = s_ref[...]


def _gla(proj, log2_decay, proj_dec, log2_decay_dec, norm_w, state_dec, lay, batch, seq):
    heads, dk, dv = lay["gla_heads"], lay["gla_dk"], lay["gla_dv"]
    qk, vw = heads * dk, heads * dv
    ct = min(seq, GLA_STEP_CHUNKS * GLA_CHUNK)
    levels = GLA_CHUNK.bit_length() - 1
    assert seq % ct == 0 and ct % GLA_CHUNK == 0
    nt = seq // ct
    n_dec = proj_dec.shape[0]
    tps = _decode_plan(n_dec, batch * nt)
    row = lambda b, t: b * nt + t
    dec_in = lambda width, col: pl.BlockSpec((n_dec, width), lambda b, t: (0, col // width))
    dec_state = pl.BlockSpec((tps, heads, dk, dv), lambda b, t: (row(b, t), 0, 0, 0))
    kern = functools.partial(_gla_kernel, heads=heads, dk=dk, dv=dv)
    return pl.pallas_call(
        kern,
        grid=(batch, nt),
        in_specs=[
            pl.BlockSpec((ct, qk), lambda b, t: (row(b, t), lay["qa"] // qk)),
            pl.BlockSpec((ct, qk), lambda b, t: (row(b, t), lay["ka"] // qk)),
            pl.BlockSpec((ct, vw), lambda b, t: (row(b, t), lay["va"] // vw)),
            pl.BlockSpec((ct, vw), lambda b, t: (row(b, t), lay["ga"] // vw)),
            pl.BlockSpec((ct, qk), lambda b, t: (row(b, t), 0)),
            dec_in(qk, lay["qa"]), dec_in(qk, lay["ka"]), dec_in(vw, lay["va"]), dec_in(vw, lay["ga"]),
            dec_in(qk, 0),
            pl.BlockSpec((1, dv), lambda b, t: (0, 0)),
            dec_state,
        ],
        out_specs=[
            pl.BlockSpec((ct, vw), lambda b, t: (row(b, t), 0)),
            pl.BlockSpec((1, heads, dk, dv), lambda b, t: (b, 0, 0, 0)),
            pl.BlockSpec((n_dec, vw), lambda b, t: (0, 0)),
            dec_state,
        ],
        out_shape=[
            jax.ShapeDtypeStruct((batch * seq, vw), BF16),
            jax.ShapeDtypeStruct((batch, heads, dk, dv), F32),
            jax.ShapeDtypeStruct((n_dec, vw), BF16),
            jax.ShapeDtypeStruct(state_dec.shape, state_dec.dtype),
        ],
        scratch_shapes=[
            pltpu.VMEM((heads, dk, dv), F32),
            pltpu.VMEM(((levels + 2) * GLA_CHUNK, 3 * GLA_CHUNK), BF16),
            pltpu.VMEM((GLA_CHUNK, GLA_CHUNK), jnp.int32),
            pltpu.VMEM((n_dec, qk), F32), pltpu.VMEM((n_dec, qk), F32), pltpu.VMEM((n_dec, qk), F32),
            pltpu.VMEM((n_dec, vw), F32), pltpu.VMEM((n_dec, vw), F32),
            pltpu.VMEM((tps, V7X_LANES, V7X_LANES), BF16),
        ],
        compiler_params=_params(2, claim_all=True),
        name="gla",
    )(proj, proj, proj, proj, log2_decay, proj_dec, proj_dec, proj_dec, proj_dec, log2_decay_dec, norm_w, state_dec)


def _rotary(x, cos, sin):
    half = x.shape[1] // 2
    x1, x2 = x[:, :half], x[:, half:]
    return jnp.concatenate([x1 * cos - x2 * sin, x1 * sin + x2 * cos], axis=1)


def _ret_kernel(q_ref, k_ref, v_ref, gb_ref, cos_ref, sin_ref, qd_ref, kd_ref, vd_ref, gbd_ref, cosd_ref, sind_ref,
                lg_ref, nw_ref, sd_in_ref,
                o_ref, s_out_ref, od_ref, sd_out_ref,
                s_ref, dmat_ref, qdec_ref, kdec_ref, q_dec, k_dec, v_dec, o_dec, sel_ref, *, heads, dk, dv, c):
    t = pl.program_id(1)
    step = pl.program_id(0) * pl.num_programs(1) + t
    last_step = pl.num_programs(0) * pl.num_programs(1) - 1

    @pl.when(step == 0)
    def _():
        cosd, sind = cosd_ref[0:1, :], sind_ref[0:1, :]
        for hh in range(heads):
            kc = slice(hh * dk, (hh + 1) * dk)
            q_dec[:, kc] = _rotary(qd_ref[:, kc].astype(F32), cosd, sind)
            k_dec[:, kc] = _rotary(kd_ref[:, kc].astype(F32), cosd, sind) * (dk ** -0.5)
        v_dec[...] = vd_ref[...].astype(F32)
        sel_ref[...] = _token_selectors(sd_in_ref.shape[0])

    def gamma(hh):
        return jnp.exp(jnp.concatenate([lg_ref[hh]] * (dv // V7X_LANES), axis=1))

    _decode_advance(step * sd_in_ref.shape[0], None, gamma, k_dec, q_dec, v_dec, sd_in_ref, sd_out_ref, o_dec,
                    sel_ref, heads=heads, dk=dk, dv=dv)

    @pl.when(step == last_step)
    def _():
        n_dec = o_dec.shape[0]
        for hh in range(heads):
            vc = slice(hh * dv, (hh + 1) * dv)
            _ln_gate_store(o_dec[:, vc], nw_ref[...], gbd_ref[:, vc].astype(F32), od_ref, slice(0, n_dec), vc)

    @pl.when(t == 0)
    def _():
        s_ref[...] = jnp.zeros_like(s_ref)
        ri = lax.broadcasted_iota(jnp.int32, (c, c), 0)
        rj = lax.broadcasted_iota(jnp.int32, (c, c), 1)
        dist = (ri - rj).astype(F32)
        rowl = lax.broadcasted_iota(jnp.int32, (c, V7X_LANES), 0).astype(F32)
        for hh in range(heads):
            lg = lg_ref[hh]
            dmat_ref[hh] = jnp.exp(jnp.where(ri >= rj, dist * lg[:, :1], -jnp.inf))
            qdec_ref[hh] = jnp.exp((rowl + 1.0) * lg)
            kdec_ref[hh] = jnp.exp((float(c - 1) - rowl) * lg)

    ct = q_ref.shape[0]
    kcs = [slice(hh * dk, (hh + 1) * dk) for hh in range(heads)]
    vcs = [slice(hh * dv, (hh + 1) * dv) for hh in range(heads)]

    def chunk(ci, carry):
        rows = pl.ds(pl.multiple_of(ci * c, c), c)
        cos, sin = cos_ref[rows, :], sin_ref[rows, :]
        qrs = [_rotary(q_ref[rows, kc].astype(F32), cos, sin).astype(BF16) for kc in kcs]
        krs = [_rotary(k_ref[rows, kc].astype(F32), cos, sin) * (dk ** -0.5) for kc in kcs]
        scores = [_dot_nt(qrs[hh], krs[hh].astype(BF16)) * dmat_ref[hh] for hh in range(heads)]
        outs = []
        for hh in range(heads):
            qdec = jnp.concatenate([qdec_ref[hh]] * (dv // V7X_LANES), axis=1)
            o = qdec * _dot(qrs[hh], s_ref[hh].astype(BF16))
            outs.append(o + _dot(scores[hh].astype(BF16), v_ref[rows, vcs[hh]]))
        for hh in range(heads):
            kdec = jnp.concatenate([kdec_ref[hh]] * (dk // V7X_LANES), axis=1)
            k_tail = (krs[hh] * kdec).astype(BF16)
            lgv = jnp.concatenate([lg_ref[hh]] * (dv // V7X_LANES), axis=1)
            s_ref[hh] = jnp.exp(float(c) * lgv) * s_ref[hh] + _dot_tn(k_tail, v_ref[rows, vcs[hh]])
        for hh in range(heads):
            _ln_gate_store(outs[hh], nw_ref[...], gb_ref[rows, vcs[hh]].astype(F32), o_ref, rows, vcs[hh])
        return carry

    lax.fori_loop(0, ct // c, chunk, 0)

    @pl.when(t == pl.num_programs(1) - 1)
    def _():
        s_out_ref[0] = s_ref[...]


def _ret(proj, cos, sin, proj_dec, cos_dec, sin_dec, log_gamma, norm_w, state_dec, lay, batch, seq):
    heads, dk, dv = lay["ret_heads"], lay["ret_dk"], lay["ret_dv"]
    qk, vw = heads * dk, heads * dv
    c = min(seq, RET_CHUNK)
    ct = min(seq, RET_STEP_CHUNKS * c)
    assert seq % ct == 0 and ct % c == 0
    nt = seq // ct
    half = dk // 2
    n_dec = proj_dec.shape[0]
    tps = _decode_plan(n_dec, batch * nt)
    row = lambda b, t: b * nt + t
    dec_in = lambda width, col: pl.BlockSpec((n_dec, width), lambda b, t: (0, col // width))
    dec_state = pl.BlockSpec((tps, heads, dk, dv), lambda b, t: (row(b, t), 0, 0, 0))
    table_dec = pl.BlockSpec((cos_dec.shape[0], half), lambda b, t: (0, 0))
    kern = functools.partial(_ret_kernel, heads=heads, dk=dk, dv=dv, c=c)
    return pl.pallas_call(
        kern,
        grid=(batch, nt),
        in_specs=[
            pl.BlockSpec((ct, qk), lambda b, t: (row(b, t), lay["qb"] // qk)),
            pl.BlockSpec((ct, qk), lambda b, t: (row(b, t), lay["kb"] // qk)),
            pl.BlockSpec((ct, vw), lambda b, t: (row(b, t), lay["vb"] // vw)),
            pl.BlockSpec((ct, vw), lambda b, t: (row(b, t), lay["gb"] // vw)),
            pl.BlockSpec((ct, half), lambda b, t: (t, 0)),
            pl.BlockSpec((ct, half), lambda b, t: (t, 0)),
            dec_in(qk, lay["qb"]), dec_in(qk, lay["kb"]), dec_in(vw, lay["vb"]), dec_in(vw, lay["gb"]),
            table_dec, table_dec,
            pl.BlockSpec((heads, 1, V7X_LANES), lambda b, t: (0, 0, 0)),
            pl.BlockSpec((1, dv), lambda b, t: (0, 0)),
            dec_state,
        ],
        out_specs=[
            pl.BlockSpec((ct, vw), lambda b, t: (row(b, t), 0)),
            pl.BlockSpec((1, heads, dk, dv), lambda b, t: (b, 0, 0, 0)),
            pl.BlockSpec((n_dec, vw), lambda b, t: (0, 0)),
            dec_state,
        ],
        out_shape=[
            jax.ShapeDtypeStruct((batch * seq, vw), BF16),
            jax.ShapeDtypeStruct((batch, heads, dk, dv), F32),
            jax.ShapeDtypeStruct((n_dec, vw), BF16),
            jax.ShapeDtypeStruct(state_dec.shape, state_dec.dtype),
        ],
        scratch_shapes=[
            pltpu.VMEM((heads, dk, dv), F32),
            pltpu.VMEM((heads, c, c), F32),
            pltpu.VMEM((heads, c, V7X_LANES), F32),
            pltpu.VMEM((heads, c, V7X_LANES), F32),
            pltpu.VMEM((n_dec, qk), F32), pltpu.VMEM((n_dec, qk), F32),
            pltpu.VMEM((n_dec, vw), F32), pltpu.VMEM((n_dec, vw), F32),
            pltpu.VMEM((tps, V7X_LANES, V7X_LANES), BF16),
        ],
        compiler_params=_params(2, claim_all=True),
        name="ret",
    )(proj, proj, proj, proj, cos, sin, proj_dec, proj_dec, proj_dec, proj_dec, cos_dec, sin_dec,
      log_gamma, norm_w, state_dec)


def _merge_kernel(oap_ref, obp_ref, oas_ref, obs_ref, wa_ref, wb_ref, g0p_ref, g1p_ref, g0s_ref, g1s_ref, wnext_ref,
                  mp_ref, ms_ref, wnext_bf_ref, wa_bf, wb_bf):
    wnext_bf_ref[...] = wnext_ref[...].astype(BF16)

    def merged(oa, ob, g0, g1):
        ya = _dot(oa, wa_bf[...])
        yb = _dot(ob, wb_bf[...])
        return _sigmoid(g0.astype(F32)) * ya + _sigmoid(g1.astype(F32)) * yb

    @pl.when(pl.program_id(1) == 0)
    def _():
        wa_bf[...] = wa_ref[...].astype(BF16)
        wb_bf[...] = wb_ref[...].astype(BF16)
        ms_ref[...] = merged(oas_ref[...], obs_ref[...], g0s_ref[...], g1s_ref[...]).astype(ms_ref.dtype)

    mp_ref[...] = merged(oap_ref[...], obp_ref[...], g0p_ref[...], g1p_ref[...]).astype(mp_ref.dtype)


def _slab_specs(w_next, n_steps, step_of):
    kn, dn = w_next.shape
    assert kn % n_steps == 0 and (kn // n_steps) % 16 == 0, (kn, n_steps)
    slab = kn // n_steps
    spec = pl.BlockSpec((slab, dn), lambda j, i: (step_of(j, i), 0))
    return spec, spec, jax.ShapeDtypeStruct((kn, dn), BF16), _nbytes((slab, dn), F32) + _nbytes((slab, dn), BF16)


def _merge(oa_p, ob_p, oa_s, ob_s, wa, wb, proj_p, proj_s, lay, w_next):
    m_p, ka = oa_p.shape
    kb = ob_p.shape[1]
    tail = oa_s.shape[0]
    d = wa.shape[1]
    tm = _row_tile(m_p, ROW_TILE)
    tn = min(d, 1024)
    assert d % tn == 0 and lay["mg"] % tn == 0
    g0 = lay["mg"] // tn
    g1 = (lay["mg"] + d) // tn
    n_m = m_p // tm
    slab_in, slab_out, slab_shape, slab_bytes = _slab_specs(w_next, (d // tn) * n_m, lambda j, i: j * n_m + i)
    return pl.pallas_call(
        _merge_kernel,
        grid=(d // tn, n_m),
        in_specs=[
            pl.BlockSpec((tm, ka), lambda j, i: (i, 0)),
            pl.BlockSpec((tm, kb), lambda j, i: (i, 0)),
            pl.BlockSpec((tail, ka), lambda j, i: (0, 0)),
            pl.BlockSpec((tail, kb), lambda j, i: (0, 0)),
            pl.BlockSpec((ka, tn), lambda j, i: (0, j)),
            pl.BlockSpec((kb, tn), lambda j, i: (0, j)),
            pl.BlockSpec((tm, tn), lambda j, i: (i, g0 + j)),
            pl.BlockSpec((tm, tn), lambda j, i: (i, g1 + j)),
            pl.BlockSpec((tail, tn), lambda j, i: (0, g0 + j)),
            pl.BlockSpec((tail, tn), lambda j, i: (0, g1 + j)),
            slab_in,
        ],
        out_specs=[pl.BlockSpec((tm, tn), lambda j, i: (i, j)), pl.BlockSpec((tail, tn), lambda j, i: (0, j)),
                   slab_out],
        out_shape=[jax.ShapeDtypeStruct((m_p, d), BF16), jax.ShapeDtypeStruct((tail, d), BF16), slab_shape],
        scratch_shapes=[pltpu.VMEM((ka, tn), BF16), pltpu.VMEM((kb, tn), BF16)],
        compiler_params=_params(
            2, _nbytes((tm, ka), BF16), _nbytes((tm, kb), BF16), _nbytes((tail, ka), BF16), _nbytes((tail, kb), BF16),
            _nbytes((ka, tn), F32), _nbytes((kb, tn), F32), 3 * _nbytes((tm, tn), BF16), 3 * _nbytes((tail, tn), BF16),
            slab_bytes,
            scratch_bytes=_nbytes((ka, tn), BF16) + _nbytes((kb, tn), BF16) + 3 * _nbytes((tm, tn), F32)),
        name="merge",
    )(oa_p, ob_p, oa_s, ob_s, wa, wb, proj_p, proj_p, proj_s, proj_s, w_next)


def _proj_res_norm_kernel(*refs, emit_sum, group, n_blocks):
    ap_refs, as_refs, w_refs = refs[:group], refs[group:2 * group], refs[2 * group:3 * group]
    resp_ref, ress_ref, nw_ref = refs[3 * group:3 * group + 3]
    out_refs = refs[3 * group + 3:]
    n_out = 2 if emit_sum else 1
    outs_p, outs_s = out_refs[:n_out], out_refs[n_out:]
    i = pl.program_id(0)
    k = pl.program_id(1)
    last_k = k == pl.num_programs(1) - 1
    d = w_refs[0].shape[1]
    col_chunk = min(d, 512)
    rest = n_blocks % group

    def step(a_refs, res_ref, outs):
        acc_ref = outs[0]
        nrow = acc_ref.shape[0]
        row_chunk = min(nrow, 128)
        assert nrow % row_chunk == 0

        @pl.when(k == 0)
        def _():
            acc_ref[...] = res_ref[...]

        def accumulate(n_used):
            a = [a_refs[s][...] for s in range(n_used)]
            for c in range(d // col_chunk):
                cs = slice(c * col_chunk, (c + 1) * col_chunk)
                part = _dot(a[0], w_refs[0][:, cs])
                for s in range(1, n_used):
                    part = part + _dot(a[s], w_refs[s][:, cs])
                acc_ref[:, cs] += part

        if rest == 0:
            accumulate(group)
        else:
            pl.when(jnp.logical_not(last_k))(lambda: accumulate(group))
            pl.when(last_k)(lambda: accumulate(rest))

        @pl.when(last_k)
        def _():
            def body(c, carry):
                rr = pl.ds(pl.multiple_of(c * row_chunk, row_chunk), row_chunk)
                y = _rmsnorm_rows(acc_ref[rr, :], nw_ref[...])
                if emit_sum:
                    outs[1][rr, :] = y.astype(outs[1].dtype)
                else:
                    acc_ref[rr, :] = y
                return carry

            lax.fori_loop(0, nrow // row_chunk, body, 0)

    step(ap_refs, resp_ref, outs_p)

    @pl.when(i == pl.num_programs(0) - 1)
    def _():
        step(as_refs, ress_ref, outs_s)


def _proj_res_norm(a_p, a_s, w, res_p, res_s, norm_w, emit_sum, group):
    m_p, kdim = a_p.shape
    tail = a_s.shape[0]
    d = w.shape[1]
    tm = _row_tile(m_p, ROW_TILE)
    tk = min(kdim, K_BLOCK)
    assert kdim % tk == 0
    n_blocks = kdim // tk
    group = min(group, n_blocks)
    n_steps = pl.cdiv(n_blocks, group)
    blk = lambda k, s: jnp.minimum(k * group + s, n_blocks - 1)
    p_spec = pl.BlockSpec((tm, d), lambda i, k: (i, 0))
    s_spec = pl.BlockSpec((tail, d), lambda i, k: (0, 0))
    s3_spec = pl.BlockSpec((tail, None, d), lambda i, k: (0, 0, 0))
    res_s_spec = s3_spec if res_s.ndim == 3 else s_spec
    out_specs = [p_spec, s3_spec]
    out_shape = [jax.ShapeDtypeStruct((m_p, d), F32), jax.ShapeDtypeStruct((tail, 1, d), F32)]
    assert w.dtype == BF16
    blocks = [group * _nbytes((tm, tk), BF16), group * _nbytes((tail, tk), BF16), group * _nbytes((tk, d), BF16),
              2 * _nbytes((tm, d), F32), 2 * _nbytes((tail, d), F32)]
    if emit_sum:
        out_specs = [p_spec, p_spec, s_spec, s_spec]
        out_shape = [out_shape[0], jax.ShapeDtypeStruct((m_p, d), BF16),
                     jax.ShapeDtypeStruct((tail, d), F32), jax.ShapeDtypeStruct((tail, d), BF16)]
        blocks += [_nbytes((tm, d), BF16), _nbytes((tail, d), BF16)]
    in_specs = (
        [pl.BlockSpec((tm, tk), lambda i, k, s=s: (i, blk(k, s))) for s in range(group)]
        + [pl.BlockSpec((tail, tk), lambda i, k, s=s: (0, blk(k, s))) for s in range(group)]
        + [pl.BlockSpec((tk, d), lambda i, k, s=s: (blk(k, s), 0)) for s in range(group)]
        + [p_spec, res_s_spec, pl.BlockSpec((1, d), lambda i, k: (0, 0))])
    return pl.pallas_call(
        functools.partial(_proj_res_norm_kernel, emit_sum=emit_sum, group=group, n_blocks=n_blocks),
        grid=(m_p // tm, n_steps),
        in_specs=in_specs,
        out_specs=out_specs,
        out_shape=out_shape,
        compiler_params=_params(2, *blocks),
        name="proj_res_norm",
    )(*([a_p] * group + [a_s] * group + [w] * group), res_p, res_s, norm_w.reshape(1, d))


def _swiglu_kernel(hp_ref, hs_ref, wg_ref, wu_ref, wnext_ref, op_ref, os_ref, wnext_bf_ref, wg_bf, wu_bf):
    wnext_bf_ref[...] = wnext_ref[...].astype(BF16)

    tn = wg_bf.shape[1]
    col_chunk = min(tn, 256)

    def act(h_ref, o_ref):
        h = h_ref[...]
        for c in range(tn // col_chunk):
            cs = slice(c * col_chunk, (c + 1) * col_chunk)
            a = _dot(h, wg_bf[:, cs])
            b = _dot(h, wu_bf[:, cs])
            o_ref[:, cs] = (_silu(a) * b).astype(o_ref.dtype)

    @pl.when(pl.program_id(1) == 0)
    def _():
        wg_bf[...] = wg_ref[...].astype(BF16)
        wu_bf[...] = wu_ref[...].astype(BF16)
        act(hs_ref, os_ref)

    act(hp_ref, op_ref)


def _swiglu(h_p, h_s, wg, wu, w_next):
    m_p, d = h_p.shape
    tail = h_s.shape[0]
    f = wg.shape[1]
    tm = _row_tile(m_p, ROW_TILE_WIDE)
    tn = 512 if f % 512 == 0 else 256
    assert f % tn == 0
    n_m = m_p // tm
    slab_in, slab_out, slab_shape, slab_bytes = _slab_specs(w_next, (f // tn) * n_m, lambda j, i: j * n_m + i)
    return pl.pallas_call(
        _swiglu_kernel,
        grid=(f // tn, n_m),
        in_specs=[
            pl.BlockSpec((tm, d), lambda j, i: (i, 0)),
            pl.BlockSpec((tail, d), lambda j, i: (0, 0)),
            pl.BlockSpec((d, tn), lambda j, i: (0, j)),
            pl.BlockSpec((d, tn), lambda j, i: (0, j)),
            slab_in,
        ],
        out_specs=[pl.BlockSpec((tm, tn), lambda j, i: (i, j)), pl.BlockSpec((tail, tn), lambda j, i: (0, j)),
                   slab_out],
        out_shape=[jax.ShapeDtypeStruct((m_p, f), BF16), jax.ShapeDtypeStruct((tail, f), BF16), slab_shape],
        scratch_shapes=[pltpu.VMEM((d, tn), BF16), pltpu.VMEM((d, tn), BF16)],
        compiler_params=_params(
            2, _nbytes((tm, d), BF16), _nbytes((tail, d), BF16), 2 * _nbytes((d, tn), F32),
            _nbytes((tm, tn), BF16), _nbytes((tail, tn), BF16), slab_bytes,
            scratch_bytes=2 * _nbytes((d, tn), BF16) + 3 * _nbytes((tm, tn), F32)),
        name="swiglu",
    )(h_p, h_s, wg, wu, w_next)


def _layout(d_model, in_width, state_gla, state_ret, gate_rank):
    _, _, gh, gdk, gdv = state_gla.shape
    _, _, rh, rdk, rdv = state_ret.shape
    gqk, gv, rqk, rv = gh * gdk, gh * gdv, rh * rdk, rh * rdv
    lay = dict(gla_heads=gh, gla_dk=gdk, gla_dv=gdv, ret_heads=rh, ret_dk=rdk, ret_dv=rdv, rank=gate_rank)
    off = 0
    for name, width in (("qa", gqk), ("ka", gqk), ("va", gv), ("ga", gv), ("qb", rqk), ("kb", rqk),
                        ("vb", rv), ("gb", rv), ("mg", 2 * d_model)):
        lay[name] = off
        off += width
    lay["out_cols"] = off
    lay["plain_cols"] = 2 * gqk + gv
    lay["gd_src"] = lay["plain_cols"]
    assert lay["gd_src"] % V7X_LANES == 0 and gate_rank <= V7X_LANES
    assert in_width == off + gate_rank
    return lay


def _layer(x_p, x_s, st_gla, st_ret, wts, lay, log_gamma, final_norm):
    (norm_mix, w_in, w_gate_up, b_gate, gla_norm_w, w_gla_up, ret_norm_w, w_ret_up, w_out, norm_ffn,
     w_ffn_gate, w_ffn_up, w_ffn_down) = wts
    batch, seq, d = x_p.shape
    rank = lay["rank"]
    gqk = lay["gla_heads"] * lay["gla_dk"]
    bup = b_gate.reshape(1, gqk)
    gnw = gla_norm_w.reshape(1, -1)
    rnw = ret_norm_w.reshape(1, -1)
    tn = 1024 if (lay["out_cols"] % 1024 == 0 and lay["plain_cols"] % 1024 == 0) else 512
    xp = x_p.reshape(batch * seq, d)
    assert x_s.ndim == 3 and x_s.shape[1] == 1, "one new token per decode sequence"
    xs = x_s
    w_in_t = w_in.T

    h_p, h_s, g_p, g_s, cos_p, sin_p, cos_s, sin_s = _rmsnorm_gate(
        xp, xs, norm_mix, w_in_t, lay["gd_src"], w_gate_up, bup, seq, lay["ret_dk"] // 2)
    proj_p, proj_s = _in_proj(h_p, h_s, w_in_t, lay["plain_cols"], rank, lay["out_cols"], tn)
    oa_p, sa_p, oa_s, sa_s = _gla(proj_p, g_p, proj_s, g_s, gnw, st_gla, lay, batch, seq)
    ob_p, sb_p, ob_s, sb_s = _ret(proj_p, cos_p, sin_p, proj_s, cos_s, sin_s, log_gamma, rnw, st_ret, lay, batch, seq)
    m_p, m_s, w_out_bf = _merge(oa_p, ob_p, oa_s, ob_s, w_gla_up, w_ret_up, proj_p, proj_s, lay, w_out)
    x1_p, h2_p, x1_s, h2_s = _proj_res_norm(m_p, m_s, w_out_bf, xp, xs, norm_ffn, True, 2)
    act_p, act_s, w_down_bf = _swiglu(h2_p, h2_s, w_ffn_gate, w_ffn_up, w_ffn_down)
    y_p, y_s = _proj_res_norm(act_p, act_s, w_down_bf, x1_p, x1_s, final_norm, False, 3)
    return (y_p, sa_p, sb_p), (y_s, sa_s, sb_s)


def kernel(x_prompt, x_sample, state_gla, state_ret, norm_mix, w_in, w_gla_gate_up, b_gla_gate, gla_norm_w,
           w_gla_up, ret_norm_w, w_ret_up, w_out, norm_ffn, w_ffn_gate, w_ffn_up, w_ffn_down, norm_final):
    depth = w_in.shape[0]
    assert depth == 1, "single-layer trunk"
    batch, seq, d = x_prompt.shape
    lay = _layout(d, w_in.shape[-1], state_gla, state_ret, w_gla_gate_up.shape[1])
    rh, rdk = lay["ret_heads"], lay["ret_dk"]
    assert rdk // 2 == V7X_LANES
    lg = jnp.log1p(-jnp.exp(jnp.linspace(math.log(1.0 / 32), math.log(1.0 / 512), rh))).astype(F32)
    log_gamma = jnp.broadcast_to(lg[:, None, None], (rh, 1, V7X_LANES))

    wts = (norm_mix[0], w_in[0], w_gla_gate_up[0], b_gla_gate[0], gla_norm_w[0], w_gla_up[0], ret_norm_w[0],
           w_ret_up[0], w_out[0], norm_ffn[0], w_ffn_gate[0], w_ffn_up[0], w_ffn_down[0])
    (y_p, ga_p, re_p), (y_s, ga_s, re_s) = _layer(
        x_prompt, x_sample, state_gla[0], state_ret[0], wts, lay, log_gamma, norm_final)

    sd = state_gla.dtype
    return (y_p.reshape(batch, seq, d), y_s.reshape(x_sample.shape),
            ga_p[None].astype(sd), re_p[None].astype(state_ret.dtype),
            ga_s[None].astype(sd), re_s[None].astype(state_ret.dtype))
```

```python
import functools
import math

import numpy as np
import jax
import jax.numpy as jnp
from jax import lax
from jax.experimental import pallas as pl
from jax.experimental.pallas import tpu as pltpu

EPS = 1e-6
ROPE_BASE = 10000.0
GLA_GATE_NORM = 16.0
PAST_LEN = 16384

V7X_LANES = 128
V7X_VMEM_REQUEST_CAP = 60000 * 1024
COMPILER_SCRATCH_BYTES = 12 * 1024 * 1024

GLA_CHUNK = 64
GLA_STEP_CHUNKS = 8
LOG2_E = 1.4426950408889634
RET_CHUNK = 128
RET_STEP_CHUNKS = 4
ROW_TILE = 1024
ROW_TILE_WIDE = 2048
K_BLOCK = 512

BF16 = jnp.bfloat16
F32 = jnp.float32


def _params(n_axes, *block_bytes, scratch_bytes=0, claim_all=False):
    need = 2 * sum(block_bytes) + scratch_bytes + COMPILER_SCRATCH_BYTES
    if claim_all:
        need = V7X_VMEM_REQUEST_CAP
    return pltpu.CompilerParams(
        dimension_semantics=("arbitrary",) * n_axes,
        vmem_limit_bytes=int(min(V7X_VMEM_REQUEST_CAP, need)),
    )


def _nbytes(shape, dtype):
    return int(np.prod(shape)) * jnp.dtype(dtype).itemsize


def _sigmoid(x):
    return 1.0 / (1.0 + jnp.exp(-x))


def _silu(x):
    return x * _sigmoid(x)


def _log_sigmoid(x):
    return jnp.minimum(x, 0.0) - jnp.log(1.0 + jnp.exp(-jnp.abs(x)))


def _dot(a, b):
    return jnp.dot(a, b, preferred_element_type=F32)


def _dot_nt(a, b):
    return lax.dot_general(a, b, (((1,), (1,)), ((), ())), preferred_element_type=F32)


def _dot_tn(a, b):
    return lax.dot_general(a, b, (((0,), (0,)), ((), ())), preferred_element_type=F32)


def _row_tile(m, want):
    t = min(m, want)
    assert m % t == 0, (m, t)
    return t


def _rmsnorm_rows(x, w):
    ms = jnp.mean(x * x, axis=-1, keepdims=True)
    return x * lax.rsqrt(ms + EPS) * w


def _rope_rows(cos_ref, sin_ref, pos0):
    rows, half = cos_ref.shape
    pos = (lax.broadcasted_iota(jnp.int32, (rows, half), 0) + pos0).astype(F32)
    idx = lax.broadcasted_iota(jnp.int32, (rows, half), 1).astype(F32)
    ang = pos * jnp.exp(idx * (-math.log(ROPE_BASE) / half))
    cos_ref[...] = jnp.cos(ang)
    sin_ref[...] = jnp.sin(ang)


def _rmsnorm_kernel(xp_ref, xs_ref, w_ref, wgd_ref, wup_ref, bup_ref,
                    hp_ref, hs_ref, gp_ref, gs_ref, cosp_ref, sinp_ref, cosd_ref, sind_ref, wgd_bf, wup_bf):
    i = pl.program_id(0)

    @pl.when(i == 0)
    def _():
        wgd_bf[...] = wgd_ref[...].astype(BF16)
        wup_bf[...] = jnp.zeros_like(wup_bf)
        wup_bf[0:wup_ref.shape[0], :] = wup_ref[...].astype(BF16)

    def rows(x, h_ref, g_ref):
        h = _rmsnorm_rows(x, w_ref[...]).astype(h_ref.dtype)
        h_ref[...] = h
        gd = _dot_nt(h, wgd_bf[...])
        x = _dot(gd.astype(BF16), wup_bf[...]) + bup_ref[...]
        g_ref[...] = _log_sigmoid(x) * (LOG2_E / GLA_GATE_NORM)

    rows(xp_ref[...], hp_ref, gp_ref)
    _rope_rows(cosp_ref, sinp_ref, i * cosp_ref.shape[0])

    @pl.when(i == 0)
    def _():
        rows(xs_ref[...], hs_ref, gs_ref)
        _rope_rows(cosd_ref, sind_ref, PAST_LEN)


def _rmsnorm_gate(x_p, x_s, w, w_in_t, gate_row0, w_gate_up, bup, seq, half):
    m_p, d = x_p.shape
    tail = x_s.shape[0]
    rank, gw = w_gate_up.shape
    tm = _row_tile(m_p, ROW_TILE)
    n_steps = m_p // tm
    assert gate_row0 % V7X_LANES == 0 and seq % n_steps == 0 and rank % 16 == 0
    pos_rows = seq // n_steps
    table = pl.BlockSpec((pos_rows, half), lambda i: (i, 0))
    table_dec = pl.BlockSpec((8, half), lambda i: (0, 0))
    return pl.pallas_call(
        _rmsnorm_kernel,
        grid=(n_steps,),
        in_specs=[
            pl.BlockSpec((tm, d), lambda i: (i, 0)),
            pl.BlockSpec((tail, None, d), lambda i: (0, 0, 0)),
            pl.BlockSpec((1, d), lambda i: (0, 0)),
            pl.BlockSpec((V7X_LANES, d), lambda i: (gate_row0 // V7X_LANES, 0)),
            pl.BlockSpec((rank, gw), lambda i: (0, 0)),
            pl.BlockSpec((1, gw), lambda i: (0, 0)),
        ],
        out_specs=[
            pl.BlockSpec((tm, d), lambda i: (i, 0)), pl.BlockSpec((tail, d), lambda i: (0, 0)),
            pl.BlockSpec((tm, gw), lambda i: (i, 0)), pl.BlockSpec((tail, gw), lambda i: (0, 0)),
            table, table, table_dec, table_dec,
        ],
        out_shape=[
            jax.ShapeDtypeStruct((m_p, d), BF16), jax.ShapeDtypeStruct((tail, d), BF16),
            jax.ShapeDtypeStruct((m_p, gw), F32), jax.ShapeDtypeStruct((tail, gw), F32),
            jax.ShapeDtypeStruct((seq, half), F32), jax.ShapeDtypeStruct((seq, half), F32),
            jax.ShapeDtypeStruct((8, half), F32), jax.ShapeDtypeStruct((8, half), F32),
        ],
        scratch_shapes=[pltpu.VMEM((V7X_LANES, d), BF16), pltpu.VMEM((V7X_LANES, gw), BF16)],
        compiler_params=_params(1, claim_all=True),
        name="rmsnorm_gate",
    )(x_p, x_s, w.reshape(1, d), w_in_t, w_gate_up, bup)


def _project_columns(x, wt_ref, o_ref, pieces, col_chunk):
    for lo, hi, kind, scale in pieces:
        assert lo % col_chunk == 0 and hi % col_chunk == 0
        for c0 in range(lo, hi, col_chunk):
            y = _dot_nt(x, wt_ref[c0:c0 + col_chunk, :])
            if kind == "silu":
                y = _silu(y)
            elif kind == "sigmoid":
                y = _sigmoid(y)
            elif scale != 1.0:
                y = y * scale
            o_ref[:, c0:c0 + col_chunk] = y.astype(o_ref.dtype)


def _in_proj_kernel(hp_ref, hs_ref, wm_ref, wn_ref, op_ref, os_ref, wbf_ref, *, n_plain, shift, plans):
    j = pl.program_id(0)
    i = pl.program_id(1)
    tn = wbf_ref.shape[0]

    @pl.when(jnp.logical_and(i == 0, j < n_plain))
    def _():
        wbf_ref[...] = wm_ref[...].astype(BF16)

    @pl.when(jnp.logical_and(i == 0, j >= n_plain))
    def _():
        wbf_ref[0:tn - shift, :] = wm_ref[shift:tn, :].astype(BF16)
        wbf_ref[tn - shift:tn, :] = wn_ref[...].astype(BF16)

    def finish(x_ref, o_ref):
        for pieces, tiles in plans.items():
            @pl.when(functools.reduce(jnp.logical_or, [j == t for t in tiles]))
            def _(pieces=pieces):
                _project_columns(x_ref[...], wbf_ref, o_ref, pieces, min(tn, 256))

    @pl.when(i == 0)
    def _():
        finish(hs_ref, os_ref)

    finish(hp_ref, op_ref)


def _column_plans(segments, out_cols, tn):
    plans = {}
    for t in range(out_cols // tn):
        lo_t, hi_t = t * tn, (t + 1) * tn
        pieces = tuple((max(lo, lo_t) - lo_t, min(hi, hi_t) - lo_t, kind, scale)
                       for lo, hi, kind, scale in segments if lo < hi_t and hi > lo_t)
        assert pieces[0][0] == 0 and pieces[-1][1] == tn
        plans.setdefault(pieces, []).append(t)
    return plans


def _in_proj(h_p, h_s, w_in_t, plain_cols, shift, out_cols, tn, segments):
    m_p, d = h_p.shape
    tail = h_s.shape[0]
    tm = _row_tile(m_p, ROW_TILE_WIDE)
    assert plain_cols % tn == 0 and out_cols % tn == 0 and tn % shift == 0 and shift % 8 == 0
    n_plain = plain_cols // tn
    kern = functools.partial(_in_proj_kernel, n_plain=n_plain, shift=shift,
                             plans=_column_plans(segments, out_cols, tn))
    return pl.pallas_call(
        kern,
        grid=(out_cols // tn, m_p // tm),
        in_specs=[
            pl.BlockSpec((tm, d), lambda j, i: (i, 0)),
            pl.BlockSpec((tail, d), lambda j, i: (0, 0)),
            pl.BlockSpec((tn, d), lambda j, i: (j, 0)),
            pl.BlockSpec((shift, d), lambda j, i: ((j + 1) * (tn // shift), 0)),
        ],
        out_specs=[pl.BlockSpec((tm, tn), lambda j, i: (i, j)), pl.BlockSpec((tail, tn), lambda j, i: (0, j))],
        out_shape=[jax.ShapeDtypeStruct((m_p, out_cols), BF16), jax.ShapeDtypeStruct((tail, out_cols), BF16)],
        scratch_shapes=[pltpu.VMEM((tn, d), BF16)],
        compiler_params=_params(
            2, _nbytes((tm, d), BF16), _nbytes((tail, d), BF16), _nbytes((tn, d), F32), _nbytes((shift, d), F32),
            _nbytes((tm, tn), BF16), _nbytes((tail, tn), BF16),
            scratch_bytes=_nbytes((tn, d), BF16) + _nbytes((tm, tn), F32)),
        name="in_proj",
    )(h_p, h_s, w_in_t, w_in_t)


def _prefix_sum_rows(sel3_bf16, g):
    g0 = g.astype(BF16)
    r1 = g - g0.astype(F32)
    g1 = r1.astype(BF16)
    g2 = (r1 - g1.astype(F32)).astype(BF16)
    return _dot(sel3_bf16, jnp.concatenate([g0, g1, g2], axis=0))


def _lane_bcast_cols(row, n):
    parts = []
    for c in range(n // V7X_LANES):
        tile = jnp.broadcast_to(row[:, c * V7X_LANES:(c + 1) * V7X_LANES], (V7X_LANES, V7X_LANES))
        parts.append(tile.T)
    return parts[0] if len(parts) == 1 else jnp.concatenate(parts, axis=0)


def _rms_gate_store(o, w, gate, out_ref, rows, cols):
    ms = jnp.mean(o * o, axis=-1, keepdims=True)
    y = o * lax.rsqrt(ms + EPS) * w
    out_ref[rows, cols] = (y * gate).astype(out_ref.dtype)


def _ln_gate_store(o, w, gate, out_ref, rows, cols):
    mu = jnp.mean(o, axis=-1, keepdims=True)
    dlt = o - mu
    var = jnp.mean(dlt * dlt, axis=-1, keepdims=True)
    y = dlt * lax.rsqrt(var + EPS) * w
    out_ref[rows, cols] = (y * gate).astype(out_ref.dtype)


def _token_selectors(n_tok):
    assert 3 * n_tok <= V7X_LANES
    j = lax.broadcasted_iota(jnp.int32, (V7X_LANES, V7X_LANES), 0)
    sel = []
    for t in range(n_tok):
        hit = jnp.logical_or(j == t, jnp.logical_or(j == n_tok + t, j == 2 * n_tok + t))
        sel.append(jnp.where(hit, 1.0, 0.0).astype(BF16))
    return jnp.stack(sel, axis=0)


def _column_source(x):
    n_tok, w = x.shape
    hi = x.astype(BF16).astype(F32)
    r1 = x - hi
    mid = r1.astype(BF16).astype(F32)
    lo = (r1 - mid).astype(BF16).astype(F32)
    x3 = jnp.concatenate([hi, mid, lo, jnp.zeros((V7X_LANES - 3 * n_tok, w), F32)], axis=0)
    parts = [x3[:, c * V7X_LANES:(c + 1) * V7X_LANES].T for c in range(w // V7X_LANES)]
    return (parts[0] if len(parts) == 1 else jnp.concatenate(parts, axis=0)).astype(BF16)


def _decode_advance(tok0, decay_rows_fn, decay_const_fn, k_ref, q_ref, v_ref, s_in_ref, s_out_ref, o_ref, sel_ref,
                    *, heads, dk, dv):
    n_tok = s_in_ref.shape[0]
    reps = dv // V7X_LANES
    rows = pl.ds(pl.multiple_of(tok0, n_tok), n_tok)

    def cols(src, tt):
        return jnp.concatenate([_dot(src, sel_ref[tt])] * reps, axis=1)

    for hh in range(heads):
        kc = slice(hh * dk, (hh + 1) * dk)
        vc = slice(hh * dv, (hh + 1) * dv)
        k_src, q_src = _column_source(k_ref[rows, kc]), _column_source(q_ref[rows, kc])
        a_src = None if decay_rows_fn is None else _column_source(decay_rows_fn(rows, hh))
        v = v_ref[rows, vc]
        o_rows = []
        for tt in range(n_tok):
            decay = decay_const_fn(hh) if a_src is None else cols(a_src, tt)
            s_new = decay * s_in_ref[tt, hh] + cols(k_src, tt) * v[tt:tt + 1, :]
            s_out_ref[tt, hh] = s_new
            o_rows.append(jnp.sum(cols(q_src, tt) * s_new, axis=0, keepdims=True))
        o_ref[rows, vc] = jnp.concatenate(o_rows, axis=0)


def _decode_plan(n_dec, n_steps):
    assert n_dec % n_steps == 0 and (n_dec // n_steps) % 8 == 0, (n_dec, n_steps)
    return n_dec // n_steps


def _gla_sum_matrices(c):
    levels = c.bit_length() - 1
    assert 1 << levels == c
    i = lax.broadcasted_iota(jnp.int32, (c, c), 0)
    j = lax.broadcasted_iota(jnp.int32, (c, c), 1)
    mats = [j <= i]
    for l in range(levels):
        ref = jnp.bitwise_or(jnp.bitwise_and(i, -(2 << l)), 1 << l)
        mats.append(jnp.logical_and(j > jnp.minimum(i, ref), j <= jnp.maximum(i, ref)))
    mats.append(j > i)
    sel = jnp.concatenate([jnp.where(m, 1.0, 0.0).astype(BF16) for m in mats], axis=0)
    return jnp.concatenate([sel, sel, sel], axis=1)


def _pair_level(c):
    levels = c.bit_length() - 1
    i = lax.broadcasted_iota(jnp.int32, (c, c), 0)
    j = lax.broadcasted_iota(jnp.int32, (c, c), 1)
    x = jnp.bitwise_xor(i, j)
    lvl = jnp.zeros((c, c), jnp.int32)
    for l in range(1, levels):
        lvl = lvl + jnp.where(x >= (1 << l), 1, 0)
    return jnp.where(i > j, lvl, jnp.where(i == j, levels, -1))


def _queries_else_keys(q, k, l):
    c = q.shape[0]
    span = 1 << l
    if span >= 8:
        parts = [(q if (b & 1) else k)[b * span:(b + 1) * span, :] for b in range(c // span)]
        return jnp.concatenate(parts, axis=0)
    row = lax.broadcasted_iota(jnp.int32, q.shape, 0)
    return jnp.where(jnp.bitwise_and(row, span) != 0, q, k)


def _gla_level_scores(q, k, sums):
    c = q.shape[0]
    levels = c.bit_length() - 1
    out = []
    for l in range(levels):
        x = _queries_else_keys(q, k, l) * jnp.exp2(sums[(1 + l) * c:(2 + l) * c, :])
        xb = x.astype(BF16)
        out.append(_dot_nt(xb, xb))
    return out


def _gla_chunk_out(q, k, v, sums, level_scores, pair_level, state):
    c = q.shape[0]
    levels = c.bit_length() - 1
    scores = jnp.where(pair_level == levels, jnp.sum(q * k, axis=-1, keepdims=True), 0.0)
    for l in range(levels):
        scores = jnp.where(pair_level == l, level_scores[l], scores)
    o = _dot((q * jnp.exp2(sums[0:c, :])).astype(BF16), state.astype(BF16))
    return o + _dot(scores.astype(BF16), v)


def _gla_next_state(k, v, sums, state):
    c, dk = k.shape
    levels = c.bit_length() - 1
    k_tail = (k * jnp.exp2(sums[(levels + 1) * c:(levels + 2) * c, :])).astype(BF16)
    decay = _lane_bcast_cols(jnp.exp2(sums[c - 1:c, :]), dk)
    decay_full = jnp.concatenate([decay] * (v.shape[1] // V7X_LANES), axis=1)
    return decay_full * state + _dot_tn(k_tail, v)


def _gla_kernel(q_ref, k_ref, v_ref, ga_ref, g_ref, qd_ref, kd_ref, vd_ref, gad_ref, gd_ref, nw_ref, sd_in_ref,
                o_ref, s_out_ref, od_ref, sd_out_ref,
                s_ref, mats_ref, lvl_ref, a_dec, q_dec, k_dec, v_dec, o_dec, sel_ref, *, heads, dk, dv):
    t = pl.program_id(1)
    step = pl.program_id(0) * pl.num_programs(1) + t
    last_step = pl.num_programs(0) * pl.num_programs(1) - 1

    @pl.when(t == 0)
    def _():
        s_ref[...] = jnp.zeros_like(s_ref)
        mats_ref[...] = _gla_sum_matrices(GLA_CHUNK)
        lvl_ref[...] = _pair_level(GLA_CHUNK)

    @pl.when(step == 0)
    def _():
        a_dec[...] = jnp.exp2(gd_ref[...])
        q_dec[...] = qd_ref[...].astype(F32)
        k_dec[...] = kd_ref[...].astype(F32)
        v_dec[...] = vd_ref[...].astype(F32)
        sel_ref[...] = _token_selectors(sd_in_ref.shape[0])

    _decode_advance(step * sd_in_ref.shape[0], lambda rows, hh: a_dec[rows, hh * dk:(hh + 1) * dk], None,
                    k_dec, q_dec, v_dec, sd_in_ref, sd_out_ref, o_dec, sel_ref, heads=heads, dk=dk, dv=dv)

    @pl.when(step == last_step)
    def _():
        n_dec = o_dec.shape[0]
        for hh in range(heads):
            vc = slice(hh * dv, (hh + 1) * dv)
            _rms_gate_store(o_dec[:, vc], nw_ref[...], gad_ref[:, vc].astype(F32), od_ref, slice(0, n_dec), vc)

    ct = q_ref.shape[0]

    kcs = [slice(hh * dk, (hh + 1) * dk) for hh in range(heads)]
    vcs = [slice(hh * dv, (hh + 1) * dv) for hh in range(heads)]
    group = 2 if (ct // GLA_CHUNK) % 2 == 0 else 1

    def chunk_group(cg, carry):
        pair_level = lvl_ref[...]
        rows, sums, qs, ks, lvl_scores = [], [], [], [], []
        for u in range(group):
            r = pl.ds(pl.multiple_of((cg * group + u) * GLA_CHUNK, GLA_CHUNK), GLA_CHUNK)
            rows.append(r)
            sums.append(_prefix_sum_rows(mats_ref[...], g_ref[r, :]))
            qs.append([q_ref[r, kc].astype(F32) for kc in kcs])
            ks.append([k_ref[r, kc].astype(F32) for kc in kcs])
            lvl_scores.append([_gla_level_scores(qs[u][hh], ks[u][hh], sums[u][:, kcs[hh]]) for hh in range(heads)])
        outs = []
        for u in range(group):
            r = rows[u]
            outs.append([_gla_chunk_out(qs[u][hh], ks[u][hh], v_ref[r, vcs[hh]], sums[u][:, kcs[hh]],
                                        lvl_scores[u][hh], pair_level, s_ref[hh]) for hh in range(heads)])
            for hh in range(heads):
                s_ref[hh] = _gla_next_state(ks[u][hh], v_ref[r, vcs[hh]], sums[u][:, kcs[hh]], s_ref[hh])
        for u in range(group):
            for hh in range(heads):
                _rms_gate_store(outs[u][hh], nw_ref[...], ga_ref[rows[u], vcs[hh]].astype(F32), o_ref, rows[u],
                                vcs[hh])
        return carry

    lax.fori_loop(0, ct // GLA_CHUNK // group, chunk_group, 0)

    @pl.when(t == pl.num_programs(1) - 1)
    def _():
        s_out_ref[0] = s_ref[...]


def _gla(proj, log2_decay, proj_dec, log2_decay_dec, norm_w, state_dec, lay, batch, seq):
    heads, dk, dv = lay["gla_heads"], lay["gla_dk"], lay["gla_dv"]
    qk, vw = heads * dk, heads * dv
    ct = min(seq, GLA_STEP_CHUNKS * GLA_CHUNK)
    levels = GLA_CHUNK.bit_length() - 1
    assert seq % ct == 0 and ct % GLA_CHUNK == 0
    nt = seq // ct
    n_dec = proj_dec.shape[0]
    tps = _decode_plan(n_dec, batch * nt)
    row = lambda b, t: b * nt + t
    dec_in = lambda width, col: pl.BlockSpec((n_dec, width), lambda b, t: (0, col // width))
    dec_state = pl.BlockSpec((tps, heads, dk, dv), lambda b, t: (row(b, t), 0, 0, 0))
    kern = functools.partial(_gla_kernel, heads=heads, dk=dk, dv=dv)
    return pl.pallas_call(
        kern,
        grid=(batch, nt),
        in_specs=[
            pl.BlockSpec((ct, qk), lambda b, t: (row(b, t), lay["qa"] // qk)),
            pl.BlockSpec((ct, qk), lambda b, t: (row(b, t), lay["ka"] // qk)),
            pl.BlockSpec((ct, vw), lambda b, t: (row(b, t), lay["va"] // vw)),
            pl.BlockSpec((ct, vw), lambda b, t: (row(b, t), lay["ga"] // vw)),
            pl.BlockSpec((ct, qk), lambda b, t: (row(b, t), 0)),
            dec_in(qk, lay["qa"]), dec_in(qk, lay["ka"]), dec_in(vw, lay["va"]), dec_in(vw, lay["ga"]),
            dec_in(qk, 0),
            pl.BlockSpec((1, dv), lambda b, t: (0, 0)),
            dec_state,
        ],
        out_specs=[
            pl.BlockSpec((ct, vw), lambda b, t: (row(b, t), 0)),
            pl.BlockSpec((1, heads, dk, dv), lambda b, t: (b, 0, 0, 0)),
            pl.BlockSpec((n_dec, vw), lambda b, t: (0, 0)),
            dec_state,
        ],
        out_shape=[
            jax.ShapeDtypeStruct((batch * seq, vw), BF16),
            jax.ShapeDtypeStruct((batch, heads, dk, dv), F32),
            jax.ShapeDtypeStruct((n_dec, vw), BF16),
            jax.ShapeDtypeStruct(state_dec.shape, state_dec.dtype),
        ],
        scratch_shapes=[
            pltpu.VMEM((heads, dk, dv), F32),
            pltpu.VMEM(((levels + 2) * GLA_CHUNK, 3 * GLA_CHUNK), BF16),
            pltpu.VMEM((GLA_CHUNK, GLA_CHUNK), jnp.int32),
            pltpu.VMEM((n_dec, qk), F32), pltpu.VMEM((n_dec, qk), F32), pltpu.VMEM((n_dec, qk), F32),
            pltpu.VMEM((n_dec, vw), F32), pltpu.VMEM((n_dec, vw), F32),
            pltpu.VMEM((tps, V7X_LANES, V7X_LANES), BF16),
        ],
        compiler_params=_params(2, claim_all=True),
        name="gla",
    )(proj, proj, proj, proj, log2_decay, proj_dec, proj_dec, proj_dec, proj_dec, log2_decay_dec, norm_w, state_dec)


def _rotary(x, cos, sin):
    half = x.shape[1] // 2
    x1, x2 = x[:, :half], x[:, half:]
    return jnp.concatenate([x1 * cos - x2 * sin, x1 * sin + x2 * cos], axis=1)


def _ret_kernel(q_ref, k_ref, v_ref, gb_ref, cos_ref, sin_ref, qd_ref, kd_ref, vd_ref, gbd_ref, cosd_ref, sind_ref,
                lg_ref, nw_ref, sd_in_ref,
                o_ref, s_out_ref, od_ref, sd_out_ref,
                s_ref, dmat_ref, qdec_ref, kdec_ref, q_dec, k_dec, v_dec, o_dec, sel_ref, *, heads, dk, dv, c):
    t = pl.program_id(1)
    step = pl.program_id(0) * pl.num_programs(1) + t
    last_step = pl.num_programs(0) * pl.num_programs(1) - 1

    @pl.when(step == 0)
    def _():
        cosd, sind = cosd_ref[0:1, :], sind_ref[0:1, :]
        for hh in range(heads):
            kc = slice(hh * dk, (hh + 1) * dk)
            q_dec[:, kc] = _rotary(qd_ref[:, kc].astype(F32), cosd, sind)
            k_dec[:, kc] = _rotary(kd_ref[:, kc].astype(F32), cosd, sind)
        v_dec[...] = vd_ref[...].astype(F32)
        sel_ref[...] = _token_selectors(sd_in_ref.shape[0])

    def gamma(hh):
        return jnp.exp(jnp.concatenate([lg_ref[hh]] * (dv // V7X_LANES), axis=1))

    _decode_advance(step * sd_in_ref.shape[0], None, gamma, k_dec, q_dec, v_dec, sd_in_ref, sd_out_ref, o_dec,
                    sel_ref, heads=heads, dk=dk, dv=dv)

    @pl.when(step == last_step)
    def _():
        n_dec = o_dec.shape[0]
        for hh in range(heads):
            vc = slice(hh * dv, (hh + 1) * dv)
            _ln_gate_store(o_dec[:, vc], nw_ref[...], gbd_ref[:, vc].astype(F32), od_ref, slice(0, n_dec), vc)

    @pl.when(t == 0)
    def _():
        s_ref[...] = jnp.zeros_like(s_ref)
        ri = lax.broadcasted_iota(jnp.int32, (c, c), 0)
        rj = lax.broadcasted_iota(jnp.int32, (c, c), 1)
        dist = (ri - rj).astype(F32)
        rowl = lax.broadcasted_iota(jnp.int32, (c, V7X_LANES), 0).astype(F32)
        for hh in range(heads):
            lg = lg_ref[hh]
            dmat_ref[hh] = jnp.exp(jnp.where(ri >= rj, dist * lg[:, :1], -jnp.inf))
            qdec_ref[hh] = jnp.exp((rowl + 1.0) * lg)
            kdec_ref[hh] = jnp.exp((float(c - 1) - rowl) * lg)

    ct = q_ref.shape[0]
    kcs = [slice(hh * dk, (hh + 1) * dk) for hh in range(heads)]
    vcs = [slice(hh * dv, (hh + 1) * dv) for hh in range(heads)]

    def chunk(ci, carry):
        rows = pl.ds(pl.multiple_of(ci * c, c), c)
        cos, sin = cos_ref[rows, :], sin_ref[rows, :]
        qrs = [_rotary(q_ref[rows, kc].astype(F32), cos, sin).astype(BF16) for kc in kcs]
        krs = [_rotary(k_ref[rows, kc].astype(F32), cos, sin) for kc in kcs]
        scores = [_dot_nt(qrs[hh], krs[hh].astype(BF16)) * dmat_ref[hh] for hh in range(heads)]
        outs = []
        for hh in range(heads):
            qdec = jnp.concatenate([qdec_ref[hh]] * (dv // V7X_LANES), axis=1)
            o = qdec * _dot(qrs[hh], s_ref[hh].astype(BF16))
            outs.append(o + _dot(scores[hh].astype(BF16), v_ref[rows, vcs[hh]]))
        for hh in range(heads):
            kdec = jnp.concatenate([kdec_ref[hh]] * (dk // V7X_LANES), axis=1)
            k_tail = (krs[hh] * kdec).astype(BF16)
            lgv = jnp.concatenate([lg_ref[hh]] * (dv // V7X_LANES), axis=1)
            s_ref[hh] = jnp.exp(float(c) * lgv) * s_ref[hh] + _dot_tn(k_tail, v_ref[rows, vcs[hh]])
        for hh in range(heads):
            _ln_gate_store(outs[hh], nw_ref[...], gb_ref[rows, vcs[hh]].astype(F32), o_ref, rows, vcs[hh])
        return carry

    lax.fori_loop(0, ct // c, chunk, 0)

    @pl.when(t == pl.num_programs(1) - 1)
    def _():
        s_out_ref[0] = s_ref[...]


def _ret(proj, cos, sin, proj_dec, cos_dec, sin_dec, log_gamma, norm_w, state_dec, lay, batch, seq):
    heads, dk, dv = lay["ret_heads"], lay["ret_dk"], lay["ret_dv"]
    qk, vw = heads * dk, heads * dv
    c = min(seq, RET_CHUNK)
    ct = min(seq, RET_STEP_CHUNKS * c)
    assert seq % ct == 0 and ct % c == 0
    nt = seq // ct
    half = dk // 2
    n_dec = proj_dec.shape[0]
    tps = _decode_plan(n_dec, batch * nt)
    row = lambda b, t: b * nt + t
    dec_in = lambda width, col: pl.BlockSpec((n_dec, width), lambda b, t: (0, col // width))
    dec_state = pl.BlockSpec((tps, heads, dk, dv), lambda b, t: (row(b, t), 0, 0, 0))
    table_dec = pl.BlockSpec((cos_dec.shape[0], half), lambda b, t: (0, 0))
    kern = functools.partial(_ret_kernel, heads=heads, dk=dk, dv=dv, c=c)
    return pl.pallas_call(
        kern,
        grid=(batch, nt),
        in_specs=[
            pl.BlockSpec((ct, qk), lambda b, t: (row(b, t), lay["qb"] // qk)),
            pl.BlockSpec((ct, qk), lambda b, t: (row(b, t), lay["kb"] // qk)),
            pl.BlockSpec((ct, vw), lambda b, t: (row(b, t), lay["vb"] // vw)),
            pl.BlockSpec((ct, vw), lambda b, t: (row(b, t), lay["gb"] // vw)),
            pl.BlockSpec((ct, half), lambda b, t: (t, 0)),
            pl.BlockSpec((ct, half), lambda b, t: (t, 0)),
            dec_in(qk, lay["qb"]), dec_in(qk, lay["kb"]), dec_in(vw, lay["vb"]), dec_in(vw, lay["gb"]),
            table_dec, table_dec,
            pl.BlockSpec((heads, 1, V7X_LANES), lambda b, t: (0, 0, 0)),
            pl.BlockSpec((1, dv), lambda b, t: (0, 0)),
            dec_state,
        ],
        out_specs=[
            pl.BlockSpec((ct, vw), lambda b, t: (row(b, t), 0)),
            pl.BlockSpec((1, heads, dk, dv), lambda b, t: (b, 0, 0, 0)),
            pl.BlockSpec((n_dec, vw), lambda b, t: (0, 0)),
            dec_state,
        ],
        out_shape=[
            jax.ShapeDtypeStruct((batch * seq, vw), BF16),
            jax.ShapeDtypeStruct((batch, heads, dk, dv), F32),
            jax.ShapeDtypeStruct((n_dec, vw), BF16),
            jax.ShapeDtypeStruct(state_dec.shape, state_dec.dtype),
        ],
        scratch_shapes=[
            pltpu.VMEM((heads, dk, dv), F32),
            pltpu.VMEM((heads, c, c), F32),
            pltpu.VMEM((heads, c, V7X_LANES), F32),
            pltpu.VMEM((heads, c, V7X_LANES), F32),
            pltpu.VMEM((n_dec, qk), F32), pltpu.VMEM((n_dec, qk), F32),
            pltpu.VMEM((n_dec, vw), F32), pltpu.VMEM((n_dec, vw), F32),
            pltpu.VMEM((tps, V7X_LANES, V7X_LANES), BF16),
        ],
        compiler_params=_params(2, claim_all=True),
        name="ret",
    )(proj, proj, proj, proj, cos, sin, proj_dec, proj_dec, proj_dec, proj_dec, cos_dec, sin_dec,
      log_gamma, norm_w, state_dec)


def _merge_kernel(oap_ref, obp_ref, oas_ref, obs_ref, wa_ref, wb_ref, g0p_ref, g1p_ref, g0s_ref, g1s_ref, wnext_ref,
                  mp_ref, ms_ref, wnext_bf_ref, wa_bf, wb_bf):
    wnext_bf_ref[...] = wnext_ref[...].astype(BF16)

    def merged(oa, ob, g0, g1):
        ya = _dot(oa, wa_bf[...])
        yb = _dot(ob, wb_bf[...])
        return g0.astype(F32) * ya + g1.astype(F32) * yb

    @pl.when(pl.program_id(1) == 0)
    def _():
        wa_bf[...] = wa_ref[...].astype(BF16)
        wb_bf[...] = wb_ref[...].astype(BF16)
        ms_ref[...] = merged(oas_ref[...], obs_ref[...], g0s_ref[...], g1s_ref[...]).astype(ms_ref.dtype)

    mp_ref[...] = merged(oap_ref[...], obp_ref[...], g0p_ref[...], g1p_ref[...]).astype(mp_ref.dtype)


def _slab_specs(w_next, n_steps, step_of):
    kn, dn = w_next.shape
    assert kn % n_steps == 0 and (kn // n_steps) % 16 == 0, (kn, n_steps)
    slab = kn // n_steps
    spec = pl.BlockSpec((slab, dn), lambda j, i: (step_of(j, i), 0))
    return spec, spec, jax.ShapeDtypeStruct((kn, dn), BF16), _nbytes((slab, dn), F32) + _nbytes((slab, dn), BF16)


def _merge(oa_p, ob_p, oa_s, ob_s, wa, wb, proj_p, proj_s, lay, w_next):
    m_p, ka = oa_p.shape
    kb = ob_p.shape[1]
    tail = oa_s.shape[0]
    d = wa.shape[1]
    tm = _row_tile(m_p, ROW_TILE)
    tn = min(d, 1024)
    assert d % tn == 0 and lay["mg"] % tn == 0
    g0 = lay["mg"] // tn
    g1 = (lay["mg"] + d) // tn
    n_m = m_p // tm
    slab_in, slab_out, slab_shape, slab_bytes = _slab_specs(w_next, (d // tn) * n_m, lambda j, i: j * n_m + i)
    return pl.pallas_call(
        _merge_kernel,
        grid=(d // tn, n_m),
        in_specs=[
            pl.BlockSpec((tm, ka), lambda j, i: (i, 0)),
            pl.BlockSpec((tm, kb), lambda j, i: (i, 0)),
            pl.BlockSpec((tail, ka), lambda j, i: (0, 0)),
            pl.BlockSpec((tail, kb), lambda j, i: (0, 0)),
            pl.BlockSpec((ka, tn), lambda j, i: (0, j)),
            pl.BlockSpec((kb, tn), lambda j, i: (0, j)),
            pl.BlockSpec((tm, tn), lambda j, i: (i, g0 + j)),
            pl.BlockSpec((tm, tn), lambda j, i: (i, g1 + j)),
            pl.BlockSpec((tail, tn), lambda j, i: (0, g0 + j)),
            pl.BlockSpec((tail, tn), lambda j, i: (0, g1 + j)),
            slab_in,
        ],
        out_specs=[pl.BlockSpec((tm, tn), lambda j, i: (i, j)), pl.BlockSpec((tail, tn), lambda j, i: (0, j)),
                   slab_out],
        out_shape=[jax.ShapeDtypeStruct((m_p, d), BF16), jax.ShapeDtypeStruct((tail, d), BF16), slab_shape],
        scratch_shapes=[pltpu.VMEM((ka, tn), BF16), pltpu.VMEM((kb, tn), BF16)],
        compiler_params=_params(
            2, _nbytes((tm, ka), BF16), _nbytes((tm, kb), BF16), _nbytes((tail, ka), BF16), _nbytes((tail, kb), BF16),
            _nbytes((ka, tn), F32), _nbytes((kb, tn), F32), 3 * _nbytes((tm, tn), BF16), 3 * _nbytes((tail, tn), BF16),
            slab_bytes,
            scratch_bytes=_nbytes((ka, tn), BF16) + _nbytes((kb, tn), BF16) + 3 * _nbytes((tm, tn), F32)),
        name="merge",
    )(oa_p, ob_p, oa_s, ob_s, wa, wb, proj_p, proj_p, proj_s, proj_s, w_next)


def _proj_res_norm_kernel(*refs, emit_sum, group, n_blocks):
    ap_refs, as_refs, w_refs = refs[:group], refs[group:2 * group], refs[2 * group:3 * group]
    resp_ref, ress_ref, nw_ref = refs[3 * group:3 * group + 3]
    out_refs = refs[3 * group + 3:]
    n_out = 2 if emit_sum else 1
    outs_p, outs_s = out_refs[:n_out], out_refs[n_out:]
    i = pl.program_id(0)
    k = pl.program_id(1)
    last_k = k == pl.num_programs(1) - 1
    d = w_refs[0].shape[1]
    col_chunk = min(d, 512)
    rest = n_blocks % group

    def step(a_refs, res_ref, outs):
        acc_ref = outs[0]
        nrow = acc_ref.shape[0]
        row_chunk = min(nrow, 128)
        assert nrow % row_chunk == 0

        @pl.when(k == 0)
        def _():
            acc_ref[...] = res_ref[...]

        def accumulate(n_used):
            a = [a_refs[s][...] for s in range(n_used)]
            for c in range(d // col_chunk):
                cs = slice(c * col_chunk, (c + 1) * col_chunk)
                part = _dot(a[0], w_refs[0][:, cs])
                for s in range(1, n_used):
                    part = part + _dot(a[s], w_refs[s][:, cs])
                acc_ref[:, cs] += part

        if rest == 0:
            accumulate(group)
        else:
            pl.when(jnp.logical_not(last_k))(lambda: accumulate(group))
            pl.when(last_k)(lambda: accumulate(rest))

        @pl.when(last_k)
        def _():
            def body(c, carry):
                rr = pl.ds(pl.multiple_of(c * row_chunk, row_chunk), row_chunk)
                y = _rmsnorm_rows(acc_ref[rr, :], nw_ref[...])
                if emit_sum:
                    outs[1][rr, :] = y.astype(outs[1].dtype)
                else:
                    acc_ref[rr, :] = y
                return carry

            lax.fori_loop(0, nrow // row_chunk, body, 0)

    step(ap_refs, resp_ref, outs_p)

    @pl.when(i == pl.num_programs(0) - 1)
    def _():
        step(as_refs, ress_ref, outs_s)


def _proj_res_norm(a_p, a_s, w, res_p, res_s, norm_w, emit_sum, group):
    m_p, kdim = a_p.shape
    tail = a_s.shape[0]
    d = w.shape[1]
    tm = _row_tile(m_p, ROW_TILE)
    tk = min(kdim, K_BLOCK)
    assert kdim % tk == 0
    n_blocks = kdim // tk
    group = min(group, n_blocks)
    n_steps = pl.cdiv(n_blocks, group)
    blk = lambda k, s: jnp.minimum(k * group + s, n_blocks - 1)
    p_spec = pl.BlockSpec((tm, d), lambda i, k: (i, 0))
    s_spec = pl.BlockSpec((tail, d), lambda i, k: (0, 0))
    s3_spec = pl.BlockSpec((tail, None, d), lambda i, k: (0, 0, 0))
    res_s_spec = s3_spec if res_s.ndim == 3 else s_spec
    out_specs = [p_spec, s3_spec]
    out_shape = [jax.ShapeDtypeStruct((m_p, d), F32), jax.ShapeDtypeStruct((tail, 1, d), F32)]
    assert w.dtype == BF16
    blocks = [group * _nbytes((tm, tk), BF16), group * _nbytes((tail, tk), BF16), group * _nbytes((tk, d), BF16),
              2 * _nbytes((tm, d), F32), 2 * _nbytes((tail, d), F32)]
    if emit_sum:
        out_specs = [p_spec, p_spec, s_spec, s_spec]
        out_shape = [out_shape[0], jax.ShapeDtypeStruct((m_p, d), BF16),
                     jax.ShapeDtypeStruct((tail, d), F32), jax.ShapeDtypeStruct((tail, d), BF16)]
        blocks += [_nbytes((tm, d), BF16), _nbytes((tail, d), BF16)]
    in_specs = (
        [pl.BlockSpec((tm, tk), lambda i, k, s=s: (i, blk(k, s))) for s in range(group)]
        + [pl.BlockSpec((tail, tk), lambda i, k, s=s: (0, blk(k, s))) for s in range(group)]
        + [pl.BlockSpec((tk, d), lambda i, k, s=s: (blk(k, s), 0)) for s in range(group)]
        + [p_spec, res_s_spec, pl.BlockSpec((1, d), lambda i, k: (0, 0))])
    return pl.pallas_call(
        functools.partial(_proj_res_norm_kernel, emit_sum=emit_sum, group=group, n_blocks=n_blocks),
        grid=(m_p // tm, n_steps),
        in_specs=in_specs,
        out_specs=out_specs,
        out_shape=out_shape,
        compiler_params=_params(2, *blocks),
        name="proj_res_norm",
    )(*([a_p] * group + [a_s] * group + [w] * group), res_p, res_s, norm_w.reshape(1, d))


def _swiglu_kernel(hp_ref, hs_ref, wg_ref, wu_ref, wnext_ref, op_ref, os_ref, wnext_bf_ref, wg_bf, wu_bf):
    wnext_bf_ref[...] = wnext_ref[...].astype(BF16)

    tn = wg_bf.shape[1]
    col_chunk = min(tn, 256)

    def act(h_ref, o_ref):
        h = h_ref[...]
        for c in range(tn // col_chunk):
            cs = slice(c * col_chunk, (c + 1) * col_chunk)
            a = _dot(h, wg_bf[:, cs])
            b = _dot(h, wu_bf[:, cs])
            o_ref[:, cs] = (_silu(a) * b).astype(o_ref.dtype)

    @pl.when(pl.program_id(1) == 0)
    def _():
        wg_bf[...] = wg_ref[...].astype(BF16)
        wu_bf[...] = wu_ref[...].astype(BF16)
        act(hs_ref, os_ref)

    act(hp_ref, op_ref)


def _swiglu(h_p, h_s, wg, wu, w_next):
    m_p, d = h_p.shape
    tail = h_s.shape[0]
    f = wg.shape[1]
    tm = _row_tile(m_p, ROW_TILE_WIDE)
    tn = 512 if f % 512 == 0 else 256
    assert f % tn == 0
    n_m = m_p // tm
    slab_in, slab_out, slab_shape, slab_bytes = _slab_specs(w_next, (f // tn) * n_m, lambda j, i: j * n_m + i)
    return pl.pallas_call(
        _swiglu_kernel,
        grid=(f // tn, n_m),
        in_specs=[
            pl.BlockSpec((tm, d), lambda j, i: (i, 0)),
            pl.BlockSpec((tail, d), lambda j, i: (0, 0)),
            pl.BlockSpec((d, tn), lambda j, i: (0, j)),
            pl.BlockSpec((d, tn), lambda j, i: (0, j)),
            slab_in,
        ],
        out_specs=[pl.BlockSpec((tm, tn), lambda j, i: (i, j)), pl.BlockSpec((tail, tn), lambda j, i: (0, j)),
                   slab_out],
        out_shape=[jax.ShapeDtypeStruct((m_p, f), BF16), jax.ShapeDtypeStruct((tail, f), BF16), slab_shape],
        scratch_shapes=[pltpu.VMEM((d, tn), BF16), pltpu.VMEM((d, tn), BF16)],
        compiler_params=_params(
            2, _nbytes((tm, d), BF16), _nbytes((tail, d), BF16), 2 * _nbytes((d, tn), F32),
            _nbytes((tm, tn), BF16), _nbytes((tail, tn), BF16), slab_bytes,
            scratch_bytes=2 * _nbytes((d, tn), BF16) + 3 * _nbytes((tm, tn), F32)),
        name="swiglu",
    )(h_p, h_s, wg, wu, w_next)


def _layout(d_model, in_width, state_gla, state_ret, gate_rank):
    _, _, gh, gdk, gdv = state_gla.shape
    _, _, rh, rdk, rdv = state_ret.shape
    gqk, gv, rqk, rv = gh * gdk, gh * gdv, rh * rdk, rh * rdv
    lay = dict(gla_heads=gh, gla_dk=gdk, gla_dv=gdv, ret_heads=rh, ret_dk=rdk, ret_dv=rdv, rank=gate_rank)
    off = 0
    for name, width in (("qa", gqk), ("ka", gqk), ("va", gv), ("ga", gv), ("qb", rqk), ("kb", rqk),
                        ("vb", rv), ("gb", rv), ("mg", 2 * d_model)):
        lay[name] = off
        off += width
    lay["out_cols"] = off
    finish = dict(qa=("scale", gdk ** -0.5), kb=("scale", rdk ** -0.5), ga=("silu", 1.0), gb=("silu", 1.0),
                  mg=("sigmoid", 1.0))
    names = ("qa", "ka", "va", "ga", "qb", "kb", "vb", "gb", "mg")
    ends = [lay[n] for n in names[1:]] + [off]
    lay["segments"] = [(lay[n], e) + finish.get(n, ("scale", 1.0)) for n, e in zip(names, ends)]
    lay["plain_cols"] = 2 * gqk + gv
    lay["gd_src"] = lay["plain_cols"]
    assert lay["gd_src"] % V7X_LANES == 0 and gate_rank <= V7X_LANES
    assert in_width == off + gate_rank
    return lay


def _layer(x_p, x_s, st_gla, st_ret, wts, lay, log_gamma, final_norm):
    (norm_mix, w_in, w_gate_up, b_gate, gla_norm_w, w_gla_up, ret_norm_w, w_ret_up, w_out, norm_ffn,
     w_ffn_gate, w_ffn_up, w_ffn_down) = wts
    batch, seq, d = x_p.shape
    rank = lay["rank"]
    gqk = lay["gla_heads"] * lay["gla_dk"]
    bup = b_gate.reshape(1, gqk)
    gnw = gla_norm_w.reshape(1, -1)
    rnw = ret_norm_w.reshape(1, -1)
    tn = 1024 if (lay["out_cols"] % 1024 == 0 and lay["plain_cols"] % 1024 == 0) else 512
    xp = x_p.reshape(batch * seq, d)
    assert x_s.ndim == 3 and x_s.shape[1] == 1, "one new token per decode sequence"
    xs = x_s
    w_in_t = w_in.T

    h_p, h_s, g_p, g_s, cos_p, sin_p, cos_s, sin_s = _rmsnorm_gate(
        xp, xs, norm_mix, w_in_t, lay["gd_src"], w_gate_up, bup, seq, lay["ret_dk"] // 2)
    proj_p, proj_s = _in_proj(h_p, h_s, w_in_t, lay["plain_cols"], rank, lay["out_cols"], tn, lay["segments"])
    oa_p, sa_p, oa_s, sa_s = _gla(proj_p, g_p, proj_s, g_s, gnw, st_gla, lay, batch, seq)
    ob_p, sb_p, ob_s, sb_s = _ret(proj_p, cos_p, sin_p, proj_s, cos_s, sin_s, log_gamma, rnw, st_ret, lay, batch, seq)
    m_p, m_s, w_out_bf = _merge(oa_p, ob_p, oa_s, ob_s, w_gla_up, w_ret_up, proj_p, proj_s, lay, w_out)
    x1_p, h2_p, x1_s, h2_s = _proj_res_norm(m_p, m_s, w_out_bf, xp, xs, norm_ffn, True, 2)
    act_p, act_s, w_down_bf = _swiglu(h2_p, h2_s, w_ffn_gate, w_ffn_up, w_ffn_down)
    y_p, y_s = _proj_res_norm(act_p, act_s, w_down_bf, x1_p, x1_s, final_norm, False, 3)
    return (y_p, sa_p, sb_p), (y_s, sa_s, sb_s)


def kernel(x_prompt, x_sample, state_gla, state_ret, norm_mix, w_in, w_gla_gate_up, b_gla_gate, gla_norm_w,
           w_gla_up, ret_norm_w, w_ret_up, w_out, norm_ffn, w_ffn_gate, w_ffn_up, w_ffn_down, norm_final):
    depth = w_in.shape[0]
    assert depth == 1, "single-layer trunk"
    batch, seq, d = x_prompt.shape
    lay = _layout(d, w_in.shape[-1], state_gla, state_ret, w_gla_gate_up.shape[1])
    rh, rdk = lay["ret_heads"], lay["ret_dk"]
    assert rdk // 2 == V7X_LANES
    lg = jnp.log1p(-jnp.exp(jnp.linspace(math.log(1.0 / 32), math.log(1.0 / 512), rh))).astype(F32)
    log_gamma = jnp.broadcast_to(lg[:, None, None], (rh, 1, V7X_LANES))

    wts = (norm_mix[0], w_in[0], w_gla_gate_up[0], b_gla_gate[0], gla_norm_w[0], w_gla_up[0], ret_norm_w[0],
           w_ret_up[0], w_out[0], norm_ffn[0], w_ffn_gate[0], w_ffn_up[0], w_ffn_down[0])
    (y_p, ga_p, re_p), (y_s, ga_s, re_s) = _layer(
        x_prompt, x_sample, state_gla[0], state_ret[0], wts, lay, log_gamma, norm_final)

    sd = state_gla.dtype
    return (y_p.reshape(batch, seq, d), y_s.reshape(x_sample.shape),
            ga_p[None].astype(sd), re_p[None].astype(state_ret.dtype),
            ga_s[None].astype(sd), re_s[None].astype(state_ret.dtype))
```

```python
import functools
import math

import numpy as np
import jax
import jax.numpy as jnp
from jax import lax
from jax.experimental import pallas as pl
from jax.experimental.pallas import tpu as pltpu

EPS = 1e-6
ROPE_BASE = 10000.0
GLA_GATE_NORM = 16.0
PAST_LEN = 16384

V7X_LANES = 128
V7X_VMEM_REQUEST_CAP = 60000 * 1024
COMPILER_SCRATCH_BYTES = 12 * 1024 * 1024

GLA_CHUNK = 64
GLA_STEP_CHUNKS = 8
LOG2_E = 1.4426950408889634
RET_CHUNK = 128
RET_STEP_CHUNKS = 4
ROW_TILE = 1024
ROW_TILE_WIDE = 2048
K_BLOCK = 512

BF16 = jnp.bfloat16
F32 = jnp.float32


def _params(n_axes, *block_bytes, scratch_bytes=0, claim_all=False):
    need = 2 * sum(block_bytes) + scratch_bytes + COMPILER_SCRATCH_BYTES
    if claim_all:
        need = V7X_VMEM_REQUEST_CAP
    return pltpu.CompilerParams(
        dimension_semantics=("arbitrary",) * n_axes,
        vmem_limit_bytes=int(min(V7X_VMEM_REQUEST_CAP, need)),
    )


def _nbytes(shape, dtype):
    return int(np.prod(shape)) * jnp.dtype(dtype).itemsize


def _sigmoid(x):
    return 1.0 / (1.0 + jnp.exp(-x))


def _silu(x):
    return x * _sigmoid(x)


def _log_sigmoid(x):
    return jnp.minimum(x, 0.0) - jnp.log(1.0 + jnp.exp(-jnp.abs(x)))


def _dot(a, b):
    return jnp.dot(a, b, preferred_element_type=F32)


def _dot_nt(a, b):
    return lax.dot_general(a, b, (((1,), (1,)), ((), ())), preferred_element_type=F32)


def _dot_tn(a, b):
    return lax.dot_general(a, b, (((0,), (0,)), ((), ())), preferred_element_type=F32)


def _row_tile(m, want):
    t = min(m, want)
    assert m % t == 0, (m, t)
    return t


def _rmsnorm_rows(x, w):
    ms = jnp.mean(x * x, axis=-1, keepdims=True)
    return x * lax.rsqrt(ms + EPS) * w


def _rope_rows(cos_ref, sin_ref, pos0):
    rows, half = cos_ref.shape
    pos = (lax.broadcasted_iota(jnp.int32, (rows, half), 0) + pos0).astype(F32)
    idx = lax.broadcasted_iota(jnp.int32, (rows, half), 1).astype(F32)
    ang = pos * jnp.exp(idx * (-math.log(ROPE_BASE) / half))
    cos_ref[...] = jnp.cos(ang)
    sin_ref[...] = jnp.sin(ang)


def _rmsnorm_kernel(xp_ref, xs_ref, w_ref, wgd_ref, wup_ref, bup_ref,
                    hp_ref, hs_ref, gp_ref, gs_ref, cosp_ref, sinp_ref, cosd_ref, sind_ref, wgd_bf, wup_bf):
    i = pl.program_id(0)

    @pl.when(i == 0)
    def _():
        wgd_bf[...] = wgd_ref[...].astype(BF16)
        wup_bf[...] = jnp.zeros_like(wup_bf)
        wup_bf[0:wup_ref.shape[0], :] = wup_ref[...].astype(BF16)

    def rows(x, h_ref, g_ref):
        h = _rmsnorm_rows(x, w_ref[...]).astype(h_ref.dtype)
        h_ref[...] = h
        gd = _dot_nt(h, wgd_bf[...])
        x = _dot(gd.astype(BF16), wup_bf[...]) + bup_ref[...]
        g_ref[...] = _log_sigmoid(x) * (LOG2_E / GLA_GATE_NORM)

    rows(xp_ref[...], hp_ref, gp_ref)
    _rope_rows(cosp_ref, sinp_ref, i * cosp_ref.shape[0])

    @pl.when(i == 0)
    def _():
        rows(xs_ref[...], hs_ref, gs_ref)
        _rope_rows(cosd_ref, sind_ref, PAST_LEN)


def _rmsnorm_gate(x_p, x_s, w, w_in_t, gate_row0, w_gate_up, bup, seq, half):
    m_p, d = x_p.shape
    tail = x_s.shape[0]
    rank, gw = w_gate_up.shape
    tm = _row_tile(m_p, ROW_TILE)
    n_steps = m_p // tm
    assert gate_row0 % V7X_LANES == 0 and seq % n_steps == 0 and rank % 16 == 0
    pos_rows = seq // n_steps
    table = pl.BlockSpec((pos_rows, half), lambda i: (i, 0))
    table_dec = pl.BlockSpec((8, half), lambda i: (0, 0))
    return pl.pallas_call(
        _rmsnorm_kernel,
        grid=(n_steps,),
        in_specs=[
            pl.BlockSpec((tm, d), lambda i: (i, 0)),
            pl.BlockSpec((tail, None, d), lambda i: (0, 0, 0)),
            pl.BlockSpec((1, d), lambda i: (0, 0)),
            pl.BlockSpec((V7X_LANES, d), lambda i: (gate_row0 // V7X_LANES, 0)),
            pl.BlockSpec((rank, gw), lambda i: (0, 0)),
            pl.BlockSpec((1, gw), lambda i: (0, 0)),
        ],
        out_specs=[
            pl.BlockSpec((tm, d), lambda i: (i, 0)), pl.BlockSpec((tail, d), lambda i: (0, 0)),
            pl.BlockSpec((tm, gw), lambda i: (i, 0)), pl.BlockSpec((tail, gw), lambda i: (0, 0)),
            table, table, table_dec, table_dec,
        ],
        out_shape=[
            jax.ShapeDtypeStruct((m_p, d), BF16), jax.ShapeDtypeStruct((tail, d), BF16),
            jax.ShapeDtypeStruct((m_p, gw), F32), jax.ShapeDtypeStruct((tail, gw), F32),
            jax.ShapeDtypeStruct((seq, half), F32), jax.ShapeDtypeStruct((seq, half), F32),
            jax.ShapeDtypeStruct((8, half), F32), jax.ShapeDtypeStruct((8, half), F32),
        ],
        scratch_shapes=[pltpu.VMEM((V7X_LANES, d), BF16), pltpu.VMEM((V7X_LANES, gw), BF16)],
        compiler_params=_params(1, claim_all=True),
        name="rmsnorm_gate",
    )(x_p, x_s, w.reshape(1, d), w_in_t, w_gate_up, bup)


def _in_proj_kernel(hp_ref, hs_ref, wm_ref, wn_ref, op_ref, os_ref, wbf_ref, *, n_plain, shift):
    j = pl.program_id(0)
    i = pl.program_id(1)
    tn = wbf_ref.shape[0]

    @pl.when(jnp.logical_and(i == 0, j < n_plain))
    def _():
        wbf_ref[...] = wm_ref[...].astype(BF16)

    @pl.when(jnp.logical_and(i == 0, j >= n_plain))
    def _():
        wbf_ref[0:tn - shift, :] = wm_ref[shift:tn, :].astype(BF16)
        wbf_ref[tn - shift:tn, :] = wn_ref[...].astype(BF16)

    @pl.when(i == 0)
    def _():
        os_ref[...] = _dot_nt(hs_ref[...], wbf_ref[...]).astype(os_ref.dtype)

    op_ref[...] = _dot_nt(hp_ref[...], wbf_ref[...]).astype(op_ref.dtype)


def _in_proj(h_p, h_s, w_in_t, plain_cols, shift, out_cols, tn):
    m_p, d = h_p.shape
    tail = h_s.shape[0]
    tm = _row_tile(m_p, ROW_TILE_WIDE)
    assert plain_cols % tn == 0 and out_cols % tn == 0 and tn % shift == 0 and shift % 8 == 0
    n_plain = plain_cols // tn
    kern = functools.partial(_in_proj_kernel, n_plain=n_plain, shift=shift)
    return pl.pallas_call(
        kern,
        grid=(out_cols // tn, m_p // tm),
        in_specs=[
            pl.BlockSpec((tm, d), lambda j, i: (i, 0)),
            pl.BlockSpec((tail, d), lambda j, i: (0, 0)),
            pl.BlockSpec((tn, d), lambda j, i: (j, 0)),
            pl.BlockSpec((shift, d), lambda j, i: ((j + 1) * (tn // shift), 0)),
        ],
        out_specs=[pl.BlockSpec((tm, tn), lambda j, i: (i, j)), pl.BlockSpec((tail, tn), lambda j, i: (0, j))],
        out_shape=[jax.ShapeDtypeStruct((m_p, out_cols), BF16), jax.ShapeDtypeStruct((tail, out_cols), BF16)],
        scratch_shapes=[pltpu.VMEM((tn, d), BF16)],
        compiler_params=_params(
            2, _nbytes((tm, d), BF16), _nbytes((tail, d), BF16), _nbytes((tn, d), F32), _nbytes((shift, d), F32),
            _nbytes((tm, tn), BF16), _nbytes((tail, tn), BF16),
            scratch_bytes=_nbytes((tn, d), BF16) + _nbytes((tm, tn), F32)),
        name="in_proj",
    )(h_p, h_s, w_in_t, w_in_t)


def _prefix_sum_rows(sel3_bf16, g):
    g0 = g.astype(BF16)
    r1 = g - g0.astype(F32)
    g1 = r1.astype(BF16)
    g2 = (r1 - g1.astype(F32)).astype(BF16)
    return _dot(sel3_bf16, jnp.concatenate([g0, g1, g2], axis=0))


def _lane_bcast_cols(row, n):
    parts = []
    for c in range(n // V7X_LANES):
        tile = jnp.broadcast_to(row[:, c * V7X_LANES:(c + 1) * V7X_LANES], (V7X_LANES, V7X_LANES))
        parts.append(tile.T)
    return parts[0] if len(parts) == 1 else jnp.concatenate(parts, axis=0)


def _rms_gate_store(o, w, gate, out_ref, rows, cols):
    ms = jnp.mean(o * o, axis=-1, keepdims=True)
    y = o * lax.rsqrt(ms + EPS) * w
    out_ref[rows, cols] = (y * _silu(gate)).astype(out_ref.dtype)


def _ln_gate_store(o, w, gate, out_ref, rows, cols):
    mu = jnp.mean(o, axis=-1, keepdims=True)
    dlt = o - mu
    var = jnp.mean(dlt * dlt, axis=-1, keepdims=True)
    y = dlt * lax.rsqrt(var + EPS) * w
    out_ref[rows, cols] = (y * _silu(gate)).astype(out_ref.dtype)


def _token_selectors(n_tok):
    assert 3 * n_tok <= V7X_LANES
    j = lax.broadcasted_iota(jnp.int32, (V7X_LANES, V7X_LANES), 0)
    sel = []
    for t in range(n_tok):
        hit = jnp.logical_or(j == t, jnp.logical_or(j == n_tok + t, j == 2 * n_tok + t))
        sel.append(jnp.where(hit, 1.0, 0.0).astype(BF16))
    return jnp.stack(sel, axis=0)


def _column_source(x):
    n_tok, w = x.shape
    hi = x.astype(BF16).astype(F32)
    r1 = x - hi
    mid = r1.astype(BF16).astype(F32)
    lo = (r1 - mid).astype(BF16).astype(F32)
    x3 = jnp.concatenate([hi, mid, lo, jnp.zeros((V7X_LANES - 3 * n_tok, w), F32)], axis=0)
    parts = [x3[:, c * V7X_LANES:(c + 1) * V7X_LANES].T for c in range(w // V7X_LANES)]
    return (parts[0] if len(parts) == 1 else jnp.concatenate(parts, axis=0)).astype(BF16)


def _decode_advance(tok0, decay_rows_fn, decay_const_fn, k_ref, q_ref, v_ref, s_in_ref, s_out_ref, o_ref, sel_ref,
                    *, heads, dk, dv):
    n_tok = s_in_ref.shape[0]
    reps = dv // V7X_LANES
    rows = pl.ds(pl.multiple_of(tok0, n_tok), n_tok)

    def cols(src, tt):
        return jnp.concatenate([_dot(src, sel_ref[tt])] * reps, axis=1)

    for hh in range(heads):
        kc = slice(hh * dk, (hh + 1) * dk)
        vc = slice(hh * dv, (hh + 1) * dv)
        k_src, q_src = _column_source(k_ref[rows, kc]), _column_source(q_ref[rows, kc])
        a_src = None if decay_rows_fn is None else _column_source(decay_rows_fn(rows, hh))
        v = v_ref[rows, vc]
        o_rows = []
        for tt in range(n_tok):
            decay = decay_const_fn(hh) if a_src is None else cols(a_src, tt)
            s_new = decay * s_in_ref[tt, hh] + cols(k_src, tt) * v[tt:tt + 1, :]
            s_out_ref[tt, hh] = s_new
            o_rows.append(jnp.sum(cols(q_src, tt) * s_new, axis=0, keepdims=True))
        o_ref[rows, vc] = jnp.concatenate(o_rows, axis=0)


def _decode_plan(n_dec, n_steps):
    assert n_dec % n_steps == 0 and (n_dec // n_steps) % 8 == 0, (n_dec, n_steps)
    return n_dec // n_steps


def _gla_decode_section(step, last_step, refs, scratch, *, heads, dk, dv):
    qd_ref, kd_ref, vd_ref, gad_ref, gd_ref, nw_ref, sd_in_ref, od_ref, sd_out_ref = refs
    a_dec, q_dec, k_dec, v_dec, o_dec, sel_ref = scratch

    @pl.when(step == 0)
    def _():
        a_dec[...] = jnp.exp2(gd_ref[...])
        q_dec[...] = qd_ref[...].astype(F32) * (dk ** -0.5)
        k_dec[...] = kd_ref[...].astype(F32)
        v_dec[...] = vd_ref[...].astype(F32)
        sel_ref[...] = _token_selectors(sd_in_ref.shape[0])

    _decode_advance(step * sd_in_ref.shape[0], lambda rows, hh: a_dec[rows, hh * dk:(hh + 1) * dk], None,
                    k_dec, q_dec, v_dec, sd_in_ref, sd_out_ref, o_dec, sel_ref, heads=heads, dk=dk, dv=dv)

    @pl.when(step == last_step)
    def _():
        n_dec = o_dec.shape[0]
        for hh in range(heads):
            vc = slice(hh * dv, (hh + 1) * dv)
            _rms_gate_store(o_dec[:, vc], nw_ref[...], gad_ref[:, vc].astype(F32), od_ref, slice(0, n_dec), vc)


def _ret_decode_section(step, last_step, refs, scratch, *, heads, dk, dv):
    qd_ref, kd_ref, vd_ref, gbd_ref, cosd_ref, sind_ref, lg_ref, nw_ref, sd_in_ref, od_ref, sd_out_ref = refs
    q_dec, k_dec, v_dec, o_dec, sel_ref = scratch

    @pl.when(step == 0)
    def _():
        cosd, sind = cosd_ref[0:1, :], sind_ref[0:1, :]
        for hh in range(heads):
            kc = slice(hh * dk, (hh + 1) * dk)
            q_dec[:, kc] = _rotary(qd_ref[:, kc].astype(F32), cosd, sind)
            k_dec[:, kc] = _rotary(kd_ref[:, kc].astype(F32), cosd, sind) * (dk ** -0.5)
        v_dec[...] = vd_ref[...].astype(F32)
        sel_ref[...] = _token_selectors(sd_in_ref.shape[0])

    def gamma(hh):
        return jnp.exp(jnp.concatenate([lg_ref[hh]] * (dv // V7X_LANES), axis=1))

    _decode_advance(step * sd_in_ref.shape[0], None, gamma, k_dec, q_dec, v_dec, sd_in_ref, sd_out_ref, o_dec,
                    sel_ref, heads=heads, dk=dk, dv=dv)

    @pl.when(step == last_step)
    def _():
        n_dec = o_dec.shape[0]
        for hh in range(heads):
            vc = slice(hh * dv, (hh + 1) * dv)
            _ln_gate_store(o_dec[:, vc], nw_ref[...], gbd_ref[:, vc].astype(F32), od_ref, slice(0, n_dec), vc)


def _decode_specs(kind, lay, n_dec, tps, row):
    heads, dk, dv = lay[kind + "_heads"], lay[kind + "_dk"], lay[kind + "_dv"]
    qk, vw = heads * dk, heads * dv
    names = ("qa", "ka", "va", "ga") if kind == "gla" else ("qb", "kb", "vb", "gb")
    widths = (qk, qk, vw, vw)
    rows_in = [pl.BlockSpec((n_dec, w), lambda b, t, c=lay[n] // w: (0, c)) for n, w in zip(names, widths)]
    state = pl.BlockSpec((tps, heads, dk, dv), lambda b, t: (row(b, t), 0, 0, 0))
    o_spec = pl.BlockSpec((n_dec, vw), lambda b, t: (0, 0))
    n_qk = 3 if kind == "gla" else 2
    scratch = ([pltpu.VMEM((n_dec, qk), F32)] * n_qk + [pltpu.VMEM((n_dec, vw), F32)] * 2
               + [pltpu.VMEM((tps, V7X_LANES, V7X_LANES), BF16)])
    return rows_in, state, o_spec, scratch


def _gla_sum_matrices(c):
    levels = c.bit_length() - 1
    assert 1 << levels == c
    i = lax.broadcasted_iota(jnp.int32, (c, c), 0)
    j = lax.broadcasted_iota(jnp.int32, (c, c), 1)
    mats = [j <= i]
    for l in range(levels):
        ref = jnp.bitwise_or(jnp.bitwise_and(i, -(2 << l)), 1 << l)
        mats.append(jnp.logical_and(j > jnp.minimum(i, ref), j <= jnp.maximum(i, ref)))
    mats.append(j > i)
    sel = jnp.concatenate([jnp.where(m, 1.0, 0.0).astype(BF16) for m in mats], axis=0)
    return jnp.concatenate([sel, sel, sel], axis=1)


def _pair_level(c):
    levels = c.bit_length() - 1
    i = lax.broadcasted_iota(jnp.int32, (c, c), 0)
    j = lax.broadcasted_iota(jnp.int32, (c, c), 1)
    x = jnp.bitwise_xor(i, j)
    lvl = jnp.zeros((c, c), jnp.int32)
    for l in range(1, levels):
        lvl = lvl + jnp.where(x >= (1 << l), 1, 0)
    return jnp.where(i > j, lvl, jnp.where(i == j, levels, -1))


def _queries_else_keys(q, k, l):
    c = q.shape[0]
    span = 1 << l
    if span >= 8:
        parts = [(q if (b & 1) else k)[b * span:(b + 1) * span, :] for b in range(c // span)]
        return jnp.concatenate(parts, axis=0)
    row = lax.broadcasted_iota(jnp.int32, q.shape, 0)
    return jnp.where(jnp.bitwise_and(row, span) != 0, q, k)


def _gla_level_scores(q, k, sums):
    c = q.shape[0]
    levels = c.bit_length() - 1
    out = []
    for l in range(levels):
        x = _queries_else_keys(q, k, l) * jnp.exp2(sums[(1 + l) * c:(2 + l) * c, :])
        xb = x.astype(BF16)
        out.append(_dot_nt(xb, xb))
    return out


def _gla_chunk_out(q, k, v, sums, level_scores, pair_level, state):
    c = q.shape[0]
    levels = c.bit_length() - 1
    scores = jnp.where(pair_level == levels, jnp.sum(q * k, axis=-1, keepdims=True), 0.0)
    for l in range(levels):
        scores = jnp.where(pair_level == l, level_scores[l], scores)
    o = _dot((q * jnp.exp2(sums[0:c, :])).astype(BF16), state.astype(BF16))
    return o + _dot(scores.astype(BF16), v)


def _gla_next_state(k, v, sums, state):
    c, dk = k.shape
    levels = c.bit_length() - 1
    k_tail = (k * jnp.exp2(sums[(levels + 1) * c:(levels + 2) * c, :])).astype(BF16)
    decay = _lane_bcast_cols(jnp.exp2(sums[c - 1:c, :]), dk)
    decay_full = jnp.concatenate([decay] * (v.shape[1] // V7X_LANES), axis=1)
    return decay_full * state + _dot_tn(k_tail, v)


def _gla_kernel(*refs, heads, dk, dv, dec_dims):
    q_ref, k_ref, v_ref, ga_ref, g_ref, nw_ref = refs[:6]
    dec_in = refs[6:15]
    o_ref, s_out_ref = refs[15:17]
    dec_out = refs[17:19]
    s_ref, mats_ref, lvl_ref = refs[19:22]
    dec_scratch = refs[22:]
    t = pl.program_id(1)
    step = pl.program_id(0) * pl.num_programs(1) + t
    last_step = pl.num_programs(0) * pl.num_programs(1) - 1

    @pl.when(t == 0)
    def _():
        s_ref[...] = jnp.zeros_like(s_ref)
        mats_ref[...] = _gla_sum_matrices(GLA_CHUNK)
        lvl_ref[...] = _pair_level(GLA_CHUNK)

    _ret_decode_section(step, last_step, dec_in + dec_out, dec_scratch,
                        heads=dec_dims[0], dk=dec_dims[1], dv=dec_dims[2])

    ct = q_ref.shape[0]

    kcs = [slice(hh * dk, (hh + 1) * dk) for hh in range(heads)]
    vcs = [slice(hh * dv, (hh + 1) * dv) for hh in range(heads)]
    group = 2 if (ct // GLA_CHUNK) % 2 == 0 else 1

    def chunk_group(cg, carry):
        pair_level = lvl_ref[...]
        rows, sums, qs, ks, lvl_scores = [], [], [], [], []
        for u in range(group):
            r = pl.ds(pl.multiple_of((cg * group + u) * GLA_CHUNK, GLA_CHUNK), GLA_CHUNK)
            rows.append(r)
            sums.append(_prefix_sum_rows(mats_ref[...], g_ref[r, :]))
            qs.append([q_ref[r, kc].astype(F32) * (dk ** -0.5) for kc in kcs])
            ks.append([k_ref[r, kc].astype(F32) for kc in kcs])
            lvl_scores.append([_gla_level_scores(qs[u][hh], ks[u][hh], sums[u][:, kcs[hh]]) for hh in range(heads)])
        outs = []
        for u in range(group):
            r = rows[u]
            outs.append([_gla_chunk_out(qs[u][hh], ks[u][hh], v_ref[r, vcs[hh]], sums[u][:, kcs[hh]],
                                        lvl_scores[u][hh], pair_level, s_ref[hh]) for hh in range(heads)])
            for hh in range(heads):
                s_ref[hh] = _gla_next_state(ks[u][hh], v_ref[r, vcs[hh]], sums[u][:, kcs[hh]], s_ref[hh])
        for u in range(group):
            for hh in range(heads):
                _rms_gate_store(outs[u][hh], nw_ref[...], ga_ref[rows[u], vcs[hh]].astype(F32), o_ref, rows[u],
                                vcs[hh])
        return carry

    lax.fori_loop(0, ct // GLA_CHUNK // group, chunk_group, 0)

    @pl.when(t == pl.num_programs(1) - 1)
    def _():
        s_out_ref[0] = s_ref[...]


def _gla_prompt_ret_decode(proj, log2_decay, gla_norm_w, proj_dec, cos_dec, sin_dec, log_gamma, ret_norm_w,
                           ret_state_dec, lay, batch, seq):
    heads, dk, dv = lay["gla_heads"], lay["gla_dk"], lay["gla_dv"]
    qk, vw = heads * dk, heads * dv
    r_heads, r_dk, r_dv = lay["ret_heads"], lay["ret_dk"], lay["ret_dv"]
    ct = min(seq, GLA_STEP_CHUNKS * GLA_CHUNK)
    levels = GLA_CHUNK.bit_length() - 1
    assert seq % ct == 0 and ct % GLA_CHUNK == 0
    nt = seq // ct
    n_dec = proj_dec.shape[0]
    tps = _decode_plan(n_dec, batch * nt)
    row = lambda b, t: b * nt + t
    dec_rows, dec_state, dec_o, dec_scratch = _decode_specs("ret", lay, n_dec, tps, row)
    table_dec = pl.BlockSpec((cos_dec.shape[0], r_dk // 2), lambda b, t: (0, 0))
    kern = functools.partial(_gla_kernel, heads=heads, dk=dk, dv=dv, dec_dims=(r_heads, r_dk, r_dv))
    return pl.pallas_call(
        kern,
        grid=(batch, nt),
        in_specs=[
            pl.BlockSpec((ct, qk), lambda b, t: (row(b, t), lay["qa"] // qk)),
            pl.BlockSpec((ct, qk), lambda b, t: (row(b, t), lay["ka"] // qk)),
            pl.BlockSpec((ct, vw), lambda b, t: (row(b, t), lay["va"] // vw)),
            pl.BlockSpec((ct, vw), lambda b, t: (row(b, t), lay["ga"] // vw)),
            pl.BlockSpec((ct, qk), lambda b, t: (row(b, t), 0)),
            pl.BlockSpec((1, dv), lambda b, t: (0, 0)),
        ] + dec_rows + [
            table_dec, table_dec,
            pl.BlockSpec((r_heads, 1, V7X_LANES), lambda b, t: (0, 0, 0)),
            pl.BlockSpec((1, r_dv), lambda b, t: (0, 0)),
            dec_state,
        ],
        out_specs=[
            pl.BlockSpec((ct, vw), lambda b, t: (row(b, t), 0)),
            pl.BlockSpec((1, heads, dk, dv), lambda b, t: (b, 0, 0, 0)),
            dec_o,
            dec_state,
        ],
        out_shape=[
            jax.ShapeDtypeStruct((batch * seq, vw), BF16),
            jax.ShapeDtypeStruct((batch, heads, dk, dv), F32),
            jax.ShapeDtypeStruct((n_dec, r_heads * r_dv), BF16),
            jax.ShapeDtypeStruct(ret_state_dec.shape, ret_state_dec.dtype),
        ],
        scratch_shapes=[
            pltpu.VMEM((heads, dk, dv), F32),
            pltpu.VMEM(((levels + 2) * GLA_CHUNK, 3 * GLA_CHUNK), BF16),
            pltpu.VMEM((GLA_CHUNK, GLA_CHUNK), jnp.int32),
        ] + dec_scratch,
        compiler_params=_params(2, claim_all=True),
        name="gla_prompt_ret_decode",
    )(proj, proj, proj, proj, log2_decay, gla_norm_w, proj_dec, proj_dec, proj_dec, proj_dec, cos_dec, sin_dec,
      log_gamma, ret_norm_w, ret_state_dec)


def _rotary(x, cos, sin):
    half = x.shape[1] // 2
    x1, x2 = x[:, :half], x[:, half:]
    return jnp.concatenate([x1 * cos - x2 * sin, x1 * sin + x2 * cos], axis=1)


def _ret_kernel(*refs, heads, dk, dv, c, dec_dims):
    q_ref, k_ref, v_ref, gb_ref, cos_ref, sin_ref, lg_ref, nw_ref = refs[:8]
    dec_in = refs[8:15]
    o_ref, s_out_ref = refs[15:17]
    dec_out = refs[17:19]
    s_ref, dmat_ref, qdec_ref, kdec_ref = refs[19:23]
    dec_scratch = refs[23:]
    t = pl.program_id(1)
    step = pl.program_id(0) * pl.num_programs(1) + t
    last_step = pl.num_programs(0) * pl.num_programs(1) - 1

    _gla_decode_section(step, last_step, dec_in + dec_out, dec_scratch,
                        heads=dec_dims[0], dk=dec_dims[1], dv=dec_dims[2])

    @pl.when(t == 0)
    def _():
        s_ref[...] = jnp.zeros_like(s_ref)
        ri = lax.broadcasted_iota(jnp.int32, (c, c), 0)
        rj = lax.broadcasted_iota(jnp.int32, (c, c), 1)
        dist = (ri - rj).astype(F32)
        rowl = lax.broadcasted_iota(jnp.int32, (c, V7X_LANES), 0).astype(F32)
        for hh in range(heads):
            lg = lg_ref[hh]
            dmat_ref[hh] = jnp.exp(jnp.where(ri >= rj, dist * lg[:, :1], -jnp.inf))
            qdec_ref[hh] = jnp.exp((rowl + 1.0) * lg)
            kdec_ref[hh] = jnp.exp((float(c - 1) - rowl) * lg)

    ct = q_ref.shape[0]
    kcs = [slice(hh * dk, (hh + 1) * dk) for hh in range(heads)]
    vcs = [slice(hh * dv, (hh + 1) * dv) for hh in range(heads)]

    def chunk(ci, carry):
        rows = pl.ds(pl.multiple_of(ci * c, c), c)
        cos, sin = cos_ref[rows, :], sin_ref[rows, :]
        qrs = [_rotary(q_ref[rows, kc].astype(F32), cos, sin).astype(BF16) for kc in kcs]
        krs = [_rotary(k_ref[rows, kc].astype(F32), cos, sin) * (dk ** -0.5) for kc in kcs]
        scores = [_dot_nt(qrs[hh], krs[hh].astype(BF16)) * dmat_ref[hh] for hh in range(heads)]
        outs = []
        for hh in range(heads):
            qdec = jnp.concatenate([qdec_ref[hh]] * (dv // V7X_LANES), axis=1)
            o = qdec * _dot(qrs[hh], s_ref[hh].astype(BF16))
            outs.append(o + _dot(scores[hh].astype(BF16), v_ref[rows, vcs[hh]]))
        for hh in range(heads):
            kdec = jnp.concatenate([kdec_ref[hh]] * (dk // V7X_LANES), axis=1)
            k_tail = (krs[hh] * kdec).astype(BF16)
            lgv = jnp.concatenate([lg_ref[hh]] * (dv // V7X_LANES), axis=1)
            s_ref[hh] = jnp.exp(float(c) * lgv) * s_ref[hh] + _dot_tn(k_tail, v_ref[rows, vcs[hh]])
        for hh in range(heads):
            _ln_gate_store(outs[hh], nw_ref[...], gb_ref[rows, vcs[hh]].astype(F32), o_ref, rows, vcs[hh])
        return carry

    lax.fori_loop(0, ct // c, chunk, 0)

    @pl.when(t == pl.num_programs(1) - 1)
    def _():
        s_out_ref[0] = s_ref[...]


def _ret_prompt_gla_decode(proj, cos, sin, log_gamma, ret_norm_w, proj_dec, log2_decay_dec, gla_norm_w,
                           gla_state_dec, lay, batch, seq):
    heads, dk, dv = lay["ret_heads"], lay["ret_dk"], lay["ret_dv"]
    qk, vw = heads * dk, heads * dv
    g_heads, g_dk, g_dv = lay["gla_heads"], lay["gla_dk"], lay["gla_dv"]
    c = min(seq, RET_CHUNK)
    ct = min(seq, RET_STEP_CHUNKS * c)
    assert seq % ct == 0 and ct % c == 0
    nt = seq // ct
    half = dk // 2
    n_dec = proj_dec.shape[0]
    tps = _decode_plan(n_dec, batch * nt)
    row = lambda b, t: b * nt + t
    dec_rows, dec_state, dec_o, dec_scratch = _decode_specs("gla", lay, n_dec, tps, row)
    kern = functools.partial(_ret_kernel, heads=heads, dk=dk, dv=dv, c=c, dec_dims=(g_heads, g_dk, g_dv))
    return pl.pallas_call(
        kern,
        grid=(batch, nt),
        in_specs=[
            pl.BlockSpec((ct, qk), lambda b, t: (row(b, t), lay["qb"] // qk)),
            pl.BlockSpec((ct, qk), lambda b, t: (row(b, t), lay["kb"] // qk)),
            pl.BlockSpec((ct, vw), lambda b, t: (row(b, t), lay["vb"] // vw)),
            pl.BlockSpec((ct, vw), lambda b, t: (row(b, t), lay["gb"] // vw)),
            pl.BlockSpec((ct, half), lambda b, t: (t, 0)),
            pl.BlockSpec((ct, half), lambda b, t: (t, 0)),
            pl.BlockSpec((heads, 1, V7X_LANES), lambda b, t: (0, 0, 0)),
            pl.BlockSpec((1, dv), lambda b, t: (0, 0)),
        ] + dec_rows + [
            pl.BlockSpec((n_dec, g_heads * g_dk), lambda b, t: (0, 0)),
            pl.BlockSpec((1, g_dv), lambda b, t: (0, 0)),
            dec_state,
        ],
        out_specs=[
            pl.BlockSpec((ct, vw), lambda b, t: (row(b, t), 0)),
            pl.BlockSpec((1, heads, dk, dv), lambda b, t: (b, 0, 0, 0)),
            dec_o,
            dec_state,
        ],
        out_shape=[
            jax.ShapeDtypeStruct((batch * seq, vw), BF16),
            jax.ShapeDtypeStruct((batch, heads, dk, dv), F32),
            jax.ShapeDtypeStruct((n_dec, g_heads * g_dv), BF16),
            jax.ShapeDtypeStruct(gla_state_dec.shape, gla_state_dec.dtype),
        ],
        scratch_shapes=[
            pltpu.VMEM((heads, dk, dv), F32),
            pltpu.VMEM((heads, c, c), F32),
            pltpu.VMEM((heads, c, V7X_LANES), F32),
            pltpu.VMEM((heads, c, V7X_LANES), F32),
        ] + dec_scratch,
        compiler_params=_params(2, claim_all=True),
        name="ret_prompt_gla_decode",
    )(proj, proj, proj, proj, cos, sin, log_gamma, ret_norm_w, proj_dec, proj_dec, proj_dec, proj_dec,
      log2_decay_dec, gla_norm_w, gla_state_dec)


def _merge_kernel(oap_ref, obp_ref, oas_ref, obs_ref, wa_ref, wb_ref, g0p_ref, g1p_ref, g0s_ref, g1s_ref, wnext_ref,
                  mp_ref, ms_ref, wnext_bf_ref, wa_bf, wb_bf):
    wnext_bf_ref[...] = wnext_ref[...].astype(BF16)

    def merged(oa, ob, g0, g1):
        ya = _dot(oa, wa_bf[...])
        yb = _dot(ob, wb_bf[...])
        return _sigmoid(g0.astype(F32)) * ya + _sigmoid(g1.astype(F32)) * yb

    @pl.when(pl.program_id(1) == 0)
    def _():
        wa_bf[...] = wa_ref[...].astype(BF16)
        wb_bf[...] = wb_ref[...].astype(BF16)
        ms_ref[...] = merged(oas_ref[...], obs_ref[...], g0s_ref[...], g1s_ref[...]).astype(ms_ref.dtype)

    mp_ref[...] = merged(oap_ref[...], obp_ref[...], g0p_ref[...], g1p_ref[...]).astype(mp_ref.dtype)


def _slab_specs(w_next, n_steps, step_of):
    kn, dn = w_next.shape
    assert kn % n_steps == 0 and (kn // n_steps) % 16 == 0, (kn, n_steps)
    slab = kn // n_steps
    spec = pl.BlockSpec((slab, dn), lambda j, i: (step_of(j, i), 0))
    return spec, spec, jax.ShapeDtypeStruct((kn, dn), BF16), _nbytes((slab, dn), F32) + _nbytes((slab, dn), BF16)


def _merge(oa_p, ob_p, oa_s, ob_s, wa, wb, proj_p, proj_s, lay, w_next):
    m_p, ka = oa_p.shape
    kb = ob_p.shape[1]
    tail = oa_s.shape[0]
    d = wa.shape[1]
    tm = _row_tile(m_p, ROW_TILE)
    tn = min(d, 1024)
    assert d % tn == 0 and lay["mg"] % tn == 0
    g0 = lay["mg"] // tn
    g1 = (lay["mg"] + d) // tn
    n_m = m_p // tm
    slab_in, slab_out, slab_shape, slab_bytes = _slab_specs(w_next, (d // tn) * n_m, lambda j, i: j * n_m + i)
    return pl.pallas_call(
        _merge_kernel,
        grid=(d // tn, n_m),
        in_specs=[
            pl.BlockSpec((tm, ka), lambda j, i: (i, 0)),
            pl.BlockSpec((tm, kb), lambda j, i: (i, 0)),
            pl.BlockSpec((tail, ka), lambda j, i: (0, 0)),
            pl.BlockSpec((tail, kb), lambda j, i: (0, 0)),
            pl.BlockSpec((ka, tn), lambda j, i: (0, j)),
            pl.BlockSpec((kb, tn), lambda j, i: (0, j)),
            pl.BlockSpec((tm, tn), lambda j, i: (i, g0 + j)),
            pl.BlockSpec((tm, tn), lambda j, i: (i, g1 + j)),
            pl.BlockSpec((tail, tn), lambda j, i: (0, g0 + j)),
            pl.BlockSpec((tail, tn), lambda j, i: (0, g1 + j)),
            slab_in,
        ],
        out_specs=[pl.BlockSpec((tm, tn), lambda j, i: (i, j)), pl.BlockSpec((tail, tn), lambda j, i: (0, j)),
                   slab_out],
        out_shape=[jax.ShapeDtypeStruct((m_p, d), BF16), jax.ShapeDtypeStruct((tail, d), BF16), slab_shape],
        scratch_shapes=[pltpu.VMEM((ka, tn), BF16), pltpu.VMEM((kb, tn), BF16)],
        compiler_params=_params(
            2, _nbytes((tm, ka), BF16), _nbytes((tm, kb), BF16), _nbytes((tail, ka), BF16), _nbytes((tail, kb), BF16),
            _nbytes((ka, tn), F32), _nbytes((kb, tn), F32), 3 * _nbytes((tm, tn), BF16), 3 * _nbytes((tail, tn), BF16),
            slab_bytes,
            scratch_bytes=_nbytes((ka, tn), BF16) + _nbytes((kb, tn), BF16) + 3 * _nbytes((tm, tn), F32)),
        name="merge",
    )(oa_p, ob_p, oa_s, ob_s, wa, wb, proj_p, proj_p, proj_s, proj_s, w_next)


def _proj_res_norm_kernel(*refs, emit_sum, group, n_blocks):
    ap_refs, as_refs, w_refs = refs[:group], refs[group:2 * group], refs[2 * group:3 * group]
    resp_ref, ress_ref, nw_ref = refs[3 * group:3 * group + 3]
    out_refs = refs[3 * group + 3:]
    n_out = 2 if emit_sum else 1
    outs_p, outs_s = out_refs[:n_out], out_refs[n_out:]
    i = pl.program_id(0)
    k = pl.program_id(1)
    last_k = k == pl.num_programs(1) - 1
    d = w_refs[0].shape[1]
    col_chunk = min(d, 512)
    rest = n_blocks % group

    def step(a_refs, res_ref, outs):
        acc_ref = outs[0]
        nrow = acc_ref.shape[0]
        row_chunk = min(nrow, 128)
        assert nrow % row_chunk == 0

        @pl.when(k == 0)
        def _():
            acc_ref[...] = res_ref[...]

        def accumulate(n_used):
            a = [a_refs[s][...] for s in range(n_used)]
            for c in range(d // col_chunk):
                cs = slice(c * col_chunk, (c + 1) * col_chunk)
                part = _dot(a[0], w_refs[0][:, cs])
                for s in range(1, n_used):
                    part = part + _dot(a[s], w_refs[s][:, cs])
                acc_ref[:, cs] += part

        if rest == 0:
            accumulate(group)
        else:
            pl.when(jnp.logical_not(last_k))(lambda: accumulate(group))
            pl.when(last_k)(lambda: accumulate(rest))

        @pl.when(last_k)
        def _():
            def body(c, carry):
                rr = pl.ds(pl.multiple_of(c * row_chunk, row_chunk), row_chunk)
                y = _rmsnorm_rows(acc_ref[rr, :], nw_ref[...])
                if emit_sum:
                    outs[1][rr, :] = y.astype(outs[1].dtype)
                else:
                    acc_ref[rr, :] = y
                return carry

            lax.fori_loop(0, nrow // row_chunk, body, 0)

    step(ap_refs, resp_ref, outs_p)

    @pl.when(i == pl.num_programs(0) - 1)
    def _():
        step(as_refs, ress_ref, outs_s)


def _proj_res_norm(a_p, a_s, w, res_p, res_s, norm_w, emit_sum, group):
    m_p, kdim = a_p.shape
    tail = a_s.shape[0]
    d = w.shape[1]
    tm = _row_tile(m_p, ROW_TILE)
    tk = min(kdim, K_BLOCK)
    assert kdim % tk == 0
    n_blocks = kdim // tk
    group = min(group, n_blocks)
    n_steps = pl.cdiv(n_blocks, group)
    blk = lambda k, s: jnp.minimum(k * group + s, n_blocks - 1)
    p_spec = pl.BlockSpec((tm, d), lambda i, k: (i, 0))
    s_spec = pl.BlockSpec((tail, d), lambda i, k: (0, 0))
    s3_spec = pl.BlockSpec((tail, None, d), lambda i, k: (0, 0, 0))
    res_s_spec = s3_spec if res_s.ndim == 3 else s_spec
    out_specs = [p_spec, s3_spec]
    out_shape = [jax.ShapeDtypeStruct((m_p, d), F32), jax.ShapeDtypeStruct((tail, 1, d), F32)]
    assert w.dtype == BF16
    blocks = [group * _nbytes((tm, tk), BF16), group * _nbytes((tail, tk), BF16), group * _nbytes((tk, d), BF16),
              2 * _nbytes((tm, d), F32), 2 * _nbytes((tail, d), F32)]
    if emit_sum:
        out_specs = [p_spec, p_spec, s_spec, s_spec]
        out_shape = [out_shape[0], jax.ShapeDtypeStruct((m_p, d), BF16),
                     jax.ShapeDtypeStruct((tail, d), F32), jax.ShapeDtypeStruct((tail, d), BF16)]
        blocks += [_nbytes((tm, d), BF16), _nbytes((tail, d), BF16)]
    in_specs = (
        [pl.BlockSpec((tm, tk), lambda i, k, s=s: (i, blk(k, s))) for s in range(group)]
        + [pl.BlockSpec((tail, tk), lambda i, k, s=s: (0, blk(k, s))) for s in range(group)]
        + [pl.BlockSpec((tk, d), lambda i, k, s=s: (blk(k, s), 0)) for s in range(group)]
        + [p_spec, res_s_spec, pl.BlockSpec((1, d), lambda i, k: (0, 0))])
    return pl.pallas_call(
        functools.partial(_proj_res_norm_kernel, emit_sum=emit_sum, group=group, n_blocks=n_blocks),
        grid=(m_p // tm, n_steps),
        in_specs=in_specs,
        out_specs=out_specs,
        out_shape=out_shape,
        compiler_params=_params(2, *blocks),
        name="proj_res_norm",
    )(*([a_p] * group + [a_s] * group + [w] * group), res_p, res_s, norm_w.reshape(1, d))


def _swiglu_kernel(hp_ref, hs_ref, wg_ref, wu_ref, wnext_ref, op_ref, os_ref, wnext_bf_ref, wg_bf, wu_bf):
    wnext_bf_ref[...] = wnext_ref[...].astype(BF16)

    tn = wg_bf.shape[1]
    col_chunk = min(tn, 256)

    def act(h_ref, o_ref):
        h = h_ref[...]
        for c in range(tn // col_chunk):
            cs = slice(c * col_chunk, (c + 1) * col_chunk)
            a = _dot(h, wg_bf[:, cs])
            b = _dot(h, wu_bf[:, cs])
            o_ref[:, cs] = (_silu(a) * b).astype(o_ref.dtype)

    @pl.when(pl.program_id(1) == 0)
    def _():
        wg_bf[...] = wg_ref[...].astype(BF16)
        wu_bf[...] = wu_ref[...].astype(BF16)
        act(hs_ref, os_ref)

    act(hp_ref, op_ref)


def _swiglu(h_p, h_s, wg, wu, w_next):
    m_p, d = h_p.shape
    tail = h_s.shape[0]
    f = wg.shape[1]
    tm = _row_tile(m_p, ROW_TILE_WIDE)
    tn = 512 if f % 512 == 0 else 256
    assert f % tn == 0
    n_m = m_p // tm
    slab_in, slab_out, slab_shape, slab_bytes = _slab_specs(w_next, (f // tn) * n_m, lambda j, i: j * n_m + i)
    return pl.pallas_call(
        _swiglu_kernel,
        grid=(f // tn, n_m),
        in_specs=[
            pl.BlockSpec((tm, d), lambda j, i: (i, 0)),
            pl.BlockSpec((tail, d), lambda j, i: (0, 0)),
            pl.BlockSpec((d, tn), lambda j, i: (0, j)),
            pl.BlockSpec((d, tn), lambda j, i: (0, j)),
            slab_in,
        ],
        out_specs=[pl.BlockSpec((tm, tn), lambda j, i: (i, j)), pl.BlockSpec((tail, tn), lambda j, i: (0, j)),
                   slab_out],
        out_shape=[jax.ShapeDtypeStruct((m_p, f), BF16), jax.ShapeDtypeStruct((tail, f), BF16), slab_shape],
        scratch_shapes=[pltpu.VMEM((d, tn), BF16), pltpu.VMEM((d, tn), BF16)],
        compiler_params=_params(
            2, _nbytes((tm, d), BF16), _nbytes((tail, d), BF16), 2 * _nbytes((d, tn), F32),
            _nbytes((tm, tn), BF16), _nbytes((tail, tn), BF16), slab_bytes,
            scratch_bytes=2 * _nbytes((d, tn), BF16) + 3 * _nbytes((tm, tn), F32)),
        name="swiglu",
    )(h_p, h_s, wg, wu, w_next)


def _layout(d_model, in_width, state_gla, state_ret, gate_rank):
    _, _, gh, gdk, gdv = state_gla.shape
    _, _, rh, rdk, rdv = state_ret.shape
    gqk, gv, rqk, rv = gh * gdk, gh * gdv, rh * rdk, rh * rdv
    lay = dict(gla_heads=gh, gla_dk=gdk, gla_dv=gdv, ret_heads=rh, ret_dk=rdk, ret_dv=rdv, rank=gate_rank)
    off = 0
    for name, width in (("qa", gqk), ("ka", gqk), ("va", gv), ("ga", gv), ("qb", rqk), ("kb", rqk),
                        ("vb", rv), ("gb", rv), ("mg", 2 * d_model)):
        lay[name] = off
        off += width
    lay["out_cols"] = off
    lay["plain_cols"] = 2 * gqk + gv
    lay["gd_src"] = lay["plain_cols"]
    assert lay["gd_src"] % V7X_LANES == 0 and gate_rank <= V7X_LANES
    assert in_width == off + gate_rank
    return lay


def _layer(x_p, x_s, st_gla, st_ret, wts, lay, log_gamma, final_norm):
    (norm_mix, w_in, w_gate_up, b_gate, gla_norm_w, w_gla_up, ret_norm_w, w_ret_up, w_out, norm_ffn,
     w_ffn_gate, w_ffn_up, w_ffn_down) = wts
    batch, seq, d = x_p.shape
    rank = lay["rank"]
    gqk = lay["gla_heads"] * lay["gla_dk"]
    bup = b_gate.reshape(1, gqk)
    gnw = gla_norm_w.reshape(1, -1)
    rnw = ret_norm_w.reshape(1, -1)
    tn = 1024 if (lay["out_cols"] % 1024 == 0 and lay["plain_cols"] % 1024 == 0) else 512
    xp = x_p.reshape(batch * seq, d)
    assert x_s.ndim == 3 and x_s.shape[1] == 1, "one new token per decode sequence"
    xs = x_s
    w_in_t = w_in.T

    h_p, h_s, g_p, g_s, cos_p, sin_p, cos_s, sin_s = _rmsnorm_gate(
        xp, xs, norm_mix, w_in_t, lay["gd_src"], w_gate_up, bup, seq, lay["ret_dk"] // 2)
    proj_p, proj_s = _in_proj(h_p, h_s, w_in_t, lay["plain_cols"], rank, lay["out_cols"], tn)
    oa_p, sa_p, ob_s, sb_s = _gla_prompt_ret_decode(
        proj_p, g_p, gnw, proj_s, cos_s, sin_s, log_gamma, rnw, st_ret, lay, batch, seq)
    ob_p, sb_p, oa_s, sa_s = _ret_prompt_gla_decode(
        proj_p, cos_p, sin_p, log_gamma, rnw, proj_s, g_s, gnw, st_gla, lay, batch, seq)
    m_p, m_s, w_out_bf = _merge(oa_p, ob_p, oa_s, ob_s, w_gla_up, w_ret_up, proj_p, proj_s, lay, w_out)
    x1_p, h2_p, x1_s, h2_s = _proj_res_norm(m_p, m_s, w_out_bf, xp, xs, norm_ffn, True, 2)
    act_p, act_s, w_down_bf = _swiglu(h2_p, h2_s, w_ffn_gate, w_ffn_up, w_ffn_down)
    y_p, y_s = _proj_res_norm(act_p, act_s, w_down_bf, x1_p, x1_s, final_norm, False, 3)
    return (y_p, sa_p, sb_p), (y_s, sa_s, sb_s)


def kernel(x_prompt, x_sample, state_gla, state_ret, norm_mix, w_in, w_gla_gate_up, b_gla_gate, gla_norm_w,
           w_gla_up, ret_norm_w, w_ret_up, w_out, norm_ffn, w_ffn_gate, w_ffn_up, w_ffn_down, norm_final):
    depth = w_in.shape[0]
    assert depth == 1, "single-layer trunk"
    batch, seq, d = x_prompt.shape
    lay = _layout(d, w_in.shape[-1], state_gla, state_ret, w_gla_gate_up.shape[1])
    rh, rdk = lay["ret_heads"], lay["ret_dk"]
    assert rdk // 2 == V7X_LANES
    lg = jnp.log1p(-jnp.exp(jnp.linspace(math.log(1.0 / 32), math.log(1.0 / 512), rh))).astype(F32)
    log_gamma = jnp.broadcast_to(lg[:, None, None], (rh, 1, V7X_LANES))

    wts = (norm_mix[0], w_in[0], w_gla_gate_up[0], b_gla_gate[0], gla_norm_w[0], w_gla_up[0], ret_norm_w[0],
           w_ret_up[0], w_out[0], norm_ffn[0], w_ffn_gate[0], w_ffn_up[0], w_ffn_down[0])
    (y_p, ga_p, re_p), (y_s, ga_s, re_s) = _layer(
        x_prompt, x_sample, state_gla[0], state_ret[0], wts, lay, log_gamma, norm_final)

    sd = state_gla.dtype
    return (y_p.reshape(batch, seq, d), y_s.reshape(x_sample.shape),
            ga_p[None].astype(sd), re_p[None].astype(state_ret.dtype),
            ga_s[None].astype(sd), re_s[None].astype(state_ret.dtype))
```

```python
import functools
import math

import numpy as np
import jax
import jax.numpy as jnp
from jax import lax
from jax.experimental import pallas as pl
from jax.experimental.pallas import tpu as pltpu

EPS = 1e-6
ROPE_BASE = 10000.0
GLA_GATE_NORM = 16.0
PAST_LEN = 16384

V7X_LANES = 128
V7X_VMEM_REQUEST_CAP = 60000 * 1024
COMPILER_SCRATCH_BYTES = 12 * 1024 * 1024

GLA_CHUNK = 64
GLA_STEP_CHUNKS = 8
LOG2_E = 1.4426950408889634
RET_CHUNK = 128
RET_STEP_CHUNKS = 4
ROW_TILE = 1024
ROW_TILE_WIDE = 2048
K_BLOCK = 512

BF16 = jnp.bfloat16
F32 = jnp.float32


def _params(n_axes, *block_bytes, scratch_bytes=0, claim_all=False):
    need = 2 * sum(block_bytes) + scratch_bytes + COMPILER_SCRATCH_BYTES
    if claim_all:
        need = V7X_VMEM_REQUEST_CAP
    return pltpu.CompilerParams(
        dimension_semantics=("arbitrary",) * n_axes,
        vmem_limit_bytes=int(min(V7X_VMEM_REQUEST_CAP, need)),
    )


def _nbytes(shape, dtype):
    return int(np.prod(shape)) * jnp.dtype(dtype).itemsize


def _sigmoid(x):
    return 1.0 / (1.0 + jnp.exp(-x))


def _silu(x):
    return x * _sigmoid(x)


def _log_sigmoid(x):
    return jnp.minimum(x, 0.0) - jnp.log(1.0 + jnp.exp(-jnp.abs(x)))


def _dot(a, b):
    return jnp.dot(a, b, preferred_element_type=F32)


def _dot_nt(a, b):
    return lax.dot_general(a, b, (((1,), (1,)), ((), ())), preferred_element_type=F32)


def _dot_tn(a, b):
    return lax.dot_general(a, b, (((0,), (0,)), ((), ())), preferred_element_type=F32)


def _row_tile(m, want):
    t = min(m, want)
    assert m % t == 0, (m, t)
    return t


def _rmsnorm_rows(x, w):
    ms = jnp.mean(x * x, axis=-1, keepdims=True)
    return x * lax.rsqrt(ms + EPS) * w


def _rope_rows(cos_ref, sin_ref, pos0):
    rows, half = cos_ref.shape
    pos = (lax.broadcasted_iota(jnp.int32, (rows, half), 0) + pos0).astype(F32)
    idx = lax.broadcasted_iota(jnp.int32, (rows, half), 1).astype(F32)
    ang = pos * jnp.exp(idx * (-math.log(ROPE_BASE) / half))
    cos_ref[...] = jnp.cos(ang)
    sin_ref[...] = jnp.sin(ang)


def _rmsnorm_kernel(xp_ref, xs_ref, w_ref, wgd_ref, wup_ref, bup_ref,
                    hp_ref, hs_ref, gp_ref, gs_ref, cosp_ref, sinp_ref, cosd_ref, sind_ref, wgd_bf, wup_bf):
    i = pl.program_id(0)

    @pl.when(i == 0)
    def _():
        wgd_bf[...] = wgd_ref[...].astype(BF16)
        wup_bf[...] = jnp.zeros_like(wup_bf)
        wup_bf[0:wup_ref.shape[0], :] = wup_ref[...].astype(BF16)

    def rows(x, h_ref, g_ref):
        h = _rmsnorm_rows(x, w_ref[...]).astype(h_ref.dtype)
        h_ref[...] = h
        gd = _dot_nt(h, wgd_bf[...])
        x = _dot(gd.astype(BF16), wup_bf[...]) + bup_ref[...]
        g_ref[...] = _log_sigmoid(x) * (LOG2_E / GLA_GATE_NORM)

    rows(xp_ref[...], hp_ref, gp_ref)
    _rope_rows(cosp_ref, sinp_ref, i * cosp_ref.shape[0])

    @pl.when(i == 0)
    def _():
        rows(xs_ref[...], hs_ref, gs_ref)
        _rope_rows(cosd_ref, sind_ref, PAST_LEN)


def _rmsnorm_gate(x_p, x_s, w, w_in_t, gate_row0, w_gate_up, bup, seq, half):
    m_p, d = x_p.shape
    tail = x_s.shape[0]
    rank, gw = w_gate_up.shape
    tm = _row_tile(m_p, ROW_TILE // 2)
    n_steps = m_p // tm
    assert gate_row0 % V7X_LANES == 0 and seq % n_steps == 0 and rank % 16 == 0
    pos_rows = seq // n_steps
    table = pl.BlockSpec((pos_rows, half), lambda i: (i, 0))
    table_dec = pl.BlockSpec((8, half), lambda i: (0, 0))
    return pl.pallas_call(
        _rmsnorm_kernel,
        grid=(n_steps,),
        in_specs=[
            pl.BlockSpec((tm, d), lambda i: (i, 0)),
            pl.BlockSpec((tail, None, d), lambda i: (0, 0, 0)),
            pl.BlockSpec((1, d), lambda i: (0, 0)),
            pl.BlockSpec((V7X_LANES, d), lambda i: (gate_row0 // V7X_LANES, 0)),
            pl.BlockSpec((rank, gw), lambda i: (0, 0)),
            pl.BlockSpec((1, gw), lambda i: (0, 0)),
        ],
        out_specs=[
            pl.BlockSpec((tm, d), lambda i: (i, 0)), pl.BlockSpec((tail, d), lambda i: (0, 0)),
            pl.BlockSpec((tm, gw), lambda i: (i, 0)), pl.BlockSpec((tail, gw), lambda i: (0, 0)),
            table, table, table_dec, table_dec,
        ],
        out_shape=[
            jax.ShapeDtypeStruct((m_p, d), BF16), jax.ShapeDtypeStruct((tail, d), BF16),
            jax.ShapeDtypeStruct((m_p, gw), F32), jax.ShapeDtypeStruct((tail, gw), F32),
            jax.ShapeDtypeStruct((seq, half), F32), jax.ShapeDtypeStruct((seq, half), F32),
            jax.ShapeDtypeStruct((8, half), F32), jax.ShapeDtypeStruct((8, half), F32),
        ],
        scratch_shapes=[pltpu.VMEM((V7X_LANES, d), BF16), pltpu.VMEM((V7X_LANES, gw), BF16)],
        compiler_params=_params(1, claim_all=True),
        name="rmsnorm_gate",
    )(x_p, x_s, w.reshape(1, d), w_in_t, w_gate_up, bup)


def _in_proj_kernel(hp_ref, hs_ref, wm_ref, wn_ref, op_ref, os_ref, wbf_ref, *, n_plain, shift):
    j = pl.program_id(0)
    i = pl.program_id(1)
    tn = wbf_ref.shape[0]

    @pl.when(jnp.logical_and(i == 0, j < n_plain))
    def _():
        wbf_ref[...] = wm_ref[...].astype(BF16)

    @pl.when(jnp.logical_and(i == 0, j >= n_plain))
    def _():
        wbf_ref[0:tn - shift, :] = wm_ref[shift:tn, :].astype(BF16)
        wbf_ref[tn - shift:tn, :] = wn_ref[...].astype(BF16)

    @pl.when(i == 0)
    def _():
        os_ref[...] = _dot_nt(hs_ref[...], wbf_ref[...]).astype(os_ref.dtype)

    op_ref[...] = _dot_nt(hp_ref[...], wbf_ref[...]).astype(op_ref.dtype)


def _in_proj(h_p, h_s, w_in_t, plain_cols, shift, out_cols, tn):
    m_p, d = h_p.shape
    tail = h_s.shape[0]
    tm = _row_tile(m_p, ROW_TILE_WIDE)
    assert plain_cols % tn == 0 and out_cols % tn == 0 and tn % shift == 0 and shift % 8 == 0
    n_plain = plain_cols // tn
    kern = functools.partial(_in_proj_kernel, n_plain=n_plain, shift=shift)
    return pl.pallas_call(
        kern,
        grid=(out_cols // tn, m_p // tm),
        in_specs=[
            pl.BlockSpec((tm, d), lambda j, i: (i, 0)),
            pl.BlockSpec((tail, d), lambda j, i: (0, 0)),
            pl.BlockSpec((tn, d), lambda j, i: (j, 0)),
            pl.BlockSpec((shift, d), lambda j, i: ((j + 1) * (tn // shift), 0)),
        ],
        out_specs=[pl.BlockSpec((tm, tn), lambda j, i: (i, j)), pl.BlockSpec((tail, tn), lambda j, i: (0, j))],
        out_shape=[jax.ShapeDtypeStruct((m_p, out_cols), BF16), jax.ShapeDtypeStruct((tail, out_cols), BF16)],
        scratch_shapes=[pltpu.VMEM((tn, d), BF16)],
        compiler_params=_params(
            2, _nbytes((tm, d), BF16), _nbytes((tail, d), BF16), _nbytes((tn, d), F32), _nbytes((shift, d), F32),
            _nbytes((tm, tn), BF16), _nbytes((tail, tn), BF16),
            scratch_bytes=_nbytes((tn, d), BF16) + _nbytes((tm, tn), F32)),
        name="in_proj",
    )(h_p, h_s, w_in_t, w_in_t)


def _prefix_sum_rows(sel3_bf16, g):
    g0 = g.astype(BF16)
    r1 = g - g0.astype(F32)
    g1 = r1.astype(BF16)
    g2 = (r1 - g1.astype(F32)).astype(BF16)
    return _dot(sel3_bf16, jnp.concatenate([g0, g1, g2], axis=0))


def _lane_bcast_cols(row, n):
    parts = []
    for c in range(n // V7X_LANES):
        tile = jnp.broadcast_to(row[:, c * V7X_LANES:(c + 1) * V7X_LANES], (V7X_LANES, V7X_LANES))
        parts.append(tile.T)
    return parts[0] if len(parts) == 1 else jnp.concatenate(parts, axis=0)


def _rms_gate_store(o, w, gate, out_ref, rows, cols):
    ms = jnp.mean(o * o, axis=-1, keepdims=True)
    y = o * lax.rsqrt(ms + EPS) * w
    out_ref[rows, cols] = (y * _silu(gate)).astype(out_ref.dtype)


def _ln_gate_store(o, w, gate, out_ref, rows, cols):
    mu = jnp.mean(o, axis=-1, keepdims=True)
    dlt = o - mu
    var = jnp.mean(dlt * dlt, axis=-1, keepdims=True)
    y = dlt * lax.rsqrt(var + EPS) * w
    out_ref[rows, cols] = (y * _silu(gate)).astype(out_ref.dtype)


def _token_selectors(n_tok):
    assert 3 * n_tok <= V7X_LANES
    j = lax.broadcasted_iota(jnp.int32, (V7X_LANES, V7X_LANES), 0)
    sel = []
    for t in range(n_tok):
        hit = jnp.logical_or(j == t, jnp.logical_or(j == n_tok + t, j == 2 * n_tok + t))
        sel.append(jnp.where(hit, 1.0, 0.0).astype(BF16))
    return jnp.stack(sel, axis=0)


def _column_source(x):
    n_tok, w = x.shape
    hi = x.astype(BF16).astype(F32)
    r1 = x - hi
    mid = r1.astype(BF16).astype(F32)
    lo = (r1 - mid).astype(BF16).astype(F32)
    x3 = jnp.concatenate([hi, mid, lo, jnp.zeros((V7X_LANES - 3 * n_tok, w), F32)], axis=0)
    parts = [x3[:, c * V7X_LANES:(c + 1) * V7X_LANES].T for c in range(w // V7X_LANES)]
    return (parts[0] if len(parts) == 1 else jnp.concatenate(parts, axis=0)).astype(BF16)


def _decode_advance(tok0, decay_rows_fn, decay_const_fn, k_ref, q_ref, v_ref, s_in_ref, s_out_ref, o_ref, sel_ref,
                    *, heads, dk, dv):
    n_tok = s_in_ref.shape[0]
    reps = dv // V7X_LANES
    rows = pl.ds(pl.multiple_of(tok0, n_tok), n_tok)

    def cols(src, tt):
        return jnp.concatenate([_dot(src, sel_ref[tt])] * reps, axis=1)

    for hh in range(heads):
        kc = slice(hh * dk, (hh + 1) * dk)
        vc = slice(hh * dv, (hh + 1) * dv)
        k_src, q_src = _column_source(k_ref[rows, kc]), _column_source(q_ref[rows, kc])
        a_src = None if decay_rows_fn is None else _column_source(decay_rows_fn(rows, hh))
        v = v_ref[rows, vc]
        o_rows = []
        for tt in range(n_tok):
            decay = decay_const_fn(hh) if a_src is None else cols(a_src, tt)
            s_new = decay * s_in_ref[tt, hh] + cols(k_src, tt) * v[tt:tt + 1, :]
            s_out_ref[tt, hh] = s_new
            o_rows.append(jnp.sum(cols(q_src, tt) * s_new, axis=0, keepdims=True))
        o_ref[rows, vc] = jnp.concatenate(o_rows, axis=0)


def _decode_plan(n_dec, n_steps):
    assert n_dec % n_steps == 0 and (n_dec // n_steps) % 8 == 0, (n_dec, n_steps)
    return n_dec // n_steps


def _gla_decode_section(step, last_step, refs, scratch, *, heads, dk, dv):
    qd_ref, kd_ref, vd_ref, gad_ref, gd_ref, nw_ref, sd_in_ref, od_ref, sd_out_ref = refs
    a_dec, q_dec, k_dec, v_dec, o_dec, sel_ref = scratch

    @pl.when(step == 0)
    def _():
        a_dec[...] = jnp.exp2(gd_ref[...])
        q_dec[...] = qd_ref[...].astype(F32) * (dk ** -0.5)
        k_dec[...] = kd_ref[...].astype(F32)
        v_dec[...] = vd_ref[...].astype(F32)
        sel_ref[...] = _token_selectors(sd_in_ref.shape[0])

    _decode_advance(step * sd_in_ref.shape[0], lambda rows, hh: a_dec[rows, hh * dk:(hh + 1) * dk], None,
                    k_dec, q_dec, v_dec, sd_in_ref, sd_out_ref, o_dec, sel_ref, heads=heads, dk=dk, dv=dv)

    @pl.when(step == last_step)
    def _():
        n_dec = o_dec.shape[0]
        for hh in range(heads):
            vc = slice(hh * dv, (hh + 1) * dv)
            _rms_gate_store(o_dec[:, vc], nw_ref[...], gad_ref[:, vc].astype(F32), od_ref, slice(0, n_dec), vc)


def _ret_decode_section(step, last_step, refs, scratch, *, heads, dk, dv):
    qd_ref, kd_ref, vd_ref, gbd_ref, cosd_ref, sind_ref, lg_ref, nw_ref, sd_in_ref, od_ref, sd_out_ref = refs
    q_dec, k_dec, v_dec, o_dec, sel_ref = scratch

    @pl.when(step == 0)
    def _():
        cosd, sind = cosd_ref[0:1, :], sind_ref[0:1, :]
        for hh in range(heads):
            kc = slice(hh * dk, (hh + 1) * dk)
            q_dec[:, kc] = _rotary(qd_ref[:, kc].astype(F32), cosd, sind)
            k_dec[:, kc] = _rotary(kd_ref[:, kc].astype(F32), cosd, sind) * (dk ** -0.5)
        v_dec[...] = vd_ref[...].astype(F32)
        sel_ref[...] = _token_selectors(sd_in_ref.shape[0])

    def gamma(hh):
        return jnp.exp(jnp.concatenate([lg_ref[hh]] * (dv // V7X_LANES), axis=1))

    _decode_advance(step * sd_in_ref.shape[0], None, gamma, k_dec, q_dec, v_dec, sd_in_ref, sd_out_ref, o_dec,
                    sel_ref, heads=heads, dk=dk, dv=dv)

    @pl.when(step == last_step)
    def _():
        n_dec = o_dec.shape[0]
        for hh in range(heads):
            vc = slice(hh * dv, (hh + 1) * dv)
            _ln_gate_store(o_dec[:, vc], nw_ref[...], gbd_ref[:, vc].astype(F32), od_ref, slice(0, n_dec), vc)


def _decode_specs(kind, lay, n_dec, tps, row):
    heads, dk, dv = lay[kind + "_heads"], lay[kind + "_dk"], lay[kind + "_dv"]
    qk, vw = heads * dk, heads * dv
    names = ("qa", "ka", "va", "ga") if kind == "gla" else ("qb", "kb", "vb", "gb")
    widths = (qk, qk, vw, vw)
    rows_in = [pl.BlockSpec((n_dec, w), lambda b, t, c=lay[n] // w: (0, c)) for n, w in zip(names, widths)]
    state = pl.BlockSpec((tps, heads, dk, dv), lambda b, t: (row(b, t), 0, 0, 0))
    o_spec = pl.BlockSpec((n_dec, vw), lambda b, t: (0, 0))
    n_qk = 3 if kind == "gla" else 2
    scratch = ([pltpu.VMEM((n_dec, qk), F32)] * n_qk + [pltpu.VMEM((n_dec, vw), F32)] * 2
               + [pltpu.VMEM((tps, V7X_LANES, V7X_LANES), BF16)])
    return rows_in, state, o_spec, scratch


def _gla_sum_matrices(c):
    levels = c.bit_length() - 1
    assert 1 << levels == c
    i = lax.broadcasted_iota(jnp.int32, (c, c), 0)
    j = lax.broadcasted_iota(jnp.int32, (c, c), 1)
    mats = [j <= i]
    for l in range(levels):
        ref = jnp.bitwise_or(jnp.bitwise_and(i, -(2 << l)), 1 << l)
        mats.append(jnp.logical_and(j > jnp.minimum(i, ref), j <= jnp.maximum(i, ref)))
    mats.append(j > i)
    sel = jnp.concatenate([jnp.where(m, 1.0, 0.0).astype(BF16) for m in mats], axis=0)
    return jnp.concatenate([sel, sel, sel], axis=1)


def _pair_level(c):
    levels = c.bit_length() - 1
    i = lax.broadcasted_iota(jnp.int32, (c, c), 0)
    j = lax.broadcasted_iota(jnp.int32, (c, c), 1)
    x = jnp.bitwise_xor(i, j)
    lvl = jnp.zeros((c, c), jnp.int32)
    for l in range(1, levels):
        lvl = lvl + jnp.where(x >= (1 << l), 1, 0)
    return jnp.where(i > j, lvl, jnp.where(i == j, levels, -1))


def _queries_else_keys(q, k, l):
    c = q.shape[0]
    span = 1 << l
    if span >= 8:
        parts = [(q if (b & 1) else k)[b * span:(b + 1) * span, :] for b in range(c // span)]
        return jnp.concatenate(parts, axis=0)
    row = lax.broadcasted_iota(jnp.int32, q.shape, 0)
    return jnp.where(jnp.bitwise_and(row, span) != 0, q, k)


def _gla_level_scores(q, k, sums):
    c = q.shape[0]
    levels = c.bit_length() - 1
    out = []
    for l in range(levels):
        x = _queries_else_keys(q, k, l) * jnp.exp2(sums[(1 + l) * c:(2 + l) * c, :])
        xb = x.astype(BF16)
        out.append(_dot_nt(xb, xb))
    return out


def _gla_chunk_out(q, k, v, sums, level_scores, pair_level, state):
    c = q.shape[0]
    levels = c.bit_length() - 1
    scores = jnp.where(pair_level == levels, jnp.sum(q * k, axis=-1, keepdims=True), 0.0)
    for l in range(levels):
        scores = jnp.where(pair_level == l, level_scores[l], scores)
    o = _dot((q * jnp.exp2(sums[0:c, :])).astype(BF16), state.astype(BF16))
    return o + _dot(scores.astype(BF16), v)


def _gla_next_state(k, v, sums, state):
    c, dk = k.shape
    levels = c.bit_length() - 1
    k_tail = (k * jnp.exp2(sums[(levels + 1) * c:(levels + 2) * c, :])).astype(BF16)
    decay = _lane_bcast_cols(jnp.exp2(sums[c - 1:c, :]), dk)
    decay_full = jnp.concatenate([decay] * (v.shape[1] // V7X_LANES), axis=1)
    return decay_full * state + _dot_tn(k_tail, v)


def _gla_kernel(*refs, heads, dk, dv, dec_dims):
    q_ref, k_ref, v_ref, ga_ref, g_ref, nw_ref = refs[:6]
    dec_in = refs[6:15]
    o_ref, s_out_ref = refs[15:17]
    dec_out = refs[17:19]
    s_ref, mats_ref, lvl_ref = refs[19:22]
    dec_scratch = refs[22:]
    t = pl.program_id(1)
    step = pl.program_id(0) * pl.num_programs(1) + t
    last_step = pl.num_programs(0) * pl.num_programs(1) - 1

    @pl.when(t == 0)
    def _():
        s_ref[...] = jnp.zeros_like(s_ref)
        mats_ref[...] = _gla_sum_matrices(GLA_CHUNK)
        lvl_ref[...] = _pair_level(GLA_CHUNK)

    _ret_decode_section(step, last_step, dec_in + dec_out, dec_scratch,
                        heads=dec_dims[0], dk=dec_dims[1], dv=dec_dims[2])

    ct = q_ref.shape[0]

    kcs = [slice(hh * dk, (hh + 1) * dk) for hh in range(heads)]
    vcs = [slice(hh * dv, (hh + 1) * dv) for hh in range(heads)]
    group = 2 if (ct // GLA_CHUNK) % 2 == 0 else 1

    def chunk_group(cg, carry):
        pair_level = lvl_ref[...]
        rows, sums, qs, ks, lvl_scores = [], [], [], [], []
        for u in range(group):
            r = pl.ds(pl.multiple_of((cg * group + u) * GLA_CHUNK, GLA_CHUNK), GLA_CHUNK)
            rows.append(r)
            sums.append(_prefix_sum_rows(mats_ref[...], g_ref[r, :]))
            qs.append([q_ref[r, kc].astype(F32) * (dk ** -0.5) for kc in kcs])
            ks.append([k_ref[r, kc].astype(F32) for kc in kcs])
            lvl_scores.append([_gla_level_scores(qs[u][hh], ks[u][hh], sums[u][:, kcs[hh]]) for hh in range(heads)])
        outs = []
        for u in range(group):
            r = rows[u]
            outs.append([_gla_chunk_out(qs[u][hh], ks[u][hh], v_ref[r, vcs[hh]], sums[u][:, kcs[hh]],
                                        lvl_scores[u][hh], pair_level, s_ref[hh]) for hh in range(heads)])
            for hh in range(heads):
                s_ref[hh] = _gla_next_state(ks[u][hh], v_ref[r, vcs[hh]], sums[u][:, kcs[hh]], s_ref[hh])
        for u in range(group):
            for hh in range(heads):
                _rms_gate_store(outs[u][hh], nw_ref[...], ga_ref[rows[u], vcs[hh]].astype(F32), o_ref, rows[u],
                                vcs[hh])
        return carry

    lax.fori_loop(0, ct // GLA_CHUNK // group, chunk_group, 0)

    @pl.when(t == pl.num_programs(1) - 1)
    def _():
        s_out_ref[0] = s_ref[...]


def _gla_prompt_ret_decode(proj, log2_decay, gla_norm_w, proj_dec, cos_dec, sin_dec, log_gamma, ret_norm_w,
                           ret_state_dec, lay, batch, seq):
    heads, dk, dv = lay["gla_heads"], lay["gla_dk"], lay["gla_dv"]
    qk, vw = heads * dk, heads * dv
    r_heads, r_dk, r_dv = lay["ret_heads"], lay["ret_dk"], lay["ret_dv"]
    ct = min(seq, GLA_STEP_CHUNKS * GLA_CHUNK)
    levels = GLA_CHUNK.bit_length() - 1
    assert seq % ct == 0 and ct % GLA_CHUNK == 0
    nt = seq // ct
    n_dec = proj_dec.shape[0]
    tps = _decode_plan(n_dec, batch * nt)
    row = lambda b, t: b * nt + t
    dec_rows, dec_state, dec_o, dec_scratch = _decode_specs("ret", lay, n_dec, tps, row)
    table_dec = pl.BlockSpec((cos_dec.shape[0], r_dk // 2), lambda b, t: (0, 0))
    kern = functools.partial(_gla_kernel, heads=heads, dk=dk, dv=dv, dec_dims=(r_heads, r_dk, r_dv))
    return pl.pallas_call(
        kern,
        grid=(batch, nt),
        in_specs=[
            pl.BlockSpec((ct, qk), lambda b, t: (row(b, t), lay["qa"] // qk)),
            pl.BlockSpec((ct, qk), lambda b, t: (row(b, t), lay["ka"] // qk)),
            pl.BlockSpec((ct, vw), lambda b, t: (row(b, t), lay["va"] // vw)),
            pl.BlockSpec((ct, vw), lambda b, t: (row(b, t), lay["ga"] // vw)),
            pl.BlockSpec((ct, qk), lambda b, t: (row(b, t), 0)),
            pl.BlockSpec((1, dv), lambda b, t: (0, 0)),
        ] + dec_rows + [
            table_dec, table_dec,
            pl.BlockSpec((r_heads, 1, V7X_LANES), lambda b, t: (0, 0, 0)),
            pl.BlockSpec((1, r_dv), lambda b, t: (0, 0)),
            dec_state,
        ],
        out_specs=[
            pl.BlockSpec((ct, vw), lambda b, t: (row(b, t), 0)),
            pl.BlockSpec((1, heads, dk, dv), lambda b, t: (b, 0, 0, 0)),
            dec_o,
            dec_state,
        ],
        out_shape=[
            jax.ShapeDtypeStruct((batch * seq, vw), BF16),
            jax.ShapeDtypeStruct((batch, heads, dk, dv), F32),
            jax.ShapeDtypeStruct((n_dec, r_heads * r_dv), BF16),
            jax.ShapeDtypeStruct(ret_state_dec.shape, ret_state_dec.dtype),
        ],
        scratch_shapes=[
            pltpu.VMEM((heads, dk, dv), F32),
            pltpu.VMEM(((levels + 2) * GLA_CHUNK, 3 * GLA_CHUNK), BF16),
            pltpu.VMEM((GLA_CHUNK, GLA_CHUNK), jnp.int32),
        ] + dec_scratch,
        compiler_params=_params(2, claim_all=True),
        name="gla_prompt_ret_decode",
    )(proj, proj, proj, proj, log2_decay, gla_norm_w, proj_dec, proj_dec, proj_dec, proj_dec, cos_dec, sin_dec,
      log_gamma, ret_norm_w, ret_state_dec)


def _rotary(x, cos, sin):
    half = x.shape[1] // 2
    x1, x2 = x[:, :half], x[:, half:]
    return jnp.concatenate([x1 * cos - x2 * sin, x1 * sin + x2 * cos], axis=1)


def _ret_kernel(*refs, heads, dk, dv, c, dec_dims):
    q_ref, k_ref, v_ref, gb_ref, cos_ref, sin_ref, lg_ref, nw_ref = refs[:8]
    dec_in = refs[8:15]
    o_ref, s_out_ref = refs[15:17]
    dec_out = refs[17:19]
    s_ref, dmat_ref, qdec_ref, kdec_ref = refs[19:23]
    dec_scratch = refs[23:]
    t = pl.program_id(1)
    step = pl.program_id(0) * pl.num_programs(1) + t
    last_step = pl.num_programs(0) * pl.num_programs(1) - 1

    _gla_decode_section(step, last_step, dec_in + dec_out, dec_scratch,
                        heads=dec_dims[0], dk=dec_dims[1], dv=dec_dims[2])

    @pl.when(t == 0)
    def _():
        s_ref[...] = jnp.zeros_like(s_ref)
        ri = lax.broadcasted_iota(jnp.int32, (c, c), 0)
        rj = lax.broadcasted_iota(jnp.int32, (c, c), 1)
        dist = (ri - rj).astype(F32)
        rowl = lax.broadcasted_iota(jnp.int32, (c, V7X_LANES), 0).astype(F32)
        for hh in range(heads):
            lg = lg_ref[hh]
            dmat_ref[hh] = jnp.exp(jnp.where(ri >= rj, dist * lg[:, :1], -jnp.inf))
            qdec_ref[hh] = jnp.exp((rowl + 1.0) * lg)
            kdec_ref[hh] = jnp.exp((float(c - 1) - rowl) * lg)

    ct = q_ref.shape[0]
    kcs = [slice(hh * dk, (hh + 1) * dk) for hh in range(heads)]
    vcs = [slice(hh * dv, (hh + 1) * dv) for hh in range(heads)]

    def chunk(ci, carry):
        rows = pl.ds(pl.multiple_of(ci * c, c), c)
        cos, sin = cos_ref[rows, :], sin_ref[rows, :]
        qrs = [_rotary(q_ref[rows, kc].astype(F32), cos, sin).astype(BF16) for kc in kcs]
        krs = [_rotary(k_ref[rows, kc].astype(F32), cos, sin) * (dk ** -0.5) for kc in kcs]
        scores = [_dot_nt(qrs[hh], krs[hh].astype(BF16)) * dmat_ref[hh] for hh in range(heads)]
        outs = []
        for hh in range(heads):
            qdec = jnp.concatenate([qdec_ref[hh]] * (dv // V7X_LANES), axis=1)
            o = qdec * _dot(qrs[hh], s_ref[hh].astype(BF16))
            outs.append(o + _dot(scores[hh].astype(BF16), v_ref[rows, vcs[hh]]))
        for hh in range(heads):
            kdec = jnp.concatenate([kdec_ref[hh]] * (dk // V7X_LANES), axis=1)
            k_tail = (krs[hh] * kdec).astype(BF16)
            lgv = jnp.concatenate([lg_ref[hh]] * (dv // V7X_LANES), axis=1)
            s_ref[hh] = jnp.exp(float(c) * lgv) * s_ref[hh] + _dot_tn(k_tail, v_ref[rows, vcs[hh]])
        for hh in range(heads):
            _ln_gate_store(outs[hh], nw_ref[...], gb_ref[rows, vcs[hh]].astype(F32), o_ref, rows, vcs[hh])
        return carry

    lax.fori_loop(0, ct // c, chunk, 0)

    @pl.when(t == pl.num_programs(1) - 1)
    def _():
        s_out_ref[0] = s_ref[...]


def _ret_prompt_gla_decode(proj, cos, sin, log_gamma, ret_norm_w, proj_dec, log2_decay_dec, gla_norm_w,
                           gla_state_dec, lay, batch, seq):
    heads, dk, dv = lay["ret_heads"], lay["ret_dk"], lay["ret_dv"]
    qk, vw = heads * dk, heads * dv
    g_heads, g_dk, g_dv = lay["gla_heads"], lay["gla_dk"], lay["gla_dv"]
    c = min(seq, RET_CHUNK)
    ct = min(seq, RET_STEP_CHUNKS * c)
    assert seq % ct == 0 and ct % c == 0
    nt = seq // ct
    half = dk // 2
    n_dec = proj_dec.shape[0]
    tps = _decode_plan(n_dec, batch * nt)
    row = lambda b, t: b * nt + t
    dec_rows, dec_state, dec_o, dec_scratch = _decode_specs("gla", lay, n_dec, tps, row)
    kern = functools.partial(_ret_kernel, heads=heads, dk=dk, dv=dv, c=c, dec_dims=(g_heads, g_dk, g_dv))
    return pl.pallas_call(
        kern,
        grid=(batch, nt),
        in_specs=[
            pl.BlockSpec((ct, qk), lambda b, t: (row(b, t), lay["qb"] // qk)),
            pl.BlockSpec((ct, qk), lambda b, t: (row(b, t), lay["kb"] // qk)),
            pl.BlockSpec((ct, vw), lambda b, t: (row(b, t), lay["vb"] // vw)),
            pl.BlockSpec((ct, vw), lambda b, t: (row(b, t), lay["gb"] // vw)),
            pl.BlockSpec((ct, half), lambda b, t: (t, 0)),
            pl.BlockSpec((ct, half), lambda b, t: (t, 0)),
            pl.BlockSpec((heads, 1, V7X_LANES), lambda b, t: (0, 0, 0)),
            pl.BlockSpec((1, dv), lambda b, t: (0, 0)),
        ] + dec_rows + [
            pl.BlockSpec((n_dec, g_heads * g_dk), lambda b, t: (0, 0)),
            pl.BlockSpec((1, g_dv), lambda b, t: (0, 0)),
            dec_state,
        ],
        out_specs=[
            pl.BlockSpec((ct, vw), lambda b, t: (row(b, t), 0)),
            pl.BlockSpec((1, heads, dk, dv), lambda b, t: (b, 0, 0, 0)),
            dec_o,
            dec_state,
        ],
        out_shape=[
            jax.ShapeDtypeStruct((batch * seq, vw), BF16),
            jax.ShapeDtypeStruct((batch, heads, dk, dv), F32),
            jax.ShapeDtypeStruct((n_dec, g_heads * g_dv), BF16),
            jax.ShapeDtypeStruct(gla_state_dec.shape, gla_state_dec.dtype),
        ],
        scratch_shapes=[
            pltpu.VMEM((heads, dk, dv), F32),
            pltpu.VMEM((heads, c, c), F32),
            pltpu.VMEM((heads, c, V7X_LANES), F32),
            pltpu.VMEM((heads, c, V7X_LANES), F32),
        ] + dec_scratch,
        compiler_params=_params(2, claim_all=True),
        name="ret_prompt_gla_decode",
    )(proj, proj, proj, proj, cos, sin, log_gamma, ret_norm_w, proj_dec, proj_dec, proj_dec, proj_dec,
      log2_decay_dec, gla_norm_w, gla_state_dec)


def _merge_kernel(oap_ref, obp_ref, oas_ref, obs_ref, wa_ref, wb_ref, g0p_ref, g1p_ref, g0s_ref, g1s_ref, wnext_ref,
                  mp_ref, ms_ref, wnext_bf_ref, wa_bf, wb_bf):
    wnext_bf_ref[...] = wnext_ref[...].astype(BF16)

    def merged(oa, ob, g0, g1):
        ya = _dot(oa, wa_bf[...])
        yb = _dot(ob, wb_bf[...])
        return _sigmoid(g0.astype(F32)) * ya + _sigmoid(g1.astype(F32)) * yb

    @pl.when(pl.program_id(1) == 0)
    def _():
        wa_bf[...] = wa_ref[...].astype(BF16)
        wb_bf[...] = wb_ref[...].astype(BF16)
        ms_ref[...] = merged(oas_ref[...], obs_ref[...], g0s_ref[...], g1s_ref[...]).astype(ms_ref.dtype)

    mp_ref[...] = merged(oap_ref[...], obp_ref[...], g0p_ref[...], g1p_ref[...]).astype(mp_ref.dtype)


def _slab_specs(w_next, n_steps, step_of):
    kn, dn = w_next.shape
    assert kn % n_steps == 0 and (kn // n_steps) % 16 == 0, (kn, n_steps)
    slab = kn // n_steps
    spec = pl.BlockSpec((slab, dn), lambda j, i: (step_of(j, i), 0))
    return spec, spec, jax.ShapeDtypeStruct((kn, dn), BF16), _nbytes((slab, dn), F32) + _nbytes((slab, dn), BF16)


def _merge(oa_p, ob_p, oa_s, ob_s, wa, wb, proj_p, proj_s, lay, w_next):
    m_p, ka = oa_p.shape
    kb = ob_p.shape[1]
    tail = oa_s.shape[0]
    d = wa.shape[1]
    tm = _row_tile(m_p, ROW_TILE)
    tn = min(d, 1024)
    assert d % tn == 0 and lay["mg"] % tn == 0
    g0 = lay["mg"] // tn
    g1 = (lay["mg"] + d) // tn
    n_m = m_p // tm
    slab_in, slab_out, slab_shape, slab_bytes = _slab_specs(w_next, (d // tn) * n_m, lambda j, i: j * n_m + i)
    return pl.pallas_call(
        _merge_kernel,
        grid=(d // tn, n_m),
        in_specs=[
            pl.BlockSpec((tm, ka), lambda j, i: (i, 0)),
            pl.BlockSpec((tm, kb), lambda j, i: (i, 0)),
            pl.BlockSpec((tail, ka), lambda j, i: (0, 0)),
            pl.BlockSpec((tail, kb), lambda j, i: (0, 0)),
            pl.BlockSpec((ka, tn), lambda j, i: (0, j)),
            pl.BlockSpec((kb, tn), lambda j, i: (0, j)),
            pl.BlockSpec((tm, tn), lambda j, i: (i, g0 + j)),
            pl.BlockSpec((tm, tn), lambda j, i: (i, g1 + j)),
            pl.BlockSpec((tail, tn), lambda j, i: (0, g0 + j)),
            pl.BlockSpec((tail, tn), lambda j, i: (0, g1 + j)),
            slab_in,
        ],
        out_specs=[pl.BlockSpec((tm, tn), lambda j, i: (i, j)), pl.BlockSpec((tail, tn), lambda j, i: (0, j)),
                   slab_out],
        out_shape=[jax.ShapeDtypeStruct((m_p, d), BF16), jax.ShapeDtypeStruct((tail, d), BF16), slab_shape],
        scratch_shapes=[pltpu.VMEM((ka, tn), BF16), pltpu.VMEM((kb, tn), BF16)],
        compiler_params=_params(
            2, _nbytes((tm, ka), BF16), _nbytes((tm, kb), BF16), _nbytes((tail, ka), BF16), _nbytes((tail, kb), BF16),
            _nbytes((ka, tn), F32), _nbytes((kb, tn), F32), 3 * _nbytes((tm, tn), BF16), 3 * _nbytes((tail, tn), BF16),
            slab_bytes,
            scratch_bytes=_nbytes((ka, tn), BF16) + _nbytes((kb, tn), BF16) + 3 * _nbytes((tm, tn), F32)),
        name="merge",
    )(oa_p, ob_p, oa_s, ob_s, wa, wb, proj_p, proj_p, proj_s, proj_s, w_next)


def _proj_res_norm_kernel(*refs, emit_sum, group, n_blocks):
    ap_refs, as_refs, w_refs = refs[:group], refs[group:2 * group], refs[2 * group:3 * group]
    resp_ref, ress_ref, nw_ref = refs[3 * group:3 * group + 3]
    out_refs = refs[3 * group + 3:]
    n_out = 2 if emit_sum else 1
    outs_p, outs_s = out_refs[:n_out], out_refs[n_out:]
    i = pl.program_id(0)
    k = pl.program_id(1)
    last_k = k == pl.num_programs(1) - 1
    d = w_refs[0].shape[1]
    col_chunk = min(d, 512)
    rest = n_blocks % group

    def step(a_refs, res_ref, outs):
        acc_ref = outs[0]
        nrow = acc_ref.shape[0]
        row_chunk = min(nrow, 128)
        assert nrow % row_chunk == 0

        @pl.when(k == 0)
        def _():
            acc_ref[...] = res_ref[...]

        def accumulate(n_used):
            a = [a_refs[s][...] for s in range(n_used)]
            for c in range(d // col_chunk):
                cs = slice(c * col_chunk, (c + 1) * col_chunk)
                part = _dot(a[0], w_refs[0][:, cs])
                for s in range(1, n_used):
                    part = part + _dot(a[s], w_refs[s][:, cs])
                acc_ref[:, cs] += part

        if rest == 0:
            accumulate(group)
        else:
            pl.when(jnp.logical_not(last_k))(lambda: accumulate(group))
            pl.when(last_k)(lambda: accumulate(rest))

        @pl.when(last_k)
        def _():
            def body(c, carry):
                rr = pl.ds(pl.multiple_of(c * row_chunk, row_chunk), row_chunk)
                y = _rmsnorm_rows(acc_ref[rr, :], nw_ref[...])
                if emit_sum:
                    outs[1][rr, :] = y.astype(outs[1].dtype)
                else:
                    acc_ref[rr, :] = y
                return carry

            lax.fori_loop(0, nrow // row_chunk, body, 0)

    step(ap_refs, resp_ref, outs_p)

    @pl.when(i == pl.num_programs(0) - 1)
    def _():
        step(as_refs, ress_ref, outs_s)


def _proj_res_norm(a_p, a_s, w, res_p, res_s, norm_w, emit_sum, group):
    m_p, kdim = a_p.shape
    tail = a_s.shape[0]
    d = w.shape[1]
    tm = _row_tile(m_p, ROW_TILE)
    tk = min(kdim, K_BLOCK)
    assert kdim % tk == 0
    n_blocks = kdim // tk
    group = min(group, n_blocks)
    n_steps = pl.cdiv(n_blocks, group)
    blk = lambda k, s: jnp.minimum(k * group + s, n_blocks - 1)
    p_spec = pl.BlockSpec((tm, d), lambda i, k: (i, 0))
    s_spec = pl.BlockSpec((tail, d), lambda i, k: (0, 0))
    s3_spec = pl.BlockSpec((tail, None, d), lambda i, k: (0, 0, 0))
    res_s_spec = s3_spec if res_s.ndim == 3 else s_spec
    out_specs = [p_spec, s3_spec]
    out_shape = [jax.ShapeDtypeStruct((m_p, d), F32), jax.ShapeDtypeStruct((tail, 1, d), F32)]
    assert w.dtype == BF16
    blocks = [group * _nbytes((tm, tk), BF16), group * _nbytes((tail, tk), BF16), group * _nbytes((tk, d), BF16),
              2 * _nbytes((tm, d), F32), 2 * _nbytes((tail, d), F32)]
    if emit_sum:
        out_specs = [p_spec, p_spec, s_spec, s_spec]
        out_shape = [out_shape[0], jax.ShapeDtypeStruct((m_p, d), BF16),
                     jax.ShapeDtypeStruct((tail, d), F32), jax.ShapeDtypeStruct((tail, d), BF16)]
        blocks += [_nbytes((tm, d), BF16), _nbytes((tail, d), BF16)]
    in_specs = (
        [pl.BlockSpec((tm, tk), lambda i, k, s=s: (i, blk(k, s))) for s in range(group)]
        + [pl.BlockSpec((tail, tk), lambda i, k, s=s: (0, blk(k, s))) for s in range(group)]
        + [pl.BlockSpec((tk, d), lambda i, k, s=s: (blk(k, s), 0)) for s in range(group)]
        + [p_spec, res_s_spec, pl.BlockSpec((1, d), lambda i, k: (0, 0))])
    return pl.pallas_call(
        functools.partial(_proj_res_norm_kernel, emit_sum=emit_sum, group=group, n_blocks=n_blocks),
        grid=(m_p // tm, n_steps),
        in_specs=in_specs,
        out_specs=out_specs,
        out_shape=out_shape,
        compiler_params=_params(2, *blocks),
        name="proj_res_norm",
    )(*([a_p] * group + [a_s] * group + [w] * group), res_p, res_s, norm_w.reshape(1, d))


def _swiglu_kernel(hp_ref, hs_ref, wg_ref, wu_ref, wnext_ref, op_ref, os_ref, wnext_bf_ref, wg_bf, wu_bf):
    wnext_bf_ref[...] = wnext_ref[...].astype(BF16)

    tn = wg_bf.shape[1]
    col_chunk = min(tn, 256)

    def act(h_ref, o_ref):
        h = h_ref[...]
        for c in range(tn // col_chunk):
            cs = slice(c * col_chunk, (c + 1) * col_chunk)
            a = _dot(h, wg_bf[:, cs])
            b = _dot(h, wu_bf[:, cs])
            o_ref[:, cs] = (_silu(a) * b).astype(o_ref.dtype)

    @pl.when(pl.program_id(1) == 0)
    def _():
        wg_bf[...] = wg_ref[...].astype(BF16)
        wu_bf[...] = wu_ref[...].astype(BF16)
        act(hs_ref, os_ref)

    act(hp_ref, op_ref)


def _swiglu(h_p, h_s, wg, wu, w_next):
    m_p, d = h_p.shape
    tail = h_s.shape[0]
    f = wg.shape[1]
    tm = _row_tile(m_p, ROW_TILE_WIDE)
    tn = 512 if f % 512 == 0 else 256
    assert f % tn == 0
    n_m = m_p // tm
    slab_in, slab_out, slab_shape, slab_bytes = _slab_specs(w_next, (f // tn) * n_m, lambda j, i: j * n_m + i)
    return pl.pallas_call(
        _swiglu_kernel,
        grid=(f // tn, n_m),
        in_specs=[
            pl.BlockSpec((tm, d), lambda j, i: (i, 0)),
            pl.BlockSpec((tail, d), lambda j, i: (0, 0)),
            pl.BlockSpec((d, tn), lambda j, i: (0, j)),
            pl.BlockSpec((d, tn), lambda j, i: (0, j)),
            slab_in,
        ],
        out_specs=[pl.BlockSpec((tm, tn), lambda j, i: (i, j)), pl.BlockSpec((tail, tn), lambda j, i: (0, j)),
                   slab_out],
        out_shape=[jax.ShapeDtypeStruct((m_p, f), BF16), jax.ShapeDtypeStruct((tail, f), BF16), slab_shape],
        scratch_shapes=[pltpu.VMEM((d, tn), BF16), pltpu.VMEM((d, tn), BF16)],
        compiler_params=_params(
            2, _nbytes((tm, d), BF16), _nbytes((tail, d), BF16), 2 * _nbytes((d, tn), F32),
            _nbytes((tm, tn), BF16), _nbytes((tail, tn), BF16), slab_bytes,
            scratch_bytes=2 * _nbytes((d, tn), BF16) + 3 * _nbytes((tm, tn), F32)),
        name="swiglu",
    )(h_p, h_s, wg, wu, w_next)


def _layout(d_model, in_width, state_gla, state_ret, gate_rank):
    _, _, gh, gdk, gdv = state_gla.shape
    _, _, rh, rdk, rdv = state_ret.shape
    gqk, gv, rqk, rv = gh * gdk, gh * gdv, rh * rdk, rh * rdv
    lay = dict(gla_heads=gh, gla_dk=gdk, gla_dv=gdv, ret_heads=rh, ret_dk=rdk, ret_dv=rdv, rank=gate_rank)
    off = 0
    for name, width in (("qa", gqk), ("ka", gqk), ("va", gv), ("ga", gv), ("qb", rqk), ("kb", rqk),
                        ("vb", rv), ("gb", rv), ("mg", 2 * d_model)):
        lay[name] = off
        off += width
    lay["out_cols"] = off
    lay["plain_cols"] = 2 * gqk + gv
    lay["gd_src"] = lay["plain_cols"]
    assert lay["gd_src"] % V7X_LANES == 0 and gate_rank <= V7X_LANES
    assert in_width == off + gate_rank
    return lay


def _layer(x_p, x_s, st_gla, st_ret, wts, lay, log_gamma, final_norm):
    (norm_mix, w_in, w_gate_up, b_gate, gla_norm_w, w_gla_up, ret_norm_w, w_ret_up, w_out, norm_ffn,
     w_ffn_gate, w_ffn_up, w_ffn_down) = wts
    batch, seq, d = x_p.shape
    rank = lay["rank"]
    gqk = lay["gla_heads"] * lay["gla_dk"]
    bup = b_gate.reshape(1, gqk)
    gnw = gla_norm_w.reshape(1, -1)
    rnw = ret_norm_w.reshape(1, -1)
    tn = 1024 if (lay["out_cols"] % 1024 == 0 and lay["plain_cols"] % 1024 == 0) else 512
    xp = x_p.reshape(batch * seq, d)
    assert x_s.ndim == 3 and x_s.shape[1] == 1, "one new token per decode sequence"
    xs = x_s
    w_in_t = w_in.T

    h_p, h_s, g_p, g_s, cos_p, sin_p, cos_s, sin_s = _rmsnorm_gate(
        xp, xs, norm_mix, w_in_t, lay["gd_src"], w_gate_up, bup, seq, lay["ret_dk"] // 2)
    proj_p, proj_s = _in_proj(h_p, h_s, w_in_t, lay["plain_cols"], rank, lay["out_cols"], tn)
    oa_p, sa_p, ob_s, sb_s = _gla_prompt_ret_decode(
        proj_p, g_p, gnw, proj_s, cos_s, sin_s, log_gamma, rnw, st_ret, lay, batch, seq)
    ob_p, sb_p, oa_s, sa_s = _ret_prompt_gla_decode(
        proj_p, cos_p, sin_p, log_gamma, rnw, proj_s, g_s, gnw, st_gla, lay, batch, seq)
    m_p, m_s, w_out_bf = _merge(oa_p, ob_p, oa_s, ob_s, w_gla_up, w_ret_up, proj_p, proj_s, lay, w_out)
    x1_p, h2_p, x1_s, h2_s = _proj_res_norm(m_p, m_s, w_out_bf, xp, xs, norm_ffn, True, 2)
    act_p, act_s, w_down_bf = _swiglu(h2_p, h2_s, w_ffn_gate, w_ffn_up, w_ffn_down)
    y_p, y_s = _proj_res_norm(act_p, act_s, w_down_bf, x1_p, x1_s, final_norm, False, 3)
    return (y_p, sa_p, sb_p), (y_s, sa_s, sb_s)


def kernel(x_prompt, x_sample, state_gla, state_ret, norm_mix, w_in, w_gla_gate_up, b_gla_gate, gla_norm_w,
           w_gla_up, ret_norm_w, w_ret_up, w_out, norm_ffn, w_ffn_gate, w_ffn_up, w_ffn_down, norm_final):
    depth = w_in.shape[0]
    assert depth == 1, "single-layer trunk"
    batch, seq, d = x_prompt.shape
    lay = _layout(d, w_in.shape[-1], state_gla, state_ret, w_gla_gate_up.shape[1])
    rh, rdk = lay["ret_heads"], lay["ret_dk"]
    assert rdk // 2 == V7X_LANES
    lg = jnp.log1p(-jnp.exp(jnp.linspace(math.log(1.0 / 32), math.log(1.0 / 512), rh))).astype(F32)
    log_gamma = jnp.broadcast_to(lg[:, None, None], (rh, 1, V7X_LANES))

    wts = (norm_mix[0], w_in[0], w_gla_gate_up[0], b_gla_gate[0], gla_norm_w[0], w_gla_up[0], ret_norm_w[0],
           w_ret_up[0], w_out[0], norm_ffn[0], w_ffn_gate[0], w_ffn_up[0], w_ffn_down[0])
    (y_p, ga_p, re_p), (y_s, ga_s, re_s) = _layer(
        x_prompt, x_sample, state_gla[0], state_ret[0], wts, lay, log_gamma, norm_final)

    sd = state_gla.dtype
    return (y_p.reshape(batch, seq, d), y_s.reshape(x_sample.shape),
            ga_p[None].astype(sd), re_p[None].astype(state_ret.dtype),
            ga_s[None].astype(sd), re_s[None].astype(state_ret.dtype))
```

```python
import functools
import math

import numpy as np
import jax
import jax.numpy as jnp
from jax import lax
from jax.experimental import pallas as pl
from jax.experimental.pallas import tpu as pltpu

EPS = 1e-6
ROPE_BASE = 10000.0
GLA_GATE_NORM = 16.0
PAST_LEN = 16384

V7X_LANES = 128
V7X_VMEM_REQUEST_CAP = 60000 * 1024
COMPILER_SCRATCH_BYTES = 12 * 1024 * 1024

GLA_CHUNK = 64
GLA_STEP_CHUNKS = 8
LOG2_E = 1.4426950408889634
RET_CHUNK = 128
RET_STEP_CHUNKS = 4
ROW_TILE = 1024
ROW_TILE_WIDE = 2048
K_BLOCK = 512

BF16 = jnp.bfloat16
F32 = jnp.float32


def _params(n_axes, *block_bytes, scratch_bytes=0, claim_all=False):
    need = 2 * sum(block_bytes) + scratch_bytes + COMPILER_SCRATCH_BYTES
    if claim_all:
        need = V7X_VMEM_REQUEST_CAP
    return pltpu.CompilerParams(
        dimension_semantics=("arbitrary",) * n_axes,
        vmem_limit_bytes=int(min(V7X_VMEM_REQUEST_CAP, need)),
    )


def _nbytes(shape, dtype):
    return int(np.prod(shape)) * jnp.dtype(dtype).itemsize


def _sigmoid(x):
    return 1.0 / (1.0 + jnp.exp(-x))


def _silu(x):
    return x * _sigmoid(x)


def _log_sigmoid(x):
    return jnp.minimum(x, 0.0) - jnp.log(1.0 + jnp.exp(-jnp.abs(x)))


def _dot(a, b):
    return jnp.dot(a, b, preferred_element_type=F32)


def _dot_nt(a, b):
    return lax.dot_general(a, b, (((1,), (1,)), ((), ())), preferred_element_type=F32)


def _dot_tn(a, b):
    return lax.dot_general(a, b, (((0,), (0,)), ((), ())), preferred_element_type=F32)


def _row_tile(m, want):
    t = min(m, want)
    assert m % t == 0, (m, t)
    return t


def _rmsnorm_rows(x, w):
    ms = jnp.mean(x * x, axis=-1, keepdims=True)
    return x * lax.rsqrt(ms + EPS) * w


def _rope_rows(cos_ref, sin_ref, pos0):
    rows, half = cos_ref.shape
    pos = (lax.broadcasted_iota(jnp.int32, (rows, half), 0) + pos0).astype(F32)
    idx = lax.broadcasted_iota(jnp.int32, (rows, half), 1).astype(F32)
    ang = pos * jnp.exp(idx * (-math.log(ROPE_BASE) / half))
    cos_ref[...] = jnp.cos(ang)
    sin_ref[...] = jnp.sin(ang)


def _rmsnorm_kernel(xp_ref, xs_ref, w_ref, wgd_ref, wup_ref, bup_ref,
                    hp_ref, hs_ref, gp_ref, gs_ref, cosp_ref, sinp_ref, cosd_ref, sind_ref, wgd_bf, wup_bf):
    i = pl.program_id(0)

    @pl.when(i == 0)
    def _():
        wgd_bf[...] = wgd_ref[...].astype(BF16)
        wup_bf[...] = jnp.zeros_like(wup_bf)
        wup_bf[0:wup_ref.shape[0], :] = wup_ref[...].astype(BF16)

    def rows(x, h_ref, g_ref):
        h = _rmsnorm_rows(x, w_ref[...]).astype(h_ref.dtype)
        h_ref[...] = h
        gd = _dot_nt(h, wgd_bf[...])
        x = _dot(gd.astype(BF16), wup_bf[...]) + bup_ref[...]
        g_ref[...] = _log_sigmoid(x) * (LOG2_E / GLA_GATE_NORM)

    rows(xp_ref[...], hp_ref, gp_ref)
    _rope_rows(cosp_ref, sinp_ref, i * cosp_ref.shape[0])

    @pl.when(i == 0)
    def _():
        rows(xs_ref[...], hs_ref, gs_ref)
        _rope_rows(cosd_ref, sind_ref, PAST_LEN)


def _rmsnorm_gate(x_p, x_s, w, w_in_t, gate_row0, w_gate_up, bup, seq, half):
    m_p, d = x_p.shape
    tail = x_s.shape[0]
    rank, gw = w_gate_up.shape
    tm = _row_tile(m_p, ROW_TILE)
    n_steps = m_p // tm
    assert gate_row0 % V7X_LANES == 0 and seq % n_steps == 0 and rank % 16 == 0
    pos_rows = seq // n_steps
    table = pl.BlockSpec((pos_rows, half), lambda i: (i, 0))
    table_dec = pl.BlockSpec((8, half), lambda i: (0, 0))
    return pl.pallas_call(
        _rmsnorm_kernel,
        grid=(n_steps,),
        in_specs=[
            pl.BlockSpec((tm, d), lambda i: (i, 0)),
            pl.BlockSpec((tail, None, d), lambda i: (0, 0, 0)),
            pl.BlockSpec((1, d), lambda i: (0, 0)),
            pl.BlockSpec((V7X_LANES, d), lambda i: (gate_row0 // V7X_LANES, 0)),
            pl.BlockSpec((rank, gw), lambda i: (0, 0)),
            pl.BlockSpec((1, gw), lambda i: (0, 0)),
        ],
        out_specs=[
            pl.BlockSpec((tm, d), lambda i: (i, 0)), pl.BlockSpec((tail, d), lambda i: (0, 0)),
            pl.BlockSpec((tm, gw), lambda i: (i, 0)), pl.BlockSpec((tail, gw), lambda i: (0, 0)),
            table, table, table_dec, table_dec,
        ],
        out_shape=[
            jax.ShapeDtypeStruct((m_p, d), BF16), jax.ShapeDtypeStruct((tail, d), BF16),
            jax.ShapeDtypeStruct((m_p, gw), F32), jax.ShapeDtypeStruct((tail, gw), F32),
            jax.ShapeDtypeStruct((seq, half), F32), jax.ShapeDtypeStruct((seq, half), F32),
            jax.ShapeDtypeStruct((8, half), F32), jax.ShapeDtypeStruct((8, half), F32),
        ],
        scratch_shapes=[pltpu.VMEM((V7X_LANES, d), BF16), pltpu.VMEM((V7X_LANES, gw), BF16)],
        compiler_params=_params(1, claim_all=True),
        name="rmsnorm_gate",
    )(x_p, x_s, w.reshape(1, d), w_in_t, w_gate_up, bup)


def _in_proj_kernel(hp_ref, hs_ref, wm_ref, wn_ref, op_ref, os_ref, wbf_ref, *, n_plain, shift):
    j = pl.program_id(0)
    i = pl.program_id(1)
    tn = wbf_ref.shape[0]

    @pl.when(jnp.logical_and(i == 0, j < n_plain))
    def _():
        wbf_ref[...] = wm_ref[...].astype(BF16)

    @pl.when(jnp.logical_and(i == 0, j >= n_plain))
    def _():
        wbf_ref[0:tn - shift, :] = wm_ref[shift:tn, :].astype(BF16)
        wbf_ref[tn - shift:tn, :] = wn_ref[...].astype(BF16)

    @pl.when(i == 0)
    def _():
        os_ref[...] = _dot_nt(hs_ref[...], wbf_ref[...]).astype(os_ref.dtype)

    op_ref[...] = _dot_nt(hp_ref[...], wbf_ref[...]).astype(op_ref.dtype)


def _in_proj(h_p, h_s, w_in_t, plain_cols, shift, out_cols, tn):
    m_p, d = h_p.shape
    tail = h_s.shape[0]
    tm = _row_tile(m_p, ROW_TILE_WIDE)
    assert plain_cols % tn == 0 and out_cols % tn == 0 and tn % shift == 0 and shift % 8 == 0
    n_plain = plain_cols // tn
    kern = functools.partial(_in_proj_kernel, n_plain=n_plain, shift=shift)
    return pl.pallas_call(
        kern,
        grid=(out_cols // tn, m_p // tm),
        in_specs=[
            pl.BlockSpec((tm, d), lambda j, i: (i, 0)),
            pl.BlockSpec((tail, d), lambda j, i: (0, 0)),
            pl.BlockSpec((tn, d), lambda j, i: (j, 0)),
            pl.BlockSpec((shift, d), lambda j, i: ((j + 1) * (tn // shift), 0)),
        ],
        out_specs=[pl.BlockSpec((tm, tn), lambda j, i: (i, j)), pl.BlockSpec((tail, tn), lambda j, i: (0, j))],
        out_shape=[jax.ShapeDtypeStruct((m_p, out_cols), BF16), jax.ShapeDtypeStruct((tail, out_cols), BF16)],
        scratch_shapes=[pltpu.VMEM((tn, d), BF16)],
        compiler_params=_params(
            2, _nbytes((tm, d), BF16), _nbytes((tail, d), BF16), _nbytes((tn, d), F32), _nbytes((shift, d), F32),
            _nbytes((tm, tn), BF16), _nbytes((tail, tn), BF16),
            scratch_bytes=_nbytes((tn, d), BF16) + _nbytes((tm, tn), F32)),
        name="in_proj",
    )(h_p, h_s, w_in_t, w_in_t)


def _prefix_sum_rows(sel3_bf16, g):
    g0 = g.astype(BF16)
    r1 = g - g0.astype(F32)
    g1 = r1.astype(BF16)
    g2 = (r1 - g1.astype(F32)).astype(BF16)
    return _dot(sel3_bf16, jnp.concatenate([g0, g1, g2], axis=0))


def _lane_bcast_cols(row, n):
    parts = []
    for c in range(n // V7X_LANES):
        tile = jnp.broadcast_to(row[:, c * V7X_LANES:(c + 1) * V7X_LANES], (V7X_LANES, V7X_LANES))
        parts.append(tile.T)
    return parts[0] if len(parts) == 1 else jnp.concatenate(parts, axis=0)


def _rms_gate_store(o, w, gate, out_ref, rows, cols):
    ms = jnp.mean(o * o, axis=-1, keepdims=True)
    y = o * lax.rsqrt(ms + EPS) * w
    out_ref[rows, cols] = (y * _silu(gate)).astype(out_ref.dtype)


def _ln_gate_store(o, w, gate, out_ref, rows, cols):
    mu = jnp.mean(o, axis=-1, keepdims=True)
    dlt = o - mu
    var = jnp.mean(dlt * dlt, axis=-1, keepdims=True)
    y = dlt * lax.rsqrt(var + EPS) * w
    out_ref[rows, cols] = (y * _silu(gate)).astype(out_ref.dtype)


def _token_selectors(n_tok):
    assert 3 * n_tok <= V7X_LANES
    j = lax.broadcasted_iota(jnp.int32, (V7X_LANES, V7X_LANES), 0)
    sel = []
    for t in range(n_tok):
        hit = jnp.logical_or(j == t, jnp.logical_or(j == n_tok + t, j == 2 * n_tok + t))
        sel.append(jnp.where(hit, 1.0, 0.0).astype(BF16))
    return jnp.stack(sel, axis=0)


def _column_source(x):
    n_tok, w = x.shape
    hi = x.astype(BF16).astype(F32)
    r1 = x - hi
    mid = r1.astype(BF16).astype(F32)
    lo = (r1 - mid).astype(BF16).astype(F32)
    x3 = jnp.concatenate([hi, mid, lo, jnp.zeros((V7X_LANES - 3 * n_tok, w), F32)], axis=0)
    parts = [x3[:, c * V7X_LANES:(c + 1) * V7X_LANES].T for c in range(w // V7X_LANES)]
    return (parts[0] if len(parts) == 1 else jnp.concatenate(parts, axis=0)).astype(BF16)


def _decode_advance(tok0, decay_rows_fn, decay_const_fn, k_ref, q_ref, v_ref, s_in_ref, s_out_ref, o_ref, sel_ref,
                    *, heads, dk, dv):
    n_tok = s_in_ref.shape[0]
    reps = dv // V7X_LANES
    rows = pl.ds(pl.multiple_of(tok0, n_tok), n_tok)

    def cols(src, tt):
        return jnp.concatenate([_dot(src, sel_ref[tt])] * reps, axis=1)

    for hh in range(heads):
        kc = slice(hh * dk, (hh + 1) * dk)
        vc = slice(hh * dv, (hh + 1) * dv)
        k_src, q_src = _column_source(k_ref[rows, kc]), _column_source(q_ref[rows, kc])
        a_src = None if decay_rows_fn is None else _column_source(decay_rows_fn(rows, hh))
        v = v_ref[rows, vc]
        o_rows = []
        for tt in range(n_tok):
            decay = decay_const_fn(hh) if a_src is None else cols(a_src, tt)
            s_new = decay * s_in_ref[tt, hh] + cols(k_src, tt) * v[tt:tt + 1, :]
            s_out_ref[tt, hh] = s_new
            o_rows.append(jnp.sum(cols(q_src, tt) * s_new, axis=0, keepdims=True))
        o_ref[rows, vc] = jnp.concatenate(o_rows, axis=0)


def _decode_plan(n_dec, n_steps):
    assert n_dec % n_steps == 0 and (n_dec // n_steps) % 8 == 0, (n_dec, n_steps)
    return n_dec // n_steps


def _gla_decode_section(step, last_step, refs, scratch, *, heads, dk, dv):
    qd_ref, kd_ref, vd_ref, gad_ref, gd_ref, nw_ref, sd_in_ref, od_ref, sd_out_ref = refs
    a_dec, q_dec, k_dec, v_dec, o_dec, sel_ref = scratch

    @pl.when(step == 0)
    def _():
        a_dec[...] = jnp.exp2(gd_ref[...])
        q_dec[...] = qd_ref[...].astype(F32) * (dk ** -0.5)
        k_dec[...] = kd_ref[...].astype(F32)
        v_dec[...] = vd_ref[...].astype(F32)
        sel_ref[...] = _token_selectors(sd_in_ref.shape[0])

    _decode_advance(step * sd_in_ref.shape[0], lambda rows, hh: a_dec[rows, hh * dk:(hh + 1) * dk], None,
                    k_dec, q_dec, v_dec, sd_in_ref, sd_out_ref, o_dec, sel_ref, heads=heads, dk=dk, dv=dv)

    @pl.when(step == last_step)
    def _():
        n_dec = o_dec.shape[0]
        for hh in range(heads):
            vc = slice(hh * dv, (hh + 1) * dv)
            _rms_gate_store(o_dec[:, vc], nw_ref[...], gad_ref[:, vc].astype(F32), od_ref, slice(0, n_dec), vc)


def _ret_decode_section(step, last_step, refs, scratch, *, heads, dk, dv):
    qd_ref, kd_ref, vd_ref, gbd_ref, cosd_ref, sind_ref, lg_ref, nw_ref, sd_in_ref, od_ref, sd_out_ref = refs
    q_dec, k_dec, v_dec, o_dec, sel_ref = scratch

    @pl.when(step == 0)
    def _():
        cosd, sind = cosd_ref[0:1, :], sind_ref[0:1, :]
        for hh in range(heads):
            kc = slice(hh * dk, (hh + 1) * dk)
            q_dec[:, kc] = _rotary(qd_ref[:, kc].astype(F32), cosd, sind)
            k_dec[:, kc] = _rotary(kd_ref[:, kc].astype(F32), cosd, sind) * (dk ** -0.5)
        v_dec[...] = vd_ref[...].astype(F32)
        sel_ref[...] = _token_selectors(sd_in_ref.shape[0])

    def gamma(hh):
        return jnp.exp(jnp.concatenate([lg_ref[hh]] * (dv // V7X_LANES), axis=1))

    _decode_advance(step * sd_in_ref.shape[0], None, gamma, k_dec, q_dec, v_dec, sd_in_ref, sd_out_ref, o_dec,
                    sel_ref, heads=heads, dk=dk, dv=dv)

    @pl.when(step == last_step)
    def _():
        n_dec = o_dec.shape[0]
        for hh in range(heads):
            vc = slice(hh * dv, (hh + 1) * dv)
            _ln_gate_store(o_dec[:, vc], nw_ref[...], gbd_ref[:, vc].astype(F32), od_ref, slice(0, n_dec), vc)


def _decode_specs(kind, lay, n_dec, tps, row):
    heads, dk, dv = lay[kind + "_heads"], lay[kind + "_dk"], lay[kind + "_dv"]
    qk, vw = heads * dk, heads * dv
    names = ("qa", "ka", "va", "ga") if kind == "gla" else ("qb", "kb", "vb", "gb")
    widths = (qk, qk, vw, vw)
    rows_in = [pl.BlockSpec((n_dec, w), lambda b, t, c=lay[n] // w: (0, c)) for n, w in zip(names, widths)]
    state = pl.BlockSpec((tps, heads, dk, dv), lambda b, t: (row(b, t), 0, 0, 0))
    o_spec = pl.BlockSpec((n_dec, vw), lambda b, t: (0, 0))
    n_qk = 3 if kind == "gla" else 2
    scratch = ([pltpu.VMEM((n_dec, qk), F32)] * n_qk + [pltpu.VMEM((n_dec, vw), F32)] * 2
               + [pltpu.VMEM((tps, V7X_LANES, V7X_LANES), BF16)])
    return rows_in, state, o_spec, scratch


def _gla_sum_matrices(c):
    levels = c.bit_length() - 1
    assert 1 << levels == c
    i = lax.broadcasted_iota(jnp.int32, (c, c), 0)
    j = lax.broadcasted_iota(jnp.int32, (c, c), 1)
    mats = [j <= i]
    for l in range(levels):
        ref = jnp.bitwise_or(jnp.bitwise_and(i, -(2 << l)), 1 << l)
        mats.append(jnp.logical_and(j > jnp.minimum(i, ref), j <= jnp.maximum(i, ref)))
    mats.append(j > i)
    sel = jnp.concatenate([jnp.where(m, 1.0, 0.0).astype(BF16) for m in mats], axis=0)
    return jnp.concatenate([sel, sel, sel], axis=1)


def _pair_level(c):
    levels = c.bit_length() - 1
    i = lax.broadcasted_iota(jnp.int32, (c, c), 0)
    j = lax.broadcasted_iota(jnp.int32, (c, c), 1)
    x = jnp.bitwise_xor(i, j)
    lvl = jnp.zeros((c, c), jnp.int32)
    for l in range(1, levels):
        lvl = lvl + jnp.where(x >= (1 << l), 1, 0)
    return jnp.where(i > j, lvl, jnp.where(i == j, levels, -1))


def _queries_else_keys(q, k, l):
    c = q.shape[0]
    span = 1 << l
    if span >= 8:
        parts = [(q if (b & 1) else k)[b * span:(b + 1) * span, :] for b in range(c // span)]
        return jnp.concatenate(parts, axis=0)
    row = lax.broadcasted_iota(jnp.int32, q.shape, 0)
    return jnp.where(jnp.bitwise_and(row, span) != 0, q, k)


def _gla_level_scores(q, k, sums):
    c = q.shape[0]
    levels = c.bit_length() - 1
    out = []
    for l in range(levels):
        x = _queries_else_keys(q, k, l) * jnp.exp2(sums[(1 + l) * c:(2 + l) * c, :])
        xb = x.astype(BF16)
        out.append(_dot_nt(xb, xb))
    return out


def _gla_chunk_out(q, k, v, sums, level_scores, pair_level, state):
    c = q.shape[0]
    levels = c.bit_length() - 1
    scores = jnp.where(pair_level == levels, jnp.sum(q * k, axis=-1, keepdims=True), 0.0)
    for l in range(levels):
        scores = jnp.where(pair_level == l, level_scores[l], scores)
    o = _dot((q * jnp.exp2(sums[0:c, :])).astype(BF16), state.astype(BF16))
    return o + _dot(scores.astype(BF16), v)


def _gla_next_state(k, v, sums, state):
    c, dk = k.shape
    levels = c.bit_length() - 1
    k_tail = (k * jnp.exp2(sums[(levels + 1) * c:(levels + 2) * c, :])).astype(BF16)
    decay = _lane_bcast_cols(jnp.exp2(sums[c - 1:c, :]), dk)
    decay_full = jnp.concatenate([decay] * (v.shape[1] // V7X_LANES), axis=1)
    return decay_full * state + _dot_tn(k_tail, v)


def _gla_kernel(*refs, heads, dk, dv, dec_dims):
    q_ref, k_ref, v_ref, ga_ref, g_ref, nw_ref = refs[:6]
    dec_in = refs[6:15]
    o_ref, s_out_ref = refs[15:17]
    dec_out = refs[17:19]
    s_ref, mats_ref, lvl_ref = refs[19:22]
    dec_scratch = refs[22:]
    t = pl.program_id(1)
    step = pl.program_id(0) * pl.num_programs(1) + t
    last_step = pl.num_programs(0) * pl.num_programs(1) - 1

    @pl.when(t == 0)
    def _():
        s_ref[...] = jnp.zeros_like(s_ref)
        mats_ref[...] = _gla_sum_matrices(GLA_CHUNK)
        lvl_ref[...] = _pair_level(GLA_CHUNK)

    _ret_decode_section(step, last_step, dec_in + dec_out, dec_scratch,
                        heads=dec_dims[0], dk=dec_dims[1], dv=dec_dims[2])

    ct = q_ref.shape[0]

    kcs = [slice(hh * dk, (hh + 1) * dk) for hh in range(heads)]
    vcs = [slice(hh * dv, (hh + 1) * dv) for hh in range(heads)]
    group = 2 if (ct // GLA_CHUNK) % 2 == 0 else 1

    def chunk_group(cg, carry):
        pair_level = lvl_ref[...]
        rows, sums, qs, ks, lvl_scores = [], [], [], [], []
        for u in range(group):
            r = pl.ds(pl.multiple_of((cg * group + u) * GLA_CHUNK, GLA_CHUNK), GLA_CHUNK)
            rows.append(r)
            sums.append(_prefix_sum_rows(mats_ref[...], g_ref[r, :]))
            qs.append([q_ref[r, kc].astype(F32) * (dk ** -0.5) for kc in kcs])
            ks.append([k_ref[r, kc].astype(F32) for kc in kcs])
            lvl_scores.append([_gla_level_scores(qs[u][hh], ks[u][hh], sums[u][:, kcs[hh]]) for hh in range(heads)])
        outs = []
        for u in range(group):
            r = rows[u]
            outs.append([_gla_chunk_out(qs[u][hh], ks[u][hh], v_ref[r, vcs[hh]], sums[u][:, kcs[hh]],
                                        lvl_scores[u][hh], pair_level, s_ref[hh]) for hh in range(heads)])
            for hh in range(heads):
                s_ref[hh] = _gla_next_state(ks[u][hh], v_ref[r, vcs[hh]], sums[u][:, kcs[hh]], s_ref[hh])
        for u in range(group):
            for hh in range(heads):
                _rms_gate_store(outs[u][hh], nw_ref[...], ga_ref[rows[u], vcs[hh]].astype(F32), o_ref, rows[u],
                                vcs[hh])
        return carry

    lax.fori_loop(0, ct // GLA_CHUNK // group, chunk_group, 0)

    @pl.when(t == pl.num_programs(1) - 1)
    def _():
        s_out_ref[0] = s_ref[...]


def _gla_prompt_ret_decode(proj, log2_decay, gla_norm_w, proj_dec, cos_dec, sin_dec, log_gamma, ret_norm_w,
                           ret_state_dec, lay, batch, seq):
    heads, dk, dv = lay["gla_heads"], lay["gla_dk"], lay["gla_dv"]
    qk, vw = heads * dk, heads * dv
    r_heads, r_dk, r_dv = lay["ret_heads"], lay["ret_dk"], lay["ret_dv"]
    ct = min(seq, GLA_STEP_CHUNKS * GLA_CHUNK)
    levels = GLA_CHUNK.bit_length() - 1
    assert seq % ct == 0 and ct % GLA_CHUNK == 0
    nt = seq // ct
    n_dec = proj_dec.shape[0]
    tps = _decode_plan(n_dec, batch * nt)
    row = lambda b, t: b * nt + t
    dec_rows, dec_state, dec_o, dec_scratch = _decode_specs("ret", lay, n_dec, tps, row)
    table_dec = pl.BlockSpec((cos_dec.shape[0], r_dk // 2), lambda b, t: (0, 0))
    kern = functools.partial(_gla_kernel, heads=heads, dk=dk, dv=dv, dec_dims=(r_heads, r_dk, r_dv))
    return pl.pallas_call(
        kern,
        grid=(batch, nt),
        in_specs=[
            pl.BlockSpec((ct, qk), lambda b, t: (row(b, t), lay["qa"] // qk)),
            pl.BlockSpec((ct, qk), lambda b, t: (row(b, t), lay["ka"] // qk)),
            pl.BlockSpec((ct, vw), lambda b, t: (row(b, t), lay["va"] // vw)),
            pl.BlockSpec((ct, vw), lambda b, t: (row(b, t), lay["ga"] // vw)),
            pl.BlockSpec((ct, qk), lambda b, t: (row(b, t), 0)),
            pl.BlockSpec((1, dv), lambda b, t: (0, 0)),
        ] + dec_rows + [
            table_dec, table_dec,
            pl.BlockSpec((r_heads, 1, V7X_LANES), lambda b, t: (0, 0, 0)),
            pl.BlockSpec((1, r_dv), lambda b, t: (0, 0)),
            dec_state,
        ],
        out_specs=[
            pl.BlockSpec((ct, vw), lambda b, t: (row(b, t), 0)),
            pl.BlockSpec((1, heads, dk, dv), lambda b, t: (b, 0, 0, 0)),
            dec_o,
            dec_state,
        ],
        out_shape=[
            jax.ShapeDtypeStruct((batch * seq, vw), BF16),
            jax.ShapeDtypeStruct((batch, heads, dk, dv), F32),
            jax.ShapeDtypeStruct((n_dec, r_heads * r_dv), BF16),
            jax.ShapeDtypeStruct(ret_state_dec.shape, ret_state_dec.dtype),
        ],
        scratch_shapes=[
            pltpu.VMEM((heads, dk, dv), F32),
            pltpu.VMEM(((levels + 2) * GLA_CHUNK, 3 * GLA_CHUNK), BF16),
            pltpu.VMEM((GLA_CHUNK, GLA_CHUNK), jnp.int32),
        ] + dec_scratch,
        compiler_params=_params(2, claim_all=True),
        name="gla_prompt_ret_decode",
    )(proj, proj, proj, proj, log2_decay, gla_norm_w, proj_dec, proj_dec, proj_dec, proj_dec, cos_dec, sin_dec,
      log_gamma, ret_norm_w, ret_state_dec)


def _rotary(x, cos, sin):
    half = x.shape[1] // 2
    x1, x2 = x[:, :half], x[:, half:]
    return jnp.concatenate([x1 * cos - x2 * sin, x1 * sin + x2 * cos], axis=1)


def _ret_kernel(*refs, heads, dk, dv, c, dec_dims):
    q_ref, k_ref, v_ref, gb_ref, cos_ref, sin_ref, lg_ref, nw_ref = refs[:8]
    dec_in = refs[8:15]
    o_ref, s_out_ref = refs[15:17]
    dec_out = refs[17:19]
    s_ref, dmat_ref, qdec_ref, kdec_ref = refs[19:23]
    dec_scratch = refs[23:]
    t = pl.program_id(1)
    step = pl.program_id(0) * pl.num_programs(1) + t
    last_step = pl.num_programs(0) * pl.num_programs(1) - 1

    _gla_decode_section(step, last_step, dec_in + dec_out, dec_scratch,
                        heads=dec_dims[0], dk=dec_dims[1], dv=dec_dims[2])

    @pl.when(t == 0)
    def _():
        s_ref[...] = jnp.zeros_like(s_ref)
        ri = lax.broadcasted_iota(jnp.int32, (c, c), 0)
        rj = lax.broadcasted_iota(jnp.int32, (c, c), 1)
        dist = (ri - rj).astype(F32)
        rowl = lax.broadcasted_iota(jnp.int32, (c, V7X_LANES), 0).astype(F32)
        for hh in range(heads):
            lg = lg_ref[hh]
            dmat_ref[hh] = jnp.exp(jnp.where(ri >= rj, dist * lg[:, :1], -jnp.inf))
            qdec_ref[hh] = jnp.exp((rowl + 1.0) * lg)
            kdec_ref[hh] = jnp.exp((float(c - 1) - rowl) * lg)

    ct = q_ref.shape[0]
    kcs = [slice(hh * dk, (hh + 1) * dk) for hh in range(heads)]
    vcs = [slice(hh * dv, (hh + 1) * dv) for hh in range(heads)]

    def chunk(ci, carry):
        rows = pl.ds(pl.multiple_of(ci * c, c), c)
        cos, sin = cos_ref[rows, :], sin_ref[rows, :]
        qrs = [_rotary(q_ref[rows, kc].astype(F32), cos, sin).astype(BF16) for kc in kcs]
        krs = [_rotary(k_ref[rows, kc].astype(F32), cos, sin) * (dk ** -0.5) for kc in kcs]
        scores = [_dot_nt(qrs[hh], krs[hh].astype(BF16)) * dmat_ref[hh] for hh in range(heads)]
        outs = []
        for hh in range(heads):
            qdec = jnp.concatenate([qdec_ref[hh]] * (dv // V7X_LANES), axis=1)
            o = qdec * _dot(qrs[hh], s_ref[hh].astype(BF16))
            outs.append(o + _dot(scores[hh].astype(BF16), v_ref[rows, vcs[hh]]))
        for hh in range(heads):
            kdec = jnp.concatenate([kdec_ref[hh]] * (dk // V7X_LANES), axis=1)
            k_tail = (krs[hh] * kdec).astype(BF16)
            lgv = jnp.concatenate([lg_ref[hh]] * (dv // V7X_LANES), axis=1)
            s_ref[hh] = jnp.exp(float(c) * lgv) * s_ref[hh] + _dot_tn(k_tail, v_ref[rows, vcs[hh]])
        for hh in range(heads):
            _ln_gate_store(outs[hh], nw_ref[...], gb_ref[rows, vcs[hh]].astype(F32), o_ref, rows, vcs[hh])
        return carry

    lax.fori_loop(0, ct // c, chunk, 0)

    @pl.when(t == pl.num_programs(1) - 1)
    def _():
        s_out_ref[0] = s_ref[...]


def _ret_prompt_gla_decode(proj, cos, sin, log_gamma, ret_norm_w, proj_dec, log2_decay_dec, gla_norm_w,
                           gla_state_dec, lay, batch, seq):
    heads, dk, dv = lay["ret_heads"], lay["ret_dk"], lay["ret_dv"]
    qk, vw = heads * dk, heads * dv
    g_heads, g_dk, g_dv = lay["gla_heads"], lay["gla_dk"], lay["gla_dv"]
    c = min(seq, RET_CHUNK)
    ct = min(seq, RET_STEP_CHUNKS * c)
    assert seq % ct == 0 and ct % c == 0
    nt = seq // ct
    half = dk // 2
    n_dec = proj_dec.shape[0]
    tps = _decode_plan(n_dec, batch * nt)
    row = lambda b, t: b * nt + t
    dec_rows, dec_state, dec_o, dec_scratch = _decode_specs("gla", lay, n_dec, tps, row)
    kern = functools.partial(_ret_kernel, heads=heads, dk=dk, dv=dv, c=c, dec_dims=(g_heads, g_dk, g_dv))
    return pl.pallas_call(
        kern,
        grid=(batch, nt),
        in_specs=[
            pl.BlockSpec((ct, qk), lambda b, t: (row(b, t), lay["qb"] // qk)),
            pl.BlockSpec((ct, qk), lambda b, t: (row(b, t), lay["kb"] // qk)),
            pl.BlockSpec((ct, vw), lambda b, t: (row(b, t), lay["vb"] // vw)),
            pl.BlockSpec((ct, vw), lambda b, t: (row(b, t), lay["gb"] // vw)),
            pl.BlockSpec((ct, half), lambda b, t: (t, 0)),
            pl.BlockSpec((ct, half), lambda b, t: (t, 0)),
            pl.BlockSpec((heads, 1, V7X_LANES), lambda b, t: (0, 0, 0)),
            pl.BlockSpec((1, dv), lambda b, t: (0, 0)),
        ] + dec_rows + [
            pl.BlockSpec((n_dec, g_heads * g_dk), lambda b, t: (0, 0)),
            pl.BlockSpec((1, g_dv), lambda b, t: (0, 0)),
            dec_state,
        ],
        out_specs=[
            pl.BlockSpec((ct, vw), lambda b, t: (row(b, t), 0)),
            pl.BlockSpec((1, heads, dk, dv), lambda b, t: (b, 0, 0, 0)),
            dec_o,
            dec_state,
        ],
        out_shape=[
            jax.ShapeDtypeStruct((batch * seq, vw), BF16),
            jax.ShapeDtypeStruct((batch, heads, dk, dv), F32),
            jax.ShapeDtypeStruct((n_dec, g_heads * g_dv), BF16),
            jax.ShapeDtypeStruct(gla_state_dec.shape, gla_state_dec.dtype),
        ],
        scratch_shapes=[
            pltpu.VMEM((heads, dk, dv), F32),
            pltpu.VMEM((heads, c, c), F32),
            pltpu.VMEM((heads, c, V7X_LANES), F32),
            pltpu.VMEM((heads, c, V7X_LANES), F32),
        ] + dec_scratch,
        compiler_params=_params(2, claim_all=True),
        name="ret_prompt_gla_decode",
    )(proj, proj, proj, proj, cos, sin, log_gamma, ret_norm_w, proj_dec, proj_dec, proj_dec, proj_dec,
      log2_decay_dec, gla_norm_w, gla_state_dec)


def _merge_kernel(oap_ref, obp_ref, oas_ref, obs_ref, wa_ref, wb_ref, g0p_ref, g1p_ref, g0s_ref, g1s_ref, wnext_ref,
                  mp_ref, ms_ref, wnext_bf_ref, wa_bf, wb_bf):
    wnext_bf_ref[...] = wnext_ref[...].astype(BF16)

    tn = wa_bf.shape[1]
    col_chunk = min(tn, 256)

    def merged(oa_ref, ob_ref, g0_ref, g1_ref, m_ref):
        oa, ob = oa_ref[...], ob_ref[...]
        for c in range(tn // col_chunk):
            cs = slice(c * col_chunk, (c + 1) * col_chunk)
            ya = _dot(oa, wa_bf[:, cs])
            yb = _dot(ob, wb_bf[:, cs])
            m = _sigmoid(g0_ref[:, cs].astype(F32)) * ya + _sigmoid(g1_ref[:, cs].astype(F32)) * yb
            m_ref[:, cs] = m.astype(m_ref.dtype)

    @pl.when(pl.program_id(1) == 0)
    def _():
        wa_bf[...] = wa_ref[...].astype(BF16)
        wb_bf[...] = wb_ref[...].astype(BF16)
        merged(oas_ref, obs_ref, g0s_ref, g1s_ref, ms_ref)

    merged(oap_ref, obp_ref, g0p_ref, g1p_ref, mp_ref)


def _slab_specs(w_next, n_steps, step_of):
    kn, dn = w_next.shape
    assert kn % n_steps == 0 and (kn // n_steps) % 16 == 0, (kn, n_steps)
    slab = kn // n_steps
    spec = pl.BlockSpec((slab, dn), lambda j, i: (step_of(j, i), 0))
    return spec, spec, jax.ShapeDtypeStruct((kn, dn), BF16), _nbytes((slab, dn), F32) + _nbytes((slab, dn), BF16)


def _merge(oa_p, ob_p, oa_s, ob_s, wa, wb, proj_p, proj_s, lay, w_next):
    m_p, ka = oa_p.shape
    kb = ob_p.shape[1]
    tail = oa_s.shape[0]
    d = wa.shape[1]
    tm = _row_tile(m_p, ROW_TILE)
    tn = min(d, 1024)
    assert d % tn == 0 and lay["mg"] % tn == 0
    g0 = lay["mg"] // tn
    g1 = (lay["mg"] + d) // tn
    n_m = m_p // tm
    slab_in, slab_out, slab_shape, slab_bytes = _slab_specs(w_next, (d // tn) * n_m, lambda j, i: j * n_m + i)
    return pl.pallas_call(
        _merge_kernel,
        grid=(d // tn, n_m),
        in_specs=[
            pl.BlockSpec((tm, ka), lambda j, i: (i, 0)),
            pl.BlockSpec((tm, kb), lambda j, i: (i, 0)),
            pl.BlockSpec((tail, ka), lambda j, i: (0, 0)),
            pl.BlockSpec((tail, kb), lambda j, i: (0, 0)),
            pl.BlockSpec((ka, tn), lambda j, i: (0, j)),
            pl.BlockSpec((kb, tn), lambda j, i: (0, j)),
            pl.BlockSpec((tm, tn), lambda j, i: (i, g0 + j)),
            pl.BlockSpec((tm, tn), lambda j, i: (i, g1 + j)),
            pl.BlockSpec((tail, tn), lambda j, i: (0, g0 + j)),
            pl.BlockSpec((tail, tn), lambda j, i: (0, g1 + j)),
            slab_in,
        ],
        out_specs=[pl.BlockSpec((tm, tn), lambda j, i: (i, j)), pl.BlockSpec((tail, tn), lambda j, i: (0, j)),
                   slab_out],
        out_shape=[jax.ShapeDtypeStruct((m_p, d), BF16), jax.ShapeDtypeStruct((tail, d), BF16), slab_shape],
        scratch_shapes=[pltpu.VMEM((ka, tn), BF16), pltpu.VMEM((kb, tn), BF16)],
        compiler_params=_params(
            2, _nbytes((tm, ka), BF16), _nbytes((tm, kb), BF16), _nbytes((tail, ka), BF16), _nbytes((tail, kb), BF16),
            _nbytes((ka, tn), F32), _nbytes((kb, tn), F32), 3 * _nbytes((tm, tn), BF16), 3 * _nbytes((tail, tn), BF16),
            slab_bytes,
            scratch_bytes=_nbytes((ka, tn), BF16) + _nbytes((kb, tn), BF16) + 3 * _nbytes((tm, tn), F32)),
        name="merge",
    )(oa_p, ob_p, oa_s, ob_s, wa, wb, proj_p, proj_p, proj_s, proj_s, w_next)


def _proj_res_norm_kernel(*refs, emit_sum, group, n_blocks):
    ap_refs, as_refs, w_refs = refs[:group], refs[group:2 * group], refs[2 * group:3 * group]
    resp_ref, ress_ref, nw_ref = refs[3 * group:3 * group + 3]
    out_refs = refs[3 * group + 3:]
    n_out = 2 if emit_sum else 1
    outs_p, outs_s = out_refs[:n_out], out_refs[n_out:]
    i = pl.program_id(0)
    k = pl.program_id(1)
    last_k = k == pl.num_programs(1) - 1
    d = w_refs[0].shape[1]
    col_chunk = min(d, 512)
    rest = n_blocks % group

    def step(a_refs, res_ref, outs):
        acc_ref = outs[0]
        nrow = acc_ref.shape[0]
        row_chunk = min(nrow, 128)
        assert nrow % row_chunk == 0

        @pl.when(k == 0)
        def _():
            acc_ref[...] = res_ref[...]

        def accumulate(n_used):
            a = [a_refs[s][...] for s in range(n_used)]
            for c in range(d // col_chunk):
                cs = slice(c * col_chunk, (c + 1) * col_chunk)
                part = _dot(a[0], w_refs[0][:, cs])
                for s in range(1, n_used):
                    part = part + _dot(a[s], w_refs[s][:, cs])
                acc_ref[:, cs] += part

        if rest == 0:
            accumulate(group)
        else:
            pl.when(jnp.logical_not(last_k))(lambda: accumulate(group))
            pl.when(last_k)(lambda: accumulate(rest))

        @pl.when(last_k)
        def _():
            def body(c, carry):
                rr = pl.ds(pl.multiple_of(c * row_chunk, row_chunk), row_chunk)
                y = _rmsnorm_rows(acc_ref[rr, :], nw_ref[...])
                if emit_sum:
                    outs[1][rr, :] = y.astype(outs[1].dtype)
                else:
                    acc_ref[rr, :] = y
                return carry

            lax.fori_loop(0, nrow // row_chunk, body, 0)

    step(ap_refs, resp_ref, outs_p)

    @pl.when(i == pl.num_programs(0) - 1)
    def _():
        step(as_refs, ress_ref, outs_s)


def _proj_res_norm(a_p, a_s, w, res_p, res_s, norm_w, emit_sum, group):
    m_p, kdim = a_p.shape
    tail = a_s.shape[0]
    d = w.shape[1]
    tm = _row_tile(m_p, ROW_TILE)
    tk = min(kdim, K_BLOCK)
    assert kdim % tk == 0
    n_blocks = kdim // tk
    group = min(group, n_blocks)
    n_steps = pl.cdiv(n_blocks, group)
    blk = lambda k, s: jnp.minimum(k * group + s, n_blocks - 1)
    p_spec = pl.BlockSpec((tm, d), lambda i, k: (i, 0))
    s_spec = pl.BlockSpec((tail, d), lambda i, k: (0, 0))
    s3_spec = pl.BlockSpec((tail, None, d), lambda i, k: (0, 0, 0))
    res_s_spec = s3_spec if res_s.ndim == 3 else s_spec
    out_specs = [p_spec, s3_spec]
    out_shape = [jax.ShapeDtypeStruct((m_p, d), F32), jax.ShapeDtypeStruct((tail, 1, d), F32)]
    assert w.dtype == BF16
    blocks = [group * _nbytes((tm, tk), BF16), group * _nbytes((tail, tk), BF16), group * _nbytes((tk, d), BF16),
              2 * _nbytes((tm, d), F32), 2 * _nbytes((tail, d), F32)]
    if emit_sum:
        out_specs = [p_spec, p_spec, s_spec, s_spec]
        out_shape = [out_shape[0], jax.ShapeDtypeStruct((m_p, d), BF16),
                     jax.ShapeDtypeStruct((tail, d), F32), jax.ShapeDtypeStruct((tail, d), BF16)]
        blocks += [_nbytes((tm, d), BF16), _nbytes((tail, d), BF16)]
    in_specs = (
        [pl.BlockSpec((tm, tk), lambda i, k, s=s: (i, blk(k, s))) for s in range(group)]
        + [pl.BlockSpec((tail, tk), lambda i, k, s=s: (0, blk(k, s))) for s in range(group)]
        + [pl.BlockSpec((tk, d), lambda i, k, s=s: (blk(k, s), 0)) for s in range(group)]
        + [p_spec, res_s_spec, pl.BlockSpec((1, d), lambda i, k: (0, 0))])
    return pl.pallas_call(
        functools.partial(_proj_res_norm_kernel, emit_sum=emit_sum, group=group, n_blocks=n_blocks),
        grid=(m_p // tm, n_steps),
        in_specs=in_specs,
        out_specs=out_specs,
        out_shape=out_shape,
        compiler_params=_params(2, *blocks),
        name="proj_res_norm",
    )(*([a_p] * group + [a_s] * group + [w] * group), res_p, res_s, norm_w.reshape(1, d))


def _swiglu_kernel(hp_ref, hs_ref, wg_ref, wu_ref, wnext_ref, op_ref, os_ref, wnext_bf_ref, wg_bf, wu_bf):
    wnext_bf_ref[...] = wnext_ref[...].astype(BF16)

    tn = wg_bf.shape[1]
    col_chunk = min(tn, 256)

    def act(h_ref, o_ref):
        h = h_ref[...]
        for c in range(tn // col_chunk):
            cs = slice(c * col_chunk, (c + 1) * col_chunk)
            a = _dot(h, wg_bf[:, cs])
            b = _dot(h, wu_bf[:, cs])
            o_ref[:, cs] = (_silu(a) * b).astype(o_ref.dtype)

    @pl.when(pl.program_id(1) == 0)
    def _():
        wg_bf[...] = wg_ref[...].astype(BF16)
        wu_bf[...] = wu_ref[...].astype(BF16)
        act(hs_ref, os_ref)

    act(hp_ref, op_ref)


def _swiglu(h_p, h_s, wg, wu, w_next):
    m_p, d = h_p.shape
    tail = h_s.shape[0]
    f = wg.shape[1]
    tm = _row_tile(m_p, ROW_TILE_WIDE)
    tn = 512 if f % 512 == 0 else 256
    assert f % tn == 0
    n_m = m_p // tm
    slab_in, slab_out, slab_shape, slab_bytes = _slab_specs(w_next, (f // tn) * n_m, lambda j, i: j * n_m + i)
    return pl.pallas_call(
        _swiglu_kernel,
        grid=(f // tn, n_m),
        in_specs=[
            pl.BlockSpec((tm, d), lambda j, i: (i, 0)),
            pl.BlockSpec((tail, d), lambda j, i: (0, 0)),
            pl.BlockSpec((d, tn), lambda j, i: (0, j)),
            pl.BlockSpec((d, tn), lambda j, i: (0, j)),
            slab_in,
        ],
        out_specs=[pl.BlockSpec((tm, tn), lambda j, i: (i, j)), pl.BlockSpec((tail, tn), lambda j, i: (0, j)),
                   slab_out],
        out_shape=[jax.ShapeDtypeStruct((m_p, f), BF16), jax.ShapeDtypeStruct((tail, f), BF16), slab_shape],
        scratch_shapes=[pltpu.VMEM((d, tn), BF16), pltpu.VMEM((d, tn), BF16)],
        compiler_params=_params(
            2, _nbytes((tm, d), BF16), _nbytes((tail, d), BF16), 2 * _nbytes((d, tn), F32),
            _nbytes((tm, tn), BF16), _nbytes((tail, tn), BF16), slab_bytes,
            scratch_bytes=2 * _nbytes((d, tn), BF16) + 3 * _nbytes((tm, tn), F32)),
        name="swiglu",
    )(h_p, h_s, wg, wu, w_next)


def _layout(d_model, in_width, state_gla, state_ret, gate_rank):
    _, _, gh, gdk, gdv = state_gla.shape
    _, _, rh, rdk, rdv = state_ret.shape
    gqk, gv, rqk, rv = gh * gdk, gh * gdv, rh * rdk, rh * rdv
    lay = dict(gla_heads=gh, gla_dk=gdk, gla_dv=gdv, ret_heads=rh, ret_dk=rdk, ret_dv=rdv, rank=gate_rank)
    off = 0
    for name, width in (("qa", gqk), ("ka", gqk), ("va", gv), ("ga", gv), ("qb", rqk), ("kb", rqk),
                        ("vb", rv), ("gb", rv), ("mg", 2 * d_model)):
        lay[name] = off
        off += width
    lay["out_cols"] = off
    lay["plain_cols"] = 2 * gqk + gv
    lay["gd_src"] = lay["plain_cols"]
    assert lay["gd_src"] % V7X_LANES == 0 and gate_rank <= V7X_LANES
    assert in_width == off + gate_rank
    return lay


def _layer(x_p, x_s, st_gla, st_ret, wts, lay, log_gamma, final_norm):
    (norm_mix, w_in, w_gate_up, b_gate, gla_norm_w, w_gla_up, ret_norm_w, w_ret_up, w_out, norm_ffn,
     w_ffn_gate, w_ffn_up, w_ffn_down) = wts
    batch, seq, d = x_p.shape
    rank = lay["rank"]
    gqk = lay["gla_heads"] * lay["gla_dk"]
    bup = b_gate.reshape(1, gqk)
    gnw = gla_norm_w.reshape(1, -1)
    rnw = ret_norm_w.reshape(1, -1)
    tn = 1024 if (lay["out_cols"] % 1024 == 0 and lay["plain_cols"] % 1024 == 0) else 512
    xp = x_p.reshape(batch * seq, d)
    assert x_s.ndim == 3 and x_s.shape[1] == 1, "one new token per decode sequence"
    xs = x_s
    w_in_t = w_in.T

    h_p, h_s, g_p, g_s, cos_p, sin_p, cos_s, sin_s = _rmsnorm_gate(
        xp, xs, norm_mix, w_in_t, lay["gd_src"], w_gate_up, bup, seq, lay["ret_dk"] // 2)
    proj_p, proj_s = _in_proj(h_p, h_s, w_in_t, lay["plain_cols"], rank, lay["out_cols"], tn)
    oa_p, sa_p, ob_s, sb_s = _gla_prompt_ret_decode(
        proj_p, g_p, gnw, proj_s, cos_s, sin_s, log_gamma, rnw, st_ret, lay, batch, seq)
    ob_p, sb_p, oa_s, sa_s = _ret_prompt_gla_decode(
        proj_p, cos_p, sin_p, log_gamma, rnw, proj_s, g_s, gnw, st_gla, lay, batch, seq)
    m_p, m_s, w_out_bf = _merge(oa_p, ob_p, oa_s, ob_s, w_gla_up, w_ret_up, proj_p, proj_s, lay, w_out)
    x1_p, h2_p, x1_s, h2_s = _proj_res_norm(m_p, m_s, w_out_bf, xp, xs, norm_ffn, True, 2)
    act_p, act_s, w_down_bf = _swiglu(h2_p, h2_s, w_ffn_gate, w_ffn_up, w_ffn_down)
    y_p, y_s = _proj_res_norm(act_p, act_s, w_down_bf, x1_p, x1_s, final_norm, False, 3)
    return (y_p, sa_p, sb_p), (y_s, sa_s, sb_s)


def kernel(x_prompt, x_sample, state_gla, state_ret, norm_mix, w_in, w_gla_gate_up, b_gla_gate, gla_norm_w,
           w_gla_up, ret_norm_w, w_ret_up, w_out, norm_ffn, w_ffn_gate, w_ffn_up, w_ffn_down, norm_final):
    depth = w_in.shape[0]
    assert depth == 1, "single-layer trunk"
    batch, seq, d = x_prompt.shape
    lay = _layout(d, w_in.shape[-1], state_gla, state_ret, w_gla_gate_up.shape[1])
    rh, rdk = lay["ret_heads"], lay["ret_dk"]
    assert rdk // 2 == V7X_LANES
    lg = jnp.log1p(-jnp.exp(jnp.linspace(math.log(1.0 / 32), math.log(1.0 / 512), rh))).astype(F32)
    log_gamma = jnp.broadcast_to(lg[:, None, None], (rh, 1, V7X_LANES))

    wts = (norm_mix[0], w_in[0], w_gla_gate_up[0], b_gla_gate[0], gla_norm_w[0], w_gla_up[0], ret_norm_w[0],
           w_ret_up[0], w_out[0], norm_ffn[0], w_ffn_gate[0], w_ffn_up[0], w_ffn_down[0])
    (y_p, ga_p, re_p), (y_s, ga_s, re_s) = _layer(
        x_prompt, x_sample, state_gla[0], state_ret[0], wts, lay, log_gamma, norm_final)

    sd = state_gla.dtype
    return (y_p.reshape(batch, seq, d), y_s.reshape(x_sample.shape),
            ga_p[None].astype(sd), re_p[None].astype(state_ret.dtype),
            ga_s[None].astype(sd), re_s[None].astype(state_ret.dtype))
```

```python
import functools
import math

import numpy as np
import jax
import jax.numpy as jnp
from jax import lax
from jax.experimental import pallas as pl
from jax.experimental.pallas import tpu as pltpu

EPS = 1e-6
ROPE_BASE = 10000.0
GLA_GATE_NORM = 16.0
PAST_LEN = 16384

V7X_LANES = 128
V7X_VMEM_REQUEST_CAP = 60000 * 1024
COMPILER_SCRATCH_BYTES = 12 * 1024 * 1024

GLA_CHUNK = 64
GLA_STEP_CHUNKS = 8
LOG2_E = 1.4426950408889634
RET_CHUNK = 128
RET_STEP_CHUNKS = 4
ROW_TILE = 1024
ROW_TILE_WIDE = 2048
K_BLOCK = 512

BF16 = jnp.bfloat16
F32 = jnp.float32


def _params(n_axes, *block_bytes, scratch_bytes=0, claim_all=False):
    need = 2 * sum(block_bytes) + scratch_bytes + COMPILER_SCRATCH_BYTES
    if claim_all:
        need = V7X_VMEM_REQUEST_CAP
    return pltpu.CompilerParams(
        dimension_semantics=("arbitrary",) * n_axes,
        vmem_limit_bytes=int(min(V7X_VMEM_REQUEST_CAP, need)),
    )


def _nbytes(shape, dtype):
    return int(np.prod(shape)) * jnp.dtype(dtype).itemsize


def _sigmoid(x):
    return 1.0 / (1.0 + jnp.exp(-x))


def _silu(x):
    return x * _sigmoid(x)


def _log_sigmoid(x):
    return jnp.minimum(x, 0.0) - jnp.log(1.0 + jnp.exp(-jnp.abs(x)))


def _dot(a, b):
    return jnp.dot(a, b, preferred_element_type=F32)


def _dot_nt(a, b):
    return lax.dot_general(a, b, (((1,), (1,)), ((), ())), preferred_element_type=F32)


def _dot_tn(a, b):
    return lax.dot_general(a, b, (((0,), (0,)), ((), ())), preferred_element_type=F32)


def _row_tile(m, want):
    t = min(m, want)
    assert m % t == 0, (m, t)
    return t


def _rmsnorm_rows(x, w):
    ms = jnp.mean(x * x, axis=-1, keepdims=True)
    return x * lax.rsqrt(ms + EPS) * w


def _rope_rows(cos_ref, sin_ref, pos0):
    rows, half = cos_ref.shape
    pos = (lax.broadcasted_iota(jnp.int32, (rows, half), 0) + pos0).astype(F32)
    idx = lax.broadcasted_iota(jnp.int32, (rows, half), 1).astype(F32)
    ang = pos * jnp.exp(idx * (-math.log(ROPE_BASE) / half))
    cos_ref[...] = jnp.cos(ang)
    sin_ref[...] = jnp.sin(ang)


def _rmsnorm_kernel(xp_ref, xs_ref, w_ref, wgd_ref, wup_ref, bup_ref, w0_ref,
                    hp_ref, hs_ref, gp_ref, gs_ref, cosp_ref, sinp_ref, cosd_ref, sind_ref, p0p_ref, p0s_ref,
                    wgd_bf, wup_bf, w0_bf):
    i = pl.program_id(0)

    @pl.when(i == 0)
    def _():
        wgd_bf[...] = wgd_ref[...].astype(BF16)
        wup_bf[...] = jnp.zeros_like(wup_bf)
        wup_bf[0:wup_ref.shape[0], :] = wup_ref[...].astype(BF16)
        w0_bf[...] = w0_ref[...].astype(BF16)

    def rows(x, h_ref, g_ref, p0_ref):
        h = _rmsnorm_rows(x, w_ref[...]).astype(h_ref.dtype)
        h_ref[...] = h
        p0_ref[...] = _dot_nt(h, w0_bf[...]).astype(p0_ref.dtype)
        gd = _dot_nt(h, wgd_bf[...])
        x = _dot(gd.astype(BF16), wup_bf[...]) + bup_ref[...]
        g_ref[...] = _log_sigmoid(x) * (LOG2_E / GLA_GATE_NORM)

    rows(xp_ref[...], hp_ref, gp_ref, p0p_ref)
    _rope_rows(cosp_ref, sinp_ref, i * cosp_ref.shape[0])

    @pl.when(i == 0)
    def _():
        rows(xs_ref[...], hs_ref, gs_ref, p0s_ref)
        _rope_rows(cosd_ref, sind_ref, PAST_LEN)


def _rmsnorm_gate(x_p, x_s, w, w_in_t, gate_row0, w_gate_up, bup, seq, half, tn0):
    m_p, d = x_p.shape
    tail = x_s.shape[0]
    rank, gw = w_gate_up.shape
    tm = _row_tile(m_p, ROW_TILE // 2)
    n_steps = m_p // tm
    assert gate_row0 % V7X_LANES == 0 and seq % n_steps == 0 and rank % 16 == 0
    pos_rows = seq // n_steps
    table = pl.BlockSpec((pos_rows, half), lambda i: (i, 0))
    table_dec = pl.BlockSpec((8, half), lambda i: (0, 0))
    return pl.pallas_call(
        _rmsnorm_kernel,
        grid=(n_steps,),
        in_specs=[
            pl.BlockSpec((tm, d), lambda i: (i, 0)),
            pl.BlockSpec((tail, None, d), lambda i: (0, 0, 0)),
            pl.BlockSpec((1, d), lambda i: (0, 0)),
            pl.BlockSpec((V7X_LANES, d), lambda i: (gate_row0 // V7X_LANES, 0)),
            pl.BlockSpec((rank, gw), lambda i: (0, 0)),
            pl.BlockSpec((1, gw), lambda i: (0, 0)),
            pl.BlockSpec((tn0, d), lambda i: (0, 0)),
        ],
        out_specs=[
            pl.BlockSpec((tm, d), lambda i: (i, 0)), pl.BlockSpec((tail, d), lambda i: (0, 0)),
            pl.BlockSpec((tm, gw), lambda i: (i, 0)), pl.BlockSpec((tail, gw), lambda i: (0, 0)),
            table, table, table_dec, table_dec,
            pl.BlockSpec((tm, tn0), lambda i: (i, 0)), pl.BlockSpec((tail, tn0), lambda i: (0, 0)),
        ],
        out_shape=[
            jax.ShapeDtypeStruct((m_p, d), BF16), jax.ShapeDtypeStruct((tail, d), BF16),
            jax.ShapeDtypeStruct((m_p, gw), F32), jax.ShapeDtypeStruct((tail, gw), F32),
            jax.ShapeDtypeStruct((seq, half), F32), jax.ShapeDtypeStruct((seq, half), F32),
            jax.ShapeDtypeStruct((8, half), F32), jax.ShapeDtypeStruct((8, half), F32),
            jax.ShapeDtypeStruct((m_p, tn0), BF16), jax.ShapeDtypeStruct((tail, tn0), BF16),
        ],
        scratch_shapes=[pltpu.VMEM((V7X_LANES, d), BF16), pltpu.VMEM((V7X_LANES, gw), BF16),
                        pltpu.VMEM((tn0, d), BF16)],
        compiler_params=_params(1, claim_all=True),
        name="rmsnorm_gate",
    )(x_p, x_s, w.reshape(1, d), w_in_t, w_gate_up, bup, w_in_t)


def _in_proj_kernel(hp_ref, hs_ref, wm_ref, wn_ref, op_ref, os_ref, wbf_ref, *, n_plain, shift, first_tile):
    j = pl.program_id(0) + first_tile
    i = pl.program_id(1)
    tn = wbf_ref.shape[0]

    @pl.when(jnp.logical_and(i == 0, j < n_plain))
    def _():
        wbf_ref[...] = wm_ref[...].astype(BF16)

    @pl.when(jnp.logical_and(i == 0, j >= n_plain))
    def _():
        wbf_ref[0:tn - shift, :] = wm_ref[shift:tn, :].astype(BF16)
        wbf_ref[tn - shift:tn, :] = wn_ref[...].astype(BF16)

    @pl.when(i == 0)
    def _():
        os_ref[...] = _dot_nt(hs_ref[...], wbf_ref[...]).astype(os_ref.dtype)

    op_ref[...] = _dot_nt(hp_ref[...], wbf_ref[...]).astype(op_ref.dtype)


def _in_proj(h_p, h_s, w_in_t, plain_cols, shift, out_cols, tn, first_tile):
    m_p, d = h_p.shape
    tail = h_s.shape[0]
    tm = _row_tile(m_p, ROW_TILE_WIDE)
    assert plain_cols % tn == 0 and out_cols % tn == 0 and tn % shift == 0 and shift % 8 == 0
    n_plain = plain_cols // tn
    n_tiles = out_cols // tn - first_tile
    kern = functools.partial(_in_proj_kernel, n_plain=n_plain, shift=shift, first_tile=first_tile)
    return pl.pallas_call(
        kern,
        grid=(n_tiles, m_p // tm),
        in_specs=[
            pl.BlockSpec((tm, d), lambda j, i: (i, 0)),
            pl.BlockSpec((tail, d), lambda j, i: (0, 0)),
            pl.BlockSpec((tn, d), lambda j, i: (j + first_tile, 0)),
            pl.BlockSpec((shift, d), lambda j, i: ((j + first_tile + 1) * (tn // shift), 0)),
        ],
        out_specs=[pl.BlockSpec((tm, tn), lambda j, i: (i, j)), pl.BlockSpec((tail, tn), lambda j, i: (0, j))],
        out_shape=[jax.ShapeDtypeStruct((m_p, n_tiles * tn), BF16), jax.ShapeDtypeStruct((tail, n_tiles * tn), BF16)],
        scratch_shapes=[pltpu.VMEM((tn, d), BF16)],
        compiler_params=_params(
            2, _nbytes((tm, d), BF16), _nbytes((tail, d), BF16), _nbytes((tn, d), F32), _nbytes((shift, d), F32),
            _nbytes((tm, tn), BF16), _nbytes((tail, tn), BF16),
            scratch_bytes=_nbytes((tn, d), BF16) + _nbytes((tm, tn), F32)),
        name="in_proj",
    )(h_p, h_s, w_in_t, w_in_t)


def _prefix_sum_rows(sel3_bf16, g):
    g0 = g.astype(BF16)
    r1 = g - g0.astype(F32)
    g1 = r1.astype(BF16)
    g2 = (r1 - g1.astype(F32)).astype(BF16)
    return _dot(sel3_bf16, jnp.concatenate([g0, g1, g2], axis=0))


def _lane_bcast_cols(row, n):
    parts = []
    for c in range(n // V7X_LANES):
        tile = jnp.broadcast_to(row[:, c * V7X_LANES:(c + 1) * V7X_LANES], (V7X_LANES, V7X_LANES))
        parts.append(tile.T)
    return parts[0] if len(parts) == 1 else jnp.concatenate(parts, axis=0)


def _rms_gate_store(o, w, gate, out_ref, rows, cols):
    ms = jnp.mean(o * o, axis=-1, keepdims=True)
    y = o * lax.rsqrt(ms + EPS) * w
    out_ref[rows, cols] = (y * _silu(gate)).astype(out_ref.dtype)


def _ln_gate_store(o, w, gate, out_ref, rows, cols):
    mu = jnp.mean(o, axis=-1, keepdims=True)
    dlt = o - mu
    var = jnp.mean(dlt * dlt, axis=-1, keepdims=True)
    y = dlt * lax.rsqrt(var + EPS) * w
    out_ref[rows, cols] = (y * _silu(gate)).astype(out_ref.dtype)


def _token_selectors(n_tok):
    assert 3 * n_tok <= V7X_LANES
    j = lax.broadcasted_iota(jnp.int32, (V7X_LANES, V7X_LANES), 0)
    sel = []
    for t in range(n_tok):
        hit = jnp.logical_or(j == t, jnp.logical_or(j == n_tok + t, j == 2 * n_tok + t))
        sel.append(jnp.where(hit, 1.0, 0.0).astype(BF16))
    return jnp.stack(sel, axis=0)


def _column_source(x):
    n_tok, w = x.shape
    hi = x.astype(BF16).astype(F32)
    r1 = x - hi
    mid = r1.astype(BF16).astype(F32)
    lo = (r1 - mid).astype(BF16).astype(F32)
    x3 = jnp.concatenate([hi, mid, lo, jnp.zeros((V7X_LANES - 3 * n_tok, w), F32)], axis=0)
    parts = [x3[:, c * V7X_LANES:(c + 1) * V7X_LANES].T for c in range(w // V7X_LANES)]
    return (parts[0] if len(parts) == 1 else jnp.concatenate(parts, axis=0)).astype(BF16)


def _decode_advance(tok0, decay_rows_fn, decay_const_fn, k_ref, q_ref, v_ref, s_in_ref, s_out_ref, o_ref, sel_ref,
                    *, heads, dk, dv):
    n_tok = s_in_ref.shape[0]
    reps = dv // V7X_LANES
    rows = pl.ds(pl.multiple_of(tok0, n_tok), n_tok)

    def cols(src, tt):
        return jnp.concatenate([_dot(src, sel_ref[tt])] * reps, axis=1)

    for hh in range(heads):
        kc = slice(hh * dk, (hh + 1) * dk)
        vc = slice(hh * dv, (hh + 1) * dv)
        k_src, q_src = _column_source(k_ref[rows, kc]), _column_source(q_ref[rows, kc])
        a_src = None if decay_rows_fn is None else _column_source(decay_rows_fn(rows, hh))
        v = v_ref[rows, vc]
        o_rows = []
        for tt in range(n_tok):
            decay = decay_const_fn(hh) if a_src is None else cols(a_src, tt)
            s_new = decay * s_in_ref[tt, hh] + cols(k_src, tt) * v[tt:tt + 1, :]
            s_out_ref[tt, hh] = s_new
            o_rows.append(jnp.sum(cols(q_src, tt) * s_new, axis=0, keepdims=True))
        o_ref[rows, vc] = jnp.concatenate(o_rows, axis=0)


def _decode_plan(n_dec, n_steps):
    assert n_dec % n_steps == 0 and (n_dec // n_steps) % 8 == 0, (n_dec, n_steps)
    return n_dec // n_steps


def _gla_decode_section(step, last_step, refs, scratch, *, heads, dk, dv):
    qd_ref, kd_ref, vd_ref, gad_ref, gd_ref, nw_ref, sd_in_ref, od_ref, sd_out_ref = refs
    a_dec, q_dec, k_dec, v_dec, o_dec, sel_ref = scratch

    @pl.when(step == 0)
    def _():
        a_dec[...] = jnp.exp2(gd_ref[...])
        q_dec[...] = qd_ref[...].astype(F32) * (dk ** -0.5)
        k_dec[...] = kd_ref[...].astype(F32)
        v_dec[...] = vd_ref[...].astype(F32)
        sel_ref[...] = _token_selectors(sd_in_ref.shape[0])

    _decode_advance(step * sd_in_ref.shape[0], lambda rows, hh: a_dec[rows, hh * dk:(hh + 1) * dk], None,
                    k_dec, q_dec, v_dec, sd_in_ref, sd_out_ref, o_dec, sel_ref, heads=heads, dk=dk, dv=dv)

    @pl.when(step == last_step)
    def _():
        n_dec = o_dec.shape[0]
        for hh in range(heads):
            vc = slice(hh * dv, (hh + 1) * dv)
            _rms_gate_store(o_dec[:, vc], nw_ref[...], gad_ref[:, vc].astype(F32), od_ref, slice(0, n_dec), vc)


def _ret_decode_section(step, last_step, refs, scratch, *, heads, dk, dv):
    qd_ref, kd_ref, vd_ref, gbd_ref, cosd_ref, sind_ref, lg_ref, nw_ref, sd_in_ref, od_ref, sd_out_ref = refs
    q_dec, k_dec, v_dec, o_dec, sel_ref = scratch

    @pl.when(step == 0)
    def _():
        cosd, sind = cosd_ref[0:1, :], sind_ref[0:1, :]
        for hh in range(heads):
            kc = slice(hh * dk, (hh + 1) * dk)
            q_dec[:, kc] = _rotary(qd_ref[:, kc].astype(F32), cosd, sind)
            k_dec[:, kc] = _rotary(kd_ref[:, kc].astype(F32), cosd, sind) * (dk ** -0.5)
        v_dec[...] = vd_ref[...].astype(F32)
        sel_ref[...] = _token_selectors(sd_in_ref.shape[0])

    def gamma(hh):
        return jnp.exp(jnp.concatenate([lg_ref[hh]] * (dv // V7X_LANES), axis=1))

    _decode_advance(step * sd_in_ref.shape[0], None, gamma, k_dec, q_dec, v_dec, sd_in_ref, sd_out_ref, o_dec,
                    sel_ref, heads=heads, dk=dk, dv=dv)

    @pl.when(step == last_step)
    def _():
        n_dec = o_dec.shape[0]
        for hh in range(heads):
            vc = slice(hh * dv, (hh + 1) * dv)
            _ln_gate_store(o_dec[:, vc], nw_ref[...], gbd_ref[:, vc].astype(F32), od_ref, slice(0, n_dec), vc)


def _segment_block(lay, name, width):
    offset = lay["cols"][name][1]
    assert offset % width == 0
    return offset // width


def _segment_arrays(lay, arrays, names):
    return tuple(arrays[lay["cols"][n][0]] for n in names)


def _decode_specs(kind, lay, n_dec, tps, row):
    heads, dk, dv = lay[kind + "_heads"], lay[kind + "_dk"], lay[kind + "_dv"]
    qk, vw = heads * dk, heads * dv
    names = ("qa", "ka", "va", "ga") if kind == "gla" else ("qb", "kb", "vb", "gb")
    widths = (qk, qk, vw, vw)
    rows_in = [pl.BlockSpec((n_dec, w), lambda b, t, c=_segment_block(lay, n, w): (0, c))
               for n, w in zip(names, widths)]
    state = pl.BlockSpec((tps, heads, dk, dv), lambda b, t: (row(b, t), 0, 0, 0))
    o_spec = pl.BlockSpec((n_dec, vw), lambda b, t: (0, 0))
    n_qk = 3 if kind == "gla" else 2
    scratch = ([pltpu.VMEM((n_dec, qk), F32)] * n_qk + [pltpu.VMEM((n_dec, vw), F32)] * 2
               + [pltpu.VMEM((tps, V7X_LANES, V7X_LANES), BF16)])
    return rows_in, state, o_spec, scratch


def _gla_sum_matrices(c):
    levels = c.bit_length() - 1
    assert 1 << levels == c
    i = lax.broadcasted_iota(jnp.int32, (c, c), 0)
    j = lax.broadcasted_iota(jnp.int32, (c, c), 1)
    mats = [j <= i]
    for l in range(levels):
        ref = jnp.bitwise_or(jnp.bitwise_and(i, -(2 << l)), 1 << l)
        mats.append(jnp.logical_and(j > jnp.minimum(i, ref), j <= jnp.maximum(i, ref)))
    mats.append(j > i)
    sel = jnp.concatenate([jnp.where(m, 1.0, 0.0).astype(BF16) for m in mats], axis=0)
    return jnp.concatenate([sel, sel, sel], axis=1)


def _pair_level(c):
    levels = c.bit_length() - 1
    i = lax.broadcasted_iota(jnp.int32, (c, c), 0)
    j = lax.broadcasted_iota(jnp.int32, (c, c), 1)
    x = jnp.bitwise_xor(i, j)
    lvl = jnp.zeros((c, c), jnp.int32)
    for l in range(1, levels):
        lvl = lvl + jnp.where(x >= (1 << l), 1, 0)
    return jnp.where(i > j, lvl, jnp.where(i == j, levels, -1))


def _queries_else_keys(q, k, l):
    c = q.shape[0]
    span = 1 << l
    if span >= 8:
        parts = [(q if (b & 1) else k)[b * span:(b + 1) * span, :] for b in range(c // span)]
        return jnp.concatenate(parts, axis=0)
    row = lax.broadcasted_iota(jnp.int32, q.shape, 0)
    return jnp.where(jnp.bitwise_and(row, span) != 0, q, k)


def _gla_level_scores(q, k, sums):
    c = q.shape[0]
    levels = c.bit_length() - 1
    out = []
    for l in range(levels):
        x = _queries_else_keys(q, k, l) * jnp.exp2(sums[(1 + l) * c:(2 + l) * c, :])
        xb = x.astype(BF16)
        out.append(_dot_nt(xb, xb))
    return out


def _gla_chunk_out(q, k, v, sums, level_scores, pair_level, state):
    c = q.shape[0]
    levels = c.bit_length() - 1
    scores = jnp.where(pair_level == levels, jnp.sum(q * k, axis=-1, keepdims=True), 0.0)
    for l in range(levels):
        scores = jnp.where(pair_level == l, level_scores[l], scores)
    o = _dot((q * jnp.exp2(sums[0:c, :])).astype(BF16), state.astype(BF16))
    return o + _dot(scores.astype(BF16), v)


def _gla_next_state(k, v, sums, state):
    c, dk = k.shape
    levels = c.bit_length() - 1
    k_tail = (k * jnp.exp2(sums[(levels + 1) * c:(levels + 2) * c, :])).astype(BF16)
    decay = _lane_bcast_cols(jnp.exp2(sums[c - 1:c, :]), dk)
    decay_full = jnp.concatenate([decay] * (v.shape[1] // V7X_LANES), axis=1)
    return decay_full * state + _dot_tn(k_tail, v)


def _gla_kernel(*refs, heads, dk, dv, dec_dims):
    q_ref, k_ref, v_ref, ga_ref, g_ref, nw_ref = refs[:6]
    dec_in = refs[6:15]
    o_ref, s_out_ref = refs[15:17]
    dec_out = refs[17:19]
    s_ref, mats_ref, lvl_ref = refs[19:22]
    dec_scratch = refs[22:]
    t = pl.program_id(1)
    step = pl.program_id(0) * pl.num_programs(1) + t
    last_step = pl.num_programs(0) * pl.num_programs(1) - 1

    @pl.when(t == 0)
    def _():
        s_ref[...] = jnp.zeros_like(s_ref)
        mats_ref[...] = _gla_sum_matrices(GLA_CHUNK)
        lvl_ref[...] = _pair_level(GLA_CHUNK)

    _ret_decode_section(step, last_step, dec_in + dec_out, dec_scratch,
                        heads=dec_dims[0], dk=dec_dims[1], dv=dec_dims[2])

    ct = q_ref.shape[0]

    kcs = [slice(hh * dk, (hh + 1) * dk) for hh in range(heads)]
    vcs = [slice(hh * dv, (hh + 1) * dv) for hh in range(heads)]
    group = 2 if (ct // GLA_CHUNK) % 2 == 0 else 1

    def chunk_group(cg, carry):
        pair_level = lvl_ref[...]
        rows, sums, qs, ks, lvl_scores = [], [], [], [], []
        for u in range(group):
            r = pl.ds(pl.multiple_of((cg * group + u) * GLA_CHUNK, GLA_CHUNK), GLA_CHUNK)
            rows.append(r)
            sums.append(_prefix_sum_rows(mats_ref[...], g_ref[r, :]))
            qs.append([q_ref[r, kc].astype(F32) * (dk ** -0.5) for kc in kcs])
            ks.append([k_ref[r, kc].astype(F32) for kc in kcs])
            lvl_scores.append([_gla_level_scores(qs[u][hh], ks[u][hh], sums[u][:, kcs[hh]]) for hh in range(heads)])
        outs = []
        for u in range(group):
            r = rows[u]
            outs.append([_gla_chunk_out(qs[u][hh], ks[u][hh], v_ref[r, vcs[hh]], sums[u][:, kcs[hh]],
                                        lvl_scores[u][hh], pair_level, s_ref[hh]) for hh in range(heads)])
            for hh in range(heads):
                s_ref[hh] = _gla_next_state(ks[u][hh], v_ref[r, vcs[hh]], sums[u][:, kcs[hh]], s_ref[hh])
        for u in range(group):
            for hh in range(heads):
                _rms_gate_store(outs[u][hh], nw_ref[...], ga_ref[rows[u], vcs[hh]].astype(F32), o_ref, rows[u],
                                vcs[hh])
        return carry

    lax.fori_loop(0, ct // GLA_CHUNK // group, chunk_group, 0)

    @pl.when(t == pl.num_programs(1) - 1)
    def _():
        s_out_ref[0] = s_ref[...]


def _gla_prompt_ret_decode(projs, log2_decay, gla_norm_w, projs_dec, cos_dec, sin_dec, log_gamma, ret_norm_w,
                           ret_state_dec, lay, batch, seq):
    heads, dk, dv = lay["gla_heads"], lay["gla_dk"], lay["gla_dv"]
    qk, vw = heads * dk, heads * dv
    r_heads, r_dk, r_dv = lay["ret_heads"], lay["ret_dk"], lay["ret_dv"]
    ct = min(seq, GLA_STEP_CHUNKS * GLA_CHUNK)
    levels = GLA_CHUNK.bit_length() - 1
    assert seq % ct == 0 and ct % GLA_CHUNK == 0
    nt = seq // ct
    n_dec = projs_dec[0].shape[0]
    tps = _decode_plan(n_dec, batch * nt)
    row = lambda b, t: b * nt + t
    dec_rows, dec_state, dec_o, dec_scratch = _decode_specs("ret", lay, n_dec, tps, row)
    table_dec = pl.BlockSpec((cos_dec.shape[0], r_dk // 2), lambda b, t: (0, 0))
    kern = functools.partial(_gla_kernel, heads=heads, dk=dk, dv=dv, dec_dims=(r_heads, r_dk, r_dv))
    return pl.pallas_call(
        kern,
        grid=(batch, nt),
        in_specs=[
            pl.BlockSpec((ct, qk), lambda b, t: (row(b, t), _segment_block(lay, "qa", qk))),
            pl.BlockSpec((ct, qk), lambda b, t: (row(b, t), _segment_block(lay, "ka", qk))),
            pl.BlockSpec((ct, vw), lambda b, t: (row(b, t), _segment_block(lay, "va", vw))),
            pl.BlockSpec((ct, vw), lambda b, t: (row(b, t), _segment_block(lay, "ga", vw))),
            pl.BlockSpec((ct, qk), lambda b, t: (row(b, t), 0)),
            pl.BlockSpec((1, dv), lambda b, t: (0, 0)),
        ] + dec_rows + [
            table_dec, table_dec,
            pl.BlockSpec((r_heads, 1, V7X_LANES), lambda b, t: (0, 0, 0)),
            pl.BlockSpec((1, r_dv), lambda b, t: (0, 0)),
            dec_state,
        ],
        out_specs=[
            pl.BlockSpec((ct, vw), lambda b, t: (row(b, t), 0)),
            pl.BlockSpec((1, heads, dk, dv), lambda b, t: (b, 0, 0, 0)),
            dec_o,
            dec_state,
        ],
        out_shape=[
            jax.ShapeDtypeStruct((batch * seq, vw), BF16),
            jax.ShapeDtypeStruct((batch, heads, dk, dv), F32),
            jax.ShapeDtypeStruct((n_dec, r_heads * r_dv), BF16),
            jax.ShapeDtypeStruct(ret_state_dec.shape, ret_state_dec.dtype),
        ],
        scratch_shapes=[
            pltpu.VMEM((heads, dk, dv), F32),
            pltpu.VMEM(((levels + 2) * GLA_CHUNK, 3 * GLA_CHUNK), BF16),
            pltpu.VMEM((GLA_CHUNK, GLA_CHUNK), jnp.int32),
        ] + dec_scratch,
        compiler_params=_params(2, claim_all=True),
        name="gla_prompt_ret_decode",
    )(*_segment_arrays(lay, projs, ("qa", "ka", "va", "ga")), log2_decay, gla_norm_w,
      *_segment_arrays(lay, projs_dec, ("qb", "kb", "vb", "gb")), cos_dec, sin_dec,
      log_gamma, ret_norm_w, ret_state_dec)


def _rotary(x, cos, sin):
    half = x.shape[1] // 2
    x1, x2 = x[:, :half], x[:, half:]
    return jnp.concatenate([x1 * cos - x2 * sin, x1 * sin + x2 * cos], axis=1)


def _ret_kernel(*refs, heads, dk, dv, c, dec_dims):
    q_ref, k_ref, v_ref, gb_ref, cos_ref, sin_ref, lg_ref, nw_ref = refs[:8]
    dec_in = refs[8:15]
    o_ref, s_out_ref = refs[15:17]
    dec_out = refs[17:19]
    s_ref, dmat_ref, qdec_ref, kdec_ref = refs[19:23]
    dec_scratch = refs[23:]
    t = pl.program_id(1)
    step = pl.program_id(0) * pl.num_programs(1) + t
    last_step = pl.num_programs(0) * pl.num_programs(1) - 1

    _gla_decode_section(step, last_step, dec_in + dec_out, dec_scratch,
                        heads=dec_dims[0], dk=dec_dims[1], dv=dec_dims[2])

    @pl.when(t == 0)
    def _():
        s_ref[...] = jnp.zeros_like(s_ref)
        ri = lax.broadcasted_iota(jnp.int32, (c, c), 0)
        rj = lax.broadcasted_iota(jnp.int32, (c, c), 1)
        dist = (ri - rj).astype(F32)
        rowl = lax.broadcasted_iota(jnp.int32, (c, V7X_LANES), 0).astype(F32)
        for hh in range(heads):
            lg = lg_ref[hh]
            dmat_ref[hh] = jnp.exp(jnp.where(ri >= rj, dist * lg[:, :1], -jnp.inf))
            qdec_ref[hh] = jnp.exp((rowl + 1.0) * lg)
            kdec_ref[hh] = jnp.exp((float(c - 1) - rowl) * lg)

    ct = q_ref.shape[0]
    kcs = [slice(hh * dk, (hh + 1) * dk) for hh in range(heads)]
    vcs = [slice(hh * dv, (hh + 1) * dv) for hh in range(heads)]

    def chunk(ci, carry):
        rows = pl.ds(pl.multiple_of(ci * c, c), c)
        cos, sin = cos_ref[rows, :], sin_ref[rows, :]
        qrs = [_rotary(q_ref[rows, kc].astype(F32), cos, sin).astype(BF16) for kc in kcs]
        krs = [_rotary(k_ref[rows, kc].astype(F32), cos, sin) * (dk ** -0.5) for kc in kcs]
        scores = [_dot_nt(qrs[hh], krs[hh].astype(BF16)) * dmat_ref[hh] for hh in range(heads)]
        outs = []
        for hh in range(heads):
            qdec = jnp.concatenate([qdec_ref[hh]] * (dv // V7X_LANES), axis=1)
            o = qdec * _dot(qrs[hh], s_ref[hh].astype(BF16))
            outs.append(o + _dot(scores[hh].astype(BF16), v_ref[rows, vcs[hh]]))
        for hh in range(heads):
            kdec = jnp.concatenate([kdec_ref[hh]] * (dk // V7X_LANES), axis=1)
            k_tail = (krs[hh] * kdec).astype(BF16)
            lgv = jnp.concatenate([lg_ref[hh]] * (dv // V7X_LANES), axis=1)
            s_ref[hh] = jnp.exp(float(c) * lgv) * s_ref[hh] + _dot_tn(k_tail, v_ref[rows, vcs[hh]])
        for hh in range(heads):
            _ln_gate_store(outs[hh], nw_ref[...], gb_ref[rows, vcs[hh]].astype(F32), o_ref, rows, vcs[hh])
        return carry

    lax.fori_loop(0, ct // c, chunk, 0)

    @pl.when(t == pl.num_programs(1) - 1)
    def _():
        s_out_ref[0] = s_ref[...]


def _ret_prompt_gla_decode(projs, cos, sin, log_gamma, ret_norm_w, projs_dec, log2_decay_dec, gla_norm_w,
                           gla_state_dec, lay, batch, seq):
    heads, dk, dv = lay["ret_heads"], lay["ret_dk"], lay["ret_dv"]
    qk, vw = heads * dk, heads * dv
    g_heads, g_dk, g_dv = lay["gla_heads"], lay["gla_dk"], lay["gla_dv"]
    c = min(seq, RET_CHUNK)
    ct = min(seq, RET_STEP_CHUNKS * c)
    assert seq % ct == 0 and ct % c == 0
    nt = seq // ct
    half = dk // 2
    n_dec = projs_dec[0].shape[0]
    tps = _decode_plan(n_dec, batch * nt)
    row = lambda b, t: b * nt + t
    dec_rows, dec_state, dec_o, dec_scratch = _decode_specs("gla", lay, n_dec, tps, row)
    kern = functools.partial(_ret_kernel, heads=heads, dk=dk, dv=dv, c=c, dec_dims=(g_heads, g_dk, g_dv))
    return pl.pallas_call(
        kern,
        grid=(batch, nt),
        in_specs=[
            pl.BlockSpec((ct, qk), lambda b, t: (row(b, t), _segment_block(lay, "qb", qk))),
            pl.BlockSpec((ct, qk), lambda b, t: (row(b, t), _segment_block(lay, "kb", qk))),
            pl.BlockSpec((ct, vw), lambda b, t: (row(b, t), _segment_block(lay, "vb", vw))),
            pl.BlockSpec((ct, vw), lambda b, t: (row(b, t), _segment_block(lay, "gb", vw))),
            pl.BlockSpec((ct, half), lambda b, t: (t, 0)),
            pl.BlockSpec((ct, half), lambda b, t: (t, 0)),
            pl.BlockSpec((heads, 1, V7X_LANES), lambda b, t: (0, 0, 0)),
            pl.BlockSpec((1, dv), lambda b, t: (0, 0)),
        ] + dec_rows + [
            pl.BlockSpec((n_dec, g_heads * g_dk), lambda b, t: (0, 0)),
            pl.BlockSpec((1, g_dv), lambda b, t: (0, 0)),
            dec_state,
        ],
        out_specs=[
            pl.BlockSpec((ct, vw), lambda b, t: (row(b, t), 0)),
            pl.BlockSpec((1, heads, dk, dv), lambda b, t: (b, 0, 0, 0)),
            dec_o,
            dec_state,
        ],
        out_shape=[
            jax.ShapeDtypeStruct((batch * seq, vw), BF16),
            jax.ShapeDtypeStruct((batch, heads, dk, dv), F32),
            jax.ShapeDtypeStruct((n_dec, g_heads * g_dv), BF16),
            jax.ShapeDtypeStruct(gla_state_dec.shape, gla_state_dec.dtype),
        ],
        scratch_shapes=[
            pltpu.VMEM((heads, dk, dv), F32),
            pltpu.VMEM((heads, c, c), F32),
            pltpu.VMEM((heads, c, V7X_LANES), F32),
            pltpu.VMEM((heads, c, V7X_LANES), F32),
        ] + dec_scratch,
        compiler_params=_params(2, claim_all=True),
        name="ret_prompt_gla_decode",
    )(*_segment_arrays(lay, projs, ("qb", "kb", "vb", "gb")), cos, sin, log_gamma, ret_norm_w,
      *_segment_arrays(lay, projs_dec, ("qa", "ka", "va", "ga")), log2_decay_dec, gla_norm_w, gla_state_dec)


def _merge_kernel(oap_ref, obp_ref, oas_ref, obs_ref, wa_ref, wb_ref, g0p_ref, g1p_ref, g0s_ref, g1s_ref, wnext_ref,
                  mp_ref, ms_ref, wnext_bf_ref, wa_bf, wb_bf):
    wnext_bf_ref[...] = wnext_ref[...].astype(BF16)

    tn = wa_bf.shape[1]
    col_chunk = min(tn, 256)

    def merged(oa_ref, ob_ref, g0_ref, g1_ref, m_ref):
        oa, ob = oa_ref[...], ob_ref[...]
        for c in range(tn // col_chunk):
            cs = slice(c * col_chunk, (c + 1) * col_chunk)
            ya = _dot(oa, wa_bf[:, cs])
            yb = _dot(ob, wb_bf[:, cs])
            m = _sigmoid(g0_ref[:, cs].astype(F32)) * ya + _sigmoid(g1_ref[:, cs].astype(F32)) * yb
            m_ref[:, cs] = m.astype(m_ref.dtype)

    @pl.when(pl.program_id(1) == 0)
    def _():
        wa_bf[...] = wa_ref[...].astype(BF16)
        wb_bf[...] = wb_ref[...].astype(BF16)
        merged(oas_ref, obs_ref, g0s_ref, g1s_ref, ms_ref)

    merged(oap_ref, obp_ref, g0p_ref, g1p_ref, mp_ref)


def _slab_specs(w_next, n_steps, step_of):
    kn, dn = w_next.shape
    assert kn % n_steps == 0 and (kn // n_steps) % 16 == 0, (kn, n_steps)
    slab = kn // n_steps
    spec = pl.BlockSpec((slab, dn), lambda j, i: (step_of(j, i), 0))
    return spec, spec, jax.ShapeDtypeStruct((kn, dn), BF16), _nbytes((slab, dn), F32) + _nbytes((slab, dn), BF16)


def _merge(oa_p, ob_p, oa_s, ob_s, wa, wb, projs_p, projs_s, lay, w_next):
    proj_p, proj_s = projs_p[lay["cols"]["mg"][0]], projs_s[lay["cols"]["mg"][0]]
    m_p, ka = oa_p.shape
    kb = ob_p.shape[1]
    tail = oa_s.shape[0]
    d = wa.shape[1]
    tm = _row_tile(m_p, ROW_TILE)
    tn = min(d, 1024)
    mg = lay["cols"]["mg"][1]
    assert d % tn == 0 and mg % tn == 0
    g0 = mg // tn
    g1 = (mg + d) // tn
    n_m = m_p // tm
    slab_in, slab_out, slab_shape, slab_bytes = _slab_specs(w_next, (d // tn) * n_m, lambda j, i: j * n_m + i)
    return pl.pallas_call(
        _merge_kernel,
        grid=(d // tn, n_m),
        in_specs=[
            pl.BlockSpec((tm, ka), lambda j, i: (i, 0)),
            pl.BlockSpec((tm, kb), lambda j, i: (i, 0)),
            pl.BlockSpec((tail, ka), lambda j, i: (0, 0)),
            pl.BlockSpec((tail, kb), lambda j, i: (0, 0)),
            pl.BlockSpec((ka, tn), lambda j, i: (0, j)),
            pl.BlockSpec((kb, tn), lambda j, i: (0, j)),
            pl.BlockSpec((tm, tn), lambda j, i: (i, g0 + j)),
            pl.BlockSpec((tm, tn), lambda j, i: (i, g1 + j)),
            pl.BlockSpec((tail, tn), lambda j, i: (0, g0 + j)),
            pl.BlockSpec((tail, tn), lambda j, i: (0, g1 + j)),
            slab_in,
        ],
        out_specs=[pl.BlockSpec((tm, tn), lambda j, i: (i, j)), pl.BlockSpec((tail, tn), lambda j, i: (0, j)),
                   slab_out],
        out_shape=[jax.ShapeDtypeStruct((m_p, d), BF16), jax.ShapeDtypeStruct((tail, d), BF16), slab_shape],
        scratch_shapes=[pltpu.VMEM((ka, tn), BF16), pltpu.VMEM((kb, tn), BF16)],
        compiler_params=_params(
            2, _nbytes((tm, ka), BF16), _nbytes((tm, kb), BF16), _nbytes((tail, ka), BF16), _nbytes((tail, kb), BF16),
            _nbytes((ka, tn), F32), _nbytes((kb, tn), F32), 3 * _nbytes((tm, tn), BF16), 3 * _nbytes((tail, tn), BF16),
            slab_bytes,
            scratch_bytes=_nbytes((ka, tn), BF16) + _nbytes((kb, tn), BF16) + 3 * _nbytes((tm, tn), F32)),
        name="merge",
    )(oa_p, ob_p, oa_s, ob_s, wa, wb, proj_p, proj_p, proj_s, proj_s, w_next)


def _proj_res_norm_kernel(*refs, emit_sum, group, n_blocks):
    ap_refs, as_refs, w_refs = refs[:group], refs[group:2 * group], refs[2 * group:3 * group]
    resp_ref, ress_ref, nw_ref = refs[3 * group:3 * group + 3]
    out_refs = refs[3 * group + 3:]
    n_out = 2 if emit_sum else 1
    outs_p, outs_s = out_refs[:n_out], out_refs[n_out:]
    i = pl.program_id(0)
    k = pl.program_id(1)
    last_k = k == pl.num_programs(1) - 1
    d = w_refs[0].shape[1]
    col_chunk = min(d, 512)
    rest = n_blocks % group

    def step(a_refs, res_ref, outs):
        acc_ref = outs[0]
        nrow = acc_ref.shape[0]
        row_chunk = min(nrow, 128)
        assert nrow % row_chunk == 0

        @pl.when(k == 0)
        def _():
            acc_ref[...] = res_ref[...]

        def accumulate(n_used):
            a = [a_refs[s][...] for s in range(n_used)]
            for c in range(d // col_chunk):
                cs = slice(c * col_chunk, (c + 1) * col_chunk)
                part = _dot(a[0], w_refs[0][:, cs])
                for s in range(1, n_used):
                    part = part + _dot(a[s], w_refs[s][:, cs])
                acc_ref[:, cs] += part

        if rest == 0:
            accumulate(group)
        else:
            pl.when(jnp.logical_not(last_k))(lambda: accumulate(group))
            pl.when(last_k)(lambda: accumulate(rest))

        @pl.when(last_k)
        def _():
            def body(c, carry):
                rr = pl.ds(pl.multiple_of(c * row_chunk, row_chunk), row_chunk)
                y = _rmsnorm_rows(acc_ref[rr, :], nw_ref[...])
                if emit_sum:
                    outs[1][rr, :] = y.astype(outs[1].dtype)
                else:
                    acc_ref[rr, :] = y
                return carry

            lax.fori_loop(0, nrow // row_chunk, body, 0)

    step(ap_refs, resp_ref, outs_p)

    @pl.when(i == pl.num_programs(0) - 1)
    def _():
        step(as_refs, ress_ref, outs_s)


def _proj_res_norm(a_p, a_s, w, res_p, res_s, norm_w, emit_sum, group):
    m_p, kdim = a_p.shape
    tail = a_s.shape[0]
    d = w.shape[1]
    tm = _row_tile(m_p, ROW_TILE)
    tk = min(kdim, K_BLOCK)
    assert kdim % tk == 0
    n_blocks = kdim // tk
    group = min(group, n_blocks)
    n_steps = pl.cdiv(n_blocks, group)
    blk = lambda k, s: jnp.minimum(k * group + s, n_blocks - 1)
    p_spec = pl.BlockSpec((tm, d), lambda i, k: (i, 0))
    s_spec = pl.BlockSpec((tail, d), lambda i, k: (0, 0))
    s3_spec = pl.BlockSpec((tail, None, d), lambda i, k: (0, 0, 0))
    res_s_spec = s3_spec if res_s.ndim == 3 else s_spec
    out_specs = [p_spec, s3_spec]
    out_shape = [jax.ShapeDtypeStruct((m_p, d), F32), jax.ShapeDtypeStruct((tail, 1, d), F32)]
    assert w.dtype == BF16
    blocks = [group * _nbytes((tm, tk), BF16), group * _nbytes((tail, tk), BF16), group * _nbytes((tk, d), BF16),
              2 * _nbytes((tm, d), F32), 2 * _nbytes((tail, d), F32)]
    if emit_sum:
        out_specs = [p_spec, p_spec, s_spec, s_spec]
        out_shape = [out_shape[0], jax.ShapeDtypeStruct((m_p, d), BF16),
                     jax.ShapeDtypeStruct((tail, d), F32), jax.ShapeDtypeStruct((tail, d), BF16)]
        blocks += [_nbytes((tm, d), BF16), _nbytes((tail, d), BF16)]
    in_specs = (
        [pl.BlockSpec((tm, tk), lambda i, k, s=s: (i, blk(k, s))) for s in range(group)]
        + [pl.BlockSpec((tail, tk), lambda i, k, s=s: (0, blk(k, s))) for s in range(group)]
        + [pl.BlockSpec((tk, d), lambda i, k, s=s: (blk(k, s), 0)) for s in range(group)]
        + [p_spec, res_s_spec, pl.BlockSpec((1, d), lambda i, k: (0, 0))])
    return pl.pallas_call(
        functools.partial(_proj_res_norm_kernel, emit_sum=emit_sum, group=group, n_blocks=n_blocks),
        grid=(m_p // tm, n_steps),
        in_specs=in_specs,
        out_specs=out_specs,
        out_shape=out_shape,
        compiler_params=_params(2, *blocks),
        name="proj_res_norm",
    )(*([a_p] * group + [a_s] * group + [w] * group), res_p, res_s, norm_w.reshape(1, d))


def _swiglu_kernel(hp_ref, hs_ref, wg_ref, wu_ref, wnext_ref, op_ref, os_ref, wnext_bf_ref, wg_bf, wu_bf):
    wnext_bf_ref[...] = wnext_ref[...].astype(BF16)

    tn = wg_bf.shape[1]
    col_chunk = min(tn, 256)

    def act(h_ref, o_ref):
        h = h_ref[...]
        for c in range(tn // col_chunk):
            cs = slice(c * col_chunk, (c + 1) * col_chunk)
            a = _dot(h, wg_bf[:, cs])
            b = _dot(h, wu_bf[:, cs])
            o_ref[:, cs] = (_silu(a) * b).astype(o_ref.dtype)

    @pl.when(pl.program_id(1) == 0)
    def _():
        wg_bf[...] = wg_ref[...].astype(BF16)
        wu_bf[...] = wu_ref[...].astype(BF16)
        act(hs_ref, os_ref)

    act(hp_ref, op_ref)


def _swiglu(h_p, h_s, wg, wu, w_next):
    m_p, d = h_p.shape
    tail = h_s.shape[0]
    f = wg.shape[1]
    tm = _row_tile(m_p, ROW_TILE_WIDE)
    tn = 512 if f % 512 == 0 else 256
    assert f % tn == 0
    n_m = m_p // tm
    slab_in, slab_out, slab_shape, slab_bytes = _slab_specs(w_next, (f // tn) * n_m, lambda j, i: j * n_m + i)
    return pl.pallas_call(
        _swiglu_kernel,
        grid=(f // tn, n_m),
        in_specs=[
            pl.BlockSpec((tm, d), lambda j, i: (i, 0)),
            pl.BlockSpec((tail, d), lambda j, i: (0, 0)),
            pl.BlockSpec((d, tn), lambda j, i: (0, j)),
            pl.BlockSpec((d, tn), lambda j, i: (0, j)),
            slab_in,
        ],
        out_specs=[pl.BlockSpec((tm, tn), lambda j, i: (i, j)), pl.BlockSpec((tail, tn), lambda j, i: (0, j)),
                   slab_out],
        out_shape=[jax.ShapeDtypeStruct((m_p, f), BF16), jax.ShapeDtypeStruct((tail, f), BF16), slab_shape],
        scratch_shapes=[pltpu.VMEM((d, tn), BF16), pltpu.VMEM((d, tn), BF16)],
        compiler_params=_params(
            2, _nbytes((tm, d), BF16), _nbytes((tail, d), BF16), 2 * _nbytes((d, tn), F32),
            _nbytes((tm, tn), BF16), _nbytes((tail, tn), BF16), slab_bytes,
            scratch_bytes=2 * _nbytes((d, tn), BF16) + 3 * _nbytes((tm, tn), F32)),
        name="swiglu",
    )(h_p, h_s, wg, wu, w_next)


def _layout(d_model, in_width, state_gla, state_ret, gate_rank):
    _, _, gh, gdk, gdv = state_gla.shape
    _, _, rh, rdk, rdv = state_ret.shape
    gqk, gv, rqk, rv = gh * gdk, gh * gdv, rh * rdk, rh * rdv
    lay = dict(gla_heads=gh, gla_dk=gdk, gla_dv=gdv, ret_heads=rh, ret_dk=rdk, ret_dv=rdv, rank=gate_rank)
    off = 0
    for name, width in (("qa", gqk), ("ka", gqk), ("va", gv), ("ga", gv), ("qb", rqk), ("kb", rqk),
                        ("vb", rv), ("gb", rv), ("mg", 2 * d_model)):
        lay[name] = off
        off += width
    lay["out_cols"] = off
    lay["plain_cols"] = 2 * gqk + gv
    lay["gd_src"] = lay["plain_cols"]
    assert lay["gd_src"] % V7X_LANES == 0 and gate_rank <= V7X_LANES
    assert in_width == off + gate_rank
    lay["widths"] = dict(qa=gqk, ka=gqk, va=gv, ga=gv, qb=rqk, kb=rqk, vb=rv, gb=rv, mg=2 * d_model)
    return lay


def _split_columns(lay, first_cols):
    cols = {}
    for name, width in lay["widths"].items():
        off = lay[name]
        assert off + width <= first_cols or off >= first_cols, "a segment straddles the two arrays"
        cols[name] = (0, off) if off < first_cols else (1, off - first_cols)
    return cols


def _layer(x_p, x_s, st_gla, st_ret, wts, lay, log_gamma, final_norm):
    (norm_mix, w_in, w_gate_up, b_gate, gla_norm_w, w_gla_up, ret_norm_w, w_ret_up, w_out, norm_ffn,
     w_ffn_gate, w_ffn_up, w_ffn_down) = wts
    batch, seq, d = x_p.shape
    rank = lay["rank"]
    gqk = lay["gla_heads"] * lay["gla_dk"]
    bup = b_gate.reshape(1, gqk)
    gnw = gla_norm_w.reshape(1, -1)
    rnw = ret_norm_w.reshape(1, -1)
    tn = 1024 if (lay["out_cols"] % 1024 == 0 and lay["plain_cols"] % 1024 == 0) else 512
    xp = x_p.reshape(batch * seq, d)
    assert x_s.ndim == 3 and x_s.shape[1] == 1, "one new token per decode sequence"
    xs = x_s
    w_in_t = w_in.T

    h_p, h_s, g_p, g_s, cos_p, sin_p, cos_s, sin_s, first_p, first_s = _rmsnorm_gate(
        xp, xs, norm_mix, w_in_t, lay["gd_src"], w_gate_up, bup, seq, lay["ret_dk"] // 2, tn)
    rest_p, rest_s = _in_proj(h_p, h_s, w_in_t, lay["plain_cols"], rank, lay["out_cols"], tn, 1)
    projs_p, projs_s = (first_p, rest_p), (first_s, rest_s)
    lay = dict(lay, cols=_split_columns(lay, tn))
    oa_p, sa_p, ob_s, sb_s = _gla_prompt_ret_decode(
        projs_p, g_p, gnw, projs_s, cos_s, sin_s, log_gamma, rnw, st_ret, lay, batch, seq)
    ob_p, sb_p, oa_s, sa_s = _ret_prompt_gla_decode(
        projs_p, cos_p, sin_p, log_gamma, rnw, projs_s, g_s, gnw, st_gla, lay, batch, seq)
    m_p, m_s, w_out_bf = _merge(oa_p, ob_p, oa_s, ob_s, w_gla_up, w_ret_up, projs_p, projs_s, lay, w_out)
    x1_p, h2_p, x1_s, h2_s = _proj_res_norm(m_p, m_s, w_out_bf, xp, xs, norm_ffn, True, 2)
    act_p, act_s, w_down_bf = _swiglu(h2_p, h2_s, w_ffn_gate, w_ffn_up, w_ffn_down)
    y_p, y_s = _proj_res_norm(act_p, act_s, w_down_bf, x1_p, x1_s, final_norm, False, 3)
    return (y_p, sa_p, sb_p), (y_s, sa_s, sb_s)


def kernel(x_prompt, x_sample, state_gla, state_ret, norm_mix, w_in, w_gla_gate_up, b_gla_gate, gla_norm_w,
           w_gla_up, ret_norm_w, w_ret_up, w_out, norm_ffn, w_ffn_gate, w_ffn_up, w_ffn_down, norm_final):
    depth = w_in.shape[0]
    assert depth == 1, "single-layer trunk"
    batch, seq, d = x_prompt.shape
    lay = _layout(d, w_in.shape[-1], state_gla, state_ret, w_gla_gate_up.shape[1])
    rh, rdk = lay["ret_heads"], lay["ret_dk"]
    assert rdk // 2 == V7X_LANES
    lg = jnp.log1p(-jnp.exp(jnp.linspace(math.log(1.0 / 32), math.log(1.0 / 512), rh))).astype(F32)
    log_gamma = jnp.broadcast_to(lg[:, None, None], (rh, 1, V7X_LANES))

    wts = (norm_mix[0], w_in[0], w_gla_gate_up[0], b_gla_gate[0], gla_norm_w[0], w_gla_up[0], ret_norm_w[0],
           w_ret_up[0], w_out[0], norm_ffn[0], w_ffn_gate[0], w_ffn_up[0], w_ffn_down[0])
    (y_p, ga_p, re_p), (y_s, ga_s, re_s) = _layer(
        x_prompt, x_sample, state_gla[0], state_ret[0], wts, lay, log_gamma, norm_final)

    sd = state_gla.dtype
    return (y_p.reshape(batch, seq, d), y_s.reshape(x_sample.shape),
            ga_p[None].astype(sd), re_p[None].astype(state_ret.dtype),
            ga_s[None].astype(sd), re_s[None].astype(state_ret.dtype))
```

```python
import functools
import math

import numpy as np
import jax
import jax.numpy as jnp
from jax import lax
from jax.experimental import pallas as pl
from jax.experimental.pallas import tpu as pltpu

EPS = 1e-6
ROPE_BASE = 10000.0
GLA_GATE_NORM = 16.0
PAST_LEN = 16384

V7X_LANES = 128
V7X_VMEM_REQUEST_CAP = 60000 * 1024
COMPILER_SCRATCH_BYTES = 12 * 1024 * 1024

GLA_CHUNK = 64
GLA_STEP_CHUNKS = 8
LOG2_E = 1.4426950408889634
RET_CHUNK = 128
RET_STEP_CHUNKS = 4
ROW_TILE = 1024
ROW_TILE_WIDE = 2048
K_BLOCK = 512

BF16 = jnp.bfloat16
F32 = jnp.float32


def _params(n_axes, *block_bytes, scratch_bytes=0, claim_all=False):
    need = 2 * sum(block_bytes) + scratch_bytes + COMPILER_SCRATCH_BYTES
    if claim_all:
        need = V7X_VMEM_REQUEST_CAP
    return pltpu.CompilerParams(
        dimension_semantics=("arbitrary",) * n_axes,
        vmem_limit_bytes=int(min(V7X_VMEM_REQUEST_CAP, need)),
    )


def _nbytes(shape, dtype):
    return int(np.prod(shape)) * jnp.dtype(dtype).itemsize


def _sigmoid(x):
    return 1.0 / (1.0 + jnp.exp(-x))


def _silu(x):
    return x * _sigmoid(x)


def _log_sigmoid(x):
    return jnp.minimum(x, 0.0) - jnp.log(1.0 + jnp.exp(-jnp.abs(x)))


def _dot(a, b):
    return jnp.dot(a, b, preferred_element_type=F32)


def _dot_nt(a, b):
    return lax.dot_general(a, b, (((1,), (1,)), ((), ())), preferred_element_type=F32)


def _dot_tn(a, b):
    return lax.dot_general(a, b, (((0,), (0,)), ((), ())), preferred_element_type=F32)


def _row_tile(m, want):
    t = min(m, want)
    assert m % t == 0, (m, t)
    return t


def _rmsnorm_rows(x, w):
    ms = jnp.mean(x * x, axis=-1, keepdims=True)
    return x * lax.rsqrt(ms + EPS) * w


def _rope_rows(cos_ref, sin_ref, pos0):
    rows, half = cos_ref.shape
    pos = (lax.broadcasted_iota(jnp.int32, (rows, half), 0) + pos0).astype(F32)
    idx = lax.broadcasted_iota(jnp.int32, (rows, half), 1).astype(F32)
    ang = pos * jnp.exp(idx * (-math.log(ROPE_BASE) / half))
    cos_ref[...] = jnp.cos(ang)
    sin_ref[...] = jnp.sin(ang)


def _rmsnorm_kernel(xp_ref, xs_ref, w_ref, wgd_ref, wup_ref, bup_ref, w0_ref,
                    hp_ref, hs_ref, gp_ref, gs_ref, cosp_ref, sinp_ref, cosd_ref, sind_ref, p0p_ref, p0s_ref,
                    wgd_bf, wup_bf, w0_bf):
    i = pl.program_id(0)

    @pl.when(i == 0)
    def _():
        wgd_bf[...] = wgd_ref[...].astype(BF16)
        wup_bf[...] = jnp.zeros_like(wup_bf)
        wup_bf[0:wup_ref.shape[0], :] = wup_ref[...].astype(BF16)
        w0_bf[...] = w0_ref[...].astype(BF16)

    def rows(x, h_ref, g_ref, p0_ref, cos_ref, sin_ref, pos0):
        h = _rmsnorm_rows(x, w_ref[...]).astype(h_ref.dtype)
        h_ref[...] = h
        gd = _dot_nt(h, wgd_bf[...])
        x = _dot(gd.astype(BF16), wup_bf[...]) + bup_ref[...]
        g_ref[...] = _log_sigmoid(x) * (LOG2_E / GLA_GATE_NORM)
        _rope_rows(cos_ref, sin_ref, pos0)
        p0_ref[...] = _dot_nt(h, w0_bf[...]).astype(p0_ref.dtype)

    rows(xp_ref[...], hp_ref, gp_ref, p0p_ref, cosp_ref, sinp_ref, i * cosp_ref.shape[0])

    @pl.when(i == 0)
    def _():
        rows(xs_ref[...], hs_ref, gs_ref, p0s_ref, cosd_ref, sind_ref, PAST_LEN)


def _rmsnorm_gate(x_p, x_s, w, w_in_t, gate_row0, w_gate_up, bup, seq, half, tn0):
    m_p, d = x_p.shape
    tail = x_s.shape[0]
    rank, gw = w_gate_up.shape
    tm = _row_tile(m_p, ROW_TILE // 2)
    n_steps = m_p // tm
    assert gate_row0 % V7X_LANES == 0 and seq % n_steps == 0 and rank % 16 == 0
    pos_rows = seq // n_steps
    table = pl.BlockSpec((pos_rows, half), lambda i: (i, 0))
    table_dec = pl.BlockSpec((8, half), lambda i: (0, 0))
    return pl.pallas_call(
        _rmsnorm_kernel,
        grid=(n_steps,),
        in_specs=[
            pl.BlockSpec((tm, d), lambda i: (i, 0)),
            pl.BlockSpec((tail, None, d), lambda i: (0, 0, 0)),
            pl.BlockSpec((1, d), lambda i: (0, 0)),
            pl.BlockSpec((V7X_LANES, d), lambda i: (gate_row0 // V7X_LANES, 0)),
            pl.BlockSpec((rank, gw), lambda i: (0, 0)),
            pl.BlockSpec((1, gw), lambda i: (0, 0)),
            pl.BlockSpec((tn0, d), lambda i: (0, 0)),
        ],
        out_specs=[
            pl.BlockSpec((tm, d), lambda i: (i, 0)), pl.BlockSpec((tail, d), lambda i: (0, 0)),
            pl.BlockSpec((tm, gw), lambda i: (i, 0)), pl.BlockSpec((tail, gw), lambda i: (0, 0)),
            table, table, table_dec, table_dec,
            pl.BlockSpec((tm, tn0), lambda i: (i, 0)), pl.BlockSpec((tail, tn0), lambda i: (0, 0)),
        ],
        out_shape=[
            jax.ShapeDtypeStruct((m_p, d), BF16), jax.ShapeDtypeStruct((tail, d), BF16),
            jax.ShapeDtypeStruct((m_p, gw), F32), jax.ShapeDtypeStruct((tail, gw), F32),
            jax.ShapeDtypeStruct((seq, half), F32), jax.ShapeDtypeStruct((seq, half), F32),
            jax.ShapeDtypeStruct((8, half), F32), jax.ShapeDtypeStruct((8, half), F32),
            jax.ShapeDtypeStruct((m_p, tn0), BF16), jax.ShapeDtypeStruct((tail, tn0), BF16),
        ],
        scratch_shapes=[pltpu.VMEM((V7X_LANES, d), BF16), pltpu.VMEM((V7X_LANES, gw), BF16),
                        pltpu.VMEM((tn0, d), BF16)],
        compiler_params=_params(1, claim_all=True),
        name="rmsnorm_gate",
    )(x_p, x_s, w.reshape(1, d), w_in_t, w_gate_up, bup, w_in_t)


def _in_proj_kernel(hp_ref, hs_ref, wm_ref, wn_ref, op_ref, os_ref, wbf_ref, *, n_plain, shift, first_tile):
    j = pl.program_id(0) + first_tile
    i = pl.program_id(1)
    tn = wbf_ref.shape[0]

    @pl.when(jnp.logical_and(i == 0, j < n_plain))
    def _():
        wbf_ref[...] = wm_ref[...].astype(BF16)

    @pl.when(jnp.logical_and(i == 0, j >= n_plain))
    def _():
        wbf_ref[0:tn - shift, :] = wm_ref[shift:tn, :].astype(BF16)
        wbf_ref[tn - shift:tn, :] = wn_ref[...].astype(BF16)

    @pl.when(i == 0)
    def _():
        os_ref[...] = _dot_nt(hs_ref[...], wbf_ref[...]).astype(os_ref.dtype)

    op_ref[...] = _dot_nt(hp_ref[...], wbf_ref[...]).astype(op_ref.dtype)


def _in_proj(h_p, h_s, w_in_t, plain_cols, shift, out_cols, tn, first_tile):
    m_p, d = h_p.shape
    tail = h_s.shape[0]
    tm = _row_tile(m_p, ROW_TILE_WIDE)
    assert plain_cols % tn == 0 and out_cols % tn == 0 and tn % shift == 0 and shift % 8 == 0
    n_plain = plain_cols // tn
    n_tiles = out_cols // tn - first_tile
    kern = functools.partial(_in_proj_kernel, n_plain=n_plain, shift=shift, first_tile=first_tile)
    return pl.pallas_call(
        kern,
        grid=(n_tiles, m_p // tm),
        in_specs=[
            pl.BlockSpec((tm, d), lambda j, i: (i, 0)),
            pl.BlockSpec((tail, d), lambda j, i: (0, 0)),
            pl.BlockSpec((tn, d), lambda j, i: (j + first_tile, 0)),
            pl.BlockSpec((shift, d), lambda j, i: ((j + first_tile + 1) * (tn // shift), 0)),
        ],
        out_specs=[pl.BlockSpec((tm, tn), lambda j, i: (i, j)), pl.BlockSpec((tail, tn), lambda j, i: (0, j))],
        out_shape=[jax.ShapeDtypeStruct((m_p, n_tiles * tn), BF16), jax.ShapeDtypeStruct((tail, n_tiles * tn), BF16)],
        scratch_shapes=[pltpu.VMEM((tn, d), BF16)],
        compiler_params=_params(
            2, _nbytes((tm, d), BF16), _nbytes((tail, d), BF16), _nbytes((tn, d), F32), _nbytes((shift, d), F32),
            _nbytes((tm, tn), BF16), _nbytes((tail, tn), BF16),
            scratch_bytes=_nbytes((tn, d), BF16) + _nbytes((tm, tn), F32)),
        name="in_proj",
    )(h_p, h_s, w_in_t, w_in_t)


def _prefix_sum_rows(sel3_bf16, g):
    g0 = g.astype(BF16)
    r1 = g - g0.astype(F32)
    g1 = r1.astype(BF16)
    g2 = (r1 - g1.astype(F32)).astype(BF16)
    return _dot(sel3_bf16, jnp.concatenate([g0, g1, g2], axis=0))


def _lane_bcast_cols(row, n):
    parts = []
    for c in range(n // V7X_LANES):
        tile = jnp.broadcast_to(row[:, c * V7X_LANES:(c + 1) * V7X_LANES], (V7X_LANES, V7X_LANES))
        parts.append(tile.T)
    return parts[0] if len(parts) == 1 else jnp.concatenate(parts, axis=0)


def _rms_gate_store(o, w, gate, out_ref, rows, cols):
    ms = jnp.mean(o * o, axis=-1, keepdims=True)
    y = o * lax.rsqrt(ms + EPS) * w
    out_ref[rows, cols] = (y * _silu(gate)).astype(out_ref.dtype)


def _ln_gate_store(o, w, gate, out_ref, rows, cols):
    mu = jnp.mean(o, axis=-1, keepdims=True)
    dlt = o - mu
    var = jnp.mean(dlt * dlt, axis=-1, keepdims=True)
    y = dlt * lax.rsqrt(var + EPS) * w
    out_ref[rows, cols] = (y * _silu(gate)).astype(out_ref.dtype)


def _token_selectors(n_tok):
    assert 3 * n_tok <= V7X_LANES
    j = lax.broadcasted_iota(jnp.int32, (V7X_LANES, V7X_LANES), 0)
    sel = []
    for t in range(n_tok):
        hit = jnp.logical_or(j == t, jnp.logical_or(j == n_tok + t, j == 2 * n_tok + t))
        sel.append(jnp.where(hit, 1.0, 0.0).astype(BF16))
    return jnp.stack(sel, axis=0)


def _column_source(x):
    n_tok, w = x.shape
    hi = x.astype(BF16).astype(F32)
    r1 = x - hi
    mid = r1.astype(BF16).astype(F32)
    lo = (r1 - mid).astype(BF16).astype(F32)
    x3 = jnp.concatenate([hi, mid, lo, jnp.zeros((V7X_LANES - 3 * n_tok, w), F32)], axis=0)
    parts = [x3[:, c * V7X_LANES:(c + 1) * V7X_LANES].T for c in range(w // V7X_LANES)]
    return (parts[0] if len(parts) == 1 else jnp.concatenate(parts, axis=0)).astype(BF16)


def _decode_advance(tok0, decay_rows_fn, decay_const_fn, k_ref, q_ref, v_ref, s_in_ref, s_out_ref, o_ref, sel_ref,
                    *, heads, dk, dv):
    n_tok = s_in_ref.shape[0]
    reps = dv // V7X_LANES
    rows = pl.ds(pl.multiple_of(tok0, n_tok), n_tok)

    def cols(src, tt):
        return jnp.concatenate([_dot(src, sel_ref[tt])] * reps, axis=1)

    for hh in range(heads):
        kc = slice(hh * dk, (hh + 1) * dk)
        vc = slice(hh * dv, (hh + 1) * dv)
        k_src, q_src = _column_source(k_ref[rows, kc]), _column_source(q_ref[rows, kc])
        a_src = None if decay_rows_fn is None else _column_source(decay_rows_fn(rows, hh))
        v = v_ref[rows, vc]
        o_rows = []
        for tt in range(n_tok):
            decay = decay_const_fn(hh) if a_src is None else cols(a_src, tt)
            s_new = decay * s_in_ref[tt, hh] + cols(k_src, tt) * v[tt:tt + 1, :]
            s_out_ref[tt, hh] = s_new
            o_rows.append(jnp.sum(cols(q_src, tt) * s_new, axis=0, keepdims=True))
        o_ref[rows, vc] = jnp.concatenate(o_rows, axis=0)


def _decode_plan(n_dec, n_steps):
    assert n_dec % n_steps == 0 and (n_dec // n_steps) % 8 == 0, (n_dec, n_steps)
    return n_dec // n_steps


def _gla_decode_section(step, last_step, refs, scratch, *, heads, dk, dv):
    qd_ref, kd_ref, vd_ref, gad_ref, gd_ref, nw_ref, sd_in_ref, od_ref, sd_out_ref = refs
    a_dec, q_dec, k_dec, v_dec, o_dec, sel_ref = scratch

    @pl.when(step == 0)
    def _():
        a_dec[...] = jnp.exp2(gd_ref[...])
        q_dec[...] = qd_ref[...].astype(F32) * (dk ** -0.5)
        k_dec[...] = kd_ref[...].astype(F32)
        v_dec[...] = vd_ref[...].astype(F32)
        sel_ref[...] = _token_selectors(sd_in_ref.shape[0])

    _decode_advance(step * sd_in_ref.shape[0], lambda rows, hh: a_dec[rows, hh * dk:(hh + 1) * dk], None,
                    k_dec, q_dec, v_dec, sd_in_ref, sd_out_ref, o_dec, sel_ref, heads=heads, dk=dk, dv=dv)

    @pl.when(step == last_step)
    def _():
        n_dec = o_dec.shape[0]
        for hh in range(heads):
            vc = slice(hh * dv, (hh + 1) * dv)
            _rms_gate_store(o_dec[:, vc], nw_ref[...], gad_ref[:, vc].astype(F32), od_ref, slice(0, n_dec), vc)


def _ret_decode_section(step, last_step, refs, scratch, *, heads, dk, dv):
    qd_ref, kd_ref, vd_ref, gbd_ref, cosd_ref, sind_ref, lg_ref, nw_ref, sd_in_ref, od_ref, sd_out_ref = refs
    q_dec, k_dec, v_dec, o_dec, sel_ref = scratch

    @pl.when(step == 0)
    def _():
        cosd, sind = cosd_ref[0:1, :], sind_ref[0:1, :]
        for hh in range(heads):
            kc = slice(hh * dk, (hh + 1) * dk)
            q_dec[:, kc] = _rotary(qd_ref[:, kc].astype(F32), cosd, sind)
            k_dec[:, kc] = _rotary(kd_ref[:, kc].astype(F32), cosd, sind) * (dk ** -0.5)
        v_dec[...] = vd_ref[...].astype(F32)
        sel_ref[...] = _token_selectors(sd_in_ref.shape[0])

    def gamma(hh):
        return jnp.exp(jnp.concatenate([lg_ref[hh]] * (dv // V7X_LANES), axis=1))

    _decode_advance(step * sd_in_ref.shape[0], None, gamma, k_dec, q_dec, v_dec, sd_in_ref, sd_out_ref, o_dec,
                    sel_ref, heads=heads, dk=dk, dv=dv)

    @pl.when(step == last_step)
    def _():
        n_dec = o_dec.shape[0]
        for hh in range(heads):
            vc = slice(hh * dv, (hh + 1) * dv)
            _ln_gate_store(o_dec[:, vc], nw_ref[...], gbd_ref[:, vc].astype(F32), od_ref, slice(0, n_dec), vc)


def _segment_block(lay, name, width):
    offset = lay["cols"][name][1]
    assert offset % width == 0
    return offset // width


def _segment_arrays(lay, arrays, names):
    return tuple(arrays[lay["cols"][n][0]] for n in names)


def _decode_specs(kind, lay, n_dec, tps, row):
    heads, dk, dv = lay[kind + "_heads"], lay[kind + "_dk"], lay[kind + "_dv"]
    qk, vw = heads * dk, heads * dv
    names = ("qa", "ka", "va", "ga") if kind == "gla" else ("qb", "kb", "vb", "gb")
    widths = (qk, qk, vw, vw)
    rows_in = [pl.BlockSpec((n_dec, w), lambda b, t, c=_segment_block(lay, n, w): (0, c))
               for n, w in zip(names, widths)]
    state = pl.BlockSpec((tps, heads, dk, dv), lambda b, t: (row(b, t), 0, 0, 0))
    o_spec = pl.BlockSpec((n_dec, vw), lambda b, t: (0, 0))
    n_qk = 3 if kind == "gla" else 2
    scratch = ([pltpu.VMEM((n_dec, qk), F32)] * n_qk + [pltpu.VMEM((n_dec, vw), F32)] * 2
               + [pltpu.VMEM((tps, V7X_LANES, V7X_LANES), BF16)])
    return rows_in, state, o_spec, scratch


def _gla_sum_matrices(c):
    levels = c.bit_length() - 1
    assert 1 << levels == c
    i = lax.broadcasted_iota(jnp.int32, (c, c), 0)
    j = lax.broadcasted_iota(jnp.int32, (c, c), 1)
    mats = [j <= i]
    for l in range(levels):
        ref = jnp.bitwise_or(jnp.bitwise_and(i, -(2 << l)), 1 << l)
        mats.append(jnp.logical_and(j > jnp.minimum(i, ref), j <= jnp.maximum(i, ref)))
    mats.append(j > i)
    sel = jnp.concatenate([jnp.where(m, 1.0, 0.0).astype(BF16) for m in mats], axis=0)
    return jnp.concatenate([sel, sel, sel], axis=1)


def _pair_level(c):
    levels = c.bit_length() - 1
    i = lax.broadcasted_iota(jnp.int32, (c, c), 0)
    j = lax.broadcasted_iota(jnp.int32, (c, c), 1)
    x = jnp.bitwise_xor(i, j)
    lvl = jnp.zeros((c, c), jnp.int32)
    for l in range(1, levels):
        lvl = lvl + jnp.where(x >= (1 << l), 1, 0)
    return jnp.where(i > j, lvl, jnp.where(i == j, levels, -1))


def _queries_else_keys(q, k, l):
    c = q.shape[0]
    span = 1 << l
    if span >= 8:
        parts = [(q if (b & 1) else k)[b * span:(b + 1) * span, :] for b in range(c // span)]
        return jnp.concatenate(parts, axis=0)
    row = lax.broadcasted_iota(jnp.int32, q.shape, 0)
    return jnp.where(jnp.bitwise_and(row, span) != 0, q, k)


def _gla_level_scores(q, k, sums):
    c = q.shape[0]
    levels = c.bit_length() - 1
    out = []
    for l in range(levels):
        x = _queries_else_keys(q, k, l) * jnp.exp2(sums[(1 + l) * c:(2 + l) * c, :])
        xb = x.astype(BF16)
        out.append(_dot_nt(xb, xb))
    return out


def _gla_chunk_out(q, k, v, sums, level_scores, pair_level, state):
    c = q.shape[0]
    levels = c.bit_length() - 1
    scores = jnp.where(pair_level == levels, jnp.sum(q * k, axis=-1, keepdims=True), 0.0)
    for l in range(levels):
        scores = jnp.where(pair_level == l, level_scores[l], scores)
    o = _dot((q * jnp.exp2(sums[0:c, :])).astype(BF16), state.astype(BF16))
    return o + _dot(scores.astype(BF16), v)


def _gla_next_state(k, v, sums, state):
    c, dk = k.shape
    levels = c.bit_length() - 1
    k_tail = (k * jnp.exp2(sums[(levels + 1) * c:(levels + 2) * c, :])).astype(BF16)
    decay = _lane_bcast_cols(jnp.exp2(sums[c - 1:c, :]), dk)
    decay_full = jnp.concatenate([decay] * (v.shape[1] // V7X_LANES), axis=1)
    return decay_full * state + _dot_tn(k_tail, v)


def _gla_kernel(*refs, heads, dk, dv, dec_dims):
    q_ref, k_ref, v_ref, ga_ref, g_ref, nw_ref = refs[:6]
    dec_in = refs[6:15]
    o_ref, s_out_ref = refs[15:17]
    dec_out = refs[17:19]
    s_ref, mats_ref, lvl_ref = refs[19:22]
    dec_scratch = refs[22:]
    t = pl.program_id(1)
    step = pl.program_id(0) * pl.num_programs(1) + t
    last_step = pl.num_programs(0) * pl.num_programs(1) - 1

    @pl.when(t == 0)
    def _():
        s_ref[...] = jnp.zeros_like(s_ref)
        mats_ref[...] = _gla_sum_matrices(GLA_CHUNK)
        lvl_ref[...] = _pair_level(GLA_CHUNK)

    _ret_decode_section(step, last_step, dec_in + dec_out, dec_scratch,
                        heads=dec_dims[0], dk=dec_dims[1], dv=dec_dims[2])

    ct = q_ref.shape[0]

    kcs = [slice(hh * dk, (hh + 1) * dk) for hh in range(heads)]
    vcs = [slice(hh * dv, (hh + 1) * dv) for hh in range(heads)]
    group = 2 if (ct // GLA_CHUNK) % 2 == 0 else 1

    def chunk_group(cg, carry):
        pair_level = lvl_ref[...]
        rows, sums, qs, ks, lvl_scores = [], [], [], [], []
        for u in range(group):
            r = pl.ds(pl.multiple_of((cg * group + u) * GLA_CHUNK, GLA_CHUNK), GLA_CHUNK)
            rows.append(r)
            sums.append(_prefix_sum_rows(mats_ref[...], g_ref[r, :]))
            qs.append([q_ref[r, kc].astype(F32) * (dk ** -0.5) for kc in kcs])
            ks.append([k_ref[r, kc].astype(F32) for kc in kcs])
            lvl_scores.append([_gla_level_scores(qs[u][hh], ks[u][hh], sums[u][:, kcs[hh]]) for hh in range(heads)])
        outs = []
        for u in range(group):
            r = rows[u]
            outs.append([_gla_chunk_out(qs[u][hh], ks[u][hh], v_ref[r, vcs[hh]], sums[u][:, kcs[hh]],
                                        lvl_scores[u][hh], pair_level, s_ref[hh]) for hh in range(heads)])
            for hh in range(heads):
                s_ref[hh] = _gla_next_state(ks[u][hh], v_ref[r, vcs[hh]], sums[u][:, kcs[hh]], s_ref[hh])
        for u in range(group):
            for hh in range(heads):
                _rms_gate_store(outs[u][hh], nw_ref[...], ga_ref[rows[u], vcs[hh]].astype(F32), o_ref, rows[u],
                                vcs[hh])
        return carry

    lax.fori_loop(0, ct // GLA_CHUNK // group, chunk_group, 0)

    @pl.when(t == pl.num_programs(1) - 1)
    def _():
        s_out_ref[0] = s_ref[...]


def _gla_prompt_ret_decode(projs, log2_decay, gla_norm_w, projs_dec, cos_dec, sin_dec, log_gamma, ret_norm_w,
                           ret_state_dec, lay, batch, seq):
    heads, dk, dv = lay["gla_heads"], lay["gla_dk"], lay["gla_dv"]
    qk, vw = heads * dk, heads * dv
    r_heads, r_dk, r_dv = lay["ret_heads"], lay["ret_dk"], lay["ret_dv"]
    ct = min(seq, GLA_STEP_CHUNKS * GLA_CHUNK)
    levels = GLA_CHUNK.bit_length() - 1
    assert seq % ct == 0 and ct % GLA_CHUNK == 0
    nt = seq // ct
    n_dec = projs_dec[0].shape[0]
    tps = _decode_plan(n_dec, batch * nt)
    row = lambda b, t: b * nt + t
    dec_rows, dec_state, dec_o, dec_scratch = _decode_specs("ret", lay, n_dec, tps, row)
    table_dec = pl.BlockSpec((cos_dec.shape[0], r_dk // 2), lambda b, t: (0, 0))
    kern = functools.partial(_gla_kernel, heads=heads, dk=dk, dv=dv, dec_dims=(r_heads, r_dk, r_dv))
    return pl.pallas_call(
        kern,
        grid=(batch, nt),
        in_specs=[
            pl.BlockSpec((ct, qk), lambda b, t: (row(b, t), _segment_block(lay, "qa", qk))),
            pl.BlockSpec((ct, qk), lambda b, t: (row(b, t), _segment_block(lay, "ka", qk))),
            pl.BlockSpec((ct, vw), lambda b, t: (row(b, t), _segment_block(lay, "va", vw))),
            pl.BlockSpec((ct, vw), lambda b, t: (row(b, t), _segment_block(lay, "ga", vw))),
            pl.BlockSpec((ct, qk), lambda b, t: (row(b, t), 0)),
            pl.BlockSpec((1, dv), lambda b, t: (0, 0)),
        ] + dec_rows + [
            table_dec, table_dec,
            pl.BlockSpec((r_heads, 1, V7X_LANES), lambda b, t: (0, 0, 0)),
            pl.BlockSpec((1, r_dv), lambda b, t: (0, 0)),
            dec_state,
        ],
        out_specs=[
            pl.BlockSpec((ct, vw), lambda b, t: (row(b, t), 0)),
            pl.BlockSpec((1, heads, dk, dv), lambda b, t: (b, 0, 0, 0)),
            dec_o,
            dec_state,
        ],
        out_shape=[
            jax.ShapeDtypeStruct((batch * seq, vw), BF16),
            jax.ShapeDtypeStruct((batch, heads, dk, dv), F32),
            jax.ShapeDtypeStruct((n_dec, r_heads * r_dv), BF16),
            jax.ShapeDtypeStruct(ret_state_dec.shape, ret_state_dec.dtype),
        ],
        scratch_shapes=[
            pltpu.VMEM((heads, dk, dv), F32),
            pltpu.VMEM(((levels + 2) * GLA_CHUNK, 3 * GLA_CHUNK), BF16),
            pltpu.VMEM((GLA_CHUNK, GLA_CHUNK), jnp.int32),
        ] + dec_scratch,
        compiler_params=_params(2, claim_all=True),
        name="gla_prompt_ret_decode",
    )(*_segment_arrays(lay, projs, ("qa", "ka", "va", "ga")), log2_decay, gla_norm_w,
      *_segment_arrays(lay, projs_dec, ("qb", "kb", "vb", "gb")), cos_dec, sin_dec,
      log_gamma, ret_norm_w, ret_state_dec)


def _rotary(x, cos, sin):
    half = x.shape[1] // 2
    x1, x2 = x[:, :half], x[:, half:]
    return jnp.concatenate([x1 * cos - x2 * sin, x1 * sin + x2 * cos], axis=1)


def _ret_kernel(*refs, heads, dk, dv, c, dec_dims):
    q_ref, k_ref, v_ref, gb_ref, cos_ref, sin_ref, lg_ref, nw_ref = refs[:8]
    dec_in = refs[8:15]
    o_ref, s_out_ref = refs[15:17]
    dec_out = refs[17:19]
    s_ref, dmat_ref, qdec_ref, kdec_ref = refs[19:23]
    dec_scratch = refs[23:]
    t = pl.program_id(1)
    step = pl.program_id(0) * pl.num_programs(1) + t
    last_step = pl.num_programs(0) * pl.num_programs(1) - 1

    _gla_decode_section(step, last_step, dec_in + dec_out, dec_scratch,
                        heads=dec_dims[0], dk=dec_dims[1], dv=dec_dims[2])

    @pl.when(t == 0)
    def _():
        s_ref[...] = jnp.zeros_like(s_ref)
        ri = lax.broadcasted_iota(jnp.int32, (c, c), 0)
        rj = lax.broadcasted_iota(jnp.int32, (c, c), 1)
        dist = (ri - rj).astype(F32)
        rowl = lax.broadcasted_iota(jnp.int32, (c, V7X_LANES), 0).astype(F32)
        for hh in range(heads):
            lg = lg_ref[hh]
            dmat_ref[hh] = jnp.exp(jnp.where(ri >= rj, dist * lg[:, :1], -jnp.inf))
            qdec_ref[hh] = jnp.exp((rowl + 1.0) * lg)
            kdec_ref[hh] = jnp.exp((float(c - 1) - rowl) * lg)

    ct = q_ref.shape[0]
    kcs = [slice(hh * dk, (hh + 1) * dk) for hh in range(heads)]
    vcs = [slice(hh * dv, (hh + 1) * dv) for hh in range(heads)]

    def chunk(ci, carry):
        rows = pl.ds(pl.multiple_of(ci * c, c), c)
        cos, sin = cos_ref[rows, :], sin_ref[rows, :]
        qrs = [_rotary(q_ref[rows, kc].astype(F32), cos, sin).astype(BF16) for kc in kcs]
        krs = [_rotary(k_ref[rows, kc].astype(F32), cos, sin) * (dk ** -0.5) for kc in kcs]
        scores = [_dot_nt(qrs[hh], krs[hh].astype(BF16)) * dmat_ref[hh] for hh in range(heads)]
        outs = []
        for hh in range(heads):
            qdec = jnp.concatenate([qdec_ref[hh]] * (dv // V7X_LANES), axis=1)
            o = qdec * _dot(qrs[hh], s_ref[hh].astype(BF16))
            outs.append(o + _dot(scores[hh].astype(BF16), v_ref[rows, vcs[hh]]))
        for hh in range(heads):
            kdec = jnp.concatenate([kdec_ref[hh]] * (dk // V7X_LANES), axis=1)
            k_tail = (krs[hh] * kdec).astype(BF16)
            lgv = jnp.concatenate([lg_ref[hh]] * (dv // V7X_LANES), axis=1)
            s_ref[hh] = jnp.exp(float(c) * lgv) * s_ref[hh] + _dot_tn(k_tail, v_ref[rows, vcs[hh]])
        for hh in range(heads):
            _ln_gate_store(outs[hh], nw_ref[...], gb_ref[rows, vcs[hh]].astype(F32), o_ref, rows, vcs[hh])
        return carry

    lax.fori_loop(0, ct // c, chunk, 0)

    @pl.when(t == pl.num_programs(1) - 1)
    def _():
        s_out_ref[0] = s_ref[...]


def _ret_prompt_gla_decode(projs, cos, sin, log_gamma, ret_norm_w, projs_dec, log2_decay_dec, gla_norm_w,
                           gla_state_dec, lay, batch, seq):
    heads, dk, dv = lay["ret_heads"], lay["ret_dk"], lay["ret_dv"]
    qk, vw = heads * dk, heads * dv
    g_heads, g_dk, g_dv = lay["gla_heads"], lay["gla_dk"], lay["gla_dv"]
    c = min(seq, RET_CHUNK)
    ct = min(seq, RET_STEP_CHUNKS * c)
    assert seq % ct == 0 and ct % c == 0
    nt = seq // ct
    half = dk // 2
    n_dec = projs_dec[0].shape[0]
    tps = _decode_plan(n_dec, batch * nt)
    row = lambda b, t: b * nt + t
    dec_rows, dec_state, dec_o, dec_scratch = _decode_specs("gla", lay, n_dec, tps, row)
    kern = functools.partial(_ret_kernel, heads=heads, dk=dk, dv=dv, c=c, dec_dims=(g_heads, g_dk, g_dv))
    return pl.pallas_call(
        kern,
        grid=(batch, nt),
        in_specs=[
            pl.BlockSpec((ct, qk), lambda b, t: (row(b, t), _segment_block(lay, "qb", qk))),
            pl.BlockSpec((ct, qk), lambda b, t: (row(b, t), _segment_block(lay, "kb", qk))),
            pl.BlockSpec((ct, vw), lambda b, t: (row(b, t), _segment_block(lay, "vb", vw))),
            pl.BlockSpec((ct, vw), lambda b, t: (row(b, t), _segment_block(lay, "gb", vw))),
            pl.BlockSpec((ct, half), lambda b, t: (t, 0)),
            pl.BlockSpec((ct, half), lambda b, t: (t, 0)),
            pl.BlockSpec((heads, 1, V7X_LANES), lambda b, t: (0, 0, 0)),
            pl.BlockSpec((1, dv), lambda b, t: (0, 0)),
        ] + dec_rows + [
            pl.BlockSpec((n_dec, g_heads * g_dk), lambda b, t: (0, 0)),
            pl.BlockSpec((1, g_dv), lambda b, t: (0, 0)),
            dec_state,
        ],
        out_specs=[
            pl.BlockSpec((ct, vw), lambda b, t: (row(b, t), 0)),
            pl.BlockSpec((1, heads, dk, dv), lambda b, t: (b, 0, 0, 0)),
            dec_o,
            dec_state,
        ],
        out_shape=[
            jax.ShapeDtypeStruct((batch * seq, vw), BF16),
            jax.ShapeDtypeStruct((batch, heads, dk, dv), F32),
            jax.ShapeDtypeStruct((n_dec, g_heads * g_dv), BF16),
            jax.ShapeDtypeStruct(gla_state_dec.shape, gla_state_dec.dtype),
        ],
        scratch_shapes=[
            pltpu.VMEM((heads, dk, dv), F32),
            pltpu.VMEM((heads, c, c), F32),
            pltpu.VMEM((heads, c, V7X_LANES), F32),
            pltpu.VMEM((heads, c, V7X_LANES), F32),
        ] + dec_scratch,
        compiler_params=_params(2, claim_all=True),
        name="ret_prompt_gla_decode",
    )(*_segment_arrays(lay, projs, ("qb", "kb", "vb", "gb")), cos, sin, log_gamma, ret_norm_w,
      *_segment_arrays(lay, projs_dec, ("qa", "ka", "va", "ga")), log2_decay_dec, gla_norm_w, gla_state_dec)


def _merge_kernel(oap_ref, obp_ref, oas_ref, obs_ref, wa_ref, wb_ref, g0p_ref, g1p_ref, g0s_ref, g1s_ref, wnext_ref,
                  mp_ref, ms_ref, wnext_bf_ref, wa_bf, wb_bf):
    wnext_bf_ref[...] = wnext_ref[...].astype(BF16)

    def merged(oa, ob, g0, g1):
        ya = _dot(oa, wa_bf[...])
        yb = _dot(ob, wb_bf[...])
        return _sigmoid(g0.astype(F32)) * ya + _sigmoid(g1.astype(F32)) * yb

    @pl.when(pl.program_id(1) == 0)
    def _():
        wa_bf[...] = wa_ref[...].astype(BF16)
        wb_bf[...] = wb_ref[...].astype(BF16)
        ms_ref[...] = merged(oas_ref[...], obs_ref[...], g0s_ref[...], g1s_ref[...]).astype(ms_ref.dtype)

    mp_ref[...] = merged(oap_ref[...], obp_ref[...], g0p_ref[...], g1p_ref[...]).astype(mp_ref.dtype)


def _slab_specs(w_next, n_steps, step_of):
    kn, dn = w_next.shape
    assert kn % n_steps == 0 and (kn // n_steps) % 16 == 0, (kn, n_steps)
    slab = kn // n_steps
    spec = pl.BlockSpec((slab, dn), lambda j, i: (step_of(j, i), 0))
    return spec, spec, jax.ShapeDtypeStruct((kn, dn), BF16), _nbytes((slab, dn), F32) + _nbytes((slab, dn), BF16)


def _merge(oa_p, ob_p, oa_s, ob_s, wa, wb, projs_p, projs_s, lay, w_next):
    proj_p, proj_s = projs_p[lay["cols"]["mg"][0]], projs_s[lay["cols"]["mg"][0]]
    m_p, ka = oa_p.shape
    kb = ob_p.shape[1]
    tail = oa_s.shape[0]
    d = wa.shape[1]
    tm = _row_tile(m_p, ROW_TILE)
    tn = min(d, 1024)
    mg = lay["cols"]["mg"][1]
    assert d % tn == 0 and mg % tn == 0
    g0 = mg // tn
    g1 = (mg + d) // tn
    n_m = m_p // tm
    slab_in, slab_out, slab_shape, slab_bytes = _slab_specs(w_next, (d // tn) * n_m, lambda j, i: j * n_m + i)
    return pl.pallas_call(
        _merge_kernel,
        grid=(d // tn, n_m),
        in_specs=[
            pl.BlockSpec((tm, ka), lambda j, i: (i, 0)),
            pl.BlockSpec((tm, kb), lambda j, i: (i, 0)),
            pl.BlockSpec((tail, ka), lambda j, i: (0, 0)),
            pl.BlockSpec((tail, kb), lambda j, i: (0, 0)),
            pl.BlockSpec((ka, tn), lambda j, i: (0, j)),
            pl.BlockSpec((kb, tn), lambda j, i: (0, j)),
            pl.BlockSpec((tm, tn), lambda j, i: (i, g0 + j)),
            pl.BlockSpec((tm, tn), lambda j, i: (i, g1 + j)),
            pl.BlockSpec((tail, tn), lambda j, i: (0, g0 + j)),
            pl.BlockSpec((tail, tn), lambda j, i: (0, g1 + j)),
            slab_in,
        ],
        out_specs=[pl.BlockSpec((tm, tn), lambda j, i: (i, j)), pl.BlockSpec((tail, tn), lambda j, i: (0, j)),
                   slab_out],
        out_shape=[jax.ShapeDtypeStruct((m_p, d), BF16), jax.ShapeDtypeStruct((tail, d), BF16), slab_shape],
        scratch_shapes=[pltpu.VMEM((ka, tn), BF16), pltpu.VMEM((kb, tn), BF16)],
        compiler_params=_params(
            2, _nbytes((tm, ka), BF16), _nbytes((tm, kb), BF16), _nbytes((tail, ka), BF16), _nbytes((tail, kb), BF16),
            _nbytes((ka, tn), F32), _nbytes((kb, tn), F32), 3 * _nbytes((tm, tn), BF16), 3 * _nbytes((tail, tn), BF16),
            slab_bytes,
            scratch_bytes=_nbytes((ka, tn), BF16) + _nbytes((kb, tn), BF16) + 3 * _nbytes((tm, tn), F32)),
        name="merge",
    )(oa_p, ob_p, oa_s, ob_s, wa, wb, proj_p, proj_p, proj_s, proj_s, w_next)


def _proj_res_norm_kernel(*refs, emit_sum, group, n_blocks):
    ap_refs, as_refs, w_refs = refs[:group], refs[group:2 * group], refs[2 * group:3 * group]
    resp_ref, ress_ref, nw_ref = refs[3 * group:3 * group + 3]
    out_refs = refs[3 * group + 3:]
    n_out = 2 if emit_sum else 1
    outs_p, outs_s = out_refs[:n_out], out_refs[n_out:]
    i = pl.program_id(0)
    k = pl.program_id(1)
    last_k = k == pl.num_programs(1) - 1
    d = w_refs[0].shape[1]
    col_chunk = min(d, 512)
    rest = n_blocks % group

    def step(a_refs, res_ref, outs):
        acc_ref = outs[0]
        nrow = acc_ref.shape[0]
        row_chunk = min(nrow, 128)
        assert nrow % row_chunk == 0

        @pl.when(k == 0)
        def _():
            acc_ref[...] = res_ref[...]

        def accumulate(n_used):
            a = [a_refs[s][...] for s in range(n_used)]
            for c in range(d // col_chunk):
                cs = slice(c * col_chunk, (c + 1) * col_chunk)
                part = _dot(a[0], w_refs[0][:, cs])
                for s in range(1, n_used):
                    part = part + _dot(a[s], w_refs[s][:, cs])
                acc_ref[:, cs] += part

        if rest == 0:
            accumulate(group)
        else:
            pl.when(jnp.logical_not(last_k))(lambda: accumulate(group))
            pl.when(last_k)(lambda: accumulate(rest))

        @pl.when(last_k)
        def _():
            def body(c, carry):
                rr = pl.ds(pl.multiple_of(c * row_chunk, row_chunk), row_chunk)
                y = _rmsnorm_rows(acc_ref[rr, :], nw_ref[...])
                if emit_sum:
                    outs[1][rr, :] = y.astype(outs[1].dtype)
                else:
                    acc_ref[rr, :] = y
                return carry

            lax.fori_loop(0, nrow // row_chunk, body, 0)

    step(ap_refs, resp_ref, outs_p)

    @pl.when(i == pl.num_programs(0) - 1)
    def _():
        step(as_refs, ress_ref, outs_s)


def _proj_res_norm(a_p, a_s, w, res_p, res_s, norm_w, emit_sum, group):
    m_p, kdim = a_p.shape
    tail = a_s.shape[0]
    d = w.shape[1]
    tm = _row_tile(m_p, ROW_TILE)
    tk = min(kdim, K_BLOCK)
    assert kdim % tk == 0
    n_blocks = kdim // tk
    group = min(group, n_blocks)
    n_steps = pl.cdiv(n_blocks, group)
    blk = lambda k, s: jnp.minimum(k * group + s, n_blocks - 1)
    p_spec = pl.BlockSpec((tm, d), lambda i, k: (i, 0))
    s_spec = pl.BlockSpec((tail, d), lambda i, k: (0, 0))
    s3_spec = pl.BlockSpec((tail, None, d), lambda i, k: (0, 0, 0))
    res_s_spec = s3_spec if res_s.ndim == 3 else s_spec
    out_specs = [p_spec, s3_spec]
    out_shape = [jax.ShapeDtypeStruct((m_p, d), F32), jax.ShapeDtypeStruct((tail, 1, d), F32)]
    assert w.dtype == BF16
    blocks = [group * _nbytes((tm, tk), BF16), group * _nbytes((tail, tk), BF16), group * _nbytes((tk, d), BF16),
              2 * _nbytes((tm, d), F32), 2 * _nbytes((tail, d), F32)]
    if emit_sum:
        out_specs = [p_spec, p_spec, s_spec, s_spec]
        out_shape = [out_shape[0], jax.ShapeDtypeStruct((m_p, d), BF16),
                     jax.ShapeDtypeStruct((tail, d), F32), jax.ShapeDtypeStruct((tail, d), BF16)]
        blocks += [_nbytes((tm, d), BF16), _nbytes((tail, d), BF16)]
    in_specs = (
        [pl.BlockSpec((tm, tk), lambda i, k, s=s: (i, blk(k, s))) for s in range(group)]
        + [pl.BlockSpec((tail, tk), lambda i, k, s=s: (0, blk(k, s))) for s in range(group)]
        + [pl.BlockSpec((tk, d), lambda i, k, s=s: (blk(k, s), 0)) for s in range(group)]
        + [p_spec, res_s_spec, pl.BlockSpec((1, d), lambda i, k: (0, 0))])
    return pl.pallas_call(
        functools.partial(_proj_res_norm_kernel, emit_sum=emit_sum, group=group, n_blocks=n_blocks),
        grid=(m_p // tm, n_steps),
        in_specs=in_specs,
        out_specs=out_specs,
        out_shape=out_shape,
        compiler_params=_params(2, *blocks),
        name="proj_res_norm",
    )(*([a_p] * group + [a_s] * group + [w] * group), res_p, res_s, norm_w.reshape(1, d))


def _swiglu_kernel(hp_ref, hs_ref, wg_ref, wu_ref, wnext_ref, op_ref, os_ref, wnext_bf_ref, wg_bf, wu_bf):
    wnext_bf_ref[...] = wnext_ref[...].astype(BF16)

    tn = wg_bf.shape[1]
    col_chunk = min(tn, 256)

    def act(h_ref, o_ref):
        h = h_ref[...]
        for c in range(tn // col_chunk):
            cs = slice(c * col_chunk, (c + 1) * col_chunk)
            a = _dot(h, wg_bf[:, cs])
            b = _dot(h, wu_bf[:, cs])
            o_ref[:, cs] = (_silu(a) * b).astype(o_ref.dtype)

    @pl.when(pl.program_id(1) == 0)
    def _():
        wg_bf[...] = wg_ref[...].astype(BF16)
        wu_bf[...] = wu_ref[...].astype(BF16)
        act(hs_ref, os_ref)

    act(hp_ref, op_ref)


def _swiglu(h_p, h_s, wg, wu, w_next):
    m_p, d = h_p.shape
    tail = h_s.shape[0]
    f = wg.shape[1]
    tm = _row_tile(m_p, ROW_TILE_WIDE)
    tn = 512 if f % 512 == 0 else 256
    assert f % tn == 0
    n_m = m_p // tm
    slab_in, slab_out, slab_shape, slab_bytes = _slab_specs(w_next, (f // tn) * n_m, lambda j, i: j * n_m + i)
    return pl.pallas_call(
        _swiglu_kernel,
        grid=(f // tn, n_m),
        in_specs=[
            pl.BlockSpec((tm, d), lambda j, i: (i, 0)),
            pl.BlockSpec((tail, d), lambda j, i: (0, 0)),
            pl.BlockSpec((d, tn), lambda j, i: (0, j)),
            pl.BlockSpec((d, tn), lambda j, i: (0, j)),
            slab_in,
        ],
        out_specs=[pl.BlockSpec((tm, tn), lambda j, i: (i, j)), pl.BlockSpec((tail, tn), lambda j, i: (0, j)),
                   slab_out],
        out_shape=[jax.ShapeDtypeStruct((m_p, f), BF16), jax.ShapeDtypeStruct((tail, f), BF16), slab_shape],
        scratch_shapes=[pltpu.VMEM((d, tn), BF16), pltpu.VMEM((d, tn), BF16)],
        compiler_params=_params(
            2, _nbytes((tm, d), BF16), _nbytes((tail, d), BF16), 2 * _nbytes((d, tn), F32),
            _nbytes((tm, tn), BF16), _nbytes((tail, tn), BF16), slab_bytes,
            scratch_bytes=2 * _nbytes((d, tn), BF16) + 3 * _nbytes((tm, tn), F32)),
        name="swiglu",
    )(h_p, h_s, wg, wu, w_next)


def _layout(d_model, in_width, state_gla, state_ret, gate_rank):
    _, _, gh, gdk, gdv = state_gla.shape
    _, _, rh, rdk, rdv = state_ret.shape
    gqk, gv, rqk, rv = gh * gdk, gh * gdv, rh * rdk, rh * rdv
    lay = dict(gla_heads=gh, gla_dk=gdk, gla_dv=gdv, ret_heads=rh, ret_dk=rdk, ret_dv=rdv, rank=gate_rank)
    off = 0
    for name, width in (("qa", gqk), ("ka", gqk), ("va", gv), ("ga", gv), ("qb", rqk), ("kb", rqk),
                        ("vb", rv), ("gb", rv), ("mg", 2 * d_model)):
        lay[name] = off
        off += width
    lay["out_cols"] = off
    lay["plain_cols"] = 2 * gqk + gv
    lay["gd_src"] = lay["plain_cols"]
    assert lay["gd_src"] % V7X_LANES == 0 and gate_rank <= V7X_LANES
    assert in_width == off + gate_rank
    lay["widths"] = dict(qa=gqk, ka=gqk, va=gv, ga=gv, qb=rqk, kb=rqk, vb=rv, gb=rv, mg=2 * d_model)
    return lay


def _split_columns(lay, first_cols):
    cols = {}
    for name, width in lay["widths"].items():
        off = lay[name]
        assert off + width <= first_cols or off >= first_cols, "a segment straddles the two arrays"
        cols[name] = (0, off) if off < first_cols else (1, off - first_cols)
    return cols


def _layer(x_p, x_s, st_gla, st_ret, wts, lay, log_gamma, final_norm):
    (norm_mix, w_in, w_gate_up, b_gate, gla_norm_w, w_gla_up, ret_norm_w, w_ret_up, w_out, norm_ffn,
     w_ffn_gate, w_ffn_up, w_ffn_down) = wts
    batch, seq, d = x_p.shape
    rank = lay["rank"]
    gqk = lay["gla_heads"] * lay["gla_dk"]
    bup = b_gate.reshape(1, gqk)
    gnw = gla_norm_w.reshape(1, -1)
    rnw = ret_norm_w.reshape(1, -1)
    tn = 1024 if (lay["out_cols"] % 1024 == 0 and lay["plain_cols"] % 1024 == 0) else 512
    xp = x_p.reshape(batch * seq, d)
    assert x_s.ndim == 3 and x_s.shape[1] == 1, "one new token per decode sequence"
    xs = x_s
    w_in_t = w_in.T

    h_p, h_s, g_p, g_s, cos_p, sin_p, cos_s, sin_s, first_p, first_s = _rmsnorm_gate(
        xp, xs, norm_mix, w_in_t, lay["gd_src"], w_gate_up, bup, seq, lay["ret_dk"] // 2, tn)
    rest_p, rest_s = _in_proj(h_p, h_s, w_in_t, lay["plain_cols"], rank, lay["out_cols"], tn, 1)
    projs_p, projs_s = (first_p, rest_p), (first_s, rest_s)
    lay = dict(lay, cols=_split_columns(lay, tn))
    oa_p, sa_p, ob_s, sb_s = _gla_prompt_ret_decode(
        projs_p, g_p, gnw, projs_s, cos_s, sin_s, log_gamma, rnw, st_ret, lay, batch, seq)
    ob_p, sb_p, oa_s, sa_s = _ret_prompt_gla_decode(
        projs_p, cos_p, sin_p, log_gamma, rnw, projs_s, g_s, gnw, st_gla, lay, batch, seq)
    m_p, m_s, w_out_bf = _merge(oa_p, ob_p, oa_s, ob_s, w_gla_up, w_ret_up, projs_p, projs_s, lay, w_out)
    x1_p, h2_p, x1_s, h2_s = _proj_res_norm(m_p, m_s, w_out_bf, xp, xs, norm_ffn, True, 2)
    act_p, act_s, w_down_bf = _swiglu(h2_p, h2_s, w_ffn_gate, w_ffn_up, w_ffn_down)
    y_p, y_s = _proj_res_norm(act_p, act_s, w_down_bf, x1_p, x1_s, final_norm, False, 3)
    return (y_p, sa_p, sb_p), (y_s, sa_s, sb_s)


def kernel(x_prompt, x_sample, state_gla, state_ret, norm_mix, w_in, w_gla_gate_up, b_gla_gate, gla_norm_w,
           w_gla_up, ret_norm_w, w_ret_up, w_out, norm_ffn, w_ffn_gate, w_ffn_up, w_ffn_down, norm_final):
    depth = w_in.shape[0]
    assert depth == 1, "single-layer trunk"
    batch, seq, d = x_prompt.shape
    lay = _layout(d, w_in.shape[-1], state_gla, state_ret, w_gla_gate_up.shape[1])
    rh, rdk = lay["ret_heads"], lay["ret_dk"]
    assert rdk // 2 == V7X_LANES
    lg = jnp.log1p(-jnp.exp(jnp.linspace(math.log(1.0 / 32), math.log(1.0 / 512), rh))).astype(F32)
    log_gamma = jnp.broadcast_to(lg[:, None, None], (rh, 1, V7X_LANES))

    wts = (norm_mix[0], w_in[0], w_gla_gate_up[0], b_gla_gate[0], gla_norm_w[0], w_gla_up[0], ret_norm_w[0],
           w_ret_up[0], w_out[0], norm_ffn[0], w_ffn_gate[0], w_ffn_up[0], w_ffn_down[0])
    (y_p, ga_p, re_p), (y_s, ga_s, re_s) = _layer(
        x_prompt, x_sample, state_gla[0], state_ret[0], wts, lay, log_gamma, norm_final)

    sd = state_gla.dtype
    return (y_p.reshape(batch, seq, d), y_s.reshape(x_sample.shape),
            ga_p[None].astype(sd), re_p[None].astype(state_ret.dtype),
            ga_s[None].astype(sd), re_s[None].astype(state_ret.dtype))
```

```python
import functools
import math

import numpy as np
import jax
import jax.numpy as jnp
from jax import lax
from jax.experimental import pallas as pl
from jax.experimental.pallas import tpu as pltpu

EPS = 1e-6
ROPE_BASE = 10000.0
GLA_GATE_NORM = 16.0
PAST_LEN = 16384

V7X_LANES = 128
V7X_VMEM_REQUEST_CAP = 60000 * 1024
COMPILER_SCRATCH_BYTES = 12 * 1024 * 1024

GLA_CHUNK = 64
GLA_STEP_CHUNKS = 8
LOG2_E = 1.4426950408889634
RET_CHUNK = 128
RET_STEP_CHUNKS = 4
ROW_TILE = 1024
ROW_TILE_WIDE = 2048
K_BLOCK = 512

BF16 = jnp.bfloat16
F32 = jnp.float32


def _params(n_axes, *block_bytes, scratch_bytes=0, claim_all=False):
    need = 2 * sum(block_bytes) + scratch_bytes + COMPILER_SCRATCH_BYTES
    if claim_all:
        need = V7X_VMEM_REQUEST_CAP
    return pltpu.CompilerParams(
        dimension_semantics=("arbitrary",) * n_axes,
        vmem_limit_bytes=int(min(V7X_VMEM_REQUEST_CAP, need)),
    )


def _nbytes(shape, dtype):
    return int(np.prod(shape)) * jnp.dtype(dtype).itemsize


def _sigmoid(x):
    return 1.0 / (1.0 + jnp.exp(-x))


def _silu(x):
    return x * _sigmoid(x)


def _log_sigmoid(x):
    return jnp.minimum(x, 0.0) - jnp.log(1.0 + jnp.exp(-jnp.abs(x)))


def _dot(a, b):
    return jnp.dot(a, b, preferred_element_type=F32)


def _dot_nt(a, b):
    return lax.dot_general(a, b, (((1,), (1,)), ((), ())), preferred_element_type=F32)


def _dot_tn(a, b):
    return lax.dot_general(a, b, (((0,), (0,)), ((), ())), preferred_element_type=F32)


def _row_tile(m, want):
    t = min(m, want)
    assert m % t == 0, (m, t)
    return t


def _rmsnorm_rows(x, w):
    ms = jnp.mean(x * x, axis=-1, keepdims=True)
    return x * lax.rsqrt(ms + EPS) * w


def _rope_rows(cos_ref, sin_ref, pos0):
    rows, half = cos_ref.shape
    pos = (lax.broadcasted_iota(jnp.int32, (rows, half), 0) + pos0).astype(F32)
    idx = lax.broadcasted_iota(jnp.int32, (rows, half), 1).astype(F32)
    ang = pos * jnp.exp(idx * (-math.log(ROPE_BASE) / half))
    cos_ref[...] = jnp.cos(ang)
    sin_ref[...] = jnp.sin(ang)


def _rmsnorm_kernel(xp_ref, xs_ref, w_ref, wgd_ref, wup_ref, bup_ref, w0_ref,
                    hp_ref, hs_ref, gp_ref, gs_ref, cosp_ref, sinp_ref, cosd_ref, sind_ref, p0p_ref, p0s_ref,
                    wgd_bf, wup_bf, w0_bf):
    i = pl.program_id(0)

    @pl.when(i == 0)
    def _():
        wgd_bf[...] = wgd_ref[...].astype(BF16)
        wup_bf[...] = jnp.zeros_like(wup_bf)
        wup_bf[0:wup_ref.shape[0], :] = wup_ref[...].astype(BF16)
        w0_bf[...] = w0_ref[...].astype(BF16)

    def rows(x, h_ref, g_ref, p0_ref, cos_ref, sin_ref, pos0):
        h = _rmsnorm_rows(x, w_ref[...]).astype(h_ref.dtype)
        h_ref[...] = h
        gd = _dot_nt(h, wgd_bf[...])
        x = _dot(gd.astype(BF16), wup_bf[...]) + bup_ref[...]
        g_ref[...] = _log_sigmoid(x) * (LOG2_E / GLA_GATE_NORM)
        _rope_rows(cos_ref, sin_ref, pos0)
        p0_ref[...] = _dot_nt(h, w0_bf[...]).astype(p0_ref.dtype)

    rows(xp_ref[...], hp_ref, gp_ref, p0p_ref, cosp_ref, sinp_ref, i * cosp_ref.shape[0])

    @pl.when(i == 0)
    def _():
        rows(xs_ref[...], hs_ref, gs_ref, p0s_ref, cosd_ref, sind_ref, PAST_LEN)


def _rmsnorm_gate(x_p, x_s, w, w_in_t, gate_row0, w_gate_up, bup, seq, half, tn0):
    m_p, d = x_p.shape
    tail = x_s.shape[0]
    rank, gw = w_gate_up.shape
    tm = _row_tile(m_p, ROW_TILE // 2)
    n_steps = m_p // tm
    assert gate_row0 % V7X_LANES == 0 and seq % n_steps == 0 and rank % 16 == 0
    pos_rows = seq // n_steps
    table = pl.BlockSpec((pos_rows, half), lambda i: (i, 0))
    table_dec = pl.BlockSpec((8, half), lambda i: (0, 0))
    return pl.pallas_call(
        _rmsnorm_kernel,
        grid=(n_steps,),
        in_specs=[
            pl.BlockSpec((tm, d), lambda i: (i, 0)),
            pl.BlockSpec((tail, None, d), lambda i: (0, 0, 0)),
            pl.BlockSpec((1, d), lambda i: (0, 0)),
            pl.BlockSpec((V7X_LANES, d), lambda i: (gate_row0 // V7X_LANES, 0)),
            pl.BlockSpec((rank, gw), lambda i: (0, 0)),
            pl.BlockSpec((1, gw), lambda i: (0, 0)),
            pl.BlockSpec((tn0, d), lambda i: (0, 0)),
        ],
        out_specs=[
            pl.BlockSpec((tm, d), lambda i: (i, 0)), pl.BlockSpec((tail, d), lambda i: (0, 0)),
            pl.BlockSpec((tm, gw), lambda i: (i, 0)), pl.BlockSpec((tail, gw), lambda i: (0, 0)),
            table, table, table_dec, table_dec,
            pl.BlockSpec((tm, tn0), lambda i: (i, 0)), pl.BlockSpec((tail, tn0), lambda i: (0, 0)),
        ],
        out_shape=[
            jax.ShapeDtypeStruct((m_p, d), BF16), jax.ShapeDtypeStruct((tail, d), BF16),
            jax.ShapeDtypeStruct((m_p, gw), F32), jax.ShapeDtypeStruct((tail, gw), F32),
            jax.ShapeDtypeStruct((seq, half), F32), jax.ShapeDtypeStruct((seq, half), F32),
            jax.ShapeDtypeStruct((8, half), F32), jax.ShapeDtypeStruct((8, half), F32),
            jax.ShapeDtypeStruct((m_p, tn0), BF16), jax.ShapeDtypeStruct((tail, tn0), BF16),
        ],
        scratch_shapes=[pltpu.VMEM((V7X_LANES, d), BF16), pltpu.VMEM((V7X_LANES, gw), BF16),
                        pltpu.VMEM((tn0, d), BF16)],
        compiler_params=_params(1, claim_all=True),
        name="rmsnorm_gate",
    )(x_p, x_s, w.reshape(1, d), w_in_t, w_gate_up, bup, w_in_t)


def _in_proj_kernel(hp_ref, hs_ref, wm_ref, wn_ref, op_ref, os_ref, wbf_ref, *, n_plain, shift, first_tile):
    j = pl.program_id(0) + first_tile
    i = pl.program_id(1)
    tn = wbf_ref.shape[0]

    @pl.when(jnp.logical_and(i == 0, j < n_plain))
    def _():
        wbf_ref[...] = wm_ref[...].astype(BF16)

    @pl.when(jnp.logical_and(i == 0, j >= n_plain))
    def _():
        wbf_ref[0:tn - shift, :] = wm_ref[shift:tn, :].astype(BF16)
        wbf_ref[tn - shift:tn, :] = wn_ref[...].astype(BF16)

    @pl.when(i == 0)
    def _():
        os_ref[...] = _dot_nt(hs_ref[...], wbf_ref[...]).astype(os_ref.dtype)

    op_ref[...] = _dot_nt(hp_ref[...], wbf_ref[...]).astype(op_ref.dtype)


def _in_proj(h_p, h_s, w_in_t, plain_cols, shift, out_cols, tn, first_tile):
    m_p, d = h_p.shape
    tail = h_s.shape[0]
    tm = _row_tile(m_p, ROW_TILE_WIDE)
    assert plain_cols % tn == 0 and out_cols % tn == 0 and tn % shift == 0 and shift % 8 == 0
    n_plain = plain_cols // tn
    n_tiles = out_cols // tn - first_tile
    kern = functools.partial(_in_proj_kernel, n_plain=n_plain, shift=shift, first_tile=first_tile)
    return pl.pallas_call(
        kern,
        grid=(n_tiles, m_p // tm),
        in_specs=[
            pl.BlockSpec((tm, d), lambda j, i: (i, 0)),
            pl.BlockSpec((tail, d), lambda j, i: (0, 0)),
            pl.BlockSpec((tn, d), lambda j, i: (j + first_tile, 0)),
            pl.BlockSpec((shift, d), lambda j, i: ((j + first_tile + 1) * (tn // shift), 0)),
        ],
        out_specs=[pl.BlockSpec((tm, tn), lambda j, i: (i, j)), pl.BlockSpec((tail, tn), lambda j, i: (0, j))],
        out_shape=[jax.ShapeDtypeStruct((m_p, n_tiles * tn), BF16), jax.ShapeDtypeStruct((tail, n_tiles * tn), BF16)],
        scratch_shapes=[pltpu.VMEM((tn, d), BF16)],
        compiler_params=_params(
            2, _nbytes((tm, d), BF16), _nbytes((tail, d), BF16), _nbytes((tn, d), F32), _nbytes((shift, d), F32),
            _nbytes((tm, tn), BF16), _nbytes((tail, tn), BF16),
            scratch_bytes=_nbytes((tn, d), BF16) + _nbytes((tm, tn), F32)),
        name="in_proj",
    )(h_p, h_s, w_in_t, w_in_t)


def _prefix_sum_rows(sel3_bf16, g):
    g0 = g.astype(BF16)
    r1 = g - g0.astype(F32)
    g1 = r1.astype(BF16)
    g2 = (r1 - g1.astype(F32)).astype(BF16)
    return _dot(sel3_bf16, jnp.concatenate([g0, g1, g2], axis=0))


def _lane_bcast_cols(row, n):
    parts = []
    for c in range(n // V7X_LANES):
        tile = jnp.broadcast_to(row[:, c * V7X_LANES:(c + 1) * V7X_LANES], (V7X_LANES, V7X_LANES))
        parts.append(tile.T)
    return parts[0] if len(parts) == 1 else jnp.concatenate(parts, axis=0)


def _rms_gate_store(o, w, gate, out_ref, rows, cols):
    ms = jnp.mean(o * o, axis=-1, keepdims=True)
    y = o * lax.rsqrt(ms + EPS) * w
    out_ref[rows, cols] = (y * _silu(gate)).astype(out_ref.dtype)


def _ln_gate_store(o, w, gate, out_ref, rows, cols):
    mu = jnp.mean(o, axis=-1, keepdims=True)
    dlt = o - mu
    var = jnp.mean(dlt * dlt, axis=-1, keepdims=True)
    y = dlt * lax.rsqrt(var + EPS) * w
    out_ref[rows, cols] = (y * _silu(gate)).astype(out_ref.dtype)


def _token_selectors(n_tok):
    assert 3 * n_tok <= V7X_LANES
    j = lax.broadcasted_iota(jnp.int32, (V7X_LANES, V7X_LANES), 0)
    sel = []
    for t in range(n_tok):
        hit = jnp.logical_or(j == t, jnp.logical_or(j == n_tok + t, j == 2 * n_tok + t))
        sel.append(jnp.where(hit, 1.0, 0.0).astype(BF16))
    return jnp.stack(sel, axis=0)


def _column_source(x):
    n_tok, w = x.shape
    hi = x.astype(BF16).astype(F32)
    r1 = x - hi
    mid = r1.astype(BF16).astype(F32)
    lo = (r1 - mid).astype(BF16).astype(F32)
    x3 = jnp.concatenate([hi, mid, lo, jnp.zeros((V7X_LANES - 3 * n_tok, w), F32)], axis=0)
    parts = [x3[:, c * V7X_LANES:(c + 1) * V7X_LANES].T for c in range(w // V7X_LANES)]
    return (parts[0] if len(parts) == 1 else jnp.concatenate(parts, axis=0)).astype(BF16)


def _decode_advance(tok0, decay_rows_fn, decay_const_fn, k_ref, q_ref, v_ref, s_in_ref, s_out_ref, o_ref, sel_ref,
                    *, heads, dk, dv):
    n_tok = s_in_ref.shape[0]
    reps = dv // V7X_LANES
    rows = pl.ds(pl.multiple_of(tok0, n_tok), n_tok)

    def cols(src, tt):
        return jnp.concatenate([_dot(src, sel_ref[tt])] * reps, axis=1)

    for hh in range(heads):
        kc = slice(hh * dk, (hh + 1) * dk)
        vc = slice(hh * dv, (hh + 1) * dv)
        k_src, q_src = _column_source(k_ref[rows, kc]), _column_source(q_ref[rows, kc])
        a_src = None if decay_rows_fn is None else _column_source(decay_rows_fn(rows, hh))
        v = v_ref[rows, vc]
        o_rows = []
        for tt in range(n_tok):
            decay = decay_const_fn(hh) if a_src is None else cols(a_src, tt)
            s_new = decay * s_in_ref[tt, hh] + cols(k_src, tt) * v[tt:tt + 1, :]
            s_out_ref[tt, hh] = s_new
            o_rows.append(jnp.sum(cols(q_src, tt) * s_new, axis=0, keepdims=True))
        o_ref[rows, vc] = jnp.concatenate(o_rows, axis=0)


def _decode_plan(n_dec, n_steps):
    assert n_dec % n_steps == 0 and (n_dec // n_steps) % 8 == 0, (n_dec, n_steps)
    return n_dec // n_steps


def _gla_decode_section(step, last_step, refs, scratch, *, heads, dk, dv):
    qd_ref, kd_ref, vd_ref, gad_ref, gd_ref, nw_ref, sd_in_ref, od_ref, sd_out_ref = refs
    a_dec, q_dec, k_dec, v_dec, o_dec, sel_ref = scratch

    @pl.when(step == 0)
    def _():
        a_dec[...] = jnp.exp2(gd_ref[...])
        q_dec[...] = qd_ref[...].astype(F32) * (dk ** -0.5)
        k_dec[...] = kd_ref[...].astype(F32)
        v_dec[...] = vd_ref[...].astype(F32)
        sel_ref[...] = _token_selectors(sd_in_ref.shape[0])

    _decode_advance(step * sd_in_ref.shape[0], lambda rows, hh: a_dec[rows, hh * dk:(hh + 1) * dk], None,
                    k_dec, q_dec, v_dec, sd_in_ref, sd_out_ref, o_dec, sel_ref, heads=heads, dk=dk, dv=dv)

    @pl.when(step == last_step)
    def _():
        n_dec = o_dec.shape[0]
        for hh in range(heads):
            vc = slice(hh * dv, (hh + 1) * dv)
            _rms_gate_store(o_dec[:, vc], nw_ref[...], gad_ref[:, vc].astype(F32), od_ref, slice(0, n_dec), vc)


def _ret_decode_section(step, last_step, refs, scratch, *, heads, dk, dv):
    qd_ref, kd_ref, vd_ref, gbd_ref, cosd_ref, sind_ref, lg_ref, nw_ref, sd_in_ref, od_ref, sd_out_ref = refs
    q_dec, k_dec, v_dec, o_dec, sel_ref = scratch

    @pl.when(step == 0)
    def _():
        cosd, sind = cosd_ref[0:1, :], sind_ref[0:1, :]
        for hh in range(heads):
            kc = slice(hh * dk, (hh + 1) * dk)
            q_dec[:, kc] = _rotary(qd_ref[:, kc].astype(F32), cosd, sind)
            k_dec[:, kc] = _rotary(kd_ref[:, kc].astype(F32), cosd, sind) * (dk ** -0.5)
        v_dec[...] = vd_ref[...].astype(F32)
        sel_ref[...] = _token_selectors(sd_in_ref.shape[0])

    def gamma(hh):
        return jnp.exp(jnp.concatenate([lg_ref[hh]] * (dv // V7X_LANES), axis=1))

    _decode_advance(step * sd_in_ref.shape[0], None, gamma, k_dec, q_dec, v_dec, sd_in_ref, sd_out_ref, o_dec,
                    sel_ref, heads=heads, dk=dk, dv=dv)

    @pl.when(step == last_step)
    def _():
        n_dec = o_dec.shape[0]
        for hh in range(heads):
            vc = slice(hh * dv, (hh + 1) * dv)
            _ln_gate_store(o_dec[:, vc], nw_ref[...], gbd_ref[:, vc].astype(F32), od_ref, slice(0, n_dec), vc)


def _segment_block(lay, name, width):
    offset = lay["cols"][name][1]
    assert offset % width == 0
    return offset // width


def _segment_arrays(lay, arrays, names):
    return tuple(arrays[lay["cols"][n][0]] for n in names)


def _decode_specs(kind, lay, n_dec, tps, row):
    heads, dk, dv = lay[kind + "_heads"], lay[kind + "_dk"], lay[kind + "_dv"]
    qk, vw = heads * dk, heads * dv
    names = ("qa", "ka", "va", "ga") if kind == "gla" else ("qb", "kb", "vb", "gb")
    widths = (qk, qk, vw, vw)
    rows_in = [pl.BlockSpec((n_dec, w), lambda b, t, c=_segment_block(lay, n, w): (0, c))
               for n, w in zip(names, widths)]
    state = pl.BlockSpec((tps, heads, dk, dv), lambda b, t: (row(b, t), 0, 0, 0))
    o_spec = pl.BlockSpec((n_dec, vw), lambda b, t: (0, 0))
    n_qk = 3 if kind == "gla" else 2
    scratch = ([pltpu.VMEM((n_dec, qk), F32)] * n_qk + [pltpu.VMEM((n_dec, vw), F32)] * 2
               + [pltpu.VMEM((tps, V7X_LANES, V7X_LANES), BF16)])
    return rows_in, state, o_spec, scratch


def _gla_sum_matrices(c):
    levels = c.bit_length() - 1
    assert 1 << levels == c
    i = lax.broadcasted_iota(jnp.int32, (c, c), 0)
    j = lax.broadcasted_iota(jnp.int32, (c, c), 1)
    mats = [j <= i]
    for l in range(levels):
        ref = jnp.bitwise_or(jnp.bitwise_and(i, -(2 << l)), 1 << l)
        mats.append(jnp.logical_and(j > jnp.minimum(i, ref), j <= jnp.maximum(i, ref)))
    mats.append(j > i)
    sel = jnp.concatenate([jnp.where(m, 1.0, 0.0).astype(BF16) for m in mats], axis=0)
    return jnp.concatenate([sel, sel, sel], axis=1)


def _pair_level(c):
    levels = c.bit_length() - 1
    i = lax.broadcasted_iota(jnp.int32, (c, c), 0)
    j = lax.broadcasted_iota(jnp.int32, (c, c), 1)
    x = jnp.bitwise_xor(i, j)
    lvl = jnp.zeros((c, c), jnp.int32)
    for l in range(1, levels):
        lvl = lvl + jnp.where(x >= (1 << l), 1, 0)
    return jnp.where(i > j, lvl, jnp.where(i == j, levels, -1))


def _queries_else_keys(q, k, l):
    c = q.shape[0]
    span = 1 << l
    if span >= 8:
        parts = [(q if (b & 1) else k)[b * span:(b + 1) * span, :] for b in range(c // span)]
        return jnp.concatenate(parts, axis=0)
    row = lax.broadcasted_iota(jnp.int32, q.shape, 0)
    return jnp.where(jnp.bitwise_and(row, span) != 0, q, k)


def _gla_level_scores(q, k, sums):
    c = q.shape[0]
    levels = c.bit_length() - 1
    out = []
    for l in range(levels):
        x = _queries_else_keys(q, k, l) * jnp.exp2(sums[(1 + l) * c:(2 + l) * c, :])
        xb = x.astype(BF16)
        out.append(_dot_nt(xb, xb))
    return out


def _gla_chunk_out(q, k, v, sums, level_scores, pair_level, state):
    c = q.shape[0]
    levels = c.bit_length() - 1
    scores = jnp.where(pair_level == levels, jnp.sum(q * k, axis=-1, keepdims=True), 0.0)
    for l in range(levels):
        scores = jnp.where(pair_level == l, level_scores[l], scores)
    o = _dot((q * jnp.exp2(sums[0:c, :])).astype(BF16), state.astype(BF16))
    return o + _dot(scores.astype(BF16), v)


def _gla_next_state(k, v, sums, state):
    c, dk = k.shape
    levels = c.bit_length() - 1
    k_tail = (k * jnp.exp2(sums[(levels + 1) * c:(levels + 2) * c, :])).astype(BF16)
    decay = _lane_bcast_cols(jnp.exp2(sums[c - 1:c, :]), dk)
    decay_full = jnp.concatenate([decay] * (v.shape[1] // V7X_LANES), axis=1)
    return decay_full * state + _dot_tn(k_tail, v)


def _gla_kernel(*refs, heads, dk, dv, dec_dims):
    q_ref, k_ref, v_ref, ga_ref, g_ref, nw_ref = refs[:6]
    dec_in = refs[6:15]
    o_ref, s_out_ref = refs[15:17]
    dec_out = refs[17:19]
    s_ref, mats_ref, lvl_ref = refs[19:22]
    dec_scratch = refs[22:]
    t = pl.program_id(1)
    step = pl.program_id(0) * pl.num_programs(1) + t
    last_step = pl.num_programs(0) * pl.num_programs(1) - 1

    @pl.when(t == 0)
    def _():
        s_ref[...] = jnp.zeros_like(s_ref)
        mats_ref[...] = _gla_sum_matrices(GLA_CHUNK)
        lvl_ref[...] = _pair_level(GLA_CHUNK)

    _ret_decode_section(step, last_step, dec_in + dec_out, dec_scratch,
                        heads=dec_dims[0], dk=dec_dims[1], dv=dec_dims[2])

    ct = q_ref.shape[0]

    kcs = [slice(hh * dk, (hh + 1) * dk) for hh in range(heads)]
    vcs = [slice(hh * dv, (hh + 1) * dv) for hh in range(heads)]
    group = 2 if (ct // GLA_CHUNK) % 2 == 0 else 1

    def chunk_group(cg, carry):
        pair_level = lvl_ref[...]
        rows, sums, qs, ks, lvl_scores = [], [], [], [], []
        for u in range(group):
            r = pl.ds(pl.multiple_of((cg * group + u) * GLA_CHUNK, GLA_CHUNK), GLA_CHUNK)
            rows.append(r)
            sums.append(_prefix_sum_rows(mats_ref[...], g_ref[r, :]))
            qs.append([q_ref[r, kc].astype(F32) * (dk ** -0.5) for kc in kcs])
            ks.append([k_ref[r, kc].astype(F32) for kc in kcs])
            lvl_scores.append([_gla_level_scores(qs[u][hh], ks[u][hh], sums[u][:, kcs[hh]]) for hh in range(heads)])
        outs = []
        for u in range(group):
            r = rows[u]
            outs.append([_gla_chunk_out(qs[u][hh], ks[u][hh], v_ref[r, vcs[hh]], sums[u][:, kcs[hh]],
                                        lvl_scores[u][hh], pair_level, s_ref[hh]) for hh in range(heads)])
            for hh in range(heads):
                s_ref[hh] = _gla_next_state(ks[u][hh], v_ref[r, vcs[hh]], sums[u][:, kcs[hh]], s_ref[hh])
        for u in range(group):
            for hh in range(heads):
                _rms_gate_store(outs[u][hh], nw_ref[...], ga_ref[rows[u], vcs[hh]].astype(F32), o_ref, rows[u],
                                vcs[hh])
        return carry

    lax.fori_loop(0, ct // GLA_CHUNK // group, chunk_group, 0)

    @pl.when(t == pl.num_programs(1) - 1)
    def _():
        s_out_ref[0] = s_ref[...]


def _gla_prompt_ret_decode(projs, log2_decay, gla_norm_w, projs_dec, cos_dec, sin_dec, log_gamma, ret_norm_w,
                           ret_state_dec, lay, batch, seq):
    heads, dk, dv = lay["gla_heads"], lay["gla_dk"], lay["gla_dv"]
    qk, vw = heads * dk, heads * dv
    r_heads, r_dk, r_dv = lay["ret_heads"], lay["ret_dk"], lay["ret_dv"]
    ct = min(seq, GLA_STEP_CHUNKS * GLA_CHUNK)
    levels = GLA_CHUNK.bit_length() - 1
    assert seq % ct == 0 and ct % GLA_CHUNK == 0
    nt = seq // ct
    n_dec = projs_dec[0].shape[0]
    tps = _decode_plan(n_dec, batch * nt)
    row = lambda b, t: b * nt + t
    dec_rows, dec_state, dec_o, dec_scratch = _decode_specs("ret", lay, n_dec, tps, row)
    table_dec = pl.BlockSpec((cos_dec.shape[0], r_dk // 2), lambda b, t: (0, 0))
    kern = functools.partial(_gla_kernel, heads=heads, dk=dk, dv=dv, dec_dims=(r_heads, r_dk, r_dv))
    return pl.pallas_call(
        kern,
        grid=(batch, nt),
        in_specs=[
            pl.BlockSpec((ct, qk), lambda b, t: (row(b, t), _segment_block(lay, "qa", qk))),
            pl.BlockSpec((ct, qk), lambda b, t: (row(b, t), _segment_block(lay, "ka", qk))),
            pl.BlockSpec((ct, vw), lambda b, t: (row(b, t), _segment_block(lay, "va", vw))),
            pl.BlockSpec((ct, vw), lambda b, t: (row(b, t), _segment_block(lay, "ga", vw))),
            pl.BlockSpec((ct, qk), lambda b, t: (row(b, t), 0)),
            pl.BlockSpec((1, dv), lambda b, t: (0, 0)),
        ] + dec_rows + [
            table_dec, table_dec,
            pl.BlockSpec((r_heads, 1, V7X_LANES), lambda b, t: (0, 0, 0)),
            pl.BlockSpec((1, r_dv), lambda b, t: (0, 0)),
            dec_state,
        ],
        out_specs=[
            pl.BlockSpec((ct, vw), lambda b, t: (row(b, t), 0)),
            pl.BlockSpec((1, heads, dk, dv), lambda b, t: (b, 0, 0, 0)),
            dec_o,
            dec_state,
        ],
        out_shape=[
            jax.ShapeDtypeStruct((batch * seq, vw), BF16),
            jax.ShapeDtypeStruct((batch, heads, dk, dv), F32),
            jax.ShapeDtypeStruct((n_dec, r_heads * r_dv), BF16),
            jax.ShapeDtypeStruct(ret_state_dec.shape, ret_state_dec.dtype),
        ],
        scratch_shapes=[
            pltpu.VMEM((heads, dk, dv), F32),
            pltpu.VMEM(((levels + 2) * GLA_CHUNK, 3 * GLA_CHUNK), BF16),
            pltpu.VMEM((GLA_CHUNK, GLA_CHUNK), jnp.int32),
        ] + dec_scratch,
        compiler_params=_params(2, claim_all=True),
        name="gla_prompt_ret_decode",
    )(*_segment_arrays(lay, projs, ("qa", "ka", "va", "ga")), log2_decay, gla_norm_w,
      *_segment_arrays(lay, projs_dec, ("qb", "kb", "vb", "gb")), cos_dec, sin_dec,
      log_gamma, ret_norm_w, ret_state_dec)


def _rotary(x, cos, sin):
    half = x.shape[1] // 2
    x1, x2 = x[:, :half], x[:, half:]
    return jnp.concatenate([x1 * cos - x2 * sin, x1 * sin + x2 * cos], axis=1)


def _ret_kernel(*refs, heads, dk, dv, c, dec_dims):
    q_ref, k_ref, v_ref, gb_ref, cos_ref, sin_ref, lg_ref, nw_ref = refs[:8]
    dec_in = refs[8:15]
    o_ref, s_out_ref = refs[15:17]
    dec_out = refs[17:19]
    s_ref, dmat_ref, qdec_ref, kdec_ref = refs[19:23]
    dec_scratch = refs[23:]
    t = pl.program_id(1)
    step = pl.program_id(0) * pl.num_programs(1) + t
    last_step = pl.num_programs(0) * pl.num_programs(1) - 1

    _gla_decode_section(step, last_step, dec_in + dec_out, dec_scratch,
                        heads=dec_dims[0], dk=dec_dims[1], dv=dec_dims[2])

    @pl.when(t == 0)
    def _():
        s_ref[...] = jnp.zeros_like(s_ref)
        ri = lax.broadcasted_iota(jnp.int32, (c, c), 0)
        rj = lax.broadcasted_iota(jnp.int32, (c, c), 1)
        dist = (ri - rj).astype(F32)
        rowl = lax.broadcasted_iota(jnp.int32, (c, V7X_LANES), 0).astype(F32)
        for hh in range(heads):
            lg = lg_ref[hh]
            dmat_ref[hh] = jnp.exp(jnp.where(ri >= rj, dist * lg[:, :1], -jnp.inf))
            qdec_ref[hh] = jnp.exp((rowl + 1.0) * lg)
            kdec_ref[hh] = jnp.exp((float(c - 1) - rowl) * lg)

    ct = q_ref.shape[0]
    kcs = [slice(hh * dk, (hh + 1) * dk) for hh in range(heads)]
    vcs = [slice(hh * dv, (hh + 1) * dv) for hh in range(heads)]

    def chunk(ci, carry):
        rows = pl.ds(pl.multiple_of(ci * c, c), c)
        cos, sin = cos_ref[rows, :], sin_ref[rows, :]
        qrs = [_rotary(q_ref[rows, kc].astype(F32), cos, sin).astype(BF16) for kc in kcs]
        krs = [_rotary(k_ref[rows, kc].astype(F32), cos, sin) * (dk ** -0.5) for kc in kcs]
        scores = [_dot_nt(qrs[hh], krs[hh].astype(BF16)) * dmat_ref[hh] for hh in range(heads)]
        outs = []
        for hh in range(heads):
            qdec = jnp.concatenate([qdec_ref[hh]] * (dv // V7X_LANES), axis=1)
            o = qdec * _dot(qrs[hh], s_ref[hh].astype(BF16))
            outs.append(o + _dot(scores[hh].astype(BF16), v_ref[rows, vcs[hh]]))
        for hh in range(heads):
            kdec = jnp.concatenate([kdec_ref[hh]] * (dk // V7X_LANES), axis=1)
            k_tail = (krs[hh] * kdec).astype(BF16)
            lgv = jnp.concatenate([lg_ref[hh]] * (dv // V7X_LANES), axis=1)
            s_ref[hh] = jnp.exp(float(c) * lgv) * s_ref[hh] + _dot_tn(k_tail, v_ref[rows, vcs[hh]])
        for hh in range(heads):
            _ln_gate_store(outs[hh], nw_ref[...], gb_ref[rows, vcs[hh]].astype(F32), o_ref, rows, vcs[hh])
        return carry

    lax.fori_loop(0, ct // c, chunk, 0)

    @pl.when(t == pl.num_programs(1) - 1)
    def _():
        s_out_ref[0] = s_ref[...]


def _ret_prompt_gla_decode(projs, cos, sin, log_gamma, ret_norm_w, projs_dec, log2_decay_dec, gla_norm_w,
                           gla_state_dec, lay, batch, seq):
    heads, dk, dv = lay["ret_heads"], lay["ret_dk"], lay["ret_dv"]
    qk, vw = heads * dk, heads * dv
    g_heads, g_dk, g_dv = lay["gla_heads"], lay["gla_dk"], lay["gla_dv"]
    c = min(seq, RET_CHUNK)
    ct = min(seq, RET_STEP_CHUNKS * c)
    assert seq % ct == 0 and ct % c == 0
    nt = seq // ct
    half = dk // 2
    n_dec = projs_dec[0].shape[0]
    tps = _decode_plan(n_dec, batch * nt)
    row = lambda b, t: b * nt + t
    dec_rows, dec_state, dec_o, dec_scratch = _decode_specs("gla", lay, n_dec, tps, row)
    kern = functools.partial(_ret_kernel, heads=heads, dk=dk, dv=dv, c=c, dec_dims=(g_heads, g_dk, g_dv))
    return pl.pallas_call(
        kern,
        grid=(batch, nt),
        in_specs=[
            pl.BlockSpec((ct, qk), lambda b, t: (row(b, t), _segment_block(lay, "qb", qk))),
            pl.BlockSpec((ct, qk), lambda b, t: (row(b, t), _segment_block(lay, "kb", qk))),
            pl.BlockSpec((ct, vw), lambda b, t: (row(b, t), _segment_block(lay, "vb", vw))),
            pl.BlockSpec((ct, vw), lambda b, t: (row(b, t), _segment_block(lay, "gb", vw))),
            pl.BlockSpec((ct, half), lambda b, t: (t, 0)),
            pl.BlockSpec((ct, half), lambda b, t: (t, 0)),
            pl.BlockSpec((heads, 1, V7X_LANES), lambda b, t: (0, 0, 0)),
            pl.BlockSpec((1, dv), lambda b, t: (0, 0)),
        ] + dec_rows + [
            pl.BlockSpec((n_dec, g_heads * g_dk), lambda b, t: (0, 0)),
            pl.BlockSpec((1, g_dv), lambda b, t: (0, 0)),
            dec_state,
        ],
        out_specs=[
            pl.BlockSpec((ct, vw), lambda b, t: (row(b, t), 0)),
            pl.BlockSpec((1, heads, dk, dv), lambda b, t: (b, 0, 0, 0)),
            dec_o,
            dec_state,
        ],
        out_shape=[
            jax.ShapeDtypeStruct((batch * seq, vw), BF16),
            jax.ShapeDtypeStruct((batch, heads, dk, dv), F32),
            jax.ShapeDtypeStruct((n_dec, g_heads * g_dv), BF16),
            jax.ShapeDtypeStruct(gla_state_dec.shape, gla_state_dec.dtype),
        ],
        scratch_shapes=[
            pltpu.VMEM((heads, dk, dv), F32),
            pltpu.VMEM((heads, c, c), F32),
            pltpu.VMEM((heads, c, V7X_LANES), F32),
            pltpu.VMEM((heads, c, V7X_LANES), F32),
        ] + dec_scratch,
        compiler_params=_params(2, claim_all=True),
        name="ret_prompt_gla_decode",
    )(*_segment_arrays(lay, projs, ("qb", "kb", "vb", "gb")), cos, sin, log_gamma, ret_norm_w,
      *_segment_arrays(lay, projs_dec, ("qa", "ka", "va", "ga")), log2_decay_dec, gla_norm_w, gla_state_dec)


def _merge_kernel(oap_ref, obp_ref, oas_ref, obs_ref, wa_ref, wb_ref, g0p_ref, g1p_ref, g0s_ref, g1s_ref, wnext_ref,
                  mp_ref, ms_ref, wnext_bf_ref, wa_bf, wb_bf):
    wnext_bf_ref[...] = wnext_ref[...].astype(BF16)

    def merged(oa, ob, g0, g1):
        ya = _dot(oa, wa_bf[...])
        yb = _dot(ob, wb_bf[...])
        return _sigmoid(g0.astype(F32)) * ya + _sigmoid(g1.astype(F32)) * yb

    @pl.when(pl.program_id(1) == 0)
    def _():
        wa_bf[...] = wa_ref[...].astype(BF16)
        wb_bf[...] = wb_ref[...].astype(BF16)
        ms_ref[...] = merged(oas_ref[...], obs_ref[...], g0s_ref[...], g1s_ref[...]).astype(ms_ref.dtype)

    mp_ref[...] = merged(oap_ref[...], obp_ref[...], g0p_ref[...], g1p_ref[...]).astype(mp_ref.dtype)


def _slab_specs(w_next, n_steps, step_of):
    kn, dn = w_next.shape
    assert kn % n_steps == 0 and (kn // n_steps) % 16 == 0, (kn, n_steps)
    slab = kn // n_steps
    spec = pl.BlockSpec((slab, dn), lambda j, i: (step_of(j, i), 0))
    return spec, spec, jax.ShapeDtypeStruct((kn, dn), BF16), _nbytes((slab, dn), F32) + _nbytes((slab, dn), BF16)


def _merge(oa_p, ob_p, oa_s, ob_s, wa, wb, projs_p, projs_s, lay, w_next):
    proj_p, proj_s = projs_p[lay["cols"]["mg"][0]], projs_s[lay["cols"]["mg"][0]]
    m_p, ka = oa_p.shape
    kb = ob_p.shape[1]
    tail = oa_s.shape[0]
    d = wa.shape[1]
    tm = _row_tile(m_p, ROW_TILE)
    tn = min(d, 1024)
    mg = lay["cols"]["mg"][1]
    assert d % tn == 0 and mg % tn == 0
    g0 = mg // tn
    g1 = (mg + d) // tn
    n_m = m_p // tm
    slab_in, slab_out, slab_shape, slab_bytes = _slab_specs(w_next, (d // tn) * n_m, lambda j, i: j * n_m + i)
    return pl.pallas_call(
        _merge_kernel,
        grid=(d // tn, n_m),
        in_specs=[
            pl.BlockSpec((tm, ka), lambda j, i: (i, 0)),
            pl.BlockSpec((tm, kb), lambda j, i: (i, 0)),
            pl.BlockSpec((tail, ka), lambda j, i: (0, 0)),
            pl.BlockSpec((tail, kb), lambda j, i: (0, 0)),
            pl.BlockSpec((ka, tn), lambda j, i: (0, j)),
            pl.BlockSpec((kb, tn), lambda j, i: (0, j)),
            pl.BlockSpec((tm, tn), lambda j, i: (i, g0 + j)),
            pl.BlockSpec((tm, tn), lambda j, i: (i, g1 + j)),
            pl.BlockSpec((tail, tn), lambda j, i: (0, g0 + j)),
            pl.BlockSpec((tail, tn), lambda j, i: (0, g1 + j)),
            slab_in,
        ],
        out_specs=[pl.BlockSpec((tm, tn), lambda j, i: (i, j)), pl.BlockSpec((tail, tn), lambda j, i: (0, j)),
                   slab_out],
        out_shape=[jax.ShapeDtypeStruct((m_p, d), BF16), jax.ShapeDtypeStruct((tail, d), BF16), slab_shape],
        scratch_shapes=[pltpu.VMEM((ka, tn), BF16), pltpu.VMEM((kb, tn), BF16)],
        compiler_params=_params(
            2, _nbytes((tm, ka), BF16), _nbytes((tm, kb), BF16), _nbytes((tail, ka), BF16), _nbytes((tail, kb), BF16),
            _nbytes((ka, tn), F32), _nbytes((kb, tn), F32), 3 * _nbytes((tm, tn), BF16), 3 * _nbytes((tail, tn), BF16),
            slab_bytes,
            scratch_bytes=_nbytes((ka, tn), BF16) + _nbytes((kb, tn), BF16) + 3 * _nbytes((tm, tn), F32)),
        name="merge",
    )(oa_p, ob_p, oa_s, ob_s, wa, wb, proj_p, proj_p, proj_s, proj_s, w_next)


def _proj_res_norm_kernel(*refs, emit_sum, group, n_blocks):
    ap_refs, as_refs, w_refs = refs[:group], refs[group:2 * group], refs[2 * group:3 * group]
    resp_ref, ress_ref, nw_ref = refs[3 * group:3 * group + 3]
    out_refs = refs[3 * group + 3:]
    n_out = 2 if emit_sum else 1
    outs_p, outs_s = out_refs[:n_out], out_refs[n_out:]
    i = pl.program_id(0)
    k = pl.program_id(1)
    last_k = k == pl.num_programs(1) - 1
    d = w_refs[0].shape[1]
    col_chunk = min(d, 512)
    rest = n_blocks % group
    n_steps = pl.cdiv(n_blocks, group)

    def step(a_refs, res_ref, outs):
        acc_ref = outs[0]
        nrow = acc_ref.shape[0]
        row_chunk = min(nrow, 128)
        assert nrow % row_chunk == 0

        def accumulate(n_used, first):
            a = [a_refs[s][...] for s in range(n_used)]
            for c in range(d // col_chunk):
                cs = slice(c * col_chunk, (c + 1) * col_chunk)
                part = _dot(a[0], w_refs[0][:, cs])
                for s in range(1, n_used):
                    part = part + _dot(a[s], w_refs[s][:, cs])
                if first:
                    acc_ref[:, cs] = part
                else:
                    acc_ref[:, cs] += part

        if n_steps == 1:
            accumulate(n_blocks, True)
        else:
            later_k = k > 0
            pl.when(k == 0)(lambda: accumulate(group, True))
            if rest == 0:
                pl.when(later_k)(lambda: accumulate(group, False))
            else:
                pl.when(jnp.logical_and(later_k, jnp.logical_not(last_k)))(lambda: accumulate(group, False))
                pl.when(last_k)(lambda: accumulate(rest, False))

        @pl.when(last_k)
        def _():
            def body(c, carry):
                rr = pl.ds(pl.multiple_of(c * row_chunk, row_chunk), row_chunk)
                x = acc_ref[rr, :] + res_ref[rr, :]
                y = _rmsnorm_rows(x, nw_ref[...])
                if emit_sum:
                    acc_ref[rr, :] = x
                    outs[1][rr, :] = y.astype(outs[1].dtype)
                else:
                    acc_ref[rr, :] = y
                return carry

            lax.fori_loop(0, nrow // row_chunk, body, 0)

    step(ap_refs, resp_ref, outs_p)

    @pl.when(i == pl.num_programs(0) - 1)
    def _():
        step(as_refs, ress_ref, outs_s)


def _proj_res_norm(a_p, a_s, w, res_p, res_s, norm_w, emit_sum, group):
    m_p, kdim = a_p.shape
    tail = a_s.shape[0]
    d = w.shape[1]
    tm = _row_tile(m_p, ROW_TILE)
    tk = min(kdim, K_BLOCK)
    assert kdim % tk == 0
    n_blocks = kdim // tk
    group = min(group, n_blocks)
    n_steps = pl.cdiv(n_blocks, group)
    blk = lambda k, s: jnp.minimum(k * group + s, n_blocks - 1)
    p_spec = pl.BlockSpec((tm, d), lambda i, k: (i, 0))
    s_spec = pl.BlockSpec((tail, d), lambda i, k: (0, 0))
    s3_spec = pl.BlockSpec((tail, None, d), lambda i, k: (0, 0, 0))
    res_s_spec = s3_spec if res_s.ndim == 3 else s_spec
    out_specs = [p_spec, s3_spec]
    out_shape = [jax.ShapeDtypeStruct((m_p, d), F32), jax.ShapeDtypeStruct((tail, 1, d), F32)]
    assert w.dtype == BF16
    blocks = [group * _nbytes((tm, tk), BF16), group * _nbytes((tail, tk), BF16), group * _nbytes((tk, d), BF16),
              2 * _nbytes((tm, d), F32), 2 * _nbytes((tail, d), F32)]
    if emit_sum:
        out_specs = [p_spec, p_spec, s_spec, s_spec]
        out_shape = [out_shape[0], jax.ShapeDtypeStruct((m_p, d), BF16),
                     jax.ShapeDtypeStruct((tail, d), F32), jax.ShapeDtypeStruct((tail, d), BF16)]
        blocks += [_nbytes((tm, d), BF16), _nbytes((tail, d), BF16)]
    in_specs = (
        [pl.BlockSpec((tm, tk), lambda i, k, s=s: (i, blk(k, s))) for s in range(group)]
        + [pl.BlockSpec((tail, tk), lambda i, k, s=s: (0, blk(k, s))) for s in range(group)]
        + [pl.BlockSpec((tk, d), lambda i, k, s=s: (blk(k, s), 0)) for s in range(group)]
        + [p_spec, res_s_spec, pl.BlockSpec((1, d), lambda i, k: (0, 0))])
    return pl.pallas_call(
        functools.partial(_proj_res_norm_kernel, emit_sum=emit_sum, group=group, n_blocks=n_blocks),
        grid=(m_p // tm, n_steps),
        in_specs=in_specs,
        out_specs=out_specs,
        out_shape=out_shape,
        compiler_params=_params(2, *blocks),
        name="proj_res_norm",
    )(*([a_p] * group + [a_s] * group + [w] * group), res_p, res_s, norm_w.reshape(1, d))


def _swiglu_kernel(hp_ref, hs_ref, wg_ref, wu_ref, wnext_ref, op_ref, os_ref, wnext_bf_ref, wg_bf, wu_bf):
    wnext_bf_ref[...] = wnext_ref[...].astype(BF16)

    tn = wg_bf.shape[1]
    col_chunk = min(tn, 256)

    def act(h_ref, o_ref):
        h = h_ref[...]
        for c in range(tn // col_chunk):
            cs = slice(c * col_chunk, (c + 1) * col_chunk)
            a = _dot(h, wg_bf[:, cs])
            b = _dot(h, wu_bf[:, cs])
            o_ref[:, cs] = (_silu(a) * b).astype(o_ref.dtype)

    @pl.when(pl.program_id(1) == 0)
    def _():
        wg_bf[...] = wg_ref[...].astype(BF16)
        wu_bf[...] = wu_ref[...].astype(BF16)
        act(hs_ref, os_ref)

    act(hp_ref, op_ref)


def _swiglu(h_p, h_s, wg, wu, w_next):
    m_p, d = h_p.shape
    tail = h_s.shape[0]
    f = wg.shape[1]
    tm = _row_tile(m_p, ROW_TILE_WIDE)
    tn = 512 if f % 512 == 0 else 256
    assert f % tn == 0
    n_m = m_p // tm
    slab_in, slab_out, slab_shape, slab_bytes = _slab_specs(w_next, (f // tn) * n_m, lambda j, i: j * n_m + i)
    return pl.pallas_call(
        _swiglu_kernel,
        grid=(f // tn, n_m),
        in_specs=[
            pl.BlockSpec((tm, d), lambda j, i: (i, 0)),
            pl.BlockSpec((tail, d), lambda j, i: (0, 0)),
            pl.BlockSpec((d, tn), lambda j, i: (0, j)),
            pl.BlockSpec((d, tn), lambda j, i: (0, j)),
            slab_in,
        ],
        out_specs=[pl.BlockSpec((tm, tn), lambda j, i: (i, j)), pl.BlockSpec((tail, tn), lambda j, i: (0, j)),
                   slab_out],
        out_shape=[jax.ShapeDtypeStruct((m_p, f), BF16), jax.ShapeDtypeStruct((tail, f), BF16), slab_shape],
        scratch_shapes=[pltpu.VMEM((d, tn), BF16), pltpu.VMEM((d, tn), BF16)],
        compiler_params=_params(
            2, _nbytes((tm, d), BF16), _nbytes((tail, d), BF16), 2 * _nbytes((d, tn), F32),
            _nbytes((tm, tn), BF16), _nbytes((tail, tn), BF16), slab_bytes,
            scratch_bytes=2 * _nbytes((d, tn), BF16) + 3 * _nbytes((tm, tn), F32)),
        name="swiglu",
    )(h_p, h_s, wg, wu, w_next)


def _layout(d_model, in_width, state_gla, state_ret, gate_rank):
    _, _, gh, gdk, gdv = state_gla.shape
    _, _, rh, rdk, rdv = state_ret.shape
    gqk, gv, rqk, rv = gh * gdk, gh * gdv, rh * rdk, rh * rdv
    lay = dict(gla_heads=gh, gla_dk=gdk, gla_dv=gdv, ret_heads=rh, ret_dk=rdk, ret_dv=rdv, rank=gate_rank)
    off = 0
    for name, width in (("qa", gqk), ("ka", gqk), ("va", gv), ("ga", gv), ("qb", rqk), ("kb", rqk),
                        ("vb", rv), ("gb", rv), ("mg", 2 * d_model)):
        lay[name] = off
        off += width
    lay["out_cols"] = off
    lay["plain_cols"] = 2 * gqk + gv
    lay["gd_src"] = lay["plain_cols"]
    assert lay["gd_src"] % V7X_LANES == 0 and gate_rank <= V7X_LANES
    assert in_width == off + gate_rank
    lay["widths"] = dict(qa=gqk, ka=gqk, va=gv, ga=gv, qb=rqk, kb=rqk, vb=rv, gb=rv, mg=2 * d_model)
    return lay


def _split_columns(lay, first_cols):
    cols = {}
    for name, width in lay["widths"].items():
        off = lay[name]
        assert off + width <= first_cols or off >= first_cols, "a segment straddles the two arrays"
        cols[name] = (0, off) if off < first_cols else (1, off - first_cols)
    return cols


def _layer(x_p, x_s, st_gla, st_ret, wts, lay, log_gamma, final_norm):
    (norm_mix, w_in, w_gate_up, b_gate, gla_norm_w, w_gla_up, ret_norm_w, w_ret_up, w_out, norm_ffn,
     w_ffn_gate, w_ffn_up, w_ffn_down) = wts
    batch, seq, d = x_p.shape
    rank = lay["rank"]
    gqk = lay["gla_heads"] * lay["gla_dk"]
    bup = b_gate.reshape(1, gqk)
    gnw = gla_norm_w.reshape(1, -1)
    rnw = ret_norm_w.reshape(1, -1)
    tn = 1024 if (lay["out_cols"] % 1024 == 0 and lay["plain_cols"] % 1024 == 0) else 512
    xp = x_p.reshape(batch * seq, d)
    assert x_s.ndim == 3 and x_s.shape[1] == 1, "one new token per decode sequence"
    xs = x_s
    w_in_t = w_in.T

    h_p, h_s, g_p, g_s, cos_p, sin_p, cos_s, sin_s, first_p, first_s = _rmsnorm_gate(
        xp, xs, norm_mix, w_in_t, lay["gd_src"], w_gate_up, bup, seq, lay["ret_dk"] // 2, tn)
    rest_p, rest_s = _in_proj(h_p, h_s, w_in_t, lay["plain_cols"], rank, lay["out_cols"], tn, 1)
    projs_p, projs_s = (first_p, rest_p), (first_s, rest_s)
    lay = dict(lay, cols=_split_columns(lay, tn))
    oa_p, sa_p, ob_s, sb_s = _gla_prompt_ret_decode(
        projs_p, g_p, gnw, projs_s, cos_s, sin_s, log_gamma, rnw, st_ret, lay, batch, seq)
    ob_p, sb_p, oa_s, sa_s = _ret_prompt_gla_decode(
        projs_p, cos_p, sin_p, log_gamma, rnw, projs_s, g_s, gnw, st_gla, lay, batch, seq)
    m_p, m_s, w_out_bf = _merge(oa_p, ob_p, oa_s, ob_s, w_gla_up, w_ret_up, projs_p, projs_s, lay, w_out)
    x1_p, h2_p, x1_s, h2_s = _proj_res_norm(m_p, m_s, w_out_bf, xp, xs, norm_ffn, True, 2)
    act_p, act_s, w_down_bf = _swiglu(h2_p, h2_s, w_ffn_gate, w_ffn_up, w_ffn_down)
    y_p, y_s = _proj_res_norm(act_p, act_s, w_down_bf, x1_p, x1_s, final_norm, False, 3)
    return (y_p, sa_p, sb_p), (y_s, sa_s, sb_s)


def kernel(x_prompt, x_sample, state_gla, state_ret, norm_mix, w_in, w_gla_gate_up, b_gla_gate, gla_norm_w,
           w_gla_up, ret_norm_w, w_ret_up, w_out, norm_ffn, w_ffn_gate, w_ffn_up, w_ffn_down, norm_final):
    depth = w_in.shape[0]
    assert depth == 1, "single-layer trunk"
    batch, seq, d = x_prompt.shape
    lay = _layout(d, w_in.shape[-1], state_gla, state_ret, w_gla_gate_up.shape[1])
    rh, rdk = lay["ret_heads"], lay["ret_dk"]
    assert rdk // 2 == V7X_LANES
    lg = jnp.log1p(-jnp.exp(jnp.linspace(math.log(1.0 / 32), math.log(1.0 / 512), rh))).astype(F32)
    log_gamma = jnp.broadcast_to(lg[:, None, None], (rh, 1, V7X_LANES))

    wts = (norm_mix[0], w_in[0], w_gla_gate_up[0], b_gla_gate[0], gla_norm_w[0], w_gla_up[0], ret_norm_w[0],
           w_ret_up[0], w_out[0], norm_ffn[0], w_ffn_gate[0], w_ffn_up[0], w_ffn_down[0])
    (y_p, ga_p, re_p), (y_s, ga_s, re_s) = _layer(
        x_prompt, x_sample, state_gla[0], state_ret[0], wts, lay, log_gamma, norm_final)

    sd = state_gla.dtype
    return (y_p.reshape(batch, seq, d), y_s.reshape(x_sample.shape),
            ga_p[None].astype(sd), re_p[None].astype(state_ret.dtype),
            ga_s[None].astype(sd), re_s[None].astype(state_ret.dtype))
```

```python
import functools
import math

import numpy as np
import jax
import jax.numpy as jnp
from jax import lax
from jax.experimental import pallas as pl
from jax.experimental.pallas import tpu as pltpu

EPS = 1e-6
ROPE_BASE = 10000.0
GLA_GATE_NORM = 16.0
PAST_LEN = 16384

V7X_LANES = 128
V7X_VMEM_REQUEST_CAP = 60000 * 1024
COMPILER_SCRATCH_BYTES = 12 * 1024 * 1024

GLA_CHUNK = 64
GLA_STEP_CHUNKS = 8
LOG2_E = 1.4426950408889634
RET_CHUNK = 128
RET_STEP_CHUNKS = 4
ROW_TILE = 1024
ROW_TILE_WIDE = 2048
K_BLOCK = 512

BF16 = jnp.bfloat16
F32 = jnp.float32


def _params(n_axes, *block_bytes, scratch_bytes=0, claim_all=False):
    need = 2 * sum(block_bytes) + scratch_bytes + COMPILER_SCRATCH_BYTES
    if claim_all:
        need = V7X_VMEM_REQUEST_CAP
    return pltpu.CompilerParams(
        dimension_semantics=("arbitrary",) * n_axes,
        vmem_limit_bytes=int(min(V7X_VMEM_REQUEST_CAP, need)),
    )


def _nbytes(shape, dtype):
    return int(np.prod(shape)) * jnp.dtype(dtype).itemsize


def _sigmoid(x):
    return 1.0 / (1.0 + jnp.exp(-x))


def _silu(x):
    return x * _sigmoid(x)


def _log_sigmoid(x):
    return jnp.minimum(x, 0.0) - jnp.log(1.0 + jnp.exp(-jnp.abs(x)))


def _dot(a, b):
    return jnp.dot(a, b, preferred_element_type=F32)


def _dot_nt(a, b):
    return lax.dot_general(a, b, (((1,), (1,)), ((), ())), preferred_element_type=F32)


def _dot_tn(a, b):
    return lax.dot_general(a, b, (((0,), (0,)), ((), ())), preferred_element_type=F32)


def _row_tile(m, want):
    t = min(m, want)
    assert m % t == 0, (m, t)
    return t


def _rmsnorm_rows(x, w):
    ms = jnp.mean(x * x, axis=-1, keepdims=True)
    return x * lax.rsqrt(ms + EPS) * w


def _rope_rows(cos_ref, sin_ref, pos0):
    rows, half = cos_ref.shape
    pos = (lax.broadcasted_iota(jnp.int32, (rows, half), 0) + pos0).astype(F32)
    idx = lax.broadcasted_iota(jnp.int32, (rows, half), 1).astype(F32)
    ang = pos * jnp.exp(idx * (-math.log(ROPE_BASE) / half))
    cos_ref[...] = jnp.cos(ang)
    sin_ref[...] = jnp.sin(ang)


def _rmsnorm_kernel(xp_ref, xs_ref, w_ref, wgd_ref, wup_ref, bup_ref, w0_ref,
                    hp_ref, hs_ref, gp_ref, gs_ref, cosp_ref, sinp_ref, cosd_ref, sind_ref, p0p_ref, p0s_ref,
                    wgd_bf, wup_bf, w0_bf):
    i = pl.program_id(0)

    @pl.when(i == 0)
    def _():
        wgd_bf[...] = wgd_ref[...].astype(BF16)
        wup_bf[...] = jnp.zeros_like(wup_bf)
        wup_bf[0:wup_ref.shape[0], :] = wup_ref[...].astype(BF16)
        w0_bf[...] = w0_ref[...].astype(BF16)

    def rows(x, h_ref, g_ref, p0_ref, cos_ref, sin_ref, pos0):
        h = _rmsnorm_rows(x, w_ref[...]).astype(h_ref.dtype)
        h_ref[...] = h
        gd = _dot_nt(h, wgd_bf[...])
        x = _dot(gd.astype(BF16), wup_bf[...]) + bup_ref[...]
        g_ref[...] = _log_sigmoid(x) * (LOG2_E / GLA_GATE_NORM)
        _rope_rows(cos_ref, sin_ref, pos0)
        p0_ref[...] = _dot_nt(h, w0_bf[...]).astype(p0_ref.dtype)

    rows(xp_ref[...], hp_ref, gp_ref, p0p_ref, cosp_ref, sinp_ref, i * cosp_ref.shape[0])

    @pl.when(i == 0)
    def _():
        rows(xs_ref[...], hs_ref, gs_ref, p0s_ref, cosd_ref, sind_ref, PAST_LEN)


def _rmsnorm_gate(x_p, x_s, w, w_in_t, gate_row0, w_gate_up, bup, seq, half, tn0):
    m_p, d = x_p.shape
    tail = x_s.shape[0]
    rank, gw = w_gate_up.shape
    tm = _row_tile(m_p, ROW_TILE // 2)
    n_steps = m_p // tm
    assert gate_row0 % V7X_LANES == 0 and seq % n_steps == 0 and rank % 16 == 0
    pos_rows = seq // n_steps
    table = pl.BlockSpec((pos_rows, half), lambda i: (i, 0))
    table_dec = pl.BlockSpec((8, half), lambda i: (0, 0))
    return pl.pallas_call(
        _rmsnorm_kernel,
        grid=(n_steps,),
        in_specs=[
            pl.BlockSpec((tm, d), lambda i: (i, 0)),
            pl.BlockSpec((tail, None, d), lambda i: (0, 0, 0)),
            pl.BlockSpec((1, d), lambda i: (0, 0)),
            pl.BlockSpec((V7X_LANES, d), lambda i: (gate_row0 // V7X_LANES, 0)),
            pl.BlockSpec((rank, gw), lambda i: (0, 0)),
            pl.BlockSpec((1, gw), lambda i: (0, 0)),
            pl.BlockSpec((tn0, d), lambda i: (0, 0)),
        ],
        out_specs=[
            pl.BlockSpec((tm, d), lambda i: (i, 0)), pl.BlockSpec((tail, d), lambda i: (0, 0)),
            pl.BlockSpec((tm, gw), lambda i: (i, 0)), pl.BlockSpec((tail, gw), lambda i: (0, 0)),
            table, table, table_dec, table_dec,
            pl.BlockSpec((tm, tn0), lambda i: (i, 0)), pl.BlockSpec((tail, tn0), lambda i: (0, 0)),
        ],
        out_shape=[
            jax.ShapeDtypeStruct((m_p, d), BF16), jax.ShapeDtypeStruct((tail, d), BF16),
            jax.ShapeDtypeStruct((m_p, gw), F32), jax.ShapeDtypeStruct((tail, gw), F32),
            jax.ShapeDtypeStruct((seq, half), F32), jax.ShapeDtypeStruct((seq, half), F32),
            jax.ShapeDtypeStruct((8, half), F32), jax.ShapeDtypeStruct((8, half), F32),
            jax.ShapeDtypeStruct((m_p, tn0), BF16), jax.ShapeDtypeStruct((tail, tn0), BF16),
        ],
        scratch_shapes=[pltpu.VMEM((V7X_LANES, d), BF16), pltpu.VMEM((V7X_LANES, gw), BF16),
                        pltpu.VMEM((tn0, d), BF16)],
        compiler_params=_params(1, claim_all=True),
        name="rmsnorm_gate",
    )(x_p, x_s, w.reshape(1, d), w_in_t, w_gate_up, bup, w_in_t)


def _in_proj_kernel(hp_ref, hs_ref, wm_ref, wn_ref, op_ref, os_ref, wbf_ref, *, n_plain, shift, first_tile):
    j = pl.program_id(0) + first_tile
    i = pl.program_id(1)
    tn = wbf_ref.shape[0]

    @pl.when(jnp.logical_and(i == 0, j < n_plain))
    def _():
        wbf_ref[...] = wm_ref[...].astype(BF16)

    @pl.when(jnp.logical_and(i == 0, j >= n_plain))
    def _():
        wbf_ref[0:tn - shift, :] = wm_ref[shift:tn, :].astype(BF16)
        wbf_ref[tn - shift:tn, :] = wn_ref[...].astype(BF16)

    @pl.when(i == 0)
    def _():
        os_ref[...] = _dot_nt(hs_ref[...], wbf_ref[...]).astype(os_ref.dtype)

    op_ref[...] = _dot_nt(hp_ref[...], wbf_ref[...]).astype(op_ref.dtype)


def _in_proj(h_p, h_s, w_in_t, plain_cols, shift, out_cols, tn, first_tile):
    m_p, d = h_p.shape
    tail = h_s.shape[0]
    tm = _row_tile(m_p, ROW_TILE_WIDE)
    assert plain_cols % tn == 0 and out_cols % tn == 0 and tn % shift == 0 and shift % 8 == 0
    n_plain = plain_cols // tn
    n_tiles = out_cols // tn - first_tile
    kern = functools.partial(_in_proj_kernel, n_plain=n_plain, shift=shift, first_tile=first_tile)
    return pl.pallas_call(
        kern,
        grid=(n_tiles, m_p // tm),
        in_specs=[
            pl.BlockSpec((tm, d), lambda j, i: (i, 0)),
            pl.BlockSpec((tail, d), lambda j, i: (0, 0)),
            pl.BlockSpec((tn, d), lambda j, i: (j + first_tile, 0)),
            pl.BlockSpec((shift, d), lambda j, i: ((j + first_tile + 1) * (tn // shift), 0)),
        ],
        out_specs=[pl.BlockSpec((tm, tn), lambda j, i: (i, j)), pl.BlockSpec((tail, tn), lambda j, i: (0, j))],
        out_shape=[jax.ShapeDtypeStruct((m_p, n_tiles * tn), BF16), jax.ShapeDtypeStruct((tail, n_tiles * tn), BF16)],
        scratch_shapes=[pltpu.VMEM((tn, d), BF16)],
        compiler_params=_params(
            2, _nbytes((tm, d), BF16), _nbytes((tail, d), BF16), _nbytes((tn, d), F32), _nbytes((shift, d), F32),
            _nbytes((tm, tn), BF16), _nbytes((tail, tn), BF16),
            scratch_bytes=_nbytes((tn, d), BF16) + _nbytes((tm, tn), F32)),
        name="in_proj",
    )(h_p, h_s, w_in_t, w_in_t)


def _prefix_sum_rows(sel3_bf16, g):
    g0 = g.astype(BF16)
    r1 = g - g0.astype(F32)
    g1 = r1.astype(BF16)
    g2 = (r1 - g1.astype(F32)).astype(BF16)
    return _dot(sel3_bf16, jnp.concatenate([g0, g1, g2], axis=0))


def _lane_bcast_cols(row, n):
    parts = []
    for c in range(n // V7X_LANES):
        tile = jnp.broadcast_to(row[:, c * V7X_LANES:(c + 1) * V7X_LANES], (V7X_LANES, V7X_LANES))
        parts.append(tile.T)
    return parts[0] if len(parts) == 1 else jnp.concatenate(parts, axis=0)


def _rms_gate_store(o, w, gate, out_ref, rows, cols):
    ms = jnp.mean(o * o, axis=-1, keepdims=True)
    y = o * lax.rsqrt(ms + EPS) * w
    out_ref[rows, cols] = (y * _silu(gate)).astype(out_ref.dtype)


def _ln_gate_store(o, w, gate, out_ref, rows, cols):
    mu = jnp.mean(o, axis=-1, keepdims=True)
    dlt = o - mu
    var = jnp.mean(dlt * dlt, axis=-1, keepdims=True)
    y = dlt * lax.rsqrt(var + EPS) * w
    out_ref[rows, cols] = (y * _silu(gate)).astype(out_ref.dtype)


def _token_selectors(n_tok):
    assert 3 * n_tok <= V7X_LANES
    j = lax.broadcasted_iota(jnp.int32, (V7X_LANES, V7X_LANES), 0)
    sel = []
    for t in range(n_tok):
        hit = jnp.logical_or(j == t, jnp.logical_or(j == n_tok + t, j == 2 * n_tok + t))
        sel.append(jnp.where(hit, 1.0, 0.0).astype(BF16))
    return jnp.stack(sel, axis=0)


def _column_source(x):
    n_tok, w = x.shape
    hi = x.astype(BF16).astype(F32)
    r1 = x - hi
    mid = r1.astype(BF16).astype(F32)
    lo = (r1 - mid).astype(BF16).astype(F32)
    x3 = jnp.concatenate([hi, mid, lo, jnp.zeros((V7X_LANES - 3 * n_tok, w), F32)], axis=0)
    parts = [x3[:, c * V7X_LANES:(c + 1) * V7X_LANES].T for c in range(w // V7X_LANES)]
    return (parts[0] if len(parts) == 1 else jnp.concatenate(parts, axis=0)).astype(BF16)


def _decode_advance(tok0, decay_rows_fn, decay_const_fn, k_ref, q_ref, v_ref, s_in_ref, s_out_ref, o_ref, sel_ref,
                    *, heads, dk, dv):
    n_tok = s_in_ref.shape[0]
    reps = dv // V7X_LANES
    rows = pl.ds(pl.multiple_of(tok0, n_tok), n_tok)

    def cols(src, tt):
        return jnp.concatenate([_dot(src, sel_ref[tt])] * reps, axis=1)

    for hh in range(heads):
        kc = slice(hh * dk, (hh + 1) * dk)
        vc = slice(hh * dv, (hh + 1) * dv)
        k_src, q_src = _column_source(k_ref[rows, kc]), _column_source(q_ref[rows, kc])
        a_src = None if decay_rows_fn is None else _column_source(decay_rows_fn(rows, hh))
        v = v_ref[rows, vc]
        o_rows = []
        for tt in range(n_tok):
            decay = decay_const_fn(hh) if a_src is None else cols(a_src, tt)
            s_new = decay * s_in_ref[tt, hh] + cols(k_src, tt) * v[tt:tt + 1, :]
            s_out_ref[tt, hh] = s_new
            o_rows.append(jnp.sum(cols(q_src, tt) * s_new, axis=0, keepdims=True))
        o_ref[rows, vc] = jnp.concatenate(o_rows, axis=0)


def _decode_plan(n_dec, n_steps):
    assert n_dec % n_steps == 0 and (n_dec // n_steps) % 8 == 0, (n_dec, n_steps)
    return n_dec // n_steps


def _gla_decode_section(step, last_step, refs, scratch, *, heads, dk, dv):
    qd_ref, kd_ref, vd_ref, gad_ref, gd_ref, nw_ref, sd_in_ref, od_ref, sd_out_ref = refs
    a_dec, q_dec, k_dec, v_dec, o_dec, sel_ref = scratch

    @pl.when(step == 0)
    def _():
        a_dec[...] = jnp.exp2(gd_ref[...])
        q_dec[...] = qd_ref[...].astype(F32) * (dk ** -0.5)
        k_dec[...] = kd_ref[...].astype(F32)
        v_dec[...] = vd_ref[...].astype(F32)
        sel_ref[...] = _token_selectors(sd_in_ref.shape[0])

    _decode_advance(step * sd_in_ref.shape[0], lambda rows, hh: a_dec[rows, hh * dk:(hh + 1) * dk], None,
                    k_dec, q_dec, v_dec, sd_in_ref, sd_out_ref, o_dec, sel_ref, heads=heads, dk=dk, dv=dv)

    @pl.when(step == last_step)
    def _():
        n_dec = o_dec.shape[0]
        for hh in range(heads):
            vc = slice(hh * dv, (hh + 1) * dv)
            _rms_gate_store(o_dec[:, vc], nw_ref[...], gad_ref[:, vc].astype(F32), od_ref, slice(0, n_dec), vc)


def _ret_decode_section(step, last_step, refs, scratch, *, heads, dk, dv):
    qd_ref, kd_ref, vd_ref, gbd_ref, cosd_ref, sind_ref, lg_ref, nw_ref, sd_in_ref, od_ref, sd_out_ref = refs
    q_dec, k_dec, v_dec, o_dec, sel_ref = scratch

    @pl.when(step == 0)
    def _():
        cosd, sind = cosd_ref[0:1, :], sind_ref[0:1, :]
        for hh in range(heads):
            kc = slice(hh * dk, (hh + 1) * dk)
            q_dec[:, kc] = _rotary(qd_ref[:, kc].astype(F32), cosd, sind)
            k_dec[:, kc] = _rotary(kd_ref[:, kc].astype(F32), cosd, sind) * (dk ** -0.5)
        v_dec[...] = vd_ref[...].astype(F32)
        sel_ref[...] = _token_selectors(sd_in_ref.shape[0])

    def gamma(hh):
        return jnp.exp(jnp.concatenate([lg_ref[hh]] * (dv // V7X_LANES), axis=1))

    _decode_advance(step * sd_in_ref.shape[0], None, gamma, k_dec, q_dec, v_dec, sd_in_ref, sd_out_ref, o_dec,
                    sel_ref, heads=heads, dk=dk, dv=dv)

    @pl.when(step == last_step)
    def _():
        n_dec = o_dec.shape[0]
        for hh in range(heads):
            vc = slice(hh * dv, (hh + 1) * dv)
            _ln_gate_store(o_dec[:, vc], nw_ref[...], gbd_ref[:, vc].astype(F32), od_ref, slice(0, n_dec), vc)


def _segment_block(lay, name, width):
    offset = lay["cols"][name][1]
    assert offset % width == 0
    return offset // width


def _segment_arrays(lay, arrays, names):
    return tuple(arrays[lay["cols"][n][0]] for n in names)


def _decode_specs(kind, lay, n_dec, tps, row):
    heads, dk, dv = lay[kind + "_heads"], lay[kind + "_dk"], lay[kind + "_dv"]
    qk, vw = heads * dk, heads * dv
    names = ("qa", "ka", "va", "ga") if kind == "gla" else ("qb", "kb", "vb", "gb")
    widths = (qk, qk, vw, vw)
    rows_in = [pl.BlockSpec((n_dec, w), lambda b, t, c=_segment_block(lay, n, w): (0, c))
               for n, w in zip(names, widths)]
    state = pl.BlockSpec((tps, heads, dk, dv), lambda b, t: (row(b, t), 0, 0, 0))
    o_spec = pl.BlockSpec((n_dec, vw), lambda b, t: (0, 0))
    n_qk = 3 if kind == "gla" else 2
    scratch = ([pltpu.VMEM((n_dec, qk), F32)] * n_qk + [pltpu.VMEM((n_dec, vw), F32)] * 2
               + [pltpu.VMEM((tps, V7X_LANES, V7X_LANES), BF16)])
    return rows_in, state, o_spec, scratch


def _gla_sum_matrices(c):
    levels = c.bit_length() - 1
    assert 1 << levels == c
    i = lax.broadcasted_iota(jnp.int32, (c, c), 0)
    j = lax.broadcasted_iota(jnp.int32, (c, c), 1)
    mats = [j <= i]
    for l in range(levels):
        ref = jnp.bitwise_or(jnp.bitwise_and(i, -(2 << l)), 1 << l)
        mats.append(jnp.logical_and(j > jnp.minimum(i, ref), j <= jnp.maximum(i, ref)))
    mats.append(j > i)
    sel = jnp.concatenate([jnp.where(m, 1.0, 0.0).astype(BF16) for m in mats], axis=0)
    return jnp.concatenate([sel, sel, sel], axis=1)


def _pair_level(c):
    levels = c.bit_length() - 1
    i = lax.broadcasted_iota(jnp.int32, (c, c), 0)
    j = lax.broadcasted_iota(jnp.int32, (c, c), 1)
    x = jnp.bitwise_xor(i, j)
    lvl = jnp.zeros((c, c), jnp.int32)
    for l in range(1, levels):
        lvl = lvl + jnp.where(x >= (1 << l), 1, 0)
    return jnp.where(i > j, lvl, jnp.where(i == j, levels, -1))


def _queries_else_keys(q, k, l):
    c = q.shape[0]
    span = 1 << l
    if span >= 8:
        parts = [(q if (b & 1) else k)[b * span:(b + 1) * span, :] for b in range(c // span)]
        return jnp.concatenate(parts, axis=0)
    row = lax.broadcasted_iota(jnp.int32, q.shape, 0)
    return jnp.where(jnp.bitwise_and(row, span) != 0, q, k)


def _gla_level_scores(q, k, sums):
    c = q.shape[0]
    levels = c.bit_length() - 1
    out = []
    for l in range(levels):
        x = _queries_else_keys(q, k, l) * jnp.exp2(sums[(1 + l) * c:(2 + l) * c, :])
        xb = x.astype(BF16)
        out.append(_dot_nt(xb, xb))
    return out


def _gla_chunk_out(q, k, v, sums, level_scores, pair_level, state):
    c = q.shape[0]
    levels = c.bit_length() - 1
    scores = jnp.where(pair_level == levels, jnp.sum(q * k, axis=-1, keepdims=True), 0.0)
    for l in range(levels):
        scores = jnp.where(pair_level == l, level_scores[l], scores)
    o = _dot((q * jnp.exp2(sums[0:c, :])).astype(BF16), state.astype(BF16))
    return o + _dot(scores.astype(BF16), v)


def _gla_next_state(k, v, sums, state):
    c, dk = k.shape
    levels = c.bit_length() - 1
    k_tail = (k * jnp.exp2(sums[(levels + 1) * c:(levels + 2) * c, :])).astype(BF16)
    decay = _lane_bcast_cols(jnp.exp2(sums[c - 1:c, :]), dk)
    decay_full = jnp.concatenate([decay] * (v.shape[1] // V7X_LANES), axis=1)
    return decay_full * state + _dot_tn(k_tail, v)


def _gla_kernel(*refs, heads, dk, dv, dec_dims):
    q_ref, k_ref, v_ref, ga_ref, g_ref, nw_ref = refs[:6]
    dec_in = refs[6:15]
    o_ref, s_out_ref = refs[15:17]
    dec_out = refs[17:19]
    s_ref, mats_ref, lvl_ref = refs[19:22]
    dec_scratch = refs[22:]
    t = pl.program_id(1)
    step = pl.program_id(0) * pl.num_programs(1) + t
    last_step = pl.num_programs(0) * pl.num_programs(1) - 1

    @pl.when(t == 0)
    def _():
        s_ref[...] = jnp.zeros_like(s_ref)
        mats_ref[...] = _gla_sum_matrices(GLA_CHUNK)
        lvl_ref[...] = _pair_level(GLA_CHUNK)

    _ret_decode_section(step, last_step, dec_in + dec_out, dec_scratch,
                        heads=dec_dims[0], dk=dec_dims[1], dv=dec_dims[2])

    ct = q_ref.shape[0]

    kcs = [slice(hh * dk, (hh + 1) * dk) for hh in range(heads)]
    vcs = [slice(hh * dv, (hh + 1) * dv) for hh in range(heads)]
    group = 2 if (ct // GLA_CHUNK) % 2 == 0 else 1

    def chunk_group(cg, carry):
        pair_level = lvl_ref[...]
        rows, sums, qs, ks, lvl_scores = [], [], [], [], []
        for u in range(group):
            r = pl.ds(pl.multiple_of((cg * group + u) * GLA_CHUNK, GLA_CHUNK), GLA_CHUNK)
            rows.append(r)
            sums.append(_prefix_sum_rows(mats_ref[...], g_ref[r, :]))
            qs.append([q_ref[r, kc].astype(F32) * (dk ** -0.5) for kc in kcs])
            ks.append([k_ref[r, kc].astype(F32) for kc in kcs])
            lvl_scores.append([_gla_level_scores(qs[u][hh], ks[u][hh], sums[u][:, kcs[hh]]) for hh in range(heads)])
        outs = []
        for u in range(group):
            r = rows[u]
            outs.append([_gla_chunk_out(qs[u][hh], ks[u][hh], v_ref[r, vcs[hh]], sums[u][:, kcs[hh]],
                                        lvl_scores[u][hh], pair_level, s_ref[hh]) for hh in range(heads)])
            for hh in range(heads):
                s_ref[hh] = _gla_next_state(ks[u][hh], v_ref[r, vcs[hh]], sums[u][:, kcs[hh]], s_ref[hh])
        for u in range(group):
            for hh in range(heads):
                _rms_gate_store(outs[u][hh], nw_ref[...], ga_ref[rows[u], vcs[hh]].astype(F32), o_ref, rows[u],
                                vcs[hh])
        return carry

    lax.fori_loop(0, ct // GLA_CHUNK // group, chunk_group, 0)

    @pl.when(t == pl.num_programs(1) - 1)
    def _():
        s_out_ref[0] = s_ref[...]


def _gla_prompt_ret_decode(projs, log2_decay, gla_norm_w, projs_dec, cos_dec, sin_dec, log_gamma, ret_norm_w,
                           ret_state_dec, lay, batch, seq):
    heads, dk, dv = lay["gla_heads"], lay["gla_dk"], lay["gla_dv"]
    qk, vw = heads * dk, heads * dv
    r_heads, r_dk, r_dv = lay["ret_heads"], lay["ret_dk"], lay["ret_dv"]
    ct = min(seq, GLA_STEP_CHUNKS * GLA_CHUNK)
    levels = GLA_CHUNK.bit_length() - 1
    assert seq % ct == 0 and ct % GLA_CHUNK == 0
    nt = seq // ct
    n_dec = projs_dec[0].shape[0]
    tps = _decode_plan(n_dec, batch * nt)
    row = lambda b, t: b * nt + t
    dec_rows, dec_state, dec_o, dec_scratch = _decode_specs("ret", lay, n_dec, tps, row)
    table_dec = pl.BlockSpec((cos_dec.shape[0], r_dk // 2), lambda b, t: (0, 0))
    kern = functools.partial(_gla_kernel, heads=heads, dk=dk, dv=dv, dec_dims=(r_heads, r_dk, r_dv))
    return pl.pallas_call(
        kern,
        grid=(batch, nt),
        in_specs=[
            pl.BlockSpec((ct, qk), lambda b, t: (row(b, t), _segment_block(lay, "qa", qk))),
            pl.BlockSpec((ct, qk), lambda b, t: (row(b, t), _segment_block(lay, "ka", qk))),
            pl.BlockSpec((ct, vw), lambda b, t: (row(b, t), _segment_block(lay, "va", vw))),
            pl.BlockSpec((ct, vw), lambda b, t: (row(b, t), _segment_block(lay, "ga", vw))),
            pl.BlockSpec((ct, qk), lambda b, t: (row(b, t), 0)),
            pl.BlockSpec((1, dv), lambda b, t: (0, 0)),
        ] + dec_rows + [
            table_dec, table_dec,
            pl.BlockSpec((r_heads, 1, V7X_LANES), lambda b, t: (0, 0, 0)),
            pl.BlockSpec((1, r_dv), lambda b, t: (0, 0)),
            dec_state,
        ],
        out_specs=[
            pl.BlockSpec((ct, vw), lambda b, t: (row(b, t), 0)),
            pl.BlockSpec((1, heads, dk, dv), lambda b, t: (b, 0, 0, 0)),
            dec_o,
            dec_state,
        ],
        out_shape=[
            jax.ShapeDtypeStruct((batch * seq, vw), BF16),
            jax.ShapeDtypeStruct((batch, heads, dk, dv), F32),
            jax.ShapeDtypeStruct((n_dec, r_heads * r_dv), BF16),
            jax.ShapeDtypeStruct(ret_state_dec.shape, ret_state_dec.dtype),
        ],
        scratch_shapes=[
            pltpu.VMEM((heads, dk, dv), F32),
            pltpu.VMEM(((levels + 2) * GLA_CHUNK, 3 * GLA_CHUNK), BF16),
            pltpu.VMEM((GLA_CHUNK, GLA_CHUNK), jnp.int32),
        ] + dec_scratch,
        compiler_params=_params(2, claim_all=True),
        name="gla_prompt_ret_decode",
    )(*_segment_arrays(lay, projs, ("qa", "ka", "va", "ga")), log2_decay, gla_norm_w,
      *_segment_arrays(lay, projs_dec, ("qb", "kb", "vb", "gb")), cos_dec, sin_dec,
      log_gamma, ret_norm_w, ret_state_dec)


def _rotary(x, cos, sin):
    half = x.shape[1] // 2
    x1, x2 = x[:, :half], x[:, half:]
    return jnp.concatenate([x1 * cos - x2 * sin, x1 * sin + x2 * cos], axis=1)


def _ret_kernel(*refs, heads, dk, dv, c, dec_dims):
    q_ref, k_ref, v_ref, gb_ref, cos_ref, sin_ref, lg_ref, nw_ref = refs[:8]
    dec_in = refs[8:15]
    o_ref, s_out_ref = refs[15:17]
    dec_out = refs[17:19]
    s_ref, dmat_ref, qdec_ref, kdec_ref = refs[19:23]
    dec_scratch = refs[23:]
    t = pl.program_id(1)
    step = pl.program_id(0) * pl.num_programs(1) + t
    last_step = pl.num_programs(0) * pl.num_programs(1) - 1

    _gla_decode_section(step, last_step, dec_in + dec_out, dec_scratch,
                        heads=dec_dims[0], dk=dec_dims[1], dv=dec_dims[2])

    @pl.when(t == 0)
    def _():
        s_ref[...] = jnp.zeros_like(s_ref)
        ri = lax.broadcasted_iota(jnp.int32, (c, c), 0)
        rj = lax.broadcasted_iota(jnp.int32, (c, c), 1)
        dist = (ri - rj).astype(F32)
        rowl = lax.broadcasted_iota(jnp.int32, (c, V7X_LANES), 0).astype(F32)
        for hh in range(heads):
            lg = lg_ref[hh]
            dmat_ref[hh] = jnp.exp(jnp.where(ri >= rj, dist * lg[:, :1], -jnp.inf))
            qdec_ref[hh] = jnp.exp((rowl + 1.0) * lg)
            kdec_ref[hh] = jnp.exp((float(c - 1) - rowl) * lg)

    ct = q_ref.shape[0]
    kcs = [slice(hh * dk, (hh + 1) * dk) for hh in range(heads)]
    vcs = [slice(hh * dv, (hh + 1) * dv) for hh in range(heads)]

    def chunk(ci, carry):
        rows = pl.ds(pl.multiple_of(ci * c, c), c)
        cos, sin = cos_ref[rows, :], sin_ref[rows, :]
        qrs = [_rotary(q_ref[rows, kc].astype(F32), cos, sin).astype(BF16) for kc in kcs]
        krs = [_rotary(k_ref[rows, kc].astype(F32), cos, sin) * (dk ** -0.5) for kc in kcs]
        scores = [_dot_nt(qrs[hh], krs[hh].astype(BF16)) * dmat_ref[hh] for hh in range(heads)]
        outs = []
        for hh in range(heads):
            qdec = jnp.concatenate([qdec_ref[hh]] * (dv // V7X_LANES), axis=1)
            o = qdec * _dot(qrs[hh], s_ref[hh].astype(BF16))
            outs.append(o + _dot(scores[hh].astype(BF16), v_ref[rows, vcs[hh]]))
        for hh in range(heads):
            kdec = jnp.concatenate([kdec_ref[hh]] * (dk // V7X_LANES), axis=1)
            k_tail = (krs[hh] * kdec).astype(BF16)
            lgv = jnp.concatenate([lg_ref[hh]] * (dv // V7X_LANES), axis=1)
            s_ref[hh] = jnp.exp(float(c) * lgv) * s_ref[hh] + _dot_tn(k_tail, v_ref[rows, vcs[hh]])
        for hh in range(heads):
            _ln_gate_store(outs[hh], nw_ref[...], gb_ref[rows, vcs[hh]].astype(F32), o_ref, rows, vcs[hh])
        return carry

    lax.fori_loop(0, ct // c, chunk, 0)

    @pl.when(t == pl.num_programs(1) - 1)
    def _():
        s_out_ref[0] = s_ref[...]


def _ret_prompt_gla_decode(projs, cos, sin, log_gamma, ret_norm_w, projs_dec, log2_decay_dec, gla_norm_w,
                           gla_state_dec, lay, batch, seq):
    heads, dk, dv = lay["ret_heads"], lay["ret_dk"], lay["ret_dv"]
    qk, vw = heads * dk, heads * dv
    g_heads, g_dk, g_dv = lay["gla_heads"], lay["gla_dk"], lay["gla_dv"]
    c = min(seq, RET_CHUNK)
    ct = min(seq, RET_STEP_CHUNKS * c)
    assert seq % ct == 0 and ct % c == 0
    nt = seq // ct
    half = dk // 2
    n_dec = projs_dec[0].shape[0]
    tps = _decode_plan(n_dec, batch * nt)
    row = lambda b, t: b * nt + t
    dec_rows, dec_state, dec_o, dec_scratch = _decode_specs("gla", lay, n_dec, tps, row)
    kern = functools.partial(_ret_kernel, heads=heads, dk=dk, dv=dv, c=c, dec_dims=(g_heads, g_dk, g_dv))
    return pl.pallas_call(
        kern,
        grid=(batch, nt),
        in_specs=[
            pl.BlockSpec((ct, qk), lambda b, t: (row(b, t), _segment_block(lay, "qb", qk))),
            pl.BlockSpec((ct, qk), lambda b, t: (row(b, t), _segment_block(lay, "kb", qk))),
            pl.BlockSpec((ct, vw), lambda b, t: (row(b, t), _segment_block(lay, "vb", vw))),
            pl.BlockSpec((ct, vw), lambda b, t: (row(b, t), _segment_block(lay, "gb", vw))),
            pl.BlockSpec((ct, half), lambda b, t: (t, 0)),
            pl.BlockSpec((ct, half), lambda b, t: (t, 0)),
            pl.BlockSpec((heads, 1, V7X_LANES), lambda b, t: (0, 0, 0)),
            pl.BlockSpec((1, dv), lambda b, t: (0, 0)),
        ] + dec_rows + [
            pl.BlockSpec((n_dec, g_heads * g_dk), lambda b, t: (0, 0)),
            pl.BlockSpec((1, g_dv), lambda b, t: (0, 0)),
            dec_state,
        ],
        out_specs=[
            pl.BlockSpec((ct, vw), lambda b, t: (row(b, t), 0)),
            pl.BlockSpec((1, heads, dk, dv), lambda b, t: (b, 0, 0, 0)),
            dec_o,
            dec_state,
        ],
        out_shape=[
            jax.ShapeDtypeStruct((batch * seq, vw), BF16),
            jax.ShapeDtypeStruct((batch, heads, dk, dv), F32),
            jax.ShapeDtypeStruct((n_dec, g_heads * g_dv), BF16),
            jax.ShapeDtypeStruct(gla_state_dec.shape, gla_state_dec.dtype),
        ],
        scratch_shapes=[
            pltpu.VMEM((heads, dk, dv), F32),
            pltpu.VMEM((heads, c, c), F32),
            pltpu.VMEM((heads, c, V7X_LANES), F32),
            pltpu.VMEM((heads, c, V7X_LANES), F32),
        ] + dec_scratch,
        compiler_params=_params(2, claim_all=True),
        name="ret_prompt_gla_decode",
    )(*_segment_arrays(lay, projs, ("qb", "kb", "vb", "gb")), cos, sin, log_gamma, ret_norm_w,
      *_segment_arrays(lay, projs_dec, ("qa", "ka", "va", "ga")), log2_decay_dec, gla_norm_w, gla_state_dec)


def _merge_kernel(oap_ref, obp_ref, oas_ref, obs_ref, wa_ref, wb_ref, g0p_ref, g1p_ref, g0s_ref, g1s_ref, wnext_ref,
                  mp_ref, ms_ref, wnext_bf_ref, wa_bf, wb_bf):
    wnext_bf_ref[...] = wnext_ref[...].astype(BF16)

    def merged(oa, ob, g0, g1):
        ya = _dot(oa, wa_bf[...])
        yb = _dot(ob, wb_bf[...])
        return _sigmoid(g0.astype(F32)) * ya + _sigmoid(g1.astype(F32)) * yb

    @pl.when(pl.program_id(1) == 0)
    def _():
        wa_bf[...] = wa_ref[...].astype(BF16)
        wb_bf[...] = wb_ref[...].astype(BF16)
        ms_ref[...] = merged(oas_ref[...], obs_ref[...], g0s_ref[...], g1s_ref[...]).astype(ms_ref.dtype)

    mp_ref[...] = merged(oap_ref[...], obp_ref[...], g0p_ref[...], g1p_ref[...]).astype(mp_ref.dtype)


def _slab_specs(w_next, n_steps, step_of):
    kn, dn = w_next.shape
    assert kn % n_steps == 0 and (kn // n_steps) % 16 == 0, (kn, n_steps)
    slab = kn // n_steps
    spec = pl.BlockSpec((slab, dn), lambda j, i: (step_of(j, i), 0))
    return spec, spec, jax.ShapeDtypeStruct((kn, dn), BF16), _nbytes((slab, dn), F32) + _nbytes((slab, dn), BF16)


def _merge(oa_p, ob_p, oa_s, ob_s, wa, wb, projs_p, projs_s, lay, w_next):
    proj_p, proj_s = projs_p[lay["cols"]["mg"][0]], projs_s[lay["cols"]["mg"][0]]
    m_p, ka = oa_p.shape
    kb = ob_p.shape[1]
    tail = oa_s.shape[0]
    d = wa.shape[1]
    tm = _row_tile(m_p, ROW_TILE)
    tn = min(d, 1024)
    mg = lay["cols"]["mg"][1]
    assert d % tn == 0 and mg % tn == 0
    g0 = mg // tn
    g1 = (mg + d) // tn
    n_m = m_p // tm
    slab_in, slab_out, slab_shape, slab_bytes = _slab_specs(w_next, (d // tn) * n_m, lambda j, i: j * n_m + i)
    return pl.pallas_call(
        _merge_kernel,
        grid=(d // tn, n_m),
        in_specs=[
            pl.BlockSpec((tm, ka), lambda j, i: (i, 0)),
            pl.BlockSpec((tm, kb), lambda j, i: (i, 0)),
            pl.BlockSpec((tail, ka), lambda j, i: (0, 0)),
            pl.BlockSpec((tail, kb), lambda j, i: (0, 0)),
            pl.BlockSpec((ka, tn), lambda j, i: (0, j)),
            pl.BlockSpec((kb, tn), lambda j, i: (0, j)),
            pl.BlockSpec((tm, tn), lambda j, i: (i, g0 + j)),
            pl.BlockSpec((tm, tn), lambda j, i: (i, g1 + j)),
            pl.BlockSpec((tail, tn), lambda j, i: (0, g0 + j)),
            pl.BlockSpec((tail, tn), lambda j, i: (0, g1 + j)),
            slab_in,
        ],
        out_specs=[pl.BlockSpec((tm, tn), lambda j, i: (i, j)), pl.BlockSpec((tail, tn), lambda j, i: (0, j)),
                   slab_out],
        out_shape=[jax.ShapeDtypeStruct((m_p, d), BF16), jax.ShapeDtypeStruct((tail, d), BF16), slab_shape],
        scratch_shapes=[pltpu.VMEM((ka, tn), BF16), pltpu.VMEM((kb, tn), BF16)],
        compiler_params=_params(
            2, _nbytes((tm, ka), BF16), _nbytes((tm, kb), BF16), _nbytes((tail, ka), BF16), _nbytes((tail, kb), BF16),
            _nbytes((ka, tn), F32), _nbytes((kb, tn), F32), 3 * _nbytes((tm, tn), BF16), 3 * _nbytes((tail, tn), BF16),
            slab_bytes,
            scratch_bytes=_nbytes((ka, tn), BF16) + _nbytes((kb, tn), BF16) + 3 * _nbytes((tm, tn), F32)),
        name="merge",
    )(oa_p, ob_p, oa_s, ob_s, wa, wb, proj_p, proj_p, proj_s, proj_s, w_next)


def _late_residual(n_steps):
    return n_steps >= 3


def _proj_res_norm_kernel(*refs, emit_sum, group, n_blocks):
    ap_refs, as_refs, w_refs = refs[:group], refs[group:2 * group], refs[2 * group:3 * group]
    resp_ref, ress_ref, nw_ref = refs[3 * group:3 * group + 3]
    out_refs = refs[3 * group + 3:]
    n_out = 2 if emit_sum else 1
    outs_p, outs_s = out_refs[:n_out], out_refs[n_out:]
    i = pl.program_id(0)
    k = pl.program_id(1)
    last_k = k == pl.num_programs(1) - 1
    d = w_refs[0].shape[1]
    col_chunk = min(d, 512)
    rest = n_blocks % group
    n_steps = pl.cdiv(n_blocks, group)
    late = _late_residual(n_steps)

    def step(a_refs, res_ref, outs):
        acc_ref = outs[0]
        nrow = acc_ref.shape[0]
        row_chunk = min(nrow, 128)
        assert nrow % row_chunk == 0

        def accumulate(n_used, first):
            a = [a_refs[s][...] for s in range(n_used)]
            for c in range(d // col_chunk):
                cs = slice(c * col_chunk, (c + 1) * col_chunk)
                part = _dot(a[0], w_refs[0][:, cs])
                for s in range(1, n_used):
                    part = part + _dot(a[s], w_refs[s][:, cs])
                if first:
                    acc_ref[:, cs] = part
                else:
                    acc_ref[:, cs] += part

        if not late:
            @pl.when(k == 0)
            def _():
                acc_ref[...] = res_ref[...]

        if n_steps == 1:
            accumulate(n_blocks, late)
        else:
            later_k = k > 0
            pl.when(k == 0)(lambda: accumulate(group, late))
            if rest == 0:
                pl.when(later_k)(lambda: accumulate(group, False))
            else:
                pl.when(jnp.logical_and(later_k, jnp.logical_not(last_k)))(lambda: accumulate(group, False))
                pl.when(last_k)(lambda: accumulate(rest, False))

        @pl.when(last_k)
        def _():
            def body(c, carry):
                rr = pl.ds(pl.multiple_of(c * row_chunk, row_chunk), row_chunk)
                x = acc_ref[rr, :]
                if late:
                    x = x + res_ref[rr, :]
                y = _rmsnorm_rows(x, nw_ref[...])
                if emit_sum:
                    if late:
                        acc_ref[rr, :] = x
                    outs[1][rr, :] = y.astype(outs[1].dtype)
                else:
                    acc_ref[rr, :] = y
                return carry

            lax.fori_loop(0, nrow // row_chunk, body, 0)

    step(ap_refs, resp_ref, outs_p)

    @pl.when(i == pl.num_programs(0) - 1)
    def _():
        step(as_refs, ress_ref, outs_s)


def _proj_res_norm(a_p, a_s, w, res_p, res_s, norm_w, emit_sum, group):
    m_p, kdim = a_p.shape
    tail = a_s.shape[0]
    d = w.shape[1]
    tm = _row_tile(m_p, ROW_TILE)
    tk = min(kdim, K_BLOCK)
    assert kdim % tk == 0
    n_blocks = kdim // tk
    group = min(group, n_blocks)
    n_steps = pl.cdiv(n_blocks, group)
    blk = lambda k, s: jnp.minimum(k * group + s, n_blocks - 1)
    p_spec = pl.BlockSpec((tm, d), lambda i, k: (i, 0))
    s_spec = pl.BlockSpec((tail, d), lambda i, k: (0, 0))
    s3_spec = pl.BlockSpec((tail, None, d), lambda i, k: (0, 0, 0))
    res_s_spec = s3_spec if res_s.ndim == 3 else s_spec
    res_p_spec = p_spec
    if _late_residual(n_steps):
        res_p_spec = pl.BlockSpec(
            (tm, d), lambda i, k: (jnp.where(k == n_steps - 1, i, jnp.maximum(i - 1, 0)), 0))
    out_specs = [p_spec, s3_spec]
    out_shape = [jax.ShapeDtypeStruct((m_p, d), F32), jax.ShapeDtypeStruct((tail, 1, d), F32)]
    assert w.dtype == BF16
    blocks = [group * _nbytes((tm, tk), BF16), group * _nbytes((tail, tk), BF16), group * _nbytes((tk, d), BF16),
              2 * _nbytes((tm, d), F32), 2 * _nbytes((tail, d), F32)]
    if emit_sum:
        out_specs = [p_spec, p_spec, s_spec, s_spec]
        out_shape = [out_shape[0], jax.ShapeDtypeStruct((m_p, d), BF16),
                     jax.ShapeDtypeStruct((tail, d), F32), jax.ShapeDtypeStruct((tail, d), BF16)]
        blocks += [_nbytes((tm, d), BF16), _nbytes((tail, d), BF16)]
    in_specs = (
        [pl.BlockSpec((tm, tk), lambda i, k, s=s: (i, blk(k, s))) for s in range(group)]
        + [pl.BlockSpec((tail, tk), lambda i, k, s=s: (0, blk(k, s))) for s in range(group)]
        + [pl.BlockSpec((tk, d), lambda i, k, s=s: (blk(k, s), 0)) for s in range(group)]
        + [res_p_spec, res_s_spec, pl.BlockSpec((1, d), lambda i, k: (0, 0))])
    return pl.pallas_call(
        functools.partial(_proj_res_norm_kernel, emit_sum=emit_sum, group=group, n_blocks=n_blocks),
        grid=(m_p // tm, n_steps),
        in_specs=in_specs,
        out_specs=out_specs,
        out_shape=out_shape,
        compiler_params=_params(2, *blocks),
        name="proj_res_norm",
    )(*([a_p] * group + [a_s] * group + [w] * group), res_p, res_s, norm_w.reshape(1, d))


def _swiglu_kernel(hp_ref, hs_ref, wg_ref, wu_ref, wnext_ref, op_ref, os_ref, wnext_bf_ref, wg_bf, wu_bf):
    wnext_bf_ref[...] = wnext_ref[...].astype(BF16)

    tn = wg_bf.shape[1]
    col_chunk = min(tn, 256)

    def act(h_ref, o_ref):
        h = h_ref[...]
        for c in range(tn // col_chunk):
            cs = slice(c * col_chunk, (c + 1) * col_chunk)
            a = _dot(h, wg_bf[:, cs])
            b = _dot(h, wu_bf[:, cs])
            o_ref[:, cs] = (_silu(a) * b).astype(o_ref.dtype)

    @pl.when(pl.program_id(1) == 0)
    def _():
        wg_bf[...] = wg_ref[...].astype(BF16)
        wu_bf[...] = wu_ref[...].astype(BF16)
        act(hs_ref, os_ref)

    act(hp_ref, op_ref)


def _swiglu(h_p, h_s, wg, wu, w_next):
    m_p, d = h_p.shape
    tail = h_s.shape[0]
    f = wg.shape[1]
    tm = _row_tile(m_p, ROW_TILE_WIDE)
    tn = 512 if f % 512 == 0 else 256
    assert f % tn == 0
    n_m = m_p // tm
    slab_in, slab_out, slab_shape, slab_bytes = _slab_specs(w_next, (f // tn) * n_m, lambda j, i: j * n_m + i)
    return pl.pallas_call(
        _swiglu_kernel,
        grid=(f // tn, n_m),
        in_specs=[
            pl.BlockSpec((tm, d), lambda j, i: (i, 0)),
            pl.BlockSpec((tail, d), lambda j, i: (0, 0)),
            pl.BlockSpec((d, tn), lambda j, i: (0, j)),
            pl.BlockSpec((d, tn), lambda j, i: (0, j)),
            slab_in,
        ],
        out_specs=[pl.BlockSpec((tm, tn), lambda j, i: (i, j)), pl.BlockSpec((tail, tn), lambda j, i: (0, j)),
                   slab_out],
        out_shape=[jax.ShapeDtypeStruct((m_p, f), BF16), jax.ShapeDtypeStruct((tail, f), BF16), slab_shape],
        scratch_shapes=[pltpu.VMEM((d, tn), BF16), pltpu.VMEM((d, tn), BF16)],
        compiler_params=_params(
            2, _nbytes((tm, d), BF16), _nbytes((tail, d), BF16), 2 * _nbytes((d, tn), F32),
            _nbytes((tm, tn), BF16), _nbytes((tail, tn), BF16), slab_bytes,
            scratch_bytes=2 * _nbytes((d, tn), BF16) + 3 * _nbytes((tm, tn), F32)),
        name="swiglu",
    )(h_p, h_s, wg, wu, w_next)


def _layout(d_model, in_width, state_gla, state_ret, gate_rank):
    _, _, gh, gdk, gdv = state_gla.shape
    _, _, rh, rdk, rdv = state_ret.shape
    gqk, gv, rqk, rv = gh * gdk, gh * gdv, rh * rdk, rh * rdv
    lay = dict(gla_heads=gh, gla_dk=gdk, gla_dv=gdv, ret_heads=rh, ret_dk=rdk, ret_dv=rdv, rank=gate_rank)
    off = 0
    for name, width in (("qa", gqk), ("ka", gqk), ("va", gv), ("ga", gv), ("qb", rqk), ("kb", rqk),
                        ("vb", rv), ("gb", rv), ("mg", 2 * d_model)):
        lay[name] = off
        off += width
    lay["out_cols"] = off
    lay["plain_cols"] = 2 * gqk + gv
    lay["gd_src"] = lay["plain_cols"]
    assert lay["gd_src"] % V7X_LANES == 0 and gate_rank <= V7X_LANES
    assert in_width == off + gate_rank
    lay["widths"] = dict(qa=gqk, ka=gqk, va=gv, ga=gv, qb=rqk, kb=rqk, vb=rv, gb=rv, mg=2 * d_model)
    return lay


def _split_columns(lay, first_cols):
    cols = {}
    for name, width in lay["widths"].items():
        off = lay[name]
        assert off + width <= first_cols or off >= first_cols, "a segment straddles the two arrays"
        cols[name] = (0, off) if off < first_cols else (1, off - first_cols)
    return cols


def _layer(x_p, x_s, st_gla, st_ret, wts, lay, log_gamma, final_norm):
    (norm_mix, w_in, w_gate_up, b_gate, gla_norm_w, w_gla_up, ret_norm_w, w_ret_up, w_out, norm_ffn,
     w_ffn_gate, w_ffn_up, w_ffn_down) = wts
    batch, seq, d = x_p.shape
    rank = lay["rank"]
    gqk = lay["gla_heads"] * lay["gla_dk"]
    bup = b_gate.reshape(1, gqk)
    gnw = gla_norm_w.reshape(1, -1)
    rnw = ret_norm_w.reshape(1, -1)
    tn = 1024 if (lay["out_cols"] % 1024 == 0 and lay["plain_cols"] % 1024 == 0) else 512
    xp = x_p.reshape(batch * seq, d)
    assert x_s.ndim == 3 and x_s.shape[1] == 1, "one new token per decode sequence"
    xs = x_s
    w_in_t = w_in.T

    h_p, h_s, g_p, g_s, cos_p, sin_p, cos_s, sin_s, first_p, first_s = _rmsnorm_gate(
        xp, xs, norm_mix, w_in_t, lay["gd_src"], w_gate_up, bup, seq, lay["ret_dk"] // 2, tn)
    rest_p, rest_s = _in_proj(h_p, h_s, w_in_t, lay["plain_cols"], rank, lay["out_cols"], tn, 1)
    projs_p, projs_s = (first_p, rest_p), (first_s, rest_s)
    lay = dict(lay, cols=_split_columns(lay, tn))
    oa_p, sa_p, ob_s, sb_s = _gla_prompt_ret_decode(
        projs_p, g_p, gnw, projs_s, cos_s, sin_s, log_gamma, rnw, st_ret, lay, batch, seq)
    ob_p, sb_p, oa_s, sa_s = _ret_prompt_gla_decode(
        projs_p, cos_p, sin_p, log_gamma, rnw, projs_s, g_s, gnw, st_gla, lay, batch, seq)
    m_p, m_s, w_out_bf = _merge(oa_p, ob_p, oa_s, ob_s, w_gla_up, w_ret_up, projs_p, projs_s, lay, w_out)
    x1_p, h2_p, x1_s, h2_s = _proj_res_norm(m_p, m_s, w_out_bf, xp, xs, norm_ffn, True, 2)
    act_p, act_s, w_down_bf = _swiglu(h2_p, h2_s, w_ffn_gate, w_ffn_up, w_ffn_down)
    y_p, y_s = _proj_res_norm(act_p, act_s, w_down_bf, x1_p, x1_s, final_norm, False, 3)
    return (y_p, sa_p, sb_p), (y_s, sa_s, sb_s)


def kernel(x_prompt, x_sample, state_gla, state_ret, norm_mix, w_in, w_gla_gate_up, b_gla_gate, gla_norm_w,
           w_gla_up, ret_norm_w, w_ret_up, w_out, norm_ffn, w_ffn_gate, w_ffn_up, w_ffn_down, norm_final):
    depth = w_in.shape[0]
    assert depth == 1, "single-layer trunk"
    batch, seq, d = x_prompt.shape
    lay = _layout(d, w_in.shape[-1], state_gla, state_ret, w_gla_gate_up.shape[1])
    rh, rdk = lay["ret_heads"], lay["ret_dk"]
    assert rdk // 2 == V7X_LANES
    lg = jnp.log1p(-jnp.exp(jnp.linspace(math.log(1.0 / 32), math.log(1.0 / 512), rh))).astype(F32)
    log_gamma = jnp.broadcast_to(lg[:, None, None], (rh, 1, V7X_LANES))

    wts = (norm_mix[0], w_in[0], w_gla_gate_up[0], b_gla_gate[0], gla_norm_w[0], w_gla_up[0], ret_norm_w[0],
           w_ret_up[0], w_out[0], norm_ffn[0], w_ffn_gate[0], w_ffn_up[0], w_ffn_down[0])
    (y_p, ga_p, re_p), (y_s, ga_s, re_s) = _layer(
        x_prompt, x_sample, state_gla[0], state_ret[0], wts, lay, log_gamma, norm_final)

    sd = state_gla.dtype
    return (y_p.reshape(batch, seq, d), y_s.reshape(x_sample.shape),
            ga_p[None].astype(sd), re_p[None].astype(state_ret.dtype),
            ga_s[None].astype(sd), re_s[None].astype(state_ret.dtype))
```

```python
import functools
import math

import numpy as np
import jax
import jax.numpy as jnp
from jax import lax
from jax.experimental import pallas as pl
from jax.experimental.pallas import tpu as pltpu

EPS = 1e-6
ROPE_BASE = 10000.0
GLA_GATE_NORM = 16.0
PAST_LEN = 16384

V7X_LANES = 128
V7X_VMEM_REQUEST_CAP = 60000 * 1024
COMPILER_SCRATCH_BYTES = 12 * 1024 * 1024

GLA_CHUNK = 64
GLA_STEP_CHUNKS = 8
LOG2_E = 1.4426950408889634
RET_CHUNK = 128
RET_STEP_CHUNKS = 4
ROW_TILE = 1024
ROW_TILE_WIDE = 2048
K_BLOCK = 512

BF16 = jnp.bfloat16
F32 = jnp.float32


def _params(n_axes, *block_bytes, scratch_bytes=0, claim_all=False):
    need = 2 * sum(block_bytes) + scratch_bytes + COMPILER_SCRATCH_BYTES
    if claim_all:
        need = V7X_VMEM_REQUEST_CAP
    return pltpu.CompilerParams(
        dimension_semantics=("arbitrary",) * n_axes,
        vmem_limit_bytes=int(min(V7X_VMEM_REQUEST_CAP, need)),
    )


def _nbytes(shape, dtype):
    return int(np.prod(shape)) * jnp.dtype(dtype).itemsize


def _sigmoid(x):
    return 1.0 / (1.0 + jnp.exp(-x))


def _silu(x):
    return x * _sigmoid(x)


def _log_sigmoid(x):
    return jnp.minimum(x, 0.0) - jnp.log(1.0 + jnp.exp(-jnp.abs(x)))


def _dot(a, b):
    return jnp.dot(a, b, preferred_element_type=F32)


def _dot_nt(a, b):
    return lax.dot_general(a, b, (((1,), (1,)), ((), ())), preferred_element_type=F32)


def _dot_tn(a, b):
    return lax.dot_general(a, b, (((0,), (0,)), ((), ())), preferred_element_type=F32)


def _row_tile(m, want):
    t = min(m, want)
    assert m % t == 0, (m, t)
    return t


def _rmsnorm_rows(x, w):
    ms = jnp.mean(x * x, axis=-1, keepdims=True)
    return x * lax.rsqrt(ms + EPS) * w


def _rope_rows(cos_ref, sin_ref, pos0):
    rows, half = cos_ref.shape
    pos = (lax.broadcasted_iota(jnp.int32, (rows, half), 0) + pos0).astype(F32)
    idx = lax.broadcasted_iota(jnp.int32, (rows, half), 1).astype(F32)
    ang = pos * jnp.exp(idx * (-math.log(ROPE_BASE) / half))
    cos_ref[...] = jnp.cos(ang)
    sin_ref[...] = jnp.sin(ang)


def _rmsnorm_kernel(xp_ref, xs_ref, w_ref, wgd_ref, wup_ref, bup_ref, w0_ref,
                    hp_ref, hs_ref, gp_ref, gs_ref, cosp_ref, sinp_ref, cosd_ref, sind_ref, p0p_ref, p0s_ref,
                    wgd_bf, wup_bf, w0_bf):
    i = pl.program_id(0)

    @pl.when(i == 0)
    def _():
        wgd_bf[...] = wgd_ref[...].astype(BF16)
        wup_bf[...] = jnp.zeros_like(wup_bf)
        wup_bf[0:wup_ref.shape[0], :] = wup_ref[...].astype(BF16)
        w0_bf[...] = w0_ref[...].astype(BF16)

    def rows(x, h_ref, g_ref, p0_ref, cos_ref, sin_ref, pos0):
        h = _rmsnorm_rows(x, w_ref[...]).astype(h_ref.dtype)
        h_ref[...] = h
        gd = _dot_nt(h, wgd_bf[...])
        x = _dot(gd.astype(BF16), wup_bf[...]) + bup_ref[...]
        g_ref[...] = _log_sigmoid(x) * (LOG2_E / GLA_GATE_NORM)
        _rope_rows(cos_ref, sin_ref, pos0)
        p0_ref[...] = _dot_nt(h, w0_bf[...]).astype(p0_ref.dtype)

    rows(xp_ref[...], hp_ref, gp_ref, p0p_ref, cosp_ref, sinp_ref, i * cosp_ref.shape[0])

    @pl.when(i == 0)
    def _():
        rows(xs_ref[...], hs_ref, gs_ref, p0s_ref, cosd_ref, sind_ref, PAST_LEN)


def _rmsnorm_gate(x_p, x_s, w, w_in_t, gate_row0, w_gate_up, bup, seq, half, tn0):
    m_p, d = x_p.shape
    tail = x_s.shape[0]
    rank, gw = w_gate_up.shape
    tm = _row_tile(m_p, ROW_TILE // 2)
    n_steps = m_p // tm
    assert gate_row0 % V7X_LANES == 0 and seq % n_steps == 0 and rank % 16 == 0
    pos_rows = seq // n_steps
    table = pl.BlockSpec((pos_rows, half), lambda i: (i, 0))
    table_dec = pl.BlockSpec((8, half), lambda i: (0, 0))
    return pl.pallas_call(
        _rmsnorm_kernel,
        grid=(n_steps,),
        in_specs=[
            pl.BlockSpec((tm, d), lambda i: (i, 0)),
            pl.BlockSpec((tail, None, d), lambda i: (0, 0, 0)),
            pl.BlockSpec((1, d), lambda i: (0, 0)),
            pl.BlockSpec((V7X_LANES, d), lambda i: (gate_row0 // V7X_LANES, 0)),
            pl.BlockSpec((rank, gw), lambda i: (0, 0)),
            pl.BlockSpec((1, gw), lambda i: (0, 0)),
            pl.BlockSpec((tn0, d), lambda i: (0, 0)),
        ],
        out_specs=[
            pl.BlockSpec((tm, d), lambda i: (i, 0)), pl.BlockSpec((tail, d), lambda i: (0, 0)),
            pl.BlockSpec((tm, gw), lambda i: (i, 0)), pl.BlockSpec((tail, gw), lambda i: (0, 0)),
            table, table, table_dec, table_dec,
            pl.BlockSpec((tm, tn0), lambda i: (i, 0)), pl.BlockSpec((tail, tn0), lambda i: (0, 0)),
        ],
        out_shape=[
            jax.ShapeDtypeStruct((m_p, d), BF16), jax.ShapeDtypeStruct((tail, d), BF16),
            jax.ShapeDtypeStruct((m_p, gw), F32), jax.ShapeDtypeStruct((tail, gw), F32),
            jax.ShapeDtypeStruct((seq, half), F32), jax.ShapeDtypeStruct((seq, half), F32),
            jax.ShapeDtypeStruct((8, half), F32), jax.ShapeDtypeStruct((8, half), F32),
            jax.ShapeDtypeStruct((m_p, tn0), BF16), jax.ShapeDtypeStruct((tail, tn0), BF16),
        ],
        scratch_shapes=[pltpu.VMEM((V7X_LANES, d), BF16), pltpu.VMEM((V7X_LANES, gw), BF16),
                        pltpu.VMEM((tn0, d), BF16)],
        compiler_params=_params(1, claim_all=True),
        name="rmsnorm_gate",
    )(x_p, x_s, w.reshape(1, d), w_in_t, w_gate_up, bup, w_in_t)


def _in_proj_kernel(hp_ref, hs_ref, wm_ref, wn_ref, op_ref, os_ref, wbf_ref, *, n_plain, shift, first_tile):
    j = pl.program_id(0) + first_tile
    i = pl.program_id(1)
    tn = wbf_ref.shape[0]

    @pl.when(jnp.logical_and(i == 0, j < n_plain))
    def _():
        wbf_ref[...] = wm_ref[...].astype(BF16)

    @pl.when(jnp.logical_and(i == 0, j >= n_plain))
    def _():
        wbf_ref[0:tn - shift, :] = wm_ref[shift:tn, :].astype(BF16)
        wbf_ref[tn - shift:tn, :] = wn_ref[...].astype(BF16)

    @pl.when(i == 0)
    def _():
        os_ref[...] = _dot_nt(hs_ref[...], wbf_ref[...]).astype(os_ref.dtype)

    op_ref[...] = _dot_nt(hp_ref[...], wbf_ref[...]).astype(op_ref.dtype)


def _in_proj(h_p, h_s, w_in_t, plain_cols, shift, out_cols, tn, first_tile):
    m_p, d = h_p.shape
    tail = h_s.shape[0]
    tm = _row_tile(m_p, ROW_TILE_WIDE)
    assert plain_cols % tn == 0 and out_cols % tn == 0 and tn % shift == 0 and shift % 8 == 0
    n_plain = plain_cols // tn
    n_tiles = out_cols // tn - first_tile
    kern = functools.partial(_in_proj_kernel, n_plain=n_plain, shift=shift, first_tile=first_tile)
    return pl.pallas_call(
        kern,
        grid=(n_tiles, m_p // tm),
        in_specs=[
            pl.BlockSpec((tm, d), lambda j, i: (i, 0)),
            pl.BlockSpec((tail, d), lambda j, i: (0, 0)),
            pl.BlockSpec((tn, d), lambda j, i: (j + first_tile, 0)),
            pl.BlockSpec((shift, d), lambda j, i: ((j + first_tile + 1) * (tn // shift), 0)),
        ],
        out_specs=[pl.BlockSpec((tm, tn), lambda j, i: (i, j)), pl.BlockSpec((tail, tn), lambda j, i: (0, j))],
        out_shape=[jax.ShapeDtypeStruct((m_p, n_tiles * tn), BF16), jax.ShapeDtypeStruct((tail, n_tiles * tn), BF16)],
        scratch_shapes=[pltpu.VMEM((tn, d), BF16)],
        compiler_params=_params(
            2, _nbytes((tm, d), BF16), _nbytes((tail, d), BF16), _nbytes((tn, d), F32), _nbytes((shift, d), F32),
            _nbytes((tm, tn), BF16), _nbytes((tail, tn), BF16),
            scratch_bytes=_nbytes((tn, d), BF16) + _nbytes((tm, tn), F32)),
        name="in_proj",
    )(h_p, h_s, w_in_t, w_in_t)


def _prefix_sum_rows(sel3_bf16, g):
    g0 = g.astype(BF16)
    r1 = g - g0.astype(F32)
    g1 = r1.astype(BF16)
    g2 = (r1 - g1.astype(F32)).astype(BF16)
    return _dot(sel3_bf16, jnp.concatenate([g0, g1, g2], axis=0))


def _lane_bcast_cols(row, n):
    parts = []
    for c in range(n // V7X_LANES):
        tile = jnp.broadcast_to(row[:, c * V7X_LANES:(c + 1) * V7X_LANES], (V7X_LANES, V7X_LANES))
        parts.append(tile.T)
    return parts[0] if len(parts) == 1 else jnp.concatenate(parts, axis=0)


def _rms_gate_store(o, w, gate, out_ref, rows, cols):
    ms = jnp.mean(o * o, axis=-1, keepdims=True)
    y = o * lax.rsqrt(ms + EPS) * w
    out_ref[rows, cols] = (y * _silu(gate)).astype(out_ref.dtype)


def _ln_gate_store(o, w, gate, out_ref, rows, cols):
    mu = jnp.mean(o, axis=-1, keepdims=True)
    dlt = o - mu
    var = jnp.mean(dlt * dlt, axis=-1, keepdims=True)
    y = dlt * lax.rsqrt(var + EPS) * w
    out_ref[rows, cols] = (y * _silu(gate)).astype(out_ref.dtype)


def _token_selectors(n_tok):
    assert 3 * n_tok <= V7X_LANES
    j = lax.broadcasted_iota(jnp.int32, (V7X_LANES, V7X_LANES), 0)
    sel = []
    for t in range(n_tok):
        hit = jnp.logical_or(j == t, jnp.logical_or(j == n_tok + t, j == 2 * n_tok + t))
        sel.append(jnp.where(hit, 1.0, 0.0).astype(BF16))
    return jnp.stack(sel, axis=0)


def _column_source(x):
    n_tok, w = x.shape
    hi = x.astype(BF16).astype(F32)
    r1 = x - hi
    mid = r1.astype(BF16).astype(F32)
    lo = (r1 - mid).astype(BF16).astype(F32)
    x3 = jnp.concatenate([hi, mid, lo, jnp.zeros((V7X_LANES - 3 * n_tok, w), F32)], axis=0)
    parts = [x3[:, c * V7X_LANES:(c + 1) * V7X_LANES].T for c in range(w // V7X_LANES)]
    return (parts[0] if len(parts) == 1 else jnp.concatenate(parts, axis=0)).astype(BF16)


def _decode_advance(tok0, decay_rows_fn, decay_const_fn, k_ref, q_ref, v_ref, s_in_ref, s_out_ref, o_ref, sel_ref,
                    *, heads, dk, dv):
    n_tok = s_in_ref.shape[0]
    reps = dv // V7X_LANES
    rows = pl.ds(pl.multiple_of(tok0, n_tok), n_tok)

    def cols(src, tt):
        return jnp.concatenate([_dot(src, sel_ref[tt])] * reps, axis=1)

    for hh in range(heads):
        kc = slice(hh * dk, (hh + 1) * dk)
        vc = slice(hh * dv, (hh + 1) * dv)
        k_src, q_src = _column_source(k_ref[rows, kc]), _column_source(q_ref[rows, kc])
        a_src = None if decay_rows_fn is None else _column_source(decay_rows_fn(rows, hh))
        v = v_ref[rows, vc]
        o_rows = []
        for tt in range(n_tok):
            decay = decay_const_fn(hh) if a_src is None else cols(a_src, tt)
            s_new = decay * s_in_ref[tt, hh] + cols(k_src, tt) * v[tt:tt + 1, :]
            s_out_ref[tt, hh] = s_new
            o_rows.append(jnp.sum(cols(q_src, tt) * s_new, axis=0, keepdims=True))
        o_ref[rows, vc] = jnp.concatenate(o_rows, axis=0)


def _decode_plan(n_dec, n_steps):
    assert n_dec % n_steps == 0 and (n_dec // n_steps) % 8 == 0, (n_dec, n_steps)
    return n_dec // n_steps


def _gla_decode_section(step, last_step, refs, scratch, *, heads, dk, dv):
    qd_ref, kd_ref, vd_ref, gad_ref, gd_ref, nw_ref, sd_in_ref, od_ref, sd_out_ref = refs
    a_dec, q_dec, k_dec, v_dec, o_dec, sel_ref = scratch

    @pl.when(step == 0)
    def _():
        a_dec[...] = jnp.exp2(gd_ref[...])
        q_dec[...] = qd_ref[...].astype(F32) * (dk ** -0.5)
        k_dec[...] = kd_ref[...].astype(F32)
        v_dec[...] = vd_ref[...].astype(F32)
        sel_ref[...] = _token_selectors(sd_in_ref.shape[0])

    _decode_advance(step * sd_in_ref.shape[0], lambda rows, hh: a_dec[rows, hh * dk:(hh + 1) * dk], None,
                    k_dec, q_dec, v_dec, sd_in_ref, sd_out_ref, o_dec, sel_ref, heads=heads, dk=dk, dv=dv)

    @pl.when(step == last_step)
    def _():
        n_dec = o_dec.shape[0]
        for hh in range(heads):
            vc = slice(hh * dv, (hh + 1) * dv)
            _rms_gate_store(o_dec[:, vc], nw_ref[...], gad_ref[:, vc].astype(F32), od_ref, slice(0, n_dec), vc)


def _ret_decode_section(step, last_step, refs, scratch, *, heads, dk, dv):
    qd_ref, kd_ref, vd_ref, gbd_ref, cosd_ref, sind_ref, lg_ref, nw_ref, sd_in_ref, od_ref, sd_out_ref = refs
    q_dec, k_dec, v_dec, o_dec, sel_ref = scratch

    @pl.when(step == 0)
    def _():
        cosd, sind = cosd_ref[0:1, :], sind_ref[0:1, :]
        for hh in range(heads):
            kc = slice(hh * dk, (hh + 1) * dk)
            q_dec[:, kc] = _rotary(qd_ref[:, kc].astype(F32), cosd, sind)
            k_dec[:, kc] = _rotary(kd_ref[:, kc].astype(F32), cosd, sind) * (dk ** -0.5)
        v_dec[...] = vd_ref[...].astype(F32)
        sel_ref[...] = _token_selectors(sd_in_ref.shape[0])

    def gamma(hh):
        return jnp.exp(jnp.concatenate([lg_ref[hh]] * (dv // V7X_LANES), axis=1))

    _decode_advance(step * sd_in_ref.shape[0], None, gamma, k_dec, q_dec, v_dec, sd_in_ref, sd_out_ref, o_dec,
                    sel_ref, heads=heads, dk=dk, dv=dv)

    @pl.when(step == last_step)
    def _():
        n_dec = o_dec.shape[0]
        for hh in range(heads):
            vc = slice(hh * dv, (hh + 1) * dv)
            _ln_gate_store(o_dec[:, vc], nw_ref[...], gbd_ref[:, vc].astype(F32), od_ref, slice(0, n_dec), vc)


def _segment_block(lay, name, width):
    offset = lay["cols"][name][1]
    assert offset % width == 0
    return offset // width


def _segment_arrays(lay, arrays, names):
    return tuple(arrays[lay["cols"][n][0]] for n in names)


def _decode_specs(kind, lay, n_dec, tps, row):
    heads, dk, dv = lay[kind + "_heads"], lay[kind + "_dk"], lay[kind + "_dv"]
    qk, vw = heads * dk, heads * dv
    names = ("qa", "ka", "va", "ga") if kind == "gla" else ("qb", "kb", "vb", "gb")
    widths = (qk, qk, vw, vw)
    rows_in = [pl.BlockSpec((n_dec, w), lambda b, t, c=_segment_block(lay, n, w): (0, c))
               for n, w in zip(names, widths)]
    state = pl.BlockSpec((tps, heads, dk, dv), lambda b, t: (row(b, t), 0, 0, 0))
    o_spec = pl.BlockSpec((n_dec, vw), lambda b, t: (0, 0))
    n_qk = 3 if kind == "gla" else 2
    scratch = ([pltpu.VMEM((n_dec, qk), F32)] * n_qk + [pltpu.VMEM((n_dec, vw), F32)] * 2
               + [pltpu.VMEM((tps, V7X_LANES, V7X_LANES), BF16)])
    return rows_in, state, o_spec, scratch


def _gla_sum_matrices(c):
    levels = c.bit_length() - 1
    assert 1 << levels == c
    i = lax.broadcasted_iota(jnp.int32, (c, c), 0)
    j = lax.broadcasted_iota(jnp.int32, (c, c), 1)
    mats = [j <= i]
    for l in range(levels):
        ref = jnp.bitwise_or(jnp.bitwise_and(i, -(2 << l)), 1 << l)
        mats.append(jnp.logical_and(j > jnp.minimum(i, ref), j <= jnp.maximum(i, ref)))
    mats.append(j > i)
    sel = jnp.concatenate([jnp.where(m, 1.0, 0.0).astype(BF16) for m in mats], axis=0)
    return jnp.concatenate([sel, sel, sel], axis=1)


def _pair_level(c):
    levels = c.bit_length() - 1
    i = lax.broadcasted_iota(jnp.int32, (c, c), 0)
    j = lax.broadcasted_iota(jnp.int32, (c, c), 1)
    x = jnp.bitwise_xor(i, j)
    lvl = jnp.zeros((c, c), jnp.int32)
    for l in range(1, levels):
        lvl = lvl + jnp.where(x >= (1 << l), 1, 0)
    return jnp.where(i > j, lvl, jnp.where(i == j, levels, -1))


def _queries_else_keys(q, k, l):
    c = q.shape[0]
    span = 1 << l
    if span >= 8:
        parts = [(q if (b & 1) else k)[b * span:(b + 1) * span, :] for b in range(c // span)]
        return jnp.concatenate(parts, axis=0)
    row = lax.broadcasted_iota(jnp.int32, q.shape, 0)
    return jnp.where(jnp.bitwise_and(row, span) != 0, q, k)


def _gla_level_scores(q, k, sums):
    c = q.shape[0]
    levels = c.bit_length() - 1
    out = []
    for l in range(levels):
        x = _queries_else_keys(q, k, l) * jnp.exp2(sums[(1 + l) * c:(2 + l) * c, :])
        xb = x.astype(BF16)
        out.append(_dot_nt(xb, xb))
    return out


def _gla_chunk_out(q, k, v, sums, level_scores, pair_level, state):
    c = q.shape[0]
    levels = c.bit_length() - 1
    scores = jnp.where(pair_level == levels, jnp.sum(q * k, axis=-1, keepdims=True), 0.0)
    for l in range(levels):
        scores = jnp.where(pair_level == l, level_scores[l], scores)
    o = _dot((q * jnp.exp2(sums[0:c, :])).astype(BF16), state.astype(BF16))
    return o + _dot(scores.astype(BF16), v)


def _gla_next_state(k, v, sums, state):
    c, dk = k.shape
    levels = c.bit_length() - 1
    k_tail = (k * jnp.exp2(sums[(levels + 1) * c:(levels + 2) * c, :])).astype(BF16)
    decay = _lane_bcast_cols(jnp.exp2(sums[c - 1:c, :]), dk)
    decay_full = jnp.concatenate([decay] * (v.shape[1] // V7X_LANES), axis=1)
    return decay_full * state + _dot_tn(k_tail, v)


def _gla_kernel(*refs, heads, dk, dv, dec_dims):
    q_ref, k_ref, v_ref, ga_ref, g_ref, nw_ref = refs[:6]
    dec_in = refs[6:15]
    o_ref, s_out_ref = refs[15:17]
    dec_out = refs[17:19]
    s_ref, mats_ref, lvl_ref = refs[19:22]
    dec_scratch = refs[22:]
    t = pl.program_id(1)
    step = pl.program_id(0) * pl.num_programs(1) + t
    last_step = pl.num_programs(0) * pl.num_programs(1) - 1

    @pl.when(t == 0)
    def _():
        s_ref[...] = jnp.zeros_like(s_ref)
        mats_ref[...] = _gla_sum_matrices(GLA_CHUNK)
        lvl_ref[...] = _pair_level(GLA_CHUNK)

    _ret_decode_section(step, last_step, dec_in + dec_out, dec_scratch,
                        heads=dec_dims[0], dk=dec_dims[1], dv=dec_dims[2])

    ct = q_ref.shape[0]

    kcs = [slice(hh * dk, (hh + 1) * dk) for hh in range(heads)]
    vcs = [slice(hh * dv, (hh + 1) * dv) for hh in range(heads)]
    group = 2 if (ct // GLA_CHUNK) % 2 == 0 else 1

    def chunk_group(cg, carry):
        pair_level = lvl_ref[...]
        rows, sums, qs, ks, lvl_scores = [], [], [], [], []
        for u in range(group):
            r = pl.ds(pl.multiple_of((cg * group + u) * GLA_CHUNK, GLA_CHUNK), GLA_CHUNK)
            rows.append(r)
            sums.append(_prefix_sum_rows(mats_ref[...], g_ref[r, :]))
            qs.append([q_ref[r, kc].astype(F32) * (dk ** -0.5) for kc in kcs])
            ks.append([k_ref[r, kc].astype(F32) for kc in kcs])
            lvl_scores.append([_gla_level_scores(qs[u][hh], ks[u][hh], sums[u][:, kcs[hh]]) for hh in range(heads)])
        outs = []
        for u in range(group):
            r = rows[u]
            outs.append([_gla_chunk_out(qs[u][hh], ks[u][hh], v_ref[r, vcs[hh]], sums[u][:, kcs[hh]],
                                        lvl_scores[u][hh], pair_level, s_ref[hh]) for hh in range(heads)])
            for hh in range(heads):
                s_ref[hh] = _gla_next_state(ks[u][hh], v_ref[r, vcs[hh]], sums[u][:, kcs[hh]], s_ref[hh])
        for u in range(group):
            for hh in range(heads):
                _rms_gate_store(outs[u][hh], nw_ref[...], ga_ref[rows[u], vcs[hh]].astype(F32), o_ref, rows[u],
                                vcs[hh])
        return carry

    lax.fori_loop(0, ct // GLA_CHUNK // group, chunk_group, 0)

    @pl.when(t == pl.num_programs(1) - 1)
    def _():
        s_out_ref[0] = s_ref[...]


def _gla_prompt_ret_decode(projs, log2_decay, gla_norm_w, projs_dec, cos_dec, sin_dec, log_gamma, ret_norm_w,
                           ret_state_dec, lay, batch, seq):
    heads, dk, dv = lay["gla_heads"], lay["gla_dk"], lay["gla_dv"]
    qk, vw = heads * dk, heads * dv
    r_heads, r_dk, r_dv = lay["ret_heads"], lay["ret_dk"], lay["ret_dv"]
    ct = min(seq, GLA_STEP_CHUNKS * GLA_CHUNK)
    levels = GLA_CHUNK.bit_length() - 1
    assert seq % ct == 0 and ct % GLA_CHUNK == 0
    nt = seq // ct
    n_dec = projs_dec[0].shape[0]
    tps = _decode_plan(n_dec, batch * nt)
    row = lambda b, t: b * nt + t
    dec_rows, dec_state, dec_o, dec_scratch = _decode_specs("ret", lay, n_dec, tps, row)
    table_dec = pl.BlockSpec((cos_dec.shape[0], r_dk // 2), lambda b, t: (0, 0))
    kern = functools.partial(_gla_kernel, heads=heads, dk=dk, dv=dv, dec_dims=(r_heads, r_dk, r_dv))
    return pl.pallas_call(
        kern,
        grid=(batch, nt),
        in_specs=[
            pl.BlockSpec((ct, qk), lambda b, t: (row(b, t), _segment_block(lay, "qa", qk))),
            pl.BlockSpec((ct, qk), lambda b, t: (row(b, t), _segment_block(lay, "ka", qk))),
            pl.BlockSpec((ct, vw), lambda b, t: (row(b, t), _segment_block(lay, "va", vw))),
            pl.BlockSpec((ct, vw), lambda b, t: (row(b, t), _segment_block(lay, "ga", vw))),
            pl.BlockSpec((ct, qk), lambda b, t: (row(b, t), 0)),
            pl.BlockSpec((1, dv), lambda b, t: (0, 0)),
        ] + dec_rows + [
            table_dec, table_dec,
            pl.BlockSpec((r_heads, 1, V7X_LANES), lambda b, t: (0, 0, 0)),
            pl.BlockSpec((1, r_dv), lambda b, t: (0, 0)),
            dec_state,
        ],
        out_specs=[
            pl.BlockSpec((ct, vw), lambda b, t: (row(b, t), 0)),
            pl.BlockSpec((1, heads, dk, dv), lambda b, t: (b, 0, 0, 0)),
            dec_o,
            dec_state,
        ],
        out_shape=[
            jax.ShapeDtypeStruct((batch * seq, vw), BF16),
            jax.ShapeDtypeStruct((batch, heads, dk, dv), F32),
            jax.ShapeDtypeStruct((n_dec, r_heads * r_dv), BF16),
            jax.ShapeDtypeStruct(ret_state_dec.shape, ret_state_dec.dtype),
        ],
        scratch_shapes=[
            pltpu.VMEM((heads, dk, dv), F32),
            pltpu.VMEM(((levels + 2) * GLA_CHUNK, 3 * GLA_CHUNK), BF16),
            pltpu.VMEM((GLA_CHUNK, GLA_CHUNK), jnp.int32),
        ] + dec_scratch,
        compiler_params=_params(2, claim_all=True),
        name="gla_prompt_ret_decode",
    )(*_segment_arrays(lay, projs, ("qa", "ka", "va", "ga")), log2_decay, gla_norm_w,
      *_segment_arrays(lay, projs_dec, ("qb", "kb", "vb", "gb")), cos_dec, sin_dec,
      log_gamma, ret_norm_w, ret_state_dec)


def _rotary(x, cos, sin):
    half = x.shape[1] // 2
    x1, x2 = x[:, :half], x[:, half:]
    return jnp.concatenate([x1 * cos - x2 * sin, x1 * sin + x2 * cos], axis=1)


def _ret_kernel(*refs, heads, dk, dv, c, dec_dims):
    q_ref, k_ref, v_ref, gb_ref, cos_ref, sin_ref, lg_ref, nw_ref = refs[:8]
    dec_in = refs[8:15]
    o_ref, s_out_ref = refs[15:17]
    dec_out = refs[17:19]
    s_ref, dmat_ref, qdec_ref, kdec_ref = refs[19:23]
    dec_scratch = refs[23:]
    t = pl.program_id(1)
    step = pl.program_id(0) * pl.num_programs(1) + t
    last_step = pl.num_programs(0) * pl.num_programs(1) - 1

    _gla_decode_section(step, last_step, dec_in + dec_out, dec_scratch,
                        heads=dec_dims[0], dk=dec_dims[1], dv=dec_dims[2])

    @pl.when(t == 0)
    def _():
        s_ref[...] = jnp.zeros_like(s_ref)
        ri = lax.broadcasted_iota(jnp.int32, (c, c), 0)
        rj = lax.broadcasted_iota(jnp.int32, (c, c), 1)
        dist = (ri - rj).astype(F32)
        rowl = lax.broadcasted_iota(jnp.int32, (c, V7X_LANES), 0).astype(F32)
        for hh in range(heads):
            lg = lg_ref[hh]
            dmat_ref[hh] = jnp.exp(jnp.where(ri >= rj, dist * lg[:, :1], -jnp.inf))
            qdec_ref[hh] = jnp.exp((rowl + 1.0) * lg)
            kdec_ref[hh] = jnp.exp((float(c - 1) - rowl) * lg)

    ct = q_ref.shape[0]
    kcs = [slice(hh * dk, (hh + 1) * dk) for hh in range(heads)]
    vcs = [slice(hh * dv, (hh + 1) * dv) for hh in range(heads)]

    def chunk(ci, carry):
        rows = pl.ds(pl.multiple_of(ci * c, c), c)
        cos, sin = cos_ref[rows, :], sin_ref[rows, :]
        qrs = [_rotary(q_ref[rows, kc].astype(F32), cos, sin).astype(BF16) for kc in kcs]
        krs = [_rotary(k_ref[rows, kc].astype(F32), cos, sin) * (dk ** -0.5) for kc in kcs]
        scores = [_dot_nt(qrs[hh], krs[hh].astype(BF16)) * dmat_ref[hh] for hh in range(heads)]
        outs = []
        for hh in range(heads):
            qdec = jnp.concatenate([qdec_ref[hh]] * (dv // V7X_LANES), axis=1)
            o = qdec * _dot(qrs[hh], s_ref[hh].astype(BF16))
            outs.append(o + _dot(scores[hh].astype(BF16), v_ref[rows, vcs[hh]]))
        for hh in range(heads):
            kdec = jnp.concatenate([kdec_ref[hh]] * (dk // V7X_LANES), axis=1)
            k_tail = (krs[hh] * kdec).astype(BF16)
            lgv = jnp.concatenate([lg_ref[hh]] * (dv // V7X_LANES), axis=1)
            s_ref[hh] = jnp.exp(float(c) * lgv) * s_ref[hh] + _dot_tn(k_tail, v_ref[rows, vcs[hh]])
        for hh in range(heads):
            _ln_gate_store(outs[hh], nw_ref[...], gb_ref[rows, vcs[hh]].astype(F32), o_ref, rows, vcs[hh])
        return carry

    lax.fori_loop(0, ct // c, chunk, 0)

    @pl.when(t == pl.num_programs(1) - 1)
    def _():
        s_out_ref[0] = s_ref[...]


def _ret_prompt_gla_decode(projs, cos, sin, log_gamma, ret_norm_w, projs_dec, log2_decay_dec, gla_norm_w,
                           gla_state_dec, lay, batch, seq):
    heads, dk, dv = lay["ret_heads"], lay["ret_dk"], lay["ret_dv"]
    qk, vw = heads * dk, heads * dv
    g_heads, g_dk, g_dv = lay["gla_heads"], lay["gla_dk"], lay["gla_dv"]
    c = min(seq, RET_CHUNK)
    ct = min(seq, RET_STEP_CHUNKS * c)
    assert seq % ct == 0 and ct % c == 0
    nt = seq // ct
    half = dk // 2
    n_dec = projs_dec[0].shape[0]
    tps = _decode_plan(n_dec, batch * nt)
    row = lambda b, t: b * nt + t
    dec_rows, dec_state, dec_o, dec_scratch = _decode_specs("gla", lay, n_dec, tps, row)
    kern = functools.partial(_ret_kernel, heads=heads, dk=dk, dv=dv, c=c, dec_dims=(g_heads, g_dk, g_dv))
    return pl.pallas_call(
        kern,
        grid=(batch, nt),
        in_specs=[
            pl.BlockSpec((ct, qk), lambda b, t: (row(b, t), _segment_block(lay, "qb", qk))),
            pl.BlockSpec((ct, qk), lambda b, t: (row(b, t), _segment_block(lay, "kb", qk))),
            pl.BlockSpec((ct, vw), lambda b, t: (row(b, t), _segment_block(lay, "vb", vw))),
            pl.BlockSpec((ct, vw), lambda b, t: (row(b, t), _segment_block(lay, "gb", vw))),
            pl.BlockSpec((ct, half), lambda b, t: (t, 0)),
            pl.BlockSpec((ct, half), lambda b, t: (t, 0)),
            pl.BlockSpec((heads, 1, V7X_LANES), lambda b, t: (0, 0, 0)),
            pl.BlockSpec((1, dv), lambda b, t: (0, 0)),
        ] + dec_rows + [
            pl.BlockSpec((n_dec, g_heads * g_dk), lambda b, t: (0, 0)),
            pl.BlockSpec((1, g_dv), lambda b, t: (0, 0)),
            dec_state,
        ],
        out_specs=[
            pl.BlockSpec((ct, vw), lambda b, t: (row(b, t), 0)),
            pl.BlockSpec((1, heads, dk, dv), lambda b, t: (b, 0, 0, 0)),
            dec_o,
            dec_state,
        ],
        out_shape=[
            jax.ShapeDtypeStruct((batch * seq, vw), BF16),
            jax.ShapeDtypeStruct((batch, heads, dk, dv), F32),
            jax.ShapeDtypeStruct((n_dec, g_heads * g_dv), BF16),
            jax.ShapeDtypeStruct(gla_state_dec.shape, gla_state_dec.dtype),
        ],
        scratch_shapes=[
            pltpu.VMEM((heads, dk, dv), F32),
            pltpu.VMEM((heads, c, c), F32),
            pltpu.VMEM((heads, c, V7X_LANES), F32),
            pltpu.VMEM((heads, c, V7X_LANES), F32),
        ] + dec_scratch,
        compiler_params=_params(2, claim_all=True),
        name="ret_prompt_gla_decode",
    )(*_segment_arrays(lay, projs, ("qb", "kb", "vb", "gb")), cos, sin, log_gamma, ret_norm_w,
      *_segment_arrays(lay, projs_dec, ("qa", "ka", "va", "ga")), log2_decay_dec, gla_norm_w, gla_state_dec)


def _merge_kernel(oap_ref, obp_ref, oas_ref, obs_ref, wa_ref, wb_ref, g0p_ref, g1p_ref, g0s_ref, g1s_ref, wnext_ref,
                  mp_ref, ms_ref, wnext_bf_ref, wa_bf, wb_bf):
    wnext_bf_ref[...] = wnext_ref[...].astype(BF16)

    def merged(oa, ob, g0, g1):
        ya = _dot(oa, wa_bf[...])
        yb = _dot(ob, wb_bf[...])
        return _sigmoid(g0.astype(F32)) * ya + _sigmoid(g1.astype(F32)) * yb

    @pl.when(pl.program_id(1) == 0)
    def _():
        wa_bf[...] = wa_ref[...].astype(BF16)
        wb_bf[...] = wb_ref[...].astype(BF16)
        ms_ref[...] = merged(oas_ref[...], obs_ref[...], g0s_ref[...], g1s_ref[...]).astype(ms_ref.dtype)

    mp_ref[...] = merged(oap_ref[...], obp_ref[...], g0p_ref[...], g1p_ref[...]).astype(mp_ref.dtype)


def _slab_specs(w_next, n_steps, step_of):
    kn, dn = w_next.shape
    assert kn % n_steps == 0 and (kn // n_steps) % 16 == 0, (kn, n_steps)
    slab = kn // n_steps
    spec = pl.BlockSpec((slab, dn), lambda j, i: (step_of(j, i), 0))
    return spec, spec, jax.ShapeDtypeStruct((kn, dn), BF16), _nbytes((slab, dn), F32) + _nbytes((slab, dn), BF16)


def _merge(oa_p, ob_p, oa_s, ob_s, wa, wb, projs_p, projs_s, lay, w_next):
    proj_p, proj_s = projs_p[lay["cols"]["mg"][0]], projs_s[lay["cols"]["mg"][0]]
    m_p, ka = oa_p.shape
    kb = ob_p.shape[1]
    tail = oa_s.shape[0]
    d = wa.shape[1]
    tm = _row_tile(m_p, ROW_TILE)
    tn = min(d, 1024)
    mg = lay["cols"]["mg"][1]
    assert d % tn == 0 and mg % tn == 0
    g0 = mg // tn
    g1 = (mg + d) // tn
    n_m = m_p // tm
    slab_in, slab_out, slab_shape, slab_bytes = _slab_specs(w_next, (d // tn) * n_m, lambda j, i: j * n_m + i)
    return pl.pallas_call(
        _merge_kernel,
        grid=(d // tn, n_m),
        in_specs=[
            pl.BlockSpec((tm, ka), lambda j, i: (i, 0)),
            pl.BlockSpec((tm, kb), lambda j, i: (i, 0)),
            pl.BlockSpec((tail, ka), lambda j, i: (0, 0)),
            pl.BlockSpec((tail, kb), lambda j, i: (0, 0)),
            pl.BlockSpec((ka, tn), lambda j, i: (0, j)),
            pl.BlockSpec((kb, tn), lambda j, i: (0, j)),
            pl.BlockSpec((tm, tn), lambda j, i: (i, g0 + j)),
            pl.BlockSpec((tm, tn), lambda j, i: (i, g1 + j)),
            pl.BlockSpec((tail, tn), lambda j, i: (0, g0 + j)),
            pl.BlockSpec((tail, tn), lambda j, i: (0, g1 + j)),
            slab_in,
        ],
        out_specs=[pl.BlockSpec((tm, tn), lambda j, i: (i, j)), pl.BlockSpec((tail, tn), lambda j, i: (0, j)),
                   slab_out],
        out_shape=[jax.ShapeDtypeStruct((m_p, d), BF16), jax.ShapeDtypeStruct((tail, d), BF16), slab_shape],
        scratch_shapes=[pltpu.VMEM((ka, tn), BF16), pltpu.VMEM((kb, tn), BF16)],
        compiler_params=_params(
            2, _nbytes((tm, ka), BF16), _nbytes((tm, kb), BF16), _nbytes((tail, ka), BF16), _nbytes((tail, kb), BF16),
            _nbytes((ka, tn), F32), _nbytes((kb, tn), F32), 3 * _nbytes((tm, tn), BF16), 3 * _nbytes((tail, tn), BF16),
            slab_bytes,
            scratch_bytes=_nbytes((ka, tn), BF16) + _nbytes((kb, tn), BF16) + 3 * _nbytes((tm, tn), F32)),
        name="merge",
    )(oa_p, ob_p, oa_s, ob_s, wa, wb, proj_p, proj_p, proj_s, proj_s, w_next)


def _proj_res_norm_kernel(*refs, emit_sum, group, n_blocks):
    ap_refs, as_refs, w_refs = refs[:group], refs[group:2 * group], refs[2 * group:3 * group]
    resp_ref, ress_ref, nw_ref = refs[3 * group:3 * group + 3]
    out_refs = refs[3 * group + 3:]
    n_out = 2 if emit_sum else 1
    outs_p, outs_s = out_refs[:n_out], out_refs[n_out:]
    i = pl.program_id(0)
    k = pl.program_id(1)
    last_k = k == pl.num_programs(1) - 1
    d = w_refs[0].shape[1]
    col_chunk = min(d, 512)
    rest = n_blocks % group

    def step(a_refs, res_ref, outs):
        acc_ref = outs[0]
        nrow = acc_ref.shape[0]
        row_chunk = min(nrow, 128)
        assert nrow % row_chunk == 0

        def accumulate(n_used, base_ref):
            a = [a_refs[s][...] for s in range(n_used)]
            for c in range(d // col_chunk):
                cs = slice(c * col_chunk, (c + 1) * col_chunk)
                part = _dot(a[0], w_refs[0][:, cs])
                for s in range(1, n_used):
                    part = part + _dot(a[s], w_refs[s][:, cs])
                acc_ref[:, cs] = base_ref[:, cs] + part

        if n_blocks <= group:
            accumulate(n_blocks, res_ref)
        else:
            @pl.when(k == 0)
            def _():
                acc_ref[...] = res_ref[...]

            if rest == 0:
                accumulate(group, acc_ref)
            else:
                pl.when(jnp.logical_not(last_k))(lambda: accumulate(group, acc_ref))
                pl.when(last_k)(lambda: accumulate(rest, acc_ref))

        @pl.when(last_k)
        def _():
            def body(c, carry):
                rr = pl.ds(pl.multiple_of(c * row_chunk, row_chunk), row_chunk)
                y = _rmsnorm_rows(acc_ref[rr, :], nw_ref[...])
                if emit_sum:
                    outs[1][rr, :] = y.astype(outs[1].dtype)
                else:
                    acc_ref[rr, :] = y
                return carry

            lax.fori_loop(0, nrow // row_chunk, body, 0)

    step(ap_refs, resp_ref, outs_p)

    @pl.when(i == pl.num_programs(0) - 1)
    def _():
        step(as_refs, ress_ref, outs_s)


def _proj_res_norm(a_p, a_s, w, res_p, res_s, norm_w, emit_sum, group, row_tile):
    m_p, kdim = a_p.shape
    tail = a_s.shape[0]
    d = w.shape[1]
    tm = _row_tile(m_p, row_tile)
    tk = min(kdim, K_BLOCK)
    assert kdim % tk == 0
    n_blocks = kdim // tk
    group = min(group, n_blocks)
    n_steps = pl.cdiv(n_blocks, group)
    blk = lambda k, s: jnp.minimum(k * group + s, n_blocks - 1)
    p_spec = pl.BlockSpec((tm, d), lambda i, k: (i, 0))
    s_spec = pl.BlockSpec((tail, d), lambda i, k: (0, 0))
    s3_spec = pl.BlockSpec((tail, None, d), lambda i, k: (0, 0, 0))
    res_s_spec = s3_spec if res_s.ndim == 3 else s_spec
    out_specs = [p_spec, s3_spec]
    out_shape = [jax.ShapeDtypeStruct((m_p, d), F32), jax.ShapeDtypeStruct((tail, 1, d), F32)]
    assert w.dtype == BF16
    blocks = [group * _nbytes((tm, tk), BF16), group * _nbytes((tail, tk), BF16), group * _nbytes((tk, d), BF16),
              2 * _nbytes((tm, d), F32), 2 * _nbytes((tail, d), F32)]
    if emit_sum:
        out_specs = [p_spec, p_spec, s_spec, s_spec]
        out_shape = [out_shape[0], jax.ShapeDtypeStruct((m_p, d), BF16),
                     jax.ShapeDtypeStruct((tail, d), F32), jax.ShapeDtypeStruct((tail, d), BF16)]
        blocks += [_nbytes((tm, d), BF16), _nbytes((tail, d), BF16)]
    in_specs = (
        [pl.BlockSpec((tm, tk), lambda i, k, s=s: (i, blk(k, s))) for s in range(group)]
        + [pl.BlockSpec((tail, tk), lambda i, k, s=s: (0, blk(k, s))) for s in range(group)]
        + [pl.BlockSpec((tk, d), lambda i, k, s=s: (blk(k, s), 0)) for s in range(group)]
        + [p_spec, res_s_spec, pl.BlockSpec((1, d), lambda i, k: (0, 0))])
    return pl.pallas_call(
        functools.partial(_proj_res_norm_kernel, emit_sum=emit_sum, group=group, n_blocks=n_blocks),
        grid=(m_p // tm, n_steps),
        in_specs=in_specs,
        out_specs=out_specs,
        out_shape=out_shape,
        compiler_params=_params(2, *blocks),
        name="proj_res_norm",
    )(*([a_p] * group + [a_s] * group + [w] * group), res_p, res_s, norm_w.reshape(1, d))


def _swiglu_kernel(hp_ref, hs_ref, wg_ref, wu_ref, wnext_ref, op_ref, os_ref, wnext_bf_ref, wg_bf, wu_bf):
    wnext_bf_ref[...] = wnext_ref[...].astype(BF16)

    tn = wg_bf.shape[1]
    col_chunk = min(tn, 256)

    def act(h_ref, o_ref):
        h = h_ref[...]
        for c in range(tn // col_chunk):
            cs = slice(c * col_chunk, (c + 1) * col_chunk)
            a = _dot(h, wg_bf[:, cs])
            b = _dot(h, wu_bf[:, cs])
            o_ref[:, cs] = (_silu(a) * b).astype(o_ref.dtype)

    @pl.when(pl.program_id(1) == 0)
    def _():
        wg_bf[...] = wg_ref[...].astype(BF16)
        wu_bf[...] = wu_ref[...].astype(BF16)
        act(hs_ref, os_ref)

    act(hp_ref, op_ref)


def _swiglu(h_p, h_s, wg, wu, w_next):
    m_p, d = h_p.shape
    tail = h_s.shape[0]
    f = wg.shape[1]
    tm = _row_tile(m_p, ROW_TILE_WIDE)
    tn = 512 if f % 512 == 0 else 256
    assert f % tn == 0
    n_m = m_p // tm
    slab_in, slab_out, slab_shape, slab_bytes = _slab_specs(w_next, (f // tn) * n_m, lambda j, i: j * n_m + i)
    return pl.pallas_call(
        _swiglu_kernel,
        grid=(f // tn, n_m),
        in_specs=[
            pl.BlockSpec((tm, d), lambda j, i: (i, 0)),
            pl.BlockSpec((tail, d), lambda j, i: (0, 0)),
            pl.BlockSpec((d, tn), lambda j, i: (0, j)),
            pl.BlockSpec((d, tn), lambda j, i: (0, j)),
            slab_in,
        ],
        out_specs=[pl.BlockSpec((tm, tn), lambda j, i: (i, j)), pl.BlockSpec((tail, tn), lambda j, i: (0, j)),
                   slab_out],
        out_shape=[jax.ShapeDtypeStruct((m_p, f), BF16), jax.ShapeDtypeStruct((tail, f), BF16), slab_shape],
        scratch_shapes=[pltpu.VMEM((d, tn), BF16), pltpu.VMEM((d, tn), BF16)],
        compiler_params=_params(
            2, _nbytes((tm, d), BF16), _nbytes((tail, d), BF16), 2 * _nbytes((d, tn), F32),
            _nbytes((tm, tn), BF16), _nbytes((tail, tn), BF16), slab_bytes,
            scratch_bytes=2 * _nbytes((d, tn), BF16) + 3 * _nbytes((tm, tn), F32)),
        name="swiglu",
    )(h_p, h_s, wg, wu, w_next)


def _layout(d_model, in_width, state_gla, state_ret, gate_rank):
    _, _, gh, gdk, gdv = state_gla.shape
    _, _, rh, rdk, rdv = state_ret.shape
    gqk, gv, rqk, rv = gh * gdk, gh * gdv, rh * rdk, rh * rdv
    lay = dict(gla_heads=gh, gla_dk=gdk, gla_dv=gdv, ret_heads=rh, ret_dk=rdk, ret_dv=rdv, rank=gate_rank)
    off = 0
    for name, width in (("qa", gqk), ("ka", gqk), ("va", gv), ("ga", gv), ("qb", rqk), ("kb", rqk),
                        ("vb", rv), ("gb", rv), ("mg", 2 * d_model)):
        lay[name] = off
        off += width
    lay["out_cols"] = off
    lay["plain_cols"] = 2 * gqk + gv
    lay["gd_src"] = lay["plain_cols"]
    assert lay["gd_src"] % V7X_LANES == 0 and gate_rank <= V7X_LANES
    assert in_width == off + gate_rank
    lay["widths"] = dict(qa=gqk, ka=gqk, va=gv, ga=gv, qb=rqk, kb=rqk, vb=rv, gb=rv, mg=2 * d_model)
    return lay


def _split_columns(lay, first_cols):
    cols = {}
    for name, width in lay["widths"].items():
        off = lay[name]
        assert off + width <= first_cols or off >= first_cols, "a segment straddles the two arrays"
        cols[name] = (0, off) if off < first_cols else (1, off - first_cols)
    return cols


def _layer(x_p, x_s, st_gla, st_ret, wts, lay, log_gamma, final_norm):
    (norm_mix, w_in, w_gate_up, b_gate, gla_norm_w, w_gla_up, ret_norm_w, w_ret_up, w_out, norm_ffn,
     w_ffn_gate, w_ffn_up, w_ffn_down) = wts
    batch, seq, d = x_p.shape
    rank = lay["rank"]
    gqk = lay["gla_heads"] * lay["gla_dk"]
    bup = b_gate.reshape(1, gqk)
    gnw = gla_norm_w.reshape(1, -1)
    rnw = ret_norm_w.reshape(1, -1)
    tn = 1024 if (lay["out_cols"] % 1024 == 0 and lay["plain_cols"] % 1024 == 0) else 512
    xp = x_p.reshape(batch * seq, d)
    assert x_s.ndim == 3 and x_s.shape[1] == 1, "one new token per decode sequence"
    xs = x_s
    w_in_t = w_in.T

    h_p, h_s, g_p, g_s, cos_p, sin_p, cos_s, sin_s, first_p, first_s = _rmsnorm_gate(
        xp, xs, norm_mix, w_in_t, lay["gd_src"], w_gate_up, bup, seq, lay["ret_dk"] // 2, tn)
    rest_p, rest_s = _in_proj(h_p, h_s, w_in_t, lay["plain_cols"], rank, lay["out_cols"], tn, 1)
    projs_p, projs_s = (first_p, rest_p), (first_s, rest_s)
    lay = dict(lay, cols=_split_columns(lay, tn))
    oa_p, sa_p, ob_s, sb_s = _gla_prompt_ret_decode(
        projs_p, g_p, gnw, projs_s, cos_s, sin_s, log_gamma, rnw, st_ret, lay, batch, seq)
    ob_p, sb_p, oa_s, sa_s = _ret_prompt_gla_decode(
        projs_p, cos_p, sin_p, log_gamma, rnw, projs_s, g_s, gnw, st_gla, lay, batch, seq)
    m_p, m_s, w_out_bf = _merge(oa_p, ob_p, oa_s, ob_s, w_gla_up, w_ret_up, projs_p, projs_s, lay, w_out)
    x1_p, h2_p, x1_s, h2_s = _proj_res_norm(m_p, m_s, w_out_bf, xp, xs, norm_ffn, True, 4, ROW_TILE // 2)
    act_p, act_s, w_down_bf = _swiglu(h2_p, h2_s, w_ffn_gate, w_ffn_up, w_ffn_down)
    y_p, y_s = _proj_res_norm(act_p, act_s, w_down_bf, x1_p, x1_s, final_norm, False, 3, ROW_TILE)
    return (y_p, sa_p, sb_p), (y_s, sa_s, sb_s)


def kernel(x_prompt, x_sample, state_gla, state_ret, norm_mix, w_in, w_gla_gate_up, b_gla_gate, gla_norm_w,
           w_gla_up, ret_norm_w, w_ret_up, w_out, norm_ffn, w_ffn_gate, w_ffn_up, w_ffn_down, norm_final):
    depth = w_in.shape[0]
    assert depth == 1, "single-layer trunk"
    batch, seq, d = x_prompt.shape
    lay = _layout(d, w_in.shape[-1], state_gla, state_ret, w_gla_gate_up.shape[1])
    rh, rdk = lay["ret_heads"], lay["ret_dk"]
    assert rdk // 2 == V7X_LANES
    lg = jnp.log1p(-jnp.exp(jnp.linspace(math.log(1.0 / 32), math.log(1.0 / 512), rh))).astype(F32)
    log_gamma = jnp.broadcast_to(lg[:, None, None], (rh, 1, V7X_LANES))

    wts = (norm_mix[0], w_in[0], w_gla_gate_up[0], b_gla_gate[0], gla_norm_w[0], w_gla_up[0], ret_norm_w[0],
           w_ret_up[0], w_out[0], norm_ffn[0], w_ffn_gate[0], w_ffn_up[0], w_ffn_down[0])
    (y_p, ga_p, re_p), (y_s, ga_s, re_s) = _layer(
        x_prompt, x_sample, state_gla[0], state_ret[0], wts, lay, log_gamma, norm_final)

    sd = state_gla.dtype
    return (y_p.reshape(batch, seq, d), y_s.reshape(x_sample.shape),
            ga_p[None].astype(sd), re_p[None].astype(state_ret.dtype),
            ga_s[None].astype(sd), re_s[None].astype(state_ret.dtype))
```

```python
import functools
import math

import numpy as np
import jax
import jax.numpy as jnp
from jax import lax
from jax.experimental import pallas as pl
from jax.experimental.pallas import tpu as pltpu

EPS = 1e-6
ROPE_BASE = 10000.0
GLA_GATE_NORM = 16.0
PAST_LEN = 16384

V7X_LANES = 128
V7X_VMEM_REQUEST_CAP = 60000 * 1024
COMPILER_SCRATCH_BYTES = 12 * 1024 * 1024

GLA_CHUNK = 64
GLA_STEP_CHUNKS = 8
LOG2_E = 1.4426950408889634
RET_CHUNK = 128
RET_STEP_CHUNKS = 4
ROW_TILE = 1024
ROW_TILE_WIDE = 2048
K_BLOCK = 512

BF16 = jnp.bfloat16
F32 = jnp.float32


def _params(n_axes, *block_bytes, scratch_bytes=0, claim_all=False):
    need = 2 * sum(block_bytes) + scratch_bytes + COMPILER_SCRATCH_BYTES
    if claim_all:
        need = V7X_VMEM_REQUEST_CAP
    return pltpu.CompilerParams(
        dimension_semantics=("arbitrary",) * n_axes,
        vmem_limit_bytes=int(min(V7X_VMEM_REQUEST_CAP, need)),
    )


def _nbytes(shape, dtype):
    return int(np.prod(shape)) * jnp.dtype(dtype).itemsize


def _sigmoid(x):
    return 1.0 / (1.0 + jnp.exp(-x))


def _silu(x):
    return x * _sigmoid(x)


def _log_sigmoid(x):
    return jnp.minimum(x, 0.0) - jnp.log(1.0 + jnp.exp(-jnp.abs(x)))


def _dot(a, b):
    return jnp.dot(a, b, preferred_element_type=F32)


def _dot_nt(a, b):
    return lax.dot_general(a, b, (((1,), (1,)), ((), ())), preferred_element_type=F32)


def _dot_tn(a, b):
    return lax.dot_general(a, b, (((0,), (0,)), ((), ())), preferred_element_type=F32)


def _row_tile(m, want):
    t = min(m, want)
    assert m % t == 0, (m, t)
    return t


def _rmsnorm_rows(x, w):
    ms = jnp.mean(x * x, axis=-1, keepdims=True)
    return x * lax.rsqrt(ms + EPS) * w


def _rope_rows(cos_ref, sin_ref, pos0):
    rows, half = cos_ref.shape
    pos = (lax.broadcasted_iota(jnp.int32, (rows, half), 0) + pos0).astype(F32)
    idx = lax.broadcasted_iota(jnp.int32, (rows, half), 1).astype(F32)
    ang = pos * jnp.exp(idx * (-math.log(ROPE_BASE) / half))
    cos_ref[...] = jnp.cos(ang)
    sin_ref[...] = jnp.sin(ang)


def _rmsnorm_kernel(xp_ref, xs_ref, w_ref, wgd_ref, wup_ref, bup_ref, w0_ref,
                    hp_ref, hs_ref, gp_ref, gs_ref, cosp_ref, sinp_ref, cosd_ref, sind_ref, p0p_ref, p0s_ref,
                    wgd_bf, wup_bf, w0_bf):
    i = pl.program_id(0)

    @pl.when(i == 0)
    def _():
        wgd_bf[...] = wgd_ref[...].astype(BF16)
        wup_bf[...] = jnp.zeros_like(wup_bf)
        wup_bf[0:wup_ref.shape[0], :] = wup_ref[...].astype(BF16)
        w0_bf[...] = w0_ref[...].astype(BF16)

    def rows(x, h_ref, g_ref, p0_ref, cos_ref, sin_ref, pos0):
        h = _rmsnorm_rows(x, w_ref[...]).astype(h_ref.dtype)
        h_ref[...] = h
        gd = _dot_nt(h, wgd_bf[...])
        x = _dot(gd.astype(BF16), wup_bf[...]) + bup_ref[...]
        g_ref[...] = _log_sigmoid(x) * (LOG2_E / GLA_GATE_NORM)
        _rope_rows(cos_ref, sin_ref, pos0)
        p0_ref[...] = _dot_nt(h, w0_bf[...]).astype(p0_ref.dtype)

    rows(xp_ref[...], hp_ref, gp_ref, p0p_ref, cosp_ref, sinp_ref, i * cosp_ref.shape[0])

    @pl.when(i == 0)
    def _():
        rows(xs_ref[...], hs_ref, gs_ref, p0s_ref, cosd_ref, sind_ref, PAST_LEN)


def _rmsnorm_gate(x_p, x_s, w, w_in_t, gate_row0, w_gate_up, bup, seq, half, tn0):
    m_p, d = x_p.shape
    tail = x_s.shape[0]
    rank, gw = w_gate_up.shape
    tm = _row_tile(m_p, ROW_TILE // 2)
    n_steps = m_p // tm
    assert gate_row0 % V7X_LANES == 0 and seq % n_steps == 0 and rank % 16 == 0
    pos_rows = seq // n_steps
    table = pl.BlockSpec((pos_rows, half), lambda i: (i, 0))
    table_dec = pl.BlockSpec((8, half), lambda i: (0, 0))
    return pl.pallas_call(
        _rmsnorm_kernel,
        grid=(n_steps,),
        in_specs=[
            pl.BlockSpec((tm, d), lambda i: (i, 0)),
            pl.BlockSpec((tail, None, d), lambda i: (0, 0, 0)),
            pl.BlockSpec((1, d), lambda i: (0, 0)),
            pl.BlockSpec((V7X_LANES, d), lambda i: (gate_row0 // V7X_LANES, 0)),
            pl.BlockSpec((rank, gw), lambda i: (0, 0)),
            pl.BlockSpec((1, gw), lambda i: (0, 0)),
            pl.BlockSpec((tn0, d), lambda i: (0, 0)),
        ],
        out_specs=[
            pl.BlockSpec((tm, d), lambda i: (i, 0)), pl.BlockSpec((tail, d), lambda i: (0, 0)),
            pl.BlockSpec((tm, gw), lambda i: (i, 0)), pl.BlockSpec((tail, gw), lambda i: (0, 0)),
            table, table, table_dec, table_dec,
            pl.BlockSpec((tm, tn0), lambda i: (i, 0)), pl.BlockSpec((tail, tn0), lambda i: (0, 0)),
        ],
        out_shape=[
            jax.ShapeDtypeStruct((m_p, d), BF16), jax.ShapeDtypeStruct((tail, d), BF16),
            jax.ShapeDtypeStruct((m_p, gw), F32), jax.ShapeDtypeStruct((tail, gw), F32),
            jax.ShapeDtypeStruct((seq, half), F32), jax.ShapeDtypeStruct((seq, half), F32),
            jax.ShapeDtypeStruct((8, half), F32), jax.ShapeDtypeStruct((8, half), F32),
            jax.ShapeDtypeStruct((m_p, tn0), BF16), jax.ShapeDtypeStruct((tail, tn0), BF16),
        ],
        scratch_shapes=[pltpu.VMEM((V7X_LANES, d), BF16), pltpu.VMEM((V7X_LANES, gw), BF16),
                        pltpu.VMEM((tn0, d), BF16)],
        compiler_params=_params(1, claim_all=True),
        name="rmsnorm_gate",
    )(x_p, x_s, w.reshape(1, d), w_in_t, w_gate_up, bup, w_in_t)


def _in_proj_kernel(hp_ref, hs_ref, wm_ref, wn_ref, op_ref, os_ref, wbf_ref, *, n_plain, shift, first_tile):
    j = pl.program_id(0) + first_tile
    i = pl.program_id(1)
    tn = wbf_ref.shape[0]

    @pl.when(jnp.logical_and(i == 0, j < n_plain))
    def _():
        wbf_ref[...] = wm_ref[...].astype(BF16)

    @pl.when(jnp.logical_and(i == 0, j >= n_plain))
    def _():
        wbf_ref[0:tn - shift, :] = wm_ref[shift:tn, :].astype(BF16)
        wbf_ref[tn - shift:tn, :] = wn_ref[...].astype(BF16)

    @pl.when(i == 0)
    def _():
        os_ref[...] = _dot_nt(hs_ref[...], wbf_ref[...]).astype(os_ref.dtype)

    op_ref[...] = _dot_nt(hp_ref[...], wbf_ref[...]).astype(op_ref.dtype)


def _in_proj(h_p, h_s, w_in_t, plain_cols, shift, out_cols, tn, first_tile):
    m_p, d = h_p.shape
    tail = h_s.shape[0]
    tm = _row_tile(m_p, ROW_TILE_WIDE)
    assert plain_cols % tn == 0 and out_cols % tn == 0 and tn % shift == 0 and shift % 8 == 0
    n_plain = plain_cols // tn
    n_tiles = out_cols // tn - first_tile
    kern = functools.partial(_in_proj_kernel, n_plain=n_plain, shift=shift, first_tile=first_tile)
    return pl.pallas_call(
        kern,
        grid=(n_tiles, m_p // tm),
        in_specs=[
            pl.BlockSpec((tm, d), lambda j, i: (i, 0)),
            pl.BlockSpec((tail, d), lambda j, i: (0, 0)),
            pl.BlockSpec((tn, d), lambda j, i: (j + first_tile, 0)),
            pl.BlockSpec((shift, d), lambda j, i: ((j + first_tile + 1) * (tn // shift), 0)),
        ],
        out_specs=[pl.BlockSpec((tm, tn), lambda j, i: (i, j)), pl.BlockSpec((tail, tn), lambda j, i: (0, j))],
        out_shape=[jax.ShapeDtypeStruct((m_p, n_tiles * tn), BF16), jax.ShapeDtypeStruct((tail, n_tiles * tn), BF16)],
        scratch_shapes=[pltpu.VMEM((tn, d), BF16)],
        compiler_params=_params(
            2, _nbytes((tm, d), BF16), _nbytes((tail, d), BF16), _nbytes((tn, d), F32), _nbytes((shift, d), F32),
            _nbytes((tm, tn), BF16), _nbytes((tail, tn), BF16),
            scratch_bytes=_nbytes((tn, d), BF16) + _nbytes((tm, tn), F32)),
        name="in_proj",
    )(h_p, h_s, w_in_t, w_in_t)


def _prefix_sum_rows(sel3_bf16, g):
    g0 = g.astype(BF16)
    r1 = g - g0.astype(F32)
    g1 = r1.astype(BF16)
    g2 = (r1 - g1.astype(F32)).astype(BF16)
    return _dot(sel3_bf16, jnp.concatenate([g0, g1, g2], axis=0))


def _lane_bcast_cols(row, n):
    parts = []
    for c in range(n // V7X_LANES):
        tile = jnp.broadcast_to(row[:, c * V7X_LANES:(c + 1) * V7X_LANES], (V7X_LANES, V7X_LANES))
        parts.append(tile.T)
    return parts[0] if len(parts) == 1 else jnp.concatenate(parts, axis=0)


def _rms_gate_store(o, w, gate, out_ref, rows, cols):
    ms = jnp.mean(o * o, axis=-1, keepdims=True)
    y = o * lax.rsqrt(ms + EPS) * w
    out_ref[rows, cols] = (y * _silu(gate)).astype(out_ref.dtype)


def _ln_gate_store(o, w, gate, out_ref, rows, cols):
    mu = jnp.mean(o, axis=-1, keepdims=True)
    dlt = o - mu
    var = jnp.mean(dlt * dlt, axis=-1, keepdims=True)
    y = dlt * lax.rsqrt(var + EPS) * w
    out_ref[rows, cols] = (y * _silu(gate)).astype(out_ref.dtype)


def _token_selectors(n_tok):
    assert 3 * n_tok <= V7X_LANES
    j = lax.broadcasted_iota(jnp.int32, (V7X_LANES, V7X_LANES), 0)
    sel = []
    for t in range(n_tok):
        hit = jnp.logical_or(j == t, jnp.logical_or(j == n_tok + t, j == 2 * n_tok + t))
        sel.append(jnp.where(hit, 1.0, 0.0).astype(BF16))
    return jnp.stack(sel, axis=0)


def _column_source(x):
    n_tok, w = x.shape
    hi = x.astype(BF16).astype(F32)
    r1 = x - hi
    mid = r1.astype(BF16).astype(F32)
    lo = (r1 - mid).astype(BF16).astype(F32)
    x3 = jnp.concatenate([hi, mid, lo, jnp.zeros((V7X_LANES - 3 * n_tok, w), F32)], axis=0)
    parts = [x3[:, c * V7X_LANES:(c + 1) * V7X_LANES].T for c in range(w // V7X_LANES)]
    return (parts[0] if len(parts) == 1 else jnp.concatenate(parts, axis=0)).astype(BF16)


def _decode_advance(tok0, decay_rows_fn, decay_const_fn, k_ref, q_ref, v_ref, s_in_ref, s_out_ref, o_ref, sel_ref,
                    *, heads, dk, dv):
    n_tok = s_in_ref.shape[0]
    reps = dv // V7X_LANES
    rows = pl.ds(pl.multiple_of(tok0, n_tok), n_tok)

    def cols(src, tt):
        return jnp.concatenate([_dot(src, sel_ref[tt])] * reps, axis=1)

    for hh in range(heads):
        kc = slice(hh * dk, (hh + 1) * dk)
        vc = slice(hh * dv, (hh + 1) * dv)
        k_src, q_src = _column_source(k_ref[rows, kc]), _column_source(q_ref[rows, kc])
        a_src = None if decay_rows_fn is None else _column_source(decay_rows_fn(rows, hh))
        v = v_ref[rows, vc]
        o_rows = []
        for tt in range(n_tok):
            decay = decay_const_fn(hh) if a_src is None else cols(a_src, tt)
            s_new = decay * s_in_ref[tt, hh] + cols(k_src, tt) * v[tt:tt + 1, :]
            s_out_ref[tt, hh] = s_new
            o_rows.append(jnp.sum(cols(q_src, tt) * s_new, axis=0, keepdims=True))
        o_ref[rows, vc] = jnp.concatenate(o_rows, axis=0)


def _decode_plan(n_dec, n_steps):
    assert n_dec % n_steps == 0 and (n_dec // n_steps) % 8 == 0, (n_dec, n_steps)
    return n_dec // n_steps


def _gla_decode_section(step, last_step, refs, scratch, *, heads, dk, dv):
    qd_ref, kd_ref, vd_ref, gad_ref, gd_ref, nw_ref, sd_in_ref, od_ref, sd_out_ref = refs
    a_dec, q_dec, k_dec, v_dec, o_dec, sel_ref = scratch

    @pl.when(step == 0)
    def _():
        a_dec[...] = jnp.exp2(gd_ref[...])
        q_dec[...] = qd_ref[...].astype(F32) * (dk ** -0.5)
        k_dec[...] = kd_ref[...].astype(F32)
        v_dec[...] = vd_ref[...].astype(F32)
        sel_ref[...] = _token_selectors(sd_in_ref.shape[0])

    _decode_advance(step * sd_in_ref.shape[0], lambda rows, hh: a_dec[rows, hh * dk:(hh + 1) * dk], None,
                    k_dec, q_dec, v_dec, sd_in_ref, sd_out_ref, o_dec, sel_ref, heads=heads, dk=dk, dv=dv)

    @pl.when(step == last_step)
    def _():
        n_dec = o_dec.shape[0]
        for hh in range(heads):
            vc = slice(hh * dv, (hh + 1) * dv)
            _rms_gate_store(o_dec[:, vc], nw_ref[...], gad_ref[:, vc].astype(F32), od_ref, slice(0, n_dec), vc)


def _ret_decode_section(step, last_step, refs, scratch, *, heads, dk, dv):
    qd_ref, kd_ref, vd_ref, gbd_ref, cosd_ref, sind_ref, lg_ref, nw_ref, sd_in_ref, od_ref, sd_out_ref = refs
    q_dec, k_dec, v_dec, o_dec, sel_ref = scratch

    @pl.when(step == 0)
    def _():
        cosd, sind = cosd_ref[0:1, :], sind_ref[0:1, :]
        for hh in range(heads):
            kc = slice(hh * dk, (hh + 1) * dk)
            q_dec[:, kc] = _rotary(qd_ref[:, kc].astype(F32), cosd, sind)
            k_dec[:, kc] = _rotary(kd_ref[:, kc].astype(F32), cosd, sind) * (dk ** -0.5)
        v_dec[...] = vd_ref[...].astype(F32)
        sel_ref[...] = _token_selectors(sd_in_ref.shape[0])

    def gamma(hh):
        return jnp.exp(jnp.concatenate([lg_ref[hh]] * (dv // V7X_LANES), axis=1))

    _decode_advance(step * sd_in_ref.shape[0], None, gamma, k_dec, q_dec, v_dec, sd_in_ref, sd_out_ref, o_dec,
                    sel_ref, heads=heads, dk=dk, dv=dv)

    @pl.when(step == last_step)
    def _():
        n_dec = o_dec.shape[0]
        for hh in range(heads):
            vc = slice(hh * dv, (hh + 1) * dv)
            _ln_gate_store(o_dec[:, vc], nw_ref[...], gbd_ref[:, vc].astype(F32), od_ref, slice(0, n_dec), vc)


def _segment_block(lay, name, width):
    offset = lay["cols"][name][1]
    assert offset % width == 0
    return offset // width


def _segment_arrays(lay, arrays, names):
    return tuple(arrays[lay["cols"][n][0]] for n in names)


def _decode_specs(kind, lay, n_dec, tps, row):
    heads, dk, dv = lay[kind + "_heads"], lay[kind + "_dk"], lay[kind + "_dv"]
    qk, vw = heads * dk, heads * dv
    names = ("qa", "ka", "va", "ga") if kind == "gla" else ("qb", "kb", "vb", "gb")
    widths = (qk, qk, vw, vw)
    rows_in = [pl.BlockSpec((n_dec, w), lambda b, t, c=_segment_block(lay, n, w): (0, c))
               for n, w in zip(names, widths)]
    state = pl.BlockSpec((tps, heads, dk, dv), lambda b, t: (row(b, t), 0, 0, 0))
    o_spec = pl.BlockSpec((n_dec, vw), lambda b, t: (0, 0))
    n_qk = 3 if kind == "gla" else 2
    scratch = ([pltpu.VMEM((n_dec, qk), F32)] * n_qk + [pltpu.VMEM((n_dec, vw), F32)] * 2
               + [pltpu.VMEM((tps, V7X_LANES, V7X_LANES), BF16)])
    return rows_in, state, o_spec, scratch


def _gla_sum_matrices(c):
    levels = c.bit_length() - 1
    assert 1 << levels == c
    i = lax.broadcasted_iota(jnp.int32, (c, c), 0)
    j = lax.broadcasted_iota(jnp.int32, (c, c), 1)
    mats = [j <= i]
    for l in range(levels):
        ref = jnp.bitwise_or(jnp.bitwise_and(i, -(2 << l)), 1 << l)
        mats.append(jnp.logical_and(j > jnp.minimum(i, ref), j <= jnp.maximum(i, ref)))
    mats.append(j > i)
    sel = jnp.concatenate([jnp.where(m, 1.0, 0.0).astype(BF16) for m in mats], axis=0)
    return jnp.concatenate([sel, sel, sel], axis=1)


def _pair_level(c):
    levels = c.bit_length() - 1
    i = lax.broadcasted_iota(jnp.int32, (c, c), 0)
    j = lax.broadcasted_iota(jnp.int32, (c, c), 1)
    x = jnp.bitwise_xor(i, j)
    lvl = jnp.zeros((c, c), jnp.int32)
    for l in range(1, levels):
        lvl = lvl + jnp.where(x >= (1 << l), 1, 0)
    return jnp.where(i > j, lvl, jnp.where(i == j, levels, -1))


def _queries_else_keys(q, k, l):
    c = q.shape[0]
    span = 1 << l
    if span >= 8:
        parts = [(q if (b & 1) else k)[b * span:(b + 1) * span, :] for b in range(c // span)]
        return jnp.concatenate(parts, axis=0)
    row = lax.broadcasted_iota(jnp.int32, q.shape, 0)
    return jnp.where(jnp.bitwise_and(row, span) != 0, q, k)


def _gla_level_scores(q, k, sums):
    c = q.shape[0]
    levels = c.bit_length() - 1
    out = []
    for l in range(levels):
        x = _queries_else_keys(q, k, l) * jnp.exp2(sums[(1 + l) * c:(2 + l) * c, :])
        xb = x.astype(BF16)
        out.append(_dot_nt(xb, xb))
    return out


def _gla_chunk_out(q, k, v, sums, level_scores, pair_level, state):
    c = q.shape[0]
    levels = c.bit_length() - 1
    scores = jnp.where(pair_level == levels, jnp.sum(q * k, axis=-1, keepdims=True), 0.0)
    for l in range(levels):
        scores = jnp.where(pair_level == l, level_scores[l], scores)
    o = _dot((q * jnp.exp2(sums[0:c, :])).astype(BF16), state.astype(BF16))
    return o + _dot(scores.astype(BF16), v)


def _gla_next_state(k, v, sums, state):
    c, dk = k.shape
    levels = c.bit_length() - 1
    k_tail = (k * jnp.exp2(sums[(levels + 1) * c:(levels + 2) * c, :])).astype(BF16)
    decay = _lane_bcast_cols(jnp.exp2(sums[c - 1:c, :]), dk)
    decay_full = jnp.concatenate([decay] * (v.shape[1] // V7X_LANES), axis=1)
    return decay_full * state + _dot_tn(k_tail, v)


def _gla_kernel(*refs, heads, dk, dv, dec_dims):
    q_ref, k_ref, v_ref, ga_ref, g_ref, nw_ref = refs[:6]
    dec_in = refs[6:15]
    o_ref, s_out_ref = refs[15:17]
    dec_out = refs[17:19]
    s_ref, mats_ref, lvl_ref = refs[19:22]
    dec_scratch = refs[22:]
    t = pl.program_id(1)
    step = pl.program_id(0) * pl.num_programs(1) + t
    last_step = pl.num_programs(0) * pl.num_programs(1) - 1

    @pl.when(t == 0)
    def _():
        s_ref[...] = jnp.zeros_like(s_ref)
        mats_ref[...] = _gla_sum_matrices(GLA_CHUNK)
        lvl_ref[...] = _pair_level(GLA_CHUNK)

    _ret_decode_section(step, last_step, dec_in + dec_out, dec_scratch,
                        heads=dec_dims[0], dk=dec_dims[1], dv=dec_dims[2])

    ct = q_ref.shape[0]

    kcs = [slice(hh * dk, (hh + 1) * dk) for hh in range(heads)]
    vcs = [slice(hh * dv, (hh + 1) * dv) for hh in range(heads)]
    group = 2 if (ct // GLA_CHUNK) % 2 == 0 else 1

    def chunk_group(cg, carry):
        pair_level = lvl_ref[...]
        rows, sums, qs, ks, lvl_scores = [], [], [], [], []
        for u in range(group):
            r = pl.ds(pl.multiple_of((cg * group + u) * GLA_CHUNK, GLA_CHUNK), GLA_CHUNK)
            rows.append(r)
            sums.append(_prefix_sum_rows(mats_ref[...], g_ref[r, :]))
            qs.append([q_ref[r, kc].astype(F32) * (dk ** -0.5) for kc in kcs])
            ks.append([k_ref[r, kc].astype(F32) for kc in kcs])
            lvl_scores.append([_gla_level_scores(qs[u][hh], ks[u][hh], sums[u][:, kcs[hh]]) for hh in range(heads)])
        outs = []
        for u in range(group):
            r = rows[u]
            outs.append([_gla_chunk_out(qs[u][hh], ks[u][hh], v_ref[r, vcs[hh]], sums[u][:, kcs[hh]],
                                        lvl_scores[u][hh], pair_level, s_ref[hh]) for hh in range(heads)])
            for hh in range(heads):
                s_ref[hh] = _gla_next_state(ks[u][hh], v_ref[r, vcs[hh]], sums[u][:, kcs[hh]], s_ref[hh])
        for u in range(group):
            for hh in range(heads):
                _rms_gate_store(outs[u][hh], nw_ref[...], ga_ref[rows[u], vcs[hh]].astype(F32), o_ref, rows[u],
                                vcs[hh])
        return carry

    lax.fori_loop(0, ct // GLA_CHUNK // group, chunk_group, 0)

    @pl.when(t == pl.num_programs(1) - 1)
    def _():
        s_out_ref[0] = s_ref[...]


def _gla_prompt_ret_decode(projs, log2_decay, gla_norm_w, projs_dec, cos_dec, sin_dec, log_gamma, ret_norm_w,
                           ret_state_dec, lay, batch, seq):
    heads, dk, dv = lay["gla_heads"], lay["gla_dk"], lay["gla_dv"]
    qk, vw = heads * dk, heads * dv
    r_heads, r_dk, r_dv = lay["ret_heads"], lay["ret_dk"], lay["ret_dv"]
    ct = min(seq, GLA_STEP_CHUNKS * GLA_CHUNK)
    levels = GLA_CHUNK.bit_length() - 1
    assert seq % ct == 0 and ct % GLA_CHUNK == 0
    nt = seq // ct
    n_dec = projs_dec[0].shape[0]
    tps = _decode_plan(n_dec, batch * nt)
    row = lambda b, t: b * nt + t
    dec_rows, dec_state, dec_o, dec_scratch = _decode_specs("ret", lay, n_dec, tps, row)
    table_dec = pl.BlockSpec((cos_dec.shape[0], r_dk // 2), lambda b, t: (0, 0))
    kern = functools.partial(_gla_kernel, heads=heads, dk=dk, dv=dv, dec_dims=(r_heads, r_dk, r_dv))
    return pl.pallas_call(
        kern,
        grid=(batch, nt),
        in_specs=[
            pl.BlockSpec((ct, qk), lambda b, t: (row(b, t), _segment_block(lay, "qa", qk))),
            pl.BlockSpec((ct, qk), lambda b, t: (row(b, t), _segment_block(lay, "ka", qk))),
            pl.BlockSpec((ct, vw), lambda b, t: (row(b, t), _segment_block(lay, "va", vw))),
            pl.BlockSpec((ct, vw), lambda b, t: (row(b, t), _segment_block(lay, "ga", vw))),
            pl.BlockSpec((ct, qk), lambda b, t: (row(b, t), 0)),
            pl.BlockSpec((1, dv), lambda b, t: (0, 0)),
        ] + dec_rows + [
            table_dec, table_dec,
            pl.BlockSpec((r_heads, 1, V7X_LANES), lambda b, t: (0, 0, 0)),
            pl.BlockSpec((1, r_dv), lambda b, t: (0, 0)),
            dec_state,
        ],
        out_specs=[
            pl.BlockSpec((ct, vw), lambda b, t: (row(b, t), 0)),
            pl.BlockSpec((1, heads, dk, dv), lambda b, t: (b, 0, 0, 0)),
            dec_o,
            dec_state,
        ],
        out_shape=[
            jax.ShapeDtypeStruct((batch * seq, vw), BF16),
            jax.ShapeDtypeStruct((batch, heads, dk, dv), F32),
            jax.ShapeDtypeStruct((n_dec, r_heads * r_dv), BF16),
            jax.ShapeDtypeStruct(ret_state_dec.shape, ret_state_dec.dtype),
        ],
        scratch_shapes=[
            pltpu.VMEM((heads, dk, dv), F32),
            pltpu.VMEM(((levels + 2) * GLA_CHUNK, 3 * GLA_CHUNK), BF16),
            pltpu.VMEM((GLA_CHUNK, GLA_CHUNK), jnp.int32),
        ] + dec_scratch,
        compiler_params=_params(2, claim_all=True),
        name="gla_prompt_ret_decode",
    )(*_segment_arrays(lay, projs, ("qa", "ka", "va", "ga")), log2_decay, gla_norm_w,
      *_segment_arrays(lay, projs_dec, ("qb", "kb", "vb", "gb")), cos_dec, sin_dec,
      log_gamma, ret_norm_w, ret_state_dec)


def _rotary(x, cos, sin):
    half = x.shape[1] // 2
    x1, x2 = x[:, :half], x[:, half:]
    return jnp.concatenate([x1 * cos - x2 * sin, x1 * sin + x2 * cos], axis=1)


def _ret_kernel(*refs, heads, dk, dv, c, dec_dims):
    q_ref, k_ref, v_ref, gb_ref, cos_ref, sin_ref, lg_ref, nw_ref = refs[:8]
    dec_in = refs[8:15]
    o_ref, s_out_ref = refs[15:17]
    dec_out = refs[17:19]
    s_ref, dmat_ref, qdec_ref, kdec_ref = refs[19:23]
    dec_scratch = refs[23:]
    t = pl.program_id(1)
    step = pl.program_id(0) * pl.num_programs(1) + t
    last_step = pl.num_programs(0) * pl.num_programs(1) - 1

    _gla_decode_section(step, last_step, dec_in + dec_out, dec_scratch,
                        heads=dec_dims[0], dk=dec_dims[1], dv=dec_dims[2])

    @pl.when(t == 0)
    def _():
        s_ref[...] = jnp.zeros_like(s_ref)
        ri = lax.broadcasted_iota(jnp.int32, (c, c), 0)
        rj = lax.broadcasted_iota(jnp.int32, (c, c), 1)
        dist = (ri - rj).astype(F32)
        rowl = lax.broadcasted_iota(jnp.int32, (c, V7X_LANES), 0).astype(F32)
        for hh in range(heads):
            lg = lg_ref[hh]
            dmat_ref[hh] = jnp.exp(jnp.where(ri >= rj, dist * lg[:, :1], -jnp.inf))
            qdec_ref[hh] = jnp.exp((rowl + 1.0) * lg)
            kdec_ref[hh] = jnp.exp((float(c - 1) - rowl) * lg)

    ct = q_ref.shape[0]
    kcs = [slice(hh * dk, (hh + 1) * dk) for hh in range(heads)]
    vcs = [slice(hh * dv, (hh + 1) * dv) for hh in range(heads)]

    def chunk(ci, carry):
        rows = pl.ds(pl.multiple_of(ci * c, c), c)
        cos, sin = cos_ref[rows, :], sin_ref[rows, :]
        qrs = [_rotary(q_ref[rows, kc].astype(F32), cos, sin).astype(BF16) for kc in kcs]
        krs = [_rotary(k_ref[rows, kc].astype(F32), cos, sin) * (dk ** -0.5) for kc in kcs]
        scores = [_dot_nt(qrs[hh], krs[hh].astype(BF16)) * dmat_ref[hh] for hh in range(heads)]
        outs = []
        for hh in range(heads):
            qdec = jnp.concatenate([qdec_ref[hh]] * (dv // V7X_LANES), axis=1)
            o = qdec * _dot(qrs[hh], s_ref[hh].astype(BF16))
            outs.append(o + _dot(scores[hh].astype(BF16), v_ref[rows, vcs[hh]]))
        for hh in range(heads):
            kdec = jnp.concatenate([kdec_ref[hh]] * (dk // V7X_LANES), axis=1)
            k_tail = (krs[hh] * kdec).astype(BF16)
            lgv = jnp.concatenate([lg_ref[hh]] * (dv // V7X_LANES), axis=1)
            s_ref[hh] = jnp.exp(float(c) * lgv) * s_ref[hh] + _dot_tn(k_tail, v_ref[rows, vcs[hh]])
        for hh in range(heads):
            _ln_gate_store(outs[hh], nw_ref[...], gb_ref[rows, vcs[hh]].astype(F32), o_ref, rows, vcs[hh])
        return carry

    lax.fori_loop(0, ct // c, chunk, 0)

    @pl.when(t == pl.num_programs(1) - 1)
    def _():
        s_out_ref[0] = s_ref[...]


def _ret_prompt_gla_decode(projs, cos, sin, log_gamma, ret_norm_w, projs_dec, log2_decay_dec, gla_norm_w,
                           gla_state_dec, lay, batch, seq):
    heads, dk, dv = lay["ret_heads"], lay["ret_dk"], lay["ret_dv"]
    qk, vw = heads * dk, heads * dv
    g_heads, g_dk, g_dv = lay["gla_heads"], lay["gla_dk"], lay["gla_dv"]
    c = min(seq, RET_CHUNK)
    ct = min(seq, RET_STEP_CHUNKS * c)
    assert seq % ct == 0 and ct % c == 0
    nt = seq // ct
    half = dk // 2
    n_dec = projs_dec[0].shape[0]
    tps = _decode_plan(n_dec, batch * nt)
    row = lambda b, t: b * nt + t
    dec_rows, dec_state, dec_o, dec_scratch = _decode_specs("gla", lay, n_dec, tps, row)
    kern = functools.partial(_ret_kernel, heads=heads, dk=dk, dv=dv, c=c, dec_dims=(g_heads, g_dk, g_dv))
    return pl.pallas_call(
        kern,
        grid=(batch, nt),
        in_specs=[
            pl.BlockSpec((ct, qk), lambda b, t: (row(b, t), _segment_block(lay, "qb", qk))),
            pl.BlockSpec((ct, qk), lambda b, t: (row(b, t), _segment_block(lay, "kb", qk))),
            pl.BlockSpec((ct, vw), lambda b, t: (row(b, t), _segment_block(lay, "vb", vw))),
            pl.BlockSpec((ct, vw), lambda b, t: (row(b, t), _segment_block(lay, "gb", vw))),
            pl.BlockSpec((ct, half), lambda b, t: (t, 0)),
            pl.BlockSpec((ct, half), lambda b, t: (t, 0)),
            pl.BlockSpec((heads, 1, V7X_LANES), lambda b, t: (0, 0, 0)),
            pl.BlockSpec((1, dv), lambda b, t: (0, 0)),
        ] + dec_rows + [
            pl.BlockSpec((n_dec, g_heads * g_dk), lambda b, t: (0, 0)),
            pl.BlockSpec((1, g_dv), lambda b, t: (0, 0)),
            dec_state,
        ],
        out_specs=[
            pl.BlockSpec((ct, vw), lambda b, t: (row(b, t), 0)),
            pl.BlockSpec((1, heads, dk, dv), lambda b, t: (b, 0, 0, 0)),
            dec_o,
            dec_state,
        ],
        out_shape=[
            jax.ShapeDtypeStruct((batch * seq, vw), BF16),
            jax.ShapeDtypeStruct((batch, heads, dk, dv), F32),
            jax.ShapeDtypeStruct((n_dec, g_heads * g_dv), BF16),
            jax.ShapeDtypeStruct(gla_state_dec.shape, gla_state_dec.dtype),
        ],
        scratch_shapes=[
            pltpu.VMEM((heads, dk, dv), F32),
            pltpu.VMEM((heads, c, c), F32),
            pltpu.VMEM((heads, c, V7X_LANES), F32),
            pltpu.VMEM((heads, c, V7X_LANES), F32),
        ] + dec_scratch,
        compiler_params=_params(2, claim_all=True),
        name="ret_prompt_gla_decode",
    )(*_segment_arrays(lay, projs, ("qb", "kb", "vb", "gb")), cos, sin, log_gamma, ret_norm_w,
      *_segment_arrays(lay, projs_dec, ("qa", "ka", "va", "ga")), log2_decay_dec, gla_norm_w, gla_state_dec)


def _merge_kernel(oap_ref, obp_ref, oas_ref, obs_ref, wa_ref, wb_ref, g0p_ref, g1p_ref, g0s_ref, g1s_ref, wnext_ref,
                  mp_ref, ms_ref, wnext_bf_ref, wa_bf, wb_bf):
    wnext_bf_ref[...] = wnext_ref[...].astype(BF16)

    def merged(oa, ob, g0, g1):
        ya = _dot(oa, wa_bf[...])
        yb = _dot(ob, wb_bf[...])
        return _sigmoid(g0.astype(F32)) * ya + _sigmoid(g1.astype(F32)) * yb

    @pl.when(pl.program_id(1) == 0)
    def _():
        wa_bf[...] = wa_ref[...].astype(BF16)
        wb_bf[...] = wb_ref[...].astype(BF16)
        ms_ref[...] = merged(oas_ref[...], obs_ref[...], g0s_ref[...], g1s_ref[...]).astype(ms_ref.dtype)

    mp_ref[...] = merged(oap_ref[...], obp_ref[...], g0p_ref[...], g1p_ref[...]).astype(mp_ref.dtype)


def _slab_specs(w_next, n_steps, step_of):
    kn, dn = w_next.shape
    assert kn % n_steps == 0 and (kn // n_steps) % 16 == 0, (kn, n_steps)
    slab = kn // n_steps
    spec = pl.BlockSpec((slab, dn), lambda j, i: (step_of(j, i), 0))
    return spec, spec, jax.ShapeDtypeStruct((kn, dn), BF16), _nbytes((slab, dn), F32) + _nbytes((slab, dn), BF16)


def _merge(oa_p, ob_p, oa_s, ob_s, wa, wb, projs_p, projs_s, lay, w_next):
    proj_p, proj_s = projs_p[lay["cols"]["mg"][0]], projs_s[lay["cols"]["mg"][0]]
    m_p, ka = oa_p.shape
    kb = ob_p.shape[1]
    tail = oa_s.shape[0]
    d = wa.shape[1]
    tm = _row_tile(m_p, ROW_TILE)
    tn = min(d, 1024)
    mg = lay["cols"]["mg"][1]
    assert d % tn == 0 and mg % tn == 0
    g0 = mg // tn
    g1 = (mg + d) // tn
    n_m = m_p // tm
    slab_in, slab_out, slab_shape, slab_bytes = _slab_specs(w_next, (d // tn) * n_m, lambda j, i: j * n_m + i)
    return pl.pallas_call(
        _merge_kernel,
        grid=(d // tn, n_m),
        in_specs=[
            pl.BlockSpec((tm, ka), lambda j, i: (i, 0)),
            pl.BlockSpec((tm, kb), lambda j, i: (i, 0)),
            pl.BlockSpec((tail, ka), lambda j, i: (0, 0)),
            pl.BlockSpec((tail, kb), lambda j, i: (0, 0)),
            pl.BlockSpec((ka, tn), lambda j, i: (0, j)),
            pl.BlockSpec((kb, tn), lambda j, i: (0, j)),
            pl.BlockSpec((tm, tn), lambda j, i: (i, g0 + j)),
            pl.BlockSpec((tm, tn), lambda j, i: (i, g1 + j)),
            pl.BlockSpec((tail, tn), lambda j, i: (0, g0 + j)),
            pl.BlockSpec((tail, tn), lambda j, i: (0, g1 + j)),
            slab_in,
        ],
        out_specs=[pl.BlockSpec((tm, tn), lambda j, i: (i, j)), pl.BlockSpec((tail, tn), lambda j, i: (0, j)),
                   slab_out],
        out_shape=[jax.ShapeDtypeStruct((m_p, d), BF16), jax.ShapeDtypeStruct((tail, d), BF16), slab_shape],
        scratch_shapes=[pltpu.VMEM((ka, tn), BF16), pltpu.VMEM((kb, tn), BF16)],
        compiler_params=_params(
            2, _nbytes((tm, ka), BF16), _nbytes((tm, kb), BF16), _nbytes((tail, ka), BF16), _nbytes((tail, kb), BF16),
            _nbytes((ka, tn), F32), _nbytes((kb, tn), F32), 3 * _nbytes((tm, tn), BF16), 3 * _nbytes((tail, tn), BF16),
            slab_bytes,
            scratch_bytes=_nbytes((ka, tn), BF16) + _nbytes((kb, tn), BF16) + 3 * _nbytes((tm, tn), F32)),
        name="merge",
    )(oa_p, ob_p, oa_s, ob_s, wa, wb, proj_p, proj_p, proj_s, proj_s, w_next)


def _proj_res_norm_kernel(*refs, emit_sum, group, n_blocks):
    ap_refs, as_refs, w_refs = refs[:group], refs[group:2 * group], refs[2 * group:3 * group]
    resp_ref, ress_ref, nw_ref = refs[3 * group:3 * group + 3]
    out_refs = refs[3 * group + 3:]
    n_out = 2 if emit_sum else 1
    outs_p, outs_s = out_refs[:n_out], out_refs[n_out:]
    i = pl.program_id(0)
    k = pl.program_id(1)
    last_k = k == pl.num_programs(1) - 1
    d = w_refs[0].shape[1]
    col_chunk = min(d, 512)
    rest = n_blocks % group

    def step(a_refs, res_ref, outs):
        acc_ref = outs[0]
        nrow = acc_ref.shape[0]
        row_chunk = min(nrow, 128)
        assert nrow % row_chunk == 0

        def accumulate(n_used, base_ref):
            a = [a_refs[s][...] for s in range(n_used)]
            for c in range(d // col_chunk):
                cs = slice(c * col_chunk, (c + 1) * col_chunk)
                part = _dot(a[0], w_refs[0][:, cs])
                for s in range(1, n_used):
                    part = part + _dot(a[s], w_refs[s][:, cs])
                acc_ref[:, cs] = base_ref[:, cs] + part

        if n_blocks <= group:
            accumulate(n_blocks, res_ref)
        else:
            @pl.when(k == 0)
            def _():
                acc_ref[...] = res_ref[...]

            if rest == 0:
                accumulate(group, acc_ref)
            else:
                pl.when(jnp.logical_not(last_k))(lambda: accumulate(group, acc_ref))
                pl.when(last_k)(lambda: accumulate(rest, acc_ref))

        @pl.when(last_k)
        def _():
            def body(c, carry):
                rr = pl.ds(pl.multiple_of(c * row_chunk, row_chunk), row_chunk)
                y = _rmsnorm_rows(acc_ref[rr, :], nw_ref[...])
                if emit_sum:
                    outs[1][rr, :] = y.astype(outs[1].dtype)
                else:
                    acc_ref[rr, :] = y
                return carry

            lax.fori_loop(0, nrow // row_chunk, body, 0)

    step(ap_refs, resp_ref, outs_p)

    @pl.when(i == pl.num_programs(0) - 1)
    def _():
        step(as_refs, ress_ref, outs_s)


def _proj_res_norm(a_p, a_s, w, res_p, res_s, norm_w, emit_sum, group, row_tile):
    m_p, kdim = a_p.shape
    tail = a_s.shape[0]
    d = w.shape[1]
    tm = _row_tile(m_p, row_tile)
    tk = min(kdim, K_BLOCK)
    assert kdim % tk == 0
    n_blocks = kdim // tk
    group = min(group, n_blocks)
    n_steps = pl.cdiv(n_blocks, group)
    blk = lambda k, s: jnp.minimum(k * group + s, n_blocks - 1)
    p_spec = pl.BlockSpec((tm, d), lambda i, k: (i, 0))
    s_spec = pl.BlockSpec((tail, d), lambda i, k: (0, 0))
    s3_spec = pl.BlockSpec((tail, None, d), lambda i, k: (0, 0, 0))
    res_s_spec = s3_spec if res_s.ndim == 3 else s_spec
    out_specs = [p_spec, s3_spec]
    out_shape = [jax.ShapeDtypeStruct((m_p, d), F32), jax.ShapeDtypeStruct((tail, 1, d), F32)]
    assert w.dtype == BF16
    resident = n_steps == 1
    w_mode = pl.Buffered(1) if resident else None
    w_bytes = group * _nbytes((tk, d), BF16) // (2 if resident else 1)
    blocks = [group * _nbytes((tm, tk), BF16), group * _nbytes((tail, tk), BF16), w_bytes,
              2 * _nbytes((tm, d), F32), 2 * _nbytes((tail, d), F32)]
    if emit_sum:
        out_specs = [p_spec, p_spec, s_spec, s_spec]
        out_shape = [out_shape[0], jax.ShapeDtypeStruct((m_p, d), BF16),
                     jax.ShapeDtypeStruct((tail, d), F32), jax.ShapeDtypeStruct((tail, d), BF16)]
        blocks += [_nbytes((tm, d), BF16), _nbytes((tail, d), BF16)]
    in_specs = (
        [pl.BlockSpec((tm, tk), lambda i, k, s=s: (i, blk(k, s))) for s in range(group)]
        + [pl.BlockSpec((tail, tk), lambda i, k, s=s: (0, blk(k, s))) for s in range(group)]
        + [pl.BlockSpec((tk, d), lambda i, k, s=s: (blk(k, s), 0), pipeline_mode=w_mode) for s in range(group)]
        + [p_spec, res_s_spec, pl.BlockSpec((1, d), lambda i, k: (0, 0))])
    return pl.pallas_call(
        functools.partial(_proj_res_norm_kernel, emit_sum=emit_sum, group=group, n_blocks=n_blocks),
        grid=(m_p // tm, n_steps),
        in_specs=in_specs,
        out_specs=out_specs,
        out_shape=out_shape,
        compiler_params=_params(2, *blocks),
        name="proj_res_norm",
    )(*([a_p] * group + [a_s] * group + [w] * group), res_p, res_s, norm_w.reshape(1, d))


def _swiglu_kernel(hp_ref, hs_ref, wg_ref, wu_ref, wnext_ref, op_ref, os_ref, wnext_bf_ref, wg_bf, wu_bf):
    wnext_bf_ref[...] = wnext_ref[...].astype(BF16)

    tn = wg_bf.shape[1]
    col_chunk = min(tn, 256)

    def act(h_ref, o_ref):
        h = h_ref[...]
        for c in range(tn // col_chunk):
            cs = slice(c * col_chunk, (c + 1) * col_chunk)
            a = _dot(h, wg_bf[:, cs])
            b = _dot(h, wu_bf[:, cs])
            o_ref[:, cs] = (_silu(a) * b).astype(o_ref.dtype)

    @pl.when(pl.program_id(1) == 0)
    def _():
        wg_bf[...] = wg_ref[...].astype(BF16)
        wu_bf[...] = wu_ref[...].astype(BF16)
        act(hs_ref, os_ref)

    act(hp_ref, op_ref)


def _swiglu(h_p, h_s, wg, wu, w_next):
    m_p, d = h_p.shape
    tail = h_s.shape[0]
    f = wg.shape[1]
    tm = _row_tile(m_p, ROW_TILE_WIDE)
    tn = 512 if f % 512 == 0 else 256
    assert f % tn == 0
    n_m = m_p // tm
    slab_in, slab_out, slab_shape, slab_bytes = _slab_specs(w_next, (f // tn) * n_m, lambda j, i: j * n_m + i)
    return pl.pallas_call(
        _swiglu_kernel,
        grid=(f // tn, n_m),
        in_specs=[
            pl.BlockSpec((tm, d), lambda j, i: (i, 0)),
            pl.BlockSpec((tail, d), lambda j, i: (0, 0)),
            pl.BlockSpec((d, tn), lambda j, i: (0, j)),
            pl.BlockSpec((d, tn), lambda j, i: (0, j)),
            slab_in,
        ],
        out_specs=[pl.BlockSpec((tm, tn), lambda j, i: (i, j)), pl.BlockSpec((tail, tn), lambda j, i: (0, j)),
                   slab_out],
        out_shape=[jax.ShapeDtypeStruct((m_p, f), BF16), jax.ShapeDtypeStruct((tail, f), BF16), slab_shape],
        scratch_shapes=[pltpu.VMEM((d, tn), BF16), pltpu.VMEM((d, tn), BF16)],
        compiler_params=_params(
            2, _nbytes((tm, d), BF16), _nbytes((tail, d), BF16), 2 * _nbytes((d, tn), F32),
            _nbytes((tm, tn), BF16), _nbytes((tail, tn), BF16), slab_bytes,
            scratch_bytes=2 * _nbytes((d, tn), BF16) + 3 * _nbytes((tm, tn), F32)),
        name="swiglu",
    )(h_p, h_s, wg, wu, w_next)


def _layout(d_model, in_width, state_gla, state_ret, gate_rank):
    _, _, gh, gdk, gdv = state_gla.shape
    _, _, rh, rdk, rdv = state_ret.shape
    gqk, gv, rqk, rv = gh * gdk, gh * gdv, rh * rdk, rh * rdv
    lay = dict(gla_heads=gh, gla_dk=gdk, gla_dv=gdv, ret_heads=rh, ret_dk=rdk, ret_dv=rdv, rank=gate_rank)
    off = 0
    for name, width in (("qa", gqk), ("ka", gqk), ("va", gv), ("ga", gv), ("qb", rqk), ("kb", rqk),
                        ("vb", rv), ("gb", rv), ("mg", 2 * d_model)):
        lay[name] = off
        off += width
    lay["out_cols"] = off
    lay["plain_cols"] = 2 * gqk + gv
    lay["gd_src"] = lay["plain_cols"]
    assert lay["gd_src"] % V7X_LANES == 0 and gate_rank <= V7X_LANES
    assert in_width == off + gate_rank
    lay["widths"] = dict(qa=gqk, ka=gqk, va=gv, ga=gv, qb=rqk, kb=rqk, vb=rv, gb=rv, mg=2 * d_model)
    return lay


def _split_columns(lay, first_cols):
    cols = {}
    for name, width in lay["widths"].items():
        off = lay[name]
        assert off + width <= first_cols or off >= first_cols, "a segment straddles the two arrays"
        cols[name] = (0, off) if off < first_cols else (1, off - first_cols)
    return cols


def _layer(x_p, x_s, st_gla, st_ret, wts, lay, log_gamma, final_norm):
    (norm_mix, w_in, w_gate_up, b_gate, gla_norm_w, w_gla_up, ret_norm_w, w_ret_up, w_out, norm_ffn,
     w_ffn_gate, w_ffn_up, w_ffn_down) = wts
    batch, seq, d = x_p.shape
    rank = lay["rank"]
    gqk = lay["gla_heads"] * lay["gla_dk"]
    bup = b_gate.reshape(1, gqk)
    gnw = gla_norm_w.reshape(1, -1)
    rnw = ret_norm_w.reshape(1, -1)
    tn = 1024 if (lay["out_cols"] % 1024 == 0 and lay["plain_cols"] % 1024 == 0) else 512
    xp = x_p.reshape(batch * seq, d)
    assert x_s.ndim == 3 and x_s.shape[1] == 1, "one new token per decode sequence"
    xs = x_s
    w_in_t = w_in.T

    h_p, h_s, g_p, g_s, cos_p, sin_p, cos_s, sin_s, first_p, first_s = _rmsnorm_gate(
        xp, xs, norm_mix, w_in_t, lay["gd_src"], w_gate_up, bup, seq, lay["ret_dk"] // 2, tn)
    rest_p, rest_s = _in_proj(h_p, h_s, w_in_t, lay["plain_cols"], rank, lay["out_cols"], tn, 1)
    projs_p, projs_s = (first_p, rest_p), (first_s, rest_s)
    lay = dict(lay, cols=_split_columns(lay, tn))
    oa_p, sa_p, ob_s, sb_s = _gla_prompt_ret_decode(
        projs_p, g_p, gnw, projs_s, cos_s, sin_s, log_gamma, rnw, st_ret, lay, batch, seq)
    ob_p, sb_p, oa_s, sa_s = _ret_prompt_gla_decode(
        projs_p, cos_p, sin_p, log_gamma, rnw, projs_s, g_s, gnw, st_gla, lay, batch, seq)
    m_p, m_s, w_out_bf = _merge(oa_p, ob_p, oa_s, ob_s, w_gla_up, w_ret_up, projs_p, projs_s, lay, w_out)
    x1_p, h2_p, x1_s, h2_s = _proj_res_norm(m_p, m_s, w_out_bf, xp, xs, norm_ffn, True, 4, ROW_TILE // 2)
    act_p, act_s, w_down_bf = _swiglu(h2_p, h2_s, w_ffn_gate, w_ffn_up, w_ffn_down)
    y_p, y_s = _proj_res_norm(act_p, act_s, w_down_bf, x1_p, x1_s, final_norm, False, 11, ROW_TILE // 2)
    return (y_p, sa_p, sb_p), (y_s, sa_s, sb_s)


def kernel(x_prompt, x_sample, state_gla, state_ret, norm_mix, w_in, w_gla_gate_up, b_gla_gate, gla_norm_w,
           w_gla_up, ret_norm_w, w_ret_up, w_out, norm_ffn, w_ffn_gate, w_ffn_up, w_ffn_down, norm_final):
    depth = w_in.shape[0]
    assert depth == 1, "single-layer trunk"
    batch, seq, d = x_prompt.shape
    lay = _layout(d, w_in.shape[-1], state_gla, state_ret, w_gla_gate_up.shape[1])
    rh, rdk = lay["ret_heads"], lay["ret_dk"]
    assert rdk // 2 == V7X_LANES
    lg = jnp.log1p(-jnp.exp(jnp.linspace(math.log(1.0 / 32), math.log(1.0 / 512), rh))).astype(F32)
    log_gamma = jnp.broadcast_to(lg[:, None, None], (rh, 1, V7X_LANES))

    wts = (norm_mix[0], w_in[0], w_gla_gate_up[0], b_gla_gate[0], gla_norm_w[0], w_gla_up[0], ret_norm_w[0],
           w_ret_up[0], w_out[0], norm_ffn[0], w_ffn_gate[0], w_ffn_up[0], w_ffn_down[0])
    (y_p, ga_p, re_p), (y_s, ga_s, re_s) = _layer(
        x_prompt, x_sample, state_gla[0], state_ret[0], wts, lay, log_gamma, norm_final)

    sd = state_gla.dtype
    return (y_p.reshape(batch, seq, d), y_s.reshape(x_sample.shape),
            ga_p[None].astype(sd), re_p[None].astype(state_ret.dtype),
            ga_s[None].astype(sd), re_s[None].astype(state_ret.dtype))
```

```python
import functools
import math

import numpy as np
import jax
import jax.numpy as jnp
from jax import lax
from jax.experimental import pallas as pl
from jax.experimental.pallas import tpu as pltpu

EPS = 1e-6
ROPE_BASE = 10000.0
GLA_GATE_NORM = 16.0
PAST_LEN = 16384

V7X_LANES = 128
V7X_VMEM_REQUEST_CAP = 60000 * 1024
COMPILER_SCRATCH_BYTES = 12 * 1024 * 1024

GLA_CHUNK = 64
GLA_STEP_CHUNKS = 8
LOG2_E = 1.4426950408889634
RET_CHUNK = 128
RET_STEP_CHUNKS = 4
ROW_TILE = 1024
ROW_TILE_WIDE = 2048

BF16 = jnp.bfloat16
F32 = jnp.float32


def _params(n_axes, *block_bytes, scratch_bytes=0, claim_all=False):
    need = 2 * sum(block_bytes) + scratch_bytes + COMPILER_SCRATCH_BYTES
    if claim_all:
        need = V7X_VMEM_REQUEST_CAP
    return pltpu.CompilerParams(
        dimension_semantics=("arbitrary",) * n_axes,
        vmem_limit_bytes=int(min(V7X_VMEM_REQUEST_CAP, need)),
    )


def _nbytes(shape, dtype):
    return int(np.prod(shape)) * jnp.dtype(dtype).itemsize


def _sigmoid(x):
    return 1.0 / (1.0 + jnp.exp(-x))


def _silu(x):
    return x * _sigmoid(x)


def _log_sigmoid(x):
    return jnp.minimum(x, 0.0) - jnp.log(1.0 + jnp.exp(-jnp.abs(x)))


def _dot(a, b):
    return jnp.dot(a, b, preferred_element_type=F32)


def _dot_nt(a, b):
    return lax.dot_general(a, b, (((1,), (1,)), ((), ())), preferred_element_type=F32)


def _dot_tn(a, b):
    return lax.dot_general(a, b, (((0,), (0,)), ((), ())), preferred_element_type=F32)


def _row_tile(m, want):
    t = min(m, want)
    assert m % t == 0, (m, t)
    return t


def _rmsnorm_rows(x, w):
    ms = jnp.mean(x * x, axis=-1, keepdims=True)
    return x * lax.rsqrt(ms + EPS) * w


def _rope_rows(cos_ref, sin_ref, pos0):
    rows, half = cos_ref.shape
    pos = (lax.broadcasted_iota(jnp.int32, (rows, half), 0) + pos0).astype(F32)
    idx = lax.broadcasted_iota(jnp.int32, (rows, half), 1).astype(F32)
    ang = pos * jnp.exp(idx * (-math.log(ROPE_BASE) / half))
    cos_ref[...] = jnp.cos(ang)
    sin_ref[...] = jnp.sin(ang)


def _rmsnorm_kernel(xp_ref, xs_ref, w_ref, wgd_ref, wup_ref, bup_ref, w0_ref,
                    hp_ref, hs_ref, gp_ref, gs_ref, cosp_ref, sinp_ref, cosd_ref, sind_ref, p0p_ref, p0s_ref,
                    wgd_bf, wup_bf, w0_bf):
    i = pl.program_id(0)

    @pl.when(i == 0)
    def _():
        wgd_bf[...] = wgd_ref[...].astype(BF16)
        wup_bf[...] = jnp.zeros_like(wup_bf)
        wup_bf[0:wup_ref.shape[0], :] = wup_ref[...].astype(BF16)
        w0_bf[...] = w0_ref[...].astype(BF16)

    def rows(x, h_ref, g_ref, p0_ref, cos_ref, sin_ref, pos0):
        h = _rmsnorm_rows(x, w_ref[...]).astype(h_ref.dtype)
        h_ref[...] = h
        gd = _dot_nt(h, wgd_bf[...])
        x = _dot(gd.astype(BF16), wup_bf[...]) + bup_ref[...]
        g_ref[...] = _log_sigmoid(x) * (LOG2_E / GLA_GATE_NORM)
        _rope_rows(cos_ref, sin_ref, pos0)
        p0_ref[...] = _dot_nt(h, w0_bf[...]).astype(p0_ref.dtype)

    rows(xp_ref[...], hp_ref, gp_ref, p0p_ref, cosp_ref, sinp_ref, i * cosp_ref.shape[0])

    @pl.when(i == 0)
    def _():
        rows(xs_ref[...], hs_ref, gs_ref, p0s_ref, cosd_ref, sind_ref, PAST_LEN)


def _rmsnorm_gate(x_p, x_s, w, w_in_t, gate_row0, w_gate_up, bup, seq, half, tn0):
    m_p, d = x_p.shape
    tail = x_s.shape[0]
    rank, gw = w_gate_up.shape
    tm = _row_tile(m_p, ROW_TILE // 2)
    n_steps = m_p // tm
    assert gate_row0 % V7X_LANES == 0 and seq % n_steps == 0 and rank % 16 == 0
    pos_rows = seq // n_steps
    table = pl.BlockSpec((pos_rows, half), lambda i: (i, 0))
    table_dec = pl.BlockSpec((8, half), lambda i: (0, 0))
    return pl.pallas_call(
        _rmsnorm_kernel,
        grid=(n_steps,),
        in_specs=[
            pl.BlockSpec((tm, d), lambda i: (i, 0)),
            pl.BlockSpec((tail, None, d), lambda i: (0, 0, 0)),
            pl.BlockSpec((1, d), lambda i: (0, 0)),
            pl.BlockSpec((V7X_LANES, d), lambda i: (gate_row0 // V7X_LANES, 0)),
            pl.BlockSpec((rank, gw), lambda i: (0, 0)),
            pl.BlockSpec((1, gw), lambda i: (0, 0)),
            pl.BlockSpec((tn0, d), lambda i: (0, 0)),
        ],
        out_specs=[
            pl.BlockSpec((tm, d), lambda i: (i, 0)), pl.BlockSpec((tail, d), lambda i: (0, 0)),
            pl.BlockSpec((tm, gw), lambda i: (i, 0)), pl.BlockSpec((tail, gw), lambda i: (0, 0)),
            table, table, table_dec, table_dec,
            pl.BlockSpec((tm, tn0), lambda i: (i, 0)), pl.BlockSpec((tail, tn0), lambda i: (0, 0)),
        ],
        out_shape=[
            jax.ShapeDtypeStruct((m_p, d), BF16), jax.ShapeDtypeStruct((tail, d), BF16),
            jax.ShapeDtypeStruct((m_p, gw), F32), jax.ShapeDtypeStruct((tail, gw), F32),
            jax.ShapeDtypeStruct((seq, half), F32), jax.ShapeDtypeStruct((seq, half), F32),
            jax.ShapeDtypeStruct((8, half), F32), jax.ShapeDtypeStruct((8, half), F32),
            jax.ShapeDtypeStruct((m_p, tn0), BF16), jax.ShapeDtypeStruct((tail, tn0), BF16),
        ],
        scratch_shapes=[pltpu.VMEM((V7X_LANES, d), BF16), pltpu.VMEM((V7X_LANES, gw), BF16),
                        pltpu.VMEM((tn0, d), BF16)],
        compiler_params=_params(1, claim_all=True),
        name="rmsnorm_gate",
    )(x_p, x_s, w.reshape(1, d), w_in_t, w_gate_up, bup, w_in_t)


def _in_proj_kernel(hp_ref, hs_ref, wm_ref, wn_ref, op_ref, os_ref, wbf_ref, *, n_plain, shift, first_tile):
    j = pl.program_id(0) + first_tile
    i = pl.program_id(1)
    tn = wbf_ref.shape[0]

    @pl.when(jnp.logical_and(i == 0, j < n_plain))
    def _():
        wbf_ref[...] = wm_ref[...].astype(BF16)

    @pl.when(jnp.logical_and(i == 0, j >= n_plain))
    def _():
        wbf_ref[0:tn - shift, :] = wm_ref[shift:tn, :].astype(BF16)
        wbf_ref[tn - shift:tn, :] = wn_ref[...].astype(BF16)

    @pl.when(i == 0)
    def _():
        os_ref[...] = _dot_nt(hs_ref[...], wbf_ref[...]).astype(os_ref.dtype)

    op_ref[...] = _dot_nt(hp_ref[...], wbf_ref[...]).astype(op_ref.dtype)


def _in_proj(h_p, h_s, w_in_t, plain_cols, shift, out_cols, tn, first_tile):
    m_p, d = h_p.shape
    tail = h_s.shape[0]
    tm = _row_tile(m_p, ROW_TILE_WIDE)
    assert plain_cols % tn == 0 and out_cols % tn == 0 and tn % shift == 0 and shift % 8 == 0
    n_plain = plain_cols // tn
    n_tiles = out_cols // tn - first_tile
    kern = functools.partial(_in_proj_kernel, n_plain=n_plain, shift=shift, first_tile=first_tile)
    return pl.pallas_call(
        kern,
        grid=(n_tiles, m_p // tm),
        in_specs=[
            pl.BlockSpec((tm, d), lambda j, i: (i, 0)),
            pl.BlockSpec((tail, d), lambda j, i: (0, 0)),
            pl.BlockSpec((tn, d), lambda j, i: (j + first_tile, 0)),
            pl.BlockSpec((shift, d), lambda j, i: ((j + first_tile + 1) * (tn // shift), 0)),
        ],
        out_specs=[pl.BlockSpec((tm, tn), lambda j, i: (i, j)), pl.BlockSpec((tail, tn), lambda j, i: (0, j))],
        out_shape=[jax.ShapeDtypeStruct((m_p, n_tiles * tn), BF16), jax.ShapeDtypeStruct((tail, n_tiles * tn), BF16)],
        scratch_shapes=[pltpu.VMEM((tn, d), BF16)],
        compiler_params=_params(
            2, _nbytes((tm, d), BF16), _nbytes((tail, d), BF16), _nbytes((tn, d), F32), _nbytes((shift, d), F32),
            _nbytes((tm, tn), BF16), _nbytes((tail, tn), BF16),
            scratch_bytes=_nbytes((tn, d), BF16) + _nbytes((tm, tn), F32)),
        name="in_proj",
    )(h_p, h_s, w_in_t, w_in_t)


def _prefix_sum_rows(sel3_bf16, g):
    g0 = g.astype(BF16)
    r1 = g - g0.astype(F32)
    g1 = r1.astype(BF16)
    g2 = (r1 - g1.astype(F32)).astype(BF16)
    return _dot(sel3_bf16, jnp.concatenate([g0, g1, g2], axis=0))


def _lane_bcast_cols(row, n):
    parts = []
    for c in range(n // V7X_LANES):
        tile = jnp.broadcast_to(row[:, c * V7X_LANES:(c + 1) * V7X_LANES], (V7X_LANES, V7X_LANES))
        parts.append(tile.T)
    return parts[0] if len(parts) == 1 else jnp.concatenate(parts, axis=0)


def _rms_gate_store(o, w, gate, out_ref, rows, cols):
    ms = jnp.mean(o * o, axis=-1, keepdims=True)
    y = o * lax.rsqrt(ms + EPS) * w
    out_ref[rows, cols] = (y * _silu(gate)).astype(out_ref.dtype)


def _ln_gate_store(o, w, gate, out_ref, rows, cols):
    mu = jnp.mean(o, axis=-1, keepdims=True)
    dlt = o - mu
    var = jnp.mean(dlt * dlt, axis=-1, keepdims=True)
    y = dlt * lax.rsqrt(var + EPS) * w
    out_ref[rows, cols] = (y * _silu(gate)).astype(out_ref.dtype)


def _token_selectors(n_tok):
    assert 3 * n_tok <= V7X_LANES
    j = lax.broadcasted_iota(jnp.int32, (V7X_LANES, V7X_LANES), 0)
    sel = []
    for t in range(n_tok):
        hit = jnp.logical_or(j == t, jnp.logical_or(j == n_tok + t, j == 2 * n_tok + t))
        sel.append(jnp.where(hit, 1.0, 0.0).astype(BF16))
    return jnp.stack(sel, axis=0)


def _column_source(x):
    n_tok, w = x.shape
    hi = x.astype(BF16).astype(F32)
    r1 = x - hi
    mid = r1.astype(BF16).astype(F32)
    lo = (r1 - mid).astype(BF16).astype(F32)
    x3 = jnp.concatenate([hi, mid, lo, jnp.zeros((V7X_LANES - 3 * n_tok, w), F32)], axis=0)
    parts = [x3[:, c * V7X_LANES:(c + 1) * V7X_LANES].T for c in range(w // V7X_LANES)]
    return (parts[0] if len(parts) == 1 else jnp.concatenate(parts, axis=0)).astype(BF16)


def _decode_advance(tok0, decay_rows_fn, decay_const_fn, k_ref, q_ref, v_ref, s_in_ref, s_out_ref, o_ref, sel_ref,
                    *, heads, dk, dv):
    n_tok = s_in_ref.shape[0]
    reps = dv // V7X_LANES
    rows = pl.ds(pl.multiple_of(tok0, n_tok), n_tok)

    def cols(src, tt):
        return jnp.concatenate([_dot(src, sel_ref[tt])] * reps, axis=1)

    for hh in range(heads):
        kc = slice(hh * dk, (hh + 1) * dk)
        vc = slice(hh * dv, (hh + 1) * dv)
        k_src, q_src = _column_source(k_ref[rows, kc]), _column_source(q_ref[rows, kc])
        a_src = None if decay_rows_fn is None else _column_source(decay_rows_fn(rows, hh))
        v = v_ref[rows, vc]
        o_rows = []
        for tt in range(n_tok):
            decay = decay_const_fn(hh) if a_src is None else cols(a_src, tt)
            s_new = decay * s_in_ref[tt, hh] + cols(k_src, tt) * v[tt:tt + 1, :]
            s_out_ref[tt, hh] = s_new
            o_rows.append(jnp.sum(cols(q_src, tt) * s_new, axis=0, keepdims=True))
        o_ref[rows, vc] = jnp.concatenate(o_rows, axis=0)


def _decode_plan(n_dec, n_steps):
    assert n_dec % n_steps == 0 and (n_dec // n_steps) % 8 == 0, (n_dec, n_steps)
    return n_dec // n_steps


def _gla_decode_section(step, last_step, refs, scratch, *, heads, dk, dv):
    qd_ref, kd_ref, vd_ref, gad_ref, gd_ref, nw_ref, sd_in_ref, od_ref, sd_out_ref = refs
    a_dec, q_dec, k_dec, v_dec, o_dec, sel_ref = scratch

    @pl.when(step == 0)
    def _():
        a_dec[...] = jnp.exp2(gd_ref[...])
        q_dec[...] = qd_ref[...].astype(F32) * (dk ** -0.5)
        k_dec[...] = kd_ref[...].astype(F32)
        v_dec[...] = vd_ref[...].astype(F32)
        sel_ref[...] = _token_selectors(sd_in_ref.shape[0])

    _decode_advance(step * sd_in_ref.shape[0], lambda rows, hh: a_dec[rows, hh * dk:(hh + 1) * dk], None,
                    k_dec, q_dec, v_dec, sd_in_ref, sd_out_ref, o_dec, sel_ref, heads=heads, dk=dk, dv=dv)

    @pl.when(step == last_step)
    def _():
        n_dec = o_dec.shape[0]
        for hh in range(heads):
            vc = slice(hh * dv, (hh + 1) * dv)
            _rms_gate_store(o_dec[:, vc], nw_ref[...], gad_ref[:, vc].astype(F32), od_ref, slice(0, n_dec), vc)


def _ret_decode_section(step, last_step, refs, scratch, *, heads, dk, dv):
    qd_ref, kd_ref, vd_ref, gbd_ref, cosd_ref, sind_ref, lg_ref, nw_ref, sd_in_ref, od_ref, sd_out_ref = refs
    q_dec, k_dec, v_dec, o_dec, sel_ref = scratch

    @pl.when(step == 0)
    def _():
        cosd, sind = cosd_ref[0:1, :], sind_ref[0:1, :]
        for hh in range(heads):
            kc = slice(hh * dk, (hh + 1) * dk)
            q_dec[:, kc] = _rotary(qd_ref[:, kc].astype(F32), cosd, sind)
            k_dec[:, kc] = _rotary(kd_ref[:, kc].astype(F32), cosd, sind) * (dk ** -0.5)
        v_dec[...] = vd_ref[...].astype(F32)
        sel_ref[...] = _token_selectors(sd_in_ref.shape[0])

    def gamma(hh):
        return jnp.exp(jnp.concatenate([lg_ref[hh]] * (dv // V7X_LANES), axis=1))

    _decode_advance(step * sd_in_ref.shape[0], None, gamma, k_dec, q_dec, v_dec, sd_in_ref, sd_out_ref, o_dec,
                    sel_ref, heads=heads, dk=dk, dv=dv)

    @pl.when(step == last_step)
    def _():
        n_dec = o_dec.shape[0]
        for hh in range(heads):
            vc = slice(hh * dv, (hh + 1) * dv)
            _ln_gate_store(o_dec[:, vc], nw_ref[...], gbd_ref[:, vc].astype(F32), od_ref, slice(0, n_dec), vc)


def _segment_block(lay, name, width):
    offset = lay["cols"][name][1]
    assert offset % width == 0
    return offset // width


def _segment_arrays(lay, arrays, names):
    return tuple(arrays[lay["cols"][n][0]] for n in names)


def _decode_specs(kind, lay, n_dec, tps, row):
    heads, dk, dv = lay[kind + "_heads"], lay[kind + "_dk"], lay[kind + "_dv"]
    qk, vw = heads * dk, heads * dv
    names = ("qa", "ka", "va", "ga") if kind == "gla" else ("qb", "kb", "vb", "gb")
    widths = (qk, qk, vw, vw)
    rows_in = [pl.BlockSpec((n_dec, w), lambda b, t, c=_segment_block(lay, n, w): (0, c))
               for n, w in zip(names, widths)]
    state = pl.BlockSpec((tps, heads, dk, dv), lambda b, t: (row(b, t), 0, 0, 0))
    o_spec = pl.BlockSpec((n_dec, vw), lambda b, t: (0, 0))
    n_qk = 3 if kind == "gla" else 2
    scratch = ([pltpu.VMEM((n_dec, qk), F32)] * n_qk + [pltpu.VMEM((n_dec, vw), F32)] * 2
               + [pltpu.VMEM((tps, V7X_LANES, V7X_LANES), BF16)])
    return rows_in, state, o_spec, scratch


def _gla_sum_matrices(c):
    levels = c.bit_length() - 1
    assert 1 << levels == c
    i = lax.broadcasted_iota(jnp.int32, (c, c), 0)
    j = lax.broadcasted_iota(jnp.int32, (c, c), 1)
    mats = [j <= i]
    for l in range(levels):
        ref = jnp.bitwise_or(jnp.bitwise_and(i, -(2 << l)), 1 << l)
        mats.append(jnp.logical_and(j > jnp.minimum(i, ref), j <= jnp.maximum(i, ref)))
    mats.append(j > i)
    sel = jnp.concatenate([jnp.where(m, 1.0, 0.0).astype(BF16) for m in mats], axis=0)
    return jnp.concatenate([sel, sel, sel], axis=1)


def _pair_level(c):
    levels = c.bit_length() - 1
    i = lax.broadcasted_iota(jnp.int32, (c, c), 0)
    j = lax.broadcasted_iota(jnp.int32, (c, c), 1)
    x = jnp.bitwise_xor(i, j)
    lvl = jnp.zeros((c, c), jnp.int32)
    for l in range(1, levels):
        lvl = lvl + jnp.where(x >= (1 << l), 1, 0)
    return jnp.where(i > j, lvl, jnp.where(i == j, levels, -1))


def _queries_else_keys(q, k, l):
    c = q.shape[0]
    span = 1 << l
    if span >= 8:
        parts = [(q if (b & 1) else k)[b * span:(b + 1) * span, :] for b in range(c // span)]
        return jnp.concatenate(parts, axis=0)
    row = lax.broadcasted_iota(jnp.int32, q.shape, 0)
    return jnp.where(jnp.bitwise_and(row, span) != 0, q, k)


def _gla_level_scores(q, k, sums):
    c = q.shape[0]
    levels = c.bit_length() - 1
    out = []
    for l in range(levels):
        x = _queries_else_keys(q, k, l) * jnp.exp2(sums[(1 + l) * c:(2 + l) * c, :])
        xb = x.astype(BF16)
        out.append(_dot_nt(xb, xb))
    return out


def _gla_chunk_out(q, k, v, sums, level_scores, pair_level, state):
    c = q.shape[0]
    levels = c.bit_length() - 1
    scores = jnp.where(pair_level == levels, jnp.sum(q * k, axis=-1, keepdims=True), 0.0)
    for l in range(levels):
        scores = jnp.where(pair_level == l, level_scores[l], scores)
    o = _dot((q * jnp.exp2(sums[0:c, :])).astype(BF16), state.astype(BF16))
    return o + _dot(scores.astype(BF16), v)


def _gla_next_state(k, v, sums, state):
    c, dk = k.shape
    levels = c.bit_length() - 1
    k_tail = (k * jnp.exp2(sums[(levels + 1) * c:(levels + 2) * c, :])).astype(BF16)
    decay = _lane_bcast_cols(jnp.exp2(sums[c - 1:c, :]), dk)
    decay_full = jnp.concatenate([decay] * (v.shape[1] // V7X_LANES), axis=1)
    return decay_full * state + _dot_tn(k_tail, v)


def _gla_kernel(*refs, heads, dk, dv, dec_dims):
    q_ref, k_ref, v_ref, ga_ref, g_ref, nw_ref = refs[:6]
    dec_in = refs[6:15]
    o_ref, s_out_ref = refs[15:17]
    dec_out = refs[17:19]
    s_ref, mats_ref, lvl_ref = refs[19:22]
    dec_scratch = refs[22:]
    t = pl.program_id(1)
    step = pl.program_id(0) * pl.num_programs(1) + t
    last_step = pl.num_programs(0) * pl.num_programs(1) - 1

    @pl.when(t == 0)
    def _():
        s_ref[...] = jnp.zeros_like(s_ref)
        mats_ref[...] = _gla_sum_matrices(GLA_CHUNK)
        lvl_ref[...] = _pair_level(GLA_CHUNK)

    _ret_decode_section(step, last_step, dec_in + dec_out, dec_scratch,
                        heads=dec_dims[0], dk=dec_dims[1], dv=dec_dims[2])

    ct = q_ref.shape[0]

    kcs = [slice(hh * dk, (hh + 1) * dk) for hh in range(heads)]
    vcs = [slice(hh * dv, (hh + 1) * dv) for hh in range(heads)]
    group = 2 if (ct // GLA_CHUNK) % 2 == 0 else 1

    def chunk_group(cg, carry):
        pair_level = lvl_ref[...]
        rows, sums, qs, ks, lvl_scores = [], [], [], [], []
        for u in range(group):
            r = pl.ds(pl.multiple_of((cg * group + u) * GLA_CHUNK, GLA_CHUNK), GLA_CHUNK)
            rows.append(r)
            sums.append(_prefix_sum_rows(mats_ref[...], g_ref[r, :]))
            qs.append([q_ref[r, kc].astype(F32) * (dk ** -0.5) for kc in kcs])
            ks.append([k_ref[r, kc].astype(F32) for kc in kcs])
            lvl_scores.append([_gla_level_scores(qs[u][hh], ks[u][hh], sums[u][:, kcs[hh]]) for hh in range(heads)])
        outs = []
        for u in range(group):
            r = rows[u]
            outs.append([_gla_chunk_out(qs[u][hh], ks[u][hh], v_ref[r, vcs[hh]], sums[u][:, kcs[hh]],
                                        lvl_scores[u][hh], pair_level, s_ref[hh]) for hh in range(heads)])
            for hh in range(heads):
                s_ref[hh] = _gla_next_state(ks[u][hh], v_ref[r, vcs[hh]], sums[u][:, kcs[hh]], s_ref[hh])
        for u in range(group):
            for hh in range(heads):
                _rms_gate_store(outs[u][hh], nw_ref[...], ga_ref[rows[u], vcs[hh]].astype(F32), o_ref, rows[u],
                                vcs[hh])
        return carry

    lax.fori_loop(0, ct // GLA_CHUNK // group, chunk_group, 0)

    @pl.when(t == pl.num_programs(1) - 1)
    def _():
        s_out_ref[0] = s_ref[...]


def _gla_prompt_ret_decode(projs, log2_decay, gla_norm_w, projs_dec, cos_dec, sin_dec, log_gamma, ret_norm_w,
                           ret_state_dec, lay, batch, seq):
    heads, dk, dv = lay["gla_heads"], lay["gla_dk"], lay["gla_dv"]
    qk, vw = heads * dk, heads * dv
    r_heads, r_dk, r_dv = lay["ret_heads"], lay["ret_dk"], lay["ret_dv"]
    ct = min(seq, GLA_STEP_CHUNKS * GLA_CHUNK)
    levels = GLA_CHUNK.bit_length() - 1
    assert seq % ct == 0 and ct % GLA_CHUNK == 0
    nt = seq // ct
    n_dec = projs_dec[0].shape[0]
    tps = _decode_plan(n_dec, batch * nt)
    row = lambda b, t: b * nt + t
    dec_rows, dec_state, dec_o, dec_scratch = _decode_specs("ret", lay, n_dec, tps, row)
    table_dec = pl.BlockSpec((cos_dec.shape[0], r_dk // 2), lambda b, t: (0, 0))
    kern = functools.partial(_gla_kernel, heads=heads, dk=dk, dv=dv, dec_dims=(r_heads, r_dk, r_dv))
    return pl.pallas_call(
        kern,
        grid=(batch, nt),
        in_specs=[
            pl.BlockSpec((ct, qk), lambda b, t: (row(b, t), _segment_block(lay, "qa", qk))),
            pl.BlockSpec((ct, qk), lambda b, t: (row(b, t), _segment_block(lay, "ka", qk))),
            pl.BlockSpec((ct, vw), lambda b, t: (row(b, t), _segment_block(lay, "va", vw))),
            pl.BlockSpec((ct, vw), lambda b, t: (row(b, t), _segment_block(lay, "ga", vw))),
            pl.BlockSpec((ct, qk), lambda b, t: (row(b, t), 0)),
            pl.BlockSpec((1, dv), lambda b, t: (0, 0)),
        ] + dec_rows + [
            table_dec, table_dec,
            pl.BlockSpec((r_heads, 1, V7X_LANES), lambda b, t: (0, 0, 0)),
            pl.BlockSpec((1, r_dv), lambda b, t: (0, 0)),
            dec_state,
        ],
        out_specs=[
            pl.BlockSpec((ct, vw), lambda b, t: (row(b, t), 0)),
            pl.BlockSpec((1, heads, dk, dv), lambda b, t: (b, 0, 0, 0)),
            dec_o,
            dec_state,
        ],
        out_shape=[
            jax.ShapeDtypeStruct((batch * seq, vw), BF16),
            jax.ShapeDtypeStruct((batch, heads, dk, dv), F32),
            jax.ShapeDtypeStruct((n_dec, r_heads * r_dv), BF16),
            jax.ShapeDtypeStruct(ret_state_dec.shape, ret_state_dec.dtype),
        ],
        scratch_shapes=[
            pltpu.VMEM((heads, dk, dv), F32),
            pltpu.VMEM(((levels + 2) * GLA_CHUNK, 3 * GLA_CHUNK), BF16),
            pltpu.VMEM((GLA_CHUNK, GLA_CHUNK), jnp.int32),
        ] + dec_scratch,
        compiler_params=_params(2, claim_all=True),
        name="gla_prompt_ret_decode",
    )(*_segment_arrays(lay, projs, ("qa", "ka", "va", "ga")), log2_decay, gla_norm_w,
      *_segment_arrays(lay, projs_dec, ("qb", "kb", "vb", "gb")), cos_dec, sin_dec,
      log_gamma, ret_norm_w, ret_state_dec)


def _rotary(x, cos, sin):
    half = x.shape[1] // 2
    x1, x2 = x[:, :half], x[:, half:]
    return jnp.concatenate([x1 * cos - x2 * sin, x1 * sin + x2 * cos], axis=1)


def _ret_kernel(*refs, heads, dk, dv, c, dec_dims):
    q_ref, k_ref, v_ref, gb_ref, cos_ref, sin_ref, lg_ref, nw_ref = refs[:8]
    dec_in = refs[8:15]
    o_ref, s_out_ref = refs[15:17]
    dec_out = refs[17:19]
    s_ref, dmat_ref, qdec_ref, kdec_ref = refs[19:23]
    dec_scratch = refs[23:]
    t = pl.program_id(1)
    step = pl.program_id(0) * pl.num_programs(1) + t
    last_step = pl.num_programs(0) * pl.num_programs(1) - 1

    _gla_decode_section(step, last_step, dec_in + dec_out, dec_scratch,
                        heads=dec_dims[0], dk=dec_dims[1], dv=dec_dims[2])

    @pl.when(t == 0)
    def _():
        s_ref[...] = jnp.zeros_like(s_ref)
        ri = lax.broadcasted_iota(jnp.int32, (c, c), 0)
        rj = lax.broadcasted_iota(jnp.int32, (c, c), 1)
        dist = (ri - rj).astype(F32)
        rowl = lax.broadcasted_iota(jnp.int32, (c, V7X_LANES), 0).astype(F32)
        for hh in range(heads):
            lg = lg_ref[hh]
            dmat_ref[hh] = jnp.exp(jnp.where(ri >= rj, dist * lg[:, :1], -jnp.inf))
            qdec_ref[hh] = jnp.exp((rowl + 1.0) * lg)
            kdec_ref[hh] = jnp.exp((float(c - 1) - rowl) * lg)

    ct = q_ref.shape[0]
    kcs = [slice(hh * dk, (hh + 1) * dk) for hh in range(heads)]
    vcs = [slice(hh * dv, (hh + 1) * dv) for hh in range(heads)]

    def chunk(ci, carry):
        rows = pl.ds(pl.multiple_of(ci * c, c), c)
        cos, sin = cos_ref[rows, :], sin_ref[rows, :]
        qrs = [_rotary(q_ref[rows, kc].astype(F32), cos, sin).astype(BF16) for kc in kcs]
        krs = [_rotary(k_ref[rows, kc].astype(F32), cos, sin) * (dk ** -0.5) for kc in kcs]
        scores = [_dot_nt(qrs[hh], krs[hh].astype(BF16)) * dmat_ref[hh] for hh in range(heads)]
        outs = []
        for hh in range(heads):
            qdec = jnp.concatenate([qdec_ref[hh]] * (dv // V7X_LANES), axis=1)
            o = qdec * _dot(qrs[hh], s_ref[hh].astype(BF16))
            outs.append(o + _dot(scores[hh].astype(BF16), v_ref[rows, vcs[hh]]))
        for hh in range(heads):
            kdec = jnp.concatenate([kdec_ref[hh]] * (dk // V7X_LANES), axis=1)
            k_tail = (krs[hh] * kdec).astype(BF16)
            lgv = jnp.concatenate([lg_ref[hh]] * (dv // V7X_LANES), axis=1)
            s_ref[hh] = jnp.exp(float(c) * lgv) * s_ref[hh] + _dot_tn(k_tail, v_ref[rows, vcs[hh]])
        for hh in range(heads):
            _ln_gate_store(outs[hh], nw_ref[...], gb_ref[rows, vcs[hh]].astype(F32), o_ref, rows, vcs[hh])
        return carry

    lax.fori_loop(0, ct // c, chunk, 0)

    @pl.when(t == pl.num_programs(1) - 1)
    def _():
        s_out_ref[0] = s_ref[...]


def _ret_prompt_gla_decode(projs, cos, sin, log_gamma, ret_norm_w, projs_dec, log2_decay_dec, gla_norm_w,
                           gla_state_dec, lay, batch, seq):
    heads, dk, dv = lay["ret_heads"], lay["ret_dk"], lay["ret_dv"]
    qk, vw = heads * dk, heads * dv
    g_heads, g_dk, g_dv = lay["gla_heads"], lay["gla_dk"], lay["gla_dv"]
    c = min(seq, RET_CHUNK)
    ct = min(seq, RET_STEP_CHUNKS * c)
    assert seq % ct == 0 and ct % c == 0
    nt = seq // ct
    half = dk // 2
    n_dec = projs_dec[0].shape[0]
    tps = _decode_plan(n_dec, batch * nt)
    row = lambda b, t: b * nt + t
    dec_rows, dec_state, dec_o, dec_scratch = _decode_specs("gla", lay, n_dec, tps, row)
    kern = functools.partial(_ret_kernel, heads=heads, dk=dk, dv=dv, c=c, dec_dims=(g_heads, g_dk, g_dv))
    return pl.pallas_call(
        kern,
        grid=(batch, nt),
        in_specs=[
            pl.BlockSpec((ct, qk), lambda b, t: (row(b, t), _segment_block(lay, "qb", qk))),
            pl.BlockSpec((ct, qk), lambda b, t: (row(b, t), _segment_block(lay, "kb", qk))),
            pl.BlockSpec((ct, vw), lambda b, t: (row(b, t), _segment_block(lay, "vb", vw))),
            pl.BlockSpec((ct, vw), lambda b, t: (row(b, t), _segment_block(lay, "gb", vw))),
            pl.BlockSpec((ct, half), lambda b, t: (t, 0)),
            pl.BlockSpec((ct, half), lambda b, t: (t, 0)),
            pl.BlockSpec((heads, 1, V7X_LANES), lambda b, t: (0, 0, 0)),
            pl.BlockSpec((1, dv), lambda b, t: (0, 0)),
        ] + dec_rows + [
            pl.BlockSpec((n_dec, g_heads * g_dk), lambda b, t: (0, 0)),
            pl.BlockSpec((1, g_dv), lambda b, t: (0, 0)),
            dec_state,
        ],
        out_specs=[
            pl.BlockSpec((ct, vw), lambda b, t: (row(b, t), 0)),
            pl.BlockSpec((1, heads, dk, dv), lambda b, t: (b, 0, 0, 0)),
            dec_o,
            dec_state,
        ],
        out_shape=[
            jax.ShapeDtypeStruct((batch * seq, vw), BF16),
            jax.ShapeDtypeStruct((batch, heads, dk, dv), F32),
            jax.ShapeDtypeStruct((n_dec, g_heads * g_dv), BF16),
            jax.ShapeDtypeStruct(gla_state_dec.shape, gla_state_dec.dtype),
        ],
        scratch_shapes=[
            pltpu.VMEM((heads, dk, dv), F32),
            pltpu.VMEM((heads, c, c), F32),
            pltpu.VMEM((heads, c, V7X_LANES), F32),
            pltpu.VMEM((heads, c, V7X_LANES), F32),
        ] + dec_scratch,
        compiler_params=_params(2, claim_all=True),
        name="ret_prompt_gla_decode",
    )(*_segment_arrays(lay, projs, ("qb", "kb", "vb", "gb")), cos, sin, log_gamma, ret_norm_w,
      *_segment_arrays(lay, projs_dec, ("qa", "ka", "va", "ga")), log2_decay_dec, gla_norm_w, gla_state_dec)


def _merge_kernel(oap_ref, obp_ref, oas_ref, obs_ref, wa_ref, wb_ref, g0p_ref, g1p_ref, g0s_ref, g1s_ref, wnext_ref,
                  mp_ref, ms_ref, wnext_bf_ref, wa_bf, wb_bf):
    wnext_bf_ref[...] = wnext_ref[...].astype(BF16)

    def merged(oa, ob, g0, g1):
        ya = _dot(oa, wa_bf[...])
        yb = _dot(ob, wb_bf[...])
        return _sigmoid(g0.astype(F32)) * ya + _sigmoid(g1.astype(F32)) * yb

    @pl.when(pl.program_id(1) == 0)
    def _():
        wa_bf[...] = wa_ref[...].astype(BF16)
        wb_bf[...] = wb_ref[...].astype(BF16)
        ms_ref[...] = merged(oas_ref[...], obs_ref[...], g0s_ref[...], g1s_ref[...]).astype(ms_ref.dtype)

    mp_ref[...] = merged(oap_ref[...], obp_ref[...], g0p_ref[...], g1p_ref[...]).astype(mp_ref.dtype)


def _slab_specs(w_next, n_steps, step_of):
    kn, dn = w_next.shape
    assert kn % n_steps == 0 and (kn // n_steps) % 16 == 0, (kn, n_steps)
    slab = kn // n_steps
    spec = pl.BlockSpec((slab, dn), lambda j, i: (step_of(j, i), 0))
    return spec, spec, jax.ShapeDtypeStruct((kn, dn), BF16), _nbytes((slab, dn), F32) + _nbytes((slab, dn), BF16)


def _merge(oa_p, ob_p, oa_s, ob_s, wa, wb, projs_p, projs_s, lay, w_next):
    proj_p, proj_s = projs_p[lay["cols"]["mg"][0]], projs_s[lay["cols"]["mg"][0]]
    m_p, ka = oa_p.shape
    kb = ob_p.shape[1]
    tail = oa_s.shape[0]
    d = wa.shape[1]
    tm = _row_tile(m_p, ROW_TILE)
    tn = min(d, 1024)
    mg = lay["cols"]["mg"][1]
    assert d % tn == 0 and mg % tn == 0
    g0 = mg // tn
    g1 = (mg + d) // tn
    n_m = m_p // tm
    slab_in, slab_out, slab_shape, slab_bytes = _slab_specs(w_next, (d // tn) * n_m, lambda j, i: j * n_m + i)
    return pl.pallas_call(
        _merge_kernel,
        grid=(d // tn, n_m),
        in_specs=[
            pl.BlockSpec((tm, ka), lambda j, i: (i, 0)),
            pl.BlockSpec((tm, kb), lambda j, i: (i, 0)),
            pl.BlockSpec((tail, ka), lambda j, i: (0, 0)),
            pl.BlockSpec((tail, kb), lambda j, i: (0, 0)),
            pl.BlockSpec((ka, tn), lambda j, i: (0, j)),
            pl.BlockSpec((kb, tn), lambda j, i: (0, j)),
            pl.BlockSpec((tm, tn), lambda j, i: (i, g0 + j)),
            pl.BlockSpec((tm, tn), lambda j, i: (i, g1 + j)),
            pl.BlockSpec((tail, tn), lambda j, i: (0, g0 + j)),
            pl.BlockSpec((tail, tn), lambda j, i: (0, g1 + j)),
            slab_in,
        ],
        out_specs=[pl.BlockSpec((tm, tn), lambda j, i: (i, j)), pl.BlockSpec((tail, tn), lambda j, i: (0, j)),
                   slab_out],
        out_shape=[jax.ShapeDtypeStruct((m_p, d), BF16), jax.ShapeDtypeStruct((tail, d), BF16), slab_shape],
        scratch_shapes=[pltpu.VMEM((ka, tn), BF16), pltpu.VMEM((kb, tn), BF16)],
        compiler_params=_params(
            2, _nbytes((tm, ka), BF16), _nbytes((tm, kb), BF16), _nbytes((tail, ka), BF16), _nbytes((tail, kb), BF16),
            _nbytes((ka, tn), F32), _nbytes((kb, tn), F32), 3 * _nbytes((tm, tn), BF16), 3 * _nbytes((tail, tn), BF16),
            slab_bytes,
            scratch_bytes=_nbytes((ka, tn), BF16) + _nbytes((kb, tn), BF16) + 3 * _nbytes((tm, tn), F32)),
        name="merge",
    )(oa_p, ob_p, oa_s, ob_s, wa, wb, proj_p, proj_p, proj_s, proj_s, w_next)


def _proj_res_norm_kernel(ap_ref, as_ref, w_ref, resp_ref, ress_ref, nw_ref, *out_refs, emit_sum):
    n_out = 2 if emit_sum else 1
    outs_p, outs_s = out_refs[:n_out], out_refs[n_out:]
    d = w_ref.shape[1]
    col_chunk = min(d, 512)

    def rows(a_ref, res_ref, outs):
        x_ref = outs[0]
        nrow = x_ref.shape[0]
        row_chunk = min(nrow, 128)
        assert nrow % row_chunk == 0
        a = a_ref[...]
        for c in range(d // col_chunk):
            cs = slice(c * col_chunk, (c + 1) * col_chunk)
            x_ref[:, cs] = res_ref[:, cs] + _dot(a, w_ref[:, cs])

        def body(c, carry):
            rr = pl.ds(pl.multiple_of(c * row_chunk, row_chunk), row_chunk)
            y = _rmsnorm_rows(x_ref[rr, :], nw_ref[...])
            if emit_sum:
                outs[1][rr, :] = y.astype(outs[1].dtype)
            else:
                x_ref[rr, :] = y
            return carry

        lax.fori_loop(0, nrow // row_chunk, body, 0)

    rows(ap_ref, resp_ref, outs_p)

    @pl.when(pl.program_id(0) == pl.num_programs(0) - 1)
    def _():
        rows(as_ref, ress_ref, outs_s)


def _proj_res_norm(a_p, a_s, w, res_p, res_s, norm_w, emit_sum):
    m_p, kdim = a_p.shape
    tail = a_s.shape[0]
    d = w.shape[1]
    tm = _row_tile(m_p, ROW_TILE // 2)
    p_spec = pl.BlockSpec((tm, d), lambda i: (i, 0))
    s_spec = pl.BlockSpec((tail, d), lambda i: (0, 0))
    s3_spec = pl.BlockSpec((tail, None, d), lambda i: (0, 0, 0))
    res_s_spec = s3_spec if res_s.ndim == 3 else s_spec
    out_specs = [p_spec, s3_spec]
    out_shape = [jax.ShapeDtypeStruct((m_p, d), F32), jax.ShapeDtypeStruct((tail, 1, d), F32)]
    assert w.dtype == BF16
    blocks = [_nbytes((tm, kdim), BF16), _nbytes((tail, kdim), BF16),
              2 * _nbytes((tm, d), F32), 2 * _nbytes((tail, d), F32)]
    if emit_sum:
        out_specs = [p_spec, p_spec, s_spec, s_spec]
        out_shape = [out_shape[0], jax.ShapeDtypeStruct((m_p, d), BF16),
                     jax.ShapeDtypeStruct((tail, d), F32), jax.ShapeDtypeStruct((tail, d), BF16)]
        blocks += [_nbytes((tm, d), BF16), _nbytes((tail, d), BF16)]
    in_specs = [
        pl.BlockSpec((tm, kdim), lambda i: (i, 0)),
        pl.BlockSpec((tail, kdim), lambda i: (0, 0)),
        pl.BlockSpec((kdim, d), lambda i: (0, 0), pipeline_mode=pl.Buffered(1)),
        p_spec, res_s_spec, pl.BlockSpec((1, d), lambda i: (0, 0))]
    return pl.pallas_call(
        functools.partial(_proj_res_norm_kernel, emit_sum=emit_sum),
        grid=(m_p // tm,),
        in_specs=in_specs,
        out_specs=out_specs,
        out_shape=out_shape,
        compiler_params=_params(1, *blocks, scratch_bytes=_nbytes((kdim, d), BF16)),
        name="proj_res_norm",
    )(a_p, a_s, w, res_p, res_s, norm_w.reshape(1, d))


def _swiglu_kernel(hp_ref, hs_ref, wg_ref, wu_ref, wnext_ref, op_ref, os_ref, wnext_bf_ref, wg_bf, wu_bf):
    wnext_bf_ref[...] = wnext_ref[...].astype(BF16)

    tn = wg_bf.shape[1]
    col_chunk = min(tn, 256)

    def act(h_ref, o_ref):
        h = h_ref[...]
        for c in range(tn // col_chunk):
            cs = slice(c * col_chunk, (c + 1) * col_chunk)
            a = _dot(h, wg_bf[:, cs])
            b = _dot(h, wu_bf[:, cs])
            o_ref[:, cs] = (_silu(a) * b).astype(o_ref.dtype)

    @pl.when(pl.program_id(1) == 0)
    def _():
        wg_bf[...] = wg_ref[...].astype(BF16)
        wu_bf[...] = wu_ref[...].astype(BF16)
        act(hs_ref, os_ref)

    act(hp_ref, op_ref)


def _swiglu(h_p, h_s, wg, wu, w_next):
    m_p, d = h_p.shape
    tail = h_s.shape[0]
    f = wg.shape[1]
    tm = _row_tile(m_p, ROW_TILE_WIDE)
    tn = 512 if f % 512 == 0 else 256
    assert f % tn == 0
    n_m = m_p // tm
    slab_in, slab_out, slab_shape, slab_bytes = _slab_specs(w_next, (f // tn) * n_m, lambda j, i: j * n_m + i)
    return pl.pallas_call(
        _swiglu_kernel,
        grid=(f // tn, n_m),
        in_specs=[
            pl.BlockSpec((tm, d), lambda j, i: (i, 0)),
            pl.BlockSpec((tail, d), lambda j, i: (0, 0)),
            pl.BlockSpec((d, tn), lambda j, i: (0, j)),
            pl.BlockSpec((d, tn), lambda j, i: (0, j)),
            slab_in,
        ],
        out_specs=[pl.BlockSpec((tm, tn), lambda j, i: (i, j)), pl.BlockSpec((tail, tn), lambda j, i: (0, j)),
                   slab_out],
        out_shape=[jax.ShapeDtypeStruct((m_p, f), BF16), jax.ShapeDtypeStruct((tail, f), BF16), slab_shape],
        scratch_shapes=[pltpu.VMEM((d, tn), BF16), pltpu.VMEM((d, tn), BF16)],
        compiler_params=_params(
            2, _nbytes((tm, d), BF16), _nbytes((tail, d), BF16), 2 * _nbytes((d, tn), F32),
            _nbytes((tm, tn), BF16), _nbytes((tail, tn), BF16), slab_bytes,
            scratch_bytes=2 * _nbytes((d, tn), BF16) + 3 * _nbytes((tm, tn), F32)),
        name="swiglu",
    )(h_p, h_s, wg, wu, w_next)


def _layout(d_model, in_width, state_gla, state_ret, gate_rank):
    _, _, gh, gdk, gdv = state_gla.shape
    _, _, rh, rdk, rdv = state_ret.shape
    gqk, gv, rqk, rv = gh * gdk, gh * gdv, rh * rdk, rh * rdv
    lay = dict(gla_heads=gh, gla_dk=gdk, gla_dv=gdv, ret_heads=rh, ret_dk=rdk, ret_dv=rdv, rank=gate_rank)
    off = 0
    for name, width in (("qa", gqk), ("ka", gqk), ("va", gv), ("ga", gv), ("qb", rqk), ("kb", rqk),
                        ("vb", rv), ("gb", rv), ("mg", 2 * d_model)):
        lay[name] = off
        off += width
    lay["out_cols"] = off
    lay["plain_cols"] = 2 * gqk + gv
    lay["gd_src"] = lay["plain_cols"]
    assert lay["gd_src"] % V7X_LANES == 0 and gate_rank <= V7X_LANES
    assert in_width == off + gate_rank
    lay["widths"] = dict(qa=gqk, ka=gqk, va=gv, ga=gv, qb=rqk, kb=rqk, vb=rv, gb=rv, mg=2 * d_model)
    return lay


def _split_columns(lay, first_cols):
    cols = {}
    for name, width in lay["widths"].items():
        off = lay[name]
        assert off + width <= first_cols or off >= first_cols, "a segment straddles the two arrays"
        cols[name] = (0, off) if off < first_cols else (1, off - first_cols)
    return cols


def _layer(x_p, x_s, st_gla, st_ret, wts, lay, log_gamma, final_norm):
    (norm_mix, w_in, w_gate_up, b_gate, gla_norm_w, w_gla_up, ret_norm_w, w_ret_up, w_out, norm_ffn,
     w_ffn_gate, w_ffn_up, w_ffn_down) = wts
    batch, seq, d = x_p.shape
    rank = lay["rank"]
    gqk = lay["gla_heads"] * lay["gla_dk"]
    bup = b_gate.reshape(1, gqk)
    gnw = gla_norm_w.reshape(1, -1)
    rnw = ret_norm_w.reshape(1, -1)
    tn = 1024 if (lay["out_cols"] % 1024 == 0 and lay["plain_cols"] % 1024 == 0) else 512
    xp = x_p.reshape(batch * seq, d)
    assert x_s.ndim == 3 and x_s.shape[1] == 1, "one new token per decode sequence"
    xs = x_s
    w_in_t = w_in.T

    h_p, h_s, g_p, g_s, cos_p, sin_p, cos_s, sin_s, first_p, first_s = _rmsnorm_gate(
        xp, xs, norm_mix, w_in_t, lay["gd_src"], w_gate_up, bup, seq, lay["ret_dk"] // 2, tn)
    rest_p, rest_s = _in_proj(h_p, h_s, w_in_t, lay["plain_cols"], rank, lay["out_cols"], tn, 1)
    projs_p, projs_s = (first_p, rest_p), (first_s, rest_s)
    lay = dict(lay, cols=_split_columns(lay, tn))
    oa_p, sa_p, ob_s, sb_s = _gla_prompt_ret_decode(
        projs_p, g_p, gnw, projs_s, cos_s, sin_s, log_gamma, rnw, st_ret, lay, batch, seq)
    ob_p, sb_p, oa_s, sa_s = _ret_prompt_gla_decode(
        projs_p, cos_p, sin_p, log_gamma, rnw, projs_s, g_s, gnw, st_gla, lay, batch, seq)
    m_p, m_s, w_out_bf = _merge(oa_p, ob_p, oa_s, ob_s, w_gla_up, w_ret_up, projs_p, projs_s, lay, w_out)
    x1_p, h2_p, x1_s, h2_s = _proj_res_norm(m_p, m_s, w_out_bf, xp, xs, norm_ffn, True)
    act_p, act_s, w_down_bf = _swiglu(h2_p, h2_s, w_ffn_gate, w_ffn_up, w_ffn_down)
    y_p, y_s = _proj_res_norm(act_p, act_s, w_down_bf, x1_p, x1_s, final_norm, False)
    return (y_p, sa_p, sb_p), (y_s, sa_s, sb_s)


def kernel(x_prompt, x_sample, state_gla, state_ret, norm_mix, w_in, w_gla_gate_up, b_gla_gate, gla_norm_w,
           w_gla_up, ret_norm_w, w_ret_up, w_out, norm_ffn, w_ffn_gate, w_ffn_up, w_ffn_down, norm_final):
    depth = w_in.shape[0]
    assert depth == 1, "single-layer trunk"
    batch, seq, d = x_prompt.shape
    lay = _layout(d, w_in.shape[-1], state_gla, state_ret, w_gla_gate_up.shape[1])
    rh, rdk = lay["ret_heads"], lay["ret_dk"]
    assert rdk // 2 == V7X_LANES
    lg = jnp.log1p(-jnp.exp(jnp.linspace(math.log(1.0 / 32), math.log(1.0 / 512), rh))).astype(F32)
    log_gamma = jnp.broadcast_to(lg[:, None, None], (rh, 1, V7X_LANES))

    wts = (norm_mix[0], w_in[0], w_gla_gate_up[0], b_gla_gate[0], gla_norm_w[0], w_gla_up[0], ret_norm_w[0],
           w_ret_up[0], w_out[0], norm_ffn[0], w_ffn_gate[0], w_ffn_up[0], w_ffn_down[0])
    (y_p, ga_p, re_p), (y_s, ga_s, re_s) = _layer(
        x_prompt, x_sample, state_gla[0], state_ret[0], wts, lay, log_gamma, norm_final)

    sd = state_gla.dtype
    return (y_p.reshape(batch, seq, d), y_s.reshape(x_sample.shape),
            ga_p[None].astype(sd), re_p[None].astype(state_ret.dtype),
            ga_s[None].astype(sd), re_s[None].astype(state_ret.dtype))
```

```python
import functools
import math

import numpy as np
import jax
import jax.numpy as jnp
from jax import lax
from jax.experimental import pallas as pl
from jax.experimental.pallas import tpu as pltpu

EPS = 1e-6
ROPE_BASE = 10000.0
GLA_GATE_NORM = 16.0
PAST_LEN = 16384

V7X_LANES = 128
V7X_VMEM_REQUEST_CAP = 60000 * 1024
COMPILER_SCRATCH_BYTES = 12 * 1024 * 1024

GLA_CHUNK = 64
GLA_STEP_CHUNKS = 8
LOG2_E = 1.4426950408889634
RET_CHUNK = 128
RET_STEP_CHUNKS = 4
ROW_TILE = 1024
ROW_TILE_WIDE = 2048
ROW_TILE_NARROW = 512

BF16 = jnp.bfloat16
F32 = jnp.float32


def _params(n_axes, *block_bytes, scratch_bytes=0, claim_all=False):
    need = 2 * sum(block_bytes) + scratch_bytes + COMPILER_SCRATCH_BYTES
    if claim_all:
        need = V7X_VMEM_REQUEST_CAP
    return pltpu.CompilerParams(
        dimension_semantics=("arbitrary",) * n_axes,
        vmem_limit_bytes=int(min(V7X_VMEM_REQUEST_CAP, need)),
    )


def _nbytes(shape, dtype):
    return int(np.prod(shape)) * jnp.dtype(dtype).itemsize


def _sigmoid(x):
    return 1.0 / (1.0 + jnp.exp(-x))


def _silu(x):
    return x * _sigmoid(x)


def _log_sigmoid(x):
    return jnp.minimum(x, 0.0) - jnp.log(1.0 + jnp.exp(-jnp.abs(x)))


def _dot(a, b):
    return jnp.dot(a, b, preferred_element_type=F32)


def _dot_nt(a, b):
    return lax.dot_general(a, b, (((1,), (1,)), ((), ())), preferred_element_type=F32)


def _dot_tn(a, b):
    return lax.dot_general(a, b, (((0,), (0,)), ((), ())), preferred_element_type=F32)


def _row_tile(m, want):
    t = min(m, want)
    assert m % t == 0, (m, t)
    return t


def _rmsnorm_rows(x, w):
    ms = jnp.mean(x * x, axis=-1, keepdims=True)
    return x * lax.rsqrt(ms + EPS) * w


def _rope_rows(cos_ref, sin_ref, pos0):
    rows, half = cos_ref.shape
    pos = (lax.broadcasted_iota(jnp.int32, (rows, half), 0) + pos0).astype(F32)
    idx = lax.broadcasted_iota(jnp.int32, (rows, half), 1).astype(F32)
    ang = pos * jnp.exp(idx * (-math.log(ROPE_BASE) / half))
    cos_ref[...] = jnp.cos(ang)
    sin_ref[...] = jnp.sin(ang)


def _rmsnorm_kernel(xp_ref, xs_ref, w_ref, wgd_ref, wup_ref, bup_ref, w0_ref,
                    hp_ref, hs_ref, gp_ref, gs_ref, cosp_ref, sinp_ref, cosd_ref, sind_ref, p0p_ref, p0s_ref,
                    wgd_bf, wup_bf, w0_bf):
    i = pl.program_id(0)

    @pl.when(i == 0)
    def _():
        wgd_bf[...] = wgd_ref[...].astype(BF16)
        wup_bf[...] = jnp.zeros_like(wup_bf)
        wup_bf[0:wup_ref.shape[0], :] = wup_ref[...].astype(BF16)
        w0_bf[...] = w0_ref[...].astype(BF16)

    def rows(x, h_ref, g_ref, p0_ref, cos_ref, sin_ref, pos0):
        h = _rmsnorm_rows(x, w_ref[...]).astype(h_ref.dtype)
        h_ref[...] = h
        gd = _dot_nt(h, wgd_bf[...])
        x = _dot(gd.astype(BF16), wup_bf[...]) + bup_ref[...]
        g_ref[...] = _log_sigmoid(x) * (LOG2_E / GLA_GATE_NORM)
        _rope_rows(cos_ref, sin_ref, pos0)
        p0_ref[...] = _dot_nt(h, w0_bf[...]).astype(p0_ref.dtype)

    rows(xp_ref[...], hp_ref, gp_ref, p0p_ref, cosp_ref, sinp_ref, i * cosp_ref.shape[0])

    @pl.when(i == 0)
    def _():
        rows(xs_ref[...], hs_ref, gs_ref, p0s_ref, cosd_ref, sind_ref, PAST_LEN)


def _rmsnorm_gate(x_p, x_s, w, w_in_t, gate_row0, w_gate_up, bup, seq, half, tn0):
    m_p, d = x_p.shape
    tail = x_s.shape[0]
    rank, gw = w_gate_up.shape
    tm = _row_tile(m_p, ROW_TILE_NARROW)
    n_steps = m_p // tm
    assert gate_row0 % V7X_LANES == 0 and seq % n_steps == 0 and rank % 16 == 0
    pos_rows = seq // n_steps
    table = pl.BlockSpec((pos_rows, half), lambda i: (i, 0))
    table_dec = pl.BlockSpec((8, half), lambda i: (0, 0))
    return pl.pallas_call(
        _rmsnorm_kernel,
        grid=(n_steps,),
        in_specs=[
            pl.BlockSpec((tm, d), lambda i: (i, 0)),
            pl.BlockSpec((tail, None, d), lambda i: (0, 0, 0)),
            pl.BlockSpec((1, d), lambda i: (0, 0)),
            pl.BlockSpec((V7X_LANES, d), lambda i: (gate_row0 // V7X_LANES, 0)),
            pl.BlockSpec((rank, gw), lambda i: (0, 0)),
            pl.BlockSpec((1, gw), lambda i: (0, 0)),
            pl.BlockSpec((tn0, d), lambda i: (0, 0)),
        ],
        out_specs=[
            pl.BlockSpec((tm, d), lambda i: (i, 0)), pl.BlockSpec((tail, d), lambda i: (0, 0)),
            pl.BlockSpec((tm, gw), lambda i: (i, 0)), pl.BlockSpec((tail, gw), lambda i: (0, 0)),
            table, table, table_dec, table_dec,
            pl.BlockSpec((tm, tn0), lambda i: (i, 0)), pl.BlockSpec((tail, tn0), lambda i: (0, 0)),
        ],
        out_shape=[
            jax.ShapeDtypeStruct((m_p, d), BF16), jax.ShapeDtypeStruct((tail, d), BF16),
            jax.ShapeDtypeStruct((m_p, gw), F32), jax.ShapeDtypeStruct((tail, gw), F32),
            jax.ShapeDtypeStruct((seq, half), F32), jax.ShapeDtypeStruct((seq, half), F32),
            jax.ShapeDtypeStruct((8, half), F32), jax.ShapeDtypeStruct((8, half), F32),
            jax.ShapeDtypeStruct((m_p, tn0), BF16), jax.ShapeDtypeStruct((tail, tn0), BF16),
        ],
        scratch_shapes=[pltpu.VMEM((V7X_LANES, d), BF16), pltpu.VMEM((V7X_LANES, gw), BF16),
                        pltpu.VMEM((tn0, d), BF16)],
        compiler_params=_params(1, claim_all=True),
        name="rmsnorm_gate",
    )(x_p, x_s, w.reshape(1, d), w_in_t, w_gate_up, bup, w_in_t)


def _in_proj_kernel(hp_ref, hs_ref, wm_ref, wn_ref, op_ref, os_ref, wbf_ref, *, n_plain, shift, first_tile):
    j = pl.program_id(0) + first_tile
    i = pl.program_id(1)
    tn = wbf_ref.shape[0]

    @pl.when(jnp.logical_and(i == 0, j < n_plain))
    def _():
        wbf_ref[...] = wm_ref[...].astype(BF16)

    @pl.when(jnp.logical_and(i == 0, j >= n_plain))
    def _():
        wbf_ref[0:tn - shift, :] = wm_ref[shift:tn, :].astype(BF16)
        wbf_ref[tn - shift:tn, :] = wn_ref[...].astype(BF16)

    @pl.when(i == 0)
    def _():
        os_ref[...] = _dot_nt(hs_ref[...], wbf_ref[...]).astype(os_ref.dtype)

    op_ref[...] = _dot_nt(hp_ref[...], wbf_ref[...]).astype(op_ref.dtype)


def _in_proj(h_p, h_s, w_in_t, plain_cols, shift, out_cols, tn, first_tile):
    m_p, d = h_p.shape
    tail = h_s.shape[0]
    tm = _row_tile(m_p, ROW_TILE_WIDE)
    assert plain_cols % tn == 0 and out_cols % tn == 0 and tn % shift == 0 and shift % 8 == 0
    n_plain = plain_cols // tn
    n_tiles = out_cols // tn - first_tile
    kern = functools.partial(_in_proj_kernel, n_plain=n_plain, shift=shift, first_tile=first_tile)
    return pl.pallas_call(
        kern,
        grid=(n_tiles, m_p // tm),
        in_specs=[
            pl.BlockSpec((tm, d), lambda j, i: (i, 0)),
            pl.BlockSpec((tail, d), lambda j, i: (0, 0)),
            pl.BlockSpec((tn, d), lambda j, i: (j + first_tile, 0)),
            pl.BlockSpec((shift, d), lambda j, i: ((j + first_tile + 1) * (tn // shift), 0)),
        ],
        out_specs=[pl.BlockSpec((tm, tn), lambda j, i: (i, j)), pl.BlockSpec((tail, tn), lambda j, i: (0, j))],
        out_shape=[jax.ShapeDtypeStruct((m_p, n_tiles * tn), BF16), jax.ShapeDtypeStruct((tail, n_tiles * tn), BF16)],
        scratch_shapes=[pltpu.VMEM((tn, d), BF16)],
        compiler_params=_params(
            2, _nbytes((tm, d), BF16), _nbytes((tail, d), BF16), _nbytes((tn, d), F32), _nbytes((shift, d), F32),
            _nbytes((tm, tn), BF16), _nbytes((tail, tn), BF16),
            scratch_bytes=_nbytes((tn, d), BF16) + _nbytes((tm, tn), F32)),
        name="in_proj",
    )(h_p, h_s, w_in_t, w_in_t)


def _prefix_sum_rows(sel3_bf16, g):
    g0 = g.astype(BF16)
    r1 = g - g0.astype(F32)
    g1 = r1.astype(BF16)
    g2 = (r1 - g1.astype(F32)).astype(BF16)
    return _dot(sel3_bf16, jnp.concatenate([g0, g1, g2], axis=0))


def _lane_bcast_cols(row, n):
    parts = []
    for c in range(n // V7X_LANES):
        tile = jnp.broadcast_to(row[:, c * V7X_LANES:(c + 1) * V7X_LANES], (V7X_LANES, V7X_LANES))
        parts.append(tile.T)
    return parts[0] if len(parts) == 1 else jnp.concatenate(parts, axis=0)


def _rms_gate_store(o, w, gate, out_ref, rows, cols):
    ms = jnp.mean(o * o, axis=-1, keepdims=True)
    y = o * lax.rsqrt(ms + EPS) * w
    out_ref[rows, cols] = (y * _silu(gate)).astype(out_ref.dtype)


def _ln_gate_store(o, w, gate, out_ref, rows, cols):
    mu = jnp.mean(o, axis=-1, keepdims=True)
    dlt = o - mu
    var = jnp.mean(dlt * dlt, axis=-1, keepdims=True)
    y = dlt * lax.rsqrt(var + EPS) * w
    out_ref[rows, cols] = (y * _silu(gate)).astype(out_ref.dtype)


def _token_selectors(n_tok):
    assert 3 * n_tok <= V7X_LANES
    j = lax.broadcasted_iota(jnp.int32, (V7X_LANES, V7X_LANES), 0)
    sel = []
    for t in range(n_tok):
        hit = jnp.logical_or(j == t, jnp.logical_or(j == n_tok + t, j == 2 * n_tok + t))
        sel.append(jnp.where(hit, 1.0, 0.0).astype(BF16))
    return jnp.stack(sel, axis=0)


def _column_source(x):
    n_tok, w = x.shape
    hi = x.astype(BF16).astype(F32)
    r1 = x - hi
    mid = r1.astype(BF16).astype(F32)
    lo = (r1 - mid).astype(BF16).astype(F32)
    x3 = jnp.concatenate([hi, mid, lo, jnp.zeros((V7X_LANES - 3 * n_tok, w), F32)], axis=0)
    parts = [x3[:, c * V7X_LANES:(c + 1) * V7X_LANES].T for c in range(w // V7X_LANES)]
    return (parts[0] if len(parts) == 1 else jnp.concatenate(parts, axis=0)).astype(BF16)


def _decode_advance(tok0, decay_rows_fn, decay_const_fn, k_ref, q_ref, v_ref, s_in_ref, s_out_ref, o_ref, sel_ref,
                    *, heads, dk, dv):
    n_tok = s_in_ref.shape[0]
    reps = dv // V7X_LANES
    rows = pl.ds(pl.multiple_of(tok0, n_tok), n_tok)

    def cols(src, tt):
        return jnp.concatenate([_dot(src, sel_ref[tt])] * reps, axis=1)

    for hh in range(heads):
        kc = slice(hh * dk, (hh + 1) * dk)
        vc = slice(hh * dv, (hh + 1) * dv)
        k_src, q_src = _column_source(k_ref[rows, kc]), _column_source(q_ref[rows, kc])
        a_src = None if decay_rows_fn is None else _column_source(decay_rows_fn(rows, hh))
        v = v_ref[rows, vc]
        o_rows = []
        for tt in range(n_tok):
            decay = decay_const_fn(hh) if a_src is None else cols(a_src, tt)
            s_new = decay * s_in_ref[tt, hh] + cols(k_src, tt) * v[tt:tt + 1, :]
            s_out_ref[tt, hh] = s_new
            o_rows.append(jnp.sum(cols(q_src, tt) * s_new, axis=0, keepdims=True))
        o_ref[rows, vc] = jnp.concatenate(o_rows, axis=0)


def _decode_plan(n_dec, n_steps):
    assert n_dec % n_steps == 0 and (n_dec // n_steps) % 8 == 0, (n_dec, n_steps)
    return n_dec // n_steps


def _gla_decode_section(step, last_step, refs, scratch, *, heads, dk, dv):
    qd_ref, kd_ref, vd_ref, gad_ref, gd_ref, nw_ref, sd_in_ref, od_ref, sd_out_ref = refs
    a_dec, q_dec, k_dec, v_dec, o_dec, sel_ref = scratch

    @pl.when(step == 0)
    def _():
        a_dec[...] = jnp.exp2(gd_ref[...])
        q_dec[...] = qd_ref[...].astype(F32) * (dk ** -0.5)
        k_dec[...] = kd_ref[...].astype(F32)
        v_dec[...] = vd_ref[...].astype(F32)
        sel_ref[...] = _token_selectors(sd_in_ref.shape[0])

    _decode_advance(step * sd_in_ref.shape[0], lambda rows, hh: a_dec[rows, hh * dk:(hh + 1) * dk], None,
                    k_dec, q_dec, v_dec, sd_in_ref, sd_out_ref, o_dec, sel_ref, heads=heads, dk=dk, dv=dv)

    @pl.when(step == last_step)
    def _():
        n_dec = o_dec.shape[0]
        for hh in range(heads):
            vc = slice(hh * dv, (hh + 1) * dv)
            _rms_gate_store(o_dec[:, vc], nw_ref[...], gad_ref[:, vc].astype(F32), od_ref, slice(0, n_dec), vc)


def _ret_decode_section(step, last_step, refs, scratch, *, heads, dk, dv):
    qd_ref, kd_ref, vd_ref, gbd_ref, cosd_ref, sind_ref, lg_ref, nw_ref, sd_in_ref, od_ref, sd_out_ref = refs
    q_dec, k_dec, v_dec, o_dec, sel_ref = scratch

    @pl.when(step == 0)
    def _():
        cosd, sind = cosd_ref[0:1, :], sind_ref[0:1, :]
        for hh in range(heads):
            kc = slice(hh * dk, (hh + 1) * dk)
            q_dec[:, kc] = _rotary(qd_ref[:, kc].astype(F32), cosd, sind)
            k_dec[:, kc] = _rotary(kd_ref[:, kc].astype(F32), cosd, sind) * (dk ** -0.5)
        v_dec[...] = vd_ref[...].astype(F32)
        sel_ref[...] = _token_selectors(sd_in_ref.shape[0])

    def gamma(hh):
        return jnp.exp(jnp.concatenate([lg_ref[hh]] * (dv // V7X_LANES), axis=1))

    _decode_advance(step * sd_in_ref.shape[0], None, gamma, k_dec, q_dec, v_dec, sd_in_ref, sd_out_ref, o_dec,
                    sel_ref, heads=heads, dk=dk, dv=dv)

    @pl.when(step == last_step)
    def _():
        n_dec = o_dec.shape[0]
        for hh in range(heads):
            vc = slice(hh * dv, (hh + 1) * dv)
            _ln_gate_store(o_dec[:, vc], nw_ref[...], gbd_ref[:, vc].astype(F32), od_ref, slice(0, n_dec), vc)


def _segment_block(lay, name, width):
    offset = lay["cols"][name][1]
    assert offset % width == 0
    return offset // width


def _segment_arrays(lay, arrays, names):
    return tuple(arrays[lay["cols"][n][0]] for n in names)


def _decode_specs(kind, lay, n_dec, tps, row):
    heads, dk, dv = lay[kind + "_heads"], lay[kind + "_dk"], lay[kind + "_dv"]
    qk, vw = heads * dk, heads * dv
    names = ("qa", "ka", "va", "ga") if kind == "gla" else ("qb", "kb", "vb", "gb")
    widths = (qk, qk, vw, vw)
    rows_in = [pl.BlockSpec((n_dec, w), lambda b, t, c=_segment_block(lay, n, w): (0, c))
               for n, w in zip(names, widths)]
    state = pl.BlockSpec((tps, heads, dk, dv), lambda b, t: (row(b, t), 0, 0, 0))
    o_spec = pl.BlockSpec((n_dec, vw), lambda b, t: (0, 0))
    n_qk = 3 if kind == "gla" else 2
    scratch = ([pltpu.VMEM((n_dec, qk), F32)] * n_qk + [pltpu.VMEM((n_dec, vw), F32)] * 2
               + [pltpu.VMEM((tps, V7X_LANES, V7X_LANES), BF16)])
    return rows_in, state, o_spec, scratch


def _gla_sum_matrices(c):
    levels = c.bit_length() - 1
    assert 1 << levels == c
    i = lax.broadcasted_iota(jnp.int32, (c, c), 0)
    j = lax.broadcasted_iota(jnp.int32, (c, c), 1)
    mats = [j <= i]
    for l in range(levels):
        ref = jnp.bitwise_or(jnp.bitwise_and(i, -(2 << l)), 1 << l)
        mats.append(jnp.logical_and(j > jnp.minimum(i, ref), j <= jnp.maximum(i, ref)))
    mats.append(j > i)
    sel = jnp.concatenate([jnp.where(m, 1.0, 0.0).astype(BF16) for m in mats], axis=0)
    return jnp.concatenate([sel, sel, sel], axis=1)


def _pair_level(c):
    levels = c.bit_length() - 1
    i = lax.broadcasted_iota(jnp.int32, (c, c), 0)
    j = lax.broadcasted_iota(jnp.int32, (c, c), 1)
    x = jnp.bitwise_xor(i, j)
    lvl = jnp.zeros((c, c), jnp.int32)
    for l in range(1, levels):
        lvl = lvl + jnp.where(x >= (1 << l), 1, 0)
    return jnp.where(i > j, lvl, jnp.where(i == j, levels, -1))


def _queries_else_keys(q, k, l):
    c = q.shape[0]
    span = 1 << l
    if span >= 8:
        parts = [(q if (b & 1) else k)[b * span:(b + 1) * span, :] for b in range(c // span)]
        return jnp.concatenate(parts, axis=0)
    row = lax.broadcasted_iota(jnp.int32, q.shape, 0)
    return jnp.where(jnp.bitwise_and(row, span) != 0, q, k)


def _gla_level_scores(q, k, sums):
    c = q.shape[0]
    levels = c.bit_length() - 1
    out = []
    for l in range(levels):
        x = _queries_else_keys(q, k, l) * jnp.exp2(sums[(1 + l) * c:(2 + l) * c, :])
        xb = x.astype(BF16)
        out.append(_dot_nt(xb, xb))
    return out


def _gla_chunk_out(q, k, v, sums, level_scores, pair_level, state):
    c = q.shape[0]
    levels = c.bit_length() - 1
    scores = jnp.where(pair_level == levels, jnp.sum(q * k, axis=-1, keepdims=True), 0.0)
    for l in range(levels):
        scores = jnp.where(pair_level == l, level_scores[l], scores)
    o = _dot((q * jnp.exp2(sums[0:c, :])).astype(BF16), state.astype(BF16))
    return o + _dot(scores.astype(BF16), v)


def _gla_next_state(k, v, sums, state):
    c, dk = k.shape
    levels = c.bit_length() - 1
    k_tail = (k * jnp.exp2(sums[(levels + 1) * c:(levels + 2) * c, :])).astype(BF16)
    decay = _lane_bcast_cols(jnp.exp2(sums[c - 1:c, :]), dk)
    decay_full = jnp.concatenate([decay] * (v.shape[1] // V7X_LANES), axis=1)
    return decay_full * state + _dot_tn(k_tail, v)


def _gla_kernel(*refs, heads, dk, dv, dec_dims):
    q_ref, k_ref, v_ref, ga_ref, g_ref, nw_ref = refs[:6]
    dec_in = refs[6:15]
    o_ref, s_out_ref = refs[15:17]
    dec_out = refs[17:19]
    s_ref, mats_ref, lvl_ref = refs[19:22]
    dec_scratch = refs[22:]
    t = pl.program_id(1)
    step = pl.program_id(0) * pl.num_programs(1) + t
    last_step = pl.num_programs(0) * pl.num_programs(1) - 1

    @pl.when(t == 0)
    def _():
        s_ref[...] = jnp.zeros_like(s_ref)
        mats_ref[...] = _gla_sum_matrices(GLA_CHUNK)
        lvl_ref[...] = _pair_level(GLA_CHUNK)

    _ret_decode_section(step, last_step, dec_in + dec_out, dec_scratch,
                        heads=dec_dims[0], dk=dec_dims[1], dv=dec_dims[2])

    ct = q_ref.shape[0]

    kcs = [slice(hh * dk, (hh + 1) * dk) for hh in range(heads)]
    vcs = [slice(hh * dv, (hh + 1) * dv) for hh in range(heads)]
    group = 2 if (ct // GLA_CHUNK) % 2 == 0 else 1

    def chunk_group(cg, carry):
        pair_level = lvl_ref[...]
        rows, sums, qs, ks, lvl_scores = [], [], [], [], []
        for u in range(group):
            r = pl.ds(pl.multiple_of((cg * group + u) * GLA_CHUNK, GLA_CHUNK), GLA_CHUNK)
            rows.append(r)
            sums.append(_prefix_sum_rows(mats_ref[...], g_ref[r, :]))
            qs.append([q_ref[r, kc].astype(F32) * (dk ** -0.5) for kc in kcs])
            ks.append([k_ref[r, kc].astype(F32) for kc in kcs])
            lvl_scores.append([_gla_level_scores(qs[u][hh], ks[u][hh], sums[u][:, kcs[hh]]) for hh in range(heads)])
        outs = []
        for u in range(group):
            r = rows[u]
            outs.append([_gla_chunk_out(qs[u][hh], ks[u][hh], v_ref[r, vcs[hh]], sums[u][:, kcs[hh]],
                                        lvl_scores[u][hh], pair_level, s_ref[hh]) for hh in range(heads)])
            for hh in range(heads):
                s_ref[hh] = _gla_next_state(ks[u][hh], v_ref[r, vcs[hh]], sums[u][:, kcs[hh]], s_ref[hh])
        for u in range(group):
            for hh in range(heads):
                _rms_gate_store(outs[u][hh], nw_ref[...], ga_ref[rows[u], vcs[hh]].astype(F32), o_ref, rows[u],
                                vcs[hh])
        return carry

    lax.fori_loop(0, ct // GLA_CHUNK // group, chunk_group, 0)

    @pl.when(t == pl.num_programs(1) - 1)
    def _():
        s_out_ref[0] = s_ref[...]


def _gla_prompt_ret_decode(projs, log2_decay, gla_norm_w, projs_dec, cos_dec, sin_dec, log_gamma, ret_norm_w,
                           ret_state_dec, lay, batch, seq):
    heads, dk, dv = lay["gla_heads"], lay["gla_dk"], lay["gla_dv"]
    qk, vw = heads * dk, heads * dv
    r_heads, r_dk, r_dv = lay["ret_heads"], lay["ret_dk"], lay["ret_dv"]
    ct = min(seq, GLA_STEP_CHUNKS * GLA_CHUNK)
    levels = GLA_CHUNK.bit_length() - 1
    assert seq % ct == 0 and ct % GLA_CHUNK == 0
    nt = seq // ct
    n_dec = projs_dec[0].shape[0]
    tps = _decode_plan(n_dec, batch * nt)
    row = lambda b, t: b * nt + t
    dec_rows, dec_state, dec_o, dec_scratch = _decode_specs("ret", lay, n_dec, tps, row)
    table_dec = pl.BlockSpec((cos_dec.shape[0], r_dk // 2), lambda b, t: (0, 0))
    kern = functools.partial(_gla_kernel, heads=heads, dk=dk, dv=dv, dec_dims=(r_heads, r_dk, r_dv))
    return pl.pallas_call(
        kern,
        grid=(batch, nt),
        in_specs=[
            pl.BlockSpec((ct, qk), lambda b, t: (row(b, t), _segment_block(lay, "qa", qk))),
            pl.BlockSpec((ct, qk), lambda b, t: (row(b, t), _segment_block(lay, "ka", qk))),
            pl.BlockSpec((ct, vw), lambda b, t: (row(b, t), _segment_block(lay, "va", vw))),
            pl.BlockSpec((ct, vw), lambda b, t: (row(b, t), _segment_block(lay, "ga", vw))),
            pl.BlockSpec((ct, qk), lambda b, t: (row(b, t), 0)),
            pl.BlockSpec((1, dv), lambda b, t: (0, 0)),
        ] + dec_rows + [
            table_dec, table_dec,
            pl.BlockSpec((r_heads, 1, V7X_LANES), lambda b, t: (0, 0, 0)),
            pl.BlockSpec((1, r_dv), lambda b, t: (0, 0)),
            dec_state,
        ],
        out_specs=[
            pl.BlockSpec((ct, vw), lambda b, t: (row(b, t), 0)),
            pl.BlockSpec((1, heads, dk, dv), lambda b, t: (b, 0, 0, 0)),
            dec_o,
            dec_state,
        ],
        out_shape=[
            jax.ShapeDtypeStruct((batch * seq, vw), BF16),
            jax.ShapeDtypeStruct((batch, heads, dk, dv), F32),
            jax.ShapeDtypeStruct((n_dec, r_heads * r_dv), BF16),
            jax.ShapeDtypeStruct(ret_state_dec.shape, ret_state_dec.dtype),
        ],
        scratch_shapes=[
            pltpu.VMEM((heads, dk, dv), F32),
            pltpu.VMEM(((levels + 2) * GLA_CHUNK, 3 * GLA_CHUNK), BF16),
            pltpu.VMEM((GLA_CHUNK, GLA_CHUNK), jnp.int32),
        ] + dec_scratch,
        compiler_params=_params(2, claim_all=True),
        name="gla_prompt_ret_decode",
    )(*_segment_arrays(lay, projs, ("qa", "ka", "va", "ga")), log2_decay, gla_norm_w,
      *_segment_arrays(lay, projs_dec, ("qb", "kb", "vb", "gb")), cos_dec, sin_dec,
      log_gamma, ret_norm_w, ret_state_dec)


def _rotary(x, cos, sin):
    half = x.shape[1] // 2
    x1, x2 = x[:, :half], x[:, half:]
    return jnp.concatenate([x1 * cos - x2 * sin, x1 * sin + x2 * cos], axis=1)


def _ret_kernel(*refs, heads, dk, dv, c, dec_dims):
    q_ref, k_ref, v_ref, gb_ref, cos_ref, sin_ref, lg_ref, nw_ref = refs[:8]
    dec_in = refs[8:15]
    o_ref, s_out_ref = refs[15:17]
    dec_out = refs[17:19]
    s_ref, dmat_ref, qdec_ref, kdec_ref = refs[19:23]
    dec_scratch = refs[23:]
    t = pl.program_id(1)
    step = pl.program_id(0) * pl.num_programs(1) + t
    last_step = pl.num_programs(0) * pl.num_programs(1) - 1

    _gla_decode_section(step, last_step, dec_in + dec_out, dec_scratch,
                        heads=dec_dims[0], dk=dec_dims[1], dv=dec_dims[2])

    @pl.when(t == 0)
    def _():
        s_ref[...] = jnp.zeros_like(s_ref)
        ri = lax.broadcasted_iota(jnp.int32, (c, c), 0)
        rj = lax.broadcasted_iota(jnp.int32, (c, c), 1)
        dist = (ri - rj).astype(F32)
        rowl = lax.broadcasted_iota(jnp.int32, (c, V7X_LANES), 0).astype(F32)
        for hh in range(heads):
            lg = lg_ref[hh]
            dmat_ref[hh] = jnp.exp(jnp.where(ri >= rj, dist * lg[:, :1], -jnp.inf))
            qdec_ref[hh] = jnp.exp((rowl + 1.0) * lg)
            kdec_ref[hh] = jnp.exp((float(c - 1) - rowl) * lg)

    ct = q_ref.shape[0]
    kcs = [slice(hh * dk, (hh + 1) * dk) for hh in range(heads)]
    vcs = [slice(hh * dv, (hh + 1) * dv) for hh in range(heads)]

    def chunk(ci, carry):
        rows = pl.ds(pl.multiple_of(ci * c, c), c)
        cos, sin = cos_ref[rows, :], sin_ref[rows, :]
        qrs = [_rotary(q_ref[rows, kc].astype(F32), cos, sin).astype(BF16) for kc in kcs]
        krs = [_rotary(k_ref[rows, kc].astype(F32), cos, sin) * (dk ** -0.5) for kc in kcs]
        scores = [_dot_nt(qrs[hh], krs[hh].astype(BF16)) * dmat_ref[hh] for hh in range(heads)]
        outs = []
        for hh in range(heads):
            qdec = jnp.concatenate([qdec_ref[hh]] * (dv // V7X_LANES), axis=1)
            o = qdec * _dot(qrs[hh], s_ref[hh].astype(BF16))
            outs.append(o + _dot(scores[hh].astype(BF16), v_ref[rows, vcs[hh]]))
        for hh in range(heads):
            kdec = jnp.concatenate([kdec_ref[hh]] * (dk // V7X_LANES), axis=1)
            k_tail = (krs[hh] * kdec).astype(BF16)
            lgv = jnp.concatenate([lg_ref[hh]] * (dv // V7X_LANES), axis=1)
            s_ref[hh] = jnp.exp(float(c) * lgv) * s_ref[hh] + _dot_tn(k_tail, v_ref[rows, vcs[hh]])
        for hh in range(heads):
            _ln_gate_store(outs[hh], nw_ref[...], gb_ref[rows, vcs[hh]].astype(F32), o_ref, rows, vcs[hh])
        return carry

    lax.fori_loop(0, ct // c, chunk, 0)

    @pl.when(t == pl.num_programs(1) - 1)
    def _():
        s_out_ref[0] = s_ref[...]


def _ret_prompt_gla_decode(projs, cos, sin, log_gamma, ret_norm_w, projs_dec, log2_decay_dec, gla_norm_w,
                           gla_state_dec, lay, batch, seq):
    heads, dk, dv = lay["ret_heads"], lay["ret_dk"], lay["ret_dv"]
    qk, vw = heads * dk, heads * dv
    g_heads, g_dk, g_dv = lay["gla_heads"], lay["gla_dk"], lay["gla_dv"]
    c = min(seq, RET_CHUNK)
    ct = min(seq, RET_STEP_CHUNKS * c)
    assert seq % ct == 0 and ct % c == 0
    nt = seq // ct
    half = dk // 2
    n_dec = projs_dec[0].shape[0]
    tps = _decode_plan(n_dec, batch * nt)
    row = lambda b, t: b * nt + t
    dec_rows, dec_state, dec_o, dec_scratch = _decode_specs("gla", lay, n_dec, tps, row)
    kern = functools.partial(_ret_kernel, heads=heads, dk=dk, dv=dv, c=c, dec_dims=(g_heads, g_dk, g_dv))
    return pl.pallas_call(
        kern,
        grid=(batch, nt),
        in_specs=[
            pl.BlockSpec((ct, qk), lambda b, t: (row(b, t), _segment_block(lay, "qb", qk))),
            pl.BlockSpec((ct, qk), lambda b, t: (row(b, t), _segment_block(lay, "kb", qk))),
            pl.BlockSpec((ct, vw), lambda b, t: (row(b, t), _segment_block(lay, "vb", vw))),
            pl.BlockSpec((ct, vw), lambda b, t: (row(b, t), _segment_block(lay, "gb", vw))),
            pl.BlockSpec((ct, half), lambda b, t: (t, 0)),
            pl.BlockSpec((ct, half), lambda b, t: (t, 0)),
            pl.BlockSpec((heads, 1, V7X_LANES), lambda b, t: (0, 0, 0)),
            pl.BlockSpec((1, dv), lambda b, t: (0, 0)),
        ] + dec_rows + [
            pl.BlockSpec((n_dec, g_heads * g_dk), lambda b, t: (0, 0)),
            pl.BlockSpec((1, g_dv), lambda b, t: (0, 0)),
            dec_state,
        ],
        out_specs=[
            pl.BlockSpec((ct, vw), lambda b, t: (row(b, t), 0)),
            pl.BlockSpec((1, heads, dk, dv), lambda b, t: (b, 0, 0, 0)),
            dec_o,
            dec_state,
        ],
        out_shape=[
            jax.ShapeDtypeStruct((batch * seq, vw), BF16),
            jax.ShapeDtypeStruct((batch, heads, dk, dv), F32),
            jax.ShapeDtypeStruct((n_dec, g_heads * g_dv), BF16),
            jax.ShapeDtypeStruct(gla_state_dec.shape, gla_state_dec.dtype),
        ],
        scratch_shapes=[
            pltpu.VMEM((heads, dk, dv), F32),
            pltpu.VMEM((heads, c, c), F32),
            pltpu.VMEM((heads, c, V7X_LANES), F32),
            pltpu.VMEM((heads, c, V7X_LANES), F32),
        ] + dec_scratch,
        compiler_params=_params(2, claim_all=True),
        name="ret_prompt_gla_decode",
    )(*_segment_arrays(lay, projs, ("qb", "kb", "vb", "gb")), cos, sin, log_gamma, ret_norm_w,
      *_segment_arrays(lay, projs_dec, ("qa", "ka", "va", "ga")), log2_decay_dec, gla_norm_w, gla_state_dec)


def _merge_kernel(oap_ref, obp_ref, oas_ref, obs_ref, wa_ref, wb_ref, g0p_ref, g1p_ref, g0s_ref, g1s_ref, wnext_ref,
                  mp_ref, ms_ref, wnext_bf_ref, wa_bf, wb_bf):
    wnext_bf_ref[...] = wnext_ref[...].astype(BF16)

    def merged(oa, ob, g0, g1):
        ya = _dot(oa, wa_bf[...])
        yb = _dot(ob, wb_bf[...])
        return _sigmoid(g0.astype(F32)) * ya + _sigmoid(g1.astype(F32)) * yb

    @pl.when(pl.program_id(1) == 0)
    def _():
        wa_bf[...] = wa_ref[...].astype(BF16)
        wb_bf[...] = wb_ref[...].astype(BF16)
        ms_ref[...] = merged(oas_ref[...], obs_ref[...], g0s_ref[...], g1s_ref[...]).astype(ms_ref.dtype)

    mp_ref[...] = merged(oap_ref[...], obp_ref[...], g0p_ref[...], g1p_ref[...]).astype(mp_ref.dtype)


def _slab_specs(w_next, n_steps, step_of):
    kn, dn = w_next.shape
    assert kn % n_steps == 0 and (kn // n_steps) % 16 == 0, (kn, n_steps)
    slab = kn // n_steps
    spec = pl.BlockSpec((slab, dn), lambda j, i: (step_of(j, i), 0))
    return spec, spec, jax.ShapeDtypeStruct((kn, dn), BF16), _nbytes((slab, dn), F32) + _nbytes((slab, dn), BF16)


def _merge(oa_p, ob_p, oa_s, ob_s, wa, wb, projs_p, projs_s, lay, w_next):
    proj_p, proj_s = projs_p[lay["cols"]["mg"][0]], projs_s[lay["cols"]["mg"][0]]
    m_p, ka = oa_p.shape
    kb = ob_p.shape[1]
    tail = oa_s.shape[0]
    d = wa.shape[1]
    tm = _row_tile(m_p, ROW_TILE)
    tn = min(d, 1024)
    mg = lay["cols"]["mg"][1]
    assert d % tn == 0 and mg % tn == 0
    g0 = mg // tn
    g1 = (mg + d) // tn
    n_m = m_p // tm
    slab_in, slab_out, slab_shape, slab_bytes = _slab_specs(w_next, (d // tn) * n_m, lambda j, i: j * n_m + i)
    return pl.pallas_call(
        _merge_kernel,
        grid=(d // tn, n_m),
        in_specs=[
            pl.BlockSpec((tm, ka), lambda j, i: (i, 0)),
            pl.BlockSpec((tm, kb), lambda j, i: (i, 0)),
            pl.BlockSpec((tail, ka), lambda j, i: (0, 0)),
            pl.BlockSpec((tail, kb), lambda j, i: (0, 0)),
            pl.BlockSpec((ka, tn), lambda j, i: (0, j)),
            pl.BlockSpec((kb, tn), lambda j, i: (0, j)),
            pl.BlockSpec((tm, tn), lambda j, i: (i, g0 + j)),
            pl.BlockSpec((tm, tn), lambda j, i: (i, g1 + j)),
            pl.BlockSpec((tail, tn), lambda j, i: (0, g0 + j)),
            pl.BlockSpec((tail, tn), lambda j, i: (0, g1 + j)),
            slab_in,
        ],
        out_specs=[pl.BlockSpec((tm, tn), lambda j, i: (i, j)), pl.BlockSpec((tail, tn), lambda j, i: (0, j)),
                   slab_out],
        out_shape=[jax.ShapeDtypeStruct((m_p, d), BF16), jax.ShapeDtypeStruct((tail, d), BF16), slab_shape],
        scratch_shapes=[pltpu.VMEM((ka, tn), BF16), pltpu.VMEM((kb, tn), BF16)],
        compiler_params=_params(
            2, _nbytes((tm, ka), BF16), _nbytes((tm, kb), BF16), _nbytes((tail, ka), BF16), _nbytes((tail, kb), BF16),
            _nbytes((ka, tn), F32), _nbytes((kb, tn), F32), 3 * _nbytes((tm, tn), BF16), 3 * _nbytes((tail, tn), BF16),
            slab_bytes,
            scratch_bytes=_nbytes((ka, tn), BF16) + _nbytes((kb, tn), BF16) + 3 * _nbytes((tm, tn), F32)),
        name="merge",
    )(oa_p, ob_p, oa_s, ob_s, wa, wb, proj_p, proj_p, proj_s, proj_s, w_next)


def _proj_res_norm_kernel(ap_ref, as_ref, w_ref, resp_ref, ress_ref, nw_ref, *out_refs, emit_sum):
    n_out = 2 if emit_sum else 1
    outs_p, outs_s = out_refs[:n_out], out_refs[n_out:]
    d = w_ref.shape[1]
    col_chunk = min(d, 512)

    def rows(a_ref, res_ref, outs):
        x_ref = outs[0]
        nrow = x_ref.shape[0]
        row_chunk = min(nrow, 128)
        assert nrow % row_chunk == 0
        a = a_ref[...]
        for c in range(d // col_chunk):
            cs = slice(c * col_chunk, (c + 1) * col_chunk)
            x_ref[:, cs] = res_ref[:, cs] + _dot(a, w_ref[:, cs])

        def body(c, carry):
            rr = pl.ds(pl.multiple_of(c * row_chunk, row_chunk), row_chunk)
            y = _rmsnorm_rows(x_ref[rr, :], nw_ref[...])
            if emit_sum:
                outs[1][rr, :] = y.astype(outs[1].dtype)
            else:
                x_ref[rr, :] = y
            return carry

        lax.fori_loop(0, nrow // row_chunk, body, 0)

    rows(ap_ref, resp_ref, outs_p)

    @pl.when(pl.program_id(0) == pl.num_programs(0) - 1)
    def _():
        rows(as_ref, ress_ref, outs_s)


def _proj_res_norm(a_p, a_s, w, res_p, res_s, norm_w, emit_sum):
    m_p, kdim = a_p.shape
    tail = a_s.shape[0]
    d = w.shape[1]
    tm = _row_tile(m_p, ROW_TILE_NARROW)
    p_spec = pl.BlockSpec((tm, d), lambda i: (i, 0))
    s_spec = pl.BlockSpec((tail, d), lambda i: (0, 0))
    s3_spec = pl.BlockSpec((tail, None, d), lambda i: (0, 0, 0))
    res_s_spec = s3_spec if res_s.ndim == 3 else s_spec
    out_specs = [p_spec, s3_spec]
    out_shape = [jax.ShapeDtypeStruct((m_p, d), F32), jax.ShapeDtypeStruct((tail, 1, d), F32)]
    assert w.dtype == BF16
    if emit_sum:
        out_specs = [p_spec, p_spec, s_spec, s_spec]
        out_shape = [out_shape[0], jax.ShapeDtypeStruct((m_p, d), BF16),
                     jax.ShapeDtypeStruct((tail, d), F32), jax.ShapeDtypeStruct((tail, d), BF16)]
    in_specs = [
        pl.BlockSpec((tm, kdim), lambda i: (i, 0)),
        pl.BlockSpec((tail, kdim), lambda i: (0, 0)),
        pl.BlockSpec((kdim, d), lambda i: (0, 0), pipeline_mode=pl.Buffered(1)),
        p_spec, res_s_spec, pl.BlockSpec((1, d), lambda i: (0, 0))]
    return pl.pallas_call(
        functools.partial(_proj_res_norm_kernel, emit_sum=emit_sum),
        grid=(m_p // tm,),
        in_specs=in_specs,
        out_specs=out_specs,
        out_shape=out_shape,
        compiler_params=_params(1, claim_all=True),
        name="proj_res_norm",
    )(a_p, a_s, w, res_p, res_s, norm_w.reshape(1, d))


def _swiglu_kernel(hp_ref, hs_ref, wg_ref, wu_ref, wnext_ref, op_ref, os_ref, wnext_bf_ref, wg_bf, wu_bf):
    wnext_bf_ref[...] = wnext_ref[...].astype(BF16)

    tn = wg_bf.shape[1]
    col_chunk = min(tn, 256)

    def act(h_ref, o_ref):
        h = h_ref[...]
        for c in range(tn // col_chunk):
            cs = slice(c * col_chunk, (c + 1) * col_chunk)
            a = _dot(h, wg_bf[:, cs])
            b = _dot(h, wu_bf[:, cs])
            o_ref[:, cs] = (_silu(a) * b).astype(o_ref.dtype)

    @pl.when(pl.program_id(1) == 0)
    def _():
        wg_bf[...] = wg_ref[...].astype(BF16)
        wu_bf[...] = wu_ref[...].astype(BF16)
        act(hs_ref, os_ref)

    act(hp_ref, op_ref)


def _swiglu(h_p, h_s, wg, wu, w_next):
    m_p, d = h_p.shape
    tail = h_s.shape[0]
    f = wg.shape[1]
    tm = _row_tile(m_p, ROW_TILE_WIDE)
    tn = 512 if f % 512 == 0 else 256
    assert f % tn == 0
    n_m = m_p // tm
    slab_in, slab_out, slab_shape, slab_bytes = _slab_specs(w_next, (f // tn) * n_m, lambda j, i: j * n_m + i)
    return pl.pallas_call(
        _swiglu_kernel,
        grid=(f // tn, n_m),
        in_specs=[
            pl.BlockSpec((tm, d), lambda j, i: (i, 0)),
            pl.BlockSpec((tail, d), lambda j, i: (0, 0)),
            pl.BlockSpec((d, tn), lambda j, i: (0, j)),
            pl.BlockSpec((d, tn), lambda j, i: (0, j)),
            slab_in,
        ],
        out_specs=[pl.BlockSpec((tm, tn), lambda j, i: (i, j)), pl.BlockSpec((tail, tn), lambda j, i: (0, j)),
                   slab_out],
        out_shape=[jax.ShapeDtypeStruct((m_p, f), BF16), jax.ShapeDtypeStruct((tail, f), BF16), slab_shape],
        scratch_shapes=[pltpu.VMEM((d, tn), BF16), pltpu.VMEM((d, tn), BF16)],
        compiler_params=_params(
            2, _nbytes((tm, d), BF16), _nbytes((tail, d), BF16), 2 * _nbytes((d, tn), F32),
            _nbytes((tm, tn), BF16), _nbytes((tail, tn), BF16), slab_bytes,
            scratch_bytes=2 * _nbytes((d, tn), BF16) + 3 * _nbytes((tm, tn), F32)),
        name="swiglu",
    )(h_p, h_s, wg, wu, w_next)


def _layout(d_model, in_width, state_gla, state_ret, gate_rank):
    _, _, gh, gdk, gdv = state_gla.shape
    _, _, rh, rdk, rdv = state_ret.shape
    gqk, gv, rqk, rv = gh * gdk, gh * gdv, rh * rdk, rh * rdv
    lay = dict(gla_heads=gh, gla_dk=gdk, gla_dv=gdv, ret_heads=rh, ret_dk=rdk, ret_dv=rdv, rank=gate_rank)
    off = 0
    for name, width in (("qa", gqk), ("ka", gqk), ("va", gv), ("ga", gv), ("qb", rqk), ("kb", rqk),
                        ("vb", rv), ("gb", rv), ("mg", 2 * d_model)):
        lay[name] = off
        off += width
    lay["out_cols"] = off
    lay["plain_cols"] = 2 * gqk + gv
    lay["gd_src"] = lay["plain_cols"]
    assert lay["gd_src"] % V7X_LANES == 0 and gate_rank <= V7X_LANES
    assert in_width == off + gate_rank
    lay["widths"] = dict(qa=gqk, ka=gqk, va=gv, ga=gv, qb=rqk, kb=rqk, vb=rv, gb=rv, mg=2 * d_model)
    return lay


def _split_columns(lay, first_cols):
    cols = {}
    for name, width in lay["widths"].items():
        off = lay[name]
        assert off + width <= first_cols or off >= first_cols, "a segment straddles the two arrays"
        cols[name] = (0, off) if off < first_cols else (1, off - first_cols)
    return cols


def _layer(x_p, x_s, st_gla, st_ret, wts, lay, log_gamma, final_norm):
    (norm_mix, w_in, w_gate_up, b_gate, gla_norm_w, w_gla_up, ret_norm_w, w_ret_up, w_out, norm_ffn,
     w_ffn_gate, w_ffn_up, w_ffn_down) = wts
    batch, seq, d = x_p.shape
    rank = lay["rank"]
    gqk = lay["gla_heads"] * lay["gla_dk"]
    bup = b_gate.reshape(1, gqk)
    gnw = gla_norm_w.reshape(1, -1)
    rnw = ret_norm_w.reshape(1, -1)
    tn = 1024 if (lay["out_cols"] % 1024 == 0 and lay["plain_cols"] % 1024 == 0) else 512
    xp = x_p.reshape(batch * seq, d)
    assert x_s.ndim == 3 and x_s.shape[1] == 1, "one new token per decode sequence"
    xs = x_s
    w_in_t = w_in.T

    h_p, h_s, g_p, g_s, cos_p, sin_p, cos_s, sin_s, first_p, first_s = _rmsnorm_gate(
        xp, xs, norm_mix, w_in_t, lay["gd_src"], w_gate_up, bup, seq, lay["ret_dk"] // 2, tn)
    rest_p, rest_s = _in_proj(h_p, h_s, w_in_t, lay["plain_cols"], rank, lay["out_cols"], tn, 1)
    projs_p, projs_s = (first_p, rest_p), (first_s, rest_s)
    lay = dict(lay, cols=_split_columns(lay, tn))
    oa_p, sa_p, ob_s, sb_s = _gla_prompt_ret_decode(
        projs_p, g_p, gnw, projs_s, cos_s, sin_s, log_gamma, rnw, st_ret, lay, batch, seq)
    ob_p, sb_p, oa_s, sa_s = _ret_prompt_gla_decode(
        projs_p, cos_p, sin_p, log_gamma, rnw, projs_s, g_s, gnw, st_gla, lay, batch, seq)
    m_p, m_s, w_out_bf = _merge(oa_p, ob_p, oa_s, ob_s, w_gla_up, w_ret_up, projs_p, projs_s, lay, w_out)
    x1_p, h2_p, x1_s, h2_s = _proj_res_norm(m_p, m_s, w_out_bf, xp, xs, norm_ffn, True)
    act_p, act_s, w_down_bf = _swiglu(h2_p, h2_s, w_ffn_gate, w_ffn_up, w_ffn_down)
    y_p, y_s = _proj_res_norm(act_p, act_s, w_down_bf, x1_p, x1_s, final_norm, False)
    return (y_p, sa_p, sb_p), (y_s, sa_s, sb_s)


def kernel(x_prompt, x_sample, state_gla, state_ret, norm_mix, w_in, w_gla_gate_up, b_gla_gate, gla_norm_w,
           w_gla_up, ret_norm_w, w_ret_up, w_out, norm_ffn, w_ffn_gate, w_ffn_up, w_ffn_down, norm_final):
    depth = w_in.shape[0]
    assert depth == 1, "single-layer trunk"
    batch, seq, d = x_prompt.shape
    lay = _layout(d, w_in.shape[-1], state_gla, state_ret, w_gla_gate_up.shape[1])
    rh, rdk = lay["ret_heads"], lay["ret_dk"]
    assert rdk // 2 == V7X_LANES
    lg = jnp.log1p(-jnp.exp(jnp.linspace(math.log(1.0 / 32), math.log(1.0 / 512), rh))).astype(F32)
    log_gamma = jnp.broadcast_to(lg[:, None, None], (rh, 1, V7X_LANES))

    wts = (norm_mix[0], w_in[0], w_gla_gate_up[0], b_gla_gate[0], gla_norm_w[0], w_gla_up[0], ret_norm_w[0],
           w_ret_up[0], w_out[0], norm_ffn[0], w_ffn_gate[0], w_ffn_up[0], w_ffn_down[0])
    (y_p, ga_p, re_p), (y_s, ga_s, re_s) = _layer(
        x_prompt, x_sample, state_gla[0], state_ret[0], wts, lay, log_gamma, norm_final)

    sd = state_gla.dtype
    return (y_p.reshape(batch, seq, d), y_s.reshape(x_sample.shape),
            ga_p[None].astype(sd), re_p[None].astype(state_ret.dtype),
            ga_s[None].astype(sd), re_s[None].astype(state_ret.dtype))
```

```python
import functools
import math

import numpy as np
import jax
import jax.numpy as jnp
from jax import lax
from jax.experimental import pallas as pl
from jax.experimental.pallas import tpu as pltpu

EPS = 1e-6
ROPE_BASE = 10000.0
GLA_GATE_NORM = 16.0
PAST_LEN = 16384

V7X_LANES = 128
V7X_VMEM_REQUEST_CAP = 60000 * 1024
COMPILER_SCRATCH_BYTES = 12 * 1024 * 1024

GLA_CHUNK = 64
GLA_STEP_CHUNKS = 8
LOG2_E = 1.4426950408889634
RET_CHUNK = 128
RET_STEP_CHUNKS = 4
ROW_TILE = 1024
ROW_TILE_WIDE = 2048
ROW_TILE_NARROW = 512

BF16 = jnp.bfloat16
F32 = jnp.float32


def _params(n_axes, *block_bytes, scratch_bytes=0, claim_all=False):
    need = 2 * sum(block_bytes) + scratch_bytes + COMPILER_SCRATCH_BYTES
    if claim_all:
        need = V7X_VMEM_REQUEST_CAP
    return pltpu.CompilerParams(
        dimension_semantics=("arbitrary",) * n_axes,
        vmem_limit_bytes=int(min(V7X_VMEM_REQUEST_CAP, need)),
    )


def _nbytes(shape, dtype):
    return int(np.prod(shape)) * jnp.dtype(dtype).itemsize


def _sigmoid(x):
    return 0.5 * jnp.tanh(0.5 * x) + 0.5


def _silu(x):
    return x * _sigmoid(x)


def _log_sigmoid(x):
    return jnp.minimum(x, 0.0) - jnp.log(1.0 + jnp.exp(-jnp.abs(x)))


def _dot(a, b):
    return jnp.dot(a, b, preferred_element_type=F32)


def _dot_nt(a, b):
    return lax.dot_general(a, b, (((1,), (1,)), ((), ())), preferred_element_type=F32)


def _dot_tn(a, b):
    return lax.dot_general(a, b, (((0,), (0,)), ((), ())), preferred_element_type=F32)


def _row_tile(m, want):
    t = min(m, want)
    assert m % t == 0, (m, t)
    return t


def _rmsnorm_rows(x, w):
    ms = jnp.mean(x * x, axis=-1, keepdims=True)
    return x * lax.rsqrt(ms + EPS) * w


def _rope_rows(cos_ref, sin_ref, pos0):
    rows, half = cos_ref.shape
    pos = (lax.broadcasted_iota(jnp.int32, (rows, half), 0) + pos0).astype(F32)
    idx = lax.broadcasted_iota(jnp.int32, (rows, half), 1).astype(F32)
    ang = pos * jnp.exp(idx * (-math.log(ROPE_BASE) / half))
    cos_ref[...] = jnp.cos(ang)
    sin_ref[...] = jnp.sin(ang)


def _rmsnorm_kernel(xp_ref, xs_ref, w_ref, wgd_ref, wup_ref, bup_ref, w0_ref,
                    hp_ref, hs_ref, gp_ref, gs_ref, cosp_ref, sinp_ref, cosd_ref, sind_ref, p0p_ref, p0s_ref,
                    wgd_bf, wup_bf, w0_bf):
    i = pl.program_id(0)

    @pl.when(i == 0)
    def _():
        wgd_bf[...] = wgd_ref[...].astype(BF16)
        wup_bf[...] = jnp.zeros_like(wup_bf)
        wup_bf[0:wup_ref.shape[0], :] = wup_ref[...].astype(BF16)
        w0_bf[...] = w0_ref[...].astype(BF16)

    def rows(x, h_ref, g_ref, p0_ref, cos_ref, sin_ref, pos0):
        h = _rmsnorm_rows(x, w_ref[...]).astype(h_ref.dtype)
        h_ref[...] = h
        gd = _dot_nt(h, wgd_bf[...])
        x = _dot(gd.astype(BF16), wup_bf[...]) + bup_ref[...]
        g_ref[...] = _log_sigmoid(x) * (LOG2_E / GLA_GATE_NORM)
        _rope_rows(cos_ref, sin_ref, pos0)
        p0_ref[...] = _dot_nt(h, w0_bf[...]).astype(p0_ref.dtype)

    rows(xp_ref[...], hp_ref, gp_ref, p0p_ref, cosp_ref, sinp_ref, i * cosp_ref.shape[0])

    @pl.when(i == 0)
    def _():
        rows(xs_ref[...], hs_ref, gs_ref, p0s_ref, cosd_ref, sind_ref, PAST_LEN)


def _rmsnorm_gate(x_p, x_s, w, w_in_t, gate_row0, w_gate_up, bup, seq, half, tn0):
    m_p, d = x_p.shape
    tail = x_s.shape[0]
    rank, gw = w_gate_up.shape
    tm = _row_tile(m_p, ROW_TILE_NARROW)
    n_steps = m_p // tm
    assert gate_row0 % V7X_LANES == 0 and seq % n_steps == 0 and rank % 16 == 0
    pos_rows = seq // n_steps
    table = pl.BlockSpec((pos_rows, half), lambda i: (i, 0))
    table_dec = pl.BlockSpec((8, half), lambda i: (0, 0))
    return pl.pallas_call(
        _rmsnorm_kernel,
        grid=(n_steps,),
        in_specs=[
            pl.BlockSpec((tm, d), lambda i: (i, 0)),
            pl.BlockSpec((tail, None, d), lambda i: (0, 0, 0)),
            pl.BlockSpec((1, d), lambda i: (0, 0)),
            pl.BlockSpec((V7X_LANES, d), lambda i: (gate_row0 // V7X_LANES, 0)),
            pl.BlockSpec((rank, gw), lambda i: (0, 0)),
            pl.BlockSpec((1, gw), lambda i: (0, 0)),
            pl.BlockSpec((tn0, d), lambda i: (0, 0)),
        ],
        out_specs=[
            pl.BlockSpec((tm, d), lambda i: (i, 0)), pl.BlockSpec((tail, d), lambda i: (0, 0)),
            pl.BlockSpec((tm, gw), lambda i: (i, 0)), pl.BlockSpec((tail, gw), lambda i: (0, 0)),
            table, table, table_dec, table_dec,
            pl.BlockSpec((tm, tn0), lambda i: (i, 0)), pl.BlockSpec((tail, tn0), lambda i: (0, 0)),
        ],
        out_shape=[
            jax.ShapeDtypeStruct((m_p, d), BF16), jax.ShapeDtypeStruct((tail, d), BF16),
            jax.ShapeDtypeStruct((m_p, gw), F32), jax.ShapeDtypeStruct((tail, gw), F32),
            jax.ShapeDtypeStruct((seq, half), F32), jax.ShapeDtypeStruct((seq, half), F32),
            jax.ShapeDtypeStruct((8, half), F32), jax.ShapeDtypeStruct((8, half), F32),
            jax.ShapeDtypeStruct((m_p, tn0), BF16), jax.ShapeDtypeStruct((tail, tn0), BF16),
        ],
        scratch_shapes=[pltpu.VMEM((V7X_LANES, d), BF16), pltpu.VMEM((V7X_LANES, gw), BF16),
                        pltpu.VMEM((tn0, d), BF16)],
        compiler_params=_params(1, claim_all=True),
        name="rmsnorm_gate",
    )(x_p, x_s, w.reshape(1, d), w_in_t, w_gate_up, bup, w_in_t)


def _in_proj_kernel(hp_ref, hs_ref, wm_ref, wn_ref, op_ref, os_ref, wbf_ref, *, n_plain, shift, first_tile):
    j = pl.program_id(0) + first_tile
    i = pl.program_id(1)
    tn = wbf_ref.shape[0]

    @pl.when(jnp.logical_and(i == 0, j < n_plain))
    def _():
        wbf_ref[...] = wm_ref[...].astype(BF16)

    @pl.when(jnp.logical_and(i == 0, j >= n_plain))
    def _():
        wbf_ref[0:tn - shift, :] = wm_ref[shift:tn, :].astype(BF16)
        wbf_ref[tn - shift:tn, :] = wn_ref[...].astype(BF16)

    @pl.when(i == 0)
    def _():
        os_ref[...] = _dot_nt(hs_ref[...], wbf_ref[...]).astype(os_ref.dtype)

    op_ref[...] = _dot_nt(hp_ref[...], wbf_ref[...]).astype(op_ref.dtype)


def _in_proj(h_p, h_s, w_in_t, plain_cols, shift, out_cols, tn, first_tile):
    m_p, d = h_p.shape
    tail = h_s.shape[0]
    tm = _row_tile(m_p, ROW_TILE_WIDE)
    assert plain_cols % tn == 0 and out_cols % tn == 0 and tn % shift == 0 and shift % 8 == 0
    n_plain = plain_cols // tn
    n_tiles = out_cols // tn - first_tile
    kern = functools.partial(_in_proj_kernel, n_plain=n_plain, shift=shift, first_tile=first_tile)
    return pl.pallas_call(
        kern,
        grid=(n_tiles, m_p // tm),
        in_specs=[
            pl.BlockSpec((tm, d), lambda j, i: (i, 0)),
            pl.BlockSpec((tail, d), lambda j, i: (0, 0)),
            pl.BlockSpec((tn, d), lambda j, i: (j + first_tile, 0)),
            pl.BlockSpec((shift, d), lambda j, i: ((j + first_tile + 1) * (tn // shift), 0)),
        ],
        out_specs=[pl.BlockSpec((tm, tn), lambda j, i: (i, j)), pl.BlockSpec((tail, tn), lambda j, i: (0, j))],
        out_shape=[jax.ShapeDtypeStruct((m_p, n_tiles * tn), BF16), jax.ShapeDtypeStruct((tail, n_tiles * tn), BF16)],
        scratch_shapes=[pltpu.VMEM((tn, d), BF16)],
        compiler_params=_params(
            2, _nbytes((tm, d), BF16), _nbytes((tail, d), BF16), _nbytes((tn, d), F32), _nbytes((shift, d), F32),
            _nbytes((tm, tn), BF16), _nbytes((tail, tn), BF16),
            scratch_bytes=_nbytes((tn, d), BF16) + _nbytes((tm, tn), F32)),
        name="in_proj",
    )(h_p, h_s, w_in_t, w_in_t)


def _prefix_sum_rows(sel3_bf16, g):
    g0 = g.astype(BF16)
    r1 = g - g0.astype(F32)
    g1 = r1.astype(BF16)
    g2 = (r1 - g1.astype(F32)).astype(BF16)
    return _dot(sel3_bf16, jnp.concatenate([g0, g1, g2], axis=0))


def _lane_bcast_cols(row, n):
    parts = []
    for c in range(n // V7X_LANES):
        tile = jnp.broadcast_to(row[:, c * V7X_LANES:(c + 1) * V7X_LANES], (V7X_LANES, V7X_LANES))
        parts.append(tile.T)
    return parts[0] if len(parts) == 1 else jnp.concatenate(parts, axis=0)


def _rms_gate_store(o, w, gate, out_ref, rows, cols):
    ms = jnp.mean(o * o, axis=-1, keepdims=True)
    y = o * lax.rsqrt(ms + EPS) * w
    out_ref[rows, cols] = (y * _silu(gate)).astype(out_ref.dtype)


def _ln_gate_store(o, w, gate, out_ref, rows, cols):
    mu = jnp.mean(o, axis=-1, keepdims=True)
    dlt = o - mu
    var = jnp.mean(dlt * dlt, axis=-1, keepdims=True)
    y = dlt * lax.rsqrt(var + EPS) * w
    out_ref[rows, cols] = (y * _silu(gate)).astype(out_ref.dtype)


def _token_selectors(n_tok):
    assert 3 * n_tok <= V7X_LANES
    j = lax.broadcasted_iota(jnp.int32, (V7X_LANES, V7X_LANES), 0)
    sel = []
    for t in range(n_tok):
        hit = jnp.logical_or(j == t, jnp.logical_or(j == n_tok + t, j == 2 * n_tok + t))
        sel.append(jnp.where(hit, 1.0, 0.0).astype(BF16))
    return jnp.stack(sel, axis=0)


def _column_source(x):
    n_tok, w = x.shape
    hi = x.astype(BF16).astype(F32)
    r1 = x - hi
    mid = r1.astype(BF16).astype(F32)
    lo = (r1 - mid).astype(BF16).astype(F32)
    x3 = jnp.concatenate([hi, mid, lo, jnp.zeros((V7X_LANES - 3 * n_tok, w), F32)], axis=0)
    parts = [x3[:, c * V7X_LANES:(c + 1) * V7X_LANES].T for c in range(w // V7X_LANES)]
    return (parts[0] if len(parts) == 1 else jnp.concatenate(parts, axis=0)).astype(BF16)


def _decode_advance(tok0, decay_rows_fn, decay_const_fn, k_ref, q_ref, v_ref, s_in_ref, s_out_ref, o_ref, sel_ref,
                    *, heads, dk, dv):
    n_tok = s_in_ref.shape[0]
    reps = dv // V7X_LANES
    rows = pl.ds(pl.multiple_of(tok0, n_tok), n_tok)

    def cols(src, tt):
        return jnp.concatenate([_dot(src, sel_ref[tt])] * reps, axis=1)

    for hh in range(heads):
        kc = slice(hh * dk, (hh + 1) * dk)
        vc = slice(hh * dv, (hh + 1) * dv)
        k_src, q_src = _column_source(k_ref[rows, kc]), _column_source(q_ref[rows, kc])
        a_src = None if decay_rows_fn is None else _column_source(decay_rows_fn(rows, hh))
        v = v_ref[rows, vc]
        o_rows = []
        for tt in range(n_tok):
            decay = decay_const_fn(hh) if a_src is None else cols(a_src, tt)
            s_new = decay * s_in_ref[tt, hh] + cols(k_src, tt) * v[tt:tt + 1, :]
            s_out_ref[tt, hh] = s_new
            o_rows.append(jnp.sum(cols(q_src, tt) * s_new, axis=0, keepdims=True))
        o_ref[rows, vc] = jnp.concatenate(o_rows, axis=0)


def _decode_plan(n_dec, n_steps):
    assert n_dec % n_steps == 0 and (n_dec // n_steps) % 8 == 0, (n_dec, n_steps)
    return n_dec // n_steps


def _gla_decode_section(step, last_step, refs, scratch, *, heads, dk, dv):
    qd_ref, kd_ref, vd_ref, gad_ref, gd_ref, nw_ref, sd_in_ref, od_ref, sd_out_ref = refs
    a_dec, q_dec, k_dec, v_dec, o_dec, sel_ref = scratch

    @pl.when(step == 0)
    def _():
        a_dec[...] = jnp.exp2(gd_ref[...])
        q_dec[...] = qd_ref[...].astype(F32) * (dk ** -0.5)
        k_dec[...] = kd_ref[...].astype(F32)
        v_dec[...] = vd_ref[...].astype(F32)
        sel_ref[...] = _token_selectors(sd_in_ref.shape[0])

    _decode_advance(step * sd_in_ref.shape[0], lambda rows, hh: a_dec[rows, hh * dk:(hh + 1) * dk], None,
                    k_dec, q_dec, v_dec, sd_in_ref, sd_out_ref, o_dec, sel_ref, heads=heads, dk=dk, dv=dv)

    @pl.when(step == last_step)
    def _():
        n_dec = o_dec.shape[0]
        for hh in range(heads):
            vc = slice(hh * dv, (hh + 1) * dv)
            _rms_gate_store(o_dec[:, vc], nw_ref[...], gad_ref[:, vc].astype(F32), od_ref, slice(0, n_dec), vc)


def _ret_decode_section(step, last_step, refs, scratch, *, heads, dk, dv):
    qd_ref, kd_ref, vd_ref, gbd_ref, cosd_ref, sind_ref, lg_ref, nw_ref, sd_in_ref, od_ref, sd_out_ref = refs
    q_dec, k_dec, v_dec, o_dec, sel_ref = scratch

    @pl.when(step == 0)
    def _():
        cosd, sind = cosd_ref[0:1, :], sind_ref[0:1, :]
        for hh in range(heads):
            kc = slice(hh * dk, (hh + 1) * dk)
            q_dec[:, kc] = _rotary(qd_ref[:, kc].astype(F32), cosd, sind)
            k_dec[:, kc] = _rotary(kd_ref[:, kc].astype(F32), cosd, sind) * (dk ** -0.5)
        v_dec[...] = vd_ref[...].astype(F32)
        sel_ref[...] = _token_selectors(sd_in_ref.shape[0])

    def gamma(hh):
        return jnp.exp(jnp.concatenate([lg_ref[hh]] * (dv // V7X_LANES), axis=1))

    _decode_advance(step * sd_in_ref.shape[0], None, gamma, k_dec, q_dec, v_dec, sd_in_ref, sd_out_ref, o_dec,
                    sel_ref, heads=heads, dk=dk, dv=dv)

    @pl.when(step == last_step)
    def _():
        n_dec = o_dec.shape[0]
        for hh in range(heads):
            vc = slice(hh * dv, (hh + 1) * dv)
            _ln_gate_store(o_dec[:, vc], nw_ref[...], gbd_ref[:, vc].astype(F32), od_ref, slice(0, n_dec), vc)


def _segment_block(lay, name, width):
    offset = lay["cols"][name][1]
    assert offset % width == 0
    return offset // width


def _segment_arrays(lay, arrays, names):
    return tuple(arrays[lay["cols"][n][0]] for n in names)


def _decode_specs(kind, lay, n_dec, tps, row):
    heads, dk, dv = lay[kind + "_heads"], lay[kind + "_dk"], lay[kind + "_dv"]
    qk, vw = heads * dk, heads * dv
    names = ("qa", "ka", "va", "ga") if kind == "gla" else ("qb", "kb", "vb", "gb")
    widths = (qk, qk, vw, vw)
    rows_in = [pl.BlockSpec((n_dec, w), lambda b, t, c=_segment_block(lay, n, w): (0, c))
               for n, w in zip(names, widths)]
    state = pl.BlockSpec((tps, heads, dk, dv), lambda b, t: (row(b, t), 0, 0, 0))
    o_spec = pl.BlockSpec((n_dec, vw), lambda b, t: (0, 0))
    n_qk = 3 if kind == "gla" else 2
    scratch = ([pltpu.VMEM((n_dec, qk), F32)] * n_qk + [pltpu.VMEM((n_dec, vw), F32)] * 2
               + [pltpu.VMEM((tps, V7X_LANES, V7X_LANES), BF16)])
    return rows_in, state, o_spec, scratch


def _gla_sum_matrices(c):
    levels = c.bit_length() - 1
    assert 1 << levels == c
    i = lax.broadcasted_iota(jnp.int32, (c, c), 0)
    j = lax.broadcasted_iota(jnp.int32, (c, c), 1)
    mats = [j <= i]
    for l in range(levels):
        ref = jnp.bitwise_or(jnp.bitwise_and(i, -(2 << l)), 1 << l)
        mats.append(jnp.logical_and(j > jnp.minimum(i, ref), j <= jnp.maximum(i, ref)))
    mats.append(j > i)
    sel = jnp.concatenate([jnp.where(m, 1.0, 0.0).astype(BF16) for m in mats], axis=0)
    return jnp.concatenate([sel, sel, sel], axis=1)


def _pair_level(c):
    levels = c.bit_length() - 1
    i = lax.broadcasted_iota(jnp.int32, (c, c), 0)
    j = lax.broadcasted_iota(jnp.int32, (c, c), 1)
    x = jnp.bitwise_xor(i, j)
    lvl = jnp.zeros((c, c), jnp.int32)
    for l in range(1, levels):
        lvl = lvl + jnp.where(x >= (1 << l), 1, 0)
    return jnp.where(i > j, lvl, jnp.where(i == j, levels, -1))


def _queries_else_keys(q, k, l):
    c = q.shape[0]
    span = 1 << l
    if span >= 8:
        parts = [(q if (b & 1) else k)[b * span:(b + 1) * span, :] for b in range(c // span)]
        return jnp.concatenate(parts, axis=0)
    row = lax.broadcasted_iota(jnp.int32, q.shape, 0)
    return jnp.where(jnp.bitwise_and(row, span) != 0, q, k)


def _gla_level_scores(q, k, sums):
    c = q.shape[0]
    levels = c.bit_length() - 1
    out = []
    for l in range(levels):
        x = _queries_else_keys(q, k, l) * jnp.exp2(sums[(1 + l) * c:(2 + l) * c, :])
        xb = x.astype(BF16)
        out.append(_dot_nt(xb, xb))
    return out


def _gla_chunk_out(q, k, v, sums, level_scores, pair_level, state):
    c = q.shape[0]
    levels = c.bit_length() - 1
    scores = jnp.where(pair_level == levels, jnp.sum(q * k, axis=-1, keepdims=True), 0.0)
    for l in range(levels):
        scores = jnp.where(pair_level == l, level_scores[l], scores)
    o = _dot((q * jnp.exp2(sums[0:c, :])).astype(BF16), state.astype(BF16))
    return o + _dot(scores.astype(BF16), v)


def _gla_next_state(k, v, sums, state):
    c, dk = k.shape
    levels = c.bit_length() - 1
    k_tail = (k * jnp.exp2(sums[(levels + 1) * c:(levels + 2) * c, :])).astype(BF16)
    decay = _lane_bcast_cols(jnp.exp2(sums[c - 1:c, :]), dk)
    decay_full = jnp.concatenate([decay] * (v.shape[1] // V7X_LANES), axis=1)
    return decay_full * state + _dot_tn(k_tail, v)


def _gla_kernel(*refs, heads, dk, dv, dec_dims):
    q_ref, k_ref, v_ref, ga_ref, g_ref, nw_ref = refs[:6]
    dec_in = refs[6:15]
    o_ref, s_out_ref = refs[15:17]
    dec_out = refs[17:19]
    s_ref, mats_ref, lvl_ref = refs[19:22]
    dec_scratch = refs[22:]
    t = pl.program_id(1)
    step = pl.program_id(0) * pl.num_programs(1) + t
    last_step = pl.num_programs(0) * pl.num_programs(1) - 1

    @pl.when(t == 0)
    def _():
        s_ref[...] = jnp.zeros_like(s_ref)
        mats_ref[...] = _gla_sum_matrices(GLA_CHUNK)
        lvl_ref[...] = _pair_level(GLA_CHUNK)

    _ret_decode_section(step, last_step, dec_in + dec_out, dec_scratch,
                        heads=dec_dims[0], dk=dec_dims[1], dv=dec_dims[2])

    ct = q_ref.shape[0]

    kcs = [slice(hh * dk, (hh + 1) * dk) for hh in range(heads)]
    vcs = [slice(hh * dv, (hh + 1) * dv) for hh in range(heads)]
    group = 2 if (ct // GLA_CHUNK) % 2 == 0 else 1

    def chunk_group(cg, carry):
        pair_level = lvl_ref[...]
        rows, sums, qs, ks, lvl_scores = [], [], [], [], []
        for u in range(group):
            r = pl.ds(pl.multiple_of((cg * group + u) * GLA_CHUNK, GLA_CHUNK), GLA_CHUNK)
            rows.append(r)
            sums.append(_prefix_sum_rows(mats_ref[...], g_ref[r, :]))
            qs.append([q_ref[r, kc].astype(F32) * (dk ** -0.5) for kc in kcs])
            ks.append([k_ref[r, kc].astype(F32) for kc in kcs])
            lvl_scores.append([_gla_level_scores(qs[u][hh], ks[u][hh], sums[u][:, kcs[hh]]) for hh in range(heads)])
        outs = []
        for u in range(group):
            r = rows[u]
            outs.append([_gla_chunk_out(qs[u][hh], ks[u][hh], v_ref[r, vcs[hh]], sums[u][:, kcs[hh]],
                                        lvl_scores[u][hh], pair_level, s_ref[hh]) for hh in range(heads)])
            for hh in range(heads):
                s_ref[hh] = _gla_next_state(ks[u][hh], v_ref[r, vcs[hh]], sums[u][:, kcs[hh]], s_ref[hh])
        for u in range(group):
            for hh in range(heads):
                _rms_gate_store(outs[u][hh], nw_ref[...], ga_ref[rows[u], vcs[hh]].astype(F32), o_ref, rows[u],
                                vcs[hh])
        return carry

    lax.fori_loop(0, ct // GLA_CHUNK // group, chunk_group, 0)

    @pl.when(t == pl.num_programs(1) - 1)
    def _():
        s_out_ref[0] = s_ref[...]


def _gla_prompt_ret_decode(projs, log2_decay, gla_norm_w, projs_dec, cos_dec, sin_dec, log_gamma, ret_norm_w,
                           ret_state_dec, lay, batch, seq):
    heads, dk, dv = lay["gla_heads"], lay["gla_dk"], lay["gla_dv"]
    qk, vw = heads * dk, heads * dv
    r_heads, r_dk, r_dv = lay["ret_heads"], lay["ret_dk"], lay["ret_dv"]
    ct = min(seq, GLA_STEP_CHUNKS * GLA_CHUNK)
    levels = GLA_CHUNK.bit_length() - 1
    assert seq % ct == 0 and ct % GLA_CHUNK == 0
    nt = seq // ct
    n_dec = projs_dec[0].shape[0]
    tps = _decode_plan(n_dec, batch * nt)
    row = lambda b, t: b * nt + t
    dec_rows, dec_state, dec_o, dec_scratch = _decode_specs("ret", lay, n_dec, tps, row)
    table_dec = pl.BlockSpec((cos_dec.shape[0], r_dk // 2), lambda b, t: (0, 0))
    kern = functools.partial(_gla_kernel, heads=heads, dk=dk, dv=dv, dec_dims=(r_heads, r_dk, r_dv))
    return pl.pallas_call(
        kern,
        grid=(batch, nt),
        in_specs=[
            pl.BlockSpec((ct, qk), lambda b, t: (row(b, t), _segment_block(lay, "qa", qk))),
            pl.BlockSpec((ct, qk), lambda b, t: (row(b, t), _segment_block(lay, "ka", qk))),
            pl.BlockSpec((ct, vw), lambda b, t: (row(b, t), _segment_block(lay, "va", vw))),
            pl.BlockSpec((ct, vw), lambda b, t: (row(b, t), _segment_block(lay, "ga", vw))),
            pl.BlockSpec((ct, qk), lambda b, t: (row(b, t), 0)),
            pl.BlockSpec((1, dv), lambda b, t: (0, 0)),
        ] + dec_rows + [
            table_dec, table_dec,
            pl.BlockSpec((r_heads, 1, V7X_LANES), lambda b, t: (0, 0, 0)),
            pl.BlockSpec((1, r_dv), lambda b, t: (0, 0)),
            dec_state,
        ],
        out_specs=[
            pl.BlockSpec((ct, vw), lambda b, t: (row(b, t), 0)),
            pl.BlockSpec((1, heads, dk, dv), lambda b, t: (b, 0, 0, 0)),
            dec_o,
            dec_state,
        ],
        out_shape=[
            jax.ShapeDtypeStruct((batch * seq, vw), BF16),
            jax.ShapeDtypeStruct((batch, heads, dk, dv), F32),
            jax.ShapeDtypeStruct((n_dec, r_heads * r_dv), BF16),
            jax.ShapeDtypeStruct(ret_state_dec.shape, ret_state_dec.dtype),
        ],
        scratch_shapes=[
            pltpu.VMEM((heads, dk, dv), F32),
            pltpu.VMEM(((levels + 2) * GLA_CHUNK, 3 * GLA_CHUNK), BF16),
            pltpu.VMEM((GLA_CHUNK, GLA_CHUNK), jnp.int32),
        ] + dec_scratch,
        compiler_params=_params(2, claim_all=True),
        name="gla_prompt_ret_decode",
    )(*_segment_arrays(lay, projs, ("qa", "ka", "va", "ga")), log2_decay, gla_norm_w,
      *_segment_arrays(lay, projs_dec, ("qb", "kb", "vb", "gb")), cos_dec, sin_dec,
      log_gamma, ret_norm_w, ret_state_dec)


def _rotary(x, cos, sin):
    half = x.shape[1] // 2
    x1, x2 = x[:, :half], x[:, half:]
    return jnp.concatenate([x1 * cos - x2 * sin, x1 * sin + x2 * cos], axis=1)


def _ret_kernel(*refs, heads, dk, dv, c, dec_dims):
    q_ref, k_ref, v_ref, gb_ref, cos_ref, sin_ref, lg_ref, nw_ref = refs[:8]
    dec_in = refs[8:15]
    o_ref, s_out_ref = refs[15:17]
    dec_out = refs[17:19]
    s_ref, dmat_ref, qdec_ref, kdec_ref = refs[19:23]
    dec_scratch = refs[23:]
    t = pl.program_id(1)
    step = pl.program_id(0) * pl.num_programs(1) + t
    last_step = pl.num_programs(0) * pl.num_programs(1) - 1

    _gla_decode_section(step, last_step, dec_in + dec_out, dec_scratch,
                        heads=dec_dims[0], dk=dec_dims[1], dv=dec_dims[2])

    @pl.when(t == 0)
    def _():
        s_ref[...] = jnp.zeros_like(s_ref)
        ri = lax.broadcasted_iota(jnp.int32, (c, c), 0)
        rj = lax.broadcasted_iota(jnp.int32, (c, c), 1)
        dist = (ri - rj).astype(F32)
        rowl = lax.broadcasted_iota(jnp.int32, (c, V7X_LANES), 0).astype(F32)
        for hh in range(heads):
            lg = lg_ref[hh]
            dmat_ref[hh] = jnp.exp(jnp.where(ri >= rj, dist * lg[:, :1], -jnp.inf))
            qdec_ref[hh] = jnp.exp((rowl + 1.0) * lg)
            kdec_ref[hh] = jnp.exp((float(c - 1) - rowl) * lg)

    ct = q_ref.shape[0]
    kcs = [slice(hh * dk, (hh + 1) * dk) for hh in range(heads)]
    vcs = [slice(hh * dv, (hh + 1) * dv) for hh in range(heads)]

    def chunk(ci, carry):
        rows = pl.ds(pl.multiple_of(ci * c, c), c)
        cos, sin = cos_ref[rows, :], sin_ref[rows, :]
        qrs = [_rotary(q_ref[rows, kc].astype(F32), cos, sin).astype(BF16) for kc in kcs]
        krs = [_rotary(k_ref[rows, kc].astype(F32), cos, sin) * (dk ** -0.5) for kc in kcs]
        scores = [_dot_nt(qrs[hh], krs[hh].astype(BF16)) * dmat_ref[hh] for hh in range(heads)]
        outs = []
        for hh in range(heads):
            qdec = jnp.concatenate([qdec_ref[hh]] * (dv // V7X_LANES), axis=1)
            o = qdec * _dot(qrs[hh], s_ref[hh].astype(BF16))
            outs.append(o + _dot(scores[hh].astype(BF16), v_ref[rows, vcs[hh]]))
        for hh in range(heads):
            kdec = jnp.concatenate([kdec_ref[hh]] * (dk // V7X_LANES), axis=1)
            k_tail = (krs[hh] * kdec).astype(BF16)
            lgv = jnp.concatenate([lg_ref[hh]] * (dv // V7X_LANES), axis=1)
            s_ref[hh] = jnp.exp(float(c) * lgv) * s_ref[hh] + _dot_tn(k_tail, v_ref[rows, vcs[hh]])
        for hh in range(heads):
            _ln_gate_store(outs[hh], nw_ref[...], gb_ref[rows, vcs[hh]].astype(F32), o_ref, rows, vcs[hh])
        return carry

    lax.fori_loop(0, ct // c, chunk, 0)

    @pl.when(t == pl.num_programs(1) - 1)
    def _():
        s_out_ref[0] = s_ref[...]


def _ret_prompt_gla_decode(projs, cos, sin, log_gamma, ret_norm_w, projs_dec, log2_decay_dec, gla_norm_w,
                           gla_state_dec, lay, batch, seq):
    heads, dk, dv = lay["ret_heads"], lay["ret_dk"], lay["ret_dv"]
    qk, vw = heads * dk, heads * dv
    g_heads, g_dk, g_dv = lay["gla_heads"], lay["gla_dk"], lay["gla_dv"]
    c = min(seq, RET_CHUNK)
    ct = min(seq, RET_STEP_CHUNKS * c)
    assert seq % ct == 0 and ct % c == 0
    nt = seq // ct
    half = dk // 2
    n_dec = projs_dec[0].shape[0]
    tps = _decode_plan(n_dec, batch * nt)
    row = lambda b, t: b * nt + t
    dec_rows, dec_state, dec_o, dec_scratch = _decode_specs("gla", lay, n_dec, tps, row)
    kern = functools.partial(_ret_kernel, heads=heads, dk=dk, dv=dv, c=c, dec_dims=(g_heads, g_dk, g_dv))
    return pl.pallas_call(
        kern,
        grid=(batch, nt),
        in_specs=[
            pl.BlockSpec((ct, qk), lambda b, t: (row(b, t), _segment_block(lay, "qb", qk))),
            pl.BlockSpec((ct, qk), lambda b, t: (row(b, t), _segment_block(lay, "kb", qk))),
            pl.BlockSpec((ct, vw), lambda b, t: (row(b, t), _segment_block(lay, "vb", vw))),
            pl.BlockSpec((ct, vw), lambda b, t: (row(b, t), _segment_block(lay, "gb", vw))),
            pl.BlockSpec((ct, half), lambda b, t: (t, 0)),
            pl.BlockSpec((ct, half), lambda b, t: (t, 0)),
            pl.BlockSpec((heads, 1, V7X_LANES), lambda b, t: (0, 0, 0)),
            pl.BlockSpec((1, dv), lambda b, t: (0, 0)),
        ] + dec_rows + [
            pl.BlockSpec((n_dec, g_heads * g_dk), lambda b, t: (0, 0)),
            pl.BlockSpec((1, g_dv), lambda b, t: (0, 0)),
            dec_state,
        ],
        out_specs=[
            pl.BlockSpec((ct, vw), lambda b, t: (row(b, t), 0)),
            pl.BlockSpec((1, heads, dk, dv), lambda b, t: (b, 0, 0, 0)),
            dec_o,
            dec_state,
        ],
        out_shape=[
            jax.ShapeDtypeStruct((batch * seq, vw), BF16),
            jax.ShapeDtypeStruct((batch, heads, dk, dv), F32),
            jax.ShapeDtypeStruct((n_dec, g_heads * g_dv), BF16),
            jax.ShapeDtypeStruct(gla_state_dec.shape, gla_state_dec.dtype),
        ],
        scratch_shapes=[
            pltpu.VMEM((heads, dk, dv), F32),
            pltpu.VMEM((heads, c, c), F32),
            pltpu.VMEM((heads, c, V7X_LANES), F32),
            pltpu.VMEM((heads, c, V7X_LANES), F32),
        ] + dec_scratch,
        compiler_params=_params(2, claim_all=True),
        name="ret_prompt_gla_decode",
    )(*_segment_arrays(lay, projs, ("qb", "kb", "vb", "gb")), cos, sin, log_gamma, ret_norm_w,
      *_segment_arrays(lay, projs_dec, ("qa", "ka", "va", "ga")), log2_decay_dec, gla_norm_w, gla_state_dec)


def _merge_kernel(oap_ref, obp_ref, oas_ref, obs_ref, wa_ref, wb_ref, g0p_ref, g1p_ref, g0s_ref, g1s_ref, wnext_ref,
                  mp_ref, ms_ref, wnext_bf_ref, wa_bf, wb_bf):
    wnext_bf_ref[...] = wnext_ref[...].astype(BF16)

    def merged(oa, ob, g0, g1):
        ya = _dot(oa, wa_bf[...])
        yb = _dot(ob, wb_bf[...])
        return _sigmoid(g0.astype(F32)) * ya + _sigmoid(g1.astype(F32)) * yb

    @pl.when(pl.program_id(1) == 0)
    def _():
        wa_bf[...] = wa_ref[...].astype(BF16)
        wb_bf[...] = wb_ref[...].astype(BF16)
        ms_ref[...] = merged(oas_ref[...], obs_ref[...], g0s_ref[...], g1s_ref[...]).astype(ms_ref.dtype)

    mp_ref[...] = merged(oap_ref[...], obp_ref[...], g0p_ref[...], g1p_ref[...]).astype(mp_ref.dtype)


def _slab_specs(w_next, n_steps, step_of):
    kn, dn = w_next.shape
    assert kn % n_steps == 0 and (kn // n_steps) % 16 == 0, (kn, n_steps)
    slab = kn // n_steps
    spec = pl.BlockSpec((slab, dn), lambda j, i: (step_of(j, i), 0))
    return spec, spec, jax.ShapeDtypeStruct((kn, dn), BF16), _nbytes((slab, dn), F32) + _nbytes((slab, dn), BF16)


def _merge(oa_p, ob_p, oa_s, ob_s, wa, wb, projs_p, projs_s, lay, w_next):
    proj_p, proj_s = projs_p[lay["cols"]["mg"][0]], projs_s[lay["cols"]["mg"][0]]
    m_p, ka = oa_p.shape
    kb = ob_p.shape[1]
    tail = oa_s.shape[0]
    d = wa.shape[1]
    tm = _row_tile(m_p, ROW_TILE)
    tn = min(d, 1024)
    mg = lay["cols"]["mg"][1]
    assert d % tn == 0 and mg % tn == 0
    g0 = mg // tn
    g1 = (mg + d) // tn
    n_m = m_p // tm
    slab_in, slab_out, slab_shape, slab_bytes = _slab_specs(w_next, (d // tn) * n_m, lambda j, i: j * n_m + i)
    return pl.pallas_call(
        _merge_kernel,
        grid=(d // tn, n_m),
        in_specs=[
            pl.BlockSpec((tm, ka), lambda j, i: (i, 0)),
            pl.BlockSpec((tm, kb), lambda j, i: (i, 0)),
            pl.BlockSpec((tail, ka), lambda j, i: (0, 0)),
            pl.BlockSpec((tail, kb), lambda j, i: (0, 0)),
            pl.BlockSpec((ka, tn), lambda j, i: (0, j)),
            pl.BlockSpec((kb, tn), lambda j, i: (0, j)),
            pl.BlockSpec((tm, tn), lambda j, i: (i, g0 + j)),
            pl.BlockSpec((tm, tn), lambda j, i: (i, g1 + j)),
            pl.BlockSpec((tail, tn), lambda j, i: (0, g0 + j)),
            pl.BlockSpec((tail, tn), lambda j, i: (0, g1 + j)),
            slab_in,
        ],
        out_specs=[pl.BlockSpec((tm, tn), lambda j, i: (i, j)), pl.BlockSpec((tail, tn), lambda j, i: (0, j)),
                   slab_out],
        out_shape=[jax.ShapeDtypeStruct((m_p, d), BF16), jax.ShapeDtypeStruct((tail, d), BF16), slab_shape],
        scratch_shapes=[pltpu.VMEM((ka, tn), BF16), pltpu.VMEM((kb, tn), BF16)],
        compiler_params=_params(
            2, _nbytes((tm, ka), BF16), _nbytes((tm, kb), BF16), _nbytes((tail, ka), BF16), _nbytes((tail, kb), BF16),
            _nbytes((ka, tn), F32), _nbytes((kb, tn), F32), 3 * _nbytes((tm, tn), BF16), 3 * _nbytes((tail, tn), BF16),
            slab_bytes,
            scratch_bytes=_nbytes((ka, tn), BF16) + _nbytes((kb, tn), BF16) + 3 * _nbytes((tm, tn), F32)),
        name="merge",
    )(oa_p, ob_p, oa_s, ob_s, wa, wb, proj_p, proj_p, proj_s, proj_s, w_next)


def _proj_res_norm_kernel(ap_ref, as_ref, w_ref, resp_ref, ress_ref, nw_ref, *out_refs, emit_sum):
    n_out = 2 if emit_sum else 1
    outs_p, outs_s = out_refs[:n_out], out_refs[n_out:]
    d = w_ref.shape[1]
    col_chunk = min(d, 512)

    def rows(a_ref, res_ref, outs):
        x_ref = outs[0]
        nrow = x_ref.shape[0]
        row_chunk = min(nrow, 128)
        assert nrow % row_chunk == 0
        a = a_ref[...]
        for c in range(d // col_chunk):
            cs = slice(c * col_chunk, (c + 1) * col_chunk)
            x_ref[:, cs] = res_ref[:, cs] + _dot(a, w_ref[:, cs])

        def body(c, carry):
            rr = pl.ds(pl.multiple_of(c * row_chunk, row_chunk), row_chunk)
            y = _rmsnorm_rows(x_ref[rr, :], nw_ref[...])
            if emit_sum:
                outs[1][rr, :] = y.astype(outs[1].dtype)
            else:
                x_ref[rr, :] = y
            return carry

        lax.fori_loop(0, nrow // row_chunk, body, 0)

    rows(ap_ref, resp_ref, outs_p)

    @pl.when(pl.program_id(0) == pl.num_programs(0) - 1)
    def _():
        rows(as_ref, ress_ref, outs_s)


def _proj_res_norm(a_p, a_s, w, res_p, res_s, norm_w, emit_sum):
    m_p, kdim = a_p.shape
    tail = a_s.shape[0]
    d = w.shape[1]
    tm = _row_tile(m_p, ROW_TILE_NARROW)
    p_spec = pl.BlockSpec((tm, d), lambda i: (i, 0))
    s_spec = pl.BlockSpec((tail, d), lambda i: (0, 0))
    s3_spec = pl.BlockSpec((tail, None, d), lambda i: (0, 0, 0))
    res_s_spec = s3_spec if res_s.ndim == 3 else s_spec
    out_specs = [p_spec, s3_spec]
    out_shape = [jax.ShapeDtypeStruct((m_p, d), F32), jax.ShapeDtypeStruct((tail, 1, d), F32)]
    assert w.dtype == BF16
    if emit_sum:
        out_specs = [p_spec, p_spec, s_spec, s_spec]
        out_shape = [out_shape[0], jax.ShapeDtypeStruct((m_p, d), BF16),
                     jax.ShapeDtypeStruct((tail, d), F32), jax.ShapeDtypeStruct((tail, d), BF16)]
    in_specs = [
        pl.BlockSpec((tm, kdim), lambda i: (i, 0)),
        pl.BlockSpec((tail, kdim), lambda i: (0, 0)),
        pl.BlockSpec((kdim, d), lambda i: (0, 0), pipeline_mode=pl.Buffered(1)),
        p_spec, res_s_spec, pl.BlockSpec((1, d), lambda i: (0, 0))]
    return pl.pallas_call(
        functools.partial(_proj_res_norm_kernel, emit_sum=emit_sum),
        grid=(m_p // tm,),
        in_specs=in_specs,
        out_specs=out_specs,
        out_shape=out_shape,
        compiler_params=_params(1, claim_all=True),
        name="proj_res_norm",
    )(a_p, a_s, w, res_p, res_s, norm_w.reshape(1, d))


def _swiglu_kernel(hp_ref, hs_ref, wg_ref, wu_ref, wnext_ref, op_ref, os_ref, wnext_bf_ref, wg_bf, wu_bf):
    wnext_bf_ref[...] = wnext_ref[...].astype(BF16)

    tn = wg_bf.shape[1]
    col_chunk = min(tn, 256)

    def act(h_ref, o_ref):
        h = h_ref[...]
        for c in range(tn // col_chunk):
            cs = slice(c * col_chunk, (c + 1) * col_chunk)
            a = _dot(h, wg_bf[:, cs])
            b = _dot(h, wu_bf[:, cs])
            o_ref[:, cs] = (_silu(a) * b).astype(o_ref.dtype)

    @pl.when(pl.program_id(1) == 0)
    def _():
        wg_bf[...] = wg_ref[...].astype(BF16)
        wu_bf[...] = wu_ref[...].astype(BF16)
        act(hs_ref, os_ref)

    act(hp_ref, op_ref)


def _swiglu(h_p, h_s, wg, wu, w_next):
    m_p, d = h_p.shape
    tail = h_s.shape[0]
    f = wg.shape[1]
    tm = _row_tile(m_p, ROW_TILE_WIDE)
    tn = 512 if f % 512 == 0 else 256
    assert f % tn == 0
    n_m = m_p // tm
    slab_in, slab_out, slab_shape, slab_bytes = _slab_specs(w_next, (f // tn) * n_m, lambda j, i: j * n_m + i)
    return pl.pallas_call(
        _swiglu_kernel,
        grid=(f // tn, n_m),
        in_specs=[
            pl.BlockSpec((tm, d), lambda j, i: (i, 0)),
            pl.BlockSpec((tail, d), lambda j, i: (0, 0)),
            pl.BlockSpec((d, tn), lambda j, i: (0, j)),
            pl.BlockSpec((d, tn), lambda j, i: (0, j)),
            slab_in,
        ],
        out_specs=[pl.BlockSpec((tm, tn), lambda j, i: (i, j)), pl.BlockSpec((tail, tn), lambda j, i: (0, j)),
                   slab_out],
        out_shape=[jax.ShapeDtypeStruct((m_p, f), BF16), jax.ShapeDtypeStruct((tail, f), BF16), slab_shape],
        scratch_shapes=[pltpu.VMEM((d, tn), BF16), pltpu.VMEM((d, tn), BF16)],
        compiler_params=_params(
            2, _nbytes((tm, d), BF16), _nbytes((tail, d), BF16), 2 * _nbytes((d, tn), F32),
            _nbytes((tm, tn), BF16), _nbytes((tail, tn), BF16), slab_bytes,
            scratch_bytes=2 * _nbytes((d, tn), BF16) + 3 * _nbytes((tm, tn), F32)),
        name="swiglu",
    )(h_p, h_s, wg, wu, w_next)


def _layout(d_model, in_width, state_gla, state_ret, gate_rank):
    _, _, gh, gdk, gdv = state_gla.shape
    _, _, rh, rdk, rdv = state_ret.shape
    gqk, gv, rqk, rv = gh * gdk, gh * gdv, rh * rdk, rh * rdv
    lay = dict(gla_heads=gh, gla_dk=gdk, gla_dv=gdv, ret_heads=rh, ret_dk=rdk, ret_dv=rdv, rank=gate_rank)
    off = 0
    for name, width in (("qa", gqk), ("ka", gqk), ("va", gv), ("ga", gv), ("qb", rqk), ("kb", rqk),
                        ("vb", rv), ("gb", rv), ("mg", 2 * d_model)):
        lay[name] = off
        off += width
    lay["out_cols"] = off
    lay["plain_cols"] = 2 * gqk + gv
    lay["gd_src"] = lay["plain_cols"]
    assert lay["gd_src"] % V7X_LANES == 0 and gate_rank <= V7X_LANES
    assert in_width == off + gate_rank
    lay["widths"] = dict(qa=gqk, ka=gqk, va=gv, ga=gv, qb=rqk, kb=rqk, vb=rv, gb=rv, mg=2 * d_model)
    return lay


def _split_columns(lay, first_cols):
    cols = {}
    for name, width in lay["widths"].items():
        off = lay[name]
        assert off + width <= first_cols or off >= first_cols, "a segment straddles the two arrays"
        cols[name] = (0, off) if off < first_cols else (1, off - first_cols)
    return cols


def _layer(x_p, x_s, st_gla, st_ret, wts, lay, log_gamma, final_norm):
    (norm_mix, w_in, w_gate_up, b_gate, gla_norm_w, w_gla_up, ret_norm_w, w_ret_up, w_out, norm_ffn,
     w_ffn_gate, w_ffn_up, w_ffn_down) = wts
    batch, seq, d = x_p.shape
    rank = lay["rank"]
    gqk = lay["gla_heads"] * lay["gla_dk"]
    bup = b_gate.reshape(1, gqk)
    gnw = gla_norm_w.reshape(1, -1)
    rnw = ret_norm_w.reshape(1, -1)
    tn = 1024 if (lay["out_cols"] % 1024 == 0 and lay["plain_cols"] % 1024 == 0) else 512
    xp = x_p.reshape(batch * seq, d)
    assert x_s.ndim == 3 and x_s.shape[1] == 1, "one new token per decode sequence"
    xs = x_s
    w_in_t = w_in.T

    h_p, h_s, g_p, g_s, cos_p, sin_p, cos_s, sin_s, first_p, first_s = _rmsnorm_gate(
        xp, xs, norm_mix, w_in_t, lay["gd_src"], w_gate_up, bup, seq, lay["ret_dk"] // 2, tn)
    rest_p, rest_s = _in_proj(h_p, h_s, w_in_t, lay["plain_cols"], rank, lay["out_cols"], tn, 1)
    projs_p, projs_s = (first_p, rest_p), (first_s, rest_s)
    lay = dict(lay, cols=_split_columns(lay, tn))
    oa_p, sa_p, ob_s, sb_s = _gla_prompt_ret_decode(
        projs_p, g_p, gnw, projs_s, cos_s, sin_s, log_gamma, rnw, st_ret, lay, batch, seq)
    ob_p, sb_p, oa_s, sa_s = _ret_prompt_gla_decode(
        projs_p, cos_p, sin_p, log_gamma, rnw, projs_s, g_s, gnw, st_gla, lay, batch, seq)
    m_p, m_s, w_out_bf = _merge(oa_p, ob_p, oa_s, ob_s, w_gla_up, w_ret_up, projs_p, projs_s, lay, w_out)
    x1_p, h2_p, x1_s, h2_s = _proj_res_norm(m_p, m_s, w_out_bf, xp, xs, norm_ffn, True)
    act_p, act_s, w_down_bf = _swiglu(h2_p, h2_s, w_ffn_gate, w_ffn_up, w_ffn_down)
    y_p, y_s = _proj_res_norm(act_p, act_s, w_down_bf, x1_p, x1_s, final_norm, False)
    return (y_p, sa_p, sb_p), (y_s, sa_s, sb_s)


def kernel(x_prompt, x_sample, state_gla, state_ret, norm_mix, w_in, w_gla_gate_up, b_gla_gate, gla_norm_w,
           w_gla_up, ret_norm_w, w_ret_up, w_out, norm_ffn, w_ffn_gate, w_ffn_up, w_ffn_down, norm_final):
    depth = w_in.shape[0]
    assert depth == 1, "single-layer trunk"
    batch, seq, d = x_prompt.shape
    lay = _layout(d, w_in.shape[-1], state_gla, state_ret, w_gla_gate_up.shape[1])
    rh, rdk = lay["ret_heads"], lay["ret_dk"]
    assert rdk // 2 == V7X_LANES
    lg = jnp.log1p(-jnp.exp(jnp.linspace(math.log(1.0 / 32), math.log(1.0 / 512), rh))).astype(F32)
    log_gamma = jnp.broadcast_to(lg[:, None, None], (rh, 1, V7X_LANES))

    wts = (norm_mix[0], w_in[0], w_gla_gate_up[0], b_gla_gate[0], gla_norm_w[0], w_gla_up[0], ret_norm_w[0],
           w_ret_up[0], w_out[0], norm_ffn[0], w_ffn_gate[0], w_ffn_up[0], w_ffn_down[0])
    (y_p, ga_p, re_p), (y_s, ga_s, re_s) = _layer(
        x_prompt, x_sample, state_gla[0], state_ret[0], wts, lay, log_gamma, norm_final)

    sd = state_gla.dtype
    return (y_p.reshape(batch, seq, d), y_s.reshape(x_sample.shape),
            ga_p[None].astype(sd), re_p[None].astype(state_ret.dtype),
            ga_s[None].astype(sd), re_s[None].astype(state_ret.dtype))
```

```python
import functools
import math

import numpy as np
import jax
import jax.numpy as jnp
from jax import lax
from jax.experimental import pallas as pl
from jax.experimental.pallas import tpu as pltpu

EPS = 1e-6
ROPE_BASE = 10000.0
GLA_GATE_NORM = 16.0
PAST_LEN = 16384

V7X_LANES = 128
V7X_VMEM_REQUEST_CAP = 60000 * 1024
COMPILER_SCRATCH_BYTES = 12 * 1024 * 1024

GLA_CHUNK = 64
GLA_STEP_CHUNKS = 8
LOG2_E = 1.4426950408889634
RET_CHUNK = 128
RET_STEP_CHUNKS = 4
ROW_TILE = 1024
ROW_TILE_WIDE = 2048
ROW_TILE_NARROW = 512

BF16 = jnp.bfloat16
F32 = jnp.float32


def _params(n_axes, *block_bytes, scratch_bytes=0, claim_all=False):
    need = 2 * sum(block_bytes) + scratch_bytes + COMPILER_SCRATCH_BYTES
    if claim_all:
        need = V7X_VMEM_REQUEST_CAP
    return pltpu.CompilerParams(
        dimension_semantics=("arbitrary",) * n_axes,
        vmem_limit_bytes=int(min(V7X_VMEM_REQUEST_CAP, need)),
    )


def _nbytes(shape, dtype):
    return int(np.prod(shape)) * jnp.dtype(dtype).itemsize


def _sigmoid(x):
    return 0.5 * jnp.tanh(0.5 * x) + 0.5


def _silu(x):
    h = 0.5 * x
    return h * jnp.tanh(h) + h


def _log_sigmoid(x):
    return jnp.minimum(x, 0.0) - jnp.log(1.0 + jnp.exp(-jnp.abs(x)))


def _dot(a, b):
    return jnp.dot(a, b, preferred_element_type=F32)


def _dot_nt(a, b):
    return lax.dot_general(a, b, (((1,), (1,)), ((), ())), preferred_element_type=F32)


def _dot_tn(a, b):
    return lax.dot_general(a, b, (((0,), (0,)), ((), ())), preferred_element_type=F32)


def _row_tile(m, want):
    t = min(m, want)
    assert m % t == 0, (m, t)
    return t


def _rmsnorm_rows(x, w):
    ms = jnp.mean(x * x, axis=-1, keepdims=True)
    return x * lax.rsqrt(ms + EPS) * w


def _rope_rows(cos_ref, sin_ref, pos0):
    rows, half = cos_ref.shape
    pos = (lax.broadcasted_iota(jnp.int32, (rows, half), 0) + pos0).astype(F32)
    idx = lax.broadcasted_iota(jnp.int32, (rows, half), 1).astype(F32)
    ang = pos * jnp.exp(idx * (-math.log(ROPE_BASE) / half))
    cos_ref[...] = jnp.cos(ang)
    sin_ref[...] = jnp.sin(ang)


def _rmsnorm_kernel(xp_ref, xs_ref, w_ref, wgd_ref, wup_ref, bup_ref, w0_ref,
                    hp_ref, hs_ref, gp_ref, gs_ref, cosp_ref, sinp_ref, cosd_ref, sind_ref, p0p_ref, p0s_ref,
                    wgd_bf, wup_bf, w0_bf):
    i = pl.program_id(0)

    @pl.when(i == 0)
    def _():
        wgd_bf[...] = wgd_ref[...].astype(BF16)
        wup_bf[...] = jnp.zeros_like(wup_bf)
        wup_bf[0:wup_ref.shape[0], :] = wup_ref[...].astype(BF16)
        w0_bf[...] = w0_ref[...].astype(BF16)

    def rows(x, h_ref, g_ref, p0_ref, cos_ref, sin_ref, pos0):
        h = _rmsnorm_rows(x, w_ref[...]).astype(h_ref.dtype)
        h_ref[...] = h
        gd = _dot_nt(h, wgd_bf[...])
        x = _dot(gd.astype(BF16), wup_bf[...]) + bup_ref[...]
        g_ref[...] = _log_sigmoid(x) * (LOG2_E / GLA_GATE_NORM)
        _rope_rows(cos_ref, sin_ref, pos0)
        p0_ref[...] = _dot_nt(h, w0_bf[...]).astype(p0_ref.dtype)

    rows(xp_ref[...], hp_ref, gp_ref, p0p_ref, cosp_ref, sinp_ref, i * cosp_ref.shape[0])

    @pl.when(i == 0)
    def _():
        rows(xs_ref[...], hs_ref, gs_ref, p0s_ref, cosd_ref, sind_ref, PAST_LEN)


def _rmsnorm_gate(x_p, x_s, w, w_in_t, gate_row0, w_gate_up, bup, seq, half, tn0):
    m_p, d = x_p.shape
    tail = x_s.shape[0]
    rank, gw = w_gate_up.shape
    tm = _row_tile(m_p, ROW_TILE_NARROW)
    n_steps = m_p // tm
    assert gate_row0 % V7X_LANES == 0 and seq % n_steps == 0 and rank % 16 == 0
    pos_rows = seq // n_steps
    table = pl.BlockSpec((pos_rows, half), lambda i: (i, 0))
    table_dec = pl.BlockSpec((8, half), lambda i: (0, 0))
    return pl.pallas_call(
        _rmsnorm_kernel,
        grid=(n_steps,),
        in_specs=[
            pl.BlockSpec((tm, d), lambda i: (i, 0)),
            pl.BlockSpec((tail, None, d), lambda i: (0, 0, 0)),
            pl.BlockSpec((1, d), lambda i: (0, 0)),
            pl.BlockSpec((V7X_LANES, d), lambda i: (gate_row0 // V7X_LANES, 0)),
            pl.BlockSpec((rank, gw), lambda i: (0, 0)),
            pl.BlockSpec((1, gw), lambda i: (0, 0)),
            pl.BlockSpec((tn0, d), lambda i: (0, 0)),
        ],
        out_specs=[
            pl.BlockSpec((tm, d), lambda i: (i, 0)), pl.BlockSpec((tail, d), lambda i: (0, 0)),
            pl.BlockSpec((tm, gw), lambda i: (i, 0)), pl.BlockSpec((tail, gw), lambda i: (0, 0)),
            table, table, table_dec, table_dec,
            pl.BlockSpec((tm, tn0), lambda i: (i, 0)), pl.BlockSpec((tail, tn0), lambda i: (0, 0)),
        ],
        out_shape=[
            jax.ShapeDtypeStruct((m_p, d), BF16), jax.ShapeDtypeStruct((tail, d), BF16),
            jax.ShapeDtypeStruct((m_p, gw), F32), jax.ShapeDtypeStruct((tail, gw), F32),
            jax.ShapeDtypeStruct((seq, half), F32), jax.ShapeDtypeStruct((seq, half), F32),
            jax.ShapeDtypeStruct((8, half), F32), jax.ShapeDtypeStruct((8, half), F32),
            jax.ShapeDtypeStruct((m_p, tn0), BF16), jax.ShapeDtypeStruct((tail, tn0), BF16),
        ],
        scratch_shapes=[pltpu.VMEM((V7X_LANES, d), BF16), pltpu.VMEM((V7X_LANES, gw), BF16),
                        pltpu.VMEM((tn0, d), BF16)],
        compiler_params=_params(1, claim_all=True),
        name="rmsnorm_gate",
    )(x_p, x_s, w.reshape(1, d), w_in_t, w_gate_up, bup, w_in_t)


def _in_proj_kernel(hp_ref, hs_ref, wm_ref, wn_ref, op_ref, os_ref, wbf_ref, *, n_plain, shift, first_tile):
    j = pl.program_id(0) + first_tile
    i = pl.program_id(1)
    tn = wbf_ref.shape[0]

    @pl.when(jnp.logical_and(i == 0, j < n_plain))
    def _():
        wbf_ref[...] = wm_ref[...].astype(BF16)

    @pl.when(jnp.logical_and(i == 0, j >= n_plain))
    def _():
        wbf_ref[0:tn - shift, :] = wm_ref[shift:tn, :].astype(BF16)
        wbf_ref[tn - shift:tn, :] = wn_ref[...].astype(BF16)

    @pl.when(i == 0)
    def _():
        os_ref[...] = _dot_nt(hs_ref[...], wbf_ref[...]).astype(os_ref.dtype)

    op_ref[...] = _dot_nt(hp_ref[...], wbf_ref[...]).astype(op_ref.dtype)


def _in_proj(h_p, h_s, w_in_t, plain_cols, shift, out_cols, tn, first_tile):
    m_p, d = h_p.shape
    tail = h_s.shape[0]
    tm = _row_tile(m_p, ROW_TILE_WIDE)
    assert plain_cols % tn == 0 and out_cols % tn == 0 and tn % shift == 0 and shift % 8 == 0
    n_plain = plain_cols // tn
    n_tiles = out_cols // tn - first_tile
    kern = functools.partial(_in_proj_kernel, n_plain=n_plain, shift=shift, first_tile=first_tile)
    return pl.pallas_call(
        kern,
        grid=(n_tiles, m_p // tm),
        in_specs=[
            pl.BlockSpec((tm, d), lambda j, i: (i, 0)),
            pl.BlockSpec((tail, d), lambda j, i: (0, 0)),
            pl.BlockSpec((tn, d), lambda j, i: (j + first_tile, 0)),
            pl.BlockSpec((shift, d), lambda j, i: ((j + first_tile + 1) * (tn // shift), 0)),
        ],
        out_specs=[pl.BlockSpec((tm, tn), lambda j, i: (i, j)), pl.BlockSpec((tail, tn), lambda j, i: (0, j))],
        out_shape=[jax.ShapeDtypeStruct((m_p, n_tiles * tn), BF16), jax.ShapeDtypeStruct((tail, n_tiles * tn), BF16)],
        scratch_shapes=[pltpu.VMEM((tn, d), BF16)],
        compiler_params=_params(
            2, _nbytes((tm, d), BF16), _nbytes((tail, d), BF16), _nbytes((tn, d), F32), _nbytes((shift, d), F32),
            _nbytes((tm, tn), BF16), _nbytes((tail, tn), BF16),
            scratch_bytes=_nbytes((tn, d), BF16) + _nbytes((tm, tn), F32)),
        name="in_proj",
    )(h_p, h_s, w_in_t, w_in_t)


def _prefix_sum_rows(sel3_bf16, g):
    g0 = g.astype(BF16)
    r1 = g - g0.astype(F32)
    g1 = r1.astype(BF16)
    g2 = (r1 - g1.astype(F32)).astype(BF16)
    return _dot(sel3_bf16, jnp.concatenate([g0, g1, g2], axis=0))


def _lane_bcast_cols(row, n):
    parts = []
    for c in range(n // V7X_LANES):
        tile = jnp.broadcast_to(row[:, c * V7X_LANES:(c + 1) * V7X_LANES], (V7X_LANES, V7X_LANES))
        parts.append(tile.T)
    return parts[0] if len(parts) == 1 else jnp.concatenate(parts, axis=0)


def _rms_gate_store(o, w, gate, out_ref, rows, cols):
    ms = jnp.mean(o * o, axis=-1, keepdims=True)
    y = o * lax.rsqrt(ms + EPS) * w
    out_ref[rows, cols] = (y * _silu(gate)).astype(out_ref.dtype)


def _ln_gate_store(o, w, gate, out_ref, rows, cols):
    mu = jnp.mean(o, axis=-1, keepdims=True)
    dlt = o - mu
    var = jnp.mean(dlt * dlt, axis=-1, keepdims=True)
    y = dlt * lax.rsqrt(var + EPS) * w
    out_ref[rows, cols] = (y * _silu(gate)).astype(out_ref.dtype)


def _token_selectors(n_tok):
    assert 3 * n_tok <= V7X_LANES
    j = lax.broadcasted_iota(jnp.int32, (V7X_LANES, V7X_LANES), 0)
    sel = []
    for t in range(n_tok):
        hit = jnp.logical_or(j == t, jnp.logical_or(j == n_tok + t, j == 2 * n_tok + t))
        sel.append(jnp.where(hit, 1.0, 0.0).astype(BF16))
    return jnp.stack(sel, axis=0)


def _column_source(x):
    n_tok, w = x.shape
    hi = x.astype(BF16).astype(F32)
    r1 = x - hi
    mid = r1.astype(BF16).astype(F32)
    lo = (r1 - mid).astype(BF16).astype(F32)
    x3 = jnp.concatenate([hi, mid, lo, jnp.zeros((V7X_LANES - 3 * n_tok, w), F32)], axis=0)
    parts = [x3[:, c * V7X_LANES:(c + 1) * V7X_LANES].T for c in range(w // V7X_LANES)]
    return (parts[0] if len(parts) == 1 else jnp.concatenate(parts, axis=0)).astype(BF16)


def _decode_advance(tok0, decay_rows_fn, decay_const_fn, k_ref, q_ref, v_ref, s_in_ref, s_out_ref, o_ref, sel_ref,
                    *, heads, dk, dv):
    n_tok = s_in_ref.shape[0]
    reps = dv // V7X_LANES
    rows = pl.ds(pl.multiple_of(tok0, n_tok), n_tok)

    def cols(src, tt):
        return jnp.concatenate([_dot(src, sel_ref[tt])] * reps, axis=1)

    for hh in range(heads):
        kc = slice(hh * dk, (hh + 1) * dk)
        vc = slice(hh * dv, (hh + 1) * dv)
        k_src, q_src = _column_source(k_ref[rows, kc]), _column_source(q_ref[rows, kc])
        a_src = None if decay_rows_fn is None else _column_source(decay_rows_fn(rows, hh))
        v = v_ref[rows, vc]
        o_rows = []
        for tt in range(n_tok):
            decay = decay_const_fn(hh) if a_src is None else cols(a_src, tt)
            s_new = decay * s_in_ref[tt, hh] + cols(k_src, tt) * v[tt:tt + 1, :]
            s_out_ref[tt, hh] = s_new
            o_rows.append(jnp.sum(cols(q_src, tt) * s_new, axis=0, keepdims=True))
        o_ref[rows, vc] = jnp.concatenate(o_rows, axis=0)


def _decode_plan(n_dec, n_steps):
    assert n_dec % n_steps == 0 and (n_dec // n_steps) % 8 == 0, (n_dec, n_steps)
    return n_dec // n_steps


def _gla_decode_section(step, last_step, refs, scratch, *, heads, dk, dv):
    qd_ref, kd_ref, vd_ref, gad_ref, gd_ref, nw_ref, sd_in_ref, od_ref, sd_out_ref = refs
    a_dec, q_dec, k_dec, v_dec, o_dec, sel_ref = scratch

    @pl.when(step == 0)
    def _():
        a_dec[...] = jnp.exp2(gd_ref[...])
        q_dec[...] = qd_ref[...].astype(F32) * (dk ** -0.5)
        k_dec[...] = kd_ref[...].astype(F32)
        v_dec[...] = vd_ref[...].astype(F32)
        sel_ref[...] = _token_selectors(sd_in_ref.shape[0])

    _decode_advance(step * sd_in_ref.shape[0], lambda rows, hh: a_dec[rows, hh * dk:(hh + 1) * dk], None,
                    k_dec, q_dec, v_dec, sd_in_ref, sd_out_ref, o_dec, sel_ref, heads=heads, dk=dk, dv=dv)

    @pl.when(step == last_step)
    def _():
        n_dec = o_dec.shape[0]
        for hh in range(heads):
            vc = slice(hh * dv, (hh + 1) * dv)
            _rms_gate_store(o_dec[:, vc], nw_ref[...], gad_ref[:, vc].astype(F32), od_ref, slice(0, n_dec), vc)


def _ret_decode_section(step, last_step, refs, scratch, *, heads, dk, dv):
    qd_ref, kd_ref, vd_ref, gbd_ref, cosd_ref, sind_ref, lg_ref, nw_ref, sd_in_ref, od_ref, sd_out_ref = refs
    q_dec, k_dec, v_dec, o_dec, sel_ref = scratch

    @pl.when(step == 0)
    def _():
        cosd, sind = cosd_ref[0:1, :], sind_ref[0:1, :]
        for hh in range(heads):
            kc = slice(hh * dk, (hh + 1) * dk)
            q_dec[:, kc] = _rotary(qd_ref[:, kc].astype(F32), cosd, sind)
            k_dec[:, kc] = _rotary(kd_ref[:, kc].astype(F32), cosd, sind) * (dk ** -0.5)
        v_dec[...] = vd_ref[...].astype(F32)
        sel_ref[...] = _token_selectors(sd_in_ref.shape[0])

    def gamma(hh):
        return jnp.exp(jnp.concatenate([lg_ref[hh]] * (dv // V7X_LANES), axis=1))

    _decode_advance(step * sd_in_ref.shape[0], None, gamma, k_dec, q_dec, v_dec, sd_in_ref, sd_out_ref, o_dec,
                    sel_ref, heads=heads, dk=dk, dv=dv)

    @pl.when(step == last_step)
    def _():
        n_dec = o_dec.shape[0]
        for hh in range(heads):
            vc = slice(hh * dv, (hh + 1) * dv)
            _ln_gate_store(o_dec[:, vc], nw_ref[...], gbd_ref[:, vc].astype(F32), od_ref, slice(0, n_dec), vc)


def _segment_block(lay, name, width):
    offset = lay["cols"][name][1]
    assert offset % width == 0
    return offset // width


def _segment_arrays(lay, arrays, names):
    return tuple(arrays[lay["cols"][n][0]] for n in names)


def _decode_specs(kind, lay, n_dec, tps, row):
    heads, dk, dv = lay[kind + "_heads"], lay[kind + "_dk"], lay[kind + "_dv"]
    qk, vw = heads * dk, heads * dv
    names = ("qa", "ka", "va", "ga") if kind == "gla" else ("qb", "kb", "vb", "gb")
    widths = (qk, qk, vw, vw)
    rows_in = [pl.BlockSpec((n_dec, w), lambda b, t, c=_segment_block(lay, n, w): (0, c))
               for n, w in zip(names, widths)]
    state = pl.BlockSpec((tps, heads, dk, dv), lambda b, t: (row(b, t), 0, 0, 0))
    o_spec = pl.BlockSpec((n_dec, vw), lambda b, t: (0, 0))
    n_qk = 3 if kind == "gla" else 2
    scratch = ([pltpu.VMEM((n_dec, qk), F32)] * n_qk + [pltpu.VMEM((n_dec, vw), F32)] * 2
               + [pltpu.VMEM((tps, V7X_LANES, V7X_LANES), BF16)])
    return rows_in, state, o_spec, scratch


def _gla_sum_matrices(c):
    levels = c.bit_length() - 1
    assert 1 << levels == c
    i = lax.broadcasted_iota(jnp.int32, (c, c), 0)
    j = lax.broadcasted_iota(jnp.int32, (c, c), 1)
    mats = [j <= i]
    for l in range(levels):
        ref = jnp.bitwise_or(jnp.bitwise_and(i, -(2 << l)), 1 << l)
        mats.append(jnp.logical_and(j > jnp.minimum(i, ref), j <= jnp.maximum(i, ref)))
    mats.append(j > i)
    sel = jnp.concatenate([jnp.where(m, 1.0, 0.0).astype(BF16) for m in mats], axis=0)
    return jnp.concatenate([sel, sel, sel], axis=1)


def _pair_level(c):
    levels = c.bit_length() - 1
    i = lax.broadcasted_iota(jnp.int32, (c, c), 0)
    j = lax.broadcasted_iota(jnp.int32, (c, c), 1)
    x = jnp.bitwise_xor(i, j)
    lvl = jnp.zeros((c, c), jnp.int32)
    for l in range(1, levels):
        lvl = lvl + jnp.where(x >= (1 << l), 1, 0)
    return jnp.where(i > j, lvl, jnp.where(i == j, levels, -1))


def _queries_else_keys(q, k, l):
    c = q.shape[0]
    span = 1 << l
    if span >= 8:
        parts = [(q if (b & 1) else k)[b * span:(b + 1) * span, :] for b in range(c // span)]
        return jnp.concatenate(parts, axis=0)
    row = lax.broadcasted_iota(jnp.int32, q.shape, 0)
    return jnp.where(jnp.bitwise_and(row, span) != 0, q, k)


def _gla_level_scores(q, k, sums):
    c = q.shape[0]
    levels = c.bit_length() - 1
    out = []
    for l in range(levels):
        x = _queries_else_keys(q, k, l) * jnp.exp2(sums[(1 + l) * c:(2 + l) * c, :])
        xb = x.astype(BF16)
        out.append(_dot_nt(xb, xb))
    return out


def _gla_chunk_out(q, k, v, sums, level_scores, pair_level, state):
    c = q.shape[0]
    levels = c.bit_length() - 1
    scores = jnp.where(pair_level == levels, jnp.sum(q * k, axis=-1, keepdims=True), 0.0)
    for l in range(levels):
        scores = jnp.where(pair_level == l, level_scores[l], scores)
    o = _dot((q * jnp.exp2(sums[0:c, :])).astype(BF16), state.astype(BF16))
    return o + _dot(scores.astype(BF16), v)


def _gla_next_state(k, v, sums, state):
    c, dk = k.shape
    levels = c.bit_length() - 1
    k_tail = (k * jnp.exp2(sums[(levels + 1) * c:(levels + 2) * c, :])).astype(BF16)
    decay = _lane_bcast_cols(jnp.exp2(sums[c - 1:c, :]), dk)
    decay_full = jnp.concatenate([decay] * (v.shape[1] // V7X_LANES), axis=1)
    return decay_full * state + _dot_tn(k_tail, v)


def _gla_kernel(*refs, heads, dk, dv, dec_dims):
    q_ref, k_ref, v_ref, ga_ref, g_ref, nw_ref = refs[:6]
    dec_in = refs[6:15]
    o_ref, s_out_ref = refs[15:17]
    dec_out = refs[17:19]
    s_ref, mats_ref, lvl_ref = refs[19:22]
    dec_scratch = refs[22:]
    t = pl.program_id(1)
    step = pl.program_id(0) * pl.num_programs(1) + t
    last_step = pl.num_programs(0) * pl.num_programs(1) - 1

    @pl.when(t == 0)
    def _():
        s_ref[...] = jnp.zeros_like(s_ref)
        mats_ref[...] = _gla_sum_matrices(GLA_CHUNK)
        lvl_ref[...] = _pair_level(GLA_CHUNK)

    _ret_decode_section(step, last_step, dec_in + dec_out, dec_scratch,
                        heads=dec_dims[0], dk=dec_dims[1], dv=dec_dims[2])

    ct = q_ref.shape[0]

    kcs = [slice(hh * dk, (hh + 1) * dk) for hh in range(heads)]
    vcs = [slice(hh * dv, (hh + 1) * dv) for hh in range(heads)]
    group = 2 if (ct // GLA_CHUNK) % 2 == 0 else 1

    def chunk_group(cg, carry):
        pair_level = lvl_ref[...]
        rows, sums, qs, ks, lvl_scores = [], [], [], [], []
        for u in range(group):
            r = pl.ds(pl.multiple_of((cg * group + u) * GLA_CHUNK, GLA_CHUNK), GLA_CHUNK)
            rows.append(r)
            sums.append(_prefix_sum_rows(mats_ref[...], g_ref[r, :]))
            qs.append([q_ref[r, kc].astype(F32) * (dk ** -0.5) for kc in kcs])
            ks.append([k_ref[r, kc].astype(F32) for kc in kcs])
            lvl_scores.append([_gla_level_scores(qs[u][hh], ks[u][hh], sums[u][:, kcs[hh]]) for hh in range(heads)])
        outs = []
        for u in range(group):
            r = rows[u]
            outs.append([_gla_chunk_out(qs[u][hh], ks[u][hh], v_ref[r, vcs[hh]], sums[u][:, kcs[hh]],
                                        lvl_scores[u][hh], pair_level, s_ref[hh]) for hh in range(heads)])
            for hh in range(heads):
                s_ref[hh] = _gla_next_state(ks[u][hh], v_ref[r, vcs[hh]], sums[u][:, kcs[hh]], s_ref[hh])
        for u in range(group):
            for hh in range(heads):
                _rms_gate_store(outs[u][hh], nw_ref[...], ga_ref[rows[u], vcs[hh]].astype(F32), o_ref, rows[u],
                                vcs[hh])
        return carry

    lax.fori_loop(0, ct // GLA_CHUNK // group, chunk_group, 0)

    @pl.when(t == pl.num_programs(1) - 1)
    def _():
        s_out_ref[0] = s_ref[...]


def _gla_prompt_ret_decode(projs, log2_decay, gla_norm_w, projs_dec, cos_dec, sin_dec, log_gamma, ret_norm_w,
                           ret_state_dec, lay, batch, seq):
    heads, dk, dv = lay["gla_heads"], lay["gla_dk"], lay["gla_dv"]
    qk, vw = heads * dk, heads * dv
    r_heads, r_dk, r_dv = lay["ret_heads"], lay["ret_dk"], lay["ret_dv"]
    ct = min(seq, GLA_STEP_CHUNKS * GLA_CHUNK)
    levels = GLA_CHUNK.bit_length() - 1
    assert seq % ct == 0 and ct % GLA_CHUNK == 0
    nt = seq // ct
    n_dec = projs_dec[0].shape[0]
    tps = _decode_plan(n_dec, batch * nt)
    row = lambda b, t: b * nt + t
    dec_rows, dec_state, dec_o, dec_scratch = _decode_specs("ret", lay, n_dec, tps, row)
    table_dec = pl.BlockSpec((cos_dec.shape[0], r_dk // 2), lambda b, t: (0, 0))
    kern = functools.partial(_gla_kernel, heads=heads, dk=dk, dv=dv, dec_dims=(r_heads, r_dk, r_dv))
    return pl.pallas_call(
        kern,
        grid=(batch, nt),
        in_specs=[
            pl.BlockSpec((ct, qk), lambda b, t: (row(b, t), _segment_block(lay, "qa", qk))),
            pl.BlockSpec((ct, qk), lambda b, t: (row(b, t), _segment_block(lay, "ka", qk))),
            pl.BlockSpec((ct, vw), lambda b, t: (row(b, t), _segment_block(lay, "va", vw))),
            pl.BlockSpec((ct, vw), lambda b, t: (row(b, t), _segment_block(lay, "ga", vw))),
            pl.BlockSpec((ct, qk), lambda b, t: (row(b, t), 0)),
            pl.BlockSpec((1, dv), lambda b, t: (0, 0)),
        ] + dec_rows + [
            table_dec, table_dec,
            pl.BlockSpec((r_heads, 1, V7X_LANES), lambda b, t: (0, 0, 0)),
            pl.BlockSpec((1, r_dv), lambda b, t: (0, 0)),
            dec_state,
        ],
        out_specs=[
            pl.BlockSpec((ct, vw), lambda b, t: (row(b, t), 0)),
            pl.BlockSpec((1, heads, dk, dv), lambda b, t: (b, 0, 0, 0)),
            dec_o,
            dec_state,
        ],
        out_shape=[
            jax.ShapeDtypeStruct((batch * seq, vw), BF16),
            jax.ShapeDtypeStruct((batch, heads, dk, dv), F32),
            jax.ShapeDtypeStruct((n_dec, r_heads * r_dv), BF16),
            jax.ShapeDtypeStruct(ret_state_dec.shape, ret_state_dec.dtype),
        ],
        scratch_shapes=[
            pltpu.VMEM((heads, dk, dv), F32),
            pltpu.VMEM(((levels + 2) * GLA_CHUNK, 3 * GLA_CHUNK), BF16),
            pltpu.VMEM((GLA_CHUNK, GLA_CHUNK), jnp.int32),
        ] + dec_scratch,
        compiler_params=_params(2, claim_all=True),
        name="gla_prompt_ret_decode",
    )(*_segment_arrays(lay, projs, ("qa", "ka", "va", "ga")), log2_decay, gla_norm_w,
      *_segment_arrays(lay, projs_dec, ("qb", "kb", "vb", "gb")), cos_dec, sin_dec,
      log_gamma, ret_norm_w, ret_state_dec)


def _rotary(x, cos, sin):
    half = x.shape[1] // 2
    x1, x2 = x[:, :half], x[:, half:]
    return jnp.concatenate([x1 * cos - x2 * sin, x1 * sin + x2 * cos], axis=1)


def _ret_kernel(*refs, heads, dk, dv, c, dec_dims):
    q_ref, k_ref, v_ref, gb_ref, cos_ref, sin_ref, lg_ref, nw_ref = refs[:8]
    dec_in = refs[8:15]
    o_ref, s_out_ref = refs[15:17]
    dec_out = refs[17:19]
    s_ref, dmat_ref, qdec_ref, kdec_ref = refs[19:23]
    dec_scratch = refs[23:]
    t = pl.program_id(1)
    step = pl.program_id(0) * pl.num_programs(1) + t
    last_step = pl.num_programs(0) * pl.num_programs(1) - 1

    _gla_decode_section(step, last_step, dec_in + dec_out, dec_scratch,
                        heads=dec_dims[0], dk=dec_dims[1], dv=dec_dims[2])

    @pl.when(t == 0)
    def _():
        s_ref[...] = jnp.zeros_like(s_ref)
        ri = lax.broadcasted_iota(jnp.int32, (c, c), 0)
        rj = lax.broadcasted_iota(jnp.int32, (c, c), 1)
        dist = (ri - rj).astype(F32)
        rowl = lax.broadcasted_iota(jnp.int32, (c, V7X_LANES), 0).astype(F32)
        for hh in range(heads):
            lg = lg_ref[hh]
            dmat_ref[hh] = jnp.exp(jnp.where(ri >= rj, dist * lg[:, :1], -jnp.inf))
            qdec_ref[hh] = jnp.exp((rowl + 1.0) * lg)
            kdec_ref[hh] = jnp.exp((float(c - 1) - rowl) * lg)

    ct = q_ref.shape[0]
    kcs = [slice(hh * dk, (hh + 1) * dk) for hh in range(heads)]
    vcs = [slice(hh * dv, (hh + 1) * dv) for hh in range(heads)]

    def chunk(ci, carry):
        rows = pl.ds(pl.multiple_of(ci * c, c), c)
        cos, sin = cos_ref[rows, :], sin_ref[rows, :]
        qrs = [_rotary(q_ref[rows, kc].astype(F32), cos, sin).astype(BF16) for kc in kcs]
        krs = [_rotary(k_ref[rows, kc].astype(F32), cos, sin) * (dk ** -0.5) for kc in kcs]
        scores = [_dot_nt(qrs[hh], krs[hh].astype(BF16)) * dmat_ref[hh] for hh in range(heads)]
        outs = []
        for hh in range(heads):
            qdec = jnp.concatenate([qdec_ref[hh]] * (dv // V7X_LANES), axis=1)
            o = qdec * _dot(qrs[hh], s_ref[hh].astype(BF16))
            outs.append(o + _dot(scores[hh].astype(BF16), v_ref[rows, vcs[hh]]))
        for hh in range(heads):
            kdec = jnp.concatenate([kdec_ref[hh]] * (dk // V7X_LANES), axis=1)
            k_tail = (krs[hh] * kdec).astype(BF16)
            lgv = jnp.concatenate([lg_ref[hh]] * (dv // V7X_LANES), axis=1)
            s_ref[hh] = jnp.exp(float(c) * lgv) * s_ref[hh] + _dot_tn(k_tail, v_ref[rows, vcs[hh]])
        for hh in range(heads):
            _ln_gate_store(outs[hh], nw_ref[...], gb_ref[rows, vcs[hh]].astype(F32), o_ref, rows, vcs[hh])
        return carry

    lax.fori_loop(0, ct // c, chunk, 0)

    @pl.when(t == pl.num_programs(1) - 1)
    def _():
        s_out_ref[0] = s_ref[...]


def _ret_prompt_gla_decode(projs, cos, sin, log_gamma, ret_norm_w, projs_dec, log2_decay_dec, gla_norm_w,
                           gla_state_dec, lay, batch, seq):
    heads, dk, dv = lay["ret_heads"], lay["ret_dk"], lay["ret_dv"]
    qk, vw = heads * dk, heads * dv
    g_heads, g_dk, g_dv = lay["gla_heads"], lay["gla_dk"], lay["gla_dv"]
    c = min(seq, RET_CHUNK)
    ct = min(seq, RET_STEP_CHUNKS * c)
    assert seq % ct == 0 and ct % c == 0
    nt = seq // ct
    half = dk // 2
    n_dec = projs_dec[0].shape[0]
    tps = _decode_plan(n_dec, batch * nt)
    row = lambda b, t: b * nt + t
    dec_rows, dec_state, dec_o, dec_scratch = _decode_specs("gla", lay, n_dec, tps, row)
    kern = functools.partial(_ret_kernel, heads=heads, dk=dk, dv=dv, c=c, dec_dims=(g_heads, g_dk, g_dv))
    return pl.pallas_call(
        kern,
        grid=(batch, nt),
        in_specs=[
            pl.BlockSpec((ct, qk), lambda b, t: (row(b, t), _segment_block(lay, "qb", qk))),
            pl.BlockSpec((ct, qk), lambda b, t: (row(b, t), _segment_block(lay, "kb", qk))),
            pl.BlockSpec((ct, vw), lambda b, t: (row(b, t), _segment_block(lay, "vb", vw))),
            pl.BlockSpec((ct, vw), lambda b, t: (row(b, t), _segment_block(lay, "gb", vw))),
            pl.BlockSpec((ct, half), lambda b, t: (t, 0)),
            pl.BlockSpec((ct, half), lambda b, t: (t, 0)),
            pl.BlockSpec((heads, 1, V7X_LANES), lambda b, t: (0, 0, 0)),
            pl.BlockSpec((1, dv), lambda b, t: (0, 0)),
        ] + dec_rows + [
            pl.BlockSpec((n_dec, g_heads * g_dk), lambda b, t: (0, 0)),
            pl.BlockSpec((1, g_dv), lambda b, t: (0, 0)),
            dec_state,
        ],
        out_specs=[
            pl.BlockSpec((ct, vw), lambda b, t: (row(b, t), 0)),
            pl.BlockSpec((1, heads, dk, dv), lambda b, t: (b, 0, 0, 0)),
            dec_o,
            dec_state,
        ],
        out_shape=[
            jax.ShapeDtypeStruct((batch * seq, vw), BF16),
            jax.ShapeDtypeStruct((batch, heads, dk, dv), F32),
            jax.ShapeDtypeStruct((n_dec, g_heads * g_dv), BF16),
            jax.ShapeDtypeStruct(gla_state_dec.shape, gla_state_dec.dtype),
        ],
        scratch_shapes=[
            pltpu.VMEM((heads, dk, dv), F32),
            pltpu.VMEM((heads, c, c), F32),
            pltpu.VMEM((heads, c, V7X_LANES), F32),
            pltpu.VMEM((heads, c, V7X_LANES), F32),
        ] + dec_scratch,
        compiler_params=_params(2, claim_all=True),
        name="ret_prompt_gla_decode",
    )(*_segment_arrays(lay, projs, ("qb", "kb", "vb", "gb")), cos, sin, log_gamma, ret_norm_w,
      *_segment_arrays(lay, projs_dec, ("qa", "ka", "va", "ga")), log2_decay_dec, gla_norm_w, gla_state_dec)


def _merge_kernel(oap_ref, obp_ref, oas_ref, obs_ref, wa_ref, wb_ref, g0p_ref, g1p_ref, g0s_ref, g1s_ref, wnext_ref,
                  mp_ref, ms_ref, wnext_bf_ref, wa_bf, wb_bf):
    wnext_bf_ref[...] = wnext_ref[...].astype(BF16)

    def merged(oa, ob, g0, g1):
        ya = _dot(oa, wa_bf[...])
        yb = _dot(ob, wb_bf[...])
        return _sigmoid(g0.astype(F32)) * ya + _sigmoid(g1.astype(F32)) * yb

    @pl.when(pl.program_id(1) == 0)
    def _():
        wa_bf[...] = wa_ref[...].astype(BF16)
        wb_bf[...] = wb_ref[...].astype(BF16)
        ms_ref[...] = merged(oas_ref[...], obs_ref[...], g0s_ref[...], g1s_ref[...]).astype(ms_ref.dtype)

    mp_ref[...] = merged(oap_ref[...], obp_ref[...], g0p_ref[...], g1p_ref[...]).astype(mp_ref.dtype)


def _slab_specs(w_next, n_steps, step_of):
    kn, dn = w_next.shape
    assert kn % n_steps == 0 and (kn // n_steps) % 16 == 0, (kn, n_steps)
    slab = kn // n_steps
    spec = pl.BlockSpec((slab, dn), lambda j, i: (step_of(j, i), 0))
    return spec, spec, jax.ShapeDtypeStruct((kn, dn), BF16), _nbytes((slab, dn), F32) + _nbytes((slab, dn), BF16)


def _merge(oa_p, ob_p, oa_s, ob_s, wa, wb, projs_p, projs_s, lay, w_next):
    proj_p, proj_s = projs_p[lay["cols"]["mg"][0]], projs_s[lay["cols"]["mg"][0]]
    m_p, ka = oa_p.shape
    kb = ob_p.shape[1]
    tail = oa_s.shape[0]
    d = wa.shape[1]
    tm = _row_tile(m_p, ROW_TILE)
    tn = min(d, 1024)
    mg = lay["cols"]["mg"][1]
    assert d % tn == 0 and mg % tn == 0
    g0 = mg // tn
    g1 = (mg + d) // tn
    n_m = m_p // tm
    slab_in, slab_out, slab_shape, slab_bytes = _slab_specs(w_next, (d // tn) * n_m, lambda j, i: j * n_m + i)
    return pl.pallas_call(
        _merge_kernel,
        grid=(d // tn, n_m),
        in_specs=[
            pl.BlockSpec((tm, ka), lambda j, i: (i, 0)),
            pl.BlockSpec((tm, kb), lambda j, i: (i, 0)),
            pl.BlockSpec((tail, ka), lambda j, i: (0, 0)),
            pl.BlockSpec((tail, kb), lambda j, i: (0, 0)),
            pl.BlockSpec((ka, tn), lambda j, i: (0, j)),
            pl.BlockSpec((kb, tn), lambda j, i: (0, j)),
            pl.BlockSpec((tm, tn), lambda j, i: (i, g0 + j)),
            pl.BlockSpec((tm, tn), lambda j, i: (i, g1 + j)),
            pl.BlockSpec((tail, tn), lambda j, i: (0, g0 + j)),
            pl.BlockSpec((tail, tn), lambda j, i: (0, g1 + j)),
            slab_in,
        ],
        out_specs=[pl.BlockSpec((tm, tn), lambda j, i: (i, j)), pl.BlockSpec((tail, tn), lambda j, i: (0, j)),
                   slab_out],
        out_shape=[jax.ShapeDtypeStruct((m_p, d), BF16), jax.ShapeDtypeStruct((tail, d), BF16), slab_shape],
        scratch_shapes=[pltpu.VMEM((ka, tn), BF16), pltpu.VMEM((kb, tn), BF16)],
        compiler_params=_params(
            2, _nbytes((tm, ka), BF16), _nbytes((tm, kb), BF16), _nbytes((tail, ka), BF16), _nbytes((tail, kb), BF16),
            _nbytes((ka, tn), F32), _nbytes((kb, tn), F32), 3 * _nbytes((tm, tn), BF16), 3 * _nbytes((tail, tn), BF16),
            slab_bytes,
            scratch_bytes=_nbytes((ka, tn), BF16) + _nbytes((kb, tn), BF16) + 3 * _nbytes((tm, tn), F32)),
        name="merge",
    )(oa_p, ob_p, oa_s, ob_s, wa, wb, proj_p, proj_p, proj_s, proj_s, w_next)


def _proj_res_norm_kernel(ap_ref, as_ref, w_ref, resp_ref, ress_ref, nw_ref, *out_refs, emit_sum):
    n_out = 2 if emit_sum else 1
    outs_p, outs_s = out_refs[:n_out], out_refs[n_out:]
    d = w_ref.shape[1]
    col_chunk = min(d, 512)

    def rows(a_ref, res_ref, outs):
        x_ref = outs[0]
        nrow = x_ref.shape[0]
        row_chunk = min(nrow, 128)
        assert nrow % row_chunk == 0
        a = a_ref[...]
        for c in range(d // col_chunk):
            cs = slice(c * col_chunk, (c + 1) * col_chunk)
            x_ref[:, cs] = res_ref[:, cs] + _dot(a, w_ref[:, cs])

        def body(c, carry):
            rr = pl.ds(pl.multiple_of(c * row_chunk, row_chunk), row_chunk)
            y = _rmsnorm_rows(x_ref[rr, :], nw_ref[...])
            if emit_sum:
                outs[1][rr, :] = y.astype(outs[1].dtype)
            else:
                x_ref[rr, :] = y
            return carry

        lax.fori_loop(0, nrow // row_chunk, body, 0)

    rows(ap_ref, resp_ref, outs_p)

    @pl.when(pl.program_id(0) == pl.num_programs(0) - 1)
    def _():
        rows(as_ref, ress_ref, outs_s)


def _proj_res_norm(a_p, a_s, w, res_p, res_s, norm_w, emit_sum):
    m_p, kdim = a_p.shape
    tail = a_s.shape[0]
    d = w.shape[1]
    tm = _row_tile(m_p, ROW_TILE_NARROW)
    p_spec = pl.BlockSpec((tm, d), lambda i: (i, 0))
    s_spec = pl.BlockSpec((tail, d), lambda i: (0, 0))
    s3_spec = pl.BlockSpec((tail, None, d), lambda i: (0, 0, 0))
    res_s_spec = s3_spec if res_s.ndim == 3 else s_spec
    out_specs = [p_spec, s3_spec]
    out_shape = [jax.ShapeDtypeStruct((m_p, d), F32), jax.ShapeDtypeStruct((tail, 1, d), F32)]
    assert w.dtype == BF16
    if emit_sum:
        out_specs = [p_spec, p_spec, s_spec, s_spec]
        out_shape = [out_shape[0], jax.ShapeDtypeStruct((m_p, d), BF16),
                     jax.ShapeDtypeStruct((tail, d), F32), jax.ShapeDtypeStruct((tail, d), BF16)]
    in_specs = [
        pl.BlockSpec((tm, kdim), lambda i: (i, 0)),
        pl.BlockSpec((tail, kdim), lambda i: (0, 0)),
        pl.BlockSpec((kdim, d), lambda i: (0, 0), pipeline_mode=pl.Buffered(1)),
        p_spec, res_s_spec, pl.BlockSpec((1, d), lambda i: (0, 0))]
    return pl.pallas_call(
        functools.partial(_proj_res_norm_kernel, emit_sum=emit_sum),
        grid=(m_p // tm,),
        in_specs=in_specs,
        out_specs=out_specs,
        out_shape=out_shape,
        compiler_params=_params(1, claim_all=True),
        name="proj_res_norm",
    )(a_p, a_s, w, res_p, res_s, norm_w.reshape(1, d))


def _swiglu_kernel(hp_ref, hs_ref, wg_ref, wu_ref, wnext_ref, op_ref, os_ref, wnext_bf_ref, wg_bf, wu_bf):
    wnext_bf_ref[...] = wnext_ref[...].astype(BF16)

    tn = wg_bf.shape[1]
    col_chunk = min(tn, 256)

    def act(h_ref, o_ref):
        h = h_ref[...]
        for c in range(tn // col_chunk):
            cs = slice(c * col_chunk, (c + 1) * col_chunk)
            a = _dot(h, wg_bf[:, cs])
            b = _dot(h, wu_bf[:, cs])
            o_ref[:, cs] = (_silu(a) * b).astype(o_ref.dtype)

    @pl.when(pl.program_id(1) == 0)
    def _():
        wg_bf[...] = wg_ref[...].astype(BF16)
        wu_bf[...] = wu_ref[...].astype(BF16)
        act(hs_ref, os_ref)

    act(hp_ref, op_ref)


def _swiglu(h_p, h_s, wg, wu, w_next):
    m_p, d = h_p.shape
    tail = h_s.shape[0]
    f = wg.shape[1]
    tm = _row_tile(m_p, ROW_TILE_WIDE)
    tn = 512 if f % 512 == 0 else 256
    assert f % tn == 0
    n_m = m_p // tm
    slab_in, slab_out, slab_shape, slab_bytes = _slab_specs(w_next, (f // tn) * n_m, lambda j, i: j * n_m + i)
    return pl.pallas_call(
        _swiglu_kernel,
        grid=(f // tn, n_m),
        in_specs=[
            pl.BlockSpec((tm, d), lambda j, i: (i, 0)),
            pl.BlockSpec((tail, d), lambda j, i: (0, 0)),
            pl.BlockSpec((d, tn), lambda j, i: (0, j)),
            pl.BlockSpec((d, tn), lambda j, i: (0, j)),
            slab_in,
        ],
        out_specs=[pl.BlockSpec((tm, tn), lambda j, i: (i, j)), pl.BlockSpec((tail, tn), lambda j, i: (0, j)),
                   slab_out],
        out_shape=[jax.ShapeDtypeStruct((m_p, f), BF16), jax.ShapeDtypeStruct((tail, f), BF16), slab_shape],
        scratch_shapes=[pltpu.VMEM((d, tn), BF16), pltpu.VMEM((d, tn), BF16)],
        compiler_params=_params(
            2, _nbytes((tm, d), BF16), _nbytes((tail, d), BF16), 2 * _nbytes((d, tn), F32),
            _nbytes((tm, tn), BF16), _nbytes((tail, tn), BF16), slab_bytes,
            scratch_bytes=2 * _nbytes((d, tn), BF16) + 3 * _nbytes((tm, tn), F32)),
        name="swiglu",
    )(h_p, h_s, wg, wu, w_next)


def _layout(d_model, in_width, state_gla, state_ret, gate_rank):
    _, _, gh, gdk, gdv = state_gla.shape
    _, _, rh, rdk, rdv = state_ret.shape
    gqk, gv, rqk, rv = gh * gdk, gh * gdv, rh * rdk, rh * rdv
    lay = dict(gla_heads=gh, gla_dk=gdk, gla_dv=gdv, ret_heads=rh, ret_dk=rdk, ret_dv=rdv, rank=gate_rank)
    off = 0
    for name, width in (("qa", gqk), ("ka", gqk), ("va", gv), ("ga", gv), ("qb", rqk), ("kb", rqk),
                        ("vb", rv), ("gb", rv), ("mg", 2 * d_model)):
        lay[name] = off
        off += width
    lay["out_cols"] = off
    lay["plain_cols"] = 2 * gqk + gv
    lay["gd_src"] = lay["plain_cols"]
    assert lay["gd_src"] % V7X_LANES == 0 and gate_rank <= V7X_LANES
    assert in_width == off + gate_rank
    lay["widths"] = dict(qa=gqk, ka=gqk, va=gv, ga=gv, qb=rqk, kb=rqk, vb=rv, gb=rv, mg=2 * d_model)
    return lay


def _split_columns(lay, first_cols):
    cols = {}
    for name, width in lay["widths"].items():
        off = lay[name]
        assert off + width <= first_cols or off >= first_cols, "a segment straddles the two arrays"
        cols[name] = (0, off) if off < first_cols else (1, off - first_cols)
    return cols


def _layer(x_p, x_s, st_gla, st_ret, wts, lay, log_gamma, final_norm):
    (norm_mix, w_in, w_gate_up, b_gate, gla_norm_w, w_gla_up, ret_norm_w, w_ret_up, w_out, norm_ffn,
     w_ffn_gate, w_ffn_up, w_ffn_down) = wts
    batch, seq, d = x_p.shape
    rank = lay["rank"]
    gqk = lay["gla_heads"] * lay["gla_dk"]
    bup = b_gate.reshape(1, gqk)
    gnw = gla_norm_w.reshape(1, -1)
    rnw = ret_norm_w.reshape(1, -1)
    tn = 1024 if (lay["out_cols"] % 1024 == 0 and lay["plain_cols"] % 1024 == 0) else 512
    xp = x_p.reshape(batch * seq, d)
    assert x_s.ndim == 3 and x_s.shape[1] == 1, "one new token per decode sequence"
    xs = x_s
    w_in_t = w_in.T

    h_p, h_s, g_p, g_s, cos_p, sin_p, cos_s, sin_s, first_p, first_s = _rmsnorm_gate(
        xp, xs, norm_mix, w_in_t, lay["gd_src"], w_gate_up, bup, seq, lay["ret_dk"] // 2, tn)
    rest_p, rest_s = _in_proj(h_p, h_s, w_in_t, lay["plain_cols"], rank, lay["out_cols"], tn, 1)
    projs_p, projs_s = (first_p, rest_p), (first_s, rest_s)
    lay = dict(lay, cols=_split_columns(lay, tn))
    oa_p, sa_p, ob_s, sb_s = _gla_prompt_ret_decode(
        projs_p, g_p, gnw, projs_s, cos_s, sin_s, log_gamma, rnw, st_ret, lay, batch, seq)
    ob_p, sb_p, oa_s, sa_s = _ret_prompt_gla_decode(
        projs_p, cos_p, sin_p, log_gamma, rnw, projs_s, g_s, gnw, st_gla, lay, batch, seq)
    m_p, m_s, w_out_bf = _merge(oa_p, ob_p, oa_s, ob_s, w_gla_up, w_ret_up, projs_p, projs_s, lay, w_out)
    x1_p, h2_p, x1_s, h2_s = _proj_res_norm(m_p, m_s, w_out_bf, xp, xs, norm_ffn, True)
    act_p, act_s, w_down_bf = _swiglu(h2_p, h2_s, w_ffn_gate, w_ffn_up, w_ffn_down)
    y_p, y_s = _proj_res_norm(act_p, act_s, w_down_bf, x1_p, x1_s, final_norm, False)
    return (y_p, sa_p, sb_p), (y_s, sa_s, sb_s)


def kernel(x_prompt, x_sample, state_gla, state_ret, norm_mix, w_in, w_gla_gate_up, b_gla_gate, gla_norm_w,
           w_gla_up, ret_norm_w, w_ret_up, w_out, norm_ffn, w_ffn_gate, w_ffn_up, w_ffn_down, norm_final):
    depth = w_in.shape[0]
    assert depth == 1, "single-layer trunk"
    batch, seq, d = x_prompt.shape
    lay = _layout(d, w_in.shape[-1], state_gla, state_ret, w_gla_gate_up.shape[1])
    rh, rdk = lay["ret_heads"], lay["ret_dk"]
    assert rdk // 2 == V7X_LANES
    lg = jnp.log1p(-jnp.exp(jnp.linspace(math.log(1.0 / 32), math.log(1.0 / 512), rh))).astype(F32)
    log_gamma = jnp.broadcast_to(lg[:, None, None], (rh, 1, V7X_LANES))

    wts = (norm_mix[0], w_in[0], w_gla_gate_up[0], b_gla_gate[0], gla_norm_w[0], w_gla_up[0], ret_norm_w[0],
           w_ret_up[0], w_out[0], norm_ffn[0], w_ffn_gate[0], w_ffn_up[0], w_ffn_down[0])
    (y_p, ga_p, re_p), (y_s, ga_s, re_s) = _layer(
        x_prompt, x_sample, state_gla[0], state_ret[0], wts, lay, log_gamma, norm_final)

    sd = state_gla.dtype
    return (y_p.reshape(batch, seq, d), y_s.reshape(x_sample.shape),
            ga_p[None].astype(sd), re_p[None].astype(state_ret.dtype),
            ga_s[None].astype(sd), re_s[None].astype(state_ret.dtype))
```
